```python
import math
import jax, jax.numpy as jnp
from jax import lax
import numpy as np

D_MODEL = 1024
BATCH = 8
SEQ = 4096
DEPTH = 2

N_MIXERS = 4
MIXER_WIDTH = D_MODEL // N_MIXERS
D_MIX = N_MIXERS * MIXER_WIDTH
GROUPS_PER_MIXER = 4
GROUP_DIM = MIXER_WIDTH // GROUPS_PER_MIXER
D_IN_PROJ = 8 * MIXER_WIDTH
SCONV_K = 3
SGU_CHUNK = 128
CCONV_K = 31
POOL_WINDOWS = (2, 4, 8, 16)
N_XATTN_HEADS = 4
XATTN_HEAD_DIM = D_MODEL // N_XATTN_HEADS
N_MEM = 256
D_FF = 2816
EPS = 1e-6

kernel_name = "hybrid_parallel_group_macaron_decoder"


def rmsnorm(x, g):
    xf = x.astype(jnp.float32)
    y = xf * lax.rsqrt(jnp.mean(xf * xf, axis=-1, keepdims=True) + EPS)
    return (y * g.astype(jnp.float32)).astype(x.dtype)


def layernorm(x, g, b=None):
    xf = x.astype(jnp.float32)
    mu = jnp.mean(xf, axis=-1, keepdims=True)
    var = jnp.mean(jnp.square(xf - mu), axis=-1, keepdims=True)
    y = (xf - mu) * lax.rsqrt(var + EPS) * g.astype(jnp.float32)
    if b is not None:
        y = y + b.astype(jnp.float32)
    return y.astype(x.dtype)


def swiglu_ffn(h, w_in, w_out):
    g, u = jnp.split(h @ w_in, 2, axis=-1)
    return (jax.nn.silu(g) * u) @ w_out


def causal_depthwise_conv(x, w):
    k = w.shape[0]
    return lax.conv_general_dilated(
        x, w[:, None, :].astype(x.dtype), window_strides=(1,), padding=((k - 1, 0),),
        dimension_numbers=("NWC", "WIO", "NWC"), feature_group_count=x.shape[-1])


def mixer_short_conv(bg, cg, xt, conv_w):
    return bg * causal_depthwise_conv(cg * xt, conv_w)


def mixer_spatial_gating(u, v, norm_g, w_s, b_s):
    bsz, s, _ = v.shape
    vn = layernorm(v, norm_g)
    vr = vn.reshape(bsz, s // SGU_CHUNK, SGU_CHUNK, GROUPS_PER_MIXER, GROUP_DIM)
    w_causal = jnp.tril(w_s)
    mixed = jnp.einsum("hts,bcshd->bcthd", w_causal.astype(vr.dtype), vr)
    mixed = mixed + b_s.T[None, None, :, :, None].astype(vr.dtype)
    return u * mixed.reshape(bsz, s, MIXER_WIDTH)


def mixer_conformer_conv(a, g, conv_w, ln_g, ln_b):
    y = a * jax.nn.sigmoid(g)
    y = causal_depthwise_conv(y, conv_w)
    y = layernorm(y, ln_g, ln_b)
    return jax.nn.silu(y)


def mixer_multiscale_pool(w, pool_w, pool_scale):
    bsz, s, _ = w.shape
    wf = w.astype(jnp.float32)
    cs = jnp.cumsum(wf, axis=1)
    pos = jnp.arange(s, dtype=jnp.int32)
    outs = []
    for gi, k in enumerate(POOL_WINDOWS):
        sl = slice(gi * GROUP_DIM, (gi + 1) * GROUP_DIM)
        csg = cs[:, :, sl]
        shifted = jnp.pad(csg, ((0, 0), (k, 0), (0, 0)))[:, :s]
        count = jnp.minimum(pos + 1, k).astype(jnp.float32)[None, :, None]
        outs.append((csg - shifted) / count - wf[:, :, sl])
    pooled = jnp.stack(outs, axis=2).astype(w.dtype)
    y = jnp.einsum("bsgc,gcd->bsgd", pooled, pool_w)
    return y.reshape(bsz, s, MIXER_WIDTH) * pool_scale


def cross_attention(h, m, wq, wkv, wo):
    bsz, s, _ = h.shape
    q = (h @ wq).reshape(bsz, s, N_XATTN_HEADS, XATTN_HEAD_DIM)
    k, v = jnp.split(m @ wkv, 2, axis=-1)
    k = k.reshape(bsz, N_MEM, N_XATTN_HEADS, XATTN_HEAD_DIM)
    v = v.reshape(bsz, N_MEM, N_XATTN_HEADS, XATTN_HEAD_DIM)
    scores = jnp.einsum("bshd,bmhd->bhsm", q, k).astype(jnp.float32) / math.sqrt(XATTN_HEAD_DIM)
    p = jax.nn.softmax(scores, axis=-1).astype(v.dtype)
    o = jnp.einsum("bhsm,bmhd->bshd", p, v).reshape(bsz, s, D_MODEL)
    return o @ wo


def _fwd_setup_inputs(seed: int = 0) -> dict:
    key = jax.random.key(seed)
    ks = iter(jax.random.split(key, 32))
    L, D, W = DEPTH, D_MODEL, MIXER_WIDTH

    def nrm(shape, fan_in):
        return jax.random.normal(next(ks), shape, jnp.float32) * (fan_in ** -0.5)

    def gain(shape):
        return 1.0 + 0.02 * jax.random.normal(next(ks), shape, jnp.float32)

    def small(shape):
        return 0.02 * jax.random.normal(next(ks), shape, jnp.float32)

    return {
        "x": jax.random.normal(next(ks), (BATCH, SEQ, D), jnp.float32),
        "mem": jax.random.normal(next(ks), (BATCH, N_MEM, D), jnp.float32),
        "norm_ffn1": gain((L, D)),
        "ffn1_w_in": nrm((L, D, 2 * D_FF), D),
        "ffn1_w_out": nrm((L, D_FF, D), D_FF),
        "norm_mix": gain((L, D)),
        "mix_w_in": nrm((L, D, D_IN_PROJ), D),
        "sconv_w": nrm((L, SCONV_K, W), SCONV_K),
        "sgu_norm_g": gain((L, W)),
        "sgu_w": nrm((L, GROUPS_PER_MIXER, SGU_CHUNK, SGU_CHUNK), SGU_CHUNK),
        "sgu_b": gain((L, GROUPS_PER_MIXER, SGU_CHUNK)),
        "cconv_w": nrm((L, CCONV_K, W), CCONV_K),
        "cconv_ln_g": gain((L, W)),
        "cconv_ln_b": small((L, W)),
        "pool_w": nrm((L, len(POOL_WINDOWS), GROUP_DIM, GROUP_DIM), GROUP_DIM),
        "pool_scale": gain((L, W)),
        "mix_w_out": nrm((L, D_MIX, D), D_MIX),
        "norm_xattn": gain((L, D)),
        "norm_mem": gain((L, D)),
        "xattn_wq": nrm((L, D, D), D),
        "xattn_wkv": nrm((L, D, 2 * D), D),
        "xattn_wo": nrm((L, D, D), D),
        "norm_ffn2": gain((L, D)),
        "ffn2_w_in": nrm((L, D, 2 * D_FF), D),
        "ffn2_w_out": nrm((L, D_FF, D), D_FF),
        "norm_final": gain((D,)),
    }


def _fwd_reference(x, mem, norm_ffn1, ffn1_w_in, ffn1_w_out, norm_mix, mix_w_in, sconv_w,
              sgu_norm_g, sgu_w, sgu_b, cconv_w, cconv_ln_g, cconv_ln_b, pool_w, pool_scale,
              mix_w_out, norm_xattn, norm_mem, xattn_wq, xattn_wkv, xattn_wo,
              norm_ffn2, ffn2_w_in, ffn2_w_out, norm_final):
    W = MIXER_WIDTH
    split_points = [W, 2 * W, 3 * W, 4 * W, 5 * W, 6 * W, 7 * W]
    for l in range(DEPTH):
        x = x + 0.5 * swiglu_ffn(rmsnorm(x, norm_ffn1[l]), ffn1_w_in[l], ffn1_w_out[l])

        h = rmsnorm(x, norm_mix[l])
        z = h @ mix_w_in[l]
        a_b, a_c, a_x, b_u, b_v, c_a, c_g, d_w = jnp.split(z, split_points, axis=-1)
        y_a = mixer_short_conv(a_b, a_c, a_x, sconv_w[l])
        y_b = mixer_spatial_gating(b_u, b_v, sgu_norm_g[l], sgu_w[l], sgu_b[l])
        y_c = mixer_conformer_conv(c_a, c_g, cconv_w[l], cconv_ln_g[l], cconv_ln_b[l])
        y_d = mixer_multiscale_pool(d_w, pool_w[l], pool_scale[l])
        y = jnp.concatenate([y_a, y_b, y_c, y_d], axis=-1)
        x = x + y @ mix_w_out[l]

        x = x + cross_attention(rmsnorm(x, norm_xattn[l]), rmsnorm(mem, norm_mem[l]),
                                xattn_wq[l], xattn_wkv[l], xattn_wo[l])

        x = x + 0.5 * swiglu_ffn(rmsnorm(x, norm_ffn2[l]), ffn2_w_in[l], ffn2_w_out[l])
    return rmsnorm(x, norm_final)


import jax as _jax
import jax.numpy as _jnp

TWIN_FORMAT = 'train_step'
FWD_PARAMS = ['x', 'mem', 'norm_ffn1', 'ffn1_w_in', 'ffn1_w_out', 'norm_mix', 'mix_w_in', 'sconv_w', 'sgu_norm_g', 'sgu_w', 'sgu_b', 'cconv_w', 'cconv_ln_g', 'cconv_ln_b', 'pool_w', 'pool_scale', 'mix_w_out', 'norm_xattn', 'norm_mem', 'xattn_wq', 'xattn_wkv', 'xattn_wo', 'norm_ffn2', 'ffn2_w_in', 'ffn2_w_out', 'norm_final']
TWIN_WEIGHTS = ['norm_ffn1', 'ffn1_w_in', 'ffn1_w_out', 'norm_mix', 'mix_w_in', 'sconv_w', 'sgu_norm_g', 'sgu_w', 'sgu_b', 'cconv_w', 'cconv_ln_g', 'cconv_ln_b', 'pool_w', 'pool_scale', 'mix_w_out', 'norm_xattn', 'norm_mem', 'xattn_wq', 'xattn_wkv', 'xattn_wo', 'norm_ffn2', 'ffn2_w_in', 'ffn2_w_out', 'norm_final']
TWIN_DIFF_INPUT = 'x'
TWIN_INPUTS = ['x', 'mem', 'norm_ffn1', 'ffn1_w_in', 'ffn1_w_out', 'norm_mix', 'mix_w_in', 'sconv_w', 'sgu_norm_g', 'sgu_w', 'sgu_b', 'cconv_w', 'cconv_ln_g', 'cconv_ln_b', 'pool_w', 'pool_scale', 'mix_w_out', 'norm_xattn', 'norm_mem', 'xattn_wq', 'xattn_wkv', 'xattn_wo', 'norm_ffn2', 'ffn2_w_in', 'ffn2_w_out', 'norm_final', 'loss_target', 'm_norm_ffn1', 'm_ffn1_w_in', 'm_ffn1_w_out', 'm_norm_mix', 'm_mix_w_in', 'm_sconv_w', 'm_sgu_norm_g', 'm_sgu_w', 'm_sgu_b', 'm_cconv_w', 'm_cconv_ln_g', 'm_cconv_ln_b', 'm_pool_w', 'm_pool_scale', 'm_mix_w_out', 'm_norm_xattn', 'm_norm_mem', 'm_xattn_wq', 'm_xattn_wkv', 'm_xattn_wo', 'm_norm_ffn2', 'm_ffn2_w_in', 'm_ffn2_w_out', 'm_norm_final', 'v_norm_ffn1', 'v_ffn1_w_in', 'v_ffn1_w_out', 'v_norm_mix', 'v_mix_w_in', 'v_sconv_w', 'v_sgu_norm_g', 'v_sgu_w', 'v_sgu_b', 'v_cconv_w', 'v_cconv_ln_g', 'v_cconv_ln_b', 'v_pool_w', 'v_pool_scale', 'v_mix_w_out', 'v_norm_xattn', 'v_norm_mem', 'v_xattn_wq', 'v_xattn_wkv', 'v_xattn_wo', 'v_norm_ffn2', 'v_ffn2_w_in', 'v_ffn2_w_out', 'v_norm_final']
TWIN_OUTPUTS = ['loss', 'grad_x', 'grad_norm_ffn1', 'grad_ffn1_w_in', 'grad_ffn1_w_out', 'grad_norm_mix', 'grad_mix_w_in', 'grad_sconv_w', 'grad_sgu_norm_g', 'grad_sgu_w', 'grad_sgu_b', 'grad_cconv_w', 'grad_cconv_ln_g', 'grad_cconv_ln_b', 'grad_pool_w', 'grad_pool_scale', 'grad_mix_w_out', 'grad_norm_xattn', 'grad_norm_mem', 'grad_xattn_wq', 'grad_xattn_wkv', 'grad_xattn_wo', 'grad_norm_ffn2', 'grad_ffn2_w_in', 'grad_ffn2_w_out', 'grad_norm_final', 'delta_norm_ffn1', 'delta_ffn1_w_in', 'delta_ffn1_w_out', 'delta_norm_mix', 'delta_mix_w_in', 'delta_sconv_w', 'delta_sgu_norm_g', 'delta_sgu_w', 'delta_sgu_b', 'delta_cconv_w', 'delta_cconv_ln_g', 'delta_cconv_ln_b', 'delta_pool_w', 'delta_pool_scale', 'delta_mix_w_out', 'delta_norm_xattn', 'delta_norm_mem', 'delta_xattn_wq', 'delta_xattn_wkv', 'delta_xattn_wo', 'delta_norm_ffn2', 'delta_ffn2_w_in', 'delta_ffn2_w_out', 'delta_norm_final', 'new_m_norm_ffn1', 'new_m_ffn1_w_in', 'new_m_ffn1_w_out', 'new_m_norm_mix', 'new_m_mix_w_in', 'new_m_sconv_w', 'new_m_sgu_norm_g', 'new_m_sgu_w', 'new_m_sgu_b', 'new_m_cconv_w', 'new_m_cconv_ln_g', 'new_m_cconv_ln_b', 'new_m_pool_w', 'new_m_pool_scale', 'new_m_mix_w_out', 'new_m_norm_xattn', 'new_m_norm_mem', 'new_m_xattn_wq', 'new_m_xattn_wkv', 'new_m_xattn_wo', 'new_m_norm_ffn2', 'new_m_ffn2_w_in', 'new_m_ffn2_w_out', 'new_m_norm_final', 'new_v_norm_ffn1', 'new_v_ffn1_w_in', 'new_v_ffn1_w_out', 'new_v_norm_mix', 'new_v_mix_w_in', 'new_v_sconv_w', 'new_v_sgu_norm_g', 'new_v_sgu_w', 'new_v_sgu_b', 'new_v_cconv_w', 'new_v_cconv_ln_g', 'new_v_cconv_ln_b', 'new_v_pool_w', 'new_v_pool_scale', 'new_v_mix_w_out', 'new_v_norm_xattn', 'new_v_norm_mem', 'new_v_xattn_wq', 'new_v_xattn_wkv', 'new_v_xattn_wo', 'new_v_norm_ffn2', 'new_v_ffn2_w_in', 'new_v_ffn2_w_out', 'new_v_norm_final']
TWIN_LEAF_KINDS = {'loss': 'loss', 'grad_x': 'grad_x', 'grad_norm_ffn1': 'grad_w', 'grad_ffn1_w_in': 'grad_w', 'grad_ffn1_w_out': 'grad_w', 'grad_norm_mix': 'grad_w', 'grad_mix_w_in': 'grad_w', 'grad_sconv_w': 'grad_w', 'grad_sgu_norm_g': 'grad_w', 'grad_sgu_w': 'grad_w', 'grad_sgu_b': 'grad_w', 'grad_cconv_w': 'grad_w', 'grad_cconv_ln_g': 'grad_w', 'grad_cconv_ln_b': 'grad_w', 'grad_pool_w': 'grad_w', 'grad_pool_scale': 'grad_w', 'grad_mix_w_out': 'grad_w', 'grad_norm_xattn': 'grad_w', 'grad_norm_mem': 'grad_w', 'grad_xattn_wq': 'grad_w', 'grad_xattn_wkv': 'grad_w', 'grad_xattn_wo': 'grad_w', 'grad_norm_ffn2': 'grad_w', 'grad_ffn2_w_in': 'grad_w', 'grad_ffn2_w_out': 'grad_w', 'grad_norm_final': 'grad_w', 'delta_norm_ffn1': 'delta_w', 'delta_ffn1_w_in': 'delta_w', 'delta_ffn1_w_out': 'delta_w', 'delta_norm_mix': 'delta_w', 'delta_mix_w_in': 'delta_w', 'delta_sconv_w': 'delta_w', 'delta_sgu_norm_g': 'delta_w', 'delta_sgu_w': 'delta_w', 'delta_sgu_b': 'delta_w', 'delta_cconv_w': 'delta_w', 'delta_cconv_ln_g': 'delta_w', 'delta_cconv_ln_b': 'delta_w', 'delta_pool_w': 'delta_w', 'delta_pool_scale': 'delta_w', 'delta_mix_w_out': 'delta_w', 'delta_norm_xattn': 'delta_w', 'delta_norm_mem': 'delta_w', 'delta_xattn_wq': 'delta_w', 'delta_xattn_wkv': 'delta_w', 'delta_xattn_wo': 'delta_w', 'delta_norm_ffn2': 'delta_w', 'delta_ffn2_w_in': 'delta_w', 'delta_ffn2_w_out': 'delta_w', 'delta_norm_final': 'delta_w', 'new_m_norm_ffn1': 'new_m', 'new_m_ffn1_w_in': 'new_m', 'new_m_ffn1_w_out': 'new_m', 'new_m_norm_mix': 'new_m', 'new_m_mix_w_in': 'new_m', 'new_m_sconv_w': 'new_m', 'new_m_sgu_norm_g': 'new_m', 'new_m_sgu_w': 'new_m', 'new_m_sgu_b': 'new_m', 'new_m_cconv_w': 'new_m', 'new_m_cconv_ln_g': 'new_m', 'new_m_cconv_ln_b': 'new_m', 'new_m_pool_w': 'new_m', 'new_m_pool_scale': 'new_m', 'new_m_mix_w_out': 'new_m', 'new_m_norm_xattn': 'new_m', 'new_m_norm_mem': 'new_m', 'new_m_xattn_wq': 'new_m', 'new_m_xattn_wkv': 'new_m', 'new_m_xattn_wo': 'new_m', 'new_m_norm_ffn2': 'new_m', 'new_m_ffn2_w_in': 'new_m', 'new_m_ffn2_w_out': 'new_m', 'new_m_norm_final': 'new_m', 'new_v_norm_ffn1': 'new_v', 'new_v_ffn1_w_in': 'new_v', 'new_v_ffn1_w_out': 'new_v', 'new_v_norm_mix': 'new_v', 'new_v_mix_w_in': 'new_v', 'new_v_sconv_w': 'new_v', 'new_v_sgu_norm_g': 'new_v', 'new_v_sgu_w': 'new_v', 'new_v_sgu_b': 'new_v', 'new_v_cconv_w': 'new_v', 'new_v_cconv_ln_g': 'new_v', 'new_v_cconv_ln_b': 'new_v', 'new_v_pool_w': 'new_v', 'new_v_pool_scale': 'new_v', 'new_v_mix_w_out': 'new_v', 'new_v_norm_xattn': 'new_v', 'new_v_norm_mem': 'new_v', 'new_v_xattn_wq': 'new_v', 'new_v_xattn_wkv': 'new_v', 'new_v_xattn_wo': 'new_v', 'new_v_norm_ffn2': 'new_v', 'new_v_ffn2_w_in': 'new_v', 'new_v_ffn2_w_out': 'new_v', 'new_v_norm_final': 'new_v'}


def _forward(args):
    return _fwd_reference(*[args[k] for k in FWD_PARAMS])


def _output_shape():
    def fwd():
        inp = _fwd_setup_inputs(0)
        return _fwd_reference(*[inp[k] for k in FWD_PARAMS])
    out = _jax.eval_shape(fwd)
    return out.shape, out.dtype

N_MICROBATCH = 1
ADAM_LR = 0.001
ADAM_B1 = 0.9
ADAM_B2 = 0.999
ADAM_EPS = 1e-08
ADAM_WD = 0.01
ADAM_STEP = 10
PER_EXAMPLE_BATCH_AXIS = {'x': 0, 'mem': 0, 'loss_target': 0}
SHARED_INPUTS = []
_WEIGHT_DTYPES = {'norm_ffn1': _jnp.float32, 'ffn1_w_in': _jnp.float32, 'ffn1_w_out': _jnp.float32, 'norm_mix': _jnp.float32, 'mix_w_in': _jnp.float32, 'sconv_w': _jnp.float32, 'sgu_norm_g': _jnp.float32, 'sgu_w': _jnp.float32, 'sgu_b': _jnp.float32, 'cconv_w': _jnp.float32, 'cconv_ln_g': _jnp.float32, 'cconv_ln_b': _jnp.float32, 'pool_w': _jnp.float32, 'pool_scale': _jnp.float32, 'mix_w_out': _jnp.float32, 'norm_xattn': _jnp.float32, 'norm_mem': _jnp.float32, 'xattn_wq': _jnp.float32, 'xattn_wkv': _jnp.float32, 'xattn_wo': _jnp.float32, 'norm_ffn2': _jnp.float32, 'ffn2_w_in': _jnp.float32, 'ffn2_w_out': _jnp.float32, 'norm_final': _jnp.float32}
MOMENT_SCALE = {'norm_ffn1': 8.282438e-02, 'ffn1_w_in': 3.410784e-02, 'ffn1_w_out': 5.569248e-02, 'norm_mix': 1.630645e-01, 'mix_w_in': 1.107535e-01, 'sconv_w': 1.319822e-01, 'sgu_norm_g': 8.945188e-02, 'sgu_w': 6.263579e-02, 'sgu_b': 9.013946e-02, 'cconv_w': 7.888026e-02, 'cconv_ln_g': 9.050910e-02, 'cconv_ln_b': 7.814540e-02, 'pool_w': 1.140287e-01, 'pool_scale': 1.190213e-01, 'mix_w_out': 1.208405e-01, 'norm_xattn': 1.314980e-02, 'norm_mem': 1.953121e-02, 'xattn_wq': 1.327325e-02, 'xattn_wkv': 1.334976e-02, 'xattn_wo': 1.341217e-02, 'norm_ffn2': 5.294107e-02, 'ffn2_w_in': 2.225293e-02, 'ffn2_w_out': 3.629547e-02, 'norm_final': 3.202575e+01}


def _to_microbatches(a, axis):
    t = _jnp.moveaxis(a, axis, 0)
    t = t.reshape((N_MICROBATCH, t.shape[0] // N_MICROBATCH) + t.shape[1:])
    return _jnp.moveaxis(t, 1, axis + 1)


def setup_inputs(seed: int = 0) -> dict:
    inp = _fwd_setup_inputs(seed)
    key = _jax.random.fold_in(_jax.random.key(seed), 7919)
    shape, _ = _output_shape()
    out = dict(inp)
    out["loss_target"] = _jax.random.normal(_jax.random.fold_in(key, 0), shape, _jnp.float32)
    for i, name in enumerate(TWIN_WEIGHTS):
        w = inp[name].astype(_jnp.float32)
        if MOMENT_SCALE is None:
            s = _jnp.sqrt(_jnp.mean(_jnp.square(w)) + 1e-30)
        else:
            s = MOMENT_SCALE[name]
        km, kv = _jax.random.split(_jax.random.fold_in(key, i + 1))
        out[name] = w
        out["m_" + name] = s * _jax.random.normal(km, w.shape, _jnp.float32)
        out["v_" + name] = (s * s) * _jax.random.uniform(kv, w.shape, _jnp.float32, 0.5, 1.5)
    if N_MICROBATCH > 1:
        for name, axis in PER_EXAMPLE_BATCH_AXIS.items():
            out[name] = _to_microbatches(out[name], axis)
    return {'x': out['x'], 'mem': out['mem'], 'norm_ffn1': out['norm_ffn1'], 'ffn1_w_in': out['ffn1_w_in'], 'ffn1_w_out': out['ffn1_w_out'], 'norm_mix': out['norm_mix'], 'mix_w_in': out['mix_w_in'], 'sconv_w': out['sconv_w'], 'sgu_norm_g': out['sgu_norm_g'], 'sgu_w': out['sgu_w'], 'sgu_b': out['sgu_b'], 'cconv_w': out['cconv_w'], 'cconv_ln_g': out['cconv_ln_g'], 'cconv_ln_b': out['cconv_ln_b'], 'pool_w': out['pool_w'], 'pool_scale': out['pool_scale'], 'mix_w_out': out['mix_w_out'], 'norm_xattn': out['norm_xattn'], 'norm_mem': out['norm_mem'], 'xattn_wq': out['xattn_wq'], 'xattn_wkv': out['xattn_wkv'], 'xattn_wo': out['xattn_wo'], 'norm_ffn2': out['norm_ffn2'], 'ffn2_w_in': out['ffn2_w_in'], 'ffn2_w_out': out['ffn2_w_out'], 'norm_final': out['norm_final'], 'loss_target': out['loss_target'], 'm_norm_ffn1': out['m_norm_ffn1'], 'm_ffn1_w_in': out['m_ffn1_w_in'], 'm_ffn1_w_out': out['m_ffn1_w_out'], 'm_norm_mix': out['m_norm_mix'], 'm_mix_w_in': out['m_mix_w_in'], 'm_sconv_w': out['m_sconv_w'], 'm_sgu_norm_g': out['m_sgu_norm_g'], 'm_sgu_w': out['m_sgu_w'], 'm_sgu_b': out['m_sgu_b'], 'm_cconv_w': out['m_cconv_w'], 'm_cconv_ln_g': out['m_cconv_ln_g'], 'm_cconv_ln_b': out['m_cconv_ln_b'], 'm_pool_w': out['m_pool_w'], 'm_pool_scale': out['m_pool_scale'], 'm_mix_w_out': out['m_mix_w_out'], 'm_norm_xattn': out['m_norm_xattn'], 'm_norm_mem': out['m_norm_mem'], 'm_xattn_wq': out['m_xattn_wq'], 'm_xattn_wkv': out['m_xattn_wkv'], 'm_xattn_wo': out['m_xattn_wo'], 'm_norm_ffn2': out['m_norm_ffn2'], 'm_ffn2_w_in': out['m_ffn2_w_in'], 'm_ffn2_w_out': out['m_ffn2_w_out'], 'm_norm_final': out['m_norm_final'], 'v_norm_ffn1': out['v_norm_ffn1'], 'v_ffn1_w_in': out['v_ffn1_w_in'], 'v_ffn1_w_out': out['v_ffn1_w_out'], 'v_norm_mix': out['v_norm_mix'], 'v_mix_w_in': out['v_mix_w_in'], 'v_sconv_w': out['v_sconv_w'], 'v_sgu_norm_g': out['v_sgu_norm_g'], 'v_sgu_w': out['v_sgu_w'], 'v_sgu_b': out['v_sgu_b'], 'v_cconv_w': out['v_cconv_w'], 'v_cconv_ln_g': out['v_cconv_ln_g'], 'v_cconv_ln_b': out['v_cconv_ln_b'], 'v_pool_w': out['v_pool_w'], 'v_pool_scale': out['v_pool_scale'], 'v_mix_w_out': out['v_mix_w_out'], 'v_norm_xattn': out['v_norm_xattn'], 'v_norm_mem': out['v_norm_mem'], 'v_xattn_wq': out['v_xattn_wq'], 'v_xattn_wkv': out['v_xattn_wkv'], 'v_xattn_wo': out['v_xattn_wo'], 'v_norm_ffn2': out['v_norm_ffn2'], 'v_ffn2_w_in': out['v_ffn2_w_in'], 'v_ffn2_w_out': out['v_ffn2_w_out'], 'v_norm_final': out['v_norm_final']}


def _loss(weights, diff, rest, loss_target):
    with _jax.named_scope("forward"):
        args = {**rest, TWIN_DIFF_INPUT: diff, **{k: w.astype(_WEIGHT_DTYPES[k]) for k, w in weights.items()}}
        y = _forward(args)
    with _jax.named_scope("loss_head"):
        err = _jnp.square(y.astype(_jnp.float32) - loss_target)
        return 0.5 * _jnp.sum(_jnp.mean(err, axis=-1)) if err.ndim else 0.5 * err


def _adamw(w, g, m, v):
    m = ADAM_B1 * m + (1.0 - ADAM_B1) * g
    v = ADAM_B2 * v + (1.0 - ADAM_B2) * _jnp.square(g)
    m_hat = m / (1.0 - ADAM_B1 ** ADAM_STEP)
    v_hat = v / (1.0 - ADAM_B2 ** ADAM_STEP)
    delta = -ADAM_LR * (m_hat / (_jnp.sqrt(v_hat) + ADAM_EPS) + ADAM_WD * w)
    return delta, m, v


def reference(x, mem, norm_ffn1, ffn1_w_in, ffn1_w_out, norm_mix, mix_w_in, sconv_w, sgu_norm_g, sgu_w, sgu_b, cconv_w, cconv_ln_g, cconv_ln_b, pool_w, pool_scale, mix_w_out, norm_xattn, norm_mem, xattn_wq, xattn_wkv, xattn_wo, norm_ffn2, ffn2_w_in, ffn2_w_out, norm_final, loss_target, m_norm_ffn1, m_ffn1_w_in, m_ffn1_w_out, m_norm_mix, m_mix_w_in, m_sconv_w, m_sgu_norm_g, m_sgu_w, m_sgu_b, m_cconv_w, m_cconv_ln_g, m_cconv_ln_b, m_pool_w, m_pool_scale, m_mix_w_out, m_norm_xattn, m_norm_mem, m_xattn_wq, m_xattn_wkv, m_xattn_wo, m_norm_ffn2, m_ffn2_w_in, m_ffn2_w_out, m_norm_final, v_norm_ffn1, v_ffn1_w_in, v_ffn1_w_out, v_norm_mix, v_mix_w_in, v_sconv_w, v_sgu_norm_g, v_sgu_w, v_sgu_b, v_cconv_w, v_cconv_ln_g, v_cconv_ln_b, v_pool_w, v_pool_scale, v_mix_w_out, v_norm_xattn, v_norm_mem, v_xattn_wq, v_xattn_wkv, v_xattn_wo, v_norm_ffn2, v_ffn2_w_in, v_ffn2_w_out, v_norm_final):
    given = dict(x=x, mem=mem, norm_ffn1=norm_ffn1, ffn1_w_in=ffn1_w_in, ffn1_w_out=ffn1_w_out, norm_mix=norm_mix, mix_w_in=mix_w_in, sconv_w=sconv_w, sgu_norm_g=sgu_norm_g, sgu_w=sgu_w, sgu_b=sgu_b, cconv_w=cconv_w, cconv_ln_g=cconv_ln_g, cconv_ln_b=cconv_ln_b, pool_w=pool_w, pool_scale=pool_scale, mix_w_out=mix_w_out, norm_xattn=norm_xattn, norm_mem=norm_mem, xattn_wq=xattn_wq, xattn_wkv=xattn_wkv, xattn_wo=xattn_wo, norm_ffn2=norm_ffn2, ffn2_w_in=ffn2_w_in, ffn2_w_out=ffn2_w_out, norm_final=norm_final, loss_target=loss_target, m_norm_ffn1=m_norm_ffn1, m_ffn1_w_in=m_ffn1_w_in, m_ffn1_w_out=m_ffn1_w_out, m_norm_mix=m_norm_mix, m_mix_w_in=m_mix_w_in, m_sconv_w=m_sconv_w, m_sgu_norm_g=m_sgu_norm_g, m_sgu_w=m_sgu_w, m_sgu_b=m_sgu_b, m_cconv_w=m_cconv_w, m_cconv_ln_g=m_cconv_ln_g, m_cconv_ln_b=m_cconv_ln_b, m_pool_w=m_pool_w, m_pool_scale=m_pool_scale, m_mix_w_out=m_mix_w_out, m_norm_xattn=m_norm_xattn, m_norm_mem=m_norm_mem, m_xattn_wq=m_xattn_wq, m_xattn_wkv=m_xattn_wkv, m_xattn_wo=m_xattn_wo, m_norm_ffn2=m_norm_ffn2, m_ffn2_w_in=m_ffn2_w_in, m_ffn2_w_out=m_ffn2_w_out, m_norm_final=m_norm_final, v_norm_ffn1=v_norm_ffn1, v_ffn1_w_in=v_ffn1_w_in, v_ffn1_w_out=v_ffn1_w_out, v_norm_mix=v_norm_mix, v_mix_w_in=v_mix_w_in, v_sconv_w=v_sconv_w, v_sgu_norm_g=v_sgu_norm_g, v_sgu_w=v_sgu_w, v_sgu_b=v_sgu_b, v_cconv_w=v_cconv_w, v_cconv_ln_g=v_cconv_ln_g, v_cconv_ln_b=v_cconv_ln_b, v_pool_w=v_pool_w, v_pool_scale=v_pool_scale, v_mix_w_out=v_mix_w_out, v_norm_xattn=v_norm_xattn, v_norm_mem=v_norm_mem, v_xattn_wq=v_xattn_wq, v_xattn_wkv=v_xattn_wkv, v_xattn_wo=v_xattn_wo, v_norm_ffn2=v_norm_ffn2, v_ffn2_w_in=v_ffn2_w_in, v_ffn2_w_out=v_ffn2_w_out, v_norm_final=v_norm_final)
    weights = {n: given[n] for n in TWIN_WEIGHTS}
    shared = {n: given[n] for n in SHARED_INPUTS}
    per_example = {n: given[n] for n in ['x', 'mem']}
    grad_fn = _jax.value_and_grad(_loss, argnums=(0, 1))

    def one_microbatch(ex, loss_target):
        ex = dict(ex)
        diff = ex.pop(TWIN_DIFF_INPUT)
        return grad_fn(weights, diff, {**shared, **ex}, loss_target)

    if N_MICROBATCH == 1:
        loss, (grad_w, grad_x) = one_microbatch(per_example, given["loss_target"])
    else:
        def body(carry, xs):
            loss_sum, grad_sum = carry
            l_k, (gw_k, gx_k) = one_microbatch(xs[0], xs[1])
            with _jax.named_scope("update"):
                return (loss_sum + l_k, _jax.tree.map(_jnp.add, grad_sum, gw_k)), gx_k

        init = (_jnp.zeros((), _jnp.float32), _jax.tree.map(_jnp.zeros_like, weights))
        (loss, grad_w), grad_x = _jax.lax.scan(body, init, (per_example, given["loss_target"]))
    with _jax.named_scope("update"):
        delta_w, new_m, new_v = {}, {}, {}
        for n in TWIN_WEIGHTS:
            delta_w[n], new_m[n], new_v[n] = _adamw(weights[n], grad_w[n], given["m_" + n], given["v_" + n])
    return (loss, grad_x, *[grad_w[n] for n in TWIN_WEIGHTS], *[delta_w[n] for n in TWIN_WEIGHTS],
            *[new_m[n] for n in TWIN_WEIGHTS], *[new_v[n] for n in TWIN_WEIGHTS])
```

```python
import functools

import jax
import jax.numpy as jnp
from jax import lax
from jax.experimental import pallas as pl
from jax.experimental.pallas import tpu as pltpu

F32 = jnp.float32
BF16 = jnp.bfloat16
MESH = pl.DeviceIdType.MESH
N_DEV = 8
EPS = 1e-6
HALO = 32
SGU_CHUNK = 128
CCONV_K = 31
SCONV_K = 3
MIX_W = 256
N_HEADS = 4
VMEM_LIMIT = 56 * 1024 * 1024
ROW_TILE = 512
MIX_TILE = 512

ADAM_LR = 0.001
ADAM_B1 = 0.9
ADAM_B2 = 0.999
ADAM_EPS = 1e-08
ADAM_WD = 0.01
ADAM_STEP = 10

HBM_SPEC = pl.BlockSpec(memory_space=pltpu.HBM)
VMEM_SPEC = pl.BlockSpec(memory_space=pltpu.VMEM)


def _params(*sem):
    return pltpu.CompilerParams(dimension_semantics=tuple(sem), vmem_limit_bytes=VMEM_LIMIT)


def _row_tile(m, pref=None):
    t = min(m, ROW_TILE if pref is None else pref)
    assert m % t == 0, (m, t)
    return t


def _my_index():
    return lax.axis_index("x") * 4 + lax.axis_index("y") * 2 + lax.axis_index("c")


def _peer(mask):
    x, y, c = lax.axis_index("x"), lax.axis_index("y"), lax.axis_index("c")
    px = 1 - x if mask & 4 else x
    py = 1 - y if mask & 2 else y
    pc = 1 - c if mask & 1 else c
    return (px, py, pc), px * 4 + py * 2 + pc


def all_gather(arrs, name):
    n = len(arrs)

    def body(*refs):
        ins, outs = refs[:n], refs[n:2 * n]
        send_sems, recv_sems, loc_sems = refs[2 * n:]
        me = _my_index()
        local = []
        for i in range(n):
            cp = pltpu.make_async_copy(ins[i], outs[i].at[me], loc_sems.at[i])
            cp.start()
            local.append(cp)
        sends = []
        for i in range(n):
            for m in range(1, N_DEV):
                peer, _ = _peer(m)
                cp = pltpu.make_async_remote_copy(
                    src_ref=ins[i], dst_ref=outs[i].at[me],
                    send_sem=send_sems.at[i, m - 1], recv_sem=recv_sems.at[i, m - 1],
                    device_id=peer, device_id_type=MESH)
                cp.start()
                sends.append(cp)
        for i in range(n):
            for m in range(1, N_DEV):
                peer, pidx = _peer(m)
                pltpu.make_async_remote_copy(
                    src_ref=ins[i], dst_ref=outs[i].at[pidx],
                    send_sem=send_sems.at[i, m - 1], recv_sem=recv_sems.at[i, m - 1],
                    device_id=peer, device_id_type=MESH).wait_recv()
        for cp in sends:
            cp.wait_send()
        for cp in local:
            cp.wait()

    return pl.pallas_call(
        body, name=name,
        out_shape=[jax.ShapeDtypeStruct((N_DEV,) + a.shape, a.dtype) for a in arrs],
        in_specs=[HBM_SPEC] * n, out_specs=[HBM_SPEC] * n,
        scratch_shapes=[pltpu.SemaphoreType.DMA((n, N_DEV - 1)),
                        pltpu.SemaphoreType.DMA((n, N_DEV - 1)),
                        pltpu.SemaphoreType.DMA((n,))],
    )(*arrs)


def all_to_all(arrs, name):
    n = len(arrs)

    def body(*refs):
        ins, outs = refs[:n], refs[n:2 * n]
        send_sems, recv_sems, loc_sems = refs[2 * n:]
        me = _my_index()
        local = []
        for i in range(n):
            cp = pltpu.make_async_copy(ins[i].at[me], outs[i].at[me], loc_sems.at[i])
            cp.start()
            local.append(cp)
        sends = []
        for i in range(n):
            for m in range(1, N_DEV):
                peer, pidx = _peer(m)
                cp = pltpu.make_async_remote_copy(
                    src_ref=ins[i].at[pidx], dst_ref=outs[i].at[me],
                    send_sem=send_sems.at[i, m - 1], recv_sem=recv_sems.at[i, m - 1],
                    device_id=peer, device_id_type=MESH)
                cp.start()
                sends.append(cp)
        for i in range(n):
            for m in range(1, N_DEV):
                peer, pidx = _peer(m)
                pltpu.make_async_remote_copy(
                    src_ref=ins[i].at[pidx], dst_ref=outs[i].at[pidx],
                    send_sem=send_sems.at[i, m - 1], recv_sem=recv_sems.at[i, m - 1],
                    device_id=peer, device_id_type=MESH).wait_recv()
        for cp in sends:
            cp.wait_send()
        for cp in local:
            cp.wait()

    return pl.pallas_call(
        body, name=name,
        out_shape=[jax.ShapeDtypeStruct(a.shape, a.dtype) for a in arrs],
        in_specs=[HBM_SPEC] * n, out_specs=[HBM_SPEC] * n,
        scratch_shapes=[pltpu.SemaphoreType.DMA((n, N_DEV - 1)),
                        pltpu.SemaphoreType.DMA((n, N_DEV - 1)),
                        pltpu.SemaphoreType.DMA((n,))],
    )(*arrs)


def all_reduce_small(packed, name):
    r, c = packed.shape

    def body(in_ref, out_ref, gath, send_sems, recv_sems):
        me = _my_index()
        gath[me] = in_ref[...]
        sends = []
        for m in range(1, N_DEV):
            peer, _ = _peer(m)
            cp = pltpu.make_async_remote_copy(
                src_ref=in_ref, dst_ref=gath.at[me],
                send_sem=send_sems.at[m - 1], recv_sem=recv_sems.at[m - 1],
                device_id=peer, device_id_type=MESH)
            cp.start()
            sends.append(cp)
        for m in range(1, N_DEV):
            peer, pidx = _peer(m)
            pltpu.make_async_remote_copy(
                src_ref=in_ref, dst_ref=gath.at[pidx],
                send_sem=send_sems.at[m - 1], recv_sem=recv_sems.at[m - 1],
                device_id=peer, device_id_type=MESH).wait_recv()
        for cp in sends:
            cp.wait_send()
        acc = gath[0]
        for p in range(1, N_DEV):
            acc = acc + gath[p]
        out_ref[...] = acc

    return pl.pallas_call(
        body, name=name,
        out_shape=jax.ShapeDtypeStruct((r, c), F32),
        in_specs=[VMEM_SPEC], out_specs=VMEM_SPEC,
        scratch_shapes=[pltpu.VMEM((N_DEV, r, c), F32),
                        pltpu.SemaphoreType.DMA((N_DEV - 1,)),
                        pltpu.SemaphoreType.DMA((N_DEV - 1,))],
        compiler_params=pltpu.CompilerParams(vmem_limit_bytes=VMEM_LIMIT),
    )(packed)


def _sigmoid(v):
    return 1.0 / (1.0 + jnp.exp(-v))


def _rms_fwd(xf, g):
    r = lax.rsqrt(jnp.mean(xf * xf, axis=-1, keepdims=True) + EPS)
    return xf * r, r


def _rms_bwd(xhat, r, g, dy):
    dg = jnp.sum(dy * xhat, axis=0, keepdims=True)
    dxh = dy * g
    dx = r * (dxh - xhat * jnp.mean(dxh * xhat, axis=-1, keepdims=True))
    return dx, dg


def _ln_stats(v):
    mu = jnp.mean(v, axis=-1, keepdims=True)
    vc = v - mu
    r = lax.rsqrt(jnp.mean(vc * vc, axis=-1, keepdims=True) + EPS)
    return vc * r, r


def _ln_bwd(xhat, r, dxh):
    return r * (dxh - jnp.mean(dxh, axis=-1, keepdims=True)
                - xhat * jnp.mean(dxh * xhat, axis=-1, keepdims=True))


def _dot(a, b):
    return jnp.dot(a, b, preferred_element_type=F32)


def _dot_nt(a, b):
    return lax.dot_general(a, b, (((1,), (1,)), ((), ())), preferred_element_type=F32)


def _dot_tn(a, b):
    return lax.dot_general(a, b, (((0,), (0,)), ((), ())), preferred_element_type=F32)


def _full_weight(w_ref, kind):
    assert kind == "row"
    p, a, b = w_ref.shape
    return w_ref[...].reshape(p * a, b)


def _wspec(wg, layer):
    p, _, a, b = wg.shape
    return pl.BlockSpec((p, None, a, b), lambda *_: (0, layer, 0, 0))


def mm_rows(a, wg, layer, kind, *, gain=None, residual=None, out_dtype=F32, name, tm=None):
    m, k = a.shape
    p, _, wa, wb = wg.shape
    n = p * wb if kind == "col" else wb
    tm = _row_tile(m, tm)
    has_gain, has_res = gain is not None, residual is not None

    def body(*refs):
        refs = list(refs)
        a_ref = refs.pop(0)
        g_ref = refs.pop(0) if has_gain else None
        w_ref = refs.pop(0)
        r_ref = refs.pop(0) if has_res else None
        o_ref = refs.pop(0)
        if has_gain:
            xhat, _ = _rms_fwd(a_ref[...].astype(F32), None)
            h = (xhat * g_ref[...]).astype(BF16)
        else:
            h = a_ref[...].astype(BF16)
        if kind == "col":
            for j in range(p):
                o = _dot(h, w_ref[j])
                if has_res:
                    o = o + r_ref[:, j * wb:(j + 1) * wb]
                o_ref[:, j * wb:(j + 1) * wb] = o.astype(out_dtype)
        else:
            o = _dot(h, _full_weight(w_ref, "row"))
            if has_res:
                o = o + r_ref[...]
            o_ref[...] = o.astype(out_dtype)

    operands = [a]
    in_specs = [pl.BlockSpec((tm, k), lambda i: (i, 0))]
    if has_gain:
        operands.append(gain.reshape(1, k))
        in_specs.append(pl.BlockSpec((1, k), lambda i: (0, 0)))
    operands.append(wg)
    in_specs.append(_wspec(wg, layer))
    if has_res:
        operands.append(residual)
        in_specs.append(pl.BlockSpec((tm, n), lambda i: (i, 0)))
    return pl.pallas_call(
        body, name=name, grid=(m // tm,),
        out_shape=jax.ShapeDtypeStruct((m, n), out_dtype),
        in_specs=in_specs, out_specs=pl.BlockSpec((tm, n), lambda i: (i, 0)),
        compiler_params=_params("parallel"),
    )(*operands)


def mm_nt(dz, wg, layer, kind, *, x=None, gain=None, dx_in=None, name, tm=None):
    m, n = dz.shape
    p, _, wa, wb = wg.shape
    k = wa if kind == "col" else p * wa
    tm = _row_tile(m, tm)
    epi = x is not None
    has_dx = dx_in is not None

    def body(*refs):
        refs = list(refs)
        dz_ref, w_ref = refs.pop(0), refs.pop(0)
        if epi:
            x_ref, g_ref = refs.pop(0), refs.pop(0)
            dxi_ref = refs.pop(0) if has_dx else None
            dx_ref, h_ref, dg_ref = refs
        else:
            (da_ref,) = refs
        dzb = dz_ref[...].astype(BF16)
        if kind == "col":
            da = _dot_nt(dzb[:, 0:wb], w_ref[0])
            for j in range(1, p):
                da = da + _dot_nt(dzb[:, j * wb:(j + 1) * wb], w_ref[j])
        else:
            da = _dot_nt(dzb, _full_weight(w_ref, "row"))
        if not epi:
            da_ref[...] = da
            return
        g = g_ref[...]
        xhat, r = _rms_fwd(x_ref[...].astype(F32), None)
        h_ref[...] = (xhat * g).astype(BF16)
        dx, dg = _rms_bwd(xhat, r, g, da)
        if has_dx:
            dx = dx + dxi_ref[...]
        dx_ref[...] = dx

        @pl.when(pl.program_id(0) == 0)
        def _():
            dg_ref[...] = jnp.zeros_like(dg_ref)
        dg_ref[...] += dg

    row = lambda i: (i, 0)
    operands = [dz, wg]
    in_specs = [pl.BlockSpec((tm, n), row), _wspec(wg, layer)]
    if epi:
        operands += [x, gain.reshape(1, k)]
        in_specs += [pl.BlockSpec((tm, k), row), pl.BlockSpec((1, k), lambda i: (0, 0))]
        if has_dx:
            operands.append(dx_in)
            in_specs.append(pl.BlockSpec((tm, k), row))
        out_shape = [jax.ShapeDtypeStruct((m, k), F32), jax.ShapeDtypeStruct((m, k), BF16),
                     jax.ShapeDtypeStruct((1, k), F32)]
        out_specs = [pl.BlockSpec((tm, k), row), pl.BlockSpec((tm, k), row),
                     pl.BlockSpec((1, k), lambda i: (0, 0))]
    else:
        out_shape = jax.ShapeDtypeStruct((m, k), F32)
        out_specs = pl.BlockSpec((tm, k), row)
    return pl.pallas_call(
        body, name=name, grid=(m // tm,), out_shape=out_shape,
        in_specs=in_specs, out_specs=out_specs,
        compiler_params=_params("arbitrary"),
    )(*operands)


def mm_tn(a, b, *, nb, a_spec, b_spec, ka, nbk, tm, m, scale=1.0, out_dtype=BF16, name):
    ni = m // tm

    def body(a_ref, b_ref, o_ref, acc):
        i = pl.program_id(1)

        @pl.when(i == 0)
        def _():
            acc[...] = jnp.zeros_like(acc)
        acc[...] += _dot_tn(a_ref[...].astype(BF16), b_ref[...].astype(BF16))

        @pl.when(i == ni - 1)
        def _():
            o_ref[...] = (acc[...] * scale).astype(out_dtype)

    return pl.pallas_call(
        body, name=name, grid=(nb, ni),
        out_shape=jax.ShapeDtypeStruct((nb, ka, nbk), out_dtype),
        in_specs=[a_spec, b_spec],
        out_specs=pl.BlockSpec((None, ka, nbk), lambda s, i: (s, 0, 0)),
        scratch_shapes=[pltpu.VMEM((ka, nbk), F32)],
        compiler_params=_params("parallel", "arbitrary"),
    )(a, b)


def _ffn_specs(w_in_g, w_out_g, layer, d):
    p, nl, _, nf = w_in_g.shape
    hr = w_out_g.shape[2]
    assert 2 * hr == nf
    w_in5 = w_in_g.reshape(2, 4, nl, d, nf)
    w_out5 = w_out_g.reshape(4, 2, nl, hr, d)
    in_spec = pl.BlockSpec((2, None, None, d, nf), lambda i, j: (0, j, layer, 0, 0))
    out_spec = pl.BlockSpec((None, 2, None, hr, d), lambda i, j: (j, 0, layer, 0, 0))
    return w_in5, w_out5, in_spec, out_spec, nf


def ffn_fwd(x, gain, w_in_g, w_out_g, layer, *, name, tm=None):
    t, d = x.shape
    tm = _row_tile(t, tm)
    w_in5, w_out5, wi_spec, wo_spec, nf = _ffn_specs(w_in_g, w_out_g, layer, d)

    def body(x_ref, g_ref, wi_ref, wo_ref, o_ref, h_scr, acc):
        j = pl.program_id(1)

        @pl.when(j == 0)
        def _():
            xhat, _ = _rms_fwd(x_ref[...], None)
            h_scr[...] = (xhat * g_ref[...]).astype(BF16)
            acc[...] = jnp.zeros_like(acc)
        h = h_scr[...]
        gt = _dot(h, wi_ref[0])
        up = _dot(h, wi_ref[1])
        act = (gt * _sigmoid(gt) * up).astype(BF16)
        acc[...] += _dot(act, wo_ref[...].reshape(nf, d))

        @pl.when(j == 3)
        def _():
            o_ref[...] = x_ref[...] + 0.5 * acc[...]

    return pl.pallas_call(
        body, name=name, grid=(t // tm, 4),
        out_shape=jax.ShapeDtypeStruct((t, d), F32),
        in_specs=[pl.BlockSpec((tm, d), lambda i, j: (i, 0)),
                  pl.BlockSpec((1, d), lambda i, j: (0, 0)), wi_spec, wo_spec],
        out_specs=pl.BlockSpec((tm, d), lambda i, j: (i, 0)),
        scratch_shapes=[pltpu.VMEM((tm, d), BF16), pltpu.VMEM((tm, d), F32)],
        compiler_params=_params("parallel", "arbitrary"),
    )(x, gain.reshape(1, d), w_in5, w_out5)


def ffn_bwd_rows(x, dy, gain, w_in_g, w_out_g, layer, *, name, tm=None):
    t, d = x.shape
    tm = _row_tile(t, tm)
    w_in5, w_out5, wi_spec, wo_spec, nf = _ffn_specs(w_in_g, w_out_g, layer, d)

    def body(x_ref, dy_ref, g_ref, wi_ref, wo_ref, dx_ref, h_ref, act_ref, dgu_ref, dg_ref,
             dh_acc, dyh_scr):
        i, j = pl.program_id(0), pl.program_id(1)

        @pl.when(j == 0)
        def _():
            xhat, _ = _rms_fwd(x_ref[...], None)
            h_ref[...] = (xhat * g_ref[...]).astype(BF16)
            dyh_scr[...] = (0.5 * dy_ref[...]).astype(BF16)
            dh_acc[...] = jnp.zeros_like(dh_acc)
        h = h_ref[...]
        gt = _dot(h, wi_ref[0])
        up = _dot(h, wi_ref[1])
        sg = _sigmoid(gt)
        silu = gt * sg
        act_ref[...] = (silu * up).astype(BF16)
        dact = _dot_nt(dyh_scr[...], wo_ref[...].reshape(nf, d))
        dgt = (dact * up * (sg * (1.0 + gt * (1.0 - sg)))).astype(BF16)
        dup = (dact * silu).astype(BF16)
        dgu_ref[0] = dgt
        dgu_ref[1] = dup
        dh_acc[...] += _dot_nt(dgt, wi_ref[0]) + _dot_nt(dup, wi_ref[1])

        @pl.when(j == 3)
        def _():
            g = g_ref[...]
            xhat, r = _rms_fwd(x_ref[...], None)
            dx, dg = _rms_bwd(xhat, r, g, dh_acc[...])
            dx_ref[...] = dy_ref[...] + dx

            @pl.when(i == 0)
            def _():
                dg_ref[...] = jnp.zeros_like(dg_ref)
            dg_ref[...] += dg

    row = lambda i, j: (i, 0)
    return pl.pallas_call(
        body, name=name, grid=(t // tm, 4),
        out_shape=[jax.ShapeDtypeStruct((t, d), F32), jax.ShapeDtypeStruct((t, d), BF16),
                   jax.ShapeDtypeStruct((4, t, nf), BF16), jax.ShapeDtypeStruct((2, 4, t, nf), BF16),
                   jax.ShapeDtypeStruct((1, d), F32)],
        in_specs=[pl.BlockSpec((tm, d), row), pl.BlockSpec((tm, d), row),
                  pl.BlockSpec((1, d), lambda i, j: (0, 0)), wi_spec, wo_spec],
        out_specs=[pl.BlockSpec((tm, d), row), pl.BlockSpec((tm, d), row),
                   pl.BlockSpec((None, tm, nf), lambda i, j: (j, i, 0)),
                   pl.BlockSpec((2, None, tm, nf), lambda i, j: (0, j, i, 0)),
                   pl.BlockSpec((1, d), lambda i, j: (0, 0))],
        scratch_shapes=[pltpu.VMEM((tm, d), F32), pltpu.VMEM((tm, d), BF16)],
        compiler_params=_params("arbitrary", "arbitrary"),
    )(x, dy, gain.reshape(1, d), w_in5, w_out5)


def ffn_bwd(x, dy, gain, w_in_g, w_out_g, layer, *, name):
    t, d = x.shape
    dx, h, act, dgu, dgain = ffn_bwd_rows(x, dy, gain, w_in_g, w_out_g, layer, name=name + "_rows")
    nf = act.shape[-1]
    tm = _row_tile(t)
    d_w_in = mm_tn(h, dgu.reshape(8, t, nf), nb=8, ka=d, nbk=nf, tm=tm, m=t,
                   a_spec=pl.BlockSpec((tm, d), lambda s, i: (i, 0)),
                   b_spec=pl.BlockSpec((None, tm, nf), lambda s, i: (s, i, 0)),
                   name=name + "_dwin")
    d_w_out = mm_tn(act, dy, nb=4, ka=nf, nbk=d, tm=tm, m=t, scale=0.5,
                    a_spec=pl.BlockSpec((None, tm, nf), lambda s, i: (s, i, 0)),
                    b_spec=pl.BlockSpec((tm, d), lambda s, i: (i, 0)),
                    name=name + "_dwout")
    return dx, dgain, d_w_in, d_w_out.reshape(8, nf // 2, d)


def _lane_group(shape):
    return lax.shift_right_logical(lax.broadcasted_iota(jnp.int32, shape, 1), 6)


def _pool_count(t0, rows):
    t = (t0 + lax.broadcasted_iota(jnp.int32, (rows, MIX_W), 0) + 1).astype(F32)
    return jnp.minimum(t, _by_group(_lane_group((rows, MIX_W)), 2.0, 4.0, 8.0, 16.0))


def _by_group(grp, v0, v1, v2, v3):
    return jnp.where(grp == 0, v0, jnp.where(grp == 1, v1, jnp.where(grp == 2, v2, v3)))


def _sgu_mix(wt_ref, vnc):
    grp = _lane_group((SGU_CHUNK, MIX_W))
    out = jnp.zeros((SGU_CHUNK, MIX_W), F32)
    for hd in range(N_HEADS):
        out = jnp.where(grp == hd, _dot(wt_ref[hd], vnc), out)
    return out


def _pool_fwd(s1, s2, s3, t0, ts, lo):
    h = lo
    s2[h - 24:h + ts] = s1[h - 24:h + ts] + s1[h - 25:h + ts - 1]
    s3[h - 16:h + ts] = s2[h - 16:h + ts] + s2[h - 18:h + ts - 2]
    sum2 = s2[h:h + ts]
    sum4 = s3[h:h + ts]
    s2[h - 8:h + ts] = s3[h - 8:h + ts] + s3[h - 12:h + ts - 4]
    sum8 = s2[h:h + ts]
    sum16 = sum8 + s2[h - 8:h + ts - 8]
    grp = _lane_group((ts, MIX_W))
    return _by_group(grp, sum2, sum4, sum8, sum16) / _pool_count(t0, ts) - s1[h:h + ts]


def mixer_fwd(z, sconv, cconv, vecs, wt, bexp, pbd, *, name, ts=None):
    t = z.shape[0]
    ts = _row_tile(t, MIX_TILE if ts is None else ts)
    hl = HALO
    w = MIX_W
    nch = ts // SGU_CHUNK

    def body(zc, zp, sconv_ref, cconv_ref, vec_ref, wt_ref, bexp_ref, pbd_ref, y_ref, s1, s2, s3):
        i = pl.program_id(0)
        has_prev = i > 0

        def col(ref, c):
            return ref[:, c * w:(c + 1) * w]

        def prev(c):
            return jnp.where(has_prev, col(zp, c), 0.0)

        s1[0:hl] = prev(1) * prev(2)
        s1[hl:hl + ts] = col(zc, 1) * col(zc, 2)
        cv = sconv_ref[0:1] * s1[hl - 2:hl - 2 + ts]
        for k in range(1, SCONV_K):
            cv = cv + sconv_ref[k:k + 1] * s1[hl - 2 + k:hl - 2 + k + ts]
        y_ref[:, 0:w] = (col(zc, 0) * cv).astype(BF16)

        xhat, _ = _ln_stats(col(zc, 4))
        vn = (xhat * vec_ref[0:1]).astype(BF16)
        for c in range(nch):
            rows = slice(c * SGU_CHUNK, (c + 1) * SGU_CHUNK)
            mixed = _sgu_mix(wt_ref, vn[rows]) + bexp_ref[...]
            y_ref[rows, w:2 * w] = (zc[rows, 3 * w:4 * w] * mixed).astype(BF16)

        s1[0:hl] = prev(5) * _sigmoid(prev(6))
        s1[hl:hl + ts] = col(zc, 5) * _sigmoid(col(zc, 6))
        off = hl - (CCONV_K - 1)
        cv = cconv_ref[0:1] * s1[off:off + ts]
        for k in range(1, CCONV_K):
            cv = cv + cconv_ref[k:k + 1] * s1[off + k:off + k + ts]
        xhat, _ = _ln_stats(cv)
        ln = xhat * vec_ref[1:2] + vec_ref[2:3]
        y_ref[:, 2 * w:3 * w] = (ln * _sigmoid(ln)).astype(BF16)

        s1[0:hl] = prev(7)
        s1[hl:hl + ts] = col(zc, 7)
        pooled = _pool_fwd(s1, s2, s3, i * ts, ts, hl)
        y_ref[:, 3 * w:4 * w] = (_dot(pooled.astype(BF16), pbd_ref[...]) * vec_ref[3:4]).astype(BF16)

    full = lambda shape: pl.BlockSpec(shape, lambda i: (0,) * len(shape))
    return pl.pallas_call(
        body, name=name, grid=(t // ts,),
        out_shape=jax.ShapeDtypeStruct((t, 4 * w), BF16),
        in_specs=[pl.BlockSpec((ts, 8 * w), lambda i: (i, 0)),
                  pl.BlockSpec((hl, 8 * w), lambda i: (jnp.maximum(i * (ts // hl) - 1, 0), 0)),
                  full((8, w)), full((32, w)), full((8, w)), full((N_HEADS, SGU_CHUNK, SGU_CHUNK)),
                  full((SGU_CHUNK, w)), full((w, w))],
        out_specs=pl.BlockSpec((ts, 4 * w), lambda i: (i, 0)),
        scratch_shapes=[pltpu.VMEM((hl + ts, w), F32)] * 3,
        compiler_params=_params("parallel"),
    )(z, z, sconv, cconv, vecs, wt, bexp, pbd)


def mixer_bwd(z, dy, sconv, cconv, vecs, wt, bexp, pbd, *, name, ts=None):
    t = z.shape[0]
    ts = _row_tile(t, MIX_TILE if ts is None else ts)
    hl = HALO
    w = MIX_W
    nch = ts // SGU_CHUNK
    ni = t // ts
    ext = ts + hl

    def body(zc, zp, zn, dyc, dyn, sconv_ref, cconv_ref, vec_ref, wt_ref, bexp_ref, pbd_ref,
             dz_ref, gvec_ref, gcc_ref, gwt_ref, gb_ref, gpbd_ref, s1, s2, s3):
        i = pl.program_id(0)
        has_prev = i > 0
        has_next = i < ni - 1

        @pl.when(i == 0)
        def _():
            gvec_ref[...] = jnp.zeros_like(gvec_ref)
            gcc_ref[...] = jnp.zeros_like(gcc_ref)
            gwt_ref[...] = jnp.zeros_like(gwt_ref)
            gb_ref[...] = jnp.zeros_like(gb_ref)
            gpbd_ref[...] = jnp.zeros_like(gpbd_ref)

        def col(ref, c):
            return ref[:, c * w:(c + 1) * w]

        def prev(c):
            return jnp.where(has_prev, col(zp, c), 0.0)

        def nxt(c):
            return jnp.where(has_next, col(zn, c), 0.0)

        def dnext(c):
            return jnp.where(has_next, col(dyn, c), 0.0)

        def rowsum(v):
            return jnp.sum(v, axis=0, keepdims=True)

        s1[0:hl] = prev(1) * prev(2)
        s1[hl:hl + ts] = col(zc, 1) * col(zc, 2)
        s1[hl + ts:hl + ts + hl] = nxt(1) * nxt(2)
        cv = sconv_ref[0:1] * s1[hl - 2:hl - 2 + ts]
        for k in range(1, SCONV_K):
            cv = cv + sconv_ref[k:k + 1] * s1[hl - 2 + k:hl - 2 + k + ts]
        dya = col(dyc, 0)
        dz_ref[:, 0:w] = (dya * cv).astype(BF16)
        s2[0:ts] = dya * col(zc, 0)
        s2[ts:ext] = dnext(0) * nxt(0)
        dv = sconv_ref[0:1] * s2[2:2 + ts]
        for k in range(1, SCONV_K):
            dv = dv + sconv_ref[k:k + 1] * s2[2 - k:2 - k + ts]
        dz_ref[:, w:2 * w] = (dv * col(zc, 2)).astype(BF16)
        dz_ref[:, 2 * w:3 * w] = (dv * col(zc, 1)).astype(BF16)
        dcv = s2[0:ts]
        for k in range(SCONV_K):
            gvec_ref[k:k + 1] += rowsum(dcv * s1[hl - 2 + k:hl - 2 + k + ts])

        g_sgu = vec_ref[0:1]
        xhat, rstd = _ln_stats(col(zc, 4))
        vn = (xhat * g_sgu).astype(BF16)
        grp = _lane_group((SGU_CHUNK, w))
        lane = lax.broadcasted_iota(jnp.int32, (SGU_CHUNK, SGU_CHUNK), 1)
        tril = lax.broadcasted_iota(jnp.int32, (SGU_CHUNK, SGU_CHUNK), 0) >= lane
        for c in range(nch):
            rows = slice(c * SGU_CHUNK, (c + 1) * SGU_CHUNK)
            vnc = vn[rows]
            mixed = _sgu_mix(wt_ref, vnc) + bexp_ref[...]
            dyb = dyc[rows, w:2 * w]
            dz_ref[rows, 3 * w:4 * w] = (dyb * mixed).astype(BF16)
            dmix = dyb * zc[rows, 3 * w:4 * w]
            dmixb = dmix.astype(BF16)
            dvn = jnp.zeros((SGU_CHUNK, w), F32)
            gb = jnp.zeros((SGU_CHUNK, SGU_CHUNK), F32)
            for hd in range(N_HEADS):
                dvn = jnp.where(grp == hd, _dot_tn(wt_ref[hd], dmixb), dvn)
                dm_h = jnp.where(grp == hd, dmix, 0.0)
                gwt_ref[hd] += jnp.where(tril, _dot_nt(dm_h.astype(BF16), vnc), 0.0)
                gb = gb + jnp.where(lane == hd, jnp.sum(dm_h, axis=1, keepdims=True), 0.0)
            gb_ref[...] += gb
            s3[rows] = dvn
        dvn = s3[0:ts]
        gvec_ref[3:4] += rowsum(dvn * xhat)
        dz_ref[:, 4 * w:5 * w] = _ln_bwd(xhat, rstd, dvn * g_sgu).astype(BF16)

        sig_c = _sigmoid(col(zc, 6))
        s1[0:hl] = prev(5) * _sigmoid(prev(6))
        s1[hl:hl + ts] = col(zc, 5) * sig_c
        s1[hl + ts:hl + ts + hl] = nxt(5) * _sigmoid(nxt(6))
        off = hl - (CCONV_K - 1)
        cv = cconv_ref[0:1] * s1[off:off + ext]
        for k in range(1, CCONV_K):
            cv = cv + cconv_ref[k:k + 1] * s1[off + k:off + k + ext]
        xhat, rstd = _ln_stats(cv)
        ln = xhat * vec_ref[1:2] + vec_ref[2:3]
        sg = _sigmoid(ln)
        s2[0:ts] = col(dyc, 2)
        s2[ts:ext] = dnext(2)
        dln = s2[0:ext] * (sg * (1.0 + ln * (1.0 - sg)))
        gvec_ref[4:5] += rowsum(dln[0:ts] * xhat[0:ts])
        gvec_ref[5:6] += rowsum(dln[0:ts])
        s3[0:ext] = _ln_bwd(xhat, rstd, dln * vec_ref[1:2])
        dyg = cconv_ref[0:1] * s3[CCONV_K - 1:CCONV_K - 1 + ts]
        for k in range(1, CCONV_K):
            dyg = dyg + cconv_ref[k:k + 1] * s3[CCONV_K - 1 - k:CCONV_K - 1 - k + ts]
        dz_ref[:, 5 * w:6 * w] = (dyg * sig_c).astype(BF16)
        dz_ref[:, 6 * w:7 * w] = (dyg * col(zc, 5) * sig_c * (1.0 - sig_c)).astype(BF16)
        dcv = s3[0:ts]
        for k in range(CCONV_K):
            gcc_ref[k:k + 1] += rowsum(dcv * s1[off + k:off + k + ts])

        scale = vec_ref[3:4]
        s1[0:hl] = prev(7)
        s1[hl:hl + ts] = col(zc, 7)
        pooled = _pool_fwd(s1, s2, s3, i * ts, ts, hl).astype(BF16)
        q0 = _dot(pooled, pbd_ref[...])
        dyd = col(dyc, 3)
        gvec_ref[6:7] += rowsum(dyd * q0)
        dq = (dyd * scale).astype(BF16)
        gpbd_ref[...] += _dot_tn(pooled, dq)
        s1[0:ts] = _dot_nt(dq, pbd_ref[...])
        s1[ts:ext] = _dot_nt((dnext(3) * scale).astype(BF16), pbd_ref[...])
        dpool = s1[0:ts]
        s2[0:ext] = s1[0:ext] / _pool_count(i * ts, ext)
        s3[0:ts + 24] = s2[0:ts + 24] + s2[1:ts + 25]
        f2 = s3[0:ts]
        s2[0:ts + 16] = s3[0:ts + 16] + s3[2:ts + 18]
        f4 = s2[0:ts]
        s3[0:ts + 8] = s2[0:ts + 8] + s2[4:ts + 12]
        f8 = s3[0:ts]
        f16 = f8 + s3[8:ts + 8]
        dz_ref[:, 7 * w:8 * w] = (_by_group(_lane_group((ts, w)), f2, f4, f8, f16) - dpool).astype(BF16)

    full = lambda shape: pl.BlockSpec(shape, lambda i: (0,) * len(shape))
    r = ts // hl
    prev_map = lambda i: (jnp.maximum(i * r - 1, 0), 0)
    next_map = lambda i: (jnp.minimum((i + 1) * r, t // hl - 1), 0)
    return pl.pallas_call(
        body, name=name, grid=(ni,),
        out_shape=[jax.ShapeDtypeStruct((t, 8 * w), BF16), jax.ShapeDtypeStruct((8, w), F32),
                   jax.ShapeDtypeStruct((32, w), F32),
                   jax.ShapeDtypeStruct((N_HEADS, SGU_CHUNK, SGU_CHUNK), F32),
                   jax.ShapeDtypeStruct((SGU_CHUNK, SGU_CHUNK), F32), jax.ShapeDtypeStruct((w, w), F32)],
        in_specs=[pl.BlockSpec((ts, 8 * w), lambda i: (i, 0)),
                  pl.BlockSpec((hl, 8 * w), prev_map), pl.BlockSpec((hl, 8 * w), next_map),
                  pl.BlockSpec((ts, 4 * w), lambda i: (i, 0)), pl.BlockSpec((hl, 4 * w), next_map),
                  full((8, w)), full((32, w)), full((8, w)), full((N_HEADS, SGU_CHUNK, SGU_CHUNK)),
                  full((SGU_CHUNK, w)), full((w, w))],
        out_specs=[pl.BlockSpec((ts, 8 * w), lambda i: (i, 0)), full((8, w)), full((32, w)),
                   full((N_HEADS, SGU_CHUNK, SGU_CHUNK)), full((SGU_CHUNK, SGU_CHUNK)), full((w, w))],
        scratch_shapes=[pltpu.VMEM((ts + 2 * hl, w), F32)] * 3,
        compiler_params=_params("arbitrary"),
    )(z, z, z, dy, dy, sconv, cconv, vecs, wt, bexp, pbd)


def _attn_head(q, kv_ref, hd, d):
    hw = d // N_HEADS
    qh = q[:, hd * hw:(hd + 1) * hw]
    kh = kv_ref[:, hd * hw:(hd + 1) * hw].astype(BF16)
    vh = kv_ref[:, d + hd * hw:d + (hd + 1) * hw].astype(BF16)
    s = _dot_nt(qh, kh) * (1.0 / (hw ** 0.5))
    e = jnp.exp(s - jnp.max(s, axis=-1, keepdims=True))
    p = e / jnp.sum(e, axis=-1, keepdims=True)
    return qh, kh, vh, p


def xattn_fwd(x, gain, kv, wq_g, wo_g, layer, *, name, tm=None):
    t, d = x.shape
    nm = kv.shape[0]
    tm = _row_tile(t, tm)
    hw = d // N_HEADS

    def body(x_ref, g_ref, kv_ref, wq_ref, wo_ref, o_ref):
        xv = x_ref[...]
        xhat, _ = _rms_fwd(xv, None)
        h = (xhat * g_ref[...]).astype(BF16)
        q = _dot(h, _full_weight(wq_ref, "row")).astype(BF16)
        wo = _full_weight(wo_ref, "row")
        out = xv
        for hd in range(N_HEADS):
            _, _, vh, p = _attn_head(q, kv_ref, hd, d)
            oh = _dot(p.astype(BF16), vh).astype(BF16)
            out = out + _dot(oh, wo[hd * hw:(hd + 1) * hw])
        o_ref[...] = out

    row = lambda i: (i, 0)
    return pl.pallas_call(
        body, name=name, grid=(t // tm,),
        out_shape=jax.ShapeDtypeStruct((t, d), F32),
        in_specs=[pl.BlockSpec((tm, d), row), pl.BlockSpec((1, d), lambda i: (0, 0)),
                  pl.BlockSpec((nm, 2 * d), lambda i: (0, 0)), _wspec(wq_g, layer), _wspec(wo_g, layer)],
        out_specs=pl.BlockSpec((tm, d), row),
        compiler_params=_params("parallel"),
    )(x, gain.reshape(1, d), kv, wq_g, wo_g)


def xattn_bwd_rows(x, dxn, gain, kv, wq_g, wo_g, layer, *, name, tm=None):
    t, d = x.shape
    nm = kv.shape[0]
    tm = _row_tile(t, tm)
    hw = d // N_HEADS

    def body(x_ref, dxn_ref, g_ref, kv_ref, wq_ref, wo_ref,
             dx_ref, h_ref, dq_ref, o_ref, dkv_ref, dg_ref):
        i = pl.program_id(0)

        @pl.when(i == 0)
        def _():
            dkv_ref[...] = jnp.zeros_like(dkv_ref)
            dg_ref[...] = jnp.zeros_like(dg_ref)
        g = g_ref[...]
        xhat, r = _rms_fwd(x_ref[...], None)
        h = (xhat * g).astype(BF16)
        h_ref[...] = h
        wq = _full_weight(wq_ref, "row")
        q = _dot(h, wq).astype(BF16)
        dxn = dxn_ref[...]
        do = _dot_nt(dxn.astype(BF16), _full_weight(wo_ref, "row")).astype(BF16)
        for hd in range(N_HEADS):
            cols = slice(hd * hw, (hd + 1) * hw)
            qh, kh, vh, p = _attn_head(q, kv_ref, hd, d)
            pb = p.astype(BF16)
            o_ref[:, cols] = _dot(pb, vh).astype(BF16)
            doh = do[:, cols]
            dkv_ref[:, d + hd * hw:d + (hd + 1) * hw] += _dot_tn(pb, doh)
            dp = _dot_nt(doh, vh)
            ds = (p * (dp - jnp.sum(dp * p, axis=-1, keepdims=True)) * (1.0 / (hw ** 0.5))).astype(BF16)
            dq_ref[:, cols] = _dot(ds, kh).astype(BF16)
            dkv_ref[:, cols] += _dot_tn(ds, qh)
        dh = _dot_nt(dq_ref[...], wq)
        dx, dg = _rms_bwd(xhat, r, g, dh)
        dx_ref[...] = dxn + dx
        dg_ref[...] += dg

    row = lambda i: (i, 0)
    fix = lambda i: (0, 0)
    return pl.pallas_call(
        body, name=name, grid=(t // tm,),
        out_shape=[jax.ShapeDtypeStruct((t, d), F32), jax.ShapeDtypeStruct((t, d), BF16),
                   jax.ShapeDtypeStruct((t, d), BF16), jax.ShapeDtypeStruct((t, d), BF16),
                   jax.ShapeDtypeStruct((nm, 2 * d), F32), jax.ShapeDtypeStruct((1, d), F32)],
        in_specs=[pl.BlockSpec((tm, d), row), pl.BlockSpec((tm, d), row), pl.BlockSpec((1, d), fix),
                  pl.BlockSpec((nm, 2 * d), fix), _wspec(wq_g, layer), _wspec(wo_g, layer)],
        out_specs=[pl.BlockSpec((tm, d), row)] * 4 + [pl.BlockSpec((nm, 2 * d), fix),
                                                      pl.BlockSpec((1, d), fix)],
        compiler_params=_params("arbitrary"),
    )(x, dxn, gain.reshape(1, d), kv, wq_g, wo_g)


def loss_head(x, target, gain, *, name, tm=None):
    t, d = x.shape
    tm = _row_tile(t, tm)

    def body(x_ref, t_ref, g_ref, dx_ref, dg_ref, loss_ref):
        @pl.when(pl.program_id(0) == 0)
        def _():
            dg_ref[...] = jnp.zeros_like(dg_ref)
            loss_ref[...] = jnp.zeros_like(loss_ref)
        g = g_ref[...]
        xhat, r = _rms_fwd(x_ref[...], None)
        err = xhat * g - t_ref[...]
        loss_ref[...] += 0.5 * jnp.sum(jnp.sum(err * err, axis=-1, keepdims=True) / d,
                                       axis=0, keepdims=True)
        dx, dg = _rms_bwd(xhat, r, g, err / d)
        dx_ref[...] = dx
        dg_ref[...] += dg

    row = lambda i: (i, 0)
    fix = lambda i: (0, 0)
    return pl.pallas_call(
        body, name=name, grid=(t // tm,),
        out_shape=[jax.ShapeDtypeStruct((t, d), F32), jax.ShapeDtypeStruct((1, d), F32),
                   jax.ShapeDtypeStruct((1, 1), F32)],
        in_specs=[pl.BlockSpec((tm, d), row), pl.BlockSpec((tm, d), row), pl.BlockSpec((1, d), fix)],
        out_specs=[pl.BlockSpec((tm, d), row), pl.BlockSpec((1, d), fix), pl.BlockSpec((1, 1), fix)],
        compiler_params=_params("arbitrary"),
    )(x, target, gain.reshape(1, d))


def _adamw_math(w, g, m, v):
    m = ADAM_B1 * m + (1.0 - ADAM_B1) * g
    v = ADAM_B2 * v + (1.0 - ADAM_B2) * (g * g)
    m_hat = m / (1.0 - ADAM_B1 ** ADAM_STEP)
    v_hat = v / (1.0 - ADAM_B2 ** ADAM_STEP)
    delta = -ADAM_LR * (m_hat / (jnp.sqrt(v_hat) + ADAM_EPS) + ADAM_WD * w)
    return delta, m, v


def adamw_sharded(recv, w, m, v, *, name):
    nl, r, c = w.shape
    assert nl == len(recv) == 2
    tr = next(cand for cand in (256, 176, 128, r) if r % cand == 0)
    nr = r // tr

    def body(r0, r1, w_ref, m_ref, v_ref, g_out, d_out, m_out, v_out):
        layer = pl.program_id(0)

        def total(ref):
            acc = ref[0].astype(F32)
            for p in range(1, N_DEV):
                acc = acc + ref[p].astype(F32)
            return acc
        g = jnp.where(layer == 0, total(r0), total(r1))
        delta, mn, vn = _adamw_math(w_ref[...], g, m_ref[...], v_ref[...])
        g_out[...] = g
        d_out[...] = delta
        m_out[...] = mn
        v_out[...] = vn

    blk = pl.BlockSpec((None, tr, c), lambda l, i: (l, i, 0))
    return pl.pallas_call(
        body, name=name, grid=(nl, nr),
        out_shape=[jax.ShapeDtypeStruct((nl, r, c), F32)] * 4,
        in_specs=[pl.BlockSpec((N_DEV, tr, c), lambda l, i: (0, jnp.where(l == 0, i, nr - 1), 0)),
                  pl.BlockSpec((N_DEV, tr, c), lambda l, i: (0, jnp.where(l == 1, i, 0), 0)),
                  blk, blk, blk],
        out_specs=[blk] * 4,
        compiler_params=_params("arbitrary", "arbitrary"),
    )(recv[0], recv[1], w, m, v)


def adamw_flat(g, w, m, v, *, name):
    def body(g_ref, w_ref, m_ref, v_ref, d_out, m_out, v_out):
        delta, mn, vn = _adamw_math(w_ref[...], g_ref[...], m_ref[...], v_ref[...])
        d_out[...] = delta
        m_out[...] = mn
        v_out[...] = vn

    return pl.pallas_call(
        body, name=name, out_shape=[jax.ShapeDtypeStruct(w.shape, F32)] * 3,
        in_specs=[VMEM_SPEC] * 4, out_specs=[VMEM_SPEC] * 3,
        compiler_params=pltpu.CompilerParams(vmem_limit_bytes=VMEM_LIMIT),
    )(g, w, m, v)


def cast_bf16(a, *, name):
    nl, r, c = a.shape
    tr = next(cand for cand in (256, 176, 128, r) if r % cand == 0)

    def body(a_ref, o_ref):
        o_ref[...] = a_ref[...].astype(BF16)

    blk = pl.BlockSpec((None, tr, c), lambda l, i: (l, i, 0))
    return pl.pallas_call(
        body, name=name, grid=(nl, r // tr), out_shape=jax.ShapeDtypeStruct(a.shape, BF16),
        in_specs=[blk], out_specs=blk, compiler_params=_params("parallel", "parallel"),
    )(a)


def _pack(arrs, rows):
    flat = jnp.concatenate([a.reshape(-1).astype(F32) for a in arrs])
    pad = rows * 128 - flat.shape[0]
    assert pad >= 0
    if pad:
        flat = jnp.concatenate([flat, jnp.zeros((pad,), F32)])
    return flat.reshape(rows, 128)


def _unpack(packed, shapes):
    flat = packed.reshape(-1)
    out, pos = [], 0
    for s in shapes:
        n = 1
        for dim in s:
            n *= dim
        out.append(flat[pos:pos + n].reshape(s))
        pos += n
    return out


def _rows_for(shapes):
    n = 0
    for s in shapes:
        k = 1
        for dim in s:
            k *= dim
        n += k
    return -(-n // 1024) * 8


BIG = ["ffn1_w_in", "ffn1_w_out", "mix_w_in", "mix_w_out", "xattn_wq", "xattn_wkv", "xattn_wo",
       "ffn2_w_in", "ffn2_w_out"]
SMALL_REPL = ["norm_ffn1", "norm_mix", "sgu_norm_g", "sgu_w", "sgu_b", "cconv_ln_g", "cconv_ln_b",
              "pool_w", "pool_scale", "norm_xattn", "norm_mem", "norm_ffn2", "norm_final"]
SMALL_SHARD = ["sconv_w", "cconv_w"]
WEIGHTS = ["norm_ffn1", "ffn1_w_in", "ffn1_w_out", "norm_mix", "mix_w_in", "sconv_w", "sgu_norm_g",
           "sgu_w", "sgu_b", "cconv_w", "cconv_ln_g", "cconv_ln_b", "pool_w", "pool_scale", "mix_w_out",
           "norm_xattn", "norm_mem", "xattn_wq", "xattn_wkv", "xattn_wo", "norm_ffn2", "ffn2_w_in",
           "ffn2_w_out", "norm_final"]


def kernel(x, mem, norm_ffn1, ffn1_w_in, ffn1_w_out, norm_mix, mix_w_in, sconv_w, sgu_norm_g, sgu_w, sgu_b, cconv_w, cconv_ln_g, cconv_ln_b, pool_w, pool_scale, mix_w_out, norm_xattn, norm_mem, xattn_wq, xattn_wkv, xattn_wo, norm_ffn2, ffn2_w_in, ffn2_w_out, norm_final, loss_target, m_norm_ffn1, m_ffn1_w_in, m_ffn1_w_out, m_norm_mix, m_mix_w_in, m_sconv_w, m_sgu_norm_g, m_sgu_w, m_sgu_b, m_cconv_w, m_cconv_ln_g, m_cconv_ln_b, m_pool_w, m_pool_scale, m_mix_w_out, m_norm_xattn, m_norm_mem, m_xattn_wq, m_xattn_wkv, m_xattn_wo, m_norm_ffn2, m_ffn2_w_in, m_ffn2_w_out, m_norm_final, v_norm_ffn1, v_ffn1_w_in, v_ffn1_w_out, v_norm_mix, v_mix_w_in, v_sconv_w, v_sgu_norm_g, v_sgu_w, v_sgu_b, v_cconv_w, v_cconv_ln_g, v_cconv_ln_b, v_pool_w, v_pool_scale, v_mix_w_out, v_norm_xattn, v_norm_mem, v_xattn_wq, v_xattn_wkv, v_xattn_wo, v_norm_ffn2, v_ffn2_w_in, v_ffn2_w_out, v_norm_final):
    args = dict(locals())
    wts = {n: args[n] for n in WEIGHTS}
    mom = {n: args["m_" + n] for n in WEIGHTS}
    var = {n: args["v_" + n] for n in WEIGHTS}
    x0 = x[0]
    mem0 = mem[0]
    target = loss_target[0]
    t, d = x0.shape
    nl = norm_ffn1.shape[0]
    w = MIX_W
    me = _my_index()

    shards = [cast_bf16(wts[n], name="cast_" + n) for n in BIG]
    gathered = all_gather(shards + [sconv_w, cconv_w], name="gather_weights")
    wg = dict(zip(BIG, gathered[:len(BIG)]))
    sconv_full = jnp.transpose(gathered[-2], (1, 2, 0, 3)).reshape(nl, SCONV_K, w)
    cconv_full = jnp.transpose(gathered[-1], (1, 2, 0, 3)).reshape(nl, CCONV_K, w)
    sconv_pad = jnp.pad(sconv_full, ((0, 0), (0, 8 - SCONV_K), (0, 0)))
    cconv_pad = jnp.pad(cconv_full, ((0, 0), (0, 32 - CCONV_K), (0, 0)))
    zeros_w = jnp.zeros((nl, w), F32)
    vecs = jnp.stack([sgu_norm_g, cconv_ln_g, cconv_ln_b, pool_scale] + [zeros_w] * 4, axis=1)
    wt = jnp.tril(sgu_w).astype(BF16)
    bexp = jnp.repeat(jnp.swapaxes(sgu_b, 1, 2), w // N_HEADS, axis=2)
    eye = jnp.eye(4, dtype=F32)
    pbd = jnp.einsum("lgcd,gh->lgchd", pool_w, eye).reshape(nl, w, w).astype(BF16)

    def mixer_args(l):
        return sconv_pad[l], cconv_pad[l], vecs[l], wt[l], bexp[l], pbd[l]

    saved = []
    xc = x0
    for l in range(nl):
        s = {"x_ffn1": xc}
        xc = ffn_fwd(xc, norm_ffn1[l], wg["ffn1_w_in"], wg["ffn1_w_out"], l, name=f"ffn1_fwd{l}")
        s["x_mix"] = xc
        z = mm_rows(xc, wg["mix_w_in"], l, "col", gain=norm_mix[l], name=f"mix_in{l}")
        y = mixer_fwd(z, *mixer_args(l), name=f"mixer_fwd{l}")
        s["z"], s["y"] = z, y
        xc = mm_rows(y, wg["mix_w_out"], l, "row", residual=xc, name=f"mix_out{l}")
        s["x_att"] = xc
        kv = mm_rows(mem0, wg["xattn_wkv"], l, "col", gain=norm_mem[l], name=f"kv{l}")
        s["kv"] = kv
        xc = xattn_fwd(xc, norm_xattn[l], kv, wg["xattn_wq"], wg["xattn_wo"], l, name=f"xattn_fwd{l}")
        s["x_ffn2"] = xc
        xc = ffn_fwd(xc, norm_ffn2[l], wg["ffn2_w_in"], wg["ffn2_w_out"], l, name=f"ffn2_fwd{l}")
        saved.append(s)

    dx, g_norm_final, loss_local = loss_head(xc, target, norm_final, name="loss_head")
    loss = lax.psum(loss_local[0, 0], ("x", "y", "c"))

    tm = _row_tile(t)
    big_grads = [None] * nl
    small = {n: [None] * nl for n in SMALL_REPL + SMALL_SHARD if n != "norm_final"}
    for l in reversed(range(nl)):
        s = saved[l]
        bg = {}
        dxn = dx
        dx, dgn, bg["ffn2_w_in"], bg["ffn2_w_out"] = ffn_bwd(
            s["x_ffn2"], dxn, norm_ffn2[l], wg["ffn2_w_in"], wg["ffn2_w_out"], l, name=f"ffn2_bwd{l}")
        small["norm_ffn2"][l] = dgn[0]

        dxn = dx
        dx, h, dq, o, dkv, dgn = xattn_bwd_rows(
            s["x_att"], dxn, norm_xattn[l], s["kv"], wg["xattn_wq"], wg["xattn_wo"], l,
            name=f"xattn_bwd{l}")
        small["norm_xattn"][l] = dgn[0]
        row_spec = pl.BlockSpec((tm, d), lambda s_, i: (i, 0))
        bg["xattn_wq"] = mm_tn(h, dq, nb=1, ka=d, nbk=d, tm=tm, m=t, a_spec=row_spec, b_spec=row_spec,
                               name=f"dwq{l}").reshape(N_DEV, d // N_DEV, d)
        bg["xattn_wo"] = mm_tn(o, dxn, nb=1, ka=d, nbk=d, tm=tm, m=t, a_spec=row_spec, b_spec=row_spec,
                               name=f"dwo{l}").reshape(N_DEV, d // N_DEV, d)
        _, mhat, dgn = mm_nt(dkv, wg["xattn_wkv"], l, "col", x=mem0, gain=norm_mem[l], name=f"dmem{l}")
        small["norm_mem"][l] = dgn[0]
        nm = mem0.shape[0]
        bg["xattn_wkv"] = mm_tn(mhat, dkv, nb=N_DEV, ka=d, nbk=2 * d // N_DEV, tm=nm, m=nm,
                                a_spec=pl.BlockSpec((nm, d), lambda s_, i: (0, 0)),
                                b_spec=pl.BlockSpec((nm, 2 * d // N_DEV), lambda s_, i: (0, s_)),
                                name=f"dwkv{l}")

        dxn = dx
        bg["mix_w_out"] = mm_tn(s["y"], dxn, nb=1, ka=d, nbk=d, tm=tm, m=t, a_spec=row_spec,
                                b_spec=row_spec, name=f"dwmo{l}").reshape(N_DEV, d // N_DEV, d)
        dy = mm_nt(dxn, wg["mix_w_out"], l, "row", name=f"dy_mix{l}")
        dz, gvec, gcc, gwt, gb, gpbd = mixer_bwd(s["z"], dy, *mixer_args(l), name=f"mixer_bwd{l}")
        small["sconv_w"][l] = gvec[0:SCONV_K]
        small["sgu_norm_g"][l] = gvec[3]
        small["cconv_ln_g"][l] = gvec[4]
        small["cconv_ln_b"][l] = gvec[5]
        small["pool_scale"][l] = gvec[6]
        small["cconv_w"][l] = gcc[0:CCONV_K]
        small["sgu_w"][l] = gwt
        small["sgu_b"][l] = jnp.transpose(gb[:, 0:N_HEADS])
        gw = w // 4
        small["pool_w"][l] = jnp.stack([gpbd[g * gw:(g + 1) * gw, g * gw:(g + 1) * gw] for g in range(4)])
        dx, h, dgn = mm_nt(dz, wg["mix_w_in"], l, "col", x=s["x_mix"], gain=norm_mix[l], dx_in=dxn,
                           name=f"dh_mix{l}")
        small["norm_mix"][l] = dgn[0]
        bg["mix_w_in"] = mm_tn(h, dz, nb=N_DEV, ka=d, nbk=w, tm=tm, m=t,
                               a_spec=pl.BlockSpec((tm, d), lambda s_, i: (i, 0)),
                               b_spec=pl.BlockSpec((tm, w), lambda s_, i: (i, s_)), name=f"dwmi{l}")

        dx, dgn, bg["ffn1_w_in"], bg["ffn1_w_out"] = ffn_bwd(
            s["x_ffn1"], dx, norm_ffn1[l], wg["ffn1_w_in"], wg["ffn1_w_out"], l, name=f"ffn1_bwd{l}")
        small["norm_ffn1"][l] = dgn[0]
        big_grads[l] = bg

    recv = [all_to_all([big_grads[l][n] for n in BIG], name=f"scatter_grads{l}") for l in range(nl)]
    small_full = {n: jnp.stack(v) for n, v in small.items()}
    small_full["norm_final"] = g_norm_final[0]
    names = SMALL_REPL + SMALL_SHARD
    shapes = [small_full[n].shape for n in names]
    rows_all = _rows_for(shapes)
    summed = all_reduce_small(_pack([small_full[n] for n in names], rows_all), name="reduce_small")
    gsm = dict(zip(names, _unpack(summed, shapes)))

    out = {}
    for k, n in enumerate(BIG):
        sh = wts[n].shape
        out[n] = adamw_sharded([recv[l][k] for l in range(nl)], wts[n], mom[n], var[n], name="adamw_" + n)
        assert out[n][0].shape == sh
    repl_shapes = [wts[n].shape for n in SMALL_REPL]
    rows_r = _rows_for(repl_shapes)
    dl, mn, vn = adamw_flat(_pack([gsm[n] for n in SMALL_REPL], rows_r),
                            _pack([wts[n] for n in SMALL_REPL], rows_r),
                            _pack([mom[n] for n in SMALL_REPL], rows_r),
                            _pack([var[n] for n in SMALL_REPL], rows_r), name="adamw_small")
    for n, a, b, c in zip(SMALL_REPL, _unpack(dl, repl_shapes), _unpack(mn, repl_shapes),
                          _unpack(vn, repl_shapes)):
        out[n] = (gsm[n], a, b, c)
    cs = w // N_DEV
    gsh = {n: lax.dynamic_slice_in_dim(gsm[n], me * cs, cs, axis=2) for n in SMALL_SHARD}
    sh_shapes = [wts[n].shape for n in SMALL_SHARD]
    rows_s = _rows_for(sh_shapes)
    dl, mn, vn = adamw_flat(_pack([gsh[n] for n in SMALL_SHARD], rows_s),
                            _pack([wts[n] for n in SMALL_SHARD], rows_s),
                            _pack([mom[n] for n in SMALL_SHARD], rows_s),
                            _pack([var[n] for n in SMALL_SHARD], rows_s), name="adamw_small_sharded")
    for n, a, b, c in zip(SMALL_SHARD, _unpack(dl, sh_shapes), _unpack(mn, sh_shapes),
                          _unpack(vn, sh_shapes)):
        out[n] = (gsh[n], a, b, c)

    grad_x = dx.reshape(1, t, d)
    return (loss, grad_x, *[out[n][0] for n in WEIGHTS], *[out[n][1] for n in WEIGHTS],
            *[out[n][2] for n in WEIGHTS], *[out[n][3] for n in WEIGHTS])
```

```python
import functools

import jax
import jax.numpy as jnp
from jax import lax
from jax.experimental import pallas as pl
from jax.experimental.pallas import tpu as pltpu

F32 = jnp.float32
BF16 = jnp.bfloat16
MESH = pl.DeviceIdType.MESH
N_DEV = 8
EPS = 1e-6
HALO = 32
SGU_CHUNK = 128
CCONV_K = 31
SCONV_K = 3
MIX_W = 256
N_HEADS = 4
VMEM_LIMIT = 56 * 1024 * 1024
ROW_TILE = 512
MIX_TILE = 512

ADAM_LR = 0.001
ADAM_B1 = 0.9
ADAM_B2 = 0.999
ADAM_EPS = 1e-08
ADAM_WD = 0.01
ADAM_STEP = 10

HBM_SPEC = pl.BlockSpec(memory_space=pltpu.HBM)
VMEM_SPEC = pl.BlockSpec(memory_space=pltpu.VMEM)


def _params(*sem):
    return pltpu.CompilerParams(dimension_semantics=tuple(sem), vmem_limit_bytes=VMEM_LIMIT)


def _row_tile(m, pref=None):
    t = min(m, ROW_TILE if pref is None else pref)
    assert m % t == 0, (m, t)
    return t


def _my_index():
    return lax.axis_index("x") * 4 + lax.axis_index("y") * 2 + lax.axis_index("c")


def _peer(mask):
    x, y, c = lax.axis_index("x"), lax.axis_index("y"), lax.axis_index("c")
    px = 1 - x if mask & 4 else x
    py = 1 - y if mask & 2 else y
    pc = 1 - c if mask & 1 else c
    return (px, py, pc), px * 4 + py * 2 + pc


def all_gather(arrs, name):
    n = len(arrs)

    def body(*refs):
        ins, outs = refs[:n], refs[n:2 * n]
        send_sems, recv_sems, loc_sems = refs[2 * n:]
        me = _my_index()
        local = []
        for i in range(n):
            cp = pltpu.make_async_copy(ins[i], outs[i].at[me], loc_sems.at[i])
            cp.start()
            local.append(cp)
        sends = []
        for i in range(n):
            for m in range(1, N_DEV):
                peer, _ = _peer(m)
                cp = pltpu.make_async_remote_copy(
                    src_ref=ins[i], dst_ref=outs[i].at[me],
                    send_sem=send_sems.at[i, m - 1], recv_sem=recv_sems.at[i, m - 1],
                    device_id=peer, device_id_type=MESH)
                cp.start()
                sends.append(cp)
        for i in range(n):
            for m in range(1, N_DEV):
                peer, pidx = _peer(m)
                pltpu.make_async_remote_copy(
                    src_ref=ins[i], dst_ref=outs[i].at[pidx],
                    send_sem=send_sems.at[i, m - 1], recv_sem=recv_sems.at[i, m - 1],
                    device_id=peer, device_id_type=MESH).wait_recv()
        for cp in sends:
            cp.wait_send()
        for cp in local:
            cp.wait()

    return pl.pallas_call(
        body, name=name,
        out_shape=[jax.ShapeDtypeStruct((N_DEV,) + a.shape, a.dtype) for a in arrs],
        in_specs=[HBM_SPEC] * n, out_specs=[HBM_SPEC] * n,
        scratch_shapes=[pltpu.SemaphoreType.DMA((n, N_DEV - 1)),
                        pltpu.SemaphoreType.DMA((n, N_DEV - 1)),
                        pltpu.SemaphoreType.DMA((n,))],
    )(*arrs)


SEM_SPEC = pl.BlockSpec(memory_space=pltpu.SEMAPHORE)
ANY_SPEC = pl.BlockSpec(memory_space=pl.ANY)
SIDE_EFFECT = pltpu.SideEffectType.DATAFLOW_SIDE_EFFECTING


def _hbm(a):
    return pltpu.with_memory_space_constraint(a, pltpu.HBM)


def _sem_pairs(n):
    return (pltpu.SemaphoreType.DMA((n * (N_DEV - 1),)), pltpu.SemaphoreType.DMA((n * (N_DEV - 1),)))


def _sem(i, m):
    return i * (N_DEV - 1) + m - 1


def _gather_copy(g_ref, i, m, send_sems, recv_sems, origin):
    peer, _ = _peer(m)
    return pltpu.make_async_remote_copy(
        src_ref=g_ref.at[origin], dst_ref=g_ref.at[origin],
        send_sem=send_sems.at[_sem(i, m)], recv_sem=recv_sems.at[_sem(i, m)],
        device_id=peer, device_id_type=MESH)


def gather_start(gs, after, name):
    n = len(gs)

    def body(*refs):
        g_in = refs[:n]
        send_sems, recv_sems = refs[n + 1], refs[n + 2]
        token = refs[-1]
        me = _my_index()
        for i in range(n):
            for m in range(1, N_DEV):
                _gather_copy(g_in[i], i, m, send_sems, recv_sems, me).start()
        token[...] = jnp.zeros_like(token)

    outs = pl.pallas_call(
        body, name=name,
        out_shape=(*_sem_pairs(n), *[pltpu.HBM(g.shape, g.dtype) for g in gs],
                   jax.ShapeDtypeStruct((8, 128), F32)),
        in_specs=[HBM_SPEC] * n + [ANY_SPEC],
        out_specs=(SEM_SPEC, SEM_SPEC, *[HBM_SPEC] * n, VMEM_SPEC),
        input_output_aliases={i: 2 + i for i in range(n)},
        compiler_params=pltpu.CompilerParams(has_side_effects=SIDE_EFFECT),
    )(*[_hbm(g) for g in gs], after)
    return outs[0], outs[1], list(outs[2:2 + n]), outs[-1]


def gather_wait(gs, send_sems, recv_sems, after, name):
    n = len(gs)

    def body(*refs):
        g_in = refs[:n]
        send, recv = refs[n], refs[n + 1]
        me = _my_index()
        for i in range(n):
            for m in range(1, N_DEV):
                _, pidx = _peer(m)
                _gather_copy(g_in[i], i, m, send, recv, me).wait_send()
                _gather_copy(g_in[i], i, m, send, recv, pidx).wait_recv()

    outs = pl.pallas_call(
        body, name=name,
        out_shape=[pltpu.HBM(g.shape, g.dtype) for g in gs],
        in_specs=[HBM_SPEC] * n + [SEM_SPEC, SEM_SPEC, ANY_SPEC],
        out_specs=[HBM_SPEC] * n,
        input_output_aliases={i: i for i in range(n)},
        compiler_params=pltpu.CompilerParams(has_side_effects=SIDE_EFFECT),
    )(*gs, send_sems, recv_sems, after)
    return list(outs)


def _scatter_copy(g_ref, l_ref, i, m, send_sems, recv_sems):
    peer, pidx = _peer(m)
    return pltpu.make_async_remote_copy(
        src_ref=g_ref.at[pidx], dst_ref=l_ref.at[m - 1],
        send_sem=send_sems.at[_sem(i, m)], recv_sem=recv_sems.at[_sem(i, m)],
        device_id=peer, device_id_type=MESH)


def scatter_start(grads, after, name):
    n = len(grads)
    lands = [lax.empty((N_DEV - 1,) + g.shape[1:], g.dtype) for g in grads]

    def body(*refs):
        g_in, l_in = refs[:n], refs[n:2 * n]
        send_sems, recv_sems = refs[2 * n + 1], refs[2 * n + 2]
        token = refs[-1]
        for i in range(n):
            for m in range(1, N_DEV):
                _scatter_copy(g_in[i], l_in[i], i, m, send_sems, recv_sems).start()
        token[...] = jnp.zeros_like(token)

    outs = pl.pallas_call(
        body, name=name,
        out_shape=(*_sem_pairs(n), *[pltpu.HBM(g.shape, g.dtype) for g in grads],
                   *[pltpu.HBM(l.shape, l.dtype) for l in lands], jax.ShapeDtypeStruct((8, 128), F32)),
        in_specs=[HBM_SPEC] * (2 * n) + [ANY_SPEC],
        out_specs=(SEM_SPEC, SEM_SPEC, *[HBM_SPEC] * (2 * n), VMEM_SPEC),
        input_output_aliases={i: 2 + i for i in range(2 * n)},
        compiler_params=pltpu.CompilerParams(has_side_effects=SIDE_EFFECT),
    )(*[_hbm(g) for g in grads], *[_hbm(l) for l in lands], after)
    return outs[0], outs[1], list(outs[2:2 + n]), list(outs[2 + n:2 + 2 * n]), outs[-1]


def scatter_wait(grads, lands, send_sems, recv_sems, after, name):
    n = len(grads)

    def body(*refs):
        g_in, l_in = refs[:n], refs[n:2 * n]
        send, recv = refs[2 * n], refs[2 * n + 1]
        for i in range(n):
            for m in range(1, N_DEV):
                cp = _scatter_copy(g_in[i], l_in[i], i, m, send, recv)
                cp.wait_send()
                cp.wait_recv()

    outs = pl.pallas_call(
        body, name=name,
        out_shape=[pltpu.HBM(a.shape, a.dtype) for a in list(grads) + list(lands)],
        in_specs=[HBM_SPEC] * (2 * n) + [SEM_SPEC, SEM_SPEC, ANY_SPEC],
        out_specs=[HBM_SPEC] * (2 * n),
        input_output_aliases={i: i for i in range(2 * n)},
        compiler_params=pltpu.CompilerParams(has_side_effects=SIDE_EFFECT),
    )(*grads, *lands, send_sems, recv_sems, after)
    return list(outs[:n]), list(outs[n:])


def all_reduce_small(packed, name):
    r, c = packed.shape

    def body(in_ref, out_ref, gath, send_sems, recv_sems):
        me = _my_index()
        gath[me] = in_ref[...]
        sends = []
        for m in range(1, N_DEV):
            peer, _ = _peer(m)
            cp = pltpu.make_async_remote_copy(
                src_ref=in_ref, dst_ref=gath.at[me],
                send_sem=send_sems.at[m - 1], recv_sem=recv_sems.at[m - 1],
                device_id=peer, device_id_type=MESH)
            cp.start()
            sends.append(cp)
        for m in range(1, N_DEV):
            peer, pidx = _peer(m)
            pltpu.make_async_remote_copy(
                src_ref=in_ref, dst_ref=gath.at[pidx],
                send_sem=send_sems.at[m - 1], recv_sem=recv_sems.at[m - 1],
                device_id=peer, device_id_type=MESH).wait_recv()
        for cp in sends:
            cp.wait_send()
        acc = gath[0]
        for p in range(1, N_DEV):
            acc = acc + gath[p]
        out_ref[...] = acc

    return pl.pallas_call(
        body, name=name,
        out_shape=jax.ShapeDtypeStruct((r, c), F32),
        in_specs=[VMEM_SPEC], out_specs=VMEM_SPEC,
        scratch_shapes=[pltpu.VMEM((N_DEV, r, c), F32),
                        pltpu.SemaphoreType.DMA((N_DEV - 1,)),
                        pltpu.SemaphoreType.DMA((N_DEV - 1,))],
        compiler_params=pltpu.CompilerParams(vmem_limit_bytes=VMEM_LIMIT),
    )(packed)


def _sigmoid(v):
    return 1.0 / (1.0 + jnp.exp(-v))


def _rms_fwd(xf, g):
    r = lax.rsqrt(jnp.mean(xf * xf, axis=-1, keepdims=True) + EPS)
    return xf * r, r


def _rms_bwd(xhat, r, g, dy):
    dg = jnp.sum(dy * xhat, axis=0, keepdims=True)
    dxh = dy * g
    dx = r * (dxh - xhat * jnp.mean(dxh * xhat, axis=-1, keepdims=True))
    return dx, dg


def _ln_stats(v):
    mu = jnp.mean(v, axis=-1, keepdims=True)
    vc = v - mu
    r = lax.rsqrt(jnp.mean(vc * vc, axis=-1, keepdims=True) + EPS)
    return vc * r, r


def _ln_bwd(xhat, r, dxh):
    return r * (dxh - jnp.mean(dxh, axis=-1, keepdims=True)
                - xhat * jnp.mean(dxh * xhat, axis=-1, keepdims=True))


def _dot(a, b):
    return jnp.dot(a, b, preferred_element_type=F32)


def _dot_nt(a, b):
    return lax.dot_general(a, b, (((1,), (1,)), ((), ())), preferred_element_type=F32)


def _dot_tn(a, b):
    return lax.dot_general(a, b, (((0,), (0,)), ((), ())), preferred_element_type=F32)


def _full_weight(w_ref, kind):
    assert kind == "row"
    p, a, b = w_ref.shape
    return w_ref[...].reshape(p * a, b)


def _wspec(wg):
    return pl.BlockSpec(wg.shape, lambda *_: (0, 0, 0))


def mm_rows(a, wg, kind, *, gain=None, residual=None, out_dtype=F32, name, tm=None):
    m, k = a.shape
    p, wa, wb = wg.shape
    n = p * wb if kind == "col" else wb
    tm = _row_tile(m, tm)
    has_gain, has_res = gain is not None, residual is not None

    def body(*refs):
        refs = list(refs)
        a_ref = refs.pop(0)
        g_ref = refs.pop(0) if has_gain else None
        w_ref = refs.pop(0)
        r_ref = refs.pop(0) if has_res else None
        o_ref = refs.pop(0)
        if has_gain:
            xhat, _ = _rms_fwd(a_ref[...].astype(F32), None)
            h = (xhat * g_ref[...]).astype(BF16)
        else:
            h = a_ref[...].astype(BF16)
        if kind == "col":
            for j in range(p):
                o = _dot(h, w_ref[j])
                if has_res:
                    o = o + r_ref[:, j * wb:(j + 1) * wb]
                o_ref[:, j * wb:(j + 1) * wb] = o.astype(out_dtype)
        else:
            o = _dot(h, _full_weight(w_ref, "row"))
            if has_res:
                o = o + r_ref[...]
            o_ref[...] = o.astype(out_dtype)

    operands = [a]
    in_specs = [pl.BlockSpec((tm, k), lambda i: (i, 0))]
    if has_gain:
        operands.append(gain.reshape(1, k))
        in_specs.append(pl.BlockSpec((1, k), lambda i: (0, 0)))
    operands.append(wg)
    in_specs.append(_wspec(wg))
    if has_res:
        operands.append(residual)
        in_specs.append(pl.BlockSpec((tm, n), lambda i: (i, 0)))
    return pl.pallas_call(
        body, name=name, grid=(m // tm,),
        out_shape=jax.ShapeDtypeStruct((m, n), out_dtype),
        in_specs=in_specs, out_specs=pl.BlockSpec((tm, n), lambda i: (i, 0)),
        compiler_params=_params("parallel"),
    )(*operands)


def mm_nt(dz, wg, kind, *, x=None, gain=None, dx_in=None, name, tm=None):
    m, n = dz.shape
    p, wa, wb = wg.shape
    k = wa if kind == "col" else p * wa
    tm = _row_tile(m, tm)
    epi = x is not None
    has_dx = dx_in is not None

    def body(*refs):
        refs = list(refs)
        dz_ref, w_ref = refs.pop(0), refs.pop(0)
        if epi:
            x_ref, g_ref = refs.pop(0), refs.pop(0)
            dxi_ref = refs.pop(0) if has_dx else None
            dx_ref, h_ref, dg_ref = refs
        else:
            (da_ref,) = refs
        dzb = dz_ref[...].astype(BF16)
        if kind == "col":
            da = _dot_nt(dzb[:, 0:wb], w_ref[0])
            for j in range(1, p):
                da = da + _dot_nt(dzb[:, j * wb:(j + 1) * wb], w_ref[j])
        else:
            da = _dot_nt(dzb, _full_weight(w_ref, "row"))
        if not epi:
            da_ref[...] = da
            return
        g = g_ref[...]
        xhat, r = _rms_fwd(x_ref[...].astype(F32), None)
        h_ref[...] = (xhat * g).astype(BF16)
        dx, dg = _rms_bwd(xhat, r, g, da)
        if has_dx:
            dx = dx + dxi_ref[...]
        dx_ref[...] = dx

        @pl.when(pl.program_id(0) == 0)
        def _():
            dg_ref[...] = jnp.zeros_like(dg_ref)
        dg_ref[...] += dg

    row = lambda i: (i, 0)
    operands = [dz, wg]
    in_specs = [pl.BlockSpec((tm, n), row), _wspec(wg)]
    if epi:
        operands += [x, gain.reshape(1, k)]
        in_specs += [pl.BlockSpec((tm, k), row), pl.BlockSpec((1, k), lambda i: (0, 0))]
        if has_dx:
            operands.append(dx_in)
            in_specs.append(pl.BlockSpec((tm, k), row))
        out_shape = [jax.ShapeDtypeStruct((m, k), F32), jax.ShapeDtypeStruct((m, k), BF16),
                     jax.ShapeDtypeStruct((1, k), F32)]
        out_specs = [pl.BlockSpec((tm, k), row), pl.BlockSpec((tm, k), row),
                     pl.BlockSpec((1, k), lambda i: (0, 0))]
    else:
        out_shape = jax.ShapeDtypeStruct((m, k), F32)
        out_specs = pl.BlockSpec((tm, k), row)
    return pl.pallas_call(
        body, name=name, grid=(m // tm,), out_shape=out_shape,
        in_specs=in_specs, out_specs=out_specs,
        compiler_params=_params("arbitrary"),
    )(*operands)


def mm_tn(a, b, *, nb, a_spec, b_spec, ka, nbk, tm, m, scale=1.0, out_dtype=BF16, name):
    ni = m // tm

    def body(a_ref, b_ref, o_ref, acc):
        i = pl.program_id(1)

        @pl.when(i == 0)
        def _():
            acc[...] = jnp.zeros_like(acc)
        acc[...] += _dot_tn(a_ref[...].astype(BF16), b_ref[...].astype(BF16))

        @pl.when(i == ni - 1)
        def _():
            o_ref[...] = (acc[...] * scale).astype(out_dtype)

    return pl.pallas_call(
        body, name=name, grid=(nb, ni),
        out_shape=jax.ShapeDtypeStruct((nb, ka, nbk), out_dtype),
        in_specs=[a_spec, b_spec],
        out_specs=pl.BlockSpec((None, ka, nbk), lambda s, i: (s, 0, 0)),
        scratch_shapes=[pltpu.VMEM((ka, nbk), F32)],
        compiler_params=_params("parallel", "arbitrary"),
    )(a, b)


def _ffn_specs(w_in_g, w_out_g, d):
    nf = w_in_g.shape[2]
    hr = w_out_g.shape[1]
    assert 2 * hr == nf
    w_in5 = w_in_g.reshape(2, 4, d, nf)
    w_out5 = w_out_g.reshape(4, 2, hr, d)
    in_spec = pl.BlockSpec((2, None, d, nf), lambda i, j: (0, j, 0, 0))
    out_spec = pl.BlockSpec((None, 2, hr, d), lambda i, j: (j, 0, 0, 0))
    return w_in5, w_out5, in_spec, out_spec, nf


def ffn_fwd(x, gain, w_in_g, w_out_g, *, name, tm=None):
    t, d = x.shape
    tm = _row_tile(t, tm)
    w_in5, w_out5, wi_spec, wo_spec, nf = _ffn_specs(w_in_g, w_out_g, d)

    def body(x_ref, g_ref, wi_ref, wo_ref, o_ref, h_scr, acc):
        j = pl.program_id(1)

        @pl.when(j == 0)
        def _():
            xhat, _ = _rms_fwd(x_ref[...], None)
            h_scr[...] = (xhat * g_ref[...]).astype(BF16)
            acc[...] = jnp.zeros_like(acc)
        h = h_scr[...]
        gt = _dot(h, wi_ref[0])
        up = _dot(h, wi_ref[1])
        act = (gt * _sigmoid(gt) * up).astype(BF16)
        acc[...] += _dot(act, wo_ref[...].reshape(nf, d))

        @pl.when(j == 3)
        def _():
            o_ref[...] = x_ref[...] + 0.5 * acc[...]

    return pl.pallas_call(
        body, name=name, grid=(t // tm, 4),
        out_shape=jax.ShapeDtypeStruct((t, d), F32),
        in_specs=[pl.BlockSpec((tm, d), lambda i, j: (i, 0)),
                  pl.BlockSpec((1, d), lambda i, j: (0, 0)), wi_spec, wo_spec],
        out_specs=pl.BlockSpec((tm, d), lambda i, j: (i, 0)),
        scratch_shapes=[pltpu.VMEM((tm, d), BF16), pltpu.VMEM((tm, d), F32)],
        compiler_params=_params("parallel", "arbitrary"),
    )(x, gain.reshape(1, d), w_in5, w_out5)


def ffn_bwd_rows(x, dy, gain, w_in_g, w_out_g, *, name, tm=None):
    t, d = x.shape
    tm = _row_tile(t, tm)
    w_in5, w_out5, wi_spec, wo_spec, nf = _ffn_specs(w_in_g, w_out_g, d)

    def body(x_ref, dy_ref, g_ref, wi_ref, wo_ref, dx_ref, h_ref, act_ref, dgu_ref, dg_ref,
             dh_acc, dyh_scr):
        i, j = pl.program_id(0), pl.program_id(1)

        @pl.when(j == 0)
        def _():
            xhat, _ = _rms_fwd(x_ref[...], None)
            h_ref[...] = (xhat * g_ref[...]).astype(BF16)
            dyh_scr[...] = (0.5 * dy_ref[...]).astype(BF16)
            dh_acc[...] = jnp.zeros_like(dh_acc)
        h = h_ref[...]
        gt = _dot(h, wi_ref[0])
        up = _dot(h, wi_ref[1])
        sg = _sigmoid(gt)
        silu = gt * sg
        act_ref[...] = (silu * up).astype(BF16)
        dact = _dot_nt(dyh_scr[...], wo_ref[...].reshape(nf, d))
        dgt = (dact * up * (sg * (1.0 + gt * (1.0 - sg)))).astype(BF16)
        dup = (dact * silu).astype(BF16)
        dgu_ref[0] = dgt
        dgu_ref[1] = dup
        dh_acc[...] += _dot_nt(dgt, wi_ref[0]) + _dot_nt(dup, wi_ref[1])

        @pl.when(j == 3)
        def _():
            g = g_ref[...]
            xhat, r = _rms_fwd(x_ref[...], None)
            dx, dg = _rms_bwd(xhat, r, g, dh_acc[...])
            dx_ref[...] = dy_ref[...] + dx

            @pl.when(i == 0)
            def _():
                dg_ref[...] = jnp.zeros_like(dg_ref)
            dg_ref[...] += dg

    row = lambda i, j: (i, 0)
    return pl.pallas_call(
        body, name=name, grid=(t // tm, 4),
        out_shape=[jax.ShapeDtypeStruct((t, d), F32), jax.ShapeDtypeStruct((t, d), BF16),
                   jax.ShapeDtypeStruct((4, t, nf), BF16), jax.ShapeDtypeStruct((2, 4, t, nf), BF16),
                   jax.ShapeDtypeStruct((1, d), F32)],
        in_specs=[pl.BlockSpec((tm, d), row), pl.BlockSpec((tm, d), row),
                  pl.BlockSpec((1, d), lambda i, j: (0, 0)), wi_spec, wo_spec],
        out_specs=[pl.BlockSpec((tm, d), row), pl.BlockSpec((tm, d), row),
                   pl.BlockSpec((None, tm, nf), lambda i, j: (j, i, 0)),
                   pl.BlockSpec((2, None, tm, nf), lambda i, j: (0, j, i, 0)),
                   pl.BlockSpec((1, d), lambda i, j: (0, 0))],
        scratch_shapes=[pltpu.VMEM((tm, d), F32), pltpu.VMEM((tm, d), BF16)],
        compiler_params=_params("arbitrary", "arbitrary"),
    )(x, dy, gain.reshape(1, d), w_in5, w_out5)


def ffn_bwd(x, dy, gain, w_in_g, w_out_g, *, name):
    t, d = x.shape
    dx, h, act, dgu, dgain = ffn_bwd_rows(x, dy, gain, w_in_g, w_out_g, name=name + "_rows")
    nf = act.shape[-1]
    tm = _row_tile(t)
    d_w_in = mm_tn(h, dgu.reshape(8, t, nf), nb=8, ka=d, nbk=nf, tm=tm, m=t,
                   a_spec=pl.BlockSpec((tm, d), lambda s, i: (i, 0)),
                   b_spec=pl.BlockSpec((None, tm, nf), lambda s, i: (s, i, 0)),
                   name=name + "_dwin")
    d_w_out = mm_tn(act, dy, nb=4, ka=nf, nbk=d, tm=tm, m=t, scale=0.5,
                    a_spec=pl.BlockSpec((None, tm, nf), lambda s, i: (s, i, 0)),
                    b_spec=pl.BlockSpec((tm, d), lambda s, i: (i, 0)),
                    name=name + "_dwout")
    return dx, dgain, d_w_in, d_w_out.reshape(8, nf // 2, d)


def _lane_group(shape):
    return lax.shift_right_logical(lax.broadcasted_iota(jnp.int32, shape, 1), 6)


def _pool_count(t0, rows):
    t = (t0 + lax.broadcasted_iota(jnp.int32, (rows, MIX_W), 0) + 1).astype(F32)
    return jnp.minimum(t, _by_group(_lane_group((rows, MIX_W)), 2.0, 4.0, 8.0, 16.0))


def _by_group(grp, v0, v1, v2, v3):
    return jnp.where(grp == 0, v0, jnp.where(grp == 1, v1, jnp.where(grp == 2, v2, v3)))


def _sgu_mix(wt_ref, vnc):
    grp = _lane_group((SGU_CHUNK, MIX_W))
    out = jnp.zeros((SGU_CHUNK, MIX_W), F32)
    for hd in range(N_HEADS):
        out = jnp.where(grp == hd, _dot(wt_ref[hd], vnc), out)
    return out


def _pool_fwd(s1, s2, s3, t0, ts, lo):
    h = lo
    s2[h - 24:h + ts] = s1[h - 24:h + ts] + s1[h - 25:h + ts - 1]
    s3[h - 16:h + ts] = s2[h - 16:h + ts] + s2[h - 18:h + ts - 2]
    sum2 = s2[h:h + ts]
    sum4 = s3[h:h + ts]
    s2[h - 8:h + ts] = s3[h - 8:h + ts] + s3[h - 12:h + ts - 4]
    sum8 = s2[h:h + ts]
    sum16 = sum8 + s2[h - 8:h + ts - 8]
    grp = _lane_group((ts, MIX_W))
    return _by_group(grp, sum2, sum4, sum8, sum16) / _pool_count(t0, ts) - s1[h:h + ts]


def mixer_fwd(z, sconv, cconv, vecs, wt, bexp, pbd, *, name, ts=None):
    t = z.shape[0]
    ts = _row_tile(t, MIX_TILE if ts is None else ts)
    hl = HALO
    w = MIX_W
    nch = ts // SGU_CHUNK

    def body(zc, zp, sconv_ref, cconv_ref, vec_ref, wt_ref, bexp_ref, pbd_ref, y_ref, s1, s2, s3):
        i = pl.program_id(0)
        has_prev = i > 0

        def col(ref, c):
            return ref[:, c * w:(c + 1) * w]

        def prev(c):
            return jnp.where(has_prev, col(zp, c), 0.0)

        s1[0:hl] = prev(1) * prev(2)
        s1[hl:hl + ts] = col(zc, 1) * col(zc, 2)
        cv = sconv_ref[0:1] * s1[hl - 2:hl - 2 + ts]
        for k in range(1, SCONV_K):
            cv = cv + sconv_ref[k:k + 1] * s1[hl - 2 + k:hl - 2 + k + ts]
        y_ref[:, 0:w] = (col(zc, 0) * cv).astype(BF16)

        xhat, _ = _ln_stats(col(zc, 4))
        vn = (xhat * vec_ref[0:1]).astype(BF16)
        for c in range(nch):
            rows = slice(c * SGU_CHUNK, (c + 1) * SGU_CHUNK)
            mixed = _sgu_mix(wt_ref, vn[rows]) + bexp_ref[...]
            y_ref[rows, w:2 * w] = (zc[rows, 3 * w:4 * w] * mixed).astype(BF16)

        s1[0:hl] = prev(5) * _sigmoid(prev(6))
        s1[hl:hl + ts] = col(zc, 5) * _sigmoid(col(zc, 6))
        off = hl - (CCONV_K - 1)
        cv = cconv_ref[0:1] * s1[off:off + ts]
        for k in range(1, CCONV_K):
            cv = cv + cconv_ref[k:k + 1] * s1[off + k:off + k + ts]
        xhat, _ = _ln_stats(cv)
        ln = xhat * vec_ref[1:2] + vec_ref[2:3]
        y_ref[:, 2 * w:3 * w] = (ln * _sigmoid(ln)).astype(BF16)

        s1[0:hl] = prev(7)
        s1[hl:hl + ts] = col(zc, 7)
        pooled = _pool_fwd(s1, s2, s3, i * ts, ts, hl)
        y_ref[:, 3 * w:4 * w] = (_dot(pooled.astype(BF16), pbd_ref[...]) * vec_ref[3:4]).astype(BF16)

    full = lambda shape: pl.BlockSpec(shape, lambda i: (0,) * len(shape))
    return pl.pallas_call(
        body, name=name, grid=(t // ts,),
        out_shape=jax.ShapeDtypeStruct((t, 4 * w), BF16),
        in_specs=[pl.BlockSpec((ts, 8 * w), lambda i: (i, 0)),
                  pl.BlockSpec((hl, 8 * w), lambda i: (jnp.maximum(i * (ts // hl) - 1, 0), 0)),
                  full((8, w)), full((32, w)), full((8, w)), full((N_HEADS, SGU_CHUNK, SGU_CHUNK)),
                  full((SGU_CHUNK, w)), full((w, w))],
        out_specs=pl.BlockSpec((ts, 4 * w), lambda i: (i, 0)),
        scratch_shapes=[pltpu.VMEM((hl + ts, w), F32)] * 3,
        compiler_params=_params("parallel"),
    )(z, z, sconv, cconv, vecs, wt, bexp, pbd)


def mixer_bwd(z, dy, sconv, cconv, vecs, wt, bexp, pbd, *, name, ts=None):
    t = z.shape[0]
    ts = _row_tile(t, MIX_TILE if ts is None else ts)
    hl = HALO
    w = MIX_W
    nch = ts // SGU_CHUNK
    ni = t // ts
    ext = ts + hl

    def body(zc, zp, zn, dyc, dyn, sconv_ref, cconv_ref, vec_ref, wt_ref, bexp_ref, pbd_ref,
             dz_ref, gvec_ref, gcc_ref, gwt_ref, gb_ref, gpbd_ref, s1, s2, s3):
        i = pl.program_id(0)
        has_prev = i > 0
        has_next = i < ni - 1

        @pl.when(i == 0)
        def _():
            gvec_ref[...] = jnp.zeros_like(gvec_ref)
            gcc_ref[...] = jnp.zeros_like(gcc_ref)
            gwt_ref[...] = jnp.zeros_like(gwt_ref)
            gb_ref[...] = jnp.zeros_like(gb_ref)
            gpbd_ref[...] = jnp.zeros_like(gpbd_ref)

        def col(ref, c):
            return ref[:, c * w:(c + 1) * w]

        def prev(c):
            return jnp.where(has_prev, col(zp, c), 0.0)

        def nxt(c):
            return jnp.where(has_next, col(zn, c), 0.0)

        def dnext(c):
            return jnp.where(has_next, col(dyn, c), 0.0)

        def rowsum(v):
            return jnp.sum(v, axis=0, keepdims=True)

        s1[0:hl] = prev(1) * prev(2)
        s1[hl:hl + ts] = col(zc, 1) * col(zc, 2)
        s1[hl + ts:hl + ts + hl] = nxt(1) * nxt(2)
        cv = sconv_ref[0:1] * s1[hl - 2:hl - 2 + ts]
        for k in range(1, SCONV_K):
            cv = cv + sconv_ref[k:k + 1] * s1[hl - 2 + k:hl - 2 + k + ts]
        dya = col(dyc, 0)
        dz_ref[:, 0:w] = (dya * cv).astype(BF16)
        s2[0:ts] = dya * col(zc, 0)
        s2[ts:ext] = dnext(0) * nxt(0)
        dv = sconv_ref[0:1] * s2[2:2 + ts]
        for k in range(1, SCONV_K):
            dv = dv + sconv_ref[k:k + 1] * s2[2 - k:2 - k + ts]
        dz_ref[:, w:2 * w] = (dv * col(zc, 2)).astype(BF16)
        dz_ref[:, 2 * w:3 * w] = (dv * col(zc, 1)).astype(BF16)
        dcv = s2[0:ts]
        for k in range(SCONV_K):
            gvec_ref[k:k + 1] += rowsum(dcv * s1[hl - 2 + k:hl - 2 + k + ts])

        g_sgu = vec_ref[0:1]
        xhat, rstd = _ln_stats(col(zc, 4))
        vn = (xhat * g_sgu).astype(BF16)
        grp = _lane_group((SGU_CHUNK, w))
        lane = lax.broadcasted_iota(jnp.int32, (SGU_CHUNK, SGU_CHUNK), 1)
        tril = lax.broadcasted_iota(jnp.int32, (SGU_CHUNK, SGU_CHUNK), 0) >= lane
        for c in range(nch):
            rows = slice(c * SGU_CHUNK, (c + 1) * SGU_CHUNK)
            vnc = vn[rows]
            mixed = _sgu_mix(wt_ref, vnc) + bexp_ref[...]
            dyb = dyc[rows, w:2 * w]
            dz_ref[rows, 3 * w:4 * w] = (dyb * mixed).astype(BF16)
            dmix = dyb * zc[rows, 3 * w:4 * w]
            dmixb = dmix.astype(BF16)
            dvn = jnp.zeros((SGU_CHUNK, w), F32)
            gb = jnp.zeros((SGU_CHUNK, SGU_CHUNK), F32)
            for hd in range(N_HEADS):
                dvn = jnp.where(grp == hd, _dot_tn(wt_ref[hd], dmixb), dvn)
                dm_h = jnp.where(grp == hd, dmix, 0.0)
                gwt_ref[hd] += jnp.where(tril, _dot_nt(dm_h.astype(BF16), vnc), 0.0)
                gb = gb + jnp.where(lane == hd, jnp.sum(dm_h, axis=1, keepdims=True), 0.0)
            gb_ref[...] += gb
            s3[rows] = dvn
        dvn = s3[0:ts]
        gvec_ref[3:4] += rowsum(dvn * xhat)
        dz_ref[:, 4 * w:5 * w] = _ln_bwd(xhat, rstd, dvn * g_sgu).astype(BF16)

        sig_c = _sigmoid(col(zc, 6))
        s1[0:hl] = prev(5) * _sigmoid(prev(6))
        s1[hl:hl + ts] = col(zc, 5) * sig_c
        s1[hl + ts:hl + ts + hl] = nxt(5) * _sigmoid(nxt(6))
        off = hl - (CCONV_K - 1)
        cv = cconv_ref[0:1] * s1[off:off + ext]
        for k in range(1, CCONV_K):
            cv = cv + cconv_ref[k:k + 1] * s1[off + k:off + k + ext]
        xhat, rstd = _ln_stats(cv)
        ln = xhat * vec_ref[1:2] + vec_ref[2:3]
        sg = _sigmoid(ln)
        s2[0:ts] = col(dyc, 2)
        s2[ts:ext] = dnext(2)
        dln = s2[0:ext] * (sg * (1.0 + ln * (1.0 - sg)))
        gvec_ref[4:5] += rowsum(dln[0:ts] * xhat[0:ts])
        gvec_ref[5:6] += rowsum(dln[0:ts])
        s3[0:ext] = _ln_bwd(xhat, rstd, dln * vec_ref[1:2])
        dyg = cconv_ref[0:1] * s3[CCONV_K - 1:CCONV_K - 1 + ts]
        for k in range(1, CCONV_K):
            dyg = dyg + cconv_ref[k:k + 1] * s3[CCONV_K - 1 - k:CCONV_K - 1 - k + ts]
        dz_ref[:, 5 * w:6 * w] = (dyg * sig_c).astype(BF16)
        dz_ref[:, 6 * w:7 * w] = (dyg * col(zc, 5) * sig_c * (1.0 - sig_c)).astype(BF16)
        dcv = s3[0:ts]
        for k in range(CCONV_K):
            gcc_ref[k:k + 1] += rowsum(dcv * s1[off + k:off + k + ts])

        scale = vec_ref[3:4]
        s1[0:hl] = prev(7)
        s1[hl:hl + ts] = col(zc, 7)
        pooled = _pool_fwd(s1, s2, s3, i * ts, ts, hl).astype(BF16)
        q0 = _dot(pooled, pbd_ref[...])
        dyd = col(dyc, 3)
        gvec_ref[6:7] += rowsum(dyd * q0)
        dq = (dyd * scale).astype(BF16)
        gpbd_ref[...] += _dot_tn(pooled, dq)
        s1[0:ts] = _dot_nt(dq, pbd_ref[...])
        s1[ts:ext] = _dot_nt((dnext(3) * scale).astype(BF16), pbd_ref[...])
        dpool = s1[0:ts]
        s2[0:ext] = s1[0:ext] / _pool_count(i * ts, ext)
        s3[0:ts + 24] = s2[0:ts + 24] + s2[1:ts + 25]
        f2 = s3[0:ts]
        s2[0:ts + 16] = s3[0:ts + 16] + s3[2:ts + 18]
        f4 = s2[0:ts]
        s3[0:ts + 8] = s2[0:ts + 8] + s2[4:ts + 12]
        f8 = s3[0:ts]
        f16 = f8 + s3[8:ts + 8]
        dz_ref[:, 7 * w:8 * w] = (_by_group(_lane_group((ts, w)), f2, f4, f8, f16) - dpool).astype(BF16)

    full = lambda shape: pl.BlockSpec(shape, lambda i: (0,) * len(shape))
    r = ts // hl
    prev_map = lambda i: (jnp.maximum(i * r - 1, 0), 0)
    next_map = lambda i: (jnp.minimum((i + 1) * r, t // hl - 1), 0)
    return pl.pallas_call(
        body, name=name, grid=(ni,),
        out_shape=[jax.ShapeDtypeStruct((t, 8 * w), BF16), jax.ShapeDtypeStruct((8, w), F32),
                   jax.ShapeDtypeStruct((32, w), F32),
                   jax.ShapeDtypeStruct((N_HEADS, SGU_CHUNK, SGU_CHUNK), F32),
                   jax.ShapeDtypeStruct((SGU_CHUNK, SGU_CHUNK), F32), jax.ShapeDtypeStruct((w, w), F32)],
        in_specs=[pl.BlockSpec((ts, 8 * w), lambda i: (i, 0)),
                  pl.BlockSpec((hl, 8 * w), prev_map), pl.BlockSpec((hl, 8 * w), next_map),
                  pl.BlockSpec((ts, 4 * w), lambda i: (i, 0)), pl.BlockSpec((hl, 4 * w), next_map),
                  full((8, w)), full((32, w)), full((8, w)), full((N_HEADS, SGU_CHUNK, SGU_CHUNK)),
                  full((SGU_CHUNK, w)), full((w, w))],
        out_specs=[pl.BlockSpec((ts, 8 * w), lambda i: (i, 0)), full((8, w)), full((32, w)),
                   full((N_HEADS, SGU_CHUNK, SGU_CHUNK)), full((SGU_CHUNK, SGU_CHUNK)), full((w, w))],
        scratch_shapes=[pltpu.VMEM((ts + 2 * hl, w), F32)] * 3,
        compiler_params=_params("arbitrary"),
    )(z, z, z, dy, dy, sconv, cconv, vecs, wt, bexp, pbd)


def _attn_head(q, kv_ref, hd, d):
    hw = d // N_HEADS
    qh = q[:, hd * hw:(hd + 1) * hw]
    kh = kv_ref[:, hd * hw:(hd + 1) * hw].astype(BF16)
    vh = kv_ref[:, d + hd * hw:d + (hd + 1) * hw].astype(BF16)
    s = _dot_nt(qh, kh) * (1.0 / (hw ** 0.5))
    e = jnp.exp(s - jnp.max(s, axis=-1, keepdims=True))
    p = e / jnp.sum(e, axis=-1, keepdims=True)
    return qh, kh, vh, p


def xattn_fwd(x, gain, kv, wq_g, wo_g, *, name, tm=None):
    t, d = x.shape
    nm = kv.shape[0]
    tm = _row_tile(t, tm)
    hw = d // N_HEADS

    def body(x_ref, g_ref, kv_ref, wq_ref, wo_ref, o_ref):
        xv = x_ref[...]
        xhat, _ = _rms_fwd(xv, None)
        h = (xhat * g_ref[...]).astype(BF16)
        q = _dot(h, _full_weight(wq_ref, "row")).astype(BF16)
        wo = _full_weight(wo_ref, "row")
        out = xv
        for hd in range(N_HEADS):
            _, _, vh, p = _attn_head(q, kv_ref, hd, d)
            oh = _dot(p.astype(BF16), vh).astype(BF16)
            out = out + _dot(oh, wo[hd * hw:(hd + 1) * hw])
        o_ref[...] = out

    row = lambda i: (i, 0)
    return pl.pallas_call(
        body, name=name, grid=(t // tm,),
        out_shape=jax.ShapeDtypeStruct((t, d), F32),
        in_specs=[pl.BlockSpec((tm, d), row), pl.BlockSpec((1, d), lambda i: (0, 0)),
                  pl.BlockSpec((nm, 2 * d), lambda i: (0, 0)), _wspec(wq_g), _wspec(wo_g)],
        out_specs=pl.BlockSpec((tm, d), row),
        compiler_params=_params("parallel"),
    )(x, gain.reshape(1, d), kv, wq_g, wo_g)


def xattn_bwd_rows(x, dxn, gain, kv, wq_g, wo_g, *, name, tm=None):
    t, d = x.shape
    nm = kv.shape[0]
    tm = _row_tile(t, tm)
    hw = d // N_HEADS

    def body(x_ref, dxn_ref, g_ref, kv_ref, wq_ref, wo_ref,
             dx_ref, h_ref, dq_ref, o_ref, dkv_ref, dg_ref):
        i = pl.program_id(0)

        @pl.when(i == 0)
        def _():
            dkv_ref[...] = jnp.zeros_like(dkv_ref)
            dg_ref[...] = jnp.zeros_like(dg_ref)
        g = g_ref[...]
        xhat, r = _rms_fwd(x_ref[...], None)
        h = (xhat * g).astype(BF16)
        h_ref[...] = h
        wq = _full_weight(wq_ref, "row")
        q = _dot(h, wq).astype(BF16)
        dxn = dxn_ref[...]
        do = _dot_nt(dxn.astype(BF16), _full_weight(wo_ref, "row")).astype(BF16)
        for hd in range(N_HEADS):
            cols = slice(hd * hw, (hd + 1) * hw)
            qh, kh, vh, p = _attn_head(q, kv_ref, hd, d)
            pb = p.astype(BF16)
            o_ref[:, cols] = _dot(pb, vh).astype(BF16)
            doh = do[:, cols]
            dkv_ref[:, d + hd * hw:d + (hd + 1) * hw] += _dot_tn(pb, doh)
            dp = _dot_nt(doh, vh)
            ds = (p * (dp - jnp.sum(dp * p, axis=-1, keepdims=True)) * (1.0 / (hw ** 0.5))).astype(BF16)
            dq_ref[:, cols] = _dot(ds, kh).astype(BF16)
            dkv_ref[:, cols] += _dot_tn(ds, qh)
        dh = _dot_nt(dq_ref[...], wq)
        dx, dg = _rms_bwd(xhat, r, g, dh)
        dx_ref[...] = dxn + dx
        dg_ref[...] += dg

    row = lambda i: (i, 0)
    fix = lambda i: (0, 0)
    return pl.pallas_call(
        body, name=name, grid=(t // tm,),
        out_shape=[jax.ShapeDtypeStruct((t, d), F32), jax.ShapeDtypeStruct((t, d), BF16),
                   jax.ShapeDtypeStruct((t, d), BF16), jax.ShapeDtypeStruct((t, d), BF16),
                   jax.ShapeDtypeStruct((nm, 2 * d), F32), jax.ShapeDtypeStruct((1, d), F32)],
        in_specs=[pl.BlockSpec((tm, d), row), pl.BlockSpec((tm, d), row), pl.BlockSpec((1, d), fix),
                  pl.BlockSpec((nm, 2 * d), fix), _wspec(wq_g), _wspec(wo_g)],
        out_specs=[pl.BlockSpec((tm, d), row)] * 4 + [pl.BlockSpec((nm, 2 * d), fix),
                                                      pl.BlockSpec((1, d), fix)],
        compiler_params=_params("arbitrary"),
    )(x, dxn, gain.reshape(1, d), kv, wq_g, wo_g)


def loss_head(x, target, gain, *, name, tm=None):
    t, d = x.shape
    tm = _row_tile(t, tm)

    def body(x_ref, t_ref, g_ref, dx_ref, dg_ref, loss_ref):
        @pl.when(pl.program_id(0) == 0)
        def _():
            dg_ref[...] = jnp.zeros_like(dg_ref)
            loss_ref[...] = jnp.zeros_like(loss_ref)
        g = g_ref[...]
        xhat, r = _rms_fwd(x_ref[...], None)
        err = xhat * g - t_ref[...]
        loss_ref[...] += 0.5 * jnp.sum(jnp.sum(err * err, axis=-1, keepdims=True) / d,
                                       axis=0, keepdims=True)
        dx, dg = _rms_bwd(xhat, r, g, err / d)
        dx_ref[...] = dx
        dg_ref[...] += dg

    row = lambda i: (i, 0)
    fix = lambda i: (0, 0)
    return pl.pallas_call(
        body, name=name, grid=(t // tm,),
        out_shape=[jax.ShapeDtypeStruct((t, d), F32), jax.ShapeDtypeStruct((1, d), F32),
                   jax.ShapeDtypeStruct((1, 1), F32)],
        in_specs=[pl.BlockSpec((tm, d), row), pl.BlockSpec((tm, d), row), pl.BlockSpec((1, d), fix)],
        out_specs=[pl.BlockSpec((tm, d), row), pl.BlockSpec((1, d), fix), pl.BlockSpec((1, 1), fix)],
        compiler_params=_params("arbitrary"),
    )(x, target, gain.reshape(1, d))


def _adamw_math(w, g, m, v):
    m = ADAM_B1 * m + (1.0 - ADAM_B1) * g
    v = ADAM_B2 * v + (1.0 - ADAM_B2) * (g * g)
    m_hat = m / (1.0 - ADAM_B1 ** ADAM_STEP)
    v_hat = v / (1.0 - ADAM_B2 ** ADAM_STEP)
    delta = -ADAM_LR * (m_hat / (jnp.sqrt(v_hat) + ADAM_EPS) + ADAM_WD * w)
    return delta, m, v


def adamw_sharded(own, lands, w, m, v, me_arr, *, name):
    nl, r, c = w.shape
    assert nl == len(own) == len(lands) == 2
    tr = next(cand for cand in (256, 176, 128, r) if r % cand == 0)
    nr = r // tr

    def body(me_ref, o0, o1, l0, l1, w_ref, m_ref, v_ref, g_out, d_out, m_out, v_out):
        def total(o_ref, l_ref):
            acc = o_ref[...].astype(F32)
            for p in range(N_DEV - 1):
                acc = acc + l_ref[p].astype(F32)
            return acc
        g = jnp.where(pl.program_id(0) == 0, total(o0, l0), total(o1, l1))
        delta, mn, vn = _adamw_math(w_ref[...], g, m_ref[...], v_ref[...])
        g_out[...] = g
        d_out[...] = delta
        m_out[...] = mn
        v_out[...] = vn

    row0 = lambda l, i: jnp.where(l == 0, i, nr - 1)
    row1 = lambda l, i: jnp.where(l == 1, i, 0)
    blk = pl.BlockSpec((None, tr, c), lambda l, i, me: (l, i, 0))
    grid_spec = pltpu.PrefetchScalarGridSpec(
        num_scalar_prefetch=1, grid=(nl, nr),
        in_specs=[pl.BlockSpec((None, tr, c), lambda l, i, me: (me[0], row0(l, i), 0)),
                  pl.BlockSpec((None, tr, c), lambda l, i, me: (me[0], row1(l, i), 0)),
                  pl.BlockSpec((N_DEV - 1, tr, c), lambda l, i, me: (0, row0(l, i), 0)),
                  pl.BlockSpec((N_DEV - 1, tr, c), lambda l, i, me: (0, row1(l, i), 0)),
                  blk, blk, blk],
        out_specs=[blk] * 4)
    return pl.pallas_call(
        body, name=name, grid_spec=grid_spec,
        out_shape=[jax.ShapeDtypeStruct((nl, r, c), F32)] * 4,
        compiler_params=_params("arbitrary", "arbitrary"),
    )(me_arr, own[0], own[1], lands[0], lands[1], w, m, v)


def adamw_flat(g, w, m, v, *, name):
    def body(g_ref, w_ref, m_ref, v_ref, d_out, m_out, v_out):
        delta, mn, vn = _adamw_math(w_ref[...], g_ref[...], m_ref[...], v_ref[...])
        d_out[...] = delta
        m_out[...] = mn
        v_out[...] = vn

    return pl.pallas_call(
        body, name=name, out_shape=[jax.ShapeDtypeStruct(w.shape, F32)] * 3,
        in_specs=[VMEM_SPEC] * 4, out_specs=[VMEM_SPEC] * 3,
        compiler_params=pltpu.CompilerParams(vmem_limit_bytes=VMEM_LIMIT),
    )(g, w, m, v)


def cast_into_slot(a, layer, me_arr, *, name):
    _, r, c = a.shape
    tr = next(cand for cand in (256, 176, 128, r) if r % cand == 0)

    def body(me_ref, a_ref, o_ref):
        o_ref[...] = a_ref[...].astype(BF16)

    grid_spec = pltpu.PrefetchScalarGridSpec(
        num_scalar_prefetch=1, grid=(r // tr,),
        in_specs=[pl.BlockSpec((None, tr, c), lambda i, me: (layer, i, 0))],
        out_specs=pl.BlockSpec((None, tr, c), lambda i, me: (me[0], i, 0)))
    return pl.pallas_call(
        body, name=name, grid_spec=grid_spec,
        out_shape=jax.ShapeDtypeStruct((N_DEV, r, c), BF16),
        compiler_params=_params("parallel"),
    )(me_arr, a)


def _pack(arrs, rows):
    flat = jnp.concatenate([a.reshape(-1).astype(F32) for a in arrs])
    pad = rows * 128 - flat.shape[0]
    assert pad >= 0
    if pad:
        flat = jnp.concatenate([flat, jnp.zeros((pad,), F32)])
    return flat.reshape(rows, 128)


def _unpack(packed, shapes):
    flat = packed.reshape(-1)
    out, pos = [], 0
    for s in shapes:
        n = 1
        for dim in s:
            n *= dim
        out.append(flat[pos:pos + n].reshape(s))
        pos += n
    return out


def _rows_for(shapes):
    n = 0
    for s in shapes:
        k = 1
        for dim in s:
            k *= dim
        n += k
    return -(-n // 1024) * 8


GATHER_GROUPS = (("ffn1", ("ffn1_w_in", "ffn1_w_out")),
                 ("mid", ("mix_w_in", "mix_w_out", "xattn_wkv", "xattn_wq", "xattn_wo")),
                 ("ffn2", ("ffn2_w_in", "ffn2_w_out")))
SMALL_REPL = ["norm_ffn1", "norm_mix", "sgu_norm_g", "sgu_w", "sgu_b", "cconv_ln_g", "cconv_ln_b",
              "pool_w", "pool_scale", "norm_xattn", "norm_mem", "norm_ffn2", "norm_final"]
SMALL_SHARD = ["sconv_w", "cconv_w"]
WEIGHTS = ["norm_ffn1", "ffn1_w_in", "ffn1_w_out", "norm_mix", "mix_w_in", "sconv_w", "sgu_norm_g",
           "sgu_w", "sgu_b", "cconv_w", "cconv_ln_g", "cconv_ln_b", "pool_w", "pool_scale", "mix_w_out",
           "norm_xattn", "norm_mem", "xattn_wq", "xattn_wkv", "xattn_wo", "norm_ffn2", "ffn2_w_in",
           "ffn2_w_out", "norm_final"]


def kernel(x, mem, norm_ffn1, ffn1_w_in, ffn1_w_out, norm_mix, mix_w_in, sconv_w, sgu_norm_g, sgu_w, sgu_b, cconv_w, cconv_ln_g, cconv_ln_b, pool_w, pool_scale, mix_w_out, norm_xattn, norm_mem, xattn_wq, xattn_wkv, xattn_wo, norm_ffn2, ffn2_w_in, ffn2_w_out, norm_final, loss_target, m_norm_ffn1, m_ffn1_w_in, m_ffn1_w_out, m_norm_mix, m_mix_w_in, m_sconv_w, m_sgu_norm_g, m_sgu_w, m_sgu_b, m_cconv_w, m_cconv_ln_g, m_cconv_ln_b, m_pool_w, m_pool_scale, m_mix_w_out, m_norm_xattn, m_norm_mem, m_xattn_wq, m_xattn_wkv, m_xattn_wo, m_norm_ffn2, m_ffn2_w_in, m_ffn2_w_out, m_norm_final, v_norm_ffn1, v_ffn1_w_in, v_ffn1_w_out, v_norm_mix, v_mix_w_in, v_sconv_w, v_sgu_norm_g, v_sgu_w, v_sgu_b, v_cconv_w, v_cconv_ln_g, v_cconv_ln_b, v_pool_w, v_pool_scale, v_mix_w_out, v_norm_xattn, v_norm_mem, v_xattn_wq, v_xattn_wkv, v_xattn_wo, v_norm_ffn2, v_ffn2_w_in, v_ffn2_w_out, v_norm_final):
    args = dict(locals())
    wts = {n: args[n] for n in WEIGHTS}
    mom = {n: args["m_" + n] for n in WEIGHTS}
    var = {n: args["v_" + n] for n in WEIGHTS}
    x0 = x[0]
    mem0 = mem[0]
    target = loss_target[0]
    t, d = x0.shape
    nl = norm_ffn1.shape[0]
    w = MIX_W
    me = _my_index()

    me_arr = jnp.reshape(me, (1,)).astype(jnp.int32)

    small_g = all_gather([sconv_w, cconv_w], name="gather_conv_taps")
    sconv_full = jnp.transpose(small_g[0], (1, 2, 0, 3)).reshape(nl, SCONV_K, w)
    cconv_full = jnp.transpose(small_g[1], (1, 2, 0, 3)).reshape(nl, CCONV_K, w)
    pending = {}
    token = small_g[1]
    for l in range(nl):
        for gname, members in GATHER_GROUPS:
            gs = [cast_into_slot(wts[n], l, me_arr, name=f"cast_{n}{l}") for n in members]
            send, recv, gs, token = gather_start(gs, token, name=f"gather_start_{gname}{l}")
            pending[gname, l] = (members, gs, send, recv)
    wg = [dict() for _ in range(nl)]

    def arrive(gname, l, after):
        members, gs, send, recv = pending.pop((gname, l))
        gs = gather_wait(gs, send, recv, after, name=f"gather_wait_{gname}{l}")
        wg[l].update(zip(members, gs))
    sconv_pad = jnp.pad(sconv_full, ((0, 0), (0, 8 - SCONV_K), (0, 0)))
    cconv_pad = jnp.pad(cconv_full, ((0, 0), (0, 32 - CCONV_K), (0, 0)))
    zeros_w = jnp.zeros((nl, w), F32)
    vecs = jnp.stack([sgu_norm_g, cconv_ln_g, cconv_ln_b, pool_scale] + [zeros_w] * 4, axis=1)
    wt = jnp.tril(sgu_w).astype(BF16)
    bexp = jnp.repeat(jnp.swapaxes(sgu_b, 1, 2), w // N_HEADS, axis=2)
    eye = jnp.eye(4, dtype=F32)
    pbd = jnp.einsum("lgcd,gh->lgchd", pool_w, eye).reshape(nl, w, w).astype(BF16)

    def mixer_args(l):
        return sconv_pad[l], cconv_pad[l], vecs[l], wt[l], bexp[l], pbd[l]

    saved = []
    xc = x0
    after = token
    for l in range(nl):
        s = {"x_ffn1": xc}
        arrive("ffn1", l, after)
        xc = ffn_fwd(xc, norm_ffn1[l], wg[l]["ffn1_w_in"], wg[l]["ffn1_w_out"], name=f"ffn1_fwd{l}")
        s["x_mix"] = xc
        arrive("mid", l, xc)
        z = mm_rows(xc, wg[l]["mix_w_in"], "col", gain=norm_mix[l], name=f"mix_in{l}")
        y = mixer_fwd(z, *mixer_args(l), name=f"mixer_fwd{l}")
        s["z"], s["y"] = z, y
        xc = mm_rows(y, wg[l]["mix_w_out"], "row", residual=xc, name=f"mix_out{l}")
        s["x_att"] = xc
        kv = mm_rows(mem0, wg[l]["xattn_wkv"], "col", gain=norm_mem[l], name=f"kv{l}")
        s["kv"] = kv
        xc = xattn_fwd(xc, norm_xattn[l], kv, wg[l]["xattn_wq"], wg[l]["xattn_wo"], name=f"xattn_fwd{l}")
        s["x_ffn2"] = xc
        arrive("ffn2", l, xc)
        xc = ffn_fwd(xc, norm_ffn2[l], wg[l]["ffn2_w_in"], wg[l]["ffn2_w_out"], name=f"ffn2_fwd{l}")
        after = xc
        saved.append(s)

    dx, g_norm_final, loss_local = loss_head(xc, target, norm_final, name="loss_head")
    loss = lax.psum(loss_local[0, 0], ("x", "y", "c"))

    tm = _row_tile(t)
    small = {n: [None] * nl for n in SMALL_REPL + SMALL_SHARD if n != "norm_final"}
    scattered = {}
    tie = [token]

    def send_grads(gname, l, grads):
        members = list(grads)
        send, recv, gs, lands, tie[0] = scatter_start(
            [grads[n] for n in members], tie[0], name=f"scatter_start_{gname}{l}")
        scattered[gname, l] = (members, gs, lands, send, recv)

    def tied(v):
        return v + tie[0][0, 0]

    for l in reversed(range(nl)):
        s = saved[l]
        wl = wg[l]
        bg = {}
        dxn = dx
        dx, dgn, bg["ffn2_w_in"], bg["ffn2_w_out"] = ffn_bwd(
            s["x_ffn2"], dxn, tied(norm_ffn2[l]), wl["ffn2_w_in"], wl["ffn2_w_out"], name=f"ffn2_bwd{l}")
        small["norm_ffn2"][l] = dgn[0]
        send_grads("ffn2", l, bg)

        bg = {}
        dxn = dx
        dx, h, dq, o, dkv, dgn = xattn_bwd_rows(
            s["x_att"], dxn, tied(norm_xattn[l]), s["kv"], wl["xattn_wq"], wl["xattn_wo"],
            name=f"xattn_bwd{l}")
        small["norm_xattn"][l] = dgn[0]
        row_spec = pl.BlockSpec((tm, d), lambda s_, i: (i, 0))
        bg["xattn_wq"] = mm_tn(h, dq, nb=1, ka=d, nbk=d, tm=tm, m=t, a_spec=row_spec, b_spec=row_spec,
                               name=f"dwq{l}").reshape(N_DEV, d // N_DEV, d)
        bg["xattn_wo"] = mm_tn(o, dxn, nb=1, ka=d, nbk=d, tm=tm, m=t, a_spec=row_spec, b_spec=row_spec,
                               name=f"dwo{l}").reshape(N_DEV, d // N_DEV, d)
        _, mhat, dgn = mm_nt(dkv, wl["xattn_wkv"], "col", x=mem0, gain=norm_mem[l], name=f"dmem{l}")
        small["norm_mem"][l] = dgn[0]
        nm = mem0.shape[0]
        bg["xattn_wkv"] = mm_tn(mhat, dkv, nb=N_DEV, ka=d, nbk=2 * d // N_DEV, tm=nm, m=nm,
                                a_spec=pl.BlockSpec((nm, d), lambda s_, i: (0, 0)),
                                b_spec=pl.BlockSpec((nm, 2 * d // N_DEV), lambda s_, i: (0, s_)),
                                name=f"dwkv{l}")
        send_grads("xattn", l, bg)

        bg = {}
        dxn = dx
        bg["mix_w_out"] = mm_tn(s["y"], dxn, nb=1, ka=d, nbk=d, tm=tm, m=t, a_spec=row_spec,
                                b_spec=row_spec, name=f"dwmo{l}").reshape(N_DEV, d // N_DEV, d)
        dy = mm_nt(dxn, wl["mix_w_out"], "row", name=f"dy_mix{l}")
        dz, gvec, gcc, gwt, gb, gpbd = mixer_bwd(s["z"], dy, *mixer_args(l), name=f"mixer_bwd{l}")
        small["sconv_w"][l] = gvec[0:SCONV_K]
        small["sgu_norm_g"][l] = gvec[3]
        small["cconv_ln_g"][l] = gvec[4]
        small["cconv_ln_b"][l] = gvec[5]
        small["pool_scale"][l] = gvec[6]
        small["cconv_w"][l] = gcc[0:CCONV_K]
        small["sgu_w"][l] = gwt
        small["sgu_b"][l] = jnp.transpose(gb[:, 0:N_HEADS])
        gw = w // 4
        small["pool_w"][l] = jnp.stack([gpbd[g * gw:(g + 1) * gw, g * gw:(g + 1) * gw] for g in range(4)])
        dx, h, dgn = mm_nt(dz, wl["mix_w_in"], "col", x=s["x_mix"], gain=tied(norm_mix[l]), dx_in=dxn,
                           name=f"dh_mix{l}")
        small["norm_mix"][l] = dgn[0]
        bg["mix_w_in"] = mm_tn(h, dz, nb=N_DEV, ka=d, nbk=w, tm=tm, m=t,
                               a_spec=pl.BlockSpec((tm, d), lambda s_, i: (i, 0)),
                               b_spec=pl.BlockSpec((tm, w), lambda s_, i: (i, s_)), name=f"dwmi{l}")
        send_grads("mix", l, bg)

        bg = {}
        dx, dgn, bg["ffn1_w_in"], bg["ffn1_w_out"] = ffn_bwd(
            s["x_ffn1"], dx, tied(norm_ffn1[l]), wl["ffn1_w_in"], wl["ffn1_w_out"], name=f"ffn1_bwd{l}")
        small["norm_ffn1"][l] = dgn[0]
        send_grads("ffn1", l, bg)

    small_full = {n: jnp.stack(v) for n, v in small.items()}
    small_full["norm_final"] = g_norm_final[0]
    names = SMALL_REPL + SMALL_SHARD
    shapes = [small_full[n].shape for n in names]
    rows_all = _rows_for(shapes)
    summed = all_reduce_small(tied(_pack([small_full[n] for n in names], rows_all)), name="reduce_small")
    gsm = dict(zip(names, _unpack(summed, shapes)))

    out = {}
    after = summed
    for gname in ("ffn2", "xattn", "mix", "ffn1"):
        own, land = {}, {}
        for l in reversed(range(nl)):
            members, gs, lands, send, recv = scattered.pop((gname, l))
            gs, lands = scatter_wait(gs, lands, send, recv, after, name=f"scatter_wait_{gname}{l}")
            for n, g_, l_ in zip(members, gs, lands):
                own.setdefault(n, {})[l] = g_
                land.setdefault(n, {})[l] = l_
        for n in own:
            out[n] = adamw_sharded([own[n][l] for l in range(nl)], [land[n][l] for l in range(nl)],
                                   wts[n], mom[n], var[n], me_arr, name="adamw_" + n)
            after = out[n][1]
    repl_shapes = [wts[n].shape for n in SMALL_REPL]
    rows_r = _rows_for(repl_shapes)
    dl, mn, vn = adamw_flat(_pack([gsm[n] for n in SMALL_REPL], rows_r),
                            _pack([wts[n] for n in SMALL_REPL], rows_r),
                            _pack([mom[n] for n in SMALL_REPL], rows_r),
                            _pack([var[n] for n in SMALL_REPL], rows_r), name="adamw_small")
    for n, a, b, c in zip(SMALL_REPL, _unpack(dl, repl_shapes), _unpack(mn, repl_shapes),
                          _unpack(vn, repl_shapes)):
        out[n] = (gsm[n], a, b, c)
    cs = w // N_DEV
    gsh = {n: lax.dynamic_slice_in_dim(gsm[n], me * cs, cs, axis=2) for n in SMALL_SHARD}
    sh_shapes = [wts[n].shape for n in SMALL_SHARD]
    rows_s = _rows_for(sh_shapes)
    dl, mn, vn = adamw_flat(_pack([gsh[n] for n in SMALL_SHARD], rows_s),
                            _pack([wts[n] for n in SMALL_SHARD], rows_s),
                            _pack([mom[n] for n in SMALL_SHARD], rows_s),
                            _pack([var[n] for n in SMALL_SHARD], rows_s), name="adamw_small_sharded")
    for n, a, b, c in zip(SMALL_SHARD, _unpack(dl, sh_shapes), _unpack(mn, sh_shapes),
                          _unpack(vn, sh_shapes)):
        out[n] = (gsh[n], a, b, c)

    grad_x = dx.reshape(1, t, d)
    return (loss, grad_x, *[out[n][0] for n in WEIGHTS], *[out[n][1] for n in WEIGHTS],
            *[out[n][2] for n in WEIGHTS], *[out[n][3] for n in WEIGHTS])
```

```python
import functools

import jax
import jax.numpy as jnp
from jax import lax
from jax.experimental import pallas as pl
from jax.experimental.pallas import tpu as pltpu

F32 = jnp.float32
BF16 = jnp.bfloat16
MESH = pl.DeviceIdType.MESH
N_DEV = 8
EPS = 1e-6
HALO = 32
SGU_CHUNK = 128
CCONV_K = 31
SCONV_K = 3
MIX_W = 256
N_HEADS = 4
VMEM_LIMIT = 56 * 1024 * 1024
ROW_TILE = 512
TN_TILE = 2048
FFN_FWD_TILE = 1024
MIX_TILE = 512

ADAM_LR = 0.001
ADAM_B1 = 0.9
ADAM_B2 = 0.999
ADAM_EPS = 1e-08
ADAM_WD = 0.01
ADAM_STEP = 10

HBM_SPEC = pl.BlockSpec(memory_space=pltpu.HBM)
VMEM_SPEC = pl.BlockSpec(memory_space=pltpu.VMEM)


def _params(*sem):
    return pltpu.CompilerParams(dimension_semantics=tuple(sem), vmem_limit_bytes=VMEM_LIMIT)


def _row_tile(m, pref=None):
    t = min(m, ROW_TILE if pref is None else pref)
    assert m % t == 0, (m, t)
    return t


def _my_index():
    return lax.axis_index("x") * 4 + lax.axis_index("y") * 2 + lax.axis_index("c")


def _peer(mask):
    x, y, c = lax.axis_index("x"), lax.axis_index("y"), lax.axis_index("c")
    px = 1 - x if mask & 4 else x
    py = 1 - y if mask & 2 else y
    pc = 1 - c if mask & 1 else c
    return (px, py, pc), px * 4 + py * 2 + pc


def all_gather(arrs, name):
    n = len(arrs)

    def body(*refs):
        ins, outs = refs[:n], refs[n:2 * n]
        send_sems, recv_sems, loc_sems = refs[2 * n:]
        me = _my_index()
        local = []
        for i in range(n):
            cp = pltpu.make_async_copy(ins[i], outs[i].at[me], loc_sems.at[i])
            cp.start()
            local.append(cp)
        sends = []
        for i in range(n):
            for m in range(1, N_DEV):
                peer, _ = _peer(m)
                cp = pltpu.make_async_remote_copy(
                    src_ref=ins[i], dst_ref=outs[i].at[me],
                    send_sem=send_sems.at[i, m - 1], recv_sem=recv_sems.at[i, m - 1],
                    device_id=peer, device_id_type=MESH)
                cp.start()
                sends.append(cp)
        for i in range(n):
            for m in range(1, N_DEV):
                peer, pidx = _peer(m)
                pltpu.make_async_remote_copy(
                    src_ref=ins[i], dst_ref=outs[i].at[pidx],
                    send_sem=send_sems.at[i, m - 1], recv_sem=recv_sems.at[i, m - 1],
                    device_id=peer, device_id_type=MESH).wait_recv()
        for cp in sends:
            cp.wait_send()
        for cp in local:
            cp.wait()

    return pl.pallas_call(
        body, name=name,
        out_shape=[jax.ShapeDtypeStruct((N_DEV,) + a.shape, a.dtype) for a in arrs],
        in_specs=[HBM_SPEC] * n, out_specs=[HBM_SPEC] * n,
        scratch_shapes=[pltpu.SemaphoreType.DMA((n, N_DEV - 1)),
                        pltpu.SemaphoreType.DMA((n, N_DEV - 1)),
                        pltpu.SemaphoreType.DMA((n,))],
    )(*arrs)


SEM_SPEC = pl.BlockSpec(memory_space=pltpu.SEMAPHORE)
ANY_SPEC = pl.BlockSpec(memory_space=pl.ANY)
SIDE_EFFECT = pltpu.SideEffectType.DATAFLOW_SIDE_EFFECTING


def _hbm(a):
    return pltpu.with_memory_space_constraint(a, pltpu.HBM)


def _sem_pairs(n):
    return (pltpu.SemaphoreType.DMA((n * (N_DEV - 1),)), pltpu.SemaphoreType.DMA((n * (N_DEV - 1),)))


def _sem(i, m):
    return i * (N_DEV - 1) + m - 1


def _gather_copy(g_ref, i, m, send_sems, recv_sems, origin):
    peer, _ = _peer(m)
    return pltpu.make_async_remote_copy(
        src_ref=g_ref.at[origin], dst_ref=g_ref.at[origin],
        send_sem=send_sems.at[_sem(i, m)], recv_sem=recv_sems.at[_sem(i, m)],
        device_id=peer, device_id_type=MESH)


GATHER_MASKS = (1, 2, 4, 6)
FORWARD_MASKS = (2, 4, 6)


def gather_start(gs, after, name):
    n = len(gs)

    def body(*refs):
        g_in = refs[:n]
        send_sems, recv_sems = refs[n + 1], refs[n + 2]
        token = refs[-1]
        me = _my_index()
        for i in range(n):
            for m in GATHER_MASKS:
                _gather_copy(g_in[i], i, m, send_sems, recv_sems, me).start()
        token[...] = jnp.zeros_like(token)

    outs = pl.pallas_call(
        body, name=name,
        out_shape=(*_sem_pairs(n), *[pltpu.HBM(g.shape, g.dtype) for g in gs],
                   jax.ShapeDtypeStruct((8, 128), F32)),
        in_specs=[HBM_SPEC] * n + [ANY_SPEC],
        out_specs=(SEM_SPEC, SEM_SPEC, *[HBM_SPEC] * n, VMEM_SPEC),
        input_output_aliases={i: 2 + i for i in range(n)},
        compiler_params=pltpu.CompilerParams(has_side_effects=SIDE_EFFECT),
    )(*[_hbm(g) for g in gs], after)
    return outs[0], outs[1], list(outs[2:2 + n]), outs[-1]


def gather_wait(gs, send_sems, recv_sems, after, name):
    n = len(gs)

    def body(*refs):
        g_in = refs[:n]
        send, recv = refs[n], refs[n + 1]
        me = _my_index()
        for i in range(n):
            for m in GATHER_MASKS:
                _, pidx = _peer(m)
                _gather_copy(g_in[i], i, m, send, recv, me).wait_send()
                _gather_copy(g_in[i], i, m, send, recv, pidx).wait_recv()

    outs = pl.pallas_call(
        body, name=name,
        out_shape=[pltpu.HBM(g.shape, g.dtype) for g in gs],
        in_specs=[HBM_SPEC] * n + [SEM_SPEC, SEM_SPEC, ANY_SPEC],
        out_specs=[HBM_SPEC] * n,
        input_output_aliases={i: i for i in range(n)},
        compiler_params=pltpu.CompilerParams(has_side_effects=SIDE_EFFECT),
    )(*gs, send_sems, recv_sems, after)
    return list(outs)


def sibling_forward(gs, name):
    n = len(gs)
    nf = len(FORWARD_MASKS)

    def body(*refs):
        g_in = refs[:n]
        send_sems, recv_sems = refs[2 * n:]
        x, y, c = lax.axis_index("x"), lax.axis_index("y"), lax.axis_index("c")
        sibling = (x, y, 1 - c)

        def copy(i, k, origin):
            return pltpu.make_async_remote_copy(
                src_ref=g_in[i].at[origin], dst_ref=g_in[i].at[origin],
                send_sem=send_sems.at[i * nf + k], recv_sem=recv_sems.at[i * nf + k],
                device_id=sibling, device_id_type=MESH)
        sends = []
        for i in range(n):
            for k, m in enumerate(FORWARD_MASKS):
                _, origin = _peer(m)
                cp = copy(i, k, origin)
                cp.start()
                sends.append(cp)
        for i in range(n):
            for k, m in enumerate(FORWARD_MASKS):
                _, origin = _peer(m ^ 1)
                copy(i, k, origin).wait_recv()
        for cp in sends:
            cp.wait_send()

    outs = pl.pallas_call(
        body, name=name,
        out_shape=[jax.ShapeDtypeStruct(g.shape, g.dtype) for g in gs],
        in_specs=[HBM_SPEC] * n, out_specs=[HBM_SPEC] * n,
        input_output_aliases={i: i for i in range(n)},
        scratch_shapes=[pltpu.SemaphoreType.DMA((n * nf,)), pltpu.SemaphoreType.DMA((n * nf,))],
    )(*gs)
    return list(outs)


def _scatter_copy(g_ref, l_ref, i, m, send_sems, recv_sems):
    peer, pidx = _peer(m)
    return pltpu.make_async_remote_copy(
        src_ref=g_ref.at[pidx], dst_ref=l_ref.at[m - 1],
        send_sem=send_sems.at[_sem(i, m)], recv_sem=recv_sems.at[_sem(i, m)],
        device_id=peer, device_id_type=MESH)


def scatter_start(grads, after, name):
    n = len(grads)
    lands = [lax.empty((N_DEV - 1,) + g.shape[1:], g.dtype) for g in grads]

    def body(*refs):
        g_in, l_in = refs[:n], refs[n:2 * n]
        send_sems, recv_sems = refs[2 * n + 1], refs[2 * n + 2]
        token = refs[-1]
        for i in range(n):
            for m in range(1, N_DEV):
                _scatter_copy(g_in[i], l_in[i], i, m, send_sems, recv_sems).start()
        token[...] = jnp.zeros_like(token)

    outs = pl.pallas_call(
        body, name=name,
        out_shape=(*_sem_pairs(n), *[pltpu.HBM(g.shape, g.dtype) for g in grads],
                   *[pltpu.HBM(l.shape, l.dtype) for l in lands], jax.ShapeDtypeStruct((8, 128), F32)),
        in_specs=[HBM_SPEC] * (2 * n) + [ANY_SPEC],
        out_specs=(SEM_SPEC, SEM_SPEC, *[HBM_SPEC] * (2 * n), VMEM_SPEC),
        input_output_aliases={i: 2 + i for i in range(2 * n)},
        compiler_params=pltpu.CompilerParams(has_side_effects=SIDE_EFFECT),
    )(*[_hbm(g) for g in grads], *[_hbm(l) for l in lands], after)
    return outs[0], outs[1], list(outs[2:2 + n]), list(outs[2 + n:2 + 2 * n]), outs[-1]


def scatter_wait(grads, lands, send_sems, recv_sems, after, name):
    n = len(grads)

    def body(*refs):
        g_in, l_in = refs[:n], refs[n:2 * n]
        send, recv = refs[2 * n], refs[2 * n + 1]
        for i in range(n):
            for m in range(1, N_DEV):
                cp = _scatter_copy(g_in[i], l_in[i], i, m, send, recv)
                cp.wait_send()
                cp.wait_recv()

    outs = pl.pallas_call(
        body, name=name,
        out_shape=[pltpu.HBM(a.shape, a.dtype) for a in list(grads) + list(lands)],
        in_specs=[HBM_SPEC] * (2 * n) + [SEM_SPEC, SEM_SPEC, ANY_SPEC],
        out_specs=[HBM_SPEC] * (2 * n),
        input_output_aliases={i: i for i in range(2 * n)},
        compiler_params=pltpu.CompilerParams(has_side_effects=SIDE_EFFECT),
    )(*grads, *lands, send_sems, recv_sems, after)
    return list(outs[:n]), list(outs[n:])


def all_reduce_small(packed, after, name):
    r, c = packed.shape

    def body(in_ref, after_ref, out_ref, gath, send_sems, recv_sems):
        me = _my_index()
        gath[me] = in_ref[...]
        sends = []
        for m in range(1, N_DEV):
            peer, _ = _peer(m)
            cp = pltpu.make_async_remote_copy(
                src_ref=in_ref, dst_ref=gath.at[me],
                send_sem=send_sems.at[m - 1], recv_sem=recv_sems.at[m - 1],
                device_id=peer, device_id_type=MESH)
            cp.start()
            sends.append(cp)
        for m in range(1, N_DEV):
            peer, pidx = _peer(m)
            pltpu.make_async_remote_copy(
                src_ref=in_ref, dst_ref=gath.at[pidx],
                send_sem=send_sems.at[m - 1], recv_sem=recv_sems.at[m - 1],
                device_id=peer, device_id_type=MESH).wait_recv()
        for cp in sends:
            cp.wait_send()
        acc = gath[0]
        for p in range(1, N_DEV):
            acc = acc + gath[p]
        out_ref[...] = acc

    return pl.pallas_call(
        body, name=name,
        out_shape=jax.ShapeDtypeStruct((r, c), F32),
        in_specs=[VMEM_SPEC, ANY_SPEC], out_specs=VMEM_SPEC,
        scratch_shapes=[pltpu.VMEM((N_DEV, r, c), F32),
                        pltpu.SemaphoreType.DMA((N_DEV - 1,)),
                        pltpu.SemaphoreType.DMA((N_DEV - 1,))],
        compiler_params=pltpu.CompilerParams(vmem_limit_bytes=VMEM_LIMIT),
    )(packed, after)


def _sigmoid(v):
    return 1.0 / (1.0 + jnp.exp(-v))


def _rms_fwd(xf, g):
    r = lax.rsqrt(jnp.mean(xf * xf, axis=-1, keepdims=True) + EPS)
    return xf * r, r


def _rms_bwd(xhat, r, g, dy):
    dg = jnp.sum(dy * xhat, axis=0, keepdims=True)
    dxh = dy * g
    dx = r * (dxh - xhat * jnp.mean(dxh * xhat, axis=-1, keepdims=True))
    return dx, dg


def _ln_stats(v):
    mu = jnp.mean(v, axis=-1, keepdims=True)
    vc = v - mu
    r = lax.rsqrt(jnp.mean(vc * vc, axis=-1, keepdims=True) + EPS)
    return vc * r, r


def _ln_bwd(xhat, r, dxh):
    return r * (dxh - jnp.mean(dxh, axis=-1, keepdims=True)
                - xhat * jnp.mean(dxh * xhat, axis=-1, keepdims=True))


def _dot(a, b):
    return jnp.dot(a, b, preferred_element_type=F32)


def _dot_nt(a, b):
    return lax.dot_general(a, b, (((1,), (1,)), ((), ())), preferred_element_type=F32)


def _dot_tn(a, b):
    return lax.dot_general(a, b, (((0,), (0,)), ((), ())), preferred_element_type=F32)


def _full_weight(w_ref, kind):
    assert kind == "row"
    p, a, b = w_ref.shape
    return w_ref[...].reshape(p * a, b)


def _wspec(wg):
    return pl.BlockSpec(wg.shape, lambda *_: (0, 0, 0))


def mm_rows(a, wg, kind, *, gain=None, residual=None, out_dtype=F32, name, tm=None):
    m, k = a.shape
    p, wa, wb = wg.shape
    n = p * wb if kind == "col" else wb
    tm = _row_tile(m, tm)
    has_gain, has_res = gain is not None, residual is not None

    def body(*refs):
        refs = list(refs)
        a_ref = refs.pop(0)
        g_ref = refs.pop(0) if has_gain else None
        w_ref = refs.pop(0)
        r_ref = refs.pop(0) if has_res else None
        o_ref = refs.pop(0)
        if has_gain:
            xhat, _ = _rms_fwd(a_ref[...].astype(F32), None)
            h = (xhat * g_ref[...]).astype(BF16)
        else:
            h = a_ref[...].astype(BF16)
        if kind == "col":
            for j in range(p):
                o = _dot(h, w_ref[j])
                if has_res:
                    o = o + r_ref[:, j * wb:(j + 1) * wb]
                o_ref[:, j * wb:(j + 1) * wb] = o.astype(out_dtype)
        else:
            o = _dot(h, _full_weight(w_ref, "row"))
            if has_res:
                o = o + r_ref[...]
            o_ref[...] = o.astype(out_dtype)

    operands = [a]
    in_specs = [pl.BlockSpec((tm, k), lambda i: (i, 0))]
    if has_gain:
        operands.append(gain.reshape(1, k))
        in_specs.append(pl.BlockSpec((1, k), lambda i: (0, 0)))
    operands.append(wg)
    in_specs.append(_wspec(wg))
    if has_res:
        operands.append(residual)
        in_specs.append(pl.BlockSpec((tm, n), lambda i: (i, 0)))
    return pl.pallas_call(
        body, name=name, grid=(m // tm,),
        out_shape=jax.ShapeDtypeStruct((m, n), out_dtype),
        in_specs=in_specs, out_specs=pl.BlockSpec((tm, n), lambda i: (i, 0)),
        compiler_params=_params("parallel"),
    )(*operands)


def mm_nt(dz, wg, kind, *, x=None, gain=None, dx_in=None, name, tm=None):
    m, n = dz.shape
    p, wa, wb = wg.shape
    k = wa if kind == "col" else p * wa
    tm = _row_tile(m, tm)
    epi = x is not None
    has_dx = dx_in is not None

    def body(*refs):
        refs = list(refs)
        dz_ref, w_ref = refs.pop(0), refs.pop(0)
        if epi:
            x_ref, g_ref = refs.pop(0), refs.pop(0)
            dxi_ref = refs.pop(0) if has_dx else None
            dx_ref, h_ref, dg_ref = refs
        else:
            (da_ref,) = refs
        dzb = dz_ref[...].astype(BF16)
        if kind == "col":
            da = _dot_nt(dzb[:, 0:wb], w_ref[0])
            for j in range(1, p):
                da = da + _dot_nt(dzb[:, j * wb:(j + 1) * wb], w_ref[j])
        else:
            da = _dot_nt(dzb, _full_weight(w_ref, "row"))
        if not epi:
            da_ref[...] = da
            return
        g = g_ref[...]
        xhat, r = _rms_fwd(x_ref[...].astype(F32), None)
        h_ref[...] = (xhat * g).astype(BF16)
        dx, dg = _rms_bwd(xhat, r, g, da)
        if has_dx:
            dx = dx + dxi_ref[...]
        dx_ref[...] = dx

        @pl.when(pl.program_id(0) == 0)
        def _():
            dg_ref[...] = jnp.zeros_like(dg_ref)
        dg_ref[...] += dg

    row = lambda i: (i, 0)
    operands = [dz, wg]
    in_specs = [pl.BlockSpec((tm, n), row), _wspec(wg)]
    if epi:
        operands += [x, gain.reshape(1, k)]
        in_specs += [pl.BlockSpec((tm, k), row), pl.BlockSpec((1, k), lambda i: (0, 0))]
        if has_dx:
            operands.append(dx_in)
            in_specs.append(pl.BlockSpec((tm, k), row))
        out_shape = [jax.ShapeDtypeStruct((m, k), F32), jax.ShapeDtypeStruct((m, k), BF16),
                     jax.ShapeDtypeStruct((1, k), F32)]
        out_specs = [pl.BlockSpec((tm, k), row), pl.BlockSpec((tm, k), row),
                     pl.BlockSpec((1, k), lambda i: (0, 0))]
    else:
        out_shape = jax.ShapeDtypeStruct((m, k), F32)
        out_specs = pl.BlockSpec((tm, k), row)
    return pl.pallas_call(
        body, name=name, grid=(m // tm,), out_shape=out_shape,
        in_specs=in_specs, out_specs=out_specs,
        compiler_params=_params("arbitrary"),
    )(*operands)


def mm_tn(a, b, *, nb, a_spec, b_spec, ka, nbk, tm, m, scale=1.0, out_dtype=BF16, col_slots=1, name):
    ni = m // tm
    assert col_slots == 1 or nb == 1
    cw = nbk // col_slots

    def body(a_ref, b_ref, o_ref, acc):
        i = pl.program_id(1)

        @pl.when(i == 0)
        def _():
            acc[...] = jnp.zeros_like(acc)
        acc[...] += _dot_tn(a_ref[...].astype(BF16), b_ref[...].astype(BF16))

        @pl.when(i == ni - 1)
        def _():
            if col_slots == 1:
                o_ref[...] = (acc[...] * scale).astype(out_dtype)
            else:
                for j in range(col_slots):
                    o_ref[j] = (acc[:, j * cw:(j + 1) * cw] * scale).astype(out_dtype)

    if col_slots == 1:
        out_shape = jax.ShapeDtypeStruct((nb, ka, nbk), out_dtype)
        out_spec = pl.BlockSpec((None, ka, nbk), lambda s, i: (s, 0, 0))
    else:
        out_shape = jax.ShapeDtypeStruct((col_slots, ka, cw), out_dtype)
        out_spec = pl.BlockSpec((col_slots, ka, cw), lambda s, i: (0, 0, 0))
    return pl.pallas_call(
        body, name=name, grid=(nb, ni), out_shape=out_shape,
        in_specs=[a_spec, b_spec], out_specs=out_spec,
        scratch_shapes=[pltpu.VMEM((ka, nbk), F32)],
        compiler_params=_params("parallel", "arbitrary"),
    )(a, b)


def _ffn_specs(w_in_g, w_out_g, d):
    nf = w_in_g.shape[2]
    hr = w_out_g.shape[1]
    assert 2 * hr == nf
    w_in5 = w_in_g.reshape(2, 4, d, nf)
    w_out5 = w_out_g.reshape(4, 2, hr, d)
    in_spec = pl.BlockSpec((2, None, d, nf), lambda i, j: (0, j, 0, 0))
    out_spec = pl.BlockSpec((None, 2, hr, d), lambda i, j: (j, 0, 0, 0))
    return w_in5, w_out5, in_spec, out_spec, nf


def ffn_fwd(x, gain, w_in_g, w_out_g, *, name, tm=None):
    t, d = x.shape
    tm = _row_tile(t, tm)
    w_in5, w_out5, wi_spec, wo_spec, nf = _ffn_specs(w_in_g, w_out_g, d)

    def body(x_ref, g_ref, wi_ref, wo_ref, o_ref, h_scr, acc):
        j = pl.program_id(1)

        @pl.when(j == 0)
        def _():
            xhat, _ = _rms_fwd(x_ref[...], None)
            h_scr[...] = (xhat * g_ref[...]).astype(BF16)
            acc[...] = jnp.zeros_like(acc)
        h = h_scr[...]
        gt = _dot(h, wi_ref[0])
        up = _dot(h, wi_ref[1])
        act = (gt * _sigmoid(gt) * up).astype(BF16)
        acc[...] += _dot(act, wo_ref[...].reshape(nf, d))

        @pl.when(j == 3)
        def _():
            o_ref[...] = x_ref[...] + 0.5 * acc[...]

    return pl.pallas_call(
        body, name=name, grid=(t // tm, 4),
        out_shape=jax.ShapeDtypeStruct((t, d), F32),
        in_specs=[pl.BlockSpec((tm, d), lambda i, j: (i, 0)),
                  pl.BlockSpec((1, d), lambda i, j: (0, 0)), wi_spec, wo_spec],
        out_specs=pl.BlockSpec((tm, d), lambda i, j: (i, 0)),
        scratch_shapes=[pltpu.VMEM((tm, d), BF16), pltpu.VMEM((tm, d), F32)],
        compiler_params=_params("parallel", "arbitrary"),
    )(x, gain.reshape(1, d), w_in5, w_out5)


def ffn_bwd_rows(x, dy, gain, w_in_g, w_out_g, *, name, tm=None):
    t, d = x.shape
    tm = _row_tile(t, tm)
    w_in5, w_out5, wi_spec, wo_spec, nf = _ffn_specs(w_in_g, w_out_g, d)

    def body(x_ref, dy_ref, g_ref, wi_ref, wo_ref, dx_ref, h_ref, act_ref, dgu_ref, dg_ref,
             dh_acc, dyh_scr):
        i, j = pl.program_id(0), pl.program_id(1)

        @pl.when(j == 0)
        def _():
            xhat, _ = _rms_fwd(x_ref[...], None)
            h_ref[...] = (xhat * g_ref[...]).astype(BF16)
            dyh_scr[...] = (0.5 * dy_ref[...]).astype(BF16)
            dh_acc[...] = jnp.zeros_like(dh_acc)
        h = h_ref[...]
        gt = _dot(h, wi_ref[0])
        up = _dot(h, wi_ref[1])
        sg = _sigmoid(gt)
        silu = gt * sg
        act_ref[...] = (silu * up).astype(BF16)
        dact = _dot_nt(dyh_scr[...], wo_ref[...].reshape(nf, d))
        dgt = (dact * up * (sg * (1.0 + gt * (1.0 - sg)))).astype(BF16)
        dup = (dact * silu).astype(BF16)
        dgu_ref[0] = dgt
        dgu_ref[1] = dup
        dh_acc[...] += _dot_nt(dgt, wi_ref[0]) + _dot_nt(dup, wi_ref[1])

        @pl.when(j == 3)
        def _():
            g = g_ref[...]
            xhat, r = _rms_fwd(x_ref[...], None)
            dx, dg = _rms_bwd(xhat, r, g, dh_acc[...])
            dx_ref[...] = dy_ref[...] + dx

            @pl.when(i == 0)
            def _():
                dg_ref[...] = jnp.zeros_like(dg_ref)
            dg_ref[...] += dg

    row = lambda i, j: (i, 0)
    return pl.pallas_call(
        body, name=name, grid=(t // tm, 4),
        out_shape=[jax.ShapeDtypeStruct((t, d), F32), jax.ShapeDtypeStruct((t, d), BF16),
                   jax.ShapeDtypeStruct((4, t, nf), BF16), jax.ShapeDtypeStruct((2, 4, t, nf), BF16),
                   jax.ShapeDtypeStruct((1, d), F32)],
        in_specs=[pl.BlockSpec((tm, d), row), pl.BlockSpec((tm, d), row),
                  pl.BlockSpec((1, d), lambda i, j: (0, 0)), wi_spec, wo_spec],
        out_specs=[pl.BlockSpec((tm, d), row), pl.BlockSpec((tm, d), row),
                   pl.BlockSpec((None, tm, nf), lambda i, j: (j, i, 0)),
                   pl.BlockSpec((2, None, tm, nf), lambda i, j: (0, j, i, 0)),
                   pl.BlockSpec((1, d), lambda i, j: (0, 0))],
        scratch_shapes=[pltpu.VMEM((tm, d), F32), pltpu.VMEM((tm, d), BF16)],
        compiler_params=_params("arbitrary", "arbitrary"),
    )(x, dy, gain.reshape(1, d), w_in5, w_out5)


def ffn_bwd(x, dy, gain, w_in_g, w_out_g, *, name):
    t, d = x.shape
    dx, h, act, dgu, dgain = ffn_bwd_rows(x, dy, gain, w_in_g, w_out_g, name=name + "_rows")
    nf = act.shape[-1]
    tm = _row_tile(t, TN_TILE)
    d_w_in = mm_tn(h, dgu.reshape(8, t, nf), nb=8, ka=d, nbk=nf, tm=tm, m=t,
                   a_spec=pl.BlockSpec((tm, d), lambda s, i: (i, 0)),
                   b_spec=pl.BlockSpec((None, tm, nf), lambda s, i: (s, i, 0)),
                   name=name + "_dwin")
    d_w_out = mm_tn(act, dy, nb=4, ka=nf, nbk=d, tm=tm, m=t, scale=0.5,
                    a_spec=pl.BlockSpec((None, tm, nf), lambda s, i: (s, i, 0)),
                    b_spec=pl.BlockSpec((tm, d), lambda s, i: (i, 0)),
                    name=name + "_dwout")
    return dx, dgain, d_w_in, d_w_out.reshape(8, nf // 2, d)


def _lane_group(shape):
    return lax.shift_right_logical(lax.broadcasted_iota(jnp.int32, shape, 1), 6)


def _pool_count(t0, rows):
    t = (t0 + lax.broadcasted_iota(jnp.int32, (rows, MIX_W), 0) + 1).astype(F32)
    return jnp.minimum(t, _by_group(_lane_group((rows, MIX_W)), 2.0, 4.0, 8.0, 16.0))


def _by_group(grp, v0, v1, v2, v3):
    return jnp.where(grp == 0, v0, jnp.where(grp == 1, v1, jnp.where(grp == 2, v2, v3)))


def _sgu_mix(wt_ref, vnc):
    grp = _lane_group((SGU_CHUNK, MIX_W))
    out = jnp.zeros((SGU_CHUNK, MIX_W), F32)
    for hd in range(N_HEADS):
        out = jnp.where(grp == hd, _dot(wt_ref[hd], vnc), out)
    return out


def _pool_fwd(s1, s2, s3, t0, ts, lo):
    h = lo
    s2[h - 24:h + ts] = s1[h - 24:h + ts] + s1[h - 25:h + ts - 1]
    s3[h - 16:h + ts] = s2[h - 16:h + ts] + s2[h - 18:h + ts - 2]
    sum2 = s2[h:h + ts]
    sum4 = s3[h:h + ts]
    s2[h - 8:h + ts] = s3[h - 8:h + ts] + s3[h - 12:h + ts - 4]
    sum8 = s2[h:h + ts]
    sum16 = sum8 + s2[h - 8:h + ts - 8]
    grp = _lane_group((ts, MIX_W))
    return _by_group(grp, sum2, sum4, sum8, sum16) / _pool_count(t0, ts) - s1[h:h + ts]


def mixer_fwd(z, sconv, cconv, vecs, wt, bexp, pbd, *, name, ts=None):
    t = z.shape[0]
    ts = _row_tile(t, MIX_TILE if ts is None else ts)
    hl = HALO
    w = MIX_W
    nch = ts // SGU_CHUNK

    def body(zc, zp, sconv_ref, cconv_ref, vec_ref, wt_ref, bexp_ref, pbd_ref, y_ref, s1, s2, s3):
        i = pl.program_id(0)
        has_prev = i > 0

        def col(ref, c):
            return ref[:, c * w:(c + 1) * w]

        def prev(c):
            return jnp.where(has_prev, col(zp, c), 0.0)

        s1[0:hl] = prev(1) * prev(2)
        s1[hl:hl + ts] = col(zc, 1) * col(zc, 2)
        cv = sconv_ref[0:1] * s1[hl - 2:hl - 2 + ts]
        for k in range(1, SCONV_K):
            cv = cv + sconv_ref[k:k + 1] * s1[hl - 2 + k:hl - 2 + k + ts]
        y_ref[:, 0:w] = (col(zc, 0) * cv).astype(BF16)

        xhat, _ = _ln_stats(col(zc, 4))
        vn = (xhat * vec_ref[0:1]).astype(BF16)
        for c in range(nch):
            rows = slice(c * SGU_CHUNK, (c + 1) * SGU_CHUNK)
            mixed = _sgu_mix(wt_ref, vn[rows]) + bexp_ref[...]
            y_ref[rows, w:2 * w] = (zc[rows, 3 * w:4 * w] * mixed).astype(BF16)

        s1[0:hl] = prev(5) * _sigmoid(prev(6))
        s1[hl:hl + ts] = col(zc, 5) * _sigmoid(col(zc, 6))
        off = hl - (CCONV_K - 1)
        cv = cconv_ref[0:1] * s1[off:off + ts]
        for k in range(1, CCONV_K):
            cv = cv + cconv_ref[k:k + 1] * s1[off + k:off + k + ts]
        xhat, _ = _ln_stats(cv)
        ln = xhat * vec_ref[1:2] + vec_ref[2:3]
        y_ref[:, 2 * w:3 * w] = (ln * _sigmoid(ln)).astype(BF16)

        s1[0:hl] = prev(7)
        s1[hl:hl + ts] = col(zc, 7)
        pooled = _pool_fwd(s1, s2, s3, i * ts, ts, hl)
        y_ref[:, 3 * w:4 * w] = (_dot(pooled.astype(BF16), pbd_ref[...]) * vec_ref[3:4]).astype(BF16)

    full = lambda shape: pl.BlockSpec(shape, lambda i: (0,) * len(shape))
    return pl.pallas_call(
        body, name=name, grid=(t // ts,),
        out_shape=jax.ShapeDtypeStruct((t, 4 * w), BF16),
        in_specs=[pl.BlockSpec((ts, 8 * w), lambda i: (i, 0)),
                  pl.BlockSpec((hl, 8 * w), lambda i: (jnp.maximum(i * (ts // hl) - 1, 0), 0)),
                  full((8, w)), full((32, w)), full((8, w)), full((N_HEADS, SGU_CHUNK, SGU_CHUNK)),
                  full((SGU_CHUNK, w)), full((w, w))],
        out_specs=pl.BlockSpec((ts, 4 * w), lambda i: (i, 0)),
        scratch_shapes=[pltpu.VMEM((hl + ts, w), F32)] * 3,
        compiler_params=_params("parallel"),
    )(z, z, sconv, cconv, vecs, wt, bexp, pbd)


def mixer_bwd(z, dy, sconv, cconv, vecs, wt, bexp, pbd, *, name, ts=None):
    t = z.shape[0]
    ts = _row_tile(t, MIX_TILE if ts is None else ts)
    hl = HALO
    w = MIX_W
    nch = ts // SGU_CHUNK
    ni = t // ts
    ext = ts + hl

    def body(zc, zp, zn, dyc, dyn, sconv_ref, cconv_ref, vec_ref, wt_ref, bexp_ref, pbd_ref,
             dz_ref, gvec_ref, gcc_ref, gwt_ref, gb_ref, gpbd_ref, s1, s2, s3):
        i = pl.program_id(0)
        has_prev = i > 0
        has_next = i < ni - 1

        @pl.when(i == 0)
        def _():
            gvec_ref[...] = jnp.zeros_like(gvec_ref)
            gcc_ref[...] = jnp.zeros_like(gcc_ref)
            gwt_ref[...] = jnp.zeros_like(gwt_ref)
            gb_ref[...] = jnp.zeros_like(gb_ref)
            gpbd_ref[...] = jnp.zeros_like(gpbd_ref)

        def col(ref, c):
            return ref[:, c * w:(c + 1) * w]

        def prev(c):
            return jnp.where(has_prev, col(zp, c), 0.0)

        def nxt(c):
            return jnp.where(has_next, col(zn, c), 0.0)

        def dnext(c):
            return jnp.where(has_next, col(dyn, c), 0.0)

        def rowsum(v):
            return jnp.sum(v, axis=0, keepdims=True)

        s1[0:hl] = prev(1) * prev(2)
        s1[hl:hl + ts] = col(zc, 1) * col(zc, 2)
        s1[hl + ts:hl + ts + hl] = nxt(1) * nxt(2)
        cv = sconv_ref[0:1] * s1[hl - 2:hl - 2 + ts]
        for k in range(1, SCONV_K):
            cv = cv + sconv_ref[k:k + 1] * s1[hl - 2 + k:hl - 2 + k + ts]
        dya = col(dyc, 0)
        dz_ref[:, 0:w] = (dya * cv).astype(BF16)
        s2[0:ts] = dya * col(zc, 0)
        s2[ts:ext] = dnext(0) * nxt(0)
        dv = sconv_ref[0:1] * s2[2:2 + ts]
        for k in range(1, SCONV_K):
            dv = dv + sconv_ref[k:k + 1] * s2[2 - k:2 - k + ts]
        dz_ref[:, w:2 * w] = (dv * col(zc, 2)).astype(BF16)
        dz_ref[:, 2 * w:3 * w] = (dv * col(zc, 1)).astype(BF16)
        dcv = s2[0:ts]
        for k in range(SCONV_K):
            gvec_ref[k:k + 1] += rowsum(dcv * s1[hl - 2 + k:hl - 2 + k + ts])

        g_sgu = vec_ref[0:1]
        xhat, rstd = _ln_stats(col(zc, 4))
        vn = (xhat * g_sgu).astype(BF16)
        grp = _lane_group((SGU_CHUNK, w))
        lane = lax.broadcasted_iota(jnp.int32, (SGU_CHUNK, SGU_CHUNK), 1)
        tril = lax.broadcasted_iota(jnp.int32, (SGU_CHUNK, SGU_CHUNK), 0) >= lane
        for c in range(nch):
            rows = slice(c * SGU_CHUNK, (c + 1) * SGU_CHUNK)
            vnc = vn[rows]
            mixed = _sgu_mix(wt_ref, vnc) + bexp_ref[...]
            dyb = dyc[rows, w:2 * w]
            dz_ref[rows, 3 * w:4 * w] = (dyb * mixed).astype(BF16)
            dmix = dyb * zc[rows, 3 * w:4 * w]
            dmixb = dmix.astype(BF16)
            dvn = jnp.zeros((SGU_CHUNK, w), F32)
            gb = jnp.zeros((SGU_CHUNK, SGU_CHUNK), F32)
            for hd in range(N_HEADS):
                dvn = jnp.where(grp == hd, _dot_tn(wt_ref[hd], dmixb), dvn)
                dm_h = jnp.where(grp == hd, dmix, 0.0)
                gwt_ref[hd] += jnp.where(tril, _dot_nt(dm_h.astype(BF16), vnc), 0.0)
                gb = gb + jnp.where(lane == hd, jnp.sum(dm_h, axis=1, keepdims=True), 0.0)
            gb_ref[...] += gb
            s3[rows] = dvn
        dvn = s3[0:ts]
        gvec_ref[3:4] += rowsum(dvn * xhat)
        dz_ref[:, 4 * w:5 * w] = _ln_bwd(xhat, rstd, dvn * g_sgu).astype(BF16)

        sig_c = _sigmoid(col(zc, 6))
        s1[0:hl] = prev(5) * _sigmoid(prev(6))
        s1[hl:hl + ts] = col(zc, 5) * sig_c
        s1[hl + ts:hl + ts + hl] = nxt(5) * _sigmoid(nxt(6))
        off = hl - (CCONV_K - 1)
        cv = cconv_ref[0:1] * s1[off:off + ext]
        for k in range(1, CCONV_K):
            cv = cv + cconv_ref[k:k + 1] * s1[off + k:off + k + ext]
        xhat, rstd = _ln_stats(cv)
        ln = xhat * vec_ref[1:2] + vec_ref[2:3]
        sg = _sigmoid(ln)
        s2[0:ts] = col(dyc, 2)
        s2[ts:ext] = dnext(2)
        dln = s2[0:ext] * (sg * (1.0 + ln * (1.0 - sg)))
        gvec_ref[4:5] += rowsum(dln[0:ts] * xhat[0:ts])
        gvec_ref[5:6] += rowsum(dln[0:ts])
        s3[0:ext] = _ln_bwd(xhat, rstd, dln * vec_ref[1:2])
        dyg = cconv_ref[0:1] * s3[CCONV_K - 1:CCONV_K - 1 + ts]
        for k in range(1, CCONV_K):
            dyg = dyg + cconv_ref[k:k + 1] * s3[CCONV_K - 1 - k:CCONV_K - 1 - k + ts]
        dz_ref[:, 5 * w:6 * w] = (dyg * sig_c).astype(BF16)
        dz_ref[:, 6 * w:7 * w] = (dyg * col(zc, 5) * sig_c * (1.0 - sig_c)).astype(BF16)
        dcv = s3[0:ts]
        for k in range(CCONV_K):
            gcc_ref[k:k + 1] += rowsum(dcv * s1[off + k:off + k + ts])

        scale = vec_ref[3:4]
        s1[0:hl] = prev(7)
        s1[hl:hl + ts] = col(zc, 7)
        pooled = _pool_fwd(s1, s2, s3, i * ts, ts, hl).astype(BF16)
        q0 = _dot(pooled, pbd_ref[...])
        dyd = col(dyc, 3)
        gvec_ref[6:7] += rowsum(dyd * q0)
        dq = (dyd * scale).astype(BF16)
        gpbd_ref[...] += _dot_tn(pooled, dq)
        s1[0:ts] = _dot_nt(dq, pbd_ref[...])
        s1[ts:ext] = _dot_nt((dnext(3) * scale).astype(BF16), pbd_ref[...])
        dpool = s1[0:ts]
        s2[0:ext] = s1[0:ext] / _pool_count(i * ts, ext)
        s3[0:ts + 24] = s2[0:ts + 24] + s2[1:ts + 25]
        f2 = s3[0:ts]
        s2[0:ts + 16] = s3[0:ts + 16] + s3[2:ts + 18]
        f4 = s2[0:ts]
        s3[0:ts + 8] = s2[0:ts + 8] + s2[4:ts + 12]
        f8 = s3[0:ts]
        f16 = f8 + s3[8:ts + 8]
        dz_ref[:, 7 * w:8 * w] = (_by_group(_lane_group((ts, w)), f2, f4, f8, f16) - dpool).astype(BF16)

    full = lambda shape: pl.BlockSpec(shape, lambda i: (0,) * len(shape))
    r = ts // hl
    prev_map = lambda i: (jnp.maximum(i * r - 1, 0), 0)
    next_map = lambda i: (jnp.minimum((i + 1) * r, t // hl - 1), 0)
    return pl.pallas_call(
        body, name=name, grid=(ni,),
        out_shape=[jax.ShapeDtypeStruct((t, 8 * w), BF16), jax.ShapeDtypeStruct((8, w), F32),
                   jax.ShapeDtypeStruct((32, w), F32),
                   jax.ShapeDtypeStruct((N_HEADS, SGU_CHUNK, SGU_CHUNK), F32),
                   jax.ShapeDtypeStruct((SGU_CHUNK, SGU_CHUNK), F32), jax.ShapeDtypeStruct((w, w), F32)],
        in_specs=[pl.BlockSpec((ts, 8 * w), lambda i: (i, 0)),
                  pl.BlockSpec((hl, 8 * w), prev_map), pl.BlockSpec((hl, 8 * w), next_map),
                  pl.BlockSpec((ts, 4 * w), lambda i: (i, 0)), pl.BlockSpec((hl, 4 * w), next_map),
                  full((8, w)), full((32, w)), full((8, w)), full((N_HEADS, SGU_CHUNK, SGU_CHUNK)),
                  full((SGU_CHUNK, w)), full((w, w))],
        out_specs=[pl.BlockSpec((ts, 8 * w), lambda i: (i, 0)), full((8, w)), full((32, w)),
                   full((N_HEADS, SGU_CHUNK, SGU_CHUNK)), full((SGU_CHUNK, SGU_CHUNK)), full((w, w))],
        scratch_shapes=[pltpu.VMEM((ts + 2 * hl, w), F32)] * 3,
        compiler_params=_params("arbitrary"),
    )(z, z, z, dy, dy, sconv, cconv, vecs, wt, bexp, pbd)


def _attn_head(q, kv_ref, hd, d):
    hw = d // N_HEADS
    qh = q[:, hd * hw:(hd + 1) * hw]
    kh = kv_ref[:, hd * hw:(hd + 1) * hw].astype(BF16)
    vh = kv_ref[:, d + hd * hw:d + (hd + 1) * hw].astype(BF16)
    s = _dot_nt(qh, kh) * (1.0 / (hw ** 0.5))
    e = jnp.exp(s - jnp.max(s, axis=-1, keepdims=True))
    p = e / jnp.sum(e, axis=-1, keepdims=True)
    return qh, kh, vh, p


def xattn_fwd(x, gain, kv, wq_g, wo_g, *, name, tm=None):
    t, d = x.shape
    nm = kv.shape[0]
    tm = _row_tile(t, tm)
    hw = d // N_HEADS

    def body(x_ref, g_ref, kv_ref, wq_ref, wo_ref, o_ref):
        xv = x_ref[...]
        xhat, _ = _rms_fwd(xv, None)
        h = (xhat * g_ref[...]).astype(BF16)
        q = _dot(h, _full_weight(wq_ref, "row")).astype(BF16)
        wo = _full_weight(wo_ref, "row")
        out = xv
        for hd in range(N_HEADS):
            _, _, vh, p = _attn_head(q, kv_ref, hd, d)
            oh = _dot(p.astype(BF16), vh).astype(BF16)
            out = out + _dot(oh, wo[hd * hw:(hd + 1) * hw])
        o_ref[...] = out

    row = lambda i: (i, 0)
    return pl.pallas_call(
        body, name=name, grid=(t // tm,),
        out_shape=jax.ShapeDtypeStruct((t, d), F32),
        in_specs=[pl.BlockSpec((tm, d), row), pl.BlockSpec((1, d), lambda i: (0, 0)),
                  pl.BlockSpec((nm, 2 * d), lambda i: (0, 0)), _wspec(wq_g), _wspec(wo_g)],
        out_specs=pl.BlockSpec((tm, d), row),
        compiler_params=_params("parallel"),
    )(x, gain.reshape(1, d), kv, wq_g, wo_g)


def xattn_bwd_rows(x, dxn, gain, kv, wq_g, wo_g, *, name, tm=None):
    t, d = x.shape
    nm = kv.shape[0]
    tm = _row_tile(t, tm)
    hw = d // N_HEADS

    def body(x_ref, dxn_ref, g_ref, kv_ref, wq_ref, wo_ref,
             dx_ref, h_ref, dq_ref, o_ref, dkv_ref, dg_ref):
        i = pl.program_id(0)

        @pl.when(i == 0)
        def _():
            dkv_ref[...] = jnp.zeros_like(dkv_ref)
            dg_ref[...] = jnp.zeros_like(dg_ref)
        g = g_ref[...]
        xhat, r = _rms_fwd(x_ref[...], None)
        h = (xhat * g).astype(BF16)
        h_ref[...] = h
        wq = _full_weight(wq_ref, "row")
        q = _dot(h, wq).astype(BF16)
        dxn = dxn_ref[...]
        do = _dot_nt(dxn.astype(BF16), _full_weight(wo_ref, "row")).astype(BF16)
        for hd in range(N_HEADS):
            cols = slice(hd * hw, (hd + 1) * hw)
            qh, kh, vh, p = _attn_head(q, kv_ref, hd, d)
            pb = p.astype(BF16)
            o_ref[:, cols] = _dot(pb, vh).astype(BF16)
            doh = do[:, cols]
            dkv_ref[:, d + hd * hw:d + (hd + 1) * hw] += _dot_tn(pb, doh)
            dp = _dot_nt(doh, vh)
            ds = (p * (dp - jnp.sum(dp * p, axis=-1, keepdims=True)) * (1.0 / (hw ** 0.5))).astype(BF16)
            dq_ref[:, cols] = _dot(ds, kh).astype(BF16)
            dkv_ref[:, cols] += _dot_tn(ds, qh)
        dh = _dot_nt(dq_ref[...], wq)
        dx, dg = _rms_bwd(xhat, r, g, dh)
        dx_ref[...] = dxn + dx
        dg_ref[...] += dg

    row = lambda i: (i, 0)
    fix = lambda i: (0, 0)
    return pl.pallas_call(
        body, name=name, grid=(t // tm,),
        out_shape=[jax.ShapeDtypeStruct((t, d), F32), jax.ShapeDtypeStruct((t, d), BF16),
                   jax.ShapeDtypeStruct((t, d), BF16), jax.ShapeDtypeStruct((t, d), BF16),
                   jax.ShapeDtypeStruct((nm, 2 * d), F32), jax.ShapeDtypeStruct((1, d), F32)],
        in_specs=[pl.BlockSpec((tm, d), row), pl.BlockSpec((tm, d), row), pl.BlockSpec((1, d), fix),
                  pl.BlockSpec((nm, 2 * d), fix), _wspec(wq_g), _wspec(wo_g)],
        out_specs=[pl.BlockSpec((tm, d), row)] * 4 + [pl.BlockSpec((nm, 2 * d), fix),
                                                      pl.BlockSpec((1, d), fix)],
        compiler_params=_params("arbitrary"),
    )(x, dxn, gain.reshape(1, d), kv, wq_g, wo_g)


def loss_head(x, target, gain, *, name, tm=None):
    t, d = x.shape
    tm = _row_tile(t, tm)

    def body(x_ref, t_ref, g_ref, dx_ref, dg_ref, loss_ref):
        @pl.when(pl.program_id(0) == 0)
        def _():
            dg_ref[...] = jnp.zeros_like(dg_ref)
            loss_ref[...] = jnp.zeros_like(loss_ref)
        g = g_ref[...]
        xhat, r = _rms_fwd(x_ref[...], None)
        err = xhat * g - t_ref[...]
        loss_ref[...] += 0.5 * jnp.sum(jnp.sum(err * err, axis=-1, keepdims=True) / d,
                                       axis=0, keepdims=True)
        dx, dg = _rms_bwd(xhat, r, g, err / d)
        dx_ref[...] = dx
        dg_ref[...] += dg

    row = lambda i: (i, 0)
    fix = lambda i: (0, 0)
    return pl.pallas_call(
        body, name=name, grid=(t // tm,),
        out_shape=[jax.ShapeDtypeStruct((t, d), F32), jax.ShapeDtypeStruct((1, d), F32),
                   jax.ShapeDtypeStruct((1, 1), F32)],
        in_specs=[pl.BlockSpec((tm, d), row), pl.BlockSpec((tm, d), row), pl.BlockSpec((1, d), fix)],
        out_specs=[pl.BlockSpec((tm, d), row), pl.BlockSpec((1, d), fix), pl.BlockSpec((1, 1), fix)],
        compiler_params=_params("arbitrary"),
    )(x, target, gain.reshape(1, d))


def _adamw_math(w, g, m, v):
    m = ADAM_B1 * m + (1.0 - ADAM_B1) * g
    v = ADAM_B2 * v + (1.0 - ADAM_B2) * (g * g)
    m_hat = m / (1.0 - ADAM_B1 ** ADAM_STEP)
    v_hat = v / (1.0 - ADAM_B2 ** ADAM_STEP)
    delta = -ADAM_LR * (m_hat / (jnp.sqrt(v_hat) + ADAM_EPS) + ADAM_WD * w)
    return delta, m, v


def adamw_sharded(own, lands, w, m, v, me_arr, *, name):
    nl, r, c = w.shape
    assert nl == len(own) == len(lands) == 2
    tr = next(cand for cand in (256, 176, 128, r) if r % cand == 0)
    nr = r // tr

    def body(me_ref, o0, o1, l0, l1, w_ref, m_ref, v_ref, g_out, d_out, m_out, v_out):
        def total(o_ref, l_ref):
            acc = o_ref[...].astype(F32)
            for p in range(N_DEV - 1):
                acc = acc + l_ref[p].astype(F32)
            return acc
        g = jnp.where(pl.program_id(0) == 0, total(o0, l0), total(o1, l1))
        delta, mn, vn = _adamw_math(w_ref[...], g, m_ref[...], v_ref[...])
        g_out[...] = g
        d_out[...] = delta
        m_out[...] = mn
        v_out[...] = vn

    row0 = lambda l, i: jnp.where(l == 0, i, nr - 1)
    row1 = lambda l, i: jnp.where(l == 1, i, 0)
    blk = pl.BlockSpec((None, tr, c), lambda l, i, me: (l, i, 0))
    grid_spec = pltpu.PrefetchScalarGridSpec(
        num_scalar_prefetch=1, grid=(nl, nr),
        in_specs=[pl.BlockSpec((None, tr, c), lambda l, i, me: (me[0], row0(l, i), 0)),
                  pl.BlockSpec((None, tr, c), lambda l, i, me: (me[0], row1(l, i), 0)),
                  pl.BlockSpec((N_DEV - 1, tr, c), lambda l, i, me: (0, row0(l, i), 0)),
                  pl.BlockSpec((N_DEV - 1, tr, c), lambda l, i, me: (0, row1(l, i), 0)),
                  blk, blk, blk],
        out_specs=[blk] * 4)
    return pl.pallas_call(
        body, name=name, grid_spec=grid_spec,
        out_shape=[jax.ShapeDtypeStruct((nl, r, c), F32)] * 4,
        compiler_params=_params("arbitrary", "arbitrary"),
    )(me_arr, own[0], own[1], lands[0], lands[1], w, m, v)


def adamw_flat(g, w, m, v, *, name):
    def body(g_ref, w_ref, m_ref, v_ref, d_out, m_out, v_out):
        delta, mn, vn = _adamw_math(w_ref[...], g_ref[...], m_ref[...], v_ref[...])
        d_out[...] = delta
        m_out[...] = mn
        v_out[...] = vn

    return pl.pallas_call(
        body, name=name, out_shape=[jax.ShapeDtypeStruct(w.shape, F32)] * 3,
        in_specs=[VMEM_SPEC] * 4, out_specs=[VMEM_SPEC] * 3,
        compiler_params=pltpu.CompilerParams(vmem_limit_bytes=VMEM_LIMIT),
    )(g, w, m, v)


def cast_into_slot(a, layer, me_arr, *, name):
    _, r, c = a.shape
    tr = next(cand for cand in (256, 176, 128, r) if r % cand == 0)

    def body(me_ref, a_ref, o_ref):
        o_ref[...] = a_ref[...].astype(BF16)

    grid_spec = pltpu.PrefetchScalarGridSpec(
        num_scalar_prefetch=1, grid=(r // tr,),
        in_specs=[pl.BlockSpec((None, tr, c), lambda i, me: (layer, i, 0))],
        out_specs=pl.BlockSpec((None, tr, c), lambda i, me: (me[0], i, 0)))
    return pl.pallas_call(
        body, name=name, grid_spec=grid_spec,
        out_shape=jax.ShapeDtypeStruct((N_DEV, r, c), BF16),
        compiler_params=_params("parallel"),
    )(me_arr, a)


def _pack(arrs, rows):
    flat = jnp.concatenate([a.reshape(-1).astype(F32) for a in arrs])
    pad = rows * 128 - flat.shape[0]
    assert pad >= 0
    if pad:
        flat = jnp.concatenate([flat, jnp.zeros((pad,), F32)])
    return flat.reshape(rows, 128)


def _unpack(packed, shapes):
    flat = packed.reshape(-1)
    out, pos = [], 0
    for s in shapes:
        n = 1
        for dim in s:
            n *= dim
        out.append(flat[pos:pos + n].reshape(s))
        pos += n
    return out


def _rows_for(shapes):
    n = 0
    for s in shapes:
        k = 1
        for dim in s:
            k *= dim
        n += k
    return -(-n // 1024) * 8


GATHER_GROUPS = (("ffn1", ("ffn1_w_in", "ffn1_w_out")),
                 ("mid", ("mix_w_in", "mix_w_out", "xattn_wkv", "xattn_wq", "xattn_wo")),
                 ("ffn2", ("ffn2_w_in", "ffn2_w_out")))
SMALL_REPL = ["norm_ffn1", "norm_mix", "sgu_norm_g", "sgu_w", "sgu_b", "cconv_ln_g", "cconv_ln_b",
              "pool_w", "pool_scale", "norm_xattn", "norm_mem", "norm_ffn2", "norm_final"]
SMALL_SHARD = ["sconv_w", "cconv_w"]
WEIGHTS = ["norm_ffn1", "ffn1_w_in", "ffn1_w_out", "norm_mix", "mix_w_in", "sconv_w", "sgu_norm_g",
           "sgu_w", "sgu_b", "cconv_w", "cconv_ln_g", "cconv_ln_b", "pool_w", "pool_scale", "mix_w_out",
           "norm_xattn", "norm_mem", "xattn_wq", "xattn_wkv", "xattn_wo", "norm_ffn2", "ffn2_w_in",
           "ffn2_w_out", "norm_final"]


def kernel(x, mem, norm_ffn1, ffn1_w_in, ffn1_w_out, norm_mix, mix_w_in, sconv_w, sgu_norm_g, sgu_w, sgu_b, cconv_w, cconv_ln_g, cconv_ln_b, pool_w, pool_scale, mix_w_out, norm_xattn, norm_mem, xattn_wq, xattn_wkv, xattn_wo, norm_ffn2, ffn2_w_in, ffn2_w_out, norm_final, loss_target, m_norm_ffn1, m_ffn1_w_in, m_ffn1_w_out, m_norm_mix, m_mix_w_in, m_sconv_w, m_sgu_norm_g, m_sgu_w, m_sgu_b, m_cconv_w, m_cconv_ln_g, m_cconv_ln_b, m_pool_w, m_pool_scale, m_mix_w_out, m_norm_xattn, m_norm_mem, m_xattn_wq, m_xattn_wkv, m_xattn_wo, m_norm_ffn2, m_ffn2_w_in, m_ffn2_w_out, m_norm_final, v_norm_ffn1, v_ffn1_w_in, v_ffn1_w_out, v_norm_mix, v_mix_w_in, v_sconv_w, v_sgu_norm_g, v_sgu_w, v_sgu_b, v_cconv_w, v_cconv_ln_g, v_cconv_ln_b, v_pool_w, v_pool_scale, v_mix_w_out, v_norm_xattn, v_norm_mem, v_xattn_wq, v_xattn_wkv, v_xattn_wo, v_norm_ffn2, v_ffn2_w_in, v_ffn2_w_out, v_norm_final):
    args = dict(locals())
    wts = {n: args[n] for n in WEIGHTS}
    mom = {n: args["m_" + n] for n in WEIGHTS}
    var = {n: args["v_" + n] for n in WEIGHTS}
    x0 = x[0]
    mem0 = mem[0]
    target = loss_target[0]
    t, d = x0.shape
    nl = norm_ffn1.shape[0]
    w = MIX_W
    me = _my_index()

    me_arr = jnp.reshape(me, (1,)).astype(jnp.int32)

    small_g = all_gather([sconv_w, cconv_w], name="gather_conv_taps")
    sconv_full = jnp.transpose(small_g[0], (1, 2, 0, 3)).reshape(nl, SCONV_K, w)
    cconv_full = jnp.transpose(small_g[1], (1, 2, 0, 3)).reshape(nl, CCONV_K, w)
    pending = {}
    token = small_g[1]
    for l in range(nl):
        for gname, members in GATHER_GROUPS:
            gs = [cast_into_slot(wts[n], l, me_arr, name=f"cast_{n}{l}") for n in members]
            send, recv, gs, token = gather_start(gs, token, name=f"gather_start_{gname}{l}")
            pending[gname, l] = (members, gs, send, recv)
    wg = [dict() for _ in range(nl)]

    def arrive(gname, l, after):
        members, gs, send, recv = pending.pop((gname, l))
        gs = gather_wait(gs, send, recv, after, name=f"gather_wait_{gname}{l}")
        gs = sibling_forward(gs, name=f"gather_forward_{gname}{l}")
        wg[l].update(zip(members, gs))
    sconv_pad = jnp.pad(sconv_full, ((0, 0), (0, 8 - SCONV_K), (0, 0)))
    cconv_pad = jnp.pad(cconv_full, ((0, 0), (0, 32 - CCONV_K), (0, 0)))
    zeros_w = jnp.zeros((nl, w), F32)
    vecs = jnp.stack([sgu_norm_g, cconv_ln_g, cconv_ln_b, pool_scale] + [zeros_w] * 4, axis=1)
    wt = jnp.tril(sgu_w).astype(BF16)
    bexp = jnp.repeat(jnp.swapaxes(sgu_b, 1, 2), w // N_HEADS, axis=2)
    eye = jnp.eye(4, dtype=F32)
    pbd = jnp.einsum("lgcd,gh->lgchd", pool_w, eye).reshape(nl, w, w).astype(BF16)

    def mixer_args(l):
        return sconv_pad[l], cconv_pad[l], vecs[l], wt[l], bexp[l], pbd[l]

    saved = []
    xc = x0
    after = token
    for l in range(nl):
        s = {"x_ffn1": xc}
        arrive("ffn1", l, after)
        xc = ffn_fwd(xc, norm_ffn1[l], wg[l]["ffn1_w_in"], wg[l]["ffn1_w_out"], name=f"ffn1_fwd{l}",
                     tm=FFN_FWD_TILE)
        s["x_mix"] = xc
        arrive("mid", l, xc)
        z = mm_rows(xc, wg[l]["mix_w_in"], "col", gain=norm_mix[l], name=f"mix_in{l}")
        y = mixer_fwd(z, *mixer_args(l), name=f"mixer_fwd{l}")
        s["z"], s["y"] = z, y
        xc = mm_rows(y, wg[l]["mix_w_out"], "row", residual=xc, name=f"mix_out{l}")
        s["x_att"] = xc
        kv = mm_rows(mem0, wg[l]["xattn_wkv"], "col", gain=norm_mem[l], name=f"kv{l}")
        s["kv"] = kv
        xc = xattn_fwd(xc, norm_xattn[l], kv, wg[l]["xattn_wq"], wg[l]["xattn_wo"], name=f"xattn_fwd{l}")
        s["x_ffn2"] = xc
        arrive("ffn2", l, xc)
        xc = ffn_fwd(xc, norm_ffn2[l], wg[l]["ffn2_w_in"], wg[l]["ffn2_w_out"], name=f"ffn2_fwd{l}",
                     tm=FFN_FWD_TILE)
        after = xc
        saved.append(s)

    dx, g_norm_final, loss_local = loss_head(xc, target, norm_final, name="loss_head")
    loss = lax.psum(loss_local[0, 0], ("x", "y", "c"))

    tm = _row_tile(t, TN_TILE)
    small ={n: [None] * nl for n in SMALL_REPL + SMALL_SHARD if n != "norm_final"}
    scattered = {}
    tie = [token]

    def send_grads(gname, l, grads):
        members = list(grads)
        send, recv, gs, lands, tie[0] = scatter_start(
            [grads[n] for n in members], tie[0], name=f"scatter_start_{gname}{l}")
        scattered[gname, l] = (members, gs, lands, send, recv)

    def tied(v):
        return v + tie[0][0, 0]

    for l in reversed(range(nl)):
        s = saved[l]
        wl = wg[l]
        bg = {}
        dxn = dx
        dx, dgn, bg["ffn2_w_in"], bg["ffn2_w_out"] = ffn_bwd(
            s["x_ffn2"], dxn, tied(norm_ffn2[l]), wl["ffn2_w_in"], wl["ffn2_w_out"], name=f"ffn2_bwd{l}")
        small["norm_ffn2"][l] = dgn[0]
        send_grads("ffn2", l, bg)

        bg = {}
        dxn = dx
        dx, h, dq, o, dkv, dgn = xattn_bwd_rows(
            s["x_att"], dxn, tied(norm_xattn[l]), s["kv"], wl["xattn_wq"], wl["xattn_wo"],
            name=f"xattn_bwd{l}")
        small["norm_xattn"][l] = dgn[0]
        row_spec = pl.BlockSpec((tm, d), lambda s_, i: (i, 0))
        bg["xattn_wq"] = mm_tn(h, dq, nb=1, ka=d, nbk=d, tm=tm, m=t, a_spec=row_spec, b_spec=row_spec,
                               name=f"dwq{l}").reshape(N_DEV, d // N_DEV, d)
        bg["xattn_wo"] = mm_tn(o, dxn, nb=1, ka=d, nbk=d, tm=tm, m=t, a_spec=row_spec, b_spec=row_spec,
                               name=f"dwo{l}").reshape(N_DEV, d // N_DEV, d)
        _, mhat, dgn = mm_nt(dkv, wl["xattn_wkv"], "col", x=mem0, gain=norm_mem[l], name=f"dmem{l}")
        small["norm_mem"][l] = dgn[0]
        nm = mem0.shape[0]
        bg["xattn_wkv"] = mm_tn(mhat, dkv, nb=N_DEV, ka=d, nbk=2 * d // N_DEV, tm=nm, m=nm,
                                a_spec=pl.BlockSpec((nm, d), lambda s_, i: (0, 0)),
                                b_spec=pl.BlockSpec((nm, 2 * d // N_DEV), lambda s_, i: (0, s_)),
                                name=f"dwkv{l}")
        send_grads("xattn", l, bg)

        bg = {}
        dxn = dx
        bg["mix_w_out"] = mm_tn(s["y"], dxn, nb=1, ka=d, nbk=d, tm=tm, m=t, a_spec=row_spec,
                                b_spec=row_spec, name=f"dwmo{l}").reshape(N_DEV, d // N_DEV, d)
        dy = mm_nt(dxn, wl["mix_w_out"], "row", name=f"dy_mix{l}")
        dz, gvec, gcc, gwt, gb, gpbd = mixer_bwd(s["z"], dy, *mixer_args(l), name=f"mixer_bwd{l}")
        small["sconv_w"][l] = gvec[0:SCONV_K]
        small["sgu_norm_g"][l] = gvec[3]
        small["cconv_ln_g"][l] = gvec[4]
        small["cconv_ln_b"][l] = gvec[5]
        small["pool_scale"][l] = gvec[6]
        small["cconv_w"][l] = gcc[0:CCONV_K]
        small["sgu_w"][l] = gwt
        small["sgu_b"][l] = jnp.transpose(gb[:, 0:N_HEADS])
        gw = w // 4
        small["pool_w"][l] = jnp.stack([gpbd[g * gw:(g + 1) * gw, g * gw:(g + 1) * gw] for g in range(4)])
        dx, h, dgn = mm_nt(dz, wl["mix_w_in"], "col", x=s["x_mix"], gain=tied(norm_mix[l]), dx_in=dxn,
                           name=f"dh_mix{l}")
        small["norm_mix"][l] = dgn[0]
        th = _row_tile(t, TN_TILE // 2)
        bg["mix_w_in"] = mm_tn(h, dz, nb=1, ka=d, nbk=N_DEV * w, tm=th, m=t, col_slots=N_DEV,
                               a_spec=pl.BlockSpec((th, d), lambda s_, i: (i, 0)),
                               b_spec=pl.BlockSpec((th, N_DEV * w), lambda s_, i: (i, 0)), name=f"dwmi{l}")
        send_grads("mix", l, bg)

        bg = {}
        dx, dgn, bg["ffn1_w_in"], bg["ffn1_w_out"] = ffn_bwd(
            s["x_ffn1"], dx, tied(norm_ffn1[l]), wl["ffn1_w_in"], wl["ffn1_w_out"], name=f"ffn1_bwd{l}")
        small["norm_ffn1"][l] = dgn[0]
        send_grads("ffn1", l, bg)

    small_full = {n: jnp.stack(v) for n, v in small.items()}
    small_full["norm_final"] = g_norm_final[0]
    names = SMALL_REPL + SMALL_SHARD
    shapes = [small_full[n].shape for n in names]
    rows_all = _rows_for(shapes)
    packed = tied(_pack([small_full[n] for n in names], rows_all))

    out = {}

    def finish(gname, after):
        own, land = {}, {}
        for l in reversed(range(nl)):
            members, gs, lands, send, recv = scattered.pop((gname, l))
            gs, lands = scatter_wait(gs, lands, send, recv, after, name=f"scatter_wait_{gname}{l}")
            for n, g_, l_ in zip(members, gs, lands):
                own.setdefault(n, {})[l] = g_
                land.setdefault(n, {})[l] = l_
        for n in own:
            out[n] = adamw_sharded([own[n][l] for l in range(nl)], [land[n][l] for l in range(nl)],
                                   wts[n], mom[n], var[n], me_arr, name="adamw_" + n)
            after = out[n][1]
        return after

    after = packed
    for gname in ("ffn2", "xattn", "mix"):
        after = finish(gname, after)
    summed = all_reduce_small(packed, after, name="reduce_small")
    gsm = dict(zip(names, _unpack(summed, shapes)))
    finish("ffn1", summed)
    repl_shapes = [wts[n].shape for n in SMALL_REPL]
    rows_r = _rows_for(repl_shapes)
    dl, mn, vn = adamw_flat(_pack([gsm[n] for n in SMALL_REPL], rows_r),
                            _pack([wts[n] for n in SMALL_REPL], rows_r),
                            _pack([mom[n] for n in SMALL_REPL], rows_r),
                            _pack([var[n] for n in SMALL_REPL], rows_r), name="adamw_small")
    for n, a, b, c in zip(SMALL_REPL, _unpack(dl, repl_shapes), _unpack(mn, repl_shapes),
                          _unpack(vn, repl_shapes)):
        out[n] = (gsm[n], a, b, c)
    cs = w // N_DEV
    gsh = {n: lax.dynamic_slice_in_dim(gsm[n], me * cs, cs, axis=2) for n in SMALL_SHARD}
    sh_shapes = [wts[n].shape for n in SMALL_SHARD]
    rows_s = _rows_for(sh_shapes)
    dl, mn, vn = adamw_flat(_pack([gsh[n] for n in SMALL_SHARD], rows_s),
                            _pack([wts[n] for n in SMALL_SHARD], rows_s),
                            _pack([mom[n] for n in SMALL_SHARD], rows_s),
                            _pack([var[n] for n in SMALL_SHARD], rows_s), name="adamw_small_sharded")
    for n, a, b, c in zip(SMALL_SHARD, _unpack(dl, sh_shapes), _unpack(mn, sh_shapes),
                          _unpack(vn, sh_shapes)):
        out[n] = (gsh[n], a, b, c)

    grad_x = dx.reshape(1, t, d)
    return (loss, grad_x, *[out[n][0] for n in WEIGHTS], *[out[n][1] for n in WEIGHTS],
            *[out[n][2] for n in WEIGHTS], *[out[n][3] for n in WEIGHTS])
```

```python
import functools

import jax
import jax.numpy as jnp
from jax import lax
from jax.experimental import pallas as pl
from jax.experimental.pallas import tpu as pltpu

F32 = jnp.float32
BF16 = jnp.bfloat16
MESH = pl.DeviceIdType.MESH
N_DEV = 8
EPS = 1e-6
HALO = 32
SGU_CHUNK = 128
CCONV_K = 31
SCONV_K = 3
MIX_W = 256
N_HEADS = 4
VMEM_LIMIT = 56 * 1024 * 1024
ROW_TILE = 512
TN_TILE = 2048
FFN_FWD_TILE = 1024
MIX_TILE = 512

ADAM_LR = 0.001
ADAM_B1 = 0.9
ADAM_B2 = 0.999
ADAM_EPS = 1e-08
ADAM_WD = 0.01
ADAM_STEP = 10

HBM_SPEC = pl.BlockSpec(memory_space=pltpu.HBM)
VMEM_SPEC = pl.BlockSpec(memory_space=pltpu.VMEM)


def _params(*sem):
    return pltpu.CompilerParams(dimension_semantics=tuple(sem), vmem_limit_bytes=VMEM_LIMIT)


def _row_tile(m, pref=None):
    t = min(m, ROW_TILE if pref is None else pref)
    assert m % t == 0, (m, t)
    return t


def _my_index():
    return lax.axis_index("x") * 4 + lax.axis_index("y") * 2 + lax.axis_index("c")


def _peer(mask):
    x, y, c = lax.axis_index("x"), lax.axis_index("y"), lax.axis_index("c")
    px = 1 - x if mask & 4 else x
    py = 1 - y if mask & 2 else y
    pc = 1 - c if mask & 1 else c
    return (px, py, pc), px * 4 + py * 2 + pc


def all_gather(arrs, name):
    n = len(arrs)

    def body(*refs):
        ins, outs = refs[:n], refs[n:2 * n]
        send_sems, recv_sems, loc_sems = refs[2 * n:]
        me = _my_index()
        local = []
        for i in range(n):
            cp = pltpu.make_async_copy(ins[i], outs[i].at[me], loc_sems.at[i])
            cp.start()
            local.append(cp)
        sends = []
        for i in range(n):
            for m in range(1, N_DEV):
                peer, _ = _peer(m)
                cp = pltpu.make_async_remote_copy(
                    src_ref=ins[i], dst_ref=outs[i].at[me],
                    send_sem=send_sems.at[i, m - 1], recv_sem=recv_sems.at[i, m - 1],
                    device_id=peer, device_id_type=MESH)
                cp.start()
                sends.append(cp)
        for i in range(n):
            for m in range(1, N_DEV):
                peer, pidx = _peer(m)
                pltpu.make_async_remote_copy(
                    src_ref=ins[i], dst_ref=outs[i].at[pidx],
                    send_sem=send_sems.at[i, m - 1], recv_sem=recv_sems.at[i, m - 1],
                    device_id=peer, device_id_type=MESH).wait_recv()
        for cp in sends:
            cp.wait_send()
        for cp in local:
            cp.wait()

    return pl.pallas_call(
        body, name=name,
        out_shape=[jax.ShapeDtypeStruct((N_DEV,) + a.shape, a.dtype) for a in arrs],
        in_specs=[HBM_SPEC] * n, out_specs=[HBM_SPEC] * n,
        scratch_shapes=[pltpu.SemaphoreType.DMA((n, N_DEV - 1)),
                        pltpu.SemaphoreType.DMA((n, N_DEV - 1)),
                        pltpu.SemaphoreType.DMA((n,))],
    )(*arrs)


SEM_SPEC = pl.BlockSpec(memory_space=pltpu.SEMAPHORE)
ANY_SPEC = pl.BlockSpec(memory_space=pl.ANY)
SIDE_EFFECT = pltpu.SideEffectType.DATAFLOW_SIDE_EFFECTING


def _hbm(a):
    return pltpu.with_memory_space_constraint(a, pltpu.HBM)


def _sem_pairs(n):
    return (pltpu.SemaphoreType.DMA((n * (N_DEV - 1),)), pltpu.SemaphoreType.DMA((n * (N_DEV - 1),)))


def _sem(i, m):
    return i * (N_DEV - 1) + m - 1


def _gather_copy(g_ref, i, m, send_sems, recv_sems, origin):
    peer, _ = _peer(m)
    return pltpu.make_async_remote_copy(
        src_ref=g_ref.at[origin], dst_ref=g_ref.at[origin],
        send_sem=send_sems.at[_sem(i, m)], recv_sem=recv_sems.at[_sem(i, m)],
        device_id=peer, device_id_type=MESH)


GATHER_MASKS = (1, 2, 4, 6)
FORWARD_MASKS = (2, 4, 6)


ALL_MASKS = tuple(range(1, N_DEV))


def gather_start(gs, after, name, masks=GATHER_MASKS):
    n = len(gs)

    def body(*refs):
        g_in = refs[:n]
        send_sems, recv_sems = refs[n + 1], refs[n + 2]
        token = refs[-1]
        me = _my_index()
        for i in range(n):
            for m in masks:
                _gather_copy(g_in[i], i, m, send_sems, recv_sems, me).start()
        token[...] = jnp.zeros_like(token)

    outs = pl.pallas_call(
        body, name=name,
        out_shape=(*_sem_pairs(n), *[pltpu.HBM(g.shape, g.dtype) for g in gs],
                   jax.ShapeDtypeStruct((8, 128), F32)),
        in_specs=[HBM_SPEC] * n + [ANY_SPEC],
        out_specs=(SEM_SPEC, SEM_SPEC, *[HBM_SPEC] * n, VMEM_SPEC),
        input_output_aliases={i: 2 + i for i in range(n)},
        compiler_params=pltpu.CompilerParams(has_side_effects=SIDE_EFFECT),
    )(*[_hbm(g) for g in gs], after)
    return outs[0], outs[1], list(outs[2:2 + n]), outs[-1]


def gather_wait(gs, send_sems, recv_sems, after, name, masks=GATHER_MASKS):
    n = len(gs)

    def body(*refs):
        g_in = refs[:n]
        send, recv = refs[n], refs[n + 1]
        me = _my_index()
        for i in range(n):
            for m in masks:
                _, pidx = _peer(m)
                _gather_copy(g_in[i], i, m, send, recv, me).wait_send()
                _gather_copy(g_in[i], i, m, send, recv, pidx).wait_recv()

    outs = pl.pallas_call(
        body, name=name,
        out_shape=[pltpu.HBM(g.shape, g.dtype) for g in gs],
        in_specs=[HBM_SPEC] * n + [SEM_SPEC, SEM_SPEC, ANY_SPEC],
        out_specs=[HBM_SPEC] * n,
        input_output_aliases={i: i for i in range(n)},
        compiler_params=pltpu.CompilerParams(has_side_effects=SIDE_EFFECT),
    )(*gs, send_sems, recv_sems, after)
    return list(outs)


def sibling_forward(gs, name):
    n = len(gs)
    nf = len(FORWARD_MASKS)

    def body(*refs):
        g_in = refs[:n]
        send_sems, recv_sems = refs[2 * n:]
        x, y, c = lax.axis_index("x"), lax.axis_index("y"), lax.axis_index("c")
        sibling = (x, y, 1 - c)

        def copy(i, k, origin):
            return pltpu.make_async_remote_copy(
                src_ref=g_in[i].at[origin], dst_ref=g_in[i].at[origin],
                send_sem=send_sems.at[i * nf + k], recv_sem=recv_sems.at[i * nf + k],
                device_id=sibling, device_id_type=MESH)
        sends = []
        for i in range(n):
            for k, m in enumerate(FORWARD_MASKS):
                _, origin = _peer(m)
                cp = copy(i, k, origin)
                cp.start()
                sends.append(cp)
        for i in range(n):
            for k, m in enumerate(FORWARD_MASKS):
                _, origin = _peer(m ^ 1)
                copy(i, k, origin).wait_recv()
        for cp in sends:
            cp.wait_send()

    outs = pl.pallas_call(
        body, name=name,
        out_shape=[jax.ShapeDtypeStruct(g.shape, g.dtype) for g in gs],
        in_specs=[HBM_SPEC] * n, out_specs=[HBM_SPEC] * n,
        input_output_aliases={i: i for i in range(n)},
        scratch_shapes=[pltpu.SemaphoreType.DMA((n * nf,)), pltpu.SemaphoreType.DMA((n * nf,))],
    )(*gs)
    return list(outs)


def _scatter_copy(g_ref, l_ref, i, m, send_sems, recv_sems):
    peer, pidx = _peer(m)
    return pltpu.make_async_remote_copy(
        src_ref=g_ref.at[pidx], dst_ref=l_ref.at[m - 1],
        send_sem=send_sems.at[_sem(i, m)], recv_sem=recv_sems.at[_sem(i, m)],
        device_id=peer, device_id_type=MESH)


def scatter_start(grads, after, name):
    n = len(grads)
    lands = [lax.empty((N_DEV - 1,) + g.shape[1:], g.dtype) for g in grads]

    def body(*refs):
        g_in, l_in = refs[:n], refs[n:2 * n]
        send_sems, recv_sems = refs[2 * n + 1], refs[2 * n + 2]
        token = refs[-1]
        for i in range(n):
            for m in range(1, N_DEV):
                _scatter_copy(g_in[i], l_in[i], i, m, send_sems, recv_sems).start()
        token[...] = jnp.zeros_like(token)

    outs = pl.pallas_call(
        body, name=name,
        out_shape=(*_sem_pairs(n), *[pltpu.HBM(g.shape, g.dtype) for g in grads],
                   *[pltpu.HBM(l.shape, l.dtype) for l in lands], jax.ShapeDtypeStruct((8, 128), F32)),
        in_specs=[HBM_SPEC] * (2 * n) + [ANY_SPEC],
        out_specs=(SEM_SPEC, SEM_SPEC, *[HBM_SPEC] * (2 * n), VMEM_SPEC),
        input_output_aliases={i: 2 + i for i in range(2 * n)},
        compiler_params=pltpu.CompilerParams(has_side_effects=SIDE_EFFECT),
    )(*[_hbm(g) for g in grads], *[_hbm(l) for l in lands], after)
    return outs[0], outs[1], list(outs[2:2 + n]), list(outs[2 + n:2 + 2 * n]), outs[-1]


def scatter_wait(grads, lands, send_sems, recv_sems, after, name):
    n = len(grads)

    def body(*refs):
        g_in, l_in = refs[:n], refs[n:2 * n]
        send, recv = refs[2 * n], refs[2 * n + 1]
        for i in range(n):
            for m in range(1, N_DEV):
                cp = _scatter_copy(g_in[i], l_in[i], i, m, send, recv)
                cp.wait_send()
                cp.wait_recv()

    outs = pl.pallas_call(
        body, name=name,
        out_shape=[pltpu.HBM(a.shape, a.dtype) for a in list(grads) + list(lands)],
        in_specs=[HBM_SPEC] * (2 * n) + [SEM_SPEC, SEM_SPEC, ANY_SPEC],
        out_specs=[HBM_SPEC] * (2 * n),
        input_output_aliases={i: i for i in range(2 * n)},
        compiler_params=pltpu.CompilerParams(has_side_effects=SIDE_EFFECT),
    )(*grads, *lands, send_sems, recv_sems, after)
    return list(outs[:n]), list(outs[n:])


def sum_slots(g, name):
    _, r, c = g.shape

    def body(g_ref, out_ref):
        acc = g_ref[0]
        for p in range(1, N_DEV):
            acc = acc + g_ref[p]
        out_ref[...] = acc

    return pl.pallas_call(
        body, name=name, out_shape=jax.ShapeDtypeStruct((r, c), F32),
        in_specs=[VMEM_SPEC], out_specs=VMEM_SPEC,
        compiler_params=pltpu.CompilerParams(vmem_limit_bytes=VMEM_LIMIT),
    )(g)


def _sigmoid(v):
    return 1.0 / (1.0 + jnp.exp(-v))


def _rms_fwd(xf, g):
    r = lax.rsqrt(jnp.mean(xf * xf, axis=-1, keepdims=True) + EPS)
    return xf * r, r


def _rms_bwd(xhat, r, g, dy):
    dg = jnp.sum(dy * xhat, axis=0, keepdims=True)
    dxh = dy * g
    dx = r * (dxh - xhat * jnp.mean(dxh * xhat, axis=-1, keepdims=True))
    return dx, dg


def _ln_stats(v):
    mu = jnp.mean(v, axis=-1, keepdims=True)
    vc = v - mu
    r = lax.rsqrt(jnp.mean(vc * vc, axis=-1, keepdims=True) + EPS)
    return vc * r, r


def _ln_bwd(xhat, r, dxh):
    return r * (dxh - jnp.mean(dxh, axis=-1, keepdims=True)
                - xhat * jnp.mean(dxh * xhat, axis=-1, keepdims=True))


def _dot(a, b):
    return jnp.dot(a, b, preferred_element_type=F32)


def _dot_nt(a, b):
    return lax.dot_general(a, b, (((1,), (1,)), ((), ())), preferred_element_type=F32)


def _dot_tn(a, b):
    return lax.dot_general(a, b, (((0,), (0,)), ((), ())), preferred_element_type=F32)


def _full_weight(w_ref, kind):
    assert kind == "row"
    p, a, b = w_ref.shape
    return w_ref[...].reshape(p * a, b)


def _wspec(wg):
    return pl.BlockSpec(wg.shape, lambda *_: (0, 0, 0))


def mm_rows(a, wg, kind, *, gain=None, residual=None, out_dtype=F32, name, tm=None):
    m, k = a.shape
    p, wa, wb = wg.shape
    n = p * wb if kind == "col" else wb
    tm = _row_tile(m, tm)
    has_gain, has_res = gain is not None, residual is not None

    def body(*refs):
        refs = list(refs)
        a_ref = refs.pop(0)
        g_ref = refs.pop(0) if has_gain else None
        w_ref = refs.pop(0)
        r_ref = refs.pop(0) if has_res else None
        o_ref = refs.pop(0)
        if has_gain:
            xhat, _ = _rms_fwd(a_ref[...].astype(F32), None)
            h = (xhat * g_ref[...]).astype(BF16)
        else:
            h = a_ref[...].astype(BF16)
        if kind == "col":
            for j in range(p):
                o = _dot(h, w_ref[j])
                if has_res:
                    o = o + r_ref[:, j * wb:(j + 1) * wb]
                o_ref[:, j * wb:(j + 1) * wb] = o.astype(out_dtype)
        else:
            o = _dot(h, _full_weight(w_ref, "row"))
            if has_res:
                o = o + r_ref[...]
            o_ref[...] = o.astype(out_dtype)

    operands = [a]
    in_specs = [pl.BlockSpec((tm, k), lambda i: (i, 0))]
    if has_gain:
        operands.append(gain.reshape(1, k))
        in_specs.append(pl.BlockSpec((1, k), lambda i: (0, 0)))
    operands.append(wg)
    in_specs.append(_wspec(wg))
    if has_res:
        operands.append(residual)
        in_specs.append(pl.BlockSpec((tm, n), lambda i: (i, 0)))
    return pl.pallas_call(
        body, name=name, grid=(m // tm,),
        out_shape=jax.ShapeDtypeStruct((m, n), out_dtype),
        in_specs=in_specs, out_specs=pl.BlockSpec((tm, n), lambda i: (i, 0)),
        compiler_params=_params("parallel"),
    )(*operands)


def mm_nt(dz, wg, kind, *, x=None, gain=None, dx_in=None, name, tm=None):
    m, n = dz.shape
    p, wa, wb = wg.shape
    k = wa if kind == "col" else p * wa
    tm = _row_tile(m, tm)
    epi = x is not None
    has_dx = dx_in is not None

    def body(*refs):
        refs = list(refs)
        dz_ref, w_ref = refs.pop(0), refs.pop(0)
        if epi:
            x_ref, g_ref = refs.pop(0), refs.pop(0)
            dxi_ref = refs.pop(0) if has_dx else None
            dx_ref, h_ref, dg_ref = refs
        else:
            (da_ref,) = refs
        dzb = dz_ref[...].astype(BF16)
        if kind == "col":
            da = _dot_nt(dzb[:, 0:wb], w_ref[0])
            for j in range(1, p):
                da = da + _dot_nt(dzb[:, j * wb:(j + 1) * wb], w_ref[j])
        else:
            da = _dot_nt(dzb, _full_weight(w_ref, "row"))
        if not epi:
            da_ref[...] = da
            return
        g = g_ref[...]
        xhat, r = _rms_fwd(x_ref[...].astype(F32), None)
        h_ref[...] = (xhat * g).astype(BF16)
        dx, dg = _rms_bwd(xhat, r, g, da)
        if has_dx:
            dx = dx + dxi_ref[...]
        dx_ref[...] = dx

        @pl.when(pl.program_id(0) == 0)
        def _():
            dg_ref[...] = jnp.zeros_like(dg_ref)
        dg_ref[...] += dg

    row = lambda i: (i, 0)
    operands = [dz, wg]
    in_specs = [pl.BlockSpec((tm, n), row), _wspec(wg)]
    if epi:
        operands += [x, gain.reshape(1, k)]
        in_specs += [pl.BlockSpec((tm, k), row), pl.BlockSpec((1, k), lambda i: (0, 0))]
        if has_dx:
            operands.append(dx_in)
            in_specs.append(pl.BlockSpec((tm, k), row))
        out_shape = [jax.ShapeDtypeStruct((m, k), F32), jax.ShapeDtypeStruct((m, k), BF16),
                     jax.ShapeDtypeStruct((1, k), F32)]
        out_specs = [pl.BlockSpec((tm, k), row), pl.BlockSpec((tm, k), row),
                     pl.BlockSpec((1, k), lambda i: (0, 0))]
    else:
        out_shape = jax.ShapeDtypeStruct((m, k), F32)
        out_specs = pl.BlockSpec((tm, k), row)
    return pl.pallas_call(
        body, name=name, grid=(m // tm,), out_shape=out_shape,
        in_specs=in_specs, out_specs=out_specs,
        compiler_params=_params("arbitrary"),
    )(*operands)


def mm_tn(a, b, *, nb, a_spec, b_spec, ka, nbk, tm, m, scale=1.0, out_dtype=BF16, col_slots=1,
          after=None, name):
    ni = m // tm
    assert col_slots == 1 or nb == 1
    cw = nbk // col_slots
    extra = [] if after is None else [after]

    def body(a_ref, b_ref, *rest):
        o_ref, acc = rest[len(extra):]
        i = pl.program_id(1)

        @pl.when(i == 0)
        def _():
            acc[...] = jnp.zeros_like(acc)
        acc[...] += _dot_tn(a_ref[...].astype(BF16), b_ref[...].astype(BF16))

        @pl.when(i == ni - 1)
        def _():
            if col_slots == 1:
                o_ref[...] = (acc[...] * scale).astype(out_dtype)
            else:
                for j in range(col_slots):
                    o_ref[j] = (acc[:, j * cw:(j + 1) * cw] * scale).astype(out_dtype)

    if col_slots == 1:
        out_shape = jax.ShapeDtypeStruct((nb, ka, nbk), out_dtype)
        out_spec = pl.BlockSpec((None, ka, nbk), lambda s, i: (s, 0, 0))
    else:
        out_shape = jax.ShapeDtypeStruct((col_slots, ka, cw), out_dtype)
        out_spec = pl.BlockSpec((col_slots, ka, cw), lambda s, i: (0, 0, 0))
    return pl.pallas_call(
        body, name=name, grid=(nb, ni), out_shape=out_shape,
        in_specs=[a_spec, b_spec] + [ANY_SPEC] * len(extra), out_specs=out_spec,
        scratch_shapes=[pltpu.VMEM((ka, nbk), F32)],
        compiler_params=_params("parallel", "arbitrary"),
    )(a, b, *extra)


def _ffn_specs(w_in_g, w_out_g, d):
    nf = w_in_g.shape[2]
    hr = w_out_g.shape[1]
    assert 2 * hr == nf
    w_in5 = w_in_g.reshape(2, 4, d, nf)
    w_out5 = w_out_g.reshape(4, 2, hr, d)
    in_spec = pl.BlockSpec((2, None, d, nf), lambda i, j: (0, j, 0, 0))
    out_spec = pl.BlockSpec((None, 2, hr, d), lambda i, j: (j, 0, 0, 0))
    return w_in5, w_out5, in_spec, out_spec, nf


def ffn_fwd(x, gain, w_in_g, w_out_g, *, name, tm=None):
    t, d = x.shape
    tm = _row_tile(t, tm)
    w_in5, w_out5, wi_spec, wo_spec, nf = _ffn_specs(w_in_g, w_out_g, d)

    def body(x_ref, g_ref, wi_ref, wo_ref, o_ref, gu_ref, h_scr, acc):
        j = pl.program_id(1)

        @pl.when(j == 0)
        def _():
            xhat, _ = _rms_fwd(x_ref[...], None)
            h_scr[...] = (xhat * g_ref[...]).astype(BF16)
            acc[...] = jnp.zeros_like(acc)
        h = h_scr[...]
        gt = _dot(h, wi_ref[0])
        up = _dot(h, wi_ref[1])
        gu_ref[0] = gt.astype(BF16)
        gu_ref[1] = up.astype(BF16)
        act = (gt * _sigmoid(gt) * up).astype(BF16)
        acc[...] += _dot(act, wo_ref[...].reshape(nf, d))

        @pl.when(j == 3)
        def _():
            o_ref[...] = x_ref[...] + 0.5 * acc[...]

    return pl.pallas_call(
        body, name=name, grid=(t // tm, 4),
        out_shape=[jax.ShapeDtypeStruct((t, d), F32), jax.ShapeDtypeStruct((2, 4, t, nf), BF16)],
        in_specs=[pl.BlockSpec((tm, d), lambda i, j: (i, 0)),
                  pl.BlockSpec((1, d), lambda i, j: (0, 0)), wi_spec, wo_spec],
        out_specs=[pl.BlockSpec((tm, d), lambda i, j: (i, 0)),
                   pl.BlockSpec((2, None, tm, nf), lambda i, j: (0, j, i, 0))],
        scratch_shapes=[pltpu.VMEM((tm, d), BF16), pltpu.VMEM((tm, d), F32)],
        compiler_params=_params("parallel", "arbitrary"),
    )(x, gain.reshape(1, d), w_in5, w_out5)


def ffn_bwd_rows(x, dy, gu, gain, w_in_g, w_out_g, *, name, tm=None):
    t, d = x.shape
    tm = _row_tile(t, tm)
    w_in5, w_out5, wi_spec, wo_spec, nf = _ffn_specs(w_in_g, w_out_g, d)

    def body(x_ref, dy_ref, gu_ref, g_ref, wi_ref, wo_ref, dx_ref, h_ref, act_ref, dgu_ref, dg_ref,
             dh_acc, dyh_scr):
        i, j = pl.program_id(0), pl.program_id(1)

        @pl.when(j == 0)
        def _():
            xhat, _ = _rms_fwd(x_ref[...], None)
            h_ref[...] = (xhat * g_ref[...]).astype(BF16)
            dyh_scr[...] = (0.5 * dy_ref[...]).astype(BF16)
            dh_acc[...] = jnp.zeros_like(dh_acc)
        gt = gu_ref[0].astype(F32)
        up = gu_ref[1].astype(F32)
        sg = _sigmoid(gt)
        silu = gt * sg
        act_ref[...] = (silu * up).astype(BF16)
        dact = _dot_nt(dyh_scr[...], wo_ref[...].reshape(nf, d))
        dgt = (dact * up * (sg * (1.0 + gt * (1.0 - sg)))).astype(BF16)
        dup = (dact * silu).astype(BF16)
        dgu_ref[0] = dgt
        dgu_ref[1] = dup
        dh_acc[...] += _dot_nt(dgt, wi_ref[0]) + _dot_nt(dup, wi_ref[1])

        @pl.when(j == 3)
        def _():
            g = g_ref[...]
            xhat, r = _rms_fwd(x_ref[...], None)
            dx, dg = _rms_bwd(xhat, r, g, dh_acc[...])
            dx_ref[...] = dy_ref[...] + dx

            @pl.when(i == 0)
            def _():
                dg_ref[...] = jnp.zeros_like(dg_ref)
            dg_ref[...] += dg

    row = lambda i, j: (i, 0)
    return pl.pallas_call(
        body, name=name, grid=(t // tm, 4),
        out_shape=[jax.ShapeDtypeStruct((t, d), F32), jax.ShapeDtypeStruct((t, d), BF16),
                   jax.ShapeDtypeStruct((4, t, nf), BF16), jax.ShapeDtypeStruct((2, 4, t, nf), BF16),
                   jax.ShapeDtypeStruct((1, d), F32)],
        in_specs=[pl.BlockSpec((tm, d), row), pl.BlockSpec((tm, d), row),
                  pl.BlockSpec((2, None, tm, nf), lambda i, j: (0, j, i, 0)),
                  pl.BlockSpec((1, d), lambda i, j: (0, 0)), wi_spec, wo_spec],
        out_specs=[pl.BlockSpec((tm, d), row), pl.BlockSpec((tm, d), row),
                   pl.BlockSpec((None, tm, nf), lambda i, j: (j, i, 0)),
                   pl.BlockSpec((2, None, tm, nf), lambda i, j: (0, j, i, 0)),
                   pl.BlockSpec((1, d), lambda i, j: (0, 0))],
        scratch_shapes=[pltpu.VMEM((tm, d), F32), pltpu.VMEM((tm, d), BF16)],
        compiler_params=_params("arbitrary", "arbitrary"),
    )(x, dy, gu, gain.reshape(1, d), w_in5, w_out5)


def ffn_grad_w_in(h, dgu, after, *, name):
    t, d = h.shape
    nf = dgu.shape[-1]
    tm = _row_tile(t, TN_TILE)
    return mm_tn(h, dgu.reshape(8, t, nf), nb=8, ka=d, nbk=nf, tm=tm, m=t, after=after,
                 a_spec=pl.BlockSpec((tm, d), lambda s, i: (i, 0)),
                 b_spec=pl.BlockSpec((None, tm, nf), lambda s, i: (s, i, 0)), name=name)


def ffn_grad_w_out(act, dy, after, *, name):
    _, t, nf = act.shape
    d = dy.shape[1]
    tm = _row_tile(t, TN_TILE)
    d_w_out = mm_tn(act, dy, nb=4, ka=nf, nbk=d, tm=tm, m=t, scale=0.5, after=after,
                    a_spec=pl.BlockSpec((None, tm, nf), lambda s, i: (s, i, 0)),
                    b_spec=pl.BlockSpec((tm, d), lambda s, i: (i, 0)), name=name)
    return d_w_out.reshape(8, nf // 2, d)


def _lane_group(shape):
    return lax.shift_right_logical(lax.broadcasted_iota(jnp.int32, shape, 1), 6)


def _pool_count(t0, rows):
    t = (t0 + lax.broadcasted_iota(jnp.int32, (rows, MIX_W), 0) + 1).astype(F32)
    return jnp.minimum(t, _by_group(_lane_group((rows, MIX_W)), 2.0, 4.0, 8.0, 16.0))


def _by_group(grp, v0, v1, v2, v3):
    return jnp.where(grp == 0, v0, jnp.where(grp == 1, v1, jnp.where(grp == 2, v2, v3)))


def _sgu_mix(wt_ref, vnc):
    grp = _lane_group((SGU_CHUNK, MIX_W))
    out = jnp.zeros((SGU_CHUNK, MIX_W), F32)
    for hd in range(N_HEADS):
        out = jnp.where(grp == hd, _dot(wt_ref[hd], vnc), out)
    return out


def _pool_fwd(s1, s2, s3, t0, ts, lo):
    h = lo
    s2[h - 24:h + ts] = s1[h - 24:h + ts] + s1[h - 25:h + ts - 1]
    s3[h - 16:h + ts] = s2[h - 16:h + ts] + s2[h - 18:h + ts - 2]
    sum2 = s2[h:h + ts]
    sum4 = s3[h:h + ts]
    s2[h - 8:h + ts] = s3[h - 8:h + ts] + s3[h - 12:h + ts - 4]
    sum8 = s2[h:h + ts]
    sum16 = sum8 + s2[h - 8:h + ts - 8]
    grp = _lane_group((ts, MIX_W))
    return _by_group(grp, sum2, sum4, sum8, sum16) / _pool_count(t0, ts) - s1[h:h + ts]


def mixer_fwd(z, sconv, cconv, vecs, wt, bexp, pbd, *, name, ts=None):
    t = z.shape[0]
    ts = _row_tile(t, MIX_TILE if ts is None else ts)
    hl = HALO
    w = MIX_W
    nch = ts // SGU_CHUNK

    def body(zc, zp, sconv_ref, cconv_ref, vec_ref, wt_ref, bexp_ref, pbd_ref, y_ref, s1, s2, s3):
        i = pl.program_id(0)
        has_prev = i > 0

        def col(ref, c):
            return ref[:, c * w:(c + 1) * w]

        def prev(c):
            return jnp.where(has_prev, col(zp, c), 0.0)

        s1[0:hl] = prev(1) * prev(2)
        s1[hl:hl + ts] = col(zc, 1) * col(zc, 2)
        cv = sconv_ref[0:1] * s1[hl - 2:hl - 2 + ts]
        for k in range(1, SCONV_K):
            cv = cv + sconv_ref[k:k + 1] * s1[hl - 2 + k:hl - 2 + k + ts]
        y_ref[:, 0:w] = (col(zc, 0) * cv).astype(BF16)

        xhat, _ = _ln_stats(col(zc, 4))
        vn = (xhat * vec_ref[0:1]).astype(BF16)
        for c in range(nch):
            rows = slice(c * SGU_CHUNK, (c + 1) * SGU_CHUNK)
            mixed = _sgu_mix(wt_ref, vn[rows]) + bexp_ref[...]
            y_ref[rows, w:2 * w] = (zc[rows, 3 * w:4 * w] * mixed).astype(BF16)

        s1[0:hl] = prev(5) * _sigmoid(prev(6))
        s1[hl:hl + ts] = col(zc, 5) * _sigmoid(col(zc, 6))
        off = hl - (CCONV_K - 1)
        cv = cconv_ref[0:1] * s1[off:off + ts]
        for k in range(1, CCONV_K):
            cv = cv + cconv_ref[k:k + 1] * s1[off + k:off + k + ts]
        xhat, _ = _ln_stats(cv)
        ln = xhat * vec_ref[1:2] + vec_ref[2:3]
        y_ref[:, 2 * w:3 * w] = (ln * _sigmoid(ln)).astype(BF16)

        s1[0:hl] = prev(7)
        s1[hl:hl + ts] = col(zc, 7)
        pooled = _pool_fwd(s1, s2, s3, i * ts, ts, hl)
        y_ref[:, 3 * w:4 * w] = (_dot(pooled.astype(BF16), pbd_ref[...]) * vec_ref[3:4]).astype(BF16)

    full = lambda shape: pl.BlockSpec(shape, lambda i: (0,) * len(shape))
    return pl.pallas_call(
        body, name=name, grid=(t // ts,),
        out_shape=jax.ShapeDtypeStruct((t, 4 * w), BF16),
        in_specs=[pl.BlockSpec((ts, 8 * w), lambda i: (i, 0)),
                  pl.BlockSpec((hl, 8 * w), lambda i: (jnp.maximum(i * (ts // hl) - 1, 0), 0)),
                  full((8, w)), full((32, w)), full((8, w)), full((N_HEADS, SGU_CHUNK, SGU_CHUNK)),
                  full((SGU_CHUNK, w)), full((w, w))],
        out_specs=pl.BlockSpec((ts, 4 * w), lambda i: (i, 0)),
        scratch_shapes=[pltpu.VMEM((hl + ts, w), F32)] * 3,
        compiler_params=_params("parallel"),
    )(z, z, sconv, cconv, vecs, wt, bexp, pbd)


def mixer_bwd(z, dy, sconv, cconv, vecs, wt, bexp, pbd, *, name, ts=None):
    t = z.shape[0]
    ts = _row_tile(t, MIX_TILE if ts is None else ts)
    hl = HALO
    w = MIX_W
    nch = ts // SGU_CHUNK
    ni = t // ts
    ext = ts + hl

    def body(zc, zp, zn, dyc, dyn, sconv_ref, cconv_ref, vec_ref, wt_ref, bexp_ref, pbd_ref,
             dz_ref, gvec_ref, gcc_ref, gwt_ref, gb_ref, gpbd_ref, s1, s2, s3):
        i = pl.program_id(0)
        has_prev = i > 0
        has_next = i < ni - 1

        @pl.when(i == 0)
        def _():
            gvec_ref[...] = jnp.zeros_like(gvec_ref)
            gcc_ref[...] = jnp.zeros_like(gcc_ref)
            gwt_ref[...] = jnp.zeros_like(gwt_ref)
            gb_ref[...] = jnp.zeros_like(gb_ref)
            gpbd_ref[...] = jnp.zeros_like(gpbd_ref)

        def col(ref, c):
            return ref[:, c * w:(c + 1) * w]

        def prev(c):
            return jnp.where(has_prev, col(zp, c), 0.0)

        def nxt(c):
            return jnp.where(has_next, col(zn, c), 0.0)

        def dnext(c):
            return jnp.where(has_next, col(dyn, c), 0.0)

        def rowsum(v):
            return jnp.sum(v, axis=0, keepdims=True)

        s1[0:hl] = prev(1) * prev(2)
        s1[hl:hl + ts] = col(zc, 1) * col(zc, 2)
        s1[hl + ts:hl + ts + hl] = nxt(1) * nxt(2)
        cv = sconv_ref[0:1] * s1[hl - 2:hl - 2 + ts]
        for k in range(1, SCONV_K):
            cv = cv + sconv_ref[k:k + 1] * s1[hl - 2 + k:hl - 2 + k + ts]
        dya = col(dyc, 0)
        dz_ref[:, 0:w] = (dya * cv).astype(BF16)
        s2[0:ts] = dya * col(zc, 0)
        s2[ts:ext] = dnext(0) * nxt(0)
        dv = sconv_ref[0:1] * s2[2:2 + ts]
        for k in range(1, SCONV_K):
            dv = dv + sconv_ref[k:k + 1] * s2[2 - k:2 - k + ts]
        dz_ref[:, w:2 * w] = (dv * col(zc, 2)).astype(BF16)
        dz_ref[:, 2 * w:3 * w] = (dv * col(zc, 1)).astype(BF16)
        dcv = s2[0:ts]
        for k in range(SCONV_K):
            gvec_ref[k:k + 1] += rowsum(dcv * s1[hl - 2 + k:hl - 2 + k + ts])

        g_sgu = vec_ref[0:1]
        xhat, rstd = _ln_stats(col(zc, 4))
        vn = (xhat * g_sgu).astype(BF16)
        grp = _lane_group((SGU_CHUNK, w))
        lane = lax.broadcasted_iota(jnp.int32, (SGU_CHUNK, SGU_CHUNK), 1)
        tril = lax.broadcasted_iota(jnp.int32, (SGU_CHUNK, SGU_CHUNK), 0) >= lane
        for c in range(nch):
            rows = slice(c * SGU_CHUNK, (c + 1) * SGU_CHUNK)
            vnc = vn[rows]
            mixed = _sgu_mix(wt_ref, vnc) + bexp_ref[...]
            dyb = dyc[rows, w:2 * w]
            dz_ref[rows, 3 * w:4 * w] = (dyb * mixed).astype(BF16)
            dmix = dyb * zc[rows, 3 * w:4 * w]
            dmixb = dmix.astype(BF16)
            dvn = jnp.zeros((SGU_CHUNK, w), F32)
            gb = jnp.zeros((SGU_CHUNK, SGU_CHUNK), F32)
            for hd in range(N_HEADS):
                dvn = jnp.where(grp == hd, _dot_tn(wt_ref[hd], dmixb), dvn)
                dm_h = jnp.where(grp == hd, dmix, 0.0)
                gwt_ref[hd] += jnp.where(tril, _dot_nt(dm_h.astype(BF16), vnc), 0.0)
                gb = gb + jnp.where(lane == hd, jnp.sum(dm_h, axis=1, keepdims=True), 0.0)
            gb_ref[...] += gb
            s3[rows] = dvn
        dvn = s3[0:ts]
        gvec_ref[3:4] += rowsum(dvn * xhat)
        dz_ref[:, 4 * w:5 * w] = _ln_bwd(xhat, rstd, dvn * g_sgu).astype(BF16)

        sig_c = _sigmoid(col(zc, 6))
        s1[0:hl] = prev(5) * _sigmoid(prev(6))
        s1[hl:hl + ts] = col(zc, 5) * sig_c
        s1[hl + ts:hl + ts + hl] = nxt(5) * _sigmoid(nxt(6))
        off = hl - (CCONV_K - 1)
        cv = cconv_ref[0:1] * s1[off:off + ext]
        for k in range(1, CCONV_K):
            cv = cv + cconv_ref[k:k + 1] * s1[off + k:off + k + ext]
        xhat, rstd = _ln_stats(cv)
        ln = xhat * vec_ref[1:2] + vec_ref[2:3]
        sg = _sigmoid(ln)
        s2[0:ts] = col(dyc, 2)
        s2[ts:ext] = dnext(2)
        dln = s2[0:ext] * (sg * (1.0 + ln * (1.0 - sg)))
        gvec_ref[4:5] += rowsum(dln[0:ts] * xhat[0:ts])
        gvec_ref[5:6] += rowsum(dln[0:ts])
        s3[0:ext] = _ln_bwd(xhat, rstd, dln * vec_ref[1:2])
        dyg = cconv_ref[0:1] * s3[CCONV_K - 1:CCONV_K - 1 + ts]
        for k in range(1, CCONV_K):
            dyg = dyg + cconv_ref[k:k + 1] * s3[CCONV_K - 1 - k:CCONV_K - 1 - k + ts]
        dz_ref[:, 5 * w:6 * w] = (dyg * sig_c).astype(BF16)
        dz_ref[:, 6 * w:7 * w] = (dyg * col(zc, 5) * sig_c * (1.0 - sig_c)).astype(BF16)
        dcv = s3[0:ts]
        for k in range(CCONV_K):
            gcc_ref[k:k + 1] += rowsum(dcv * s1[off + k:off + k + ts])

        scale = vec_ref[3:4]
        s1[0:hl] = prev(7)
        s1[hl:hl + ts] = col(zc, 7)
        pooled = _pool_fwd(s1, s2, s3, i * ts, ts, hl).astype(BF16)
        q0 = _dot(pooled, pbd_ref[...])
        dyd = col(dyc, 3)
        gvec_ref[6:7] += rowsum(dyd * q0)
        dq = (dyd * scale).astype(BF16)
        gpbd_ref[...] += _dot_tn(pooled, dq)
        s1[0:ts] = _dot_nt(dq, pbd_ref[...])
        s1[ts:ext] = _dot_nt((dnext(3) * scale).astype(BF16), pbd_ref[...])
        dpool = s1[0:ts]
        s2[0:ext] = s1[0:ext] / _pool_count(i * ts, ext)
        s3[0:ts + 24] = s2[0:ts + 24] + s2[1:ts + 25]
        f2 = s3[0:ts]
        s2[0:ts + 16] = s3[0:ts + 16] + s3[2:ts + 18]
        f4 = s2[0:ts]
        s3[0:ts + 8] = s2[0:ts + 8] + s2[4:ts + 12]
        f8 = s3[0:ts]
        f16 = f8 + s3[8:ts + 8]
        dz_ref[:, 7 * w:8 * w] = (_by_group(_lane_group((ts, w)), f2, f4, f8, f16) - dpool).astype(BF16)

    full = lambda shape: pl.BlockSpec(shape, lambda i: (0,) * len(shape))
    r = ts // hl
    prev_map = lambda i: (jnp.maximum(i * r - 1, 0), 0)
    next_map = lambda i: (jnp.minimum((i + 1) * r, t // hl - 1), 0)
    return pl.pallas_call(
        body, name=name, grid=(ni,),
        out_shape=[jax.ShapeDtypeStruct((t, 8 * w), BF16), jax.ShapeDtypeStruct((8, w), F32),
                   jax.ShapeDtypeStruct((32, w), F32),
                   jax.ShapeDtypeStruct((N_HEADS, SGU_CHUNK, SGU_CHUNK), F32),
                   jax.ShapeDtypeStruct((SGU_CHUNK, SGU_CHUNK), F32), jax.ShapeDtypeStruct((w, w), F32)],
        in_specs=[pl.BlockSpec((ts, 8 * w), lambda i: (i, 0)),
                  pl.BlockSpec((hl, 8 * w), prev_map), pl.BlockSpec((hl, 8 * w), next_map),
                  pl.BlockSpec((ts, 4 * w), lambda i: (i, 0)), pl.BlockSpec((hl, 4 * w), next_map),
                  full((8, w)), full((32, w)), full((8, w)), full((N_HEADS, SGU_CHUNK, SGU_CHUNK)),
                  full((SGU_CHUNK, w)), full((w, w))],
        out_specs=[pl.BlockSpec((ts, 8 * w), lambda i: (i, 0)), full((8, w)), full((32, w)),
                   full((N_HEADS, SGU_CHUNK, SGU_CHUNK)), full((SGU_CHUNK, SGU_CHUNK)), full((w, w))],
        scratch_shapes=[pltpu.VMEM((ts + 2 * hl, w), F32)] * 3,
        compiler_params=_params("arbitrary"),
    )(z, z, z, dy, dy, sconv, cconv, vecs, wt, bexp, pbd)


def _attn_head(q, kv_ref, hd, d):
    hw = d // N_HEADS
    qh = q[:, hd * hw:(hd + 1) * hw]
    kh = kv_ref[:, hd * hw:(hd + 1) * hw].astype(BF16)
    vh = kv_ref[:, d + hd * hw:d + (hd + 1) * hw].astype(BF16)
    s = _dot_nt(qh, kh) * (1.0 / (hw ** 0.5))
    e = jnp.exp(s - jnp.max(s, axis=-1, keepdims=True))
    p = e / jnp.sum(e, axis=-1, keepdims=True)
    return qh, kh, vh, p


def xattn_fwd(x, gain, kv, wq_g, wo_g, *, name, tm=None):
    t, d = x.shape
    nm = kv.shape[0]
    tm = _row_tile(t, tm)
    hw = d // N_HEADS

    def body(x_ref, g_ref, kv_ref, wq_ref, wo_ref, o_ref):
        xv = x_ref[...]
        xhat, _ = _rms_fwd(xv, None)
        h = (xhat * g_ref[...]).astype(BF16)
        q = _dot(h, _full_weight(wq_ref, "row")).astype(BF16)
        wo = _full_weight(wo_ref, "row")
        out = xv
        for hd in range(N_HEADS):
            _, _, vh, p = _attn_head(q, kv_ref, hd, d)
            oh = _dot(p.astype(BF16), vh).astype(BF16)
            out = out + _dot(oh, wo[hd * hw:(hd + 1) * hw])
        o_ref[...] = out

    row = lambda i: (i, 0)
    return pl.pallas_call(
        body, name=name, grid=(t // tm,),
        out_shape=jax.ShapeDtypeStruct((t, d), F32),
        in_specs=[pl.BlockSpec((tm, d), row), pl.BlockSpec((1, d), lambda i: (0, 0)),
                  pl.BlockSpec((nm, 2 * d), lambda i: (0, 0)), _wspec(wq_g), _wspec(wo_g)],
        out_specs=pl.BlockSpec((tm, d), row),
        compiler_params=_params("parallel"),
    )(x, gain.reshape(1, d), kv, wq_g, wo_g)


def xattn_bwd_rows(x, dxn, gain, kv, wq_g, wo_g, *, name, tm=None):
    t, d = x.shape
    nm = kv.shape[0]
    tm = _row_tile(t, tm)
    hw = d // N_HEADS

    def body(x_ref, dxn_ref, g_ref, kv_ref, wq_ref, wo_ref,
             dx_ref, h_ref, dq_ref, o_ref, dkv_ref, dg_ref):
        i = pl.program_id(0)

        @pl.when(i == 0)
        def _():
            dkv_ref[...] = jnp.zeros_like(dkv_ref)
            dg_ref[...] = jnp.zeros_like(dg_ref)
        g = g_ref[...]
        xhat, r = _rms_fwd(x_ref[...], None)
        h = (xhat * g).astype(BF16)
        h_ref[...] = h
        wq = _full_weight(wq_ref, "row")
        q = _dot(h, wq).astype(BF16)
        dxn = dxn_ref[...]
        do = _dot_nt(dxn.astype(BF16), _full_weight(wo_ref, "row")).astype(BF16)
        for hd in range(N_HEADS):
            cols = slice(hd * hw, (hd + 1) * hw)
            qh, kh, vh, p = _attn_head(q, kv_ref, hd, d)
            pb = p.astype(BF16)
            o_ref[:, cols] = _dot(pb, vh).astype(BF16)
            doh = do[:, cols]
            dkv_ref[:, d + hd * hw:d + (hd + 1) * hw] += _dot_tn(pb, doh)
            dp = _dot_nt(doh, vh)
            ds = (p * (dp - jnp.sum(dp * p, axis=-1, keepdims=True)) * (1.0 / (hw ** 0.5))).astype(BF16)
            dq_ref[:, cols] = _dot(ds, kh).astype(BF16)
            dkv_ref[:, cols] += _dot_tn(ds, qh)
        dh = _dot_nt(dq_ref[...], wq)
        dx, dg = _rms_bwd(xhat, r, g, dh)
        dx_ref[...] = dxn + dx
        dg_ref[...] += dg

    row = lambda i: (i, 0)
    fix = lambda i: (0, 0)
    return pl.pallas_call(
        body, name=name, grid=(t // tm,),
        out_shape=[jax.ShapeDtypeStruct((t, d), F32), jax.ShapeDtypeStruct((t, d), BF16),
                   jax.ShapeDtypeStruct((t, d), BF16), jax.ShapeDtypeStruct((t, d), BF16),
                   jax.ShapeDtypeStruct((nm, 2 * d), F32), jax.ShapeDtypeStruct((1, d), F32)],
        in_specs=[pl.BlockSpec((tm, d), row), pl.BlockSpec((tm, d), row), pl.BlockSpec((1, d), fix),
                  pl.BlockSpec((nm, 2 * d), fix), _wspec(wq_g), _wspec(wo_g)],
        out_specs=[pl.BlockSpec((tm, d), row)] * 4 + [pl.BlockSpec((nm, 2 * d), fix),
                                                      pl.BlockSpec((1, d), fix)],
        compiler_params=_params("arbitrary"),
    )(x, dxn, gain.reshape(1, d), kv, wq_g, wo_g)


def loss_head(x, target, gain, *, name, tm=None):
    t, d = x.shape
    tm = _row_tile(t, tm)

    def body(x_ref, t_ref, g_ref, dx_ref, dg_ref, loss_ref):
        @pl.when(pl.program_id(0) == 0)
        def _():
            dg_ref[...] = jnp.zeros_like(dg_ref)
            loss_ref[...] = jnp.zeros_like(loss_ref)
        g = g_ref[...]
        xhat, r = _rms_fwd(x_ref[...], None)
        err = xhat * g - t_ref[...]
        loss_ref[...] += 0.5 * jnp.sum(jnp.sum(err * err, axis=-1, keepdims=True) / d,
                                       axis=0, keepdims=True)
        dx, dg = _rms_bwd(xhat, r, g, err / d)
        dx_ref[...] = dx
        dg_ref[...] += dg

    row = lambda i: (i, 0)
    fix = lambda i: (0, 0)
    return pl.pallas_call(
        body, name=name, grid=(t // tm,),
        out_shape=[jax.ShapeDtypeStruct((t, d), F32), jax.ShapeDtypeStruct((1, d), F32),
                   jax.ShapeDtypeStruct((1, 1), F32)],
        in_specs=[pl.BlockSpec((tm, d), row), pl.BlockSpec((tm, d), row), pl.BlockSpec((1, d), fix)],
        out_specs=[pl.BlockSpec((tm, d), row), pl.BlockSpec((1, d), fix), pl.BlockSpec((1, 1), fix)],
        compiler_params=_params("arbitrary"),
    )(x, target, gain.reshape(1, d))


def _adamw_math(w, g, m, v):
    m = ADAM_B1 * m + (1.0 - ADAM_B1) * g
    v = ADAM_B2 * v + (1.0 - ADAM_B2) * (g * g)
    m_hat = m / (1.0 - ADAM_B1 ** ADAM_STEP)
    v_hat = v / (1.0 - ADAM_B2 ** ADAM_STEP)
    delta = -ADAM_LR * (m_hat / (jnp.sqrt(v_hat) + ADAM_EPS) + ADAM_WD * w)
    return delta, m, v


def adamw_sharded(own, lands, w, m, v, me_arr, *, name):
    nl, r, c = w.shape
    assert nl == len(own) == len(lands) == 2
    tr = next(cand for cand in (256, 176, 128, r) if r % cand == 0)
    nr = r // tr

    def body(me_ref, o0, o1, l0, l1, w_ref, m_ref, v_ref, g_out, d_out, m_out, v_out):
        def total(o_ref, l_ref):
            acc = o_ref[...].astype(F32)
            for p in range(N_DEV - 1):
                acc = acc + l_ref[p].astype(F32)
            return acc
        g = jnp.where(pl.program_id(0) == 0, total(o0, l0), total(o1, l1))
        delta, mn, vn = _adamw_math(w_ref[...], g, m_ref[...], v_ref[...])
        g_out[...] = g
        d_out[...] = delta
        m_out[...] = mn
        v_out[...] = vn

    row0 = lambda l, i: jnp.where(l == 0, i, nr - 1)
    row1 = lambda l, i: jnp.where(l == 1, i, 0)
    blk = pl.BlockSpec((None, tr, c), lambda l, i, me: (l, i, 0))
    grid_spec = pltpu.PrefetchScalarGridSpec(
        num_scalar_prefetch=1, grid=(nl, nr),
        in_specs=[pl.BlockSpec((None, tr, c), lambda l, i, me: (me[0], row0(l, i), 0)),
                  pl.BlockSpec((None, tr, c), lambda l, i, me: (me[0], row1(l, i), 0)),
                  pl.BlockSpec((N_DEV - 1, tr, c), lambda l, i, me: (0, row0(l, i), 0)),
                  pl.BlockSpec((N_DEV - 1, tr, c), lambda l, i, me: (0, row1(l, i), 0)),
                  blk, blk, blk],
        out_specs=[blk] * 4)
    return pl.pallas_call(
        body, name=name, grid_spec=grid_spec,
        out_shape=[jax.ShapeDtypeStruct((nl, r, c), F32)] * 4,
        compiler_params=_params("arbitrary", "arbitrary"),
    )(me_arr, own[0], own[1], lands[0], lands[1], w, m, v)


def adamw_flat(g, w, m, v, *, name):
    def body(g_ref, w_ref, m_ref, v_ref, d_out, m_out, v_out):
        delta, mn, vn = _adamw_math(w_ref[...], g_ref[...], m_ref[...], v_ref[...])
        d_out[...] = delta
        m_out[...] = mn
        v_out[...] = vn

    return pl.pallas_call(
        body, name=name, out_shape=[jax.ShapeDtypeStruct(w.shape, F32)] * 3,
        in_specs=[VMEM_SPEC] * 4, out_specs=[VMEM_SPEC] * 3,
        compiler_params=pltpu.CompilerParams(vmem_limit_bytes=VMEM_LIMIT),
    )(g, w, m, v)


def cast_into_slot(a, layer, me_arr, *, name, dtype=None):
    dtype = BF16 if dtype is None else dtype
    _, r, c = a.shape
    tr = next(cand for cand in (256, 176, 128, r) if r % cand == 0)

    def body(me_ref, a_ref, o_ref):
        o_ref[...] = a_ref[...].astype(dtype)

    grid_spec = pltpu.PrefetchScalarGridSpec(
        num_scalar_prefetch=1, grid=(r // tr,),
        in_specs=[pl.BlockSpec((None, tr, c), lambda i, me: (layer, i, 0))],
        out_specs=pl.BlockSpec((None, tr, c), lambda i, me: (me[0], i, 0)))
    return pl.pallas_call(
        body, name=name, grid_spec=grid_spec,
        out_shape=jax.ShapeDtypeStruct((N_DEV, r, c), dtype),
        compiler_params=_params("parallel"),
    )(me_arr, a)


def _pack(arrs, rows):
    flat = jnp.concatenate([a.reshape(-1).astype(F32) for a in arrs])
    pad = rows * 128 - flat.shape[0]
    assert pad >= 0
    if pad:
        flat = jnp.concatenate([flat, jnp.zeros((pad,), F32)])
    return flat.reshape(rows, 128)


def _unpack(packed, shapes):
    flat = packed.reshape(-1)
    out, pos = [], 0
    for s in shapes:
        n = 1
        for dim in s:
            n *= dim
        out.append(flat[pos:pos + n].reshape(s))
        pos += n
    return out


def _rows_for(shapes):
    n = 0
    for s in shapes:
        k = 1
        for dim in s:
            k *= dim
        n += k
    return -(-n // 1024) * 8


GATHER_GROUPS = (("ffn1", ("ffn1_w_in", "ffn1_w_out")),
                 ("mid", ("mix_w_in", "mix_w_out", "xattn_wkv", "xattn_wq", "xattn_wo")),
                 ("ffn2", ("ffn2_w_in", "ffn2_w_out")))
SMALL_REPL = ["norm_ffn1", "norm_mix", "sgu_norm_g", "sgu_w", "sgu_b", "cconv_ln_g", "cconv_ln_b",
              "pool_w", "pool_scale", "norm_xattn", "norm_mem", "norm_ffn2", "norm_final"]
SMALL_SHARD = ["sconv_w", "cconv_w"]
WEIGHTS = ["norm_ffn1", "ffn1_w_in", "ffn1_w_out", "norm_mix", "mix_w_in", "sconv_w", "sgu_norm_g",
           "sgu_w", "sgu_b", "cconv_w", "cconv_ln_g", "cconv_ln_b", "pool_w", "pool_scale", "mix_w_out",
           "norm_xattn", "norm_mem", "xattn_wq", "xattn_wkv", "xattn_wo", "norm_ffn2", "ffn2_w_in",
           "ffn2_w_out", "norm_final"]


def kernel(x, mem, norm_ffn1, ffn1_w_in, ffn1_w_out, norm_mix, mix_w_in, sconv_w, sgu_norm_g, sgu_w, sgu_b, cconv_w, cconv_ln_g, cconv_ln_b, pool_w, pool_scale, mix_w_out, norm_xattn, norm_mem, xattn_wq, xattn_wkv, xattn_wo, norm_ffn2, ffn2_w_in, ffn2_w_out, norm_final, loss_target, m_norm_ffn1, m_ffn1_w_in, m_ffn1_w_out, m_norm_mix, m_mix_w_in, m_sconv_w, m_sgu_norm_g, m_sgu_w, m_sgu_b, m_cconv_w, m_cconv_ln_g, m_cconv_ln_b, m_pool_w, m_pool_scale, m_mix_w_out, m_norm_xattn, m_norm_mem, m_xattn_wq, m_xattn_wkv, m_xattn_wo, m_norm_ffn2, m_ffn2_w_in, m_ffn2_w_out, m_norm_final, v_norm_ffn1, v_ffn1_w_in, v_ffn1_w_out, v_norm_mix, v_mix_w_in, v_sconv_w, v_sgu_norm_g, v_sgu_w, v_sgu_b, v_cconv_w, v_cconv_ln_g, v_cconv_ln_b, v_pool_w, v_pool_scale, v_mix_w_out, v_norm_xattn, v_norm_mem, v_xattn_wq, v_xattn_wkv, v_xattn_wo, v_norm_ffn2, v_ffn2_w_in, v_ffn2_w_out, v_norm_final):
    args = dict(locals())
    wts = {n: args[n] for n in WEIGHTS}
    mom = {n: args["m_" + n] for n in WEIGHTS}
    var = {n: args["v_" + n] for n in WEIGHTS}
    x0 = x[0]
    mem0 = mem[0]
    target = loss_target[0]
    t, d = x0.shape
    nl = norm_ffn1.shape[0]
    w = MIX_W
    me = _my_index()

    me_arr = jnp.reshape(me, (1,)).astype(jnp.int32)

    small_g = all_gather([sconv_w, cconv_w], name="gather_conv_taps")
    sconv_full = jnp.transpose(small_g[0], (1, 2, 0, 3)).reshape(nl, SCONV_K, w)
    cconv_full = jnp.transpose(small_g[1], (1, 2, 0, 3)).reshape(nl, CCONV_K, w)
    pending = {}
    token = small_g[1]
    for l in range(nl):
        for gname, members in GATHER_GROUPS:
            gs = [cast_into_slot(wts[n], l, me_arr, name=f"cast_{n}{l}") for n in members]
            send, recv, gs, token = gather_start(gs, token, name=f"gather_start_{gname}{l}")
            pending[gname, l] = (members, gs, send, recv)
    wg = [dict() for _ in range(nl)]

    def arrive(gname, l, after):
        members, gs, send, recv = pending.pop((gname, l))
        gs = gather_wait(gs, send, recv, after, name=f"gather_wait_{gname}{l}")
        gs = sibling_forward(gs, name=f"gather_forward_{gname}{l}")
        wg[l].update(zip(members, gs))
    sconv_pad = jnp.pad(sconv_full, ((0, 0), (0, 8 - SCONV_K), (0, 0)))
    cconv_pad = jnp.pad(cconv_full, ((0, 0), (0, 32 - CCONV_K), (0, 0)))
    zeros_w = jnp.zeros((nl, w), F32)
    vecs = jnp.stack([sgu_norm_g, cconv_ln_g, cconv_ln_b, pool_scale] + [zeros_w] * 4, axis=1)
    wt = jnp.tril(sgu_w).astype(BF16)
    bexp = jnp.repeat(jnp.swapaxes(sgu_b, 1, 2), w // N_HEADS, axis=2)
    eye = jnp.eye(4, dtype=F32)
    pbd = jnp.einsum("lgcd,gh->lgchd", pool_w, eye).reshape(nl, w, w).astype(BF16)

    def mixer_args(l):
        return sconv_pad[l], cconv_pad[l], vecs[l], wt[l], bexp[l], pbd[l]

    saved = []
    xc = x0
    after = token
    for l in range(nl):
        s = {"x_ffn1": xc}
        arrive("ffn1", l, after)
        xc, s["gu_ffn1"] = ffn_fwd(xc, norm_ffn1[l], wg[l]["ffn1_w_in"], wg[l]["ffn1_w_out"],
                                   name=f"ffn1_fwd{l}", tm=FFN_FWD_TILE)
        s["x_mix"] = xc
        arrive("mid", l, xc)
        z = mm_rows(xc, wg[l]["mix_w_in"], "col", gain=norm_mix[l], name=f"mix_in{l}")
        y = mixer_fwd(z, *mixer_args(l), name=f"mixer_fwd{l}")
        s["z"], s["y"] = z, y
        xc = mm_rows(y, wg[l]["mix_w_out"], "row", residual=xc, name=f"mix_out{l}")
        s["x_att"] = xc
        kv = mm_rows(mem0, wg[l]["xattn_wkv"], "col", gain=norm_mem[l], name=f"kv{l}")
        s["kv"] = kv
        xc = xattn_fwd(xc, norm_xattn[l], kv, wg[l]["xattn_wq"], wg[l]["xattn_wo"], name=f"xattn_fwd{l}")
        s["x_ffn2"] = xc
        arrive("ffn2", l, xc)
        xc, s["gu_ffn2"] = ffn_fwd(xc, norm_ffn2[l], wg[l]["ffn2_w_in"], wg[l]["ffn2_w_out"],
                                   name=f"ffn2_fwd{l}", tm=FFN_FWD_TILE)
        after = xc
        saved.append(s)

    dx, g_norm_final, loss_local = loss_head(xc, target, norm_final, name="loss_head")
    loss = lax.psum(loss_local[0, 0], ("x", "y", "c"))

    tm = _row_tile(t, TN_TILE)
    small ={n: [None] * nl for n in SMALL_REPL + SMALL_SHARD if n != "norm_final"}
    scattered = {}
    tie = [token]

    def send_grads(gname, l, grads):
        members = list(grads)
        send, recv, gs, lands, tie[0] = scatter_start(
            [grads[n] for n in members], tie[0], name=f"scatter_start_{gname}{l}")
        scattered[gname, l] = (members, gs, lands, send, recv)

    def tied(v):
        return v + tie[0][0, 0]

    names = SMALL_REPL + SMALL_SHARD
    small_pending = []

    def start_small():
        small_full = {n: jnp.stack(v) for n, v in small.items()}
        small_full["norm_final"] = g_norm_final[0]
        shapes = [small_full[n].shape for n in names]
        packed = _pack([small_full[n] for n in names], _rows_for(shapes))
        slot = cast_into_slot(packed[None], 0, me_arr, name="small_into_slot", dtype=F32)
        send, recv, gs, tie[0] = gather_start([slot], tie[0], name="small_gather_start", masks=ALL_MASKS)
        small_pending.append((gs, send, recv, shapes))

    def ffn_backward(which, l, x_in, dy, gu, gain):
        w_in, w_out = wg[l][which + "_w_in"], wg[l][which + "_w_out"]
        dx_, h_, act, dgu, dgn = ffn_bwd_rows(x_in, dy, gu, tied(gain), w_in, w_out,
                                              name=f"{which}_bwd{l}_rows")
        small["norm_" + which][l] = dgn[0]
        if which == "ffn1" and l == 0:
            start_small()
        send_grads(which + "_in", l,
                   {which + "_w_in": ffn_grad_w_in(h_, dgu, tie[0], name=f"{which}_bwd{l}_dwin")})
        send_grads(which + "_out", l,
                   {which + "_w_out": ffn_grad_w_out(act, dy, tie[0], name=f"{which}_bwd{l}_dwout")})
        return dx_

    for l in reversed(range(nl)):
        s = saved[l]
        wl = wg[l]
        dx = ffn_backward("ffn2", l, s["x_ffn2"], dx, s["gu_ffn2"], norm_ffn2[l])

        bg = {}
        dxn = dx
        dx, h, dq, o, dkv, dgn = xattn_bwd_rows(
            s["x_att"], dxn, tied(norm_xattn[l]), s["kv"], wl["xattn_wq"], wl["xattn_wo"],
            name=f"xattn_bwd{l}")
        small["norm_xattn"][l] = dgn[0]
        row_spec = pl.BlockSpec((tm, d), lambda s_, i: (i, 0))
        bg["xattn_wq"] = mm_tn(h, dq, nb=1, ka=d, nbk=d, tm=tm, m=t, a_spec=row_spec, b_spec=row_spec,
                               name=f"dwq{l}").reshape(N_DEV, d // N_DEV, d)
        bg["xattn_wo"] = mm_tn(o, dxn, nb=1, ka=d, nbk=d, tm=tm, m=t, a_spec=row_spec, b_spec=row_spec,
                               name=f"dwo{l}").reshape(N_DEV, d // N_DEV, d)
        _, mhat, dgn = mm_nt(dkv, wl["xattn_wkv"], "col", x=mem0, gain=norm_mem[l], name=f"dmem{l}")
        small["norm_mem"][l] = dgn[0]
        nm = mem0.shape[0]
        bg["xattn_wkv"] = mm_tn(mhat, dkv, nb=N_DEV, ka=d, nbk=2 * d // N_DEV, tm=nm, m=nm,
                                a_spec=pl.BlockSpec((nm, d), lambda s_, i: (0, 0)),
                                b_spec=pl.BlockSpec((nm, 2 * d // N_DEV), lambda s_, i: (0, s_)),
                                name=f"dwkv{l}")
        send_grads("xattn", l, bg)

        bg = {}
        dxn = dx
        bg["mix_w_out"] = mm_tn(s["y"], dxn, nb=1, ka=d, nbk=d, tm=tm, m=t, a_spec=row_spec,
                                b_spec=row_spec, name=f"dwmo{l}").reshape(N_DEV, d // N_DEV, d)
        dy = mm_nt(dxn, wl["mix_w_out"], "row", name=f"dy_mix{l}")
        dz, gvec, gcc, gwt, gb, gpbd = mixer_bwd(s["z"], dy, *mixer_args(l), name=f"mixer_bwd{l}")
        small["sconv_w"][l] = gvec[0:SCONV_K]
        small["sgu_norm_g"][l] = gvec[3]
        small["cconv_ln_g"][l] = gvec[4]
        small["cconv_ln_b"][l] = gvec[5]
        small["pool_scale"][l] = gvec[6]
        small["cconv_w"][l] = gcc[0:CCONV_K]
        small["sgu_w"][l] = gwt
        small["sgu_b"][l] = jnp.transpose(gb[:, 0:N_HEADS])
        gw = w // 4
        small["pool_w"][l] = jnp.stack([gpbd[g * gw:(g + 1) * gw, g * gw:(g + 1) * gw] for g in range(4)])
        dx, h, dgn = mm_nt(dz, wl["mix_w_in"], "col", x=s["x_mix"], gain=tied(norm_mix[l]), dx_in=dxn,
                           name=f"dh_mix{l}")
        small["norm_mix"][l] = dgn[0]
        th = _row_tile(t, TN_TILE // 2)
        bg["mix_w_in"] = mm_tn(h, dz, nb=1, ka=d, nbk=N_DEV * w, tm=th, m=t, col_slots=N_DEV,
                               a_spec=pl.BlockSpec((th, d), lambda s_, i: (i, 0)),
                               b_spec=pl.BlockSpec((th, N_DEV * w), lambda s_, i: (i, 0)), name=f"dwmi{l}")
        send_grads("mix", l, bg)

        dx = ffn_backward("ffn1", l, s["x_ffn1"], dx, s["gu_ffn1"], norm_ffn1[l])

    out = {}

    def finish(gname, after):
        own, land = {}, {}
        for l in reversed(range(nl)):
            members, gs, lands, send, recv = scattered.pop((gname, l))
            gs, lands = scatter_wait(gs, lands, send, recv, after, name=f"scatter_wait_{gname}{l}")
            for n, g_, l_ in zip(members, gs, lands):
                own.setdefault(n, {})[l] = g_
                land.setdefault(n, {})[l] = l_
        for n in own:
            out[n] = adamw_sharded([own[n][l] for l in range(nl)], [land[n][l] for l in range(nl)],
                                   wts[n], mom[n], var[n], me_arr, name="adamw_" + n)
            after = out[n][1]
        return after

    after = tie[0]
    for gname in ("ffn2_in", "ffn2_out", "xattn", "mix"):
        after = finish(gname, after)
    (gs, send, recv, shapes), = small_pending
    gs = gather_wait(gs, send, recv, after, name="small_gather_wait", masks=ALL_MASKS)
    summed = sum_slots(gs[0], name="small_sum")
    gsm = dict(zip(names, _unpack(summed, shapes)))
    after = finish("ffn1_in", summed)
    finish("ffn1_out", after)
    repl_shapes = [wts[n].shape for n in SMALL_REPL]
    rows_r = _rows_for(repl_shapes)
    dl, mn, vn = adamw_flat(_pack([gsm[n] for n in SMALL_REPL], rows_r),
                            _pack([wts[n] for n in SMALL_REPL], rows_r),
                            _pack([mom[n] for n in SMALL_REPL], rows_r),
                            _pack([var[n] for n in SMALL_REPL], rows_r), name="adamw_small")
    for n, a, b, c in zip(SMALL_REPL, _unpack(dl, repl_shapes), _unpack(mn, repl_shapes),
                          _unpack(vn, repl_shapes)):
        out[n] = (gsm[n], a, b, c)
    cs = w // N_DEV
    gsh = {n: lax.dynamic_slice_in_dim(gsm[n], me * cs, cs, axis=2) for n in SMALL_SHARD}
    sh_shapes = [wts[n].shape for n in SMALL_SHARD]
    rows_s = _rows_for(sh_shapes)
    dl, mn, vn = adamw_flat(_pack([gsh[n] for n in SMALL_SHARD], rows_s),
                            _pack([wts[n] for n in SMALL_SHARD], rows_s),
                            _pack([mom[n] for n in SMALL_SHARD], rows_s),
                            _pack([var[n] for n in SMALL_SHARD], rows_s), name="adamw_small_sharded")
    for n, a, b, c in zip(SMALL_SHARD, _unpack(dl, sh_shapes), _unpack(mn, sh_shapes),
                          _unpack(vn, sh_shapes)):
        out[n] = (gsh[n], a, b, c)

    grad_x = dx.reshape(1, t, d)
    return (loss, grad_x, *[out[n][0] for n in WEIGHTS], *[out[n][1] for n in WEIGHTS],
            *[out[n][2] for n in WEIGHTS], *[out[n][3] for n in WEIGHTS])
```

```python
import functools

import jax
import jax.numpy as jnp
from jax import lax
from jax.experimental import pallas as pl
from jax.experimental.pallas import tpu as pltpu

F32 = jnp.float32
BF16 = jnp.bfloat16
MESH = pl.DeviceIdType.MESH
N_DEV = 8
EPS = 1e-6
HALO = 32
SGU_CHUNK = 128
CCONV_K = 31
SCONV_K = 3
MIX_W = 256
N_HEADS = 4
VMEM_LIMIT = 56 * 1024 * 1024
ROW_TILE = 512
TN_TILE = 2048
FFN_FWD_TILE = 1024
FFN_BWD_SPLIT = 2
FFN_FWD_SPLIT = 2
MIX_TILE = 512

ADAM_LR = 0.001
ADAM_B1 = 0.9
ADAM_B2 = 0.999
ADAM_EPS = 1e-08
ADAM_WD = 0.01
ADAM_STEP = 10

HBM_SPEC = pl.BlockSpec(memory_space=pltpu.HBM)
VMEM_SPEC = pl.BlockSpec(memory_space=pltpu.VMEM)


def _params(*sem):
    return pltpu.CompilerParams(dimension_semantics=tuple(sem), vmem_limit_bytes=VMEM_LIMIT)


def _row_tile(m, pref=None):
    t = min(m, ROW_TILE if pref is None else pref)
    assert m % t == 0, (m, t)
    return t


def _my_index():
    return lax.axis_index("x") * 4 + lax.axis_index("y") * 2 + lax.axis_index("c")


def _peer(mask):
    x, y, c = lax.axis_index("x"), lax.axis_index("y"), lax.axis_index("c")
    px = 1 - x if mask & 4 else x
    py = 1 - y if mask & 2 else y
    pc = 1 - c if mask & 1 else c
    return (px, py, pc), px * 4 + py * 2 + pc


def all_gather(arrs, name):
    n = len(arrs)

    def body(*refs):
        ins, outs = refs[:n], refs[n:2 * n]
        send_sems, recv_sems, loc_sems = refs[2 * n:]
        me = _my_index()
        local = []
        for i in range(n):
            cp = pltpu.make_async_copy(ins[i], outs[i].at[me], loc_sems.at[i])
            cp.start()
            local.append(cp)
        sends = []
        for i in range(n):
            for m in range(1, N_DEV):
                peer, _ = _peer(m)
                cp = pltpu.make_async_remote_copy(
                    src_ref=ins[i], dst_ref=outs[i].at[me],
                    send_sem=send_sems.at[i, m - 1], recv_sem=recv_sems.at[i, m - 1],
                    device_id=peer, device_id_type=MESH)
                cp.start()
                sends.append(cp)
        for i in range(n):
            for m in range(1, N_DEV):
                peer, pidx = _peer(m)
                pltpu.make_async_remote_copy(
                    src_ref=ins[i], dst_ref=outs[i].at[pidx],
                    send_sem=send_sems.at[i, m - 1], recv_sem=recv_sems.at[i, m - 1],
                    device_id=peer, device_id_type=MESH).wait_recv()
        for cp in sends:
            cp.wait_send()
        for cp in local:
            cp.wait()

    return pl.pallas_call(
        body, name=name,
        out_shape=[jax.ShapeDtypeStruct((N_DEV,) + a.shape, a.dtype) for a in arrs],
        in_specs=[HBM_SPEC] * n, out_specs=[HBM_SPEC] * n,
        scratch_shapes=[pltpu.SemaphoreType.DMA((n, N_DEV - 1)),
                        pltpu.SemaphoreType.DMA((n, N_DEV - 1)),
                        pltpu.SemaphoreType.DMA((n,))],
    )(*arrs)


SEM_SPEC = pl.BlockSpec(memory_space=pltpu.SEMAPHORE)
ANY_SPEC = pl.BlockSpec(memory_space=pl.ANY)
SIDE_EFFECT = pltpu.SideEffectType.DATAFLOW_SIDE_EFFECTING


def _hbm(a):
    return pltpu.with_memory_space_constraint(a, pltpu.HBM)


def _sem_pairs(n):
    return (pltpu.SemaphoreType.DMA((n * (N_DEV - 1),)), pltpu.SemaphoreType.DMA((n * (N_DEV - 1),)))


def _sem(i, m):
    return i * (N_DEV - 1) + m - 1


def _gather_copy(g_ref, i, m, send_sems, recv_sems, origin):
    peer, _ = _peer(m)
    return pltpu.make_async_remote_copy(
        src_ref=g_ref.at[origin], dst_ref=g_ref.at[origin],
        send_sem=send_sems.at[_sem(i, m)], recv_sem=recv_sems.at[_sem(i, m)],
        device_id=peer, device_id_type=MESH)


GATHER_MASKS = (1, 2, 4, 6)
FORWARD_MASKS = (2, 4, 6)


ALL_MASKS = tuple(range(1, N_DEV))


def gather_start(gs, after, name, masks=GATHER_MASKS):
    n = len(gs)

    def body(*refs):
        g_in = refs[:n]
        send_sems, recv_sems = refs[n + 1], refs[n + 2]
        token = refs[-1]
        me = _my_index()
        for i in range(n):
            for m in masks:
                _gather_copy(g_in[i], i, m, send_sems, recv_sems, me).start()
        token[...] = jnp.zeros_like(token)

    outs = pl.pallas_call(
        body, name=name,
        out_shape=(*_sem_pairs(n), *[pltpu.HBM(g.shape, g.dtype) for g in gs],
                   jax.ShapeDtypeStruct((8, 128), F32)),
        in_specs=[HBM_SPEC] * n + [ANY_SPEC],
        out_specs=(SEM_SPEC, SEM_SPEC, *[HBM_SPEC] * n, VMEM_SPEC),
        input_output_aliases={i: 2 + i for i in range(n)},
        compiler_params=pltpu.CompilerParams(has_side_effects=SIDE_EFFECT),
    )(*[_hbm(g) for g in gs], after)
    return outs[0], outs[1], list(outs[2:2 + n]), outs[-1]


def gather_wait(gs, send_sems, recv_sems, after, name, masks=GATHER_MASKS):
    n = len(gs)

    def body(*refs):
        g_in = refs[:n]
        send, recv = refs[n], refs[n + 1]
        me = _my_index()
        for i in range(n):
            for m in masks:
                _, pidx = _peer(m)
                _gather_copy(g_in[i], i, m, send, recv, me).wait_send()
                _gather_copy(g_in[i], i, m, send, recv, pidx).wait_recv()

    outs = pl.pallas_call(
        body, name=name,
        out_shape=[pltpu.HBM(g.shape, g.dtype) for g in gs],
        in_specs=[HBM_SPEC] * n + [SEM_SPEC, SEM_SPEC, ANY_SPEC],
        out_specs=[HBM_SPEC] * n,
        input_output_aliases={i: i for i in range(n)},
        compiler_params=pltpu.CompilerParams(has_side_effects=SIDE_EFFECT),
    )(*gs, send_sems, recv_sems, after)
    return list(outs)


def sibling_forward(gs, name):
    n = len(gs)
    nf = len(FORWARD_MASKS)

    def body(*refs):
        g_in = refs[:n]
        send_sems, recv_sems = refs[2 * n:]
        x, y, c = lax.axis_index("x"), lax.axis_index("y"), lax.axis_index("c")
        sibling = (x, y, 1 - c)

        def copy(i, k, origin):
            return pltpu.make_async_remote_copy(
                src_ref=g_in[i].at[origin], dst_ref=g_in[i].at[origin],
                send_sem=send_sems.at[i * nf + k], recv_sem=recv_sems.at[i * nf + k],
                device_id=sibling, device_id_type=MESH)
        sends = []
        for i in range(n):
            for k, m in enumerate(FORWARD_MASKS):
                _, origin = _peer(m)
                cp = copy(i, k, origin)
                cp.start()
                sends.append(cp)
        for i in range(n):
            for k, m in enumerate(FORWARD_MASKS):
                _, origin = _peer(m ^ 1)
                copy(i, k, origin).wait_recv()
        for cp in sends:
            cp.wait_send()

    outs = pl.pallas_call(
        body, name=name,
        out_shape=[jax.ShapeDtypeStruct(g.shape, g.dtype) for g in gs],
        in_specs=[HBM_SPEC] * n, out_specs=[HBM_SPEC] * n,
        input_output_aliases={i: i for i in range(n)},
        scratch_shapes=[pltpu.SemaphoreType.DMA((n * nf,)), pltpu.SemaphoreType.DMA((n * nf,))],
    )(*gs)
    return list(outs)


def _scatter_copy(g_ref, l_ref, i, m, send_sems, recv_sems):
    peer, pidx = _peer(m)
    return pltpu.make_async_remote_copy(
        src_ref=g_ref.at[pidx], dst_ref=l_ref.at[m - 1],
        send_sem=send_sems.at[_sem(i, m)], recv_sem=recv_sems.at[_sem(i, m)],
        device_id=peer, device_id_type=MESH)


def scatter_start(grads, after, name):
    n = len(grads)
    lands = [lax.empty((N_DEV - 1,) + g.shape[1:], g.dtype) for g in grads]

    def body(*refs):
        g_in, l_in = refs[:n], refs[n:2 * n]
        send_sems, recv_sems = refs[2 * n + 1], refs[2 * n + 2]
        token = refs[-1]
        for i in range(n):
            for m in range(1, N_DEV):
                _scatter_copy(g_in[i], l_in[i], i, m, send_sems, recv_sems).start()
        token[...] = jnp.zeros_like(token)

    outs = pl.pallas_call(
        body, name=name,
        out_shape=(*_sem_pairs(n), *[pltpu.HBM(g.shape, g.dtype) for g in grads],
                   *[pltpu.HBM(l.shape, l.dtype) for l in lands], jax.ShapeDtypeStruct((8, 128), F32)),
        in_specs=[HBM_SPEC] * (2 * n) + [ANY_SPEC],
        out_specs=(SEM_SPEC, SEM_SPEC, *[HBM_SPEC] * (2 * n), VMEM_SPEC),
        input_output_aliases={i: 2 + i for i in range(2 * n)},
        compiler_params=pltpu.CompilerParams(has_side_effects=SIDE_EFFECT),
    )(*[_hbm(g) for g in grads], *[_hbm(l) for l in lands], after)
    return outs[0], outs[1], list(outs[2:2 + n]), list(outs[2 + n:2 + 2 * n]), outs[-1]


def scatter_wait(grads, lands, send_sems, recv_sems, after, name):
    n = len(grads)

    def body(*refs):
        g_in, l_in = refs[:n], refs[n:2 * n]
        send, recv = refs[2 * n], refs[2 * n + 1]
        for i in range(n):
            for m in range(1, N_DEV):
                cp = _scatter_copy(g_in[i], l_in[i], i, m, send, recv)
                cp.wait_send()
                cp.wait_recv()

    outs = pl.pallas_call(
        body, name=name,
        out_shape=[pltpu.HBM(a.shape, a.dtype) for a in list(grads) + list(lands)],
        in_specs=[HBM_SPEC] * (2 * n) + [SEM_SPEC, SEM_SPEC, ANY_SPEC],
        out_specs=[HBM_SPEC] * (2 * n),
        input_output_aliases={i: i for i in range(2 * n)},
        compiler_params=pltpu.CompilerParams(has_side_effects=SIDE_EFFECT),
    )(*grads, *lands, send_sems, recv_sems, after)
    return list(outs[:n]), list(outs[n:])


def sum_slots(g, name):
    _, r, c = g.shape

    def body(g_ref, out_ref):
        acc = g_ref[0]
        for p in range(1, N_DEV):
            acc = acc + g_ref[p]
        out_ref[...] = acc

    return pl.pallas_call(
        body, name=name, out_shape=jax.ShapeDtypeStruct((r, c), F32),
        in_specs=[VMEM_SPEC], out_specs=VMEM_SPEC,
        compiler_params=pltpu.CompilerParams(vmem_limit_bytes=VMEM_LIMIT),
    )(g)


def _sigmoid(v):
    return 1.0 / (1.0 + jnp.exp(-v))


def _rms_fwd(xf, g):
    r = lax.rsqrt(jnp.mean(xf * xf, axis=-1, keepdims=True) + EPS)
    return xf * r, r


def _rms_bwd(xhat, r, g, dy):
    dg = jnp.sum(dy * xhat, axis=0, keepdims=True)
    dxh = dy * g
    dx = r * (dxh - xhat * jnp.mean(dxh * xhat, axis=-1, keepdims=True))
    return dx, dg


def _ln_stats(v):
    mu = jnp.mean(v, axis=-1, keepdims=True)
    vc = v - mu
    r = lax.rsqrt(jnp.mean(vc * vc, axis=-1, keepdims=True) + EPS)
    return vc * r, r


def _ln_bwd(xhat, r, dxh):
    return r * (dxh - jnp.mean(dxh, axis=-1, keepdims=True)
                - xhat * jnp.mean(dxh * xhat, axis=-1, keepdims=True))


def _dot(a, b):
    return jnp.dot(a, b, preferred_element_type=F32)


def _dot_nt(a, b):
    return lax.dot_general(a, b, (((1,), (1,)), ((), ())), preferred_element_type=F32)


def _dot_tn(a, b):
    return lax.dot_general(a, b, (((0,), (0,)), ((), ())), preferred_element_type=F32)


def _full_weight(w_ref, kind):
    assert kind == "row"
    p, a, b = w_ref.shape
    return w_ref[...].reshape(p * a, b)


def _wspec(wg):
    return pl.BlockSpec(wg.shape, lambda *_: (0, 0, 0))


def mm_rows(a, wg, kind, *, gain=None, residual=None, out_dtype=F32, name, tm=None):
    m, k = a.shape
    p, wa, wb = wg.shape
    n = p * wb if kind == "col" else wb
    tm = _row_tile(m, tm)
    has_gain, has_res = gain is not None, residual is not None

    def body(*refs):
        refs = list(refs)
        a_ref = refs.pop(0)
        g_ref = refs.pop(0) if has_gain else None
        w_ref = refs.pop(0)
        r_ref = refs.pop(0) if has_res else None
        o_ref = refs.pop(0)
        if has_gain:
            xhat, _ = _rms_fwd(a_ref[...].astype(F32), None)
            h = (xhat * g_ref[...]).astype(BF16)
        else:
            h = a_ref[...].astype(BF16)
        if kind == "col":
            for j in range(p):
                o = _dot(h, w_ref[j])
                if has_res:
                    o = o + r_ref[:, j * wb:(j + 1) * wb]
                o_ref[:, j * wb:(j + 1) * wb] = o.astype(out_dtype)
        else:
            o = _dot(h, _full_weight(w_ref, "row"))
            if has_res:
                o = o + r_ref[...]
            o_ref[...] = o.astype(out_dtype)

    operands = [a]
    in_specs = [pl.BlockSpec((tm, k), lambda i: (i, 0))]
    if has_gain:
        operands.append(gain.reshape(1, k))
        in_specs.append(pl.BlockSpec((1, k), lambda i: (0, 0)))
    operands.append(wg)
    in_specs.append(_wspec(wg))
    if has_res:
        operands.append(residual)
        in_specs.append(pl.BlockSpec((tm, n), lambda i: (i, 0)))
    return pl.pallas_call(
        body, name=name, grid=(m // tm,),
        out_shape=jax.ShapeDtypeStruct((m, n), out_dtype),
        in_specs=in_specs, out_specs=pl.BlockSpec((tm, n), lambda i: (i, 0)),
        compiler_params=_params("parallel"),
    )(*operands)


def mm_nt(dz, wg, kind, *, x=None, gain=None, dx_in=None, name, tm=None):
    m, n = dz.shape
    p, wa, wb = wg.shape
    k = wa if kind == "col" else p * wa
    tm = _row_tile(m, tm)
    epi = x is not None
    has_dx = dx_in is not None

    def body(*refs):
        refs = list(refs)
        dz_ref, w_ref = refs.pop(0), refs.pop(0)
        if epi:
            x_ref, g_ref = refs.pop(0), refs.pop(0)
            dxi_ref = refs.pop(0) if has_dx else None
            dx_ref, h_ref, dg_ref = refs
        else:
            (da_ref,) = refs
        dzb = dz_ref[...].astype(BF16)
        if kind == "col":
            da = _dot_nt(dzb[:, 0:wb], w_ref[0])
            for j in range(1, p):
                da = da + _dot_nt(dzb[:, j * wb:(j + 1) * wb], w_ref[j])
        else:
            da = _dot_nt(dzb, _full_weight(w_ref, "row"))
        if not epi:
            da_ref[...] = da
            return
        g = g_ref[...]
        xhat, r = _rms_fwd(x_ref[...].astype(F32), None)
        h_ref[...] = (xhat * g).astype(BF16)
        dx, dg = _rms_bwd(xhat, r, g, da)
        if has_dx:
            dx = dx + dxi_ref[...]
        dx_ref[...] = dx

        @pl.when(pl.program_id(0) == 0)
        def _():
            dg_ref[...] = jnp.zeros_like(dg_ref)
        dg_ref[...] += dg

    row = lambda i: (i, 0)
    operands = [dz, wg]
    in_specs = [pl.BlockSpec((tm, n), row), _wspec(wg)]
    if epi:
        operands += [x, gain.reshape(1, k)]
        in_specs += [pl.BlockSpec((tm, k), row), pl.BlockSpec((1, k), lambda i: (0, 0))]
        if has_dx:
            operands.append(dx_in)
            in_specs.append(pl.BlockSpec((tm, k), row))
        out_shape = [jax.ShapeDtypeStruct((m, k), F32), jax.ShapeDtypeStruct((m, k), BF16),
                     jax.ShapeDtypeStruct((1, k), F32)]
        out_specs = [pl.BlockSpec((tm, k), row), pl.BlockSpec((tm, k), row),
                     pl.BlockSpec((1, k), lambda i: (0, 0))]
    else:
        out_shape = jax.ShapeDtypeStruct((m, k), F32)
        out_specs = pl.BlockSpec((tm, k), row)
    return pl.pallas_call(
        body, name=name, grid=(m // tm,), out_shape=out_shape,
        in_specs=in_specs, out_specs=out_specs,
        compiler_params=_params("arbitrary"),
    )(*operands)


def mm_tn(a, b, *, nb, a_spec, b_spec, ka, nbk, tm, m, scale=1.0, out_dtype=BF16, col_slots=1,
          after=None, name):
    ni = m // tm
    assert col_slots == 1 or nb == 1
    cw = nbk // col_slots
    extra = [] if after is None else [after]

    def body(a_ref, b_ref, *rest):
        o_ref, acc = rest[len(extra):]
        i = pl.program_id(1)

        @pl.when(i == 0)
        def _():
            acc[...] = jnp.zeros_like(acc)
        acc[...] += _dot_tn(a_ref[...].astype(BF16), b_ref[...].astype(BF16))

        @pl.when(i == ni - 1)
        def _():
            if col_slots == 1:
                o_ref[...] = (acc[...] * scale).astype(out_dtype)
            else:
                for j in range(col_slots):
                    o_ref[j] = (acc[:, j * cw:(j + 1) * cw] * scale).astype(out_dtype)

    if col_slots == 1:
        out_shape = jax.ShapeDtypeStruct((nb, ka, nbk), out_dtype)
        out_spec = pl.BlockSpec((None, ka, nbk), lambda s, i: (s, 0, 0))
    else:
        out_shape = jax.ShapeDtypeStruct((col_slots, ka, cw), out_dtype)
        out_spec = pl.BlockSpec((col_slots, ka, cw), lambda s, i: (0, 0, 0))
    return pl.pallas_call(
        body, name=name, grid=(nb, ni), out_shape=out_shape,
        in_specs=[a_spec, b_spec] + [ANY_SPEC] * len(extra), out_specs=out_spec,
        scratch_shapes=[pltpu.VMEM((ka, nbk), F32)],
        compiler_params=_params("parallel", "arbitrary"),
    )(a, b, *extra)


def _ffn_specs(w_in_g, w_out_g, d):
    nf = w_in_g.shape[1]
    hr = w_out_g.shape[1]
    assert 2 * hr == nf
    w_in5 = w_in_g.reshape(2, 4, nf, d)
    w_out5 = w_out_g.reshape(4, 2, hr, d)
    in_spec = pl.BlockSpec((2, None, nf, d), lambda i, j: (0, j, 0, 0))
    out_spec = pl.BlockSpec((None, 2, hr, d), lambda i, j: (j, 0, 0, 0))
    return w_in5, w_out5, in_spec, out_spec, nf


def ffn_fwd(x, gain, w_in_g, w_out_g, *, name, tm=None):
    t, d = x.shape
    tm = _row_tile(t, tm)
    w_in5, w_out5, wi_spec, wo_spec, nf = _ffn_specs(w_in_g, w_out_g, d)

    def body(x_ref, g_ref, wi_ref, wo_ref, o_ref, gu_ref, h_scr, acc):
        j = pl.program_id(1)

        @pl.when(j == 0)
        def _():
            xhat, _ = _rms_fwd(x_ref[...], None)
            h_scr[...] = (xhat * g_ref[...]).astype(BF16)
            acc[...] = jnp.zeros_like(acc)
        wo = wo_ref[...].reshape(nf, d)

        def project(rows):
            h = h_scr[rows]
            return _dot_nt(h, wi_ref[0]), _dot_nt(h, wi_ref[1])

        sub = tm // FFN_FWD_SPLIT
        parts = [slice(k * sub, (k + 1) * sub) for k in range(FFN_FWD_SPLIT)]
        gt, up = project(parts[0])
        for k, rows in enumerate(parts):
            if k + 1 < len(parts):
                nxt = project(parts[k + 1])
            gu_ref[0, rows] = gt.astype(BF16)
            gu_ref[1, rows] = up.astype(BF16)
            act = (gt * _sigmoid(gt) * up).astype(BF16)
            acc[rows] += _dot(act, wo)
            if k + 1 < len(parts):
                gt, up = nxt

        @pl.when(j == 3)
        def _():
            o_ref[...] = x_ref[...] + 0.5 * acc[...]

    return pl.pallas_call(
        body, name=name, grid=(t // tm, 4),
        out_shape=[jax.ShapeDtypeStruct((t, d), F32), jax.ShapeDtypeStruct((2, 4, t, nf), BF16)],
        in_specs=[pl.BlockSpec((tm, d), lambda i, j: (i, 0)),
                  pl.BlockSpec((1, d), lambda i, j: (0, 0)), wi_spec, wo_spec],
        out_specs=[pl.BlockSpec((tm, d), lambda i, j: (i, 0)),
                   pl.BlockSpec((2, None, tm, nf), lambda i, j: (0, j, i, 0))],
        scratch_shapes=[pltpu.VMEM((tm, d), BF16), pltpu.VMEM((tm, d), F32)],
        compiler_params=_params("parallel", "arbitrary"),
    )(x, gain.reshape(1, d), w_in5, w_out5)


def ffn_bwd_rows(x, dy, gu, gain, w_in_g, w_out_g, *, name, tm=None):
    t, d = x.shape
    tm = _row_tile(t, tm)
    w_in5, w_out5, wi_spec, wo_spec, nf = _ffn_specs(w_in_g, w_out_g, d)

    def body(x_ref, dy_ref, gu_ref, g_ref, wi_ref, wo_ref, dx_ref, h_ref, act_ref, dgu_ref, dg_ref,
             dh_acc, dyh_scr):
        i, j = pl.program_id(0), pl.program_id(1)

        @pl.when(j == 0)
        def _():
            xhat, _ = _rms_fwd(x_ref[...], None)
            h_ref[...] = (xhat * g_ref[...]).astype(BF16)
            dyh_scr[...] = (0.5 * dy_ref[...]).astype(BF16)
            dh_acc[...] = jnp.zeros_like(dh_acc)
        wo = wo_ref[...].reshape(nf, d)

        def gates(rows):
            gt = gu_ref[0, rows].astype(F32)
            up = gu_ref[1, rows].astype(F32)
            sg = _sigmoid(gt)
            silu = gt * sg
            act_ref[rows] = (silu * up).astype(BF16)
            return up * (sg * (1.0 + gt * (1.0 - sg))), silu

        def grads(rows, dact, dsilu_up, silu):
            dgt = (dact * dsilu_up).astype(BF16)
            dup = (dact * silu).astype(BF16)
            dgu_ref[0, rows] = dgt
            dgu_ref[1, rows] = dup
            return dgt, dup

        sub = tm // FFN_BWD_SPLIT
        parts = [slice(k * sub, (k + 1) * sub) for k in range(FFN_BWD_SPLIT)]
        dact = _dot_nt(dyh_scr[parts[0]], wo)
        gate = gates(parts[0])
        for k, rows in enumerate(parts):
            if k + 1 < len(parts):
                dact_next = _dot_nt(dyh_scr[parts[k + 1]], wo)
            dgt, dup = grads(rows, dact, *gate)
            dh_acc[rows] += _dot(dgt, wi_ref[0]) + _dot(dup, wi_ref[1])
            if k + 1 < len(parts):
                gate = gates(parts[k + 1])
                dact = dact_next

        @pl.when(j == 3)
        def _():
            g = g_ref[...]
            xhat, r = _rms_fwd(x_ref[...], None)
            dx, dg = _rms_bwd(xhat, r, g, dh_acc[...])
            dx_ref[...] = dy_ref[...] + dx

            @pl.when(i == 0)
            def _():
                dg_ref[...] = jnp.zeros_like(dg_ref)
            dg_ref[...] += dg

    row = lambda i, j: (i, 0)
    return pl.pallas_call(
        body, name=name, grid=(t // tm, 4),
        out_shape=[jax.ShapeDtypeStruct((t, d), F32), jax.ShapeDtypeStruct((t, d), BF16),
                   jax.ShapeDtypeStruct((4, t, nf), BF16), jax.ShapeDtypeStruct((2, 4, t, nf), BF16),
                   jax.ShapeDtypeStruct((1, d), F32)],
        in_specs=[pl.BlockSpec((tm, d), row), pl.BlockSpec((tm, d), row),
                  pl.BlockSpec((2, None, tm, nf), lambda i, j: (0, j, i, 0)),
                  pl.BlockSpec((1, d), lambda i, j: (0, 0)), wi_spec, wo_spec],
        out_specs=[pl.BlockSpec((tm, d), row), pl.BlockSpec((tm, d), row),
                   pl.BlockSpec((None, tm, nf), lambda i, j: (j, i, 0)),
                   pl.BlockSpec((2, None, tm, nf), lambda i, j: (0, j, i, 0)),
                   pl.BlockSpec((1, d), lambda i, j: (0, 0))],
        scratch_shapes=[pltpu.VMEM((tm, d), F32), pltpu.VMEM((tm, d), BF16)],
        compiler_params=_params("arbitrary", "arbitrary"),
    )(x, dy, gu, gain.reshape(1, d), w_in5, w_out5)


def ffn_grad_w_in(h, dgu, after, *, name):
    t, d = h.shape
    nf = dgu.shape[-1]
    tm = _row_tile(t, TN_TILE)
    return mm_tn(dgu.reshape(8, t, nf), h, nb=8, ka=nf, nbk=d, tm=tm, m=t, after=after,
                 a_spec=pl.BlockSpec((None, tm, nf), lambda s, i: (s, i, 0)),
                 b_spec=pl.BlockSpec((tm, d), lambda s, i: (i, 0)), name=name)


def ffn_grad_w_out(act, dy, after, *, name):
    _, t, nf = act.shape
    d = dy.shape[1]
    tm = _row_tile(t, TN_TILE)
    d_w_out = mm_tn(act, dy, nb=4, ka=nf, nbk=d, tm=tm, m=t, scale=0.5, after=after,
                    a_spec=pl.BlockSpec((None, tm, nf), lambda s, i: (s, i, 0)),
                    b_spec=pl.BlockSpec((tm, d), lambda s, i: (i, 0)), name=name)
    return d_w_out.reshape(8, nf // 2, d)


def _lane_group(shape):
    return lax.shift_right_logical(lax.broadcasted_iota(jnp.int32, shape, 1), 6)


def _pool_count(t0, rows):
    t = (t0 + lax.broadcasted_iota(jnp.int32, (rows, MIX_W), 0) + 1).astype(F32)
    return jnp.minimum(t, _by_group(_lane_group((rows, MIX_W)), 2.0, 4.0, 8.0, 16.0))


def _by_group(grp, v0, v1, v2, v3):
    return jnp.where(grp == 0, v0, jnp.where(grp == 1, v1, jnp.where(grp == 2, v2, v3)))


def _sgu_mix(wt_ref, vnc):
    grp = _lane_group((SGU_CHUNK, MIX_W))
    out = jnp.zeros((SGU_CHUNK, MIX_W), F32)
    for hd in range(N_HEADS):
        out = jnp.where(grp == hd, _dot(wt_ref[hd], vnc), out)
    return out


def _pool_fwd(s1, s2, s3, t0, ts, lo):
    h = lo
    s2[h - 24:h + ts] = s1[h - 24:h + ts] + s1[h - 25:h + ts - 1]
    s3[h - 16:h + ts] = s2[h - 16:h + ts] + s2[h - 18:h + ts - 2]
    sum2 = s2[h:h + ts]
    sum4 = s3[h:h + ts]
    s2[h - 8:h + ts] = s3[h - 8:h + ts] + s3[h - 12:h + ts - 4]
    sum8 = s2[h:h + ts]
    sum16 = sum8 + s2[h - 8:h + ts - 8]
    grp = _lane_group((ts, MIX_W))
    return _by_group(grp, sum2, sum4, sum8, sum16) / _pool_count(t0, ts) - s1[h:h + ts]


def mixer_fwd(z, sconv, cconv, vecs, wt, bexp, pbd, *, name, ts=None):
    t = z.shape[0]
    ts = _row_tile(t, MIX_TILE if ts is None else ts)
    hl = HALO
    w = MIX_W
    nch = ts // SGU_CHUNK

    def body(zc, zp, sconv_ref, cconv_ref, vec_ref, wt_ref, bexp_ref, pbd_ref, y_ref, s1, s2, s3):
        i = pl.program_id(0)
        has_prev = i > 0

        def col(ref, c):
            return ref[:, c * w:(c + 1) * w]

        def prev(c):
            return jnp.where(has_prev, col(zp, c), 0.0)

        s1[0:hl] = prev(1) * prev(2)
        s1[hl:hl + ts] = col(zc, 1) * col(zc, 2)
        cv = sconv_ref[0:1] * s1[hl - 2:hl - 2 + ts]
        for k in range(1, SCONV_K):
            cv = cv + sconv_ref[k:k + 1] * s1[hl - 2 + k:hl - 2 + k + ts]
        y_ref[:, 0:w] = (col(zc, 0) * cv).astype(BF16)

        xhat, _ = _ln_stats(col(zc, 4))
        vn = (xhat * vec_ref[0:1]).astype(BF16)
        for c in range(nch):
            rows = slice(c * SGU_CHUNK, (c + 1) * SGU_CHUNK)
            mixed = _sgu_mix(wt_ref, vn[rows]) + bexp_ref[...]
            y_ref[rows, w:2 * w] = (zc[rows, 3 * w:4 * w] * mixed).astype(BF16)

        s1[0:hl] = prev(5) * _sigmoid(prev(6))
        s1[hl:hl + ts] = col(zc, 5) * _sigmoid(col(zc, 6))
        off = hl - (CCONV_K - 1)
        cv = cconv_ref[0:1] * s1[off:off + ts]
        for k in range(1, CCONV_K):
            cv = cv + cconv_ref[k:k + 1] * s1[off + k:off + k + ts]
        xhat, _ = _ln_stats(cv)
        ln = xhat * vec_ref[1:2] + vec_ref[2:3]
        y_ref[:, 2 * w:3 * w] = (ln * _sigmoid(ln)).astype(BF16)

        s1[0:hl] = prev(7)
        s1[hl:hl + ts] = col(zc, 7)
        pooled = _pool_fwd(s1, s2, s3, i * ts, ts, hl)
        y_ref[:, 3 * w:4 * w] = (_dot(pooled.astype(BF16), pbd_ref[...]) * vec_ref[3:4]).astype(BF16)

    full = lambda shape: pl.BlockSpec(shape, lambda i: (0,) * len(shape))
    return pl.pallas_call(
        body, name=name, grid=(t // ts,),
        out_shape=jax.ShapeDtypeStruct((t, 4 * w), BF16),
        in_specs=[pl.BlockSpec((ts, 8 * w), lambda i: (i, 0)),
                  pl.BlockSpec((hl, 8 * w), lambda i: (jnp.maximum(i * (ts // hl) - 1, 0), 0)),
                  full((8, w)), full((32, w)), full((8, w)), full((N_HEADS, SGU_CHUNK, SGU_CHUNK)),
                  full((SGU_CHUNK, w)), full((w, w))],
        out_specs=pl.BlockSpec((ts, 4 * w), lambda i: (i, 0)),
        scratch_shapes=[pltpu.VMEM((hl + ts, w), F32)] * 3,
        compiler_params=_params("parallel"),
    )(z, z, sconv, cconv, vecs, wt, bexp, pbd)


def mixer_bwd(z, dy, sconv, cconv, vecs, wt, bexp, pbd, *, name, ts=None):
    t = z.shape[0]
    ts = _row_tile(t, MIX_TILE if ts is None else ts)
    hl = HALO
    w = MIX_W
    nch = ts // SGU_CHUNK
    ni = t // ts
    ext = ts + hl

    def body(zc, zp, zn, dyc, dyn, sconv_ref, cconv_ref, vec_ref, wt_ref, bexp_ref, pbd_ref,
             dz_ref, gvec_ref, gcc_ref, gwt_ref, gb_ref, gpbd_ref, s1, s2, s3):
        i = pl.program_id(0)
        has_prev = i > 0
        has_next = i < ni - 1

        @pl.when(i == 0)
        def _():
            gvec_ref[...] = jnp.zeros_like(gvec_ref)
            gcc_ref[...] = jnp.zeros_like(gcc_ref)
            gwt_ref[...] = jnp.zeros_like(gwt_ref)
            gb_ref[...] = jnp.zeros_like(gb_ref)
            gpbd_ref[...] = jnp.zeros_like(gpbd_ref)

        def col(ref, c):
            return ref[:, c * w:(c + 1) * w]

        def prev(c):
            return jnp.where(has_prev, col(zp, c), 0.0)

        def nxt(c):
            return jnp.where(has_next, col(zn, c), 0.0)

        def dnext(c):
            return jnp.where(has_next, col(dyn, c), 0.0)

        def rowsum(v):
            return jnp.sum(v, axis=0, keepdims=True)

        s1[0:hl] = prev(1) * prev(2)
        s1[hl:hl + ts] = col(zc, 1) * col(zc, 2)
        s1[hl + ts:hl + ts + hl] = nxt(1) * nxt(2)
        cv = sconv_ref[0:1] * s1[hl - 2:hl - 2 + ts]
        for k in range(1, SCONV_K):
            cv = cv + sconv_ref[k:k + 1] * s1[hl - 2 + k:hl - 2 + k + ts]
        dya = col(dyc, 0)
        dz_ref[:, 0:w] = (dya * cv).astype(BF16)
        s2[0:ts] = dya * col(zc, 0)
        s2[ts:ext] = dnext(0) * nxt(0)
        dv = sconv_ref[0:1] * s2[2:2 + ts]
        for k in range(1, SCONV_K):
            dv = dv + sconv_ref[k:k + 1] * s2[2 - k:2 - k + ts]
        dz_ref[:, w:2 * w] = (dv * col(zc, 2)).astype(BF16)
        dz_ref[:, 2 * w:3 * w] = (dv * col(zc, 1)).astype(BF16)
        dcv = s2[0:ts]
        for k in range(SCONV_K):
            gvec_ref[k:k + 1] += rowsum(dcv * s1[hl - 2 + k:hl - 2 + k + ts])

        g_sgu = vec_ref[0:1]
        xhat, rstd = _ln_stats(col(zc, 4))
        vn = (xhat * g_sgu).astype(BF16)
        grp = _lane_group((SGU_CHUNK, w))
        lane = lax.broadcasted_iota(jnp.int32, (SGU_CHUNK, SGU_CHUNK), 1)
        tril = lax.broadcasted_iota(jnp.int32, (SGU_CHUNK, SGU_CHUNK), 0) >= lane
        for c in range(nch):
            rows = slice(c * SGU_CHUNK, (c + 1) * SGU_CHUNK)
            vnc = vn[rows]
            mixed = _sgu_mix(wt_ref, vnc) + bexp_ref[...]
            dyb = dyc[rows, w:2 * w]
            dz_ref[rows, 3 * w:4 * w] = (dyb * mixed).astype(BF16)
            dmix = dyb * zc[rows, 3 * w:4 * w]
            dmixb = dmix.astype(BF16)
            dvn = jnp.zeros((SGU_CHUNK, w), F32)
            gb = jnp.zeros((SGU_CHUNK, SGU_CHUNK), F32)
            for hd in range(N_HEADS):
                dvn = jnp.where(grp == hd, _dot_tn(wt_ref[hd], dmixb), dvn)
                dm_h = jnp.where(grp == hd, dmix, 0.0)
                gwt_ref[hd] += jnp.where(tril, _dot_nt(dm_h.astype(BF16), vnc), 0.0)
                gb = gb + jnp.where(lane == hd, jnp.sum(dm_h, axis=1, keepdims=True), 0.0)
            gb_ref[...] += gb
            s3[rows] = dvn
        dvn = s3[0:ts]
        gvec_ref[3:4] += rowsum(dvn * xhat)
        dz_ref[:, 4 * w:5 * w] = _ln_bwd(xhat, rstd, dvn * g_sgu).astype(BF16)

        sig_c = _sigmoid(col(zc, 6))
        s1[0:hl] = prev(5) * _sigmoid(prev(6))
        s1[hl:hl + ts] = col(zc, 5) * sig_c
        s1[hl + ts:hl + ts + hl] = nxt(5) * _sigmoid(nxt(6))
        off = hl - (CCONV_K - 1)
        cv = cconv_ref[0:1] * s1[off:off + ext]
        for k in range(1, CCONV_K):
            cv = cv + cconv_ref[k:k + 1] * s1[off + k:off + k + ext]
        xhat, rstd = _ln_stats(cv)
        ln = xhat * vec_ref[1:2] + vec_ref[2:3]
        sg = _sigmoid(ln)
        s2[0:ts] = col(dyc, 2)
        s2[ts:ext] = dnext(2)
        dln = s2[0:ext] * (sg * (1.0 + ln * (1.0 - sg)))
        gvec_ref[4:5] += rowsum(dln[0:ts] * xhat[0:ts])
        gvec_ref[5:6] += rowsum(dln[0:ts])
        s3[0:ext] = _ln_bwd(xhat, rstd, dln * vec_ref[1:2])
        dyg = cconv_ref[0:1] * s3[CCONV_K - 1:CCONV_K - 1 + ts]
        for k in range(1, CCONV_K):
            dyg = dyg + cconv_ref[k:k + 1] * s3[CCONV_K - 1 - k:CCONV_K - 1 - k + ts]
        dz_ref[:, 5 * w:6 * w] = (dyg * sig_c).astype(BF16)
        dz_ref[:, 6 * w:7 * w] = (dyg * col(zc, 5) * sig_c * (1.0 - sig_c)).astype(BF16)
        dcv = s3[0:ts]
        for k in range(CCONV_K):
            gcc_ref[k:k + 1] += rowsum(dcv * s1[off + k:off + k + ts])

        scale = vec_ref[3:4]
        s1[0:hl] = prev(7)
        s1[hl:hl + ts] = col(zc, 7)
        pooled = _pool_fwd(s1, s2, s3, i * ts, ts, hl).astype(BF16)
        q0 = _dot(pooled, pbd_ref[...])
        dyd = col(dyc, 3)
        gvec_ref[6:7] += rowsum(dyd * q0)
        dq = (dyd * scale).astype(BF16)
        gpbd_ref[...] += _dot_tn(pooled, dq)
        s1[0:ts] = _dot_nt(dq, pbd_ref[...])
        s1[ts:ext] = _dot_nt((dnext(3) * scale).astype(BF16), pbd_ref[...])
        dpool = s1[0:ts]
        s2[0:ext] = s1[0:ext] / _pool_count(i * ts, ext)
        s3[0:ts + 24] = s2[0:ts + 24] + s2[1:ts + 25]
        f2 = s3[0:ts]
        s2[0:ts + 16] = s3[0:ts + 16] + s3[2:ts + 18]
        f4 = s2[0:ts]
        s3[0:ts + 8] = s2[0:ts + 8] + s2[4:ts + 12]
        f8 = s3[0:ts]
        f16 = f8 + s3[8:ts + 8]
        dz_ref[:, 7 * w:8 * w] = (_by_group(_lane_group((ts, w)), f2, f4, f8, f16) - dpool).astype(BF16)

    full = lambda shape: pl.BlockSpec(shape, lambda i: (0,) * len(shape))
    r = ts // hl
    prev_map = lambda i: (jnp.maximum(i * r - 1, 0), 0)
    next_map = lambda i: (jnp.minimum((i + 1) * r, t // hl - 1), 0)
    return pl.pallas_call(
        body, name=name, grid=(ni,),
        out_shape=[jax.ShapeDtypeStruct((t, 8 * w), BF16), jax.ShapeDtypeStruct((8, w), F32),
                   jax.ShapeDtypeStruct((32, w), F32),
                   jax.ShapeDtypeStruct((N_HEADS, SGU_CHUNK, SGU_CHUNK), F32),
                   jax.ShapeDtypeStruct((SGU_CHUNK, SGU_CHUNK), F32), jax.ShapeDtypeStruct((w, w), F32)],
        in_specs=[pl.BlockSpec((ts, 8 * w), lambda i: (i, 0)),
                  pl.BlockSpec((hl, 8 * w), prev_map), pl.BlockSpec((hl, 8 * w), next_map),
                  pl.BlockSpec((ts, 4 * w), lambda i: (i, 0)), pl.BlockSpec((hl, 4 * w), next_map),
                  full((8, w)), full((32, w)), full((8, w)), full((N_HEADS, SGU_CHUNK, SGU_CHUNK)),
                  full((SGU_CHUNK, w)), full((w, w))],
        out_specs=[pl.BlockSpec((ts, 8 * w), lambda i: (i, 0)), full((8, w)), full((32, w)),
                   full((N_HEADS, SGU_CHUNK, SGU_CHUNK)), full((SGU_CHUNK, SGU_CHUNK)), full((w, w))],
        scratch_shapes=[pltpu.VMEM((ts + 2 * hl, w), F32)] * 3,
        compiler_params=_params("arbitrary"),
    )(z, z, z, dy, dy, sconv, cconv, vecs, wt, bexp, pbd)


def _attn_head(q, kv_ref, hd, d):
    hw = d // N_HEADS
    qh = q[:, hd * hw:(hd + 1) * hw]
    kh = kv_ref[:, hd * hw:(hd + 1) * hw].astype(BF16)
    vh = kv_ref[:, d + hd * hw:d + (hd + 1) * hw].astype(BF16)
    s = _dot_nt(qh, kh) * (1.0 / (hw ** 0.5))
    e = jnp.exp(s - jnp.max(s, axis=-1, keepdims=True))
    p = e / jnp.sum(e, axis=-1, keepdims=True)
    return qh, kh, vh, p


def xattn_fwd(x, gain, kv, wq_g, wo_g, *, name, tm=None):
    t, d = x.shape
    nm = kv.shape[0]
    tm = _row_tile(t, tm)
    hw = d // N_HEADS

    def body(x_ref, g_ref, kv_ref, wq_ref, wo_ref, o_ref):
        xv = x_ref[...]
        xhat, _ = _rms_fwd(xv, None)
        h = (xhat * g_ref[...]).astype(BF16)
        q = _dot(h, _full_weight(wq_ref, "row")).astype(BF16)
        wo = _full_weight(wo_ref, "row")
        out = xv
        for hd in range(N_HEADS):
            _, _, vh, p = _attn_head(q, kv_ref, hd, d)
            oh = _dot(p.astype(BF16), vh).astype(BF16)
            out = out + _dot(oh, wo[hd * hw:(hd + 1) * hw])
        o_ref[...] = out

    row = lambda i: (i, 0)
    return pl.pallas_call(
        body, name=name, grid=(t // tm,),
        out_shape=jax.ShapeDtypeStruct((t, d), F32),
        in_specs=[pl.BlockSpec((tm, d), row), pl.BlockSpec((1, d), lambda i: (0, 0)),
                  pl.BlockSpec((nm, 2 * d), lambda i: (0, 0)), _wspec(wq_g), _wspec(wo_g)],
        out_specs=pl.BlockSpec((tm, d), row),
        compiler_params=_params("parallel"),
    )(x, gain.reshape(1, d), kv, wq_g, wo_g)


def xattn_bwd_rows(x, dxn, gain, kv, wq_g, wo_g, *, name, tm=None):
    t, d = x.shape
    nm = kv.shape[0]
    tm = _row_tile(t, tm)
    hw = d // N_HEADS

    def body(x_ref, dxn_ref, g_ref, kv_ref, wq_ref, wo_ref,
             dx_ref, h_ref, dq_ref, o_ref, dkv_ref, dg_ref):
        i = pl.program_id(0)

        @pl.when(i == 0)
        def _():
            dkv_ref[...] = jnp.zeros_like(dkv_ref)
            dg_ref[...] = jnp.zeros_like(dg_ref)
        g = g_ref[...]
        xhat, r = _rms_fwd(x_ref[...], None)
        h = (xhat * g).astype(BF16)
        h_ref[...] = h
        wq = _full_weight(wq_ref, "row")
        q = _dot(h, wq).astype(BF16)
        dxn = dxn_ref[...]
        do = _dot_nt(dxn.astype(BF16), _full_weight(wo_ref, "row")).astype(BF16)
        for hd in range(N_HEADS):
            cols = slice(hd * hw, (hd + 1) * hw)
            qh, kh, vh, p = _attn_head(q, kv_ref, hd, d)
            pb = p.astype(BF16)
            o_ref[:, cols] = _dot(pb, vh).astype(BF16)
            doh = do[:, cols]
            dkv_ref[:, d + hd * hw:d + (hd + 1) * hw] += _dot_tn(pb, doh)
            dp = _dot_nt(doh, vh)
            ds = (p * (dp - jnp.sum(dp * p, axis=-1, keepdims=True)) * (1.0 / (hw ** 0.5))).astype(BF16)
            dq_ref[:, cols] = _dot(ds, kh).astype(BF16)
            dkv_ref[:, cols] += _dot_tn(ds, qh)
        dh = _dot_nt(dq_ref[...], wq)
        dx, dg = _rms_bwd(xhat, r, g, dh)
        dx_ref[...] = dxn + dx
        dg_ref[...] += dg

    row = lambda i: (i, 0)
    fix = lambda i: (0, 0)
    return pl.pallas_call(
        body, name=name, grid=(t // tm,),
        out_shape=[jax.ShapeDtypeStruct((t, d), F32), jax.ShapeDtypeStruct((t, d), BF16),
                   jax.ShapeDtypeStruct((t, d), BF16), jax.ShapeDtypeStruct((t, d), BF16),
                   jax.ShapeDtypeStruct((nm, 2 * d), F32), jax.ShapeDtypeStruct((1, d), F32)],
        in_specs=[pl.BlockSpec((tm, d), row), pl.BlockSpec((tm, d), row), pl.BlockSpec((1, d), fix),
                  pl.BlockSpec((nm, 2 * d), fix), _wspec(wq_g), _wspec(wo_g)],
        out_specs=[pl.BlockSpec((tm, d), row)] * 4 + [pl.BlockSpec((nm, 2 * d), fix),
                                                      pl.BlockSpec((1, d), fix)],
        compiler_params=_params("arbitrary"),
    )(x, dxn, gain.reshape(1, d), kv, wq_g, wo_g)


def loss_head(x, target, gain, *, name, tm=None):
    t, d = x.shape
    tm = _row_tile(t, tm)

    def body(x_ref, t_ref, g_ref, dx_ref, dg_ref, loss_ref):
        @pl.when(pl.program_id(0) == 0)
        def _():
            dg_ref[...] = jnp.zeros_like(dg_ref)
            loss_ref[...] = jnp.zeros_like(loss_ref)
        g = g_ref[...]
        xhat, r = _rms_fwd(x_ref[...], None)
        err = xhat * g - t_ref[...]
        loss_ref[...] += 0.5 * jnp.sum(jnp.sum(err * err, axis=-1, keepdims=True) / d,
                                       axis=0, keepdims=True)
        dx, dg = _rms_bwd(xhat, r, g, err / d)
        dx_ref[...] = dx
        dg_ref[...] += dg

    row = lambda i: (i, 0)
    fix = lambda i: (0, 0)
    return pl.pallas_call(
        body, name=name, grid=(t // tm,),
        out_shape=[jax.ShapeDtypeStruct((t, d), F32), jax.ShapeDtypeStruct((1, d), F32),
                   jax.ShapeDtypeStruct((1, 1), F32)],
        in_specs=[pl.BlockSpec((tm, d), row), pl.BlockSpec((tm, d), row), pl.BlockSpec((1, d), fix)],
        out_specs=[pl.BlockSpec((tm, d), row), pl.BlockSpec((1, d), fix), pl.BlockSpec((1, 1), fix)],
        compiler_params=_params("arbitrary"),
    )(x, target, gain.reshape(1, d))


def _adamw_math(w, g, m, v):
    m = ADAM_B1 * m + (1.0 - ADAM_B1) * g
    v = ADAM_B2 * v + (1.0 - ADAM_B2) * (g * g)
    m_hat = m / (1.0 - ADAM_B1 ** ADAM_STEP)
    v_hat = v / (1.0 - ADAM_B2 ** ADAM_STEP)
    delta = -ADAM_LR * (m_hat / (jnp.sqrt(v_hat) + ADAM_EPS) + ADAM_WD * w)
    return delta, m, v


def adamw_sharded(own, lands, w, m, v, me_arr, *, name):
    nl, r, c = w.shape
    assert nl == len(own) == len(lands) == 2
    tr = next(cand for cand in (256, 176, 128, r) if r % cand == 0)
    nr = r // tr

    def body(me_ref, o0, o1, l0, l1, w_ref, m_ref, v_ref, g_out, d_out, m_out, v_out):
        def total(o_ref, l_ref):
            acc = o_ref[...].astype(F32)
            for p in range(N_DEV - 1):
                acc = acc + l_ref[p].astype(F32)
            return acc
        g = jnp.where(pl.program_id(0) == 0, total(o0, l0), total(o1, l1))
        delta, mn, vn = _adamw_math(w_ref[...], g, m_ref[...], v_ref[...])
        g_out[...] = g
        d_out[...] = delta
        m_out[...] = mn
        v_out[...] = vn

    row0 = lambda l, i: jnp.where(l == 0, i, nr - 1)
    row1 = lambda l, i: jnp.where(l == 1, i, 0)
    blk = pl.BlockSpec((None, tr, c), lambda l, i, me: (l, i, 0))
    grid_spec = pltpu.PrefetchScalarGridSpec(
        num_scalar_prefetch=1, grid=(nl, nr),
        in_specs=[pl.BlockSpec((None, tr, c), lambda l, i, me: (me[0], row0(l, i), 0)),
                  pl.BlockSpec((None, tr, c), lambda l, i, me: (me[0], row1(l, i), 0)),
                  pl.BlockSpec((N_DEV - 1, tr, c), lambda l, i, me: (0, row0(l, i), 0)),
                  pl.BlockSpec((N_DEV - 1, tr, c), lambda l, i, me: (0, row1(l, i), 0)),
                  blk, blk, blk],
        out_specs=[blk] * 4)
    return pl.pallas_call(
        body, name=name, grid_spec=grid_spec,
        out_shape=[jax.ShapeDtypeStruct((nl, r, c), F32)] * 4,
        compiler_params=_params("arbitrary", "arbitrary"),
    )(me_arr, own[0], own[1], lands[0], lands[1], w, m, v)


def adamw_flat(g, w, m, v, *, name):
    def body(g_ref, w_ref, m_ref, v_ref, d_out, m_out, v_out):
        delta, mn, vn = _adamw_math(w_ref[...], g_ref[...], m_ref[...], v_ref[...])
        d_out[...] = delta
        m_out[...] = mn
        v_out[...] = vn

    return pl.pallas_call(
        body, name=name, out_shape=[jax.ShapeDtypeStruct(w.shape, F32)] * 3,
        in_specs=[VMEM_SPEC] * 4, out_specs=[VMEM_SPEC] * 3,
        compiler_params=pltpu.CompilerParams(vmem_limit_bytes=VMEM_LIMIT),
    )(g, w, m, v)


def cast_into_slot(a, layer, me_arr, *, name, dtype=None):
    dtype = BF16 if dtype is None else dtype
    _, r, c = a.shape
    tr = next(cand for cand in (256, 176, 128, r) if r % cand == 0)

    def body(me_ref, a_ref, o_ref):
        o_ref[...] = a_ref[...].astype(dtype)

    grid_spec = pltpu.PrefetchScalarGridSpec(
        num_scalar_prefetch=1, grid=(r // tr,),
        in_specs=[pl.BlockSpec((None, tr, c), lambda i, me: (layer, i, 0))],
        out_specs=pl.BlockSpec((None, tr, c), lambda i, me: (me[0], i, 0)))
    return pl.pallas_call(
        body, name=name, grid_spec=grid_spec,
        out_shape=jax.ShapeDtypeStruct((N_DEV, r, c), dtype),
        compiler_params=_params("parallel"),
    )(me_arr, a)


def _pack(arrs, rows):
    flat = jnp.concatenate([a.reshape(-1).astype(F32) for a in arrs])
    pad = rows * 128 - flat.shape[0]
    assert pad >= 0
    if pad:
        flat = jnp.concatenate([flat, jnp.zeros((pad,), F32)])
    return flat.reshape(rows, 128)


def _unpack(packed, shapes):
    flat = packed.reshape(-1)
    out, pos = [], 0
    for s in shapes:
        n = 1
        for dim in s:
            n *= dim
        out.append(flat[pos:pos + n].reshape(s))
        pos += n
    return out


def _rows_for(shapes):
    n = 0
    for s in shapes:
        k = 1
        for dim in s:
            k *= dim
        n += k
    return -(-n // 1024) * 8


GATHER_GROUPS = (("ffn1", ("ffn1_w_in", "ffn1_w_out")),
                 ("mid", ("mix_w_in", "mix_w_out", "xattn_wkv", "xattn_wq", "xattn_wo")),
                 ("ffn2", ("ffn2_w_in", "ffn2_w_out")))
SMALL_REPL = ["norm_ffn1", "norm_mix", "sgu_norm_g", "sgu_w", "sgu_b", "cconv_ln_g", "cconv_ln_b",
              "pool_w", "pool_scale", "norm_xattn", "norm_mem", "norm_ffn2", "norm_final"]
SMALL_SHARD = ["sconv_w", "cconv_w"]
TRANSPOSED = ("ffn1_w_in", "ffn2_w_in")
WEIGHTS = ["norm_ffn1", "ffn1_w_in", "ffn1_w_out", "norm_mix", "mix_w_in", "sconv_w", "sgu_norm_g",
           "sgu_w", "sgu_b", "cconv_w", "cconv_ln_g", "cconv_ln_b", "pool_w", "pool_scale", "mix_w_out",
           "norm_xattn", "norm_mem", "xattn_wq", "xattn_wkv", "xattn_wo", "norm_ffn2", "ffn2_w_in",
           "ffn2_w_out", "norm_final"]


def kernel(x, mem, norm_ffn1, ffn1_w_in, ffn1_w_out, norm_mix, mix_w_in, sconv_w, sgu_norm_g, sgu_w, sgu_b, cconv_w, cconv_ln_g, cconv_ln_b, pool_w, pool_scale, mix_w_out, norm_xattn, norm_mem, xattn_wq, xattn_wkv, xattn_wo, norm_ffn2, ffn2_w_in, ffn2_w_out, norm_final, loss_target, m_norm_ffn1, m_ffn1_w_in, m_ffn1_w_out, m_norm_mix, m_mix_w_in, m_sconv_w, m_sgu_norm_g, m_sgu_w, m_sgu_b, m_cconv_w, m_cconv_ln_g, m_cconv_ln_b, m_pool_w, m_pool_scale, m_mix_w_out, m_norm_xattn, m_norm_mem, m_xattn_wq, m_xattn_wkv, m_xattn_wo, m_norm_ffn2, m_ffn2_w_in, m_ffn2_w_out, m_norm_final, v_norm_ffn1, v_ffn1_w_in, v_ffn1_w_out, v_norm_mix, v_mix_w_in, v_sconv_w, v_sgu_norm_g, v_sgu_w, v_sgu_b, v_cconv_w, v_cconv_ln_g, v_cconv_ln_b, v_pool_w, v_pool_scale, v_mix_w_out, v_norm_xattn, v_norm_mem, v_xattn_wq, v_xattn_wkv, v_xattn_wo, v_norm_ffn2, v_ffn2_w_in, v_ffn2_w_out, v_norm_final):
    args = dict(locals())
    wts = {n: args[n] for n in WEIGHTS}
    mom = {n: args["m_" + n] for n in WEIGHTS}
    var = {n: args["v_" + n] for n in WEIGHTS}
    for n in TRANSPOSED:
        wts[n], mom[n], var[n] = (jnp.swapaxes(a, 1, 2) for a in (wts[n], mom[n], var[n]))
    x0 = x[0]
    mem0 = mem[0]
    target = loss_target[0]
    t, d = x0.shape
    nl = norm_ffn1.shape[0]
    w = MIX_W
    me = _my_index()

    me_arr = jnp.reshape(me, (1,)).astype(jnp.int32)

    small_g = all_gather([sconv_w, cconv_w], name="gather_conv_taps")
    sconv_full = jnp.transpose(small_g[0], (1, 2, 0, 3)).reshape(nl, SCONV_K, w)
    cconv_full = jnp.transpose(small_g[1], (1, 2, 0, 3)).reshape(nl, CCONV_K, w)
    pending = {}
    token = small_g[1]
    for l in range(nl):
        for gname, members in GATHER_GROUPS:
            gs = [cast_into_slot(wts[n], l, me_arr, name=f"cast_{n}{l}") for n in members]
            send, recv, gs, token = gather_start(gs, token, name=f"gather_start_{gname}{l}")
            pending[gname, l] = (members, gs, send, recv)
    wg = [dict() for _ in range(nl)]

    def arrive(gname, l, after):
        members, gs, send, recv = pending.pop((gname, l))
        gs = gather_wait(gs, send, recv, after, name=f"gather_wait_{gname}{l}")
        gs = sibling_forward(gs, name=f"gather_forward_{gname}{l}")
        wg[l].update(zip(members, gs))
    sconv_pad = jnp.pad(sconv_full, ((0, 0), (0, 8 - SCONV_K), (0, 0)))
    cconv_pad = jnp.pad(cconv_full, ((0, 0), (0, 32 - CCONV_K), (0, 0)))
    zeros_w = jnp.zeros((nl, w), F32)
    vecs = jnp.stack([sgu_norm_g, cconv_ln_g, cconv_ln_b, pool_scale] + [zeros_w] * 4, axis=1)
    wt = jnp.tril(sgu_w).astype(BF16)
    bexp = jnp.repeat(jnp.swapaxes(sgu_b, 1, 2), w // N_HEADS, axis=2)
    eye = jnp.eye(4, dtype=F32)
    pbd = jnp.einsum("lgcd,gh->lgchd", pool_w, eye).reshape(nl, w, w).astype(BF16)

    def mixer_args(l):
        return sconv_pad[l], cconv_pad[l], vecs[l], wt[l], bexp[l], pbd[l]

    saved = []
    xc = x0
    after = token
    for l in range(nl):
        s = {"x_ffn1": xc}
        arrive("ffn1", l, after)
        xc, s["gu_ffn1"] = ffn_fwd(xc, norm_ffn1[l], wg[l]["ffn1_w_in"], wg[l]["ffn1_w_out"],
                                   name=f"ffn1_fwd{l}", tm=FFN_FWD_TILE)
        s["x_mix"] = xc
        arrive("mid", l, xc)
        z = mm_rows(xc, wg[l]["mix_w_in"], "col", gain=norm_mix[l], name=f"mix_in{l}")
        y = mixer_fwd(z, *mixer_args(l), name=f"mixer_fwd{l}")
        s["z"], s["y"] = z, y
        xc = mm_rows(y, wg[l]["mix_w_out"], "row", residual=xc, name=f"mix_out{l}")
        s["x_att"] = xc
        kv = mm_rows(mem0, wg[l]["xattn_wkv"], "col", gain=norm_mem[l], name=f"kv{l}")
        s["kv"] = kv
        xc = xattn_fwd(xc, norm_xattn[l], kv, wg[l]["xattn_wq"], wg[l]["xattn_wo"], name=f"xattn_fwd{l}")
        s["x_ffn2"] = xc
        arrive("ffn2", l, xc)
        xc, s["gu_ffn2"] = ffn_fwd(xc, norm_ffn2[l], wg[l]["ffn2_w_in"], wg[l]["ffn2_w_out"],
                                   name=f"ffn2_fwd{l}", tm=FFN_FWD_TILE)
        after = xc
        saved.append(s)

    dx, g_norm_final, loss_local = loss_head(xc, target, norm_final, name="loss_head")
    loss = lax.psum(loss_local[0, 0], ("x", "y", "c"))

    tm = _row_tile(t, TN_TILE)
    small ={n: [None] * nl for n in SMALL_REPL + SMALL_SHARD if n != "norm_final"}
    scattered = {}
    tie = [token]

    def send_grads(gname, l, grads):
        members = list(grads)
        send, recv, gs, lands, tie[0] = scatter_start(
            [grads[n] for n in members], tie[0], name=f"scatter_start_{gname}{l}")
        scattered[gname, l] = (members, gs, lands, send, recv)

    def tied(v):
        return v + tie[0][0, 0]

    names = SMALL_REPL + SMALL_SHARD
    small_pending = []

    def start_small():
        small_full = {n: jnp.stack(v) for n, v in small.items()}
        small_full["norm_final"] = g_norm_final[0]
        shapes = [small_full[n].shape for n in names]
        packed = _pack([small_full[n] for n in names], _rows_for(shapes))
        slot = cast_into_slot(packed[None], 0, me_arr, name="small_into_slot", dtype=F32)
        send, recv, gs, tie[0] = gather_start([slot], tie[0], name="small_gather_start", masks=ALL_MASKS)
        small_pending.append((gs, send, recv, shapes))

    def ffn_backward(which, l, x_in, dy, gu, gain):
        w_in, w_out = wg[l][which + "_w_in"], wg[l][which + "_w_out"]
        dx_, h_, act, dgu, dgn = ffn_bwd_rows(x_in, dy, gu, tied(gain), w_in, w_out,
                                              name=f"{which}_bwd{l}_rows")
        small["norm_" + which][l] = dgn[0]
        if which == "ffn1" and l == 0:
            start_small()
        send_grads(which + "_in", l,
                   {which + "_w_in": ffn_grad_w_in(h_, dgu, tie[0], name=f"{which}_bwd{l}_dwin")})
        send_grads(which + "_out", l,
                   {which + "_w_out": ffn_grad_w_out(act, dy, tie[0], name=f"{which}_bwd{l}_dwout")})
        return dx_

    for l in reversed(range(nl)):
        s = saved[l]
        wl = wg[l]
        dx = ffn_backward("ffn2", l, s["x_ffn2"], dx, s["gu_ffn2"], norm_ffn2[l])

        bg = {}
        dxn = dx
        dx, h, dq, o, dkv, dgn = xattn_bwd_rows(
            s["x_att"], dxn, tied(norm_xattn[l]), s["kv"], wl["xattn_wq"], wl["xattn_wo"],
            name=f"xattn_bwd{l}")
        small["norm_xattn"][l] = dgn[0]
        row_spec = pl.BlockSpec((tm, d), lambda s_, i: (i, 0))
        bg["xattn_wq"] = mm_tn(h, dq, nb=1, ka=d, nbk=d, tm=tm, m=t, a_spec=row_spec, b_spec=row_spec,
                               name=f"dwq{l}").reshape(N_DEV, d // N_DEV, d)
        bg["xattn_wo"] = mm_tn(o, dxn, nb=1, ka=d, nbk=d, tm=tm, m=t, a_spec=row_spec, b_spec=row_spec,
                               name=f"dwo{l}").reshape(N_DEV, d // N_DEV, d)
        _, mhat, dgn = mm_nt(dkv, wl["xattn_wkv"], "col", x=mem0, gain=norm_mem[l], name=f"dmem{l}")
        small["norm_mem"][l] = dgn[0]
        nm = mem0.shape[0]
        bg["xattn_wkv"] = mm_tn(mhat, dkv, nb=N_DEV, ka=d, nbk=2 * d // N_DEV, tm=nm, m=nm,
                                a_spec=pl.BlockSpec((nm, d), lambda s_, i: (0, 0)),
                                b_spec=pl.BlockSpec((nm, 2 * d // N_DEV), lambda s_, i: (0, s_)),
                                name=f"dwkv{l}")
        send_grads("xattn", l, bg)

        bg = {}
        dxn = dx
        bg["mix_w_out"] = mm_tn(s["y"], dxn, nb=1, ka=d, nbk=d, tm=tm, m=t, a_spec=row_spec,
                                b_spec=row_spec, name=f"dwmo{l}").reshape(N_DEV, d // N_DEV, d)
        dy = mm_nt(dxn, wl["mix_w_out"], "row", name=f"dy_mix{l}")
        dz, gvec, gcc, gwt, gb, gpbd = mixer_bwd(s["z"], dy, *mixer_args(l), name=f"mixer_bwd{l}")
        small["sconv_w"][l] = gvec[0:SCONV_K]
        small["sgu_norm_g"][l] = gvec[3]
        small["cconv_ln_g"][l] = gvec[4]
        small["cconv_ln_b"][l] = gvec[5]
        small["pool_scale"][l] = gvec[6]
        small["cconv_w"][l] = gcc[0:CCONV_K]
        small["sgu_w"][l] = gwt
        small["sgu_b"][l] = jnp.transpose(gb[:, 0:N_HEADS])
        gw = w // 4
        small["pool_w"][l] = jnp.stack([gpbd[g * gw:(g + 1) * gw, g * gw:(g + 1) * gw] for g in range(4)])
        dx, h, dgn = mm_nt(dz, wl["mix_w_in"], "col", x=s["x_mix"], gain=tied(norm_mix[l]), dx_in=dxn,
                           name=f"dh_mix{l}")
        small["norm_mix"][l] = dgn[0]
        th = _row_tile(t, TN_TILE // 2)
        bg["mix_w_in"] = mm_tn(h, dz, nb=1, ka=d, nbk=N_DEV * w, tm=th, m=t, col_slots=N_DEV,
                               a_spec=pl.BlockSpec((th, d), lambda s_, i: (i, 0)),
                               b_spec=pl.BlockSpec((th, N_DEV * w), lambda s_, i: (i, 0)), name=f"dwmi{l}")
        send_grads("mix", l, bg)

        dx = ffn_backward("ffn1", l, s["x_ffn1"], dx, s["gu_ffn1"], norm_ffn1[l])

    out = {}

    def finish(gname, after):
        own, land = {}, {}
        for l in reversed(range(nl)):
            members, gs, lands, send, recv = scattered.pop((gname, l))
            gs, lands = scatter_wait(gs, lands, send, recv, after, name=f"scatter_wait_{gname}{l}")
            for n, g_, l_ in zip(members, gs, lands):
                own.setdefault(n, {})[l] = g_
                land.setdefault(n, {})[l] = l_
        for n in own:
            out[n] = adamw_sharded([own[n][l] for l in range(nl)], [land[n][l] for l in range(nl)],
                                   wts[n], mom[n], var[n], me_arr, name="adamw_" + n)
            after = out[n][1]
        return after

    after = tie[0]
    for gname in ("ffn2_in", "ffn2_out", "xattn", "mix"):
        after = finish(gname, after)
    (gs, send, recv, shapes), = small_pending
    gs = gather_wait(gs, send, recv, after, name="small_gather_wait", masks=ALL_MASKS)
    summed = sum_slots(gs[0], name="small_sum")
    gsm = dict(zip(names, _unpack(summed, shapes)))
    after = finish("ffn1_in", summed)
    finish("ffn1_out", after)
    repl_shapes = [wts[n].shape for n in SMALL_REPL]
    rows_r = _rows_for(repl_shapes)
    dl, mn, vn = adamw_flat(_pack([gsm[n] for n in SMALL_REPL], rows_r),
                            _pack([wts[n] for n in SMALL_REPL], rows_r),
                            _pack([mom[n] for n in SMALL_REPL], rows_r),
                            _pack([var[n] for n in SMALL_REPL], rows_r), name="adamw_small")
    for n, a, b, c in zip(SMALL_REPL, _unpack(dl, repl_shapes), _unpack(mn, repl_shapes),
                          _unpack(vn, repl_shapes)):
        out[n] = (gsm[n], a, b, c)
    cs = w // N_DEV
    gsh = {n: lax.dynamic_slice_in_dim(gsm[n], me * cs, cs, axis=2) for n in SMALL_SHARD}
    sh_shapes = [wts[n].shape for n in SMALL_SHARD]
    rows_s = _rows_for(sh_shapes)
    dl, mn, vn = adamw_flat(_pack([gsh[n] for n in SMALL_SHARD], rows_s),
                            _pack([wts[n] for n in SMALL_SHARD], rows_s),
                            _pack([mom[n] for n in SMALL_SHARD], rows_s),
                            _pack([var[n] for n in SMALL_SHARD], rows_s), name="adamw_small_sharded")
    for n, a, b, c in zip(SMALL_SHARD, _unpack(dl, sh_shapes), _unpack(mn, sh_shapes),
                          _unpack(vn, sh_shapes)):
        out[n] = (gsh[n], a, b, c)
    for n in TRANSPOSED:
        out[n] = tuple(jnp.swapaxes(a, 1, 2) for a in out[n])

    grad_x = dx.reshape(1, t, d)
    return (loss, grad_x, *[out[n][0] for n in WEIGHTS], *[out[n][1] for n in WEIGHTS],
            *[out[n][2] for n in WEIGHTS], *[out[n][3] for n in WEIGHTS])
```

```python
import functools

import jax
import jax.numpy as jnp
from jax import lax
from jax.experimental import pallas as pl
from jax.experimental.pallas import tpu as pltpu

F32 = jnp.float32
BF16 = jnp.bfloat16
MESH = pl.DeviceIdType.MESH
N_DEV = 8
EPS = 1e-6
HALO = 32
SGU_CHUNK = 128
CCONV_K = 31
SCONV_K = 3
MIX_W = 256
N_HEADS = 4
VMEM_LIMIT = 56 * 1024 * 1024
ROW_TILE = 512
TN_TILE = 2048
FFN_FWD_TILE = 1024
FFN_BWD_SPLIT = 2
FFN_FWD_SPLIT = 2
MIX_TILE = 512

ADAM_LR = 0.001
ADAM_B1 = 0.9
ADAM_B2 = 0.999
ADAM_EPS = 1e-08
ADAM_WD = 0.01
ADAM_STEP = 10

HBM_SPEC = pl.BlockSpec(memory_space=pltpu.HBM)
VMEM_SPEC = pl.BlockSpec(memory_space=pltpu.VMEM)


def _params(*sem):
    return pltpu.CompilerParams(dimension_semantics=tuple(sem), vmem_limit_bytes=VMEM_LIMIT)


def _row_tile(m, pref=None):
    t = min(m, ROW_TILE if pref is None else pref)
    assert m % t == 0, (m, t)
    return t


def _my_index():
    return lax.axis_index("x") * 4 + lax.axis_index("y") * 2 + lax.axis_index("c")


def _peer(mask):
    x, y, c = lax.axis_index("x"), lax.axis_index("y"), lax.axis_index("c")
    px = 1 - x if mask & 4 else x
    py = 1 - y if mask & 2 else y
    pc = 1 - c if mask & 1 else c
    return (px, py, pc), px * 4 + py * 2 + pc


def all_gather(arrs, name):
    n = len(arrs)

    def body(*refs):
        ins, outs = refs[:n], refs[n:2 * n]
        send_sems, recv_sems, loc_sems = refs[2 * n:]
        me = _my_index()
        local = []
        for i in range(n):
            cp = pltpu.make_async_copy(ins[i], outs[i].at[me], loc_sems.at[i])
            cp.start()
            local.append(cp)
        sends = []
        for i in range(n):
            for m in range(1, N_DEV):
                peer, _ = _peer(m)
                cp = pltpu.make_async_remote_copy(
                    src_ref=ins[i], dst_ref=outs[i].at[me],
                    send_sem=send_sems.at[i, m - 1], recv_sem=recv_sems.at[i, m - 1],
                    device_id=peer, device_id_type=MESH)
                cp.start()
                sends.append(cp)
        for i in range(n):
            for m in range(1, N_DEV):
                peer, pidx = _peer(m)
                pltpu.make_async_remote_copy(
                    src_ref=ins[i], dst_ref=outs[i].at[pidx],
                    send_sem=send_sems.at[i, m - 1], recv_sem=recv_sems.at[i, m - 1],
                    device_id=peer, device_id_type=MESH).wait_recv()
        for cp in sends:
            cp.wait_send()
        for cp in local:
            cp.wait()

    return pl.pallas_call(
        body, name=name,
        out_shape=[jax.ShapeDtypeStruct((N_DEV,) + a.shape, a.dtype) for a in arrs],
        in_specs=[HBM_SPEC] * n, out_specs=[HBM_SPEC] * n,
        scratch_shapes=[pltpu.SemaphoreType.DMA((n, N_DEV - 1)),
                        pltpu.SemaphoreType.DMA((n, N_DEV - 1)),
                        pltpu.SemaphoreType.DMA((n,))],
    )(*arrs)


SEM_SPEC = pl.BlockSpec(memory_space=pltpu.SEMAPHORE)
ANY_SPEC = pl.BlockSpec(memory_space=pl.ANY)
SIDE_EFFECT = pltpu.SideEffectType.DATAFLOW_SIDE_EFFECTING


def _hbm(a):
    return pltpu.with_memory_space_constraint(a, pltpu.HBM)


def _sem_pairs(n):
    return (pltpu.SemaphoreType.DMA((n * (N_DEV - 1),)), pltpu.SemaphoreType.DMA((n * (N_DEV - 1),)))


def _sem(i, m):
    return i * (N_DEV - 1) + m - 1


def _gather_copy(g_ref, i, m, send_sems, recv_sems, origin):
    peer, _ = _peer(m)
    return pltpu.make_async_remote_copy(
        src_ref=g_ref.at[origin], dst_ref=g_ref.at[origin],
        send_sem=send_sems.at[_sem(i, m)], recv_sem=recv_sems.at[_sem(i, m)],
        device_id=peer, device_id_type=MESH)


GATHER_MASKS = (1, 2, 4, 6)
FORWARD_MASKS = (2, 4, 6)


ALL_MASKS = tuple(range(1, N_DEV))


def gather_start(gs, after, name, masks=GATHER_MASKS):
    n = len(gs)

    def body(*refs):
        g_in = refs[:n]
        send_sems, recv_sems = refs[n + 1], refs[n + 2]
        token = refs[-1]
        me = _my_index()
        for i in range(n):
            for m in masks:
                _gather_copy(g_in[i], i, m, send_sems, recv_sems, me).start()
        token[...] = jnp.zeros_like(token)

    outs = pl.pallas_call(
        body, name=name,
        out_shape=(*_sem_pairs(n), *[pltpu.HBM(g.shape, g.dtype) for g in gs],
                   jax.ShapeDtypeStruct((8, 128), F32)),
        in_specs=[HBM_SPEC] * n + [ANY_SPEC],
        out_specs=(SEM_SPEC, SEM_SPEC, *[HBM_SPEC] * n, VMEM_SPEC),
        input_output_aliases={i: 2 + i for i in range(n)},
        compiler_params=pltpu.CompilerParams(has_side_effects=SIDE_EFFECT),
    )(*[_hbm(g) for g in gs], after)
    return outs[0], outs[1], list(outs[2:2 + n]), outs[-1]


def gather_wait(gs, send_sems, recv_sems, after, name, masks=GATHER_MASKS):
    n = len(gs)

    def body(*refs):
        g_in = refs[:n]
        send, recv = refs[n], refs[n + 1]
        me = _my_index()
        for i in range(n):
            for m in masks:
                _, pidx = _peer(m)
                _gather_copy(g_in[i], i, m, send, recv, me).wait_send()
                _gather_copy(g_in[i], i, m, send, recv, pidx).wait_recv()

    outs = pl.pallas_call(
        body, name=name,
        out_shape=[pltpu.HBM(g.shape, g.dtype) for g in gs],
        in_specs=[HBM_SPEC] * n + [SEM_SPEC, SEM_SPEC, ANY_SPEC],
        out_specs=[HBM_SPEC] * n,
        input_output_aliases={i: i for i in range(n)},
        compiler_params=pltpu.CompilerParams(has_side_effects=SIDE_EFFECT),
    )(*gs, send_sems, recv_sems, after)
    return list(outs)


def sibling_forward(gs, name):
    n = len(gs)
    nf = len(FORWARD_MASKS)

    def body(*refs):
        g_in = refs[:n]
        send_sems, recv_sems = refs[2 * n:]
        x, y, c = lax.axis_index("x"), lax.axis_index("y"), lax.axis_index("c")
        sibling = (x, y, 1 - c)

        def copy(i, k, origin):
            return pltpu.make_async_remote_copy(
                src_ref=g_in[i].at[origin], dst_ref=g_in[i].at[origin],
                send_sem=send_sems.at[i * nf + k], recv_sem=recv_sems.at[i * nf + k],
                device_id=sibling, device_id_type=MESH)
        sends = []
        for i in range(n):
            for k, m in enumerate(FORWARD_MASKS):
                _, origin = _peer(m)
                cp = copy(i, k, origin)
                cp.start()
                sends.append(cp)
        for i in range(n):
            for k, m in enumerate(FORWARD_MASKS):
                _, origin = _peer(m ^ 1)
                copy(i, k, origin).wait_recv()
        for cp in sends:
            cp.wait_send()

    outs = pl.pallas_call(
        body, name=name,
        out_shape=[jax.ShapeDtypeStruct(g.shape, g.dtype) for g in gs],
        in_specs=[HBM_SPEC] * n, out_specs=[HBM_SPEC] * n,
        input_output_aliases={i: i for i in range(n)},
        scratch_shapes=[pltpu.SemaphoreType.DMA((n * nf,)), pltpu.SemaphoreType.DMA((n * nf,))],
    )(*gs)
    return list(outs)


def _scatter_copy(g_ref, l_ref, i, m, send_sems, recv_sems):
    peer, pidx = _peer(m)
    return pltpu.make_async_remote_copy(
        src_ref=g_ref.at[pidx], dst_ref=l_ref.at[m - 1],
        send_sem=send_sems.at[_sem(i, m)], recv_sem=recv_sems.at[_sem(i, m)],
        device_id=peer, device_id_type=MESH)


def scatter_start(grads, after, name):
    n = len(grads)
    lands = [lax.empty((N_DEV - 1,) + g.shape[1:], g.dtype) for g in grads]

    def body(*refs):
        g_in, l_in = refs[:n], refs[n:2 * n]
        send_sems, recv_sems = refs[2 * n + 1], refs[2 * n + 2]
        token = refs[-1]
        for i in range(n):
            for m in range(1, N_DEV):
                _scatter_copy(g_in[i], l_in[i], i, m, send_sems, recv_sems).start()
        token[...] = jnp.zeros_like(token)

    outs = pl.pallas_call(
        body, name=name,
        out_shape=(*_sem_pairs(n), *[pltpu.HBM(g.shape, g.dtype) for g in grads],
                   *[pltpu.HBM(l.shape, l.dtype) for l in lands], jax.ShapeDtypeStruct((8, 128), F32)),
        in_specs=[HBM_SPEC] * (2 * n) + [ANY_SPEC],
        out_specs=(SEM_SPEC, SEM_SPEC, *[HBM_SPEC] * (2 * n), VMEM_SPEC),
        input_output_aliases={i: 2 + i for i in range(2 * n)},
        compiler_params=pltpu.CompilerParams(has_side_effects=SIDE_EFFECT),
    )(*[_hbm(g) for g in grads], *[_hbm(l) for l in lands], after)
    return outs[0], outs[1], list(outs[2:2 + n]), list(outs[2 + n:2 + 2 * n]), outs[-1]


def scatter_wait(grads, lands, send_sems, recv_sems, after, name):
    n = len(grads)

    def body(*refs):
        g_in, l_in = refs[:n], refs[n:2 * n]
        send, recv = refs[2 * n], refs[2 * n + 1]
        for i in range(n):
            for m in range(1, N_DEV):
                cp = _scatter_copy(g_in[i], l_in[i], i, m, send, recv)
                cp.wait_send()
                cp.wait_recv()

    outs = pl.pallas_call(
        body, name=name,
        out_shape=[pltpu.HBM(a.shape, a.dtype) for a in list(grads) + list(lands)],
        in_specs=[HBM_SPEC] * (2 * n) + [SEM_SPEC, SEM_SPEC, ANY_SPEC],
        out_specs=[HBM_SPEC] * (2 * n),
        input_output_aliases={i: i for i in range(2 * n)},
        compiler_params=pltpu.CompilerParams(has_side_effects=SIDE_EFFECT),
    )(*grads, *lands, send_sems, recv_sems, after)
    return list(outs[:n]), list(outs[n:])


def sum_slots(g, name):
    _, r, c = g.shape

    def body(g_ref, out_ref):
        acc = g_ref[0]
        for p in range(1, N_DEV):
            acc = acc + g_ref[p]
        out_ref[...] = acc

    return pl.pallas_call(
        body, name=name, out_shape=jax.ShapeDtypeStruct((r, c), F32),
        in_specs=[VMEM_SPEC], out_specs=VMEM_SPEC,
        compiler_params=pltpu.CompilerParams(vmem_limit_bytes=VMEM_LIMIT),
    )(g)


def _sigmoid(v):
    return 1.0 / (1.0 + jnp.exp(-v))


def _rms_fwd(xf, g):
    r = lax.rsqrt(jnp.mean(xf * xf, axis=-1, keepdims=True) + EPS)
    return xf * r, r


def _rms_bwd(xhat, r, g, dy):
    dg = jnp.sum(dy * xhat, axis=0, keepdims=True)
    dxh = dy * g
    dx = r * (dxh - xhat * jnp.mean(dxh * xhat, axis=-1, keepdims=True))
    return dx, dg


def _ln_stats(v):
    mu = jnp.mean(v, axis=-1, keepdims=True)
    vc = v - mu
    r = lax.rsqrt(jnp.mean(vc * vc, axis=-1, keepdims=True) + EPS)
    return vc * r, r


def _ln_bwd(xhat, r, dxh):
    return r * (dxh - jnp.mean(dxh, axis=-1, keepdims=True)
                - xhat * jnp.mean(dxh * xhat, axis=-1, keepdims=True))


def _dot(a, b):
    return jnp.dot(a, b, preferred_element_type=F32)


def _dot_nt(a, b):
    return lax.dot_general(a, b, (((1,), (1,)), ((), ())), preferred_element_type=F32)


def _dot_tn(a, b):
    return lax.dot_general(a, b, (((0,), (0,)), ((), ())), preferred_element_type=F32)


def _full_weight(w_ref, kind):
    assert kind == "row"
    p, a, b = w_ref.shape
    return w_ref[...].reshape(p * a, b)


def _wspec(wg):
    return pl.BlockSpec(wg.shape, lambda *_: (0, 0, 0))


def mm_rows(a, wg, kind, *, gain=None, residual=None, out_dtype=F32, name, tm=None):
    m, k = a.shape
    p, wa, wb = wg.shape
    n = p * wb if kind == "col" else wb
    tm = _row_tile(m, tm)
    has_gain, has_res = gain is not None, residual is not None

    def body(*refs):
        refs = list(refs)
        a_ref = refs.pop(0)
        g_ref = refs.pop(0) if has_gain else None
        w_ref = refs.pop(0)
        r_ref = refs.pop(0) if has_res else None
        o_ref = refs.pop(0)
        if has_gain:
            xhat, _ = _rms_fwd(a_ref[...].astype(F32), None)
            h = (xhat * g_ref[...]).astype(BF16)
        else:
            h = a_ref[...].astype(BF16)
        if kind == "col":
            for j in range(p):
                o = _dot(h, w_ref[j])
                if has_res:
                    o = o + r_ref[:, j * wb:(j + 1) * wb]
                o_ref[:, j * wb:(j + 1) * wb] = o.astype(out_dtype)
        else:
            o = _dot(h, _full_weight(w_ref, "row"))
            if has_res:
                o = o + r_ref[...]
            o_ref[...] = o.astype(out_dtype)

    operands = [a]
    in_specs = [pl.BlockSpec((tm, k), lambda i: (i, 0))]
    if has_gain:
        operands.append(gain.reshape(1, k))
        in_specs.append(pl.BlockSpec((1, k), lambda i: (0, 0)))
    operands.append(wg)
    in_specs.append(_wspec(wg))
    if has_res:
        operands.append(residual)
        in_specs.append(pl.BlockSpec((tm, n), lambda i: (i, 0)))
    return pl.pallas_call(
        body, name=name, grid=(m // tm,),
        out_shape=jax.ShapeDtypeStruct((m, n), out_dtype),
        in_specs=in_specs, out_specs=pl.BlockSpec((tm, n), lambda i: (i, 0)),
        compiler_params=_params("parallel"),
    )(*operands)


def mm_nt(dz, wg, kind, *, x=None, gain=None, dx_in=None, name, tm=None):
    m, n = dz.shape
    p, wa, wb = wg.shape
    k = wa if kind == "col" else p * wa
    tm = _row_tile(m, tm)
    epi = x is not None
    has_dx = dx_in is not None

    def body(*refs):
        refs = list(refs)
        dz_ref, w_ref = refs.pop(0), refs.pop(0)
        if epi:
            x_ref, g_ref = refs.pop(0), refs.pop(0)
            dxi_ref = refs.pop(0) if has_dx else None
            dx_ref, h_ref, dg_ref = refs
        else:
            (da_ref,) = refs
        dzb = dz_ref[...].astype(BF16)
        if kind == "col":
            da = _dot_nt(dzb[:, 0:wb], w_ref[0])
            for j in range(1, p):
                da = da + _dot_nt(dzb[:, j * wb:(j + 1) * wb], w_ref[j])
        else:
            da = _dot_nt(dzb, _full_weight(w_ref, "row"))
        if not epi:
            da_ref[...] = da
            return
        g = g_ref[...]
        xhat, r = _rms_fwd(x_ref[...].astype(F32), None)
        h_ref[...] = (xhat * g).astype(BF16)
        dx, dg = _rms_bwd(xhat, r, g, da)
        if has_dx:
            dx = dx + dxi_ref[...]
        dx_ref[...] = dx

        @pl.when(pl.program_id(0) == 0)
        def _():
            dg_ref[...] = jnp.zeros_like(dg_ref)
        dg_ref[...] += dg

    row = lambda i: (i, 0)
    operands = [dz, wg]
    in_specs = [pl.BlockSpec((tm, n), row), _wspec(wg)]
    if epi:
        operands += [x, gain.reshape(1, k)]
        in_specs += [pl.BlockSpec((tm, k), row), pl.BlockSpec((1, k), lambda i: (0, 0))]
        if has_dx:
            operands.append(dx_in)
            in_specs.append(pl.BlockSpec((tm, k), row))
        out_shape = [jax.ShapeDtypeStruct((m, k), F32), jax.ShapeDtypeStruct((m, k), BF16),
                     jax.ShapeDtypeStruct((1, k), F32)]
        out_specs = [pl.BlockSpec((tm, k), row), pl.BlockSpec((tm, k), row),
                     pl.BlockSpec((1, k), lambda i: (0, 0))]
    else:
        out_shape = jax.ShapeDtypeStruct((m, k), F32)
        out_specs = pl.BlockSpec((tm, k), row)
    return pl.pallas_call(
        body, name=name, grid=(m // tm,), out_shape=out_shape,
        in_specs=in_specs, out_specs=out_specs,
        compiler_params=_params("arbitrary"),
    )(*operands)


def mm_tn(a, b, *, nb, a_spec, b_spec, ka, nbk, tm, m, scale=1.0, out_dtype=BF16, col_slots=1,
          after=None, name):
    ni = m // tm
    assert col_slots == 1 or nb == 1
    cw = nbk // col_slots
    extra = [] if after is None else [after]

    def body(a_ref, b_ref, *rest):
        o_ref, acc = rest[len(extra):]
        i = pl.program_id(1)

        @pl.when(i == 0)
        def _():
            acc[...] = jnp.zeros_like(acc)
        acc[...] += _dot_tn(a_ref[...].astype(BF16), b_ref[...].astype(BF16))

        @pl.when(i == ni - 1)
        def _():
            if col_slots == 1:
                o_ref[...] = (acc[...] * scale).astype(out_dtype)
            else:
                for j in range(col_slots):
                    o_ref[j] = (acc[:, j * cw:(j + 1) * cw] * scale).astype(out_dtype)

    if col_slots == 1:
        out_shape = jax.ShapeDtypeStruct((nb, ka, nbk), out_dtype)
        out_spec = pl.BlockSpec((None, ka, nbk), lambda s, i: (s, 0, 0))
    else:
        out_shape = jax.ShapeDtypeStruct((col_slots, ka, cw), out_dtype)
        out_spec = pl.BlockSpec((col_slots, ka, cw), lambda s, i: (0, 0, 0))
    return pl.pallas_call(
        body, name=name, grid=(nb, ni), out_shape=out_shape,
        in_specs=[a_spec, b_spec] + [ANY_SPEC] * len(extra), out_specs=out_spec,
        scratch_shapes=[pltpu.VMEM((ka, nbk), F32)],
        compiler_params=_params("parallel", "arbitrary"),
    )(a, b, *extra)


def _ffn_specs(w_in_g, w_out_g, d):
    nf = w_in_g.shape[1]
    hr = w_out_g.shape[1]
    assert 2 * hr == nf
    w_in5 = w_in_g.reshape(2, 4, nf, d)
    w_out5 = w_out_g.reshape(4, 2, hr, d)
    in_spec = pl.BlockSpec((2, None, nf, d), lambda i, j: (0, j, 0, 0))
    out_spec = pl.BlockSpec((None, 2, hr, d), lambda i, j: (j, 0, 0, 0))
    return w_in5, w_out5, in_spec, out_spec, nf


def ffn_fwd(x, gain, w_in_g, w_out_g, *, name, tm=None):
    t, d = x.shape
    tm = _row_tile(t, tm)
    w_in5, w_out5, wi_spec, wo_spec, nf = _ffn_specs(w_in_g, w_out_g, d)

    def body(x_ref, g_ref, wi_ref, wo_ref, o_ref, gu_ref, h_scr, acc):
        j = pl.program_id(1)

        @pl.when(j == 0)
        def _():
            xhat, _ = _rms_fwd(x_ref[...], None)
            h_scr[...] = (xhat * g_ref[...]).astype(BF16)
            acc[...] = jnp.zeros_like(acc)
        wo = wo_ref[...].reshape(nf, d)

        def project(rows):
            h = h_scr[rows]
            return _dot_nt(h, wi_ref[0]), _dot_nt(h, wi_ref[1])

        sub = tm // FFN_FWD_SPLIT
        parts = [slice(k * sub, (k + 1) * sub) for k in range(FFN_FWD_SPLIT)]
        gt, up = project(parts[0])
        for k, rows in enumerate(parts):
            if k + 1 < len(parts):
                nxt = project(parts[k + 1])
            gu_ref[0, rows] = gt.astype(BF16)
            gu_ref[1, rows] = up.astype(BF16)
            act = (gt * _sigmoid(gt) * up).astype(BF16)
            acc[rows] += _dot(act, wo)
            if k + 1 < len(parts):
                gt, up = nxt

        @pl.when(j == 3)
        def _():
            o_ref[...] = x_ref[...] + 0.5 * acc[...]

    return pl.pallas_call(
        body, name=name, grid=(t // tm, 4),
        out_shape=[jax.ShapeDtypeStruct((t, d), F32), jax.ShapeDtypeStruct((2, 4, t, nf), BF16)],
        in_specs=[pl.BlockSpec((tm, d), lambda i, j: (i, 0)),
                  pl.BlockSpec((1, d), lambda i, j: (0, 0)), wi_spec, wo_spec],
        out_specs=[pl.BlockSpec((tm, d), lambda i, j: (i, 0)),
                   pl.BlockSpec((2, None, tm, nf), lambda i, j: (0, j, i, 0))],
        scratch_shapes=[pltpu.VMEM((tm, d), BF16), pltpu.VMEM((tm, d), F32)],
        compiler_params=_params("parallel", "arbitrary"),
    )(x, gain.reshape(1, d), w_in5, w_out5)


def ffn_bwd_rows(x, dy, gu, gain, w_in_g, w_out_g, *, name, tm=None):
    t, d = x.shape
    tm = _row_tile(t, tm)
    w_in5, w_out5, wi_spec, wo_spec, nf = _ffn_specs(w_in_g, w_out_g, d)

    def body(x_ref, dy_ref, gu_ref, g_ref, wi_ref, wo_ref, dx_ref, h_ref, act_ref, dgu_ref, dg_ref,
             dh_acc, dyh_scr):
        i, j = pl.program_id(0), pl.program_id(1)

        @pl.when(j == 0)
        def _():
            xhat, _ = _rms_fwd(x_ref[...], None)
            h_ref[...] = (xhat * g_ref[...]).astype(BF16)
            dyh_scr[...] = (0.5 * dy_ref[...]).astype(BF16)
            dh_acc[...] = jnp.zeros_like(dh_acc)
        wo = wo_ref[...].reshape(nf, d)

        def gates(rows):
            gt = gu_ref[0, rows].astype(F32)
            up = gu_ref[1, rows].astype(F32)
            sg = _sigmoid(gt)
            silu = gt * sg
            act_ref[rows] = (silu * up).astype(BF16)
            return up * (sg * (1.0 + gt * (1.0 - sg))), silu

        def grads(rows, dact, dsilu_up, silu):
            dgt = (dact * dsilu_up).astype(BF16)
            dup = (dact * silu).astype(BF16)
            dgu_ref[0, rows] = dgt
            dgu_ref[1, rows] = dup
            return dgt, dup

        sub = tm // FFN_BWD_SPLIT
        parts = [slice(k * sub, (k + 1) * sub) for k in range(FFN_BWD_SPLIT)]
        dact = _dot_nt(dyh_scr[parts[0]], wo)
        gate = gates(parts[0])
        for k, rows in enumerate(parts):
            if k + 1 < len(parts):
                dact_next = _dot_nt(dyh_scr[parts[k + 1]], wo)
            dgt, dup = grads(rows, dact, *gate)
            dh_acc[rows] += _dot(dgt, wi_ref[0]) + _dot(dup, wi_ref[1])
            if k + 1 < len(parts):
                gate = gates(parts[k + 1])
                dact = dact_next

        @pl.when(j == 3)
        def _():
            g = g_ref[...]
            xhat, r = _rms_fwd(x_ref[...], None)
            dx, dg = _rms_bwd(xhat, r, g, dh_acc[...])
            dx_ref[...] = dy_ref[...] + dx

            @pl.when(i == 0)
            def _():
                dg_ref[...] = jnp.zeros_like(dg_ref)
            dg_ref[...] += dg

    row = lambda i, j: (i, 0)
    return pl.pallas_call(
        body, name=name, grid=(t // tm, 4),
        out_shape=[jax.ShapeDtypeStruct((t, d), F32), jax.ShapeDtypeStruct((t, d), BF16),
                   jax.ShapeDtypeStruct((4, t, nf), BF16), jax.ShapeDtypeStruct((2, 4, t, nf), BF16),
                   jax.ShapeDtypeStruct((1, d), F32)],
        in_specs=[pl.BlockSpec((tm, d), row), pl.BlockSpec((tm, d), row),
                  pl.BlockSpec((2, None, tm, nf), lambda i, j: (0, j, i, 0)),
                  pl.BlockSpec((1, d), lambda i, j: (0, 0)), wi_spec, wo_spec],
        out_specs=[pl.BlockSpec((tm, d), row), pl.BlockSpec((tm, d), row),
                   pl.BlockSpec((None, tm, nf), lambda i, j: (j, i, 0)),
                   pl.BlockSpec((2, None, tm, nf), lambda i, j: (0, j, i, 0)),
                   pl.BlockSpec((1, d), lambda i, j: (0, 0))],
        scratch_shapes=[pltpu.VMEM((tm, d), F32), pltpu.VMEM((tm, d), BF16)],
        compiler_params=_params("arbitrary", "arbitrary"),
    )(x, dy, gu, gain.reshape(1, d), w_in5, w_out5)


def ffn_grad_w_in(h, dgu, after, *, name):
    t, d = h.shape
    nf = dgu.shape[-1]
    tm = _row_tile(t, TN_TILE)
    return mm_tn(dgu.reshape(8, t, nf), h, nb=8, ka=nf, nbk=d, tm=tm, m=t, after=after,
                 a_spec=pl.BlockSpec((None, tm, nf), lambda s, i: (s, i, 0)),
                 b_spec=pl.BlockSpec((tm, d), lambda s, i: (i, 0)), name=name)


def ffn_grad_w_out(act, dy, after, *, name):
    _, t, nf = act.shape
    d = dy.shape[1]
    tm = _row_tile(t, TN_TILE)
    d_w_out = mm_tn(act, dy, nb=4, ka=nf, nbk=d, tm=tm, m=t, scale=0.5, after=after,
                    a_spec=pl.BlockSpec((None, tm, nf), lambda s, i: (s, i, 0)),
                    b_spec=pl.BlockSpec((tm, d), lambda s, i: (i, 0)), name=name)
    return d_w_out.reshape(8, nf // 2, d)


def _lane_group(shape):
    return lax.shift_right_logical(lax.broadcasted_iota(jnp.int32, shape, 1), 6)


def _pool_count(t0, rows):
    t = (t0 + lax.broadcasted_iota(jnp.int32, (rows, MIX_W), 0) + 1).astype(F32)
    return jnp.minimum(t, _by_group(_lane_group((rows, MIX_W)), 2.0, 4.0, 8.0, 16.0))


def _by_group(grp, v0, v1, v2, v3):
    return jnp.where(grp == 0, v0, jnp.where(grp == 1, v1, jnp.where(grp == 2, v2, v3)))


def _sgu_mix(wt_ref, vnc):
    grp = _lane_group((SGU_CHUNK, MIX_W))
    out = jnp.zeros((SGU_CHUNK, MIX_W), F32)
    for hd in range(N_HEADS):
        out = jnp.where(grp == hd, _dot(wt_ref[hd], vnc), out)
    return out


def _pool_fwd(s1, s2, s3, t0, ts, lo):
    h = lo
    s2[h - 24:h + ts] = s1[h - 24:h + ts] + s1[h - 25:h + ts - 1]
    s3[h - 16:h + ts] = s2[h - 16:h + ts] + s2[h - 18:h + ts - 2]
    sum2 = s2[h:h + ts]
    sum4 = s3[h:h + ts]
    s2[h - 8:h + ts] = s3[h - 8:h + ts] + s3[h - 12:h + ts - 4]
    sum8 = s2[h:h + ts]
    sum16 = sum8 + s2[h - 8:h + ts - 8]
    grp = _lane_group((ts, MIX_W))
    return _by_group(grp, sum2, sum4, sum8, sum16) / _pool_count(t0, ts) - s1[h:h + ts]


def _make_shifts(src, sh, rows):
    for b in range(1, 8):
        sh[b, 0:rows] = src[b:b + rows]


def _rows_at(src, sh, start, n):
    a, b = divmod(start, 8)
    return src[8 * a:8 * a + n] if b == 0 else sh[b, 8 * a:8 * a + n]


def mixer_fwd(z, sconv, cconv, vecs, wt, bexp, pbd, *, name, ts=None):
    t = z.shape[0]
    ts = _row_tile(t, MIX_TILE if ts is None else ts)
    hl = HALO
    w = MIX_W
    nch = ts // SGU_CHUNK

    def body(zc, zp, sconv_ref, cconv_ref, vec_ref, wt_ref, bexp_ref, pbd_ref, y_ref, s1, s2, s3, sh):
        i = pl.program_id(0)
        has_prev = i > 0

        def col(ref, c):
            return ref[:, c * w:(c + 1) * w]

        def prev(c):
            return jnp.where(has_prev, col(zp, c), 0.0)

        s1[0:hl] = prev(1) * prev(2)
        s1[hl:hl + ts] = col(zc, 1) * col(zc, 2)
        cv = sconv_ref[0:1] * s1[hl - 2:hl - 2 + ts]
        for k in range(1, SCONV_K):
            cv = cv + sconv_ref[k:k + 1] * s1[hl - 2 + k:hl - 2 + k + ts]
        y_ref[:, 0:w] = (col(zc, 0) * cv).astype(BF16)

        xhat, _ = _ln_stats(col(zc, 4))
        vn = (xhat * vec_ref[0:1]).astype(BF16)
        for c in range(nch):
            rows = slice(c * SGU_CHUNK, (c + 1) * SGU_CHUNK)
            mixed = _sgu_mix(wt_ref, vn[rows]) + bexp_ref[...]
            y_ref[rows, w:2 * w] = (zc[rows, 3 * w:4 * w] * mixed).astype(BF16)

        s1[0:hl] = prev(5) * _sigmoid(prev(6))
        s1[hl:hl + ts] = col(zc, 5) * _sigmoid(col(zc, 6))
        off = hl - (CCONV_K - 1)
        _make_shifts(s1, sh, hl + ts - 8)
        cv = cconv_ref[0:1] * _rows_at(s1, sh, off, ts)
        for k in range(1, CCONV_K):
            cv = cv + cconv_ref[k:k + 1] * _rows_at(s1, sh, off + k, ts)
        xhat, _ = _ln_stats(cv)
        ln = xhat * vec_ref[1:2] + vec_ref[2:3]
        y_ref[:, 2 * w:3 * w] = (ln * _sigmoid(ln)).astype(BF16)

        s1[0:hl] = prev(7)
        s1[hl:hl + ts] = col(zc, 7)
        pooled = _pool_fwd(s1, s2, s3, i * ts, ts, hl)
        y_ref[:, 3 * w:4 * w] = (_dot(pooled.astype(BF16), pbd_ref[...]) * vec_ref[3:4]).astype(BF16)

    full = lambda shape: pl.BlockSpec(shape, lambda i: (0,) * len(shape))
    return pl.pallas_call(
        body, name=name, grid=(t // ts,),
        out_shape=jax.ShapeDtypeStruct((t, 4 * w), BF16),
        in_specs=[pl.BlockSpec((ts, 8 * w), lambda i: (i, 0)),
                  pl.BlockSpec((hl, 8 * w), lambda i: (jnp.maximum(i * (ts // hl) - 1, 0), 0)),
                  full((8, w)), full((32, w)), full((8, w)), full((N_HEADS, SGU_CHUNK, SGU_CHUNK)),
                  full((SGU_CHUNK, w)), full((w, w))],
        out_specs=pl.BlockSpec((ts, 4 * w), lambda i: (i, 0)),
        scratch_shapes=[pltpu.VMEM((hl + ts, w), F32)] * 3 + [pltpu.VMEM((8, hl + ts, w), F32)],
        compiler_params=_params("parallel"),
    )(z, z, sconv, cconv, vecs, wt, bexp, pbd)


def mixer_bwd(z, dy, sconv, cconv, vecs, wt, bexp, pbd, *, name, ts=None):
    t = z.shape[0]
    ts = _row_tile(t, MIX_TILE if ts is None else ts)
    hl = HALO
    w = MIX_W
    nch = ts // SGU_CHUNK
    ni = t // ts
    ext = ts + hl

    def body(zc, zp, zn, dyc, dyn, sconv_ref, cconv_ref, vec_ref, wt_ref, bexp_ref, pbd_ref,
             dz_ref, gvec_ref, gcc_ref, gwt_ref, gb_ref, gpbd_ref, s1, s2, s3, sh1, sh3):
        i = pl.program_id(0)
        has_prev = i > 0
        has_next = i < ni - 1

        @pl.when(i == 0)
        def _():
            gvec_ref[...] = jnp.zeros_like(gvec_ref)
            gcc_ref[...] = jnp.zeros_like(gcc_ref)
            gwt_ref[...] = jnp.zeros_like(gwt_ref)
            gb_ref[...] = jnp.zeros_like(gb_ref)
            gpbd_ref[...] = jnp.zeros_like(gpbd_ref)

        def col(ref, c):
            return ref[:, c * w:(c + 1) * w]

        def prev(c):
            return jnp.where(has_prev, col(zp, c), 0.0)

        def nxt(c):
            return jnp.where(has_next, col(zn, c), 0.0)

        def dnext(c):
            return jnp.where(has_next, col(dyn, c), 0.0)

        def rowsum(v):
            return jnp.sum(v, axis=0, keepdims=True)

        s1[0:hl] = prev(1) * prev(2)
        s1[hl:hl + ts] = col(zc, 1) * col(zc, 2)
        s1[hl + ts:hl + ts + hl] = nxt(1) * nxt(2)
        cv = sconv_ref[0:1] * s1[hl - 2:hl - 2 + ts]
        for k in range(1, SCONV_K):
            cv = cv + sconv_ref[k:k + 1] * s1[hl - 2 + k:hl - 2 + k + ts]
        dya = col(dyc, 0)
        dz_ref[:, 0:w] = (dya * cv).astype(BF16)
        s2[0:ts] = dya * col(zc, 0)
        s2[ts:ext] = dnext(0) * nxt(0)
        dv = sconv_ref[0:1] * s2[2:2 + ts]
        for k in range(1, SCONV_K):
            dv = dv + sconv_ref[k:k + 1] * s2[2 - k:2 - k + ts]
        dz_ref[:, w:2 * w] = (dv * col(zc, 2)).astype(BF16)
        dz_ref[:, 2 * w:3 * w] = (dv * col(zc, 1)).astype(BF16)
        dcv = s2[0:ts]
        for k in range(SCONV_K):
            gvec_ref[k:k + 1] += rowsum(dcv * s1[hl - 2 + k:hl - 2 + k + ts])

        g_sgu = vec_ref[0:1]
        xhat, rstd = _ln_stats(col(zc, 4))
        vn = (xhat * g_sgu).astype(BF16)
        grp = _lane_group((SGU_CHUNK, w))
        lane = lax.broadcasted_iota(jnp.int32, (SGU_CHUNK, SGU_CHUNK), 1)
        tril = lax.broadcasted_iota(jnp.int32, (SGU_CHUNK, SGU_CHUNK), 0) >= lane
        for c in range(nch):
            rows = slice(c * SGU_CHUNK, (c + 1) * SGU_CHUNK)
            vnc = vn[rows]
            mixed = _sgu_mix(wt_ref, vnc) + bexp_ref[...]
            dyb = dyc[rows, w:2 * w]
            dz_ref[rows, 3 * w:4 * w] = (dyb * mixed).astype(BF16)
            dmix = dyb * zc[rows, 3 * w:4 * w]
            dmixb = dmix.astype(BF16)
            dvn = jnp.zeros((SGU_CHUNK, w), F32)
            gb = jnp.zeros((SGU_CHUNK, SGU_CHUNK), F32)
            for hd in range(N_HEADS):
                dvn = jnp.where(grp == hd, _dot_tn(wt_ref[hd], dmixb), dvn)
                dm_h = jnp.where(grp == hd, dmix, 0.0)
                gwt_ref[hd] += jnp.where(tril, _dot_nt(dm_h.astype(BF16), vnc), 0.0)
                gb = gb + jnp.where(lane == hd, jnp.sum(dm_h, axis=1, keepdims=True), 0.0)
            gb_ref[...] += gb
            s3[rows] = dvn
        dvn = s3[0:ts]
        gvec_ref[3:4] += rowsum(dvn * xhat)
        dz_ref[:, 4 * w:5 * w] = _ln_bwd(xhat, rstd, dvn * g_sgu).astype(BF16)

        sig_c = _sigmoid(col(zc, 6))
        s1[0:hl] = prev(5) * _sigmoid(prev(6))
        s1[hl:hl + ts] = col(zc, 5) * sig_c
        s1[hl + ts:hl + ts + hl] = nxt(5) * _sigmoid(nxt(6))
        off = hl - (CCONV_K - 1)
        _make_shifts(s1, sh1, ts + 2 * hl - 8)
        cv = cconv_ref[0:1] * _rows_at(s1, sh1, off, ext)
        for k in range(1, CCONV_K):
            cv = cv + cconv_ref[k:k + 1] * _rows_at(s1, sh1, off + k, ext)
        xhat, rstd = _ln_stats(cv)
        ln = xhat * vec_ref[1:2] + vec_ref[2:3]
        sg = _sigmoid(ln)
        s2[0:ts] = col(dyc, 2)
        s2[ts:ext] = dnext(2)
        dln = s2[0:ext] * (sg * (1.0 + ln * (1.0 - sg)))
        gvec_ref[4:5] += rowsum(dln[0:ts] * xhat[0:ts])
        gvec_ref[5:6] += rowsum(dln[0:ts])
        s3[0:ext] = _ln_bwd(xhat, rstd, dln * vec_ref[1:2])
        _make_shifts(s3, sh3, ext - 8)
        dyg = cconv_ref[0:1] * _rows_at(s3, sh3, CCONV_K - 1, ts)
        for k in range(1, CCONV_K):
            dyg = dyg + cconv_ref[k:k + 1] * _rows_at(s3, sh3, CCONV_K - 1 - k, ts)
        dz_ref[:, 5 * w:6 * w] = (dyg * sig_c).astype(BF16)
        dz_ref[:, 6 * w:7 * w] = (dyg * col(zc, 5) * sig_c * (1.0 - sig_c)).astype(BF16)
        dcv = s3[0:ts]
        for k in range(CCONV_K):
            gcc_ref[k:k + 1] += rowsum(dcv * _rows_at(s1, sh1, off + k, ts))

        scale = vec_ref[3:4]
        s1[0:hl] = prev(7)
        s1[hl:hl + ts] = col(zc, 7)
        pooled = _pool_fwd(s1, s2, s3, i * ts, ts, hl).astype(BF16)
        q0 = _dot(pooled, pbd_ref[...])
        dyd = col(dyc, 3)
        gvec_ref[6:7] += rowsum(dyd * q0)
        dq = (dyd * scale).astype(BF16)
        gpbd_ref[...] += _dot_tn(pooled, dq)
        s1[0:ts] = _dot_nt(dq, pbd_ref[...])
        s1[ts:ext] = _dot_nt((dnext(3) * scale).astype(BF16), pbd_ref[...])
        dpool = s1[0:ts]
        s2[0:ext] = s1[0:ext] / _pool_count(i * ts, ext)
        s3[0:ts + 24] = s2[0:ts + 24] + s2[1:ts + 25]
        f2 = s3[0:ts]
        s2[0:ts + 16] = s3[0:ts + 16] + s3[2:ts + 18]
        f4 = s2[0:ts]
        s3[0:ts + 8] = s2[0:ts + 8] + s2[4:ts + 12]
        f8 = s3[0:ts]
        f16 = f8 + s3[8:ts + 8]
        dz_ref[:, 7 * w:8 * w] = (_by_group(_lane_group((ts, w)), f2, f4, f8, f16) - dpool).astype(BF16)

    full = lambda shape: pl.BlockSpec(shape, lambda i: (0,) * len(shape))
    r = ts // hl
    prev_map = lambda i: (jnp.maximum(i * r - 1, 0), 0)
    next_map = lambda i: (jnp.minimum((i + 1) * r, t // hl - 1), 0)
    return pl.pallas_call(
        body, name=name, grid=(ni,),
        out_shape=[jax.ShapeDtypeStruct((t, 8 * w), BF16), jax.ShapeDtypeStruct((8, w), F32),
                   jax.ShapeDtypeStruct((32, w), F32),
                   jax.ShapeDtypeStruct((N_HEADS, SGU_CHUNK, SGU_CHUNK), F32),
                   jax.ShapeDtypeStruct((SGU_CHUNK, SGU_CHUNK), F32), jax.ShapeDtypeStruct((w, w), F32)],
        in_specs=[pl.BlockSpec((ts, 8 * w), lambda i: (i, 0)),
                  pl.BlockSpec((hl, 8 * w), prev_map), pl.BlockSpec((hl, 8 * w), next_map),
                  pl.BlockSpec((ts, 4 * w), lambda i: (i, 0)), pl.BlockSpec((hl, 4 * w), next_map),
                  full((8, w)), full((32, w)), full((8, w)), full((N_HEADS, SGU_CHUNK, SGU_CHUNK)),
                  full((SGU_CHUNK, w)), full((w, w))],
        out_specs=[pl.BlockSpec((ts, 8 * w), lambda i: (i, 0)), full((8, w)), full((32, w)),
                   full((N_HEADS, SGU_CHUNK, SGU_CHUNK)), full((SGU_CHUNK, SGU_CHUNK)), full((w, w))],
        scratch_shapes=[pltpu.VMEM((ts + 2 * hl, w), F32)] * 3 + [pltpu.VMEM((8, ts + 2 * hl, w), F32)] * 2,
        compiler_params=_params("arbitrary"),
    )(z, z, z, dy, dy, sconv, cconv, vecs, wt, bexp, pbd)


def _attn_head(q, kv_ref, hd, d):
    hw = d // N_HEADS
    qh = q[:, hd * hw:(hd + 1) * hw]
    kh = kv_ref[:, hd * hw:(hd + 1) * hw].astype(BF16)
    vh = kv_ref[:, d + hd * hw:d + (hd + 1) * hw].astype(BF16)
    s = _dot_nt(qh, kh) * (1.0 / (hw ** 0.5))
    e = jnp.exp(s - jnp.max(s, axis=-1, keepdims=True))
    p = e / jnp.sum(e, axis=-1, keepdims=True)
    return qh, kh, vh, p


def xattn_fwd(x, gain, kv, wq_g, wo_g, *, name, tm=None):
    t, d = x.shape
    nm = kv.shape[0]
    tm = _row_tile(t, tm)
    hw = d // N_HEADS

    def body(x_ref, g_ref, kv_ref, wq_ref, wo_ref, o_ref):
        xv = x_ref[...]
        xhat, _ = _rms_fwd(xv, None)
        h = (xhat * g_ref[...]).astype(BF16)
        q = _dot(h, _full_weight(wq_ref, "row")).astype(BF16)
        wo = _full_weight(wo_ref, "row")
        out = xv
        for hd in range(N_HEADS):
            _, _, vh, p = _attn_head(q, kv_ref, hd, d)
            oh = _dot(p.astype(BF16), vh).astype(BF16)
            out = out + _dot(oh, wo[hd * hw:(hd + 1) * hw])
        o_ref[...] = out

    row = lambda i: (i, 0)
    return pl.pallas_call(
        body, name=name, grid=(t // tm,),
        out_shape=jax.ShapeDtypeStruct((t, d), F32),
        in_specs=[pl.BlockSpec((tm, d), row), pl.BlockSpec((1, d), lambda i: (0, 0)),
                  pl.BlockSpec((nm, 2 * d), lambda i: (0, 0)), _wspec(wq_g), _wspec(wo_g)],
        out_specs=pl.BlockSpec((tm, d), row),
        compiler_params=_params("parallel"),
    )(x, gain.reshape(1, d), kv, wq_g, wo_g)


def xattn_bwd_rows(x, dxn, gain, kv, wq_g, wo_g, *, name, tm=None):
    t, d = x.shape
    nm = kv.shape[0]
    tm = _row_tile(t, tm)
    hw = d // N_HEADS

    def body(x_ref, dxn_ref, g_ref, kv_ref, wq_ref, wo_ref,
             dx_ref, h_ref, dq_ref, o_ref, dkv_ref, dg_ref):
        i = pl.program_id(0)

        @pl.when(i == 0)
        def _():
            dkv_ref[...] = jnp.zeros_like(dkv_ref)
            dg_ref[...] = jnp.zeros_like(dg_ref)
        g = g_ref[...]
        xhat, r = _rms_fwd(x_ref[...], None)
        h = (xhat * g).astype(BF16)
        h_ref[...] = h
        wq = _full_weight(wq_ref, "row")
        q = _dot(h, wq).astype(BF16)
        dxn = dxn_ref[...]
        do = _dot_nt(dxn.astype(BF16), _full_weight(wo_ref, "row")).astype(BF16)
        for hd in range(N_HEADS):
            cols = slice(hd * hw, (hd + 1) * hw)
            qh, kh, vh, p = _attn_head(q, kv_ref, hd, d)
            pb = p.astype(BF16)
            o_ref[:, cols] = _dot(pb, vh).astype(BF16)
            doh = do[:, cols]
            dkv_ref[:, d + hd * hw:d + (hd + 1) * hw] += _dot_tn(pb, doh)
            dp = _dot_nt(doh, vh)
            ds = (p * (dp - jnp.sum(dp * p, axis=-1, keepdims=True)) * (1.0 / (hw ** 0.5))).astype(BF16)
            dq_ref[:, cols] = _dot(ds, kh).astype(BF16)
            dkv_ref[:, cols] += _dot_tn(ds, qh)
        dh = _dot_nt(dq_ref[...], wq)
        dx, dg = _rms_bwd(xhat, r, g, dh)
        dx_ref[...] = dxn + dx
        dg_ref[...] += dg

    row = lambda i: (i, 0)
    fix = lambda i: (0, 0)
    return pl.pallas_call(
        body, name=name, grid=(t // tm,),
        out_shape=[jax.ShapeDtypeStruct((t, d), F32), jax.ShapeDtypeStruct((t, d), BF16),
                   jax.ShapeDtypeStruct((t, d), BF16), jax.ShapeDtypeStruct((t, d), BF16),
                   jax.ShapeDtypeStruct((nm, 2 * d), F32), jax.ShapeDtypeStruct((1, d), F32)],
        in_specs=[pl.BlockSpec((tm, d), row), pl.BlockSpec((tm, d), row), pl.BlockSpec((1, d), fix),
                  pl.BlockSpec((nm, 2 * d), fix), _wspec(wq_g), _wspec(wo_g)],
        out_specs=[pl.BlockSpec((tm, d), row)] * 4 + [pl.BlockSpec((nm, 2 * d), fix),
                                                      pl.BlockSpec((1, d), fix)],
        compiler_params=_params("arbitrary"),
    )(x, dxn, gain.reshape(1, d), kv, wq_g, wo_g)


def loss_head(x, target, gain, *, name, tm=None):
    t, d = x.shape
    tm = _row_tile(t, tm)

    def body(x_ref, t_ref, g_ref, dx_ref, dg_ref, loss_ref):
        @pl.when(pl.program_id(0) == 0)
        def _():
            dg_ref[...] = jnp.zeros_like(dg_ref)
            loss_ref[...] = jnp.zeros_like(loss_ref)
        g = g_ref[...]
        xhat, r = _rms_fwd(x_ref[...], None)
        err = xhat * g - t_ref[...]
        loss_ref[...] += 0.5 * jnp.sum(jnp.sum(err * err, axis=-1, keepdims=True) / d,
                                       axis=0, keepdims=True)
        dx, dg = _rms_bwd(xhat, r, g, err / d)
        dx_ref[...] = dx
        dg_ref[...] += dg

    row = lambda i: (i, 0)
    fix = lambda i: (0, 0)
    return pl.pallas_call(
        body, name=name, grid=(t // tm,),
        out_shape=[jax.ShapeDtypeStruct((t, d), F32), jax.ShapeDtypeStruct((1, d), F32),
                   jax.ShapeDtypeStruct((1, 1), F32)],
        in_specs=[pl.BlockSpec((tm, d), row), pl.BlockSpec((tm, d), row), pl.BlockSpec((1, d), fix)],
        out_specs=[pl.BlockSpec((tm, d), row), pl.BlockSpec((1, d), fix), pl.BlockSpec((1, 1), fix)],
        compiler_params=_params("arbitrary"),
    )(x, target, gain.reshape(1, d))


def _adamw_math(w, g, m, v):
    m = ADAM_B1 * m + (1.0 - ADAM_B1) * g
    v = ADAM_B2 * v + (1.0 - ADAM_B2) * (g * g)
    m_hat = m / (1.0 - ADAM_B1 ** ADAM_STEP)
    v_hat = v / (1.0 - ADAM_B2 ** ADAM_STEP)
    delta = -ADAM_LR * (m_hat / (jnp.sqrt(v_hat) + ADAM_EPS) + ADAM_WD * w)
    return delta, m, v


def adamw_sharded(own, lands, w, m, v, me_arr, *, name):
    nl, r, c = w.shape
    assert nl == len(own) == len(lands) == 2
    tr = next(cand for cand in (256, 176, 128, r) if r % cand == 0)
    nr = r // tr

    def body(me_ref, o0, o1, l0, l1, w_ref, m_ref, v_ref, g_out, d_out, m_out, v_out):
        def total(o_ref, l_ref):
            acc = o_ref[...].astype(F32)
            for p in range(N_DEV - 1):
                acc = acc + l_ref[p].astype(F32)
            return acc
        g = jnp.where(pl.program_id(0) == 0, total(o0, l0), total(o1, l1))
        delta, mn, vn = _adamw_math(w_ref[...], g, m_ref[...], v_ref[...])
        g_out[...] = g
        d_out[...] = delta
        m_out[...] = mn
        v_out[...] = vn

    row0 = lambda l, i: jnp.where(l == 0, i, nr - 1)
    row1 = lambda l, i: jnp.where(l == 1, i, 0)
    blk = pl.BlockSpec((None, tr, c), lambda l, i, me: (l, i, 0))
    grid_spec = pltpu.PrefetchScalarGridSpec(
        num_scalar_prefetch=1, grid=(nl, nr),
        in_specs=[pl.BlockSpec((None, tr, c), lambda l, i, me: (me[0], row0(l, i), 0)),
                  pl.BlockSpec((None, tr, c), lambda l, i, me: (me[0], row1(l, i), 0)),
                  pl.BlockSpec((N_DEV - 1, tr, c), lambda l, i, me: (0, row0(l, i), 0)),
                  pl.BlockSpec((N_DEV - 1, tr, c), lambda l, i, me: (0, row1(l, i), 0)),
                  blk, blk, blk],
        out_specs=[blk] * 4)
    return pl.pallas_call(
        body, name=name, grid_spec=grid_spec,
        out_shape=[jax.ShapeDtypeStruct((nl, r, c), F32)] * 4,
        compiler_params=_params("arbitrary", "arbitrary"),
    )(me_arr, own[0], own[1], lands[0], lands[1], w, m, v)


def adamw_flat(g, w, m, v, *, name):
    def body(g_ref, w_ref, m_ref, v_ref, d_out, m_out, v_out):
        delta, mn, vn = _adamw_math(w_ref[...], g_ref[...], m_ref[...], v_ref[...])
        d_out[...] = delta
        m_out[...] = mn
        v_out[...] = vn

    return pl.pallas_call(
        body, name=name, out_shape=[jax.ShapeDtypeStruct(w.shape, F32)] * 3,
        in_specs=[VMEM_SPEC] * 4, out_specs=[VMEM_SPEC] * 3,
        compiler_params=pltpu.CompilerParams(vmem_limit_bytes=VMEM_LIMIT),
    )(g, w, m, v)


def cast_into_slot(a, layer, me_arr, *, name, dtype=None):
    dtype = BF16 if dtype is None else dtype
    _, r, c = a.shape
    tr = next(cand for cand in (256, 176, 128, r) if r % cand == 0)

    def body(me_ref, a_ref, o_ref):
        o_ref[...] = a_ref[...].astype(dtype)

    grid_spec = pltpu.PrefetchScalarGridSpec(
        num_scalar_prefetch=1, grid=(r // tr,),
        in_specs=[pl.BlockSpec((None, tr, c), lambda i, me: (layer, i, 0))],
        out_specs=pl.BlockSpec((None, tr, c), lambda i, me: (me[0], i, 0)))
    return pl.pallas_call(
        body, name=name, grid_spec=grid_spec,
        out_shape=jax.ShapeDtypeStruct((N_DEV, r, c), dtype),
        compiler_params=_params("parallel"),
    )(me_arr, a)


def _pack(arrs, rows):
    flat = jnp.concatenate([a.reshape(-1).astype(F32) for a in arrs])
    pad = rows * 128 - flat.shape[0]
    assert pad >= 0
    if pad:
        flat = jnp.concatenate([flat, jnp.zeros((pad,), F32)])
    return flat.reshape(rows, 128)


def _unpack(packed, shapes):
    flat = packed.reshape(-1)
    out, pos = [], 0
    for s in shapes:
        n = 1
        for dim in s:
            n *= dim
        out.append(flat[pos:pos + n].reshape(s))
        pos += n
    return out


def _rows_for(shapes):
    n = 0
    for s in shapes:
        k = 1
        for dim in s:
            k *= dim
        n += k
    return -(-n // 1024) * 8


GATHER_GROUPS = (("ffn1", ("ffn1_w_in", "ffn1_w_out")),
                 ("mid", ("mix_w_in", "mix_w_out", "xattn_wkv", "xattn_wq", "xattn_wo")),
                 ("ffn2", ("ffn2_w_in", "ffn2_w_out")))
SMALL_REPL = ["norm_ffn1", "norm_mix", "sgu_norm_g", "sgu_w", "sgu_b", "cconv_ln_g", "cconv_ln_b",
              "pool_w", "pool_scale", "norm_xattn", "norm_mem", "norm_ffn2", "norm_final"]
SMALL_SHARD = ["sconv_w", "cconv_w"]
TRANSPOSED = ("ffn1_w_in", "ffn2_w_in")
WEIGHTS = ["norm_ffn1", "ffn1_w_in", "ffn1_w_out", "norm_mix", "mix_w_in", "sconv_w", "sgu_norm_g",
           "sgu_w", "sgu_b", "cconv_w", "cconv_ln_g", "cconv_ln_b", "pool_w", "pool_scale", "mix_w_out",
           "norm_xattn", "norm_mem", "xattn_wq", "xattn_wkv", "xattn_wo", "norm_ffn2", "ffn2_w_in",
           "ffn2_w_out", "norm_final"]


def kernel(x, mem, norm_ffn1, ffn1_w_in, ffn1_w_out, norm_mix, mix_w_in, sconv_w, sgu_norm_g, sgu_w, sgu_b, cconv_w, cconv_ln_g, cconv_ln_b, pool_w, pool_scale, mix_w_out, norm_xattn, norm_mem, xattn_wq, xattn_wkv, xattn_wo, norm_ffn2, ffn2_w_in, ffn2_w_out, norm_final, loss_target, m_norm_ffn1, m_ffn1_w_in, m_ffn1_w_out, m_norm_mix, m_mix_w_in, m_sconv_w, m_sgu_norm_g, m_sgu_w, m_sgu_b, m_cconv_w, m_cconv_ln_g, m_cconv_ln_b, m_pool_w, m_pool_scale, m_mix_w_out, m_norm_xattn, m_norm_mem, m_xattn_wq, m_xattn_wkv, m_xattn_wo, m_norm_ffn2, m_ffn2_w_in, m_ffn2_w_out, m_norm_final, v_norm_ffn1, v_ffn1_w_in, v_ffn1_w_out, v_norm_mix, v_mix_w_in, v_sconv_w, v_sgu_norm_g, v_sgu_w, v_sgu_b, v_cconv_w, v_cconv_ln_g, v_cconv_ln_b, v_pool_w, v_pool_scale, v_mix_w_out, v_norm_xattn, v_norm_mem, v_xattn_wq, v_xattn_wkv, v_xattn_wo, v_norm_ffn2, v_ffn2_w_in, v_ffn2_w_out, v_norm_final):
    args = dict(locals())
    wts = {n: args[n] for n in WEIGHTS}
    mom = {n: args["m_" + n] for n in WEIGHTS}
    var = {n: args["v_" + n] for n in WEIGHTS}
    for n in TRANSPOSED:
        wts[n], mom[n], var[n] = (jnp.swapaxes(a, 1, 2) for a in (wts[n], mom[n], var[n]))
    x0 = x[0]
    mem0 = mem[0]
    target = loss_target[0]
    t, d = x0.shape
    nl = norm_ffn1.shape[0]
    w = MIX_W
    me = _my_index()

    me_arr = jnp.reshape(me, (1,)).astype(jnp.int32)

    small_g = all_gather([sconv_w, cconv_w], name="gather_conv_taps")
    sconv_full = jnp.transpose(small_g[0], (1, 2, 0, 3)).reshape(nl, SCONV_K, w)
    cconv_full = jnp.transpose(small_g[1], (1, 2, 0, 3)).reshape(nl, CCONV_K, w)
    pending = {}
    token = small_g[1]
    for l in range(nl):
        masks = GATHER_MASKS if l == 0 else ALL_MASKS
        for gname, members in GATHER_GROUPS:
            gs = [cast_into_slot(wts[n], l, me_arr, name=f"cast_{n}{l}") for n in members]
            send, recv, gs, token = gather_start(gs, token, name=f"gather_start_{gname}{l}", masks=masks)
            pending[gname, l] = (members, gs, send, recv, masks)
    wg = [dict() for _ in range(nl)]

    def arrive(gname, l, after):
        members, gs, send, recv, masks = pending.pop((gname, l))
        gs = gather_wait(gs, send, recv, after, name=f"gather_wait_{gname}{l}", masks=masks)
        if masks is GATHER_MASKS:
            gs = sibling_forward(gs, name=f"gather_forward_{gname}{l}")
        wg[l].update(zip(members, gs))
    sconv_pad = jnp.pad(sconv_full, ((0, 0), (0, 8 - SCONV_K), (0, 0)))
    cconv_pad = jnp.pad(cconv_full, ((0, 0), (0, 32 - CCONV_K), (0, 0)))
    zeros_w = jnp.zeros((nl, w), F32)
    vecs = jnp.stack([sgu_norm_g, cconv_ln_g, cconv_ln_b, pool_scale] + [zeros_w] * 4, axis=1)
    wt = jnp.tril(sgu_w).astype(BF16)
    bexp = jnp.repeat(jnp.swapaxes(sgu_b, 1, 2), w // N_HEADS, axis=2)
    eye = jnp.eye(4, dtype=F32)
    pbd = jnp.einsum("lgcd,gh->lgchd", pool_w, eye).reshape(nl, w, w).astype(BF16)

    def mixer_args(l):
        return sconv_pad[l], cconv_pad[l], vecs[l], wt[l], bexp[l], pbd[l]

    saved = []
    xc = x0
    after = token
    for l in range(nl):
        s = {"x_ffn1": xc}
        arrive("ffn1", l, after)
        xc, s["gu_ffn1"] = ffn_fwd(xc, norm_ffn1[l], wg[l]["ffn1_w_in"], wg[l]["ffn1_w_out"],
                                   name=f"ffn1_fwd{l}", tm=FFN_FWD_TILE)
        s["x_mix"] = xc
        arrive("mid", l, xc)
        z = mm_rows(xc, wg[l]["mix_w_in"], "col", gain=norm_mix[l], name=f"mix_in{l}")
        y = mixer_fwd(z, *mixer_args(l), name=f"mixer_fwd{l}")
        s["z"], s["y"] = z, y
        xc = mm_rows(y, wg[l]["mix_w_out"], "row", residual=xc, name=f"mix_out{l}")
        s["x_att"] = xc
        kv = mm_rows(mem0, wg[l]["xattn_wkv"], "col", gain=norm_mem[l], name=f"kv{l}")
        s["kv"] = kv
        xc = xattn_fwd(xc, norm_xattn[l], kv, wg[l]["xattn_wq"], wg[l]["xattn_wo"], name=f"xattn_fwd{l}")
        s["x_ffn2"] = xc
        arrive("ffn2", l, xc)
        xc, s["gu_ffn2"] = ffn_fwd(xc, norm_ffn2[l], wg[l]["ffn2_w_in"], wg[l]["ffn2_w_out"],
                                   name=f"ffn2_fwd{l}", tm=FFN_FWD_TILE)
        after = xc
        saved.append(s)

    dx, g_norm_final, loss_local = loss_head(xc, target, norm_final, name="loss_head")
    loss = lax.psum(loss_local[0, 0], ("x", "y", "c"))

    tm = _row_tile(t, TN_TILE)
    small ={n: [None] * nl for n in SMALL_REPL + SMALL_SHARD if n != "norm_final"}
    scattered = {}
    tie = [token]

    def send_grads(gname, l, grads):
        members = list(grads)
        send, recv, gs, lands, tie[0] = scatter_start(
            [grads[n] for n in members], tie[0], name=f"scatter_start_{gname}{l}")
        scattered[gname, l] = (members, gs, lands, send, recv)

    def tied(v):
        return v + tie[0][0, 0]

    names = SMALL_REPL + SMALL_SHARD
    small_pending = []

    def start_small():
        small_full = {n: jnp.stack(v) for n, v in small.items()}
        small_full["norm_final"] = g_norm_final[0]
        shapes = [small_full[n].shape for n in names]
        packed = _pack([small_full[n] for n in names], _rows_for(shapes))
        slot = cast_into_slot(packed[None], 0, me_arr, name="small_into_slot", dtype=F32)
        send, recv, gs, tie[0] = gather_start([slot], tie[0], name="small_gather_start", masks=ALL_MASKS)
        small_pending.append((gs, send, recv, shapes))

    def ffn_backward(which, l, x_in, dy, gu, gain):
        w_in, w_out = wg[l][which + "_w_in"], wg[l][which + "_w_out"]
        dx_, h_, act, dgu, dgn = ffn_bwd_rows(x_in, dy, gu, tied(gain), w_in, w_out,
                                              name=f"{which}_bwd{l}_rows")
        small["norm_" + which][l] = dgn[0]
        if which == "ffn1" and l == 0:
            start_small()
        send_grads(which + "_in", l,
                   {which + "_w_in": ffn_grad_w_in(h_, dgu, tie[0], name=f"{which}_bwd{l}_dwin")})
        send_grads(which + "_out", l,
                   {which + "_w_out": ffn_grad_w_out(act, dy, tie[0], name=f"{which}_bwd{l}_dwout")})
        return dx_

    for l in reversed(range(nl)):
        s = saved[l]
        wl = wg[l]
        dx = ffn_backward("ffn2", l, s["x_ffn2"], dx, s["gu_ffn2"], norm_ffn2[l])

        bg = {}
        dxn = dx
        dx, h, dq, o, dkv, dgn = xattn_bwd_rows(
            s["x_att"], dxn, tied(norm_xattn[l]), s["kv"], wl["xattn_wq"], wl["xattn_wo"],
            name=f"xattn_bwd{l}")
        small["norm_xattn"][l] = dgn[0]
        row_spec = pl.BlockSpec((tm, d), lambda s_, i: (i, 0))
        bg["xattn_wq"] = mm_tn(h, dq, nb=1, ka=d, nbk=d, tm=tm, m=t, a_spec=row_spec, b_spec=row_spec,
                               name=f"dwq{l}").reshape(N_DEV, d // N_DEV, d)
        bg["xattn_wo"] = mm_tn(o, dxn, nb=1, ka=d, nbk=d, tm=tm, m=t, a_spec=row_spec, b_spec=row_spec,
                               name=f"dwo{l}").reshape(N_DEV, d // N_DEV, d)
        _, mhat, dgn = mm_nt(dkv, wl["xattn_wkv"], "col", x=mem0, gain=norm_mem[l], name=f"dmem{l}")
        small["norm_mem"][l] = dgn[0]
        nm = mem0.shape[0]
        bg["xattn_wkv"] = mm_tn(mhat, dkv, nb=N_DEV, ka=d, nbk=2 * d // N_DEV, tm=nm, m=nm,
                                a_spec=pl.BlockSpec((nm, d), lambda s_, i: (0, 0)),
                                b_spec=pl.BlockSpec((nm, 2 * d // N_DEV), lambda s_, i: (0, s_)),
                                name=f"dwkv{l}")
        send_grads("xattn", l, bg)

        bg = {}
        dxn = dx
        bg["mix_w_out"] = mm_tn(s["y"], dxn, nb=1, ka=d, nbk=d, tm=tm, m=t, a_spec=row_spec,
                                b_spec=row_spec, name=f"dwmo{l}").reshape(N_DEV, d // N_DEV, d)
        dy = mm_nt(dxn, wl["mix_w_out"], "row", name=f"dy_mix{l}")
        dz, gvec, gcc, gwt, gb, gpbd = mixer_bwd(s["z"], dy, *mixer_args(l), name=f"mixer_bwd{l}")
        small["sconv_w"][l] = gvec[0:SCONV_K]
        small["sgu_norm_g"][l] = gvec[3]
        small["cconv_ln_g"][l] = gvec[4]
        small["cconv_ln_b"][l] = gvec[5]
        small["pool_scale"][l] = gvec[6]
        small["cconv_w"][l] = gcc[0:CCONV_K]
        small["sgu_w"][l] = gwt
        small["sgu_b"][l] = jnp.transpose(gb[:, 0:N_HEADS])
        gw = w // 4
        small["pool_w"][l] = jnp.stack([gpbd[g * gw:(g + 1) * gw, g * gw:(g + 1) * gw] for g in range(4)])
        dx, h, dgn = mm_nt(dz, wl["mix_w_in"], "col", x=s["x_mix"], gain=tied(norm_mix[l]), dx_in=dxn,
                           name=f"dh_mix{l}")
        small["norm_mix"][l] = dgn[0]
        th = _row_tile(t, TN_TILE // 2)
        bg["mix_w_in"] = mm_tn(h, dz, nb=1, ka=d, nbk=N_DEV * w, tm=th, m=t, col_slots=N_DEV,
                               a_spec=pl.BlockSpec((th, d), lambda s_, i: (i, 0)),
                               b_spec=pl.BlockSpec((th, N_DEV * w), lambda s_, i: (i, 0)), name=f"dwmi{l}")
        send_grads("mix", l, bg)

        dx = ffn_backward("ffn1", l, s["x_ffn1"], dx, s["gu_ffn1"], norm_ffn1[l])

    out = {}

    def finish(gname, after):
        own, land = {}, {}
        for l in reversed(range(nl)):
            members, gs, lands, send, recv = scattered.pop((gname, l))
            gs, lands = scatter_wait(gs, lands, send, recv, after, name=f"scatter_wait_{gname}{l}")
            for n, g_, l_ in zip(members, gs, lands):
                own.setdefault(n, {})[l] = g_
                land.setdefault(n, {})[l] = l_
        for n in own:
            out[n] = adamw_sharded([own[n][l] for l in range(nl)], [land[n][l] for l in range(nl)],
                                   wts[n], mom[n], var[n], me_arr, name="adamw_" + n)
            after = out[n][1]
        return after

    after = tie[0]
    for gname in ("ffn2_in", "ffn2_out", "xattn", "mix"):
        after = finish(gname, after)
    (gs, send, recv, shapes), = small_pending
    gs = gather_wait(gs, send, recv, after, name="small_gather_wait", masks=ALL_MASKS)
    summed = sum_slots(gs[0], name="small_sum")
    gsm = dict(zip(names, _unpack(summed, shapes)))
    after = finish("ffn1_in", summed)
    finish("ffn1_out", after)
    repl_shapes = [wts[n].shape for n in SMALL_REPL]
    rows_r = _rows_for(repl_shapes)
    dl, mn, vn = adamw_flat(_pack([gsm[n] for n in SMALL_REPL], rows_r),
                            _pack([wts[n] for n in SMALL_REPL], rows_r),
                            _pack([mom[n] for n in SMALL_REPL], rows_r),
                            _pack([var[n] for n in SMALL_REPL], rows_r), name="adamw_small")
    for n, a, b, c in zip(SMALL_REPL, _unpack(dl, repl_shapes), _unpack(mn, repl_shapes),
                          _unpack(vn, repl_shapes)):
        out[n] = (gsm[n], a, b, c)
    cs = w // N_DEV
    gsh = {n: lax.dynamic_slice_in_dim(gsm[n], me * cs, cs, axis=2) for n in SMALL_SHARD}
    sh_shapes = [wts[n].shape for n in SMALL_SHARD]
    rows_s = _rows_for(sh_shapes)
    dl, mn, vn = adamw_flat(_pack([gsh[n] for n in SMALL_SHARD], rows_s),
                            _pack([wts[n] for n in SMALL_SHARD], rows_s),
                            _pack([mom[n] for n in SMALL_SHARD], rows_s),
                            _pack([var[n] for n in SMALL_SHARD], rows_s), name="adamw_small_sharded")
    for n, a, b, c in zip(SMALL_SHARD, _unpack(dl, sh_shapes), _unpack(mn, sh_shapes),
                          _unpack(vn, sh_shapes)):
        out[n] = (gsh[n], a, b, c)
    for n in TRANSPOSED:
        out[n] = tuple(jnp.swapaxes(a, 1, 2) for a in out[n])

    grad_x = dx.reshape(1, t, d)
    return (loss, grad_x, *[out[n][0] for n in WEIGHTS], *[out[n][1] for n in WEIGHTS],
            *[out[n][2] for n in WEIGHTS], *[out[n][3] for n in WEIGHTS])
```

```python
import functools

import jax
import jax.numpy as jnp
from jax import lax
from jax.experimental import pallas as pl
from jax.experimental.pallas import tpu as pltpu

F32 = jnp.float32
BF16 = jnp.bfloat16
MESH = pl.DeviceIdType.MESH
N_DEV = 8
EPS = 1e-6
HALO = 32
SGU_CHUNK = 128
CCONV_K = 31
SCONV_K = 3
MIX_W = 256
N_HEADS = 4
VMEM_LIMIT = 56 * 1024 * 1024
ROW_TILE = 512
TN_TILE = 2048
FFN_FWD_TILE = 1024
FFN_BWD_SPLIT = 2
FFN_FWD_SPLIT = 2
MIX_TILE = 512

ADAM_LR = 0.001
ADAM_B1 = 0.9
ADAM_B2 = 0.999
ADAM_EPS = 1e-08
ADAM_WD = 0.01
ADAM_STEP = 10

HBM_SPEC = pl.BlockSpec(memory_space=pltpu.HBM)
VMEM_SPEC = pl.BlockSpec(memory_space=pltpu.VMEM)


def _params(*sem):
    return pltpu.CompilerParams(dimension_semantics=tuple(sem), vmem_limit_bytes=VMEM_LIMIT)


def _row_tile(m, pref=None):
    t = min(m, ROW_TILE if pref is None else pref)
    assert m % t == 0, (m, t)
    return t


def _my_index():
    return lax.axis_index("x") * 4 + lax.axis_index("y") * 2 + lax.axis_index("c")


def _peer(mask):
    x, y, c = lax.axis_index("x"), lax.axis_index("y"), lax.axis_index("c")
    px = 1 - x if mask & 4 else x
    py = 1 - y if mask & 2 else y
    pc = 1 - c if mask & 1 else c
    return (px, py, pc), px * 4 + py * 2 + pc


def all_gather(arrs, name):
    n = len(arrs)

    def body(*refs):
        ins, outs = refs[:n], refs[n:2 * n]
        send_sems, recv_sems, loc_sems = refs[2 * n:]
        me = _my_index()
        local = []
        for i in range(n):
            cp = pltpu.make_async_copy(ins[i], outs[i].at[me], loc_sems.at[i])
            cp.start()
            local.append(cp)
        sends = []
        for i in range(n):
            for m in range(1, N_DEV):
                peer, _ = _peer(m)
                cp = pltpu.make_async_remote_copy(
                    src_ref=ins[i], dst_ref=outs[i].at[me],
                    send_sem=send_sems.at[i, m - 1], recv_sem=recv_sems.at[i, m - 1],
                    device_id=peer, device_id_type=MESH)
                cp.start()
                sends.append(cp)
        for i in range(n):
            for m in range(1, N_DEV):
                peer, pidx = _peer(m)
                pltpu.make_async_remote_copy(
                    src_ref=ins[i], dst_ref=outs[i].at[pidx],
                    send_sem=send_sems.at[i, m - 1], recv_sem=recv_sems.at[i, m - 1],
                    device_id=peer, device_id_type=MESH).wait_recv()
        for cp in sends:
            cp.wait_send()
        for cp in local:
            cp.wait()

    return pl.pallas_call(
        body, name=name,
        out_shape=[jax.ShapeDtypeStruct((N_DEV,) + a.shape, a.dtype) for a in arrs],
        in_specs=[HBM_SPEC] * n, out_specs=[HBM_SPEC] * n,
        scratch_shapes=[pltpu.SemaphoreType.DMA((n, N_DEV - 1)),
                        pltpu.SemaphoreType.DMA((n, N_DEV - 1)),
                        pltpu.SemaphoreType.DMA((n,))],
    )(*arrs)


SEM_SPEC = pl.BlockSpec(memory_space=pltpu.SEMAPHORE)
ANY_SPEC = pl.BlockSpec(memory_space=pl.ANY)
SIDE_EFFECT = pltpu.SideEffectType.DATAFLOW_SIDE_EFFECTING


def _hbm(a):
    return pltpu.with_memory_space_constraint(a, pltpu.HBM)


def _sem_pairs(n):
    return (pltpu.SemaphoreType.DMA((n * (N_DEV - 1),)), pltpu.SemaphoreType.DMA((n * (N_DEV - 1),)))


def _sem(i, m):
    return i * (N_DEV - 1) + m - 1


def _gather_copy(g_ref, i, m, send_sems, recv_sems, origin):
    peer, _ = _peer(m)
    return pltpu.make_async_remote_copy(
        src_ref=g_ref.at[origin], dst_ref=g_ref.at[origin],
        send_sem=send_sems.at[_sem(i, m)], recv_sem=recv_sems.at[_sem(i, m)],
        device_id=peer, device_id_type=MESH)


GATHER_MASKS = (1, 2, 4, 6)
FORWARD_MASKS = (2, 4, 6)


ALL_MASKS = tuple(range(1, N_DEV))


def gather_start(gs, after, name, masks=GATHER_MASKS):
    n = len(gs)

    def body(*refs):
        g_in = refs[:n]
        send_sems, recv_sems = refs[n + 1], refs[n + 2]
        token = refs[-1]
        me = _my_index()
        for i in range(n):
            for m in masks:
                _gather_copy(g_in[i], i, m, send_sems, recv_sems, me).start()
        token[...] = jnp.zeros_like(token)

    outs = pl.pallas_call(
        body, name=name,
        out_shape=(*_sem_pairs(n), *[pltpu.HBM(g.shape, g.dtype) for g in gs],
                   jax.ShapeDtypeStruct((8, 128), F32)),
        in_specs=[HBM_SPEC] * n + [ANY_SPEC],
        out_specs=(SEM_SPEC, SEM_SPEC, *[HBM_SPEC] * n, VMEM_SPEC),
        input_output_aliases={i: 2 + i for i in range(n)},
        compiler_params=pltpu.CompilerParams(has_side_effects=SIDE_EFFECT),
    )(*[_hbm(g) for g in gs], after)
    return outs[0], outs[1], list(outs[2:2 + n]), outs[-1]


def gather_start_groups(groups, after, name, masks=GATHER_MASKS):
    sizes = [len(g) for g in groups]
    flat = [a for g in groups for a in g]
    n, ng = len(flat), len(groups)

    def body(*refs):
        g_in = refs[:n]
        sems = refs[n + 1:n + 1 + 2 * ng]
        token = refs[-1]
        me = _my_index()
        pos = 0
        for k, size in enumerate(sizes):
            for i in range(size):
                for m in masks:
                    _gather_copy(g_in[pos + i], i, m, sems[2 * k], sems[2 * k + 1], me).start()
            pos += size
        token[...] = jnp.zeros_like(token)

    outs = pl.pallas_call(
        body, name=name,
        out_shape=(*[s for size in sizes for s in _sem_pairs(size)],
                   *[pltpu.HBM(g.shape, g.dtype) for g in flat], jax.ShapeDtypeStruct((8, 128), F32)),
        in_specs=[HBM_SPEC] * n + [ANY_SPEC],
        out_specs=(*[SEM_SPEC] * (2 * ng), *[HBM_SPEC] * n, VMEM_SPEC),
        input_output_aliases={i: 2 * ng + i for i in range(n)},
        compiler_params=pltpu.CompilerParams(has_side_effects=SIDE_EFFECT),
    )(*[_hbm(g) for g in flat], after)
    result, pos = [], 2 * ng
    for k, size in enumerate(sizes):
        result.append((outs[2 * k], outs[2 * k + 1], list(outs[pos:pos + size])))
        pos += size
    return result, outs[-1]


def gather_wait(gs, send_sems, recv_sems, after, name, masks=GATHER_MASKS):
    n = len(gs)

    def body(*refs):
        g_in = refs[:n]
        send, recv = refs[n], refs[n + 1]
        me = _my_index()
        for i in range(n):
            for m in masks:
                _, pidx = _peer(m)
                _gather_copy(g_in[i], i, m, send, recv, me).wait_send()
                _gather_copy(g_in[i], i, m, send, recv, pidx).wait_recv()

    outs = pl.pallas_call(
        body, name=name,
        out_shape=[pltpu.HBM(g.shape, g.dtype) for g in gs],
        in_specs=[HBM_SPEC] * n + [SEM_SPEC, SEM_SPEC, ANY_SPEC],
        out_specs=[HBM_SPEC] * n,
        input_output_aliases={i: i for i in range(n)},
        compiler_params=pltpu.CompilerParams(has_side_effects=SIDE_EFFECT),
    )(*gs, send_sems, recv_sems, after)
    return list(outs)


def sibling_forward(gs, name):
    n = len(gs)
    nf = len(FORWARD_MASKS)

    def body(*refs):
        g_in = refs[:n]
        send_sems, recv_sems = refs[2 * n:]
        x, y, c = lax.axis_index("x"), lax.axis_index("y"), lax.axis_index("c")
        sibling = (x, y, 1 - c)

        def copy(i, k, origin):
            return pltpu.make_async_remote_copy(
                src_ref=g_in[i].at[origin], dst_ref=g_in[i].at[origin],
                send_sem=send_sems.at[i * nf + k], recv_sem=recv_sems.at[i * nf + k],
                device_id=sibling, device_id_type=MESH)
        sends = []
        for i in range(n):
            for k, m in enumerate(FORWARD_MASKS):
                _, origin = _peer(m)
                cp = copy(i, k, origin)
                cp.start()
                sends.append(cp)
        for i in range(n):
            for k, m in enumerate(FORWARD_MASKS):
                _, origin = _peer(m ^ 1)
                copy(i, k, origin).wait_recv()
        for cp in sends:
            cp.wait_send()

    outs = pl.pallas_call(
        body, name=name,
        out_shape=[jax.ShapeDtypeStruct(g.shape, g.dtype) for g in gs],
        in_specs=[HBM_SPEC] * n, out_specs=[HBM_SPEC] * n,
        input_output_aliases={i: i for i in range(n)},
        scratch_shapes=[pltpu.SemaphoreType.DMA((n * nf,)), pltpu.SemaphoreType.DMA((n * nf,))],
    )(*gs)
    return list(outs)


def _scatter_copy(g_ref, l_ref, i, m, send_sems, recv_sems):
    peer, pidx = _peer(m)
    return pltpu.make_async_remote_copy(
        src_ref=g_ref.at[pidx], dst_ref=l_ref.at[m - 1],
        send_sem=send_sems.at[_sem(i, m)], recv_sem=recv_sems.at[_sem(i, m)],
        device_id=peer, device_id_type=MESH)


def scatter_start(grads, after, name):
    n = len(grads)
    lands = [lax.empty((N_DEV - 1,) + g.shape[1:], g.dtype) for g in grads]

    def body(*refs):
        g_in, l_in = refs[:n], refs[n:2 * n]
        send_sems, recv_sems = refs[2 * n + 1], refs[2 * n + 2]
        token = refs[-1]
        for i in range(n):
            for m in range(1, N_DEV):
                _scatter_copy(g_in[i], l_in[i], i, m, send_sems, recv_sems).start()
        token[...] = jnp.zeros_like(token)

    outs = pl.pallas_call(
        body, name=name,
        out_shape=(*_sem_pairs(n), *[pltpu.HBM(g.shape, g.dtype) for g in grads],
                   *[pltpu.HBM(l.shape, l.dtype) for l in lands], jax.ShapeDtypeStruct((8, 128), F32)),
        in_specs=[HBM_SPEC] * (2 * n) + [ANY_SPEC],
        out_specs=(SEM_SPEC, SEM_SPEC, *[HBM_SPEC] * (2 * n), VMEM_SPEC),
        input_output_aliases={i: 2 + i for i in range(2 * n)},
        compiler_params=pltpu.CompilerParams(has_side_effects=SIDE_EFFECT),
    )(*[_hbm(g) for g in grads], *[_hbm(l) for l in lands], after)
    return outs[0], outs[1], list(outs[2:2 + n]), list(outs[2 + n:2 + 2 * n]), outs[-1]


def scatter_wait(grads, lands, send_sems, recv_sems, after, name):
    n = len(grads)

    def body(*refs):
        g_in, l_in = refs[:n], refs[n:2 * n]
        send, recv = refs[2 * n], refs[2 * n + 1]
        for i in range(n):
            for m in range(1, N_DEV):
                cp = _scatter_copy(g_in[i], l_in[i], i, m, send, recv)
                cp.wait_send()
                cp.wait_recv()

    outs = pl.pallas_call(
        body, name=name,
        out_shape=[pltpu.HBM(a.shape, a.dtype) for a in list(grads) + list(lands)],
        in_specs=[HBM_SPEC] * (2 * n) + [SEM_SPEC, SEM_SPEC, ANY_SPEC],
        out_specs=[HBM_SPEC] * (2 * n),
        input_output_aliases={i: i for i in range(2 * n)},
        compiler_params=pltpu.CompilerParams(has_side_effects=SIDE_EFFECT),
    )(*grads, *lands, send_sems, recv_sems, after)
    return list(outs[:n]), list(outs[n:])


def sum_slots(g, name):
    _, r, c = g.shape

    def body(g_ref, out_ref):
        acc = g_ref[0]
        for p in range(1, N_DEV):
            acc = acc + g_ref[p]
        out_ref[...] = acc

    return pl.pallas_call(
        body, name=name, out_shape=jax.ShapeDtypeStruct((r, c), F32),
        in_specs=[VMEM_SPEC], out_specs=VMEM_SPEC,
        compiler_params=pltpu.CompilerParams(vmem_limit_bytes=VMEM_LIMIT),
    )(g)


def _sigmoid(v):
    return 1.0 / (1.0 + jnp.exp(-v))


def _rms_fwd(xf, g):
    r = lax.rsqrt(jnp.mean(xf * xf, axis=-1, keepdims=True) + EPS)
    return xf * r, r


def _rms_bwd(xhat, r, g, dy):
    dg = jnp.sum(dy * xhat, axis=0, keepdims=True)
    dxh = dy * g
    dx = r * (dxh - xhat * jnp.mean(dxh * xhat, axis=-1, keepdims=True))
    return dx, dg


def _ln_stats(v):
    mu = jnp.mean(v, axis=-1, keepdims=True)
    vc = v - mu
    r = lax.rsqrt(jnp.mean(vc * vc, axis=-1, keepdims=True) + EPS)
    return vc * r, r


def _ln_bwd(xhat, r, dxh):
    return r * (dxh - jnp.mean(dxh, axis=-1, keepdims=True)
                - xhat * jnp.mean(dxh * xhat, axis=-1, keepdims=True))


def _dot(a, b):
    return jnp.dot(a, b, preferred_element_type=F32)


def _dot_nt(a, b):
    return lax.dot_general(a, b, (((1,), (1,)), ((), ())), preferred_element_type=F32)


def _dot_tn(a, b):
    return lax.dot_general(a, b, (((0,), (0,)), ((), ())), preferred_element_type=F32)


def _full_weight(w_ref, kind):
    assert kind == "row"
    p, a, b = w_ref.shape
    return w_ref[...].reshape(p * a, b)


def _wspec(wg):
    return pl.BlockSpec(wg.shape, lambda *_: (0, 0, 0))


def mm_rows(a, wg, kind, *, gain=None, residual=None, out_dtype=F32, name, tm=None):
    m, k = a.shape
    p, wa, wb = wg.shape
    n = p * wb if kind == "col" else wb
    tm = _row_tile(m, tm)
    has_gain, has_res = gain is not None, residual is not None

    def body(*refs):
        refs = list(refs)
        a_ref = refs.pop(0)
        g_ref = refs.pop(0) if has_gain else None
        w_ref = refs.pop(0)
        r_ref = refs.pop(0) if has_res else None
        o_ref = refs.pop(0)
        if has_gain:
            xhat, _ = _rms_fwd(a_ref[...].astype(F32), None)
            h = (xhat * g_ref[...]).astype(BF16)
        else:
            h = a_ref[...].astype(BF16)
        if kind == "col":
            for j in range(p):
                o = _dot(h, w_ref[j])
                if has_res:
                    o = o + r_ref[:, j * wb:(j + 1) * wb]
                o_ref[:, j * wb:(j + 1) * wb] = o.astype(out_dtype)
        else:
            o = _dot(h, _full_weight(w_ref, "row"))
            if has_res:
                o = o + r_ref[...]
            o_ref[...] = o.astype(out_dtype)

    operands = [a]
    in_specs = [pl.BlockSpec((tm, k), lambda i: (i, 0))]
    if has_gain:
        operands.append(gain.reshape(1, k))
        in_specs.append(pl.BlockSpec((1, k), lambda i: (0, 0)))
    operands.append(wg)
    in_specs.append(_wspec(wg))
    if has_res:
        operands.append(residual)
        in_specs.append(pl.BlockSpec((tm, n), lambda i: (i, 0)))
    return pl.pallas_call(
        body, name=name, grid=(m // tm,),
        out_shape=jax.ShapeDtypeStruct((m, n), out_dtype),
        in_specs=in_specs, out_specs=pl.BlockSpec((tm, n), lambda i: (i, 0)),
        compiler_params=_params("parallel"),
    )(*operands)


def mm_nt(dz, wg, kind, *, x=None, gain=None, dx_in=None, name, tm=None):
    m, n = dz.shape
    p, wa, wb = wg.shape
    k = wa if kind == "col" else p * wa
    tm = _row_tile(m, tm)
    epi = x is not None
    has_dx = dx_in is not None

    def body(*refs):
        refs = list(refs)
        dz_ref, w_ref = refs.pop(0), refs.pop(0)
        if epi:
            x_ref, g_ref = refs.pop(0), refs.pop(0)
            dxi_ref = refs.pop(0) if has_dx else None
            dx_ref, h_ref, dg_ref = refs
        else:
            (da_ref,) = refs
        dzb = dz_ref[...].astype(BF16)
        if kind == "col":
            da = _dot_nt(dzb[:, 0:wb], w_ref[0])
            for j in range(1, p):
                da = da + _dot_nt(dzb[:, j * wb:(j + 1) * wb], w_ref[j])
        else:
            da = _dot_nt(dzb, _full_weight(w_ref, "row"))
        if not epi:
            da_ref[...] = da
            return
        g = g_ref[...]
        xhat, r = _rms_fwd(x_ref[...].astype(F32), None)
        h_ref[...] = (xhat * g).astype(BF16)
        dx, dg = _rms_bwd(xhat, r, g, da)
        if has_dx:
            dx = dx + dxi_ref[...]
        dx_ref[...] = dx

        @pl.when(pl.program_id(0) == 0)
        def _():
            dg_ref[...] = jnp.zeros_like(dg_ref)
        dg_ref[...] += dg

    row = lambda i: (i, 0)
    operands = [dz, wg]
    in_specs = [pl.BlockSpec((tm, n), row), _wspec(wg)]
    if epi:
        operands += [x, gain.reshape(1, k)]
        in_specs += [pl.BlockSpec((tm, k), row), pl.BlockSpec((1, k), lambda i: (0, 0))]
        if has_dx:
            operands.append(dx_in)
            in_specs.append(pl.BlockSpec((tm, k), row))
        out_shape = [jax.ShapeDtypeStruct((m, k), F32), jax.ShapeDtypeStruct((m, k), BF16),
                     jax.ShapeDtypeStruct((1, k), F32)]
        out_specs = [pl.BlockSpec((tm, k), row), pl.BlockSpec((tm, k), row),
                     pl.BlockSpec((1, k), lambda i: (0, 0))]
    else:
        out_shape = jax.ShapeDtypeStruct((m, k), F32)
        out_specs = pl.BlockSpec((tm, k), row)
    return pl.pallas_call(
        body, name=name, grid=(m // tm,), out_shape=out_shape,
        in_specs=in_specs, out_specs=out_specs,
        compiler_params=_params("arbitrary"),
    )(*operands)


def mm_tn(a, b, *, nb, a_spec, b_spec, ka, nbk, tm, m, scale=1.0, out_dtype=BF16, col_slots=1,
          after=None, name):
    ni = m // tm
    assert col_slots == 1 or nb == 1
    cw = nbk // col_slots
    extra = [] if after is None else [after]

    def body(a_ref, b_ref, *rest):
        o_ref, acc = rest[len(extra):]
        i = pl.program_id(1)

        @pl.when(i == 0)
        def _():
            acc[...] = jnp.zeros_like(acc)
        acc[...] += _dot_tn(a_ref[...].astype(BF16), b_ref[...].astype(BF16))

        @pl.when(i == ni - 1)
        def _():
            if col_slots == 1:
                o_ref[...] = (acc[...] * scale).astype(out_dtype)
            else:
                for j in range(col_slots):
                    o_ref[j] = (acc[:, j * cw:(j + 1) * cw] * scale).astype(out_dtype)

    if col_slots == 1:
        out_shape = jax.ShapeDtypeStruct((nb, ka, nbk), out_dtype)
        out_spec = pl.BlockSpec((None, ka, nbk), lambda s, i: (s, 0, 0))
    else:
        out_shape = jax.ShapeDtypeStruct((col_slots, ka, cw), out_dtype)
        out_spec = pl.BlockSpec((col_slots, ka, cw), lambda s, i: (0, 0, 0))
    return pl.pallas_call(
        body, name=name, grid=(nb, ni), out_shape=out_shape,
        in_specs=[a_spec, b_spec] + [ANY_SPEC] * len(extra), out_specs=out_spec,
        scratch_shapes=[pltpu.VMEM((ka, nbk), F32)],
        compiler_params=_params("parallel", "arbitrary"),
    )(a, b, *extra)


def _ffn_specs(w_in_g, w_out_g, d):
    nf = w_in_g.shape[1]
    hr = w_out_g.shape[1]
    assert 2 * hr == nf
    w_in5 = w_in_g.reshape(2, 4, nf, d)
    w_out5 = w_out_g.reshape(4, 2, hr, d)
    in_spec = pl.BlockSpec((2, None, nf, d), lambda i, j: (0, j, 0, 0))
    out_spec = pl.BlockSpec((None, 2, hr, d), lambda i, j: (j, 0, 0, 0))
    return w_in5, w_out5, in_spec, out_spec, nf


def ffn_fwd(x, gain, w_in_g, w_out_g, *, name, tm=None):
    t, d = x.shape
    tm = _row_tile(t, tm)
    w_in5, w_out5, wi_spec, wo_spec, nf = _ffn_specs(w_in_g, w_out_g, d)

    def body(x_ref, g_ref, wi_ref, wo_ref, o_ref, gu_ref, h_scr, acc):
        j = pl.program_id(1)

        @pl.when(j == 0)
        def _():
            xhat, _ = _rms_fwd(x_ref[...], None)
            h_scr[...] = (xhat * g_ref[...]).astype(BF16)
            acc[...] = jnp.zeros_like(acc)
        wo = wo_ref[...].reshape(nf, d)

        def project(rows):
            h = h_scr[rows]
            return _dot_nt(h, wi_ref[0]), _dot_nt(h, wi_ref[1])

        sub = tm // FFN_FWD_SPLIT
        parts = [slice(k * sub, (k + 1) * sub) for k in range(FFN_FWD_SPLIT)]
        gt, up = project(parts[0])
        for k, rows in enumerate(parts):
            if k + 1 < len(parts):
                nxt = project(parts[k + 1])
            gu_ref[0, rows] = gt.astype(BF16)
            gu_ref[1, rows] = up.astype(BF16)
            act = (gt * _sigmoid(gt) * up).astype(BF16)
            acc[rows] += _dot(act, wo)
            if k + 1 < len(parts):
                gt, up = nxt

        @pl.when(j == 3)
        def _():
            o_ref[...] = x_ref[...] + 0.5 * acc[...]

    return pl.pallas_call(
        body, name=name, grid=(t // tm, 4),
        out_shape=[jax.ShapeDtypeStruct((t, d), F32), jax.ShapeDtypeStruct((2, 4, t, nf), BF16)],
        in_specs=[pl.BlockSpec((tm, d), lambda i, j: (i, 0)),
                  pl.BlockSpec((1, d), lambda i, j: (0, 0)), wi_spec, wo_spec],
        out_specs=[pl.BlockSpec((tm, d), lambda i, j: (i, 0)),
                   pl.BlockSpec((2, None, tm, nf), lambda i, j: (0, j, i, 0))],
        scratch_shapes=[pltpu.VMEM((tm, d), BF16), pltpu.VMEM((tm, d), F32)],
        compiler_params=_params("parallel", "arbitrary"),
    )(x, gain.reshape(1, d), w_in5, w_out5)


def ffn_bwd_rows(x, dy, gu, gain, w_in_g, w_out_g, *, name, tm=None):
    t, d = x.shape
    tm = _row_tile(t, tm)
    w_in5, w_out5, wi_spec, wo_spec, nf = _ffn_specs(w_in_g, w_out_g, d)

    def body(x_ref, dy_ref, gu_ref, g_ref, wi_ref, wo_ref, dx_ref, h_ref, act_ref, dgu_ref, dg_ref,
             dyh_scr, dh_acc):
        i, j = pl.program_id(0), pl.program_id(1)

        @pl.when(j == 0)
        def _():
            xhat, _ = _rms_fwd(x_ref[...], None)
            h_ref[...] = (xhat * g_ref[...]).astype(BF16)
            dyh_scr[...] = (0.5 * dy_ref[...]).astype(BF16)
            dh_acc[...] = jnp.zeros_like(dh_acc)
        wo = wo_ref[...].reshape(nf, d)

        def gates(rows):
            gt = gu_ref[0, rows].astype(F32)
            up = gu_ref[1, rows].astype(F32)
            sg = _sigmoid(gt)
            silu = gt * sg
            act_ref[rows] = (silu * up).astype(BF16)
            return up * (sg * (1.0 + gt * (1.0 - sg))), silu

        def grads(rows, dact, dsilu_up, silu):
            dgt = (dact * dsilu_up).astype(BF16)
            dup = (dact * silu).astype(BF16)
            dgu_ref[0, rows] = dgt
            dgu_ref[1, rows] = dup
            return dgt, dup

        sub = tm // FFN_BWD_SPLIT
        parts = [slice(k * sub, (k + 1) * sub) for k in range(FFN_BWD_SPLIT)]
        dact = _dot_nt(dyh_scr[parts[0]], wo)
        gate = gates(parts[0])
        for k, rows in enumerate(parts):
            if k + 1 < len(parts):
                dact_next = _dot_nt(dyh_scr[parts[k + 1]], wo)
            dgt, dup = grads(rows, dact, *gate)
            dh_acc[rows] += _dot(dgt, wi_ref[0]) + _dot(dup, wi_ref[1])
            if k + 1 < len(parts):
                gate = gates(parts[k + 1])
                dact = dact_next

        @pl.when(j == 3)
        def _():
            g = g_ref[...]
            xhat, r = _rms_fwd(x_ref[...], None)
            dx, dg = _rms_bwd(xhat, r, g, dh_acc[...])
            dx_ref[...] = dy_ref[...] + dx

            @pl.when(i == 0)
            def _():
                dg_ref[...] = jnp.zeros_like(dg_ref)
            dg_ref[...] += dg

    row = lambda i, j: (i, 0)
    return pl.pallas_call(
        body, name=name, grid=(t // tm, 4),
        out_shape=[jax.ShapeDtypeStruct((t, d), F32), jax.ShapeDtypeStruct((t, d), BF16),
                   jax.ShapeDtypeStruct((4, t, nf), BF16), jax.ShapeDtypeStruct((2, 4, t, nf), BF16),
                   jax.ShapeDtypeStruct((1, d), F32), jax.ShapeDtypeStruct((t, d), BF16)],
        in_specs=[pl.BlockSpec((tm, d), row), pl.BlockSpec((tm, d), row),
                  pl.BlockSpec((2, None, tm, nf), lambda i, j: (0, j, i, 0)),
                  pl.BlockSpec((1, d), lambda i, j: (0, 0)), wi_spec, wo_spec],
        out_specs=[pl.BlockSpec((tm, d), row), pl.BlockSpec((tm, d), row),
                   pl.BlockSpec((None, tm, nf), lambda i, j: (j, i, 0)),
                   pl.BlockSpec((2, None, tm, nf), lambda i, j: (0, j, i, 0)),
                   pl.BlockSpec((1, d), lambda i, j: (0, 0)), pl.BlockSpec((tm, d), row)],
        scratch_shapes=[pltpu.VMEM((tm, d), F32)],
        compiler_params=_params("arbitrary", "arbitrary"),
    )(x, dy, gu, gain.reshape(1, d), w_in5, w_out5)


def ffn_grad_w_in(h, dgu, after, *, name):
    t, d = h.shape
    nf = dgu.shape[-1]
    tm = _row_tile(t, TN_TILE)
    return mm_tn(dgu.reshape(8, t, nf), h, nb=8, ka=nf, nbk=d, tm=tm, m=t, after=after,
                 a_spec=pl.BlockSpec((None, tm, nf), lambda s, i: (s, i, 0)),
                 b_spec=pl.BlockSpec((tm, d), lambda s, i: (i, 0)), name=name)


def ffn_grad_w_out(act, dyh, after, *, name):
    _, t, nf = act.shape
    d = dyh.shape[1]
    tm = _row_tile(t, TN_TILE)
    d_w_out = mm_tn(act, dyh, nb=4, ka=nf, nbk=d, tm=tm, m=t, after=after,
                    a_spec=pl.BlockSpec((None, tm, nf), lambda s, i: (s, i, 0)),
                    b_spec=pl.BlockSpec((tm, d), lambda s, i: (i, 0)), name=name)
    return d_w_out.reshape(8, nf // 2, d)


def _lane_group(shape):
    return lax.shift_right_logical(lax.broadcasted_iota(jnp.int32, shape, 1), 6)


def _pool_count(t0, rows):
    t = (t0 + lax.broadcasted_iota(jnp.int32, (rows, MIX_W), 0) + 1).astype(F32)
    return jnp.minimum(t, _by_group(_lane_group((rows, MIX_W)), 2.0, 4.0, 8.0, 16.0))


def _by_group(grp, v0, v1, v2, v3):
    return jnp.where(grp == 0, v0, jnp.where(grp == 1, v1, jnp.where(grp == 2, v2, v3)))


def _sgu_mix(wt_ref, vnc):
    grp = _lane_group((SGU_CHUNK, MIX_W))
    out = jnp.zeros((SGU_CHUNK, MIX_W), F32)
    for hd in range(N_HEADS):
        out = jnp.where(grp == hd, _dot(wt_ref[hd], vnc), out)
    return out


def _pool_fwd(s1, s2, s3, t0, ts, lo):
    h = lo
    s2[h - 24:h + ts] = s1[h - 24:h + ts] + s1[h - 25:h + ts - 1]
    s3[h - 16:h + ts] = s2[h - 16:h + ts] + s2[h - 18:h + ts - 2]
    sum2 = s2[h:h + ts]
    sum4 = s3[h:h + ts]
    s2[h - 8:h + ts] = s3[h - 8:h + ts] + s3[h - 12:h + ts - 4]
    sum8 = s2[h:h + ts]
    sum16 = sum8 + s2[h - 8:h + ts - 8]
    grp = _lane_group((ts, MIX_W))
    return _by_group(grp, sum2, sum4, sum8, sum16) / _pool_count(t0, ts) - s1[h:h + ts]


def _make_shifts(src, sh, rows):
    for b in range(1, 8):
        sh[b, 0:rows] = src[b:b + rows]


def _rows_at(src, sh, start, n):
    a, b = divmod(start, 8)
    return src[8 * a:8 * a + n] if b == 0 else sh[b, 8 * a:8 * a + n]


def mixer_fwd(z, sconv, cconv, vecs, wt, bexp, pbd, *, name, ts=None):
    t = z.shape[0]
    ts = _row_tile(t, MIX_TILE if ts is None else ts)
    hl = HALO
    w = MIX_W
    nch = ts // SGU_CHUNK

    def body(zc, zp, sconv_ref, cconv_ref, vec_ref, wt_ref, bexp_ref, pbd_ref, y_ref, s1, s2, s3, sh):
        i = pl.program_id(0)
        has_prev = i > 0

        def col(ref, c):
            return ref[:, c * w:(c + 1) * w]

        def prev(c):
            return jnp.where(has_prev, col(zp, c), 0.0)

        s1[0:hl] = prev(1) * prev(2)
        s1[hl:hl + ts] = col(zc, 1) * col(zc, 2)
        cv = sconv_ref[0:1] * s1[hl - 2:hl - 2 + ts]
        for k in range(1, SCONV_K):
            cv = cv + sconv_ref[k:k + 1] * s1[hl - 2 + k:hl - 2 + k + ts]
        y_ref[:, 0:w] = (col(zc, 0) * cv).astype(BF16)

        xhat, _ = _ln_stats(col(zc, 4))
        vn = (xhat * vec_ref[0:1]).astype(BF16)
        for c in range(nch):
            rows = slice(c * SGU_CHUNK, (c + 1) * SGU_CHUNK)
            mixed = _sgu_mix(wt_ref, vn[rows]) + bexp_ref[...]
            y_ref[rows, w:2 * w] = (zc[rows, 3 * w:4 * w] * mixed).astype(BF16)

        s1[0:hl] = prev(5) * _sigmoid(prev(6))
        s1[hl:hl + ts] = col(zc, 5) * _sigmoid(col(zc, 6))
        off = hl - (CCONV_K - 1)
        _make_shifts(s1, sh, hl + ts - 8)
        cv = cconv_ref[0:1] * _rows_at(s1, sh, off, ts)
        for k in range(1, CCONV_K):
            cv = cv + cconv_ref[k:k + 1] * _rows_at(s1, sh, off + k, ts)
        xhat, _ = _ln_stats(cv)
        ln = xhat * vec_ref[1:2] + vec_ref[2:3]
        y_ref[:, 2 * w:3 * w] = (ln * _sigmoid(ln)).astype(BF16)

        s1[0:hl] = prev(7)
        s1[hl:hl + ts] = col(zc, 7)
        pooled = _pool_fwd(s1, s2, s3, i * ts, ts, hl)
        y_ref[:, 3 * w:4 * w] = (_dot(pooled.astype(BF16), pbd_ref[...]) * vec_ref[3:4]).astype(BF16)

    full = lambda shape: pl.BlockSpec(shape, lambda i: (0,) * len(shape))
    return pl.pallas_call(
        body, name=name, grid=(t // ts,),
        out_shape=jax.ShapeDtypeStruct((t, 4 * w), BF16),
        in_specs=[pl.BlockSpec((ts, 8 * w), lambda i: (i, 0)),
                  pl.BlockSpec((hl, 8 * w), lambda i: (jnp.maximum(i * (ts // hl) - 1, 0), 0)),
                  full((8, w)), full((32, w)), full((8, w)), full((N_HEADS, SGU_CHUNK, SGU_CHUNK)),
                  full((SGU_CHUNK, w)), full((w, w))],
        out_specs=pl.BlockSpec((ts, 4 * w), lambda i: (i, 0)),
        scratch_shapes=[pltpu.VMEM((hl + ts, w), F32)] * 3 + [pltpu.VMEM((8, hl + ts, w), F32)],
        compiler_params=_params("parallel"),
    )(z, z, sconv, cconv, vecs, wt, bexp, pbd)


def mixer_bwd(z, dy, sconv, cconv, vecs, wt, bexp, pbd, *, name, ts=None):
    t = z.shape[0]
    ts = _row_tile(t, MIX_TILE if ts is None else ts)
    hl = HALO
    w = MIX_W
    nch = ts // SGU_CHUNK
    ni = t // ts
    ext = ts + hl

    def body(zc, zp, zn, dyc, dyn, sconv_ref, cconv_ref, vec_ref, wt_ref, bexp_ref, pbd_ref,
             dz_ref, gvec_ref, gcc_ref, gwt_ref, gb_ref, gpbd_ref, s1, s2, s3, sh1, sh3):
        i = pl.program_id(0)
        has_prev = i > 0
        has_next = i < ni - 1

        @pl.when(i == 0)
        def _():
            gvec_ref[...] = jnp.zeros_like(gvec_ref)
            gcc_ref[...] = jnp.zeros_like(gcc_ref)
            gwt_ref[...] = jnp.zeros_like(gwt_ref)
            gb_ref[...] = jnp.zeros_like(gb_ref)
            gpbd_ref[...] = jnp.zeros_like(gpbd_ref)

        def col(ref, c):
            return ref[:, c * w:(c + 1) * w]

        def prev(c):
            return jnp.where(has_prev, col(zp, c), 0.0)

        def nxt(c):
            return jnp.where(has_next, col(zn, c), 0.0)

        def dnext(c):
            return jnp.where(has_next, col(dyn, c), 0.0)

        def rowsum(v):
            return jnp.sum(v, axis=0, keepdims=True)

        s1[0:hl] = prev(1) * prev(2)
        s1[hl:hl + ts] = col(zc, 1) * col(zc, 2)
        s1[hl + ts:hl + ts + hl] = nxt(1) * nxt(2)
        cv = sconv_ref[0:1] * s1[hl - 2:hl - 2 + ts]
        for k in range(1, SCONV_K):
            cv = cv + sconv_ref[k:k + 1] * s1[hl - 2 + k:hl - 2 + k + ts]
        dya = col(dyc, 0)
        dz_ref[:, 0:w] = (dya * cv).astype(BF16)
        s2[0:ts] = dya * col(zc, 0)
        s2[ts:ext] = dnext(0) * nxt(0)
        dv = sconv_ref[0:1] * s2[2:2 + ts]
        for k in range(1, SCONV_K):
            dv = dv + sconv_ref[k:k + 1] * s2[2 - k:2 - k + ts]
        dz_ref[:, w:2 * w] = (dv * col(zc, 2)).astype(BF16)
        dz_ref[:, 2 * w:3 * w] = (dv * col(zc, 1)).astype(BF16)
        dcv = s2[0:ts]
        for k in range(SCONV_K):
            gvec_ref[k:k + 1] += rowsum(dcv * s1[hl - 2 + k:hl - 2 + k + ts])

        g_sgu = vec_ref[0:1]
        xhat, rstd = _ln_stats(col(zc, 4))
        vn = (xhat * g_sgu).astype(BF16)
        grp = _lane_group((SGU_CHUNK, w))
        lane = lax.broadcasted_iota(jnp.int32, (SGU_CHUNK, SGU_CHUNK), 1)
        tril = lax.broadcasted_iota(jnp.int32, (SGU_CHUNK, SGU_CHUNK), 0) >= lane
        for c in range(nch):
            rows = slice(c * SGU_CHUNK, (c + 1) * SGU_CHUNK)
            vnc = vn[rows]
            mixed = _sgu_mix(wt_ref, vnc) + bexp_ref[...]
            dyb = dyc[rows, w:2 * w]
            dz_ref[rows, 3 * w:4 * w] = (dyb * mixed).astype(BF16)
            dmix = dyb * zc[rows, 3 * w:4 * w]
            dmixb = dmix.astype(BF16)
            dvn = jnp.zeros((SGU_CHUNK, w), F32)
            gb = jnp.zeros((SGU_CHUNK, SGU_CHUNK), F32)
            for hd in range(N_HEADS):
                dvn = jnp.where(grp == hd, _dot_tn(wt_ref[hd], dmixb), dvn)
                dm_h = jnp.where(grp == hd, dmix, 0.0)
                gwt_ref[hd] += jnp.where(tril, _dot_nt(dm_h.astype(BF16), vnc), 0.0)
                gb = gb + jnp.where(lane == hd, jnp.sum(dm_h, axis=1, keepdims=True), 0.0)
            gb_ref[...] += gb
            s3[rows] = dvn
        dvn = s3[0:ts]
        gvec_ref[3:4] += rowsum(dvn * xhat)
        dz_ref[:, 4 * w:5 * w] = _ln_bwd(xhat, rstd, dvn * g_sgu).astype(BF16)

        sig_c = _sigmoid(col(zc, 6))
        s1[0:hl] = prev(5) * _sigmoid(prev(6))
        s1[hl:hl + ts] = col(zc, 5) * sig_c
        s1[hl + ts:hl + ts + hl] = nxt(5) * _sigmoid(nxt(6))
        off = hl - (CCONV_K - 1)
        _make_shifts(s1, sh1, ts + 2 * hl - 8)
        cv = cconv_ref[0:1] * _rows_at(s1, sh1, off, ext)
        for k in range(1, CCONV_K):
            cv = cv + cconv_ref[k:k + 1] * _rows_at(s1, sh1, off + k, ext)
        xhat, rstd = _ln_stats(cv)
        ln = xhat * vec_ref[1:2] + vec_ref[2:3]
        sg = _sigmoid(ln)
        s2[0:ts] = col(dyc, 2)
        s2[ts:ext] = dnext(2)
        dln = s2[0:ext] * (sg * (1.0 + ln * (1.0 - sg)))
        gvec_ref[4:5] += rowsum(dln[0:ts] * xhat[0:ts])
        gvec_ref[5:6] += rowsum(dln[0:ts])
        s3[0:ext] = _ln_bwd(xhat, rstd, dln * vec_ref[1:2])
        _make_shifts(s3, sh3, ext - 8)
        dyg = cconv_ref[0:1] * _rows_at(s3, sh3, CCONV_K - 1, ts)
        for k in range(1, CCONV_K):
            dyg = dyg + cconv_ref[k:k + 1] * _rows_at(s3, sh3, CCONV_K - 1 - k, ts)
        dz_ref[:, 5 * w:6 * w] = (dyg * sig_c).astype(BF16)
        dz_ref[:, 6 * w:7 * w] = (dyg * col(zc, 5) * sig_c * (1.0 - sig_c)).astype(BF16)
        dcv = s3[0:ts]
        for k in range(CCONV_K):
            gcc_ref[k:k + 1] += rowsum(dcv * _rows_at(s1, sh1, off + k, ts))

        scale = vec_ref[3:4]
        s1[0:hl] = prev(7)
        s1[hl:hl + ts] = col(zc, 7)
        pooled = _pool_fwd(s1, s2, s3, i * ts, ts, hl).astype(BF16)
        q0 = _dot(pooled, pbd_ref[...])
        dyd = col(dyc, 3)
        gvec_ref[6:7] += rowsum(dyd * q0)
        dq = (dyd * scale).astype(BF16)
        gpbd_ref[...] += _dot_tn(pooled, dq)
        s1[0:ts] = _dot_nt(dq, pbd_ref[...])
        s1[ts:ext] = _dot_nt((dnext(3) * scale).astype(BF16), pbd_ref[...])
        dpool = s1[0:ts]
        s2[0:ext] = s1[0:ext] / _pool_count(i * ts, ext)
        s3[0:ts + 24] = s2[0:ts + 24] + s2[1:ts + 25]
        f2 = s3[0:ts]
        s2[0:ts + 16] = s3[0:ts + 16] + s3[2:ts + 18]
        f4 = s2[0:ts]
        s3[0:ts + 8] = s2[0:ts + 8] + s2[4:ts + 12]
        f8 = s3[0:ts]
        f16 = f8 + s3[8:ts + 8]
        dz_ref[:, 7 * w:8 * w] = (_by_group(_lane_group((ts, w)), f2, f4, f8, f16) - dpool).astype(BF16)

    full = lambda shape: pl.BlockSpec(shape, lambda i: (0,) * len(shape))
    r = ts // hl
    prev_map = lambda i: (jnp.maximum(i * r - 1, 0), 0)
    next_map = lambda i: (jnp.minimum((i + 1) * r, t // hl - 1), 0)
    return pl.pallas_call(
        body, name=name, grid=(ni,),
        out_shape=[jax.ShapeDtypeStruct((t, 8 * w), BF16), jax.ShapeDtypeStruct((8, w), F32),
                   jax.ShapeDtypeStruct((32, w), F32),
                   jax.ShapeDtypeStruct((N_HEADS, SGU_CHUNK, SGU_CHUNK), F32),
                   jax.ShapeDtypeStruct((SGU_CHUNK, SGU_CHUNK), F32), jax.ShapeDtypeStruct((w, w), F32)],
        in_specs=[pl.BlockSpec((ts, 8 * w), lambda i: (i, 0)),
                  pl.BlockSpec((hl, 8 * w), prev_map), pl.BlockSpec((hl, 8 * w), next_map),
                  pl.BlockSpec((ts, 4 * w), lambda i: (i, 0)), pl.BlockSpec((hl, 4 * w), next_map),
                  full((8, w)), full((32, w)), full((8, w)), full((N_HEADS, SGU_CHUNK, SGU_CHUNK)),
                  full((SGU_CHUNK, w)), full((w, w))],
        out_specs=[pl.BlockSpec((ts, 8 * w), lambda i: (i, 0)), full((8, w)), full((32, w)),
                   full((N_HEADS, SGU_CHUNK, SGU_CHUNK)), full((SGU_CHUNK, SGU_CHUNK)), full((w, w))],
        scratch_shapes=[pltpu.VMEM((ts + 2 * hl, w), F32)] * 3 + [pltpu.VMEM((8, ts + 2 * hl, w), F32)] * 2,
        compiler_params=_params("arbitrary"),
    )(z, z, z, dy, dy, sconv, cconv, vecs, wt, bexp, pbd)


def _attn_head(q, kv_ref, hd, d):
    hw = d // N_HEADS
    qh = q[:, hd * hw:(hd + 1) * hw]
    kh = kv_ref[:, hd * hw:(hd + 1) * hw].astype(BF16)
    vh = kv_ref[:, d + hd * hw:d + (hd + 1) * hw].astype(BF16)
    s = _dot_nt(qh, kh) * (1.0 / (hw ** 0.5))
    e = jnp.exp(s - jnp.max(s, axis=-1, keepdims=True))
    p = e / jnp.sum(e, axis=-1, keepdims=True)
    return qh, kh, vh, p


def xattn_fwd(x, gain, kv, wq_g, wo_g, *, name, tm=None):
    t, d = x.shape
    nm = kv.shape[0]
    tm = _row_tile(t, tm)
    hw = d // N_HEADS

    def body(x_ref, g_ref, kv_ref, wq_ref, wo_ref, o_ref):
        xv = x_ref[...]
        xhat, _ = _rms_fwd(xv, None)
        h = (xhat * g_ref[...]).astype(BF16)
        q = _dot(h, _full_weight(wq_ref, "row")).astype(BF16)
        wo = _full_weight(wo_ref, "row")
        out = xv
        for hd in range(N_HEADS):
            _, _, vh, p = _attn_head(q, kv_ref, hd, d)
            oh = _dot(p.astype(BF16), vh).astype(BF16)
            out = out + _dot(oh, wo[hd * hw:(hd + 1) * hw])
        o_ref[...] = out

    row = lambda i: (i, 0)
    return pl.pallas_call(
        body, name=name, grid=(t // tm,),
        out_shape=jax.ShapeDtypeStruct((t, d), F32),
        in_specs=[pl.BlockSpec((tm, d), row), pl.BlockSpec((1, d), lambda i: (0, 0)),
                  pl.BlockSpec((nm, 2 * d), lambda i: (0, 0)), _wspec(wq_g), _wspec(wo_g)],
        out_specs=pl.BlockSpec((tm, d), row),
        compiler_params=_params("parallel"),
    )(x, gain.reshape(1, d), kv, wq_g, wo_g)


def xattn_bwd_rows(x, dxn, gain, kv, wq_g, wo_g, *, name, tm=None):
    t, d = x.shape
    nm = kv.shape[0]
    tm = _row_tile(t, tm)
    hw = d // N_HEADS

    def body(x_ref, dxn_ref, g_ref, kv_ref, wq_ref, wo_ref,
             dx_ref, h_ref, dq_ref, o_ref, dkv_ref, dg_ref):
        i = pl.program_id(0)

        @pl.when(i == 0)
        def _():
            dkv_ref[...] = jnp.zeros_like(dkv_ref)
            dg_ref[...] = jnp.zeros_like(dg_ref)
        g = g_ref[...]
        xhat, r = _rms_fwd(x_ref[...], None)
        h = (xhat * g).astype(BF16)
        h_ref[...] = h
        wq = _full_weight(wq_ref, "row")
        q = _dot(h, wq).astype(BF16)
        dxn = dxn_ref[...]
        do = _dot_nt(dxn.astype(BF16), _full_weight(wo_ref, "row")).astype(BF16)
        for hd in range(N_HEADS):
            cols = slice(hd * hw, (hd + 1) * hw)
            qh, kh, vh, p = _attn_head(q, kv_ref, hd, d)
            pb = p.astype(BF16)
            o_ref[:, cols] = _dot(pb, vh).astype(BF16)
            doh = do[:, cols]
            dkv_ref[:, d + hd * hw:d + (hd + 1) * hw] += _dot_tn(pb, doh)
            dp = _dot_nt(doh, vh)
            ds = (p * (dp - jnp.sum(dp * p, axis=-1, keepdims=True)) * (1.0 / (hw ** 0.5))).astype(BF16)
            dq_ref[:, cols] = _dot(ds, kh).astype(BF16)
            dkv_ref[:, cols] += _dot_tn(ds, qh)
        dh = _dot_nt(dq_ref[...], wq)
        dx, dg = _rms_bwd(xhat, r, g, dh)
        dx_ref[...] = dxn + dx
        dg_ref[...] += dg

    row = lambda i: (i, 0)
    fix = lambda i: (0, 0)
    return pl.pallas_call(
        body, name=name, grid=(t // tm,),
        out_shape=[jax.ShapeDtypeStruct((t, d), F32), jax.ShapeDtypeStruct((t, d), BF16),
                   jax.ShapeDtypeStruct((t, d), BF16), jax.ShapeDtypeStruct((t, d), BF16),
                   jax.ShapeDtypeStruct((nm, 2 * d), F32), jax.ShapeDtypeStruct((1, d), F32)],
        in_specs=[pl.BlockSpec((tm, d), row), pl.BlockSpec((tm, d), row), pl.BlockSpec((1, d), fix),
                  pl.BlockSpec((nm, 2 * d), fix), _wspec(wq_g), _wspec(wo_g)],
        out_specs=[pl.BlockSpec((tm, d), row)] * 4 + [pl.BlockSpec((nm, 2 * d), fix),
                                                      pl.BlockSpec((1, d), fix)],
        compiler_params=_params("arbitrary"),
    )(x, dxn, gain.reshape(1, d), kv, wq_g, wo_g)


def loss_head(x, target, gain, *, name, tm=None):
    t, d = x.shape
    tm = _row_tile(t, tm)

    def body(x_ref, t_ref, g_ref, dx_ref, dg_ref, loss_ref):
        @pl.when(pl.program_id(0) == 0)
        def _():
            dg_ref[...] = jnp.zeros_like(dg_ref)
            loss_ref[...] = jnp.zeros_like(loss_ref)
        g = g_ref[...]
        xhat, r = _rms_fwd(x_ref[...], None)
        err = xhat * g - t_ref[...]
        loss_ref[...] += 0.5 * jnp.sum(jnp.sum(err * err, axis=-1, keepdims=True) / d,
                                       axis=0, keepdims=True)
        dx, dg = _rms_bwd(xhat, r, g, err / d)
        dx_ref[...] = dx
        dg_ref[...] += dg

    row = lambda i: (i, 0)
    fix = lambda i: (0, 0)
    return pl.pallas_call(
        body, name=name, grid=(t // tm,),
        out_shape=[jax.ShapeDtypeStruct((t, d), F32), jax.ShapeDtypeStruct((1, d), F32),
                   jax.ShapeDtypeStruct((1, 1), F32)],
        in_specs=[pl.BlockSpec((tm, d), row), pl.BlockSpec((tm, d), row), pl.BlockSpec((1, d), fix)],
        out_specs=[pl.BlockSpec((tm, d), row), pl.BlockSpec((1, d), fix), pl.BlockSpec((1, 1), fix)],
        compiler_params=_params("arbitrary"),
    )(x, target, gain.reshape(1, d))


def _adamw_math(w, g, m, v):
    m = ADAM_B1 * m + (1.0 - ADAM_B1) * g
    v = ADAM_B2 * v + (1.0 - ADAM_B2) * (g * g)
    m_hat = m / (1.0 - ADAM_B1 ** ADAM_STEP)
    v_hat = v / (1.0 - ADAM_B2 ** ADAM_STEP)
    delta = -ADAM_LR * (m_hat / (jnp.sqrt(v_hat) + ADAM_EPS) + ADAM_WD * w)
    return delta, m, v


def adamw_sharded(own, lands, w, m, v, me_arr, *, name):
    nl, r, c = w.shape
    assert nl == len(own) == len(lands) == 2
    tr = next(cand for cand in (256, 176, 128, r) if r % cand == 0)
    nr = r // tr

    def body(me_ref, o0, o1, l0, l1, w_ref, m_ref, v_ref, g_out, d_out, m_out, v_out):
        def total(o_ref, l_ref):
            acc = o_ref[...].astype(F32)
            for p in range(N_DEV - 1):
                acc = acc + l_ref[p].astype(F32)
            return acc
        g = jnp.where(pl.program_id(0) == 0, total(o0, l0), total(o1, l1))
        delta, mn, vn = _adamw_math(w_ref[...], g, m_ref[...], v_ref[...])
        g_out[...] = g
        d_out[...] = delta
        m_out[...] = mn
        v_out[...] = vn

    row0 = lambda l, i: jnp.where(l == 0, i, nr - 1)
    row1 = lambda l, i: jnp.where(l == 1, i, 0)
    blk = pl.BlockSpec((None, tr, c), lambda l, i, me: (l, i, 0))
    grid_spec = pltpu.PrefetchScalarGridSpec(
        num_scalar_prefetch=1, grid=(nl, nr),
        in_specs=[pl.BlockSpec((None, tr, c), lambda l, i, me: (me[0], row0(l, i), 0)),
                  pl.BlockSpec((None, tr, c), lambda l, i, me: (me[0], row1(l, i), 0)),
                  pl.BlockSpec((N_DEV - 1, tr, c), lambda l, i, me: (0, row0(l, i), 0)),
                  pl.BlockSpec((N_DEV - 1, tr, c), lambda l, i, me: (0, row1(l, i), 0)),
                  blk, blk, blk],
        out_specs=[blk] * 4)
    return pl.pallas_call(
        body, name=name, grid_spec=grid_spec,
        out_shape=[jax.ShapeDtypeStruct((nl, r, c), F32)] * 4,
        compiler_params=_params("arbitrary", "arbitrary"),
    )(me_arr, own[0], own[1], lands[0], lands[1], w, m, v)


def adamw_flat(g, w, m, v, *, name):
    def body(g_ref, w_ref, m_ref, v_ref, d_out, m_out, v_out):
        delta, mn, vn = _adamw_math(w_ref[...], g_ref[...], m_ref[...], v_ref[...])
        d_out[...] = delta
        m_out[...] = mn
        v_out[...] = vn

    return pl.pallas_call(
        body, name=name, out_shape=[jax.ShapeDtypeStruct(w.shape, F32)] * 3,
        in_specs=[VMEM_SPEC] * 4, out_specs=[VMEM_SPEC] * 3,
        compiler_params=pltpu.CompilerParams(vmem_limit_bytes=VMEM_LIMIT),
    )(g, w, m, v)


def cast_into_slot(a, layer, me_arr, *, name, dtype=None, after=None):
    dtype = BF16 if dtype is None else dtype
    _, r, c = a.shape
    tr = next(cand for cand in (256, 176, 128, r) if r % cand == 0)
    extra = [] if after is None else [after]

    def body(me_ref, a_ref, *rest):
        rest[-1][...] = a_ref[...].astype(dtype)

    grid_spec = pltpu.PrefetchScalarGridSpec(
        num_scalar_prefetch=1, grid=(r // tr,),
        in_specs=[pl.BlockSpec((None, tr, c), lambda i, me: (layer, i, 0))] + [ANY_SPEC] * len(extra),
        out_specs=pl.BlockSpec((None, tr, c), lambda i, me: (me[0], i, 0)))
    return pl.pallas_call(
        body, name=name, grid_spec=grid_spec,
        out_shape=jax.ShapeDtypeStruct((N_DEV, r, c), dtype),
        compiler_params=_params("parallel"),
    )(me_arr, a, *extra)


def _pack(arrs, rows):
    flat = jnp.concatenate([a.reshape(-1).astype(F32) for a in arrs])
    pad = rows * 128 - flat.shape[0]
    assert pad >= 0
    if pad:
        flat = jnp.concatenate([flat, jnp.zeros((pad,), F32)])
    return flat.reshape(rows, 128)


def _unpack(packed, shapes):
    flat = packed.reshape(-1)
    out, pos = [], 0
    for s in shapes:
        n = 1
        for dim in s:
            n *= dim
        out.append(flat[pos:pos + n].reshape(s))
        pos += n
    return out


def _rows_for(shapes):
    n = 0
    for s in shapes:
        k = 1
        for dim in s:
            k *= dim
        n += k
    return -(-n // 1024) * 8


GATHER_GROUPS = (("ffn1", ("ffn1_w_in", "ffn1_w_out")),
                 ("mid", ("mix_w_in", "mix_w_out", "xattn_wkv", "xattn_wq", "xattn_wo")),
                 ("ffn2", ("ffn2_w_in", "ffn2_w_out")))
SMALL_REPL = ["norm_ffn1", "norm_mix", "sgu_norm_g", "sgu_w", "sgu_b", "cconv_ln_g", "cconv_ln_b",
              "pool_w", "pool_scale", "norm_xattn", "norm_mem", "norm_ffn2", "norm_final"]
SMALL_SHARD = ["sconv_w", "cconv_w"]
TRANSPOSED = ("ffn1_w_in", "ffn2_w_in")
WEIGHTS = ["norm_ffn1", "ffn1_w_in", "ffn1_w_out", "norm_mix", "mix_w_in", "sconv_w", "sgu_norm_g",
           "sgu_w", "sgu_b", "cconv_w", "cconv_ln_g", "cconv_ln_b", "pool_w", "pool_scale", "mix_w_out",
           "norm_xattn", "norm_mem", "xattn_wq", "xattn_wkv", "xattn_wo", "norm_ffn2", "ffn2_w_in",
           "ffn2_w_out", "norm_final"]


def kernel(x, mem, norm_ffn1, ffn1_w_in, ffn1_w_out, norm_mix, mix_w_in, sconv_w, sgu_norm_g, sgu_w, sgu_b, cconv_w, cconv_ln_g, cconv_ln_b, pool_w, pool_scale, mix_w_out, norm_xattn, norm_mem, xattn_wq, xattn_wkv, xattn_wo, norm_ffn2, ffn2_w_in, ffn2_w_out, norm_final, loss_target, m_norm_ffn1, m_ffn1_w_in, m_ffn1_w_out, m_norm_mix, m_mix_w_in, m_sconv_w, m_sgu_norm_g, m_sgu_w, m_sgu_b, m_cconv_w, m_cconv_ln_g, m_cconv_ln_b, m_pool_w, m_pool_scale, m_mix_w_out, m_norm_xattn, m_norm_mem, m_xattn_wq, m_xattn_wkv, m_xattn_wo, m_norm_ffn2, m_ffn2_w_in, m_ffn2_w_out, m_norm_final, v_norm_ffn1, v_ffn1_w_in, v_ffn1_w_out, v_norm_mix, v_mix_w_in, v_sconv_w, v_sgu_norm_g, v_sgu_w, v_sgu_b, v_cconv_w, v_cconv_ln_g, v_cconv_ln_b, v_pool_w, v_pool_scale, v_mix_w_out, v_norm_xattn, v_norm_mem, v_xattn_wq, v_xattn_wkv, v_xattn_wo, v_norm_ffn2, v_ffn2_w_in, v_ffn2_w_out, v_norm_final):
    args = dict(locals())
    wts = {n: args[n] for n in WEIGHTS}
    mom = {n: args["m_" + n] for n in WEIGHTS}
    var = {n: args["v_" + n] for n in WEIGHTS}
    for n in TRANSPOSED:
        wts[n], mom[n], var[n] = (jnp.swapaxes(a, 1, 2) for a in (wts[n], mom[n], var[n]))
    x0 = x[0]
    mem0 = mem[0]
    target = loss_target[0]
    t, d = x0.shape
    nl = norm_ffn1.shape[0]
    w = MIX_W
    me = _my_index()

    me_arr = jnp.reshape(me, (1,)).astype(jnp.int32)

    small_g = all_gather([sconv_w, cconv_w], name="gather_conv_taps")
    sconv_full = jnp.transpose(small_g[0], (1, 2, 0, 3)).reshape(nl, SCONV_K, w)
    cconv_full = jnp.transpose(small_g[1], (1, 2, 0, 3)).reshape(nl, CCONV_K, w)
    pending = {}
    token = small_g[1]
    masks = GATHER_MASKS
    keys = [(gname, l, members) for l in range(nl) for gname, members in GATHER_GROUPS]
    first = [[cast_into_slot(wts[n], keys[0][1], me_arr, name=f"cast_{n}{keys[0][1]}") for n in keys[0][2]]]
    started, token = gather_start_groups(first, token, name="gather_start_first", masks=masks)
    casts = [[cast_into_slot(wts[n], l, me_arr, name=f"cast_{n}{l}", after=token) for n in members]
             for gname, l, members in keys[1:]]
    rest, token = gather_start_groups(casts, token, name="gather_start_rest", masks=masks)
    for (gname, l, members), (send, recv, gs) in zip(keys, started + rest):
        pending[gname, l] = (members, gs, send, recv, masks)
    wg = [dict() for _ in range(nl)]

    def arrive(gname, l, after):
        members, gs, send, recv, masks = pending.pop((gname, l))
        gs = gather_wait(gs, send, recv, after, name=f"gather_wait_{gname}{l}", masks=masks)
        if masks is GATHER_MASKS:
            gs = sibling_forward(gs, name=f"gather_forward_{gname}{l}")
        wg[l].update(zip(members, gs))
    sconv_pad = jnp.pad(sconv_full, ((0, 0), (0, 8 - SCONV_K), (0, 0)))
    cconv_pad = jnp.pad(cconv_full, ((0, 0), (0, 32 - CCONV_K), (0, 0)))
    zeros_w = jnp.zeros((nl, w), F32)
    vecs = jnp.stack([sgu_norm_g, cconv_ln_g, cconv_ln_b, pool_scale] + [zeros_w] * 4, axis=1)
    wt = jnp.tril(sgu_w).astype(BF16)
    bexp = jnp.repeat(jnp.swapaxes(sgu_b, 1, 2), w // N_HEADS, axis=2)
    eye = jnp.eye(4, dtype=F32)
    pbd = jnp.einsum("lgcd,gh->lgchd", pool_w, eye).reshape(nl, w, w).astype(BF16)

    def mixer_args(l):
        return sconv_pad[l], cconv_pad[l], vecs[l], wt[l], bexp[l], pbd[l]

    saved = []
    xc = x0
    after = token
    for l in range(nl):
        s = {"x_ffn1": xc}
        arrive("ffn1", l, after)
        xc, s["gu_ffn1"] = ffn_fwd(xc, norm_ffn1[l], wg[l]["ffn1_w_in"], wg[l]["ffn1_w_out"],
                                   name=f"ffn1_fwd{l}", tm=FFN_FWD_TILE)
        s["x_mix"] = xc
        arrive("mid", l, xc)
        z = mm_rows(xc, wg[l]["mix_w_in"], "col", gain=norm_mix[l], name=f"mix_in{l}")
        y = mixer_fwd(z, *mixer_args(l), name=f"mixer_fwd{l}")
        s["z"], s["y"] = z, y
        xc = mm_rows(y, wg[l]["mix_w_out"], "row", residual=xc, name=f"mix_out{l}")
        s["x_att"] = xc
        kv = mm_rows(mem0, wg[l]["xattn_wkv"], "col", gain=norm_mem[l], name=f"kv{l}")
        s["kv"] = kv
        xc = xattn_fwd(xc, norm_xattn[l], kv, wg[l]["xattn_wq"], wg[l]["xattn_wo"], name=f"xattn_fwd{l}")
        s["x_ffn2"] = xc
        arrive("ffn2", l, xc)
        xc, s["gu_ffn2"] = ffn_fwd(xc, norm_ffn2[l], wg[l]["ffn2_w_in"], wg[l]["ffn2_w_out"],
                                   name=f"ffn2_fwd{l}", tm=FFN_FWD_TILE)
        after = xc
        saved.append(s)

    dx, g_norm_final, loss_local = loss_head(xc, target, norm_final, name="loss_head")
    loss = lax.psum(loss_local[0, 0], ("x", "y", "c"))

    tm = _row_tile(t, TN_TILE)
    small ={n: [None] * nl for n in SMALL_REPL + SMALL_SHARD if n != "norm_final"}
    scattered = {}
    tie = [token]

    def send_grads(gname, l, grads):
        members = list(grads)
        send, recv, gs, lands, tie[0] = scatter_start(
            [grads[n] for n in members], tie[0], name=f"scatter_start_{gname}{l}")
        scattered[gname, l] = (members, gs, lands, send, recv)

    def tied(v):
        return v + tie[0][0, 0]

    names = SMALL_REPL + SMALL_SHARD
    small_pending = []

    def start_small():
        small_full = {n: jnp.stack(v) for n, v in small.items()}
        small_full["norm_final"] = g_norm_final[0]
        shapes = [small_full[n].shape for n in names]
        packed = _pack([small_full[n] for n in names], _rows_for(shapes))
        slot = cast_into_slot(packed[None], 0, me_arr, name="small_into_slot", dtype=F32)
        send, recv, gs, tie[0] = gather_start([slot], tie[0], name="small_gather_start", masks=ALL_MASKS)
        small_pending.append((gs, send, recv, shapes))

    def ffn_backward(which, l, x_in, dy, gu, gain):
        w_in, w_out = wg[l][which + "_w_in"], wg[l][which + "_w_out"]
        dx_, h_, act, dgu, dgn, dyh = ffn_bwd_rows(x_in, dy, gu, tied(gain), w_in, w_out,
                                                   name=f"{which}_bwd{l}_rows")
        small["norm_" + which][l] = dgn[0]
        last = which == "ffn1" and l == 0
        if last:
            start_small()
        g_in = ffn_grad_w_in(h_, dgu, tie[0], name=f"{which}_bwd{l}_dwin")
        if last:
            send_grads(which + "_in", l, {which + "_w_in": g_in})
        g_out = ffn_grad_w_out(act, dyh, tie[0], name=f"{which}_bwd{l}_dwout")
        if last:
            send_grads(which + "_out", l, {which + "_w_out": g_out})
        else:
            send_grads(which, l, {which + "_w_in": g_in, which + "_w_out": g_out})
        return dx_

    for l in reversed(range(nl)):
        s = saved[l]
        wl = wg[l]
        dx = ffn_backward("ffn2", l, s["x_ffn2"], dx, s["gu_ffn2"], norm_ffn2[l])

        bg = {}
        dxn = dx
        dx, h, dq, o, dkv, dgn = xattn_bwd_rows(
            s["x_att"], dxn, tied(norm_xattn[l]), s["kv"], wl["xattn_wq"], wl["xattn_wo"],
            name=f"xattn_bwd{l}")
        small["norm_xattn"][l] = dgn[0]
        row_spec = pl.BlockSpec((tm, d), lambda s_, i: (i, 0))
        bg["xattn_wq"] = mm_tn(h, dq, nb=1, ka=d, nbk=d, tm=tm, m=t, a_spec=row_spec, b_spec=row_spec,
                               name=f"dwq{l}").reshape(N_DEV, d // N_DEV, d)
        bg["xattn_wo"] = mm_tn(o, dxn, nb=1, ka=d, nbk=d, tm=tm, m=t, a_spec=row_spec, b_spec=row_spec,
                               name=f"dwo{l}").reshape(N_DEV, d // N_DEV, d)
        _, mhat, dgn = mm_nt(dkv, wl["xattn_wkv"], "col", x=mem0, gain=norm_mem[l], name=f"dmem{l}")
        small["norm_mem"][l] = dgn[0]
        nm = mem0.shape[0]
        bg["xattn_wkv"] = mm_tn(mhat, dkv, nb=N_DEV, ka=d, nbk=2 * d // N_DEV, tm=nm, m=nm,
                                a_spec=pl.BlockSpec((nm, d), lambda s_, i: (0, 0)),
                                b_spec=pl.BlockSpec((nm, 2 * d // N_DEV), lambda s_, i: (0, s_)),
                                name=f"dwkv{l}")
        send_grads("xattn", l, bg)

        bg = {}
        dxn = dx
        bg["mix_w_out"] = mm_tn(s["y"], dxn, nb=1, ka=d, nbk=d, tm=tm, m=t, a_spec=row_spec,
                                b_spec=row_spec, name=f"dwmo{l}").reshape(N_DEV, d // N_DEV, d)
        dy = mm_nt(dxn, wl["mix_w_out"], "row", name=f"dy_mix{l}")
        dz, gvec, gcc, gwt, gb, gpbd = mixer_bwd(s["z"], dy, *mixer_args(l), name=f"mixer_bwd{l}")
        small["sconv_w"][l] = gvec[0:SCONV_K]
        small["sgu_norm_g"][l] = gvec[3]
        small["cconv_ln_g"][l] = gvec[4]
        small["cconv_ln_b"][l] = gvec[5]
        small["pool_scale"][l] = gvec[6]
        small["cconv_w"][l] = gcc[0:CCONV_K]
        small["sgu_w"][l] = gwt
        small["sgu_b"][l] = jnp.transpose(gb[:, 0:N_HEADS])
        gw = w // 4
        small["pool_w"][l] = jnp.stack([gpbd[g * gw:(g + 1) * gw, g * gw:(g + 1) * gw] for g in range(4)])
        dx, h, dgn = mm_nt(dz, wl["mix_w_in"], "col", x=s["x_mix"], gain=tied(norm_mix[l]), dx_in=dxn,
                           name=f"dh_mix{l}")
        small["norm_mix"][l] = dgn[0]
        th = _row_tile(t, TN_TILE // 2)
        bg["mix_w_in"] = mm_tn(h, dz, nb=1, ka=d, nbk=N_DEV * w, tm=th, m=t, col_slots=N_DEV,
                               a_spec=pl.BlockSpec((th, d), lambda s_, i: (i, 0)),
                               b_spec=pl.BlockSpec((th, N_DEV * w), lambda s_, i: (i, 0)), name=f"dwmi{l}")
        send_grads("mix", l, bg)

        dx = ffn_backward("ffn1", l, s["x_ffn1"], dx, s["gu_ffn1"], norm_ffn1[l])

    out = {}

    def finish(keys, after):
        own, land = {}, {}
        for gname, l in keys:
            members, gs, lands, send, recv = scattered.pop((gname, l))
            gs, lands = scatter_wait(gs, lands, send, recv, after, name=f"scatter_wait_{gname}{l}")
            for n, g_, l_ in zip(members, gs, lands):
                own.setdefault(n, {})[l] = g_
                land.setdefault(n, {})[l] = l_
        for n in own:
            out[n] = adamw_sharded([own[n][l] for l in range(nl)], [land[n][l] for l in range(nl)],
                                   wts[n], mom[n], var[n], me_arr, name="adamw_" + n)
            after = out[n][1]
        return after

    after = tie[0]
    for gname in ("ffn2", "xattn", "mix"):
        after = finish([(gname, l) for l in reversed(range(nl))], after)
    (gs, send, recv, shapes), = small_pending
    gs = gather_wait(gs, send, recv, after, name="small_gather_wait", masks=ALL_MASKS)
    summed = sum_slots(gs[0], name="small_sum")
    gsm = dict(zip(names, _unpack(summed, shapes)))
    finish([("ffn1", l) for l in reversed(range(1, nl))] + [("ffn1_in", 0), ("ffn1_out", 0)], summed)
    repl_shapes = [wts[n].shape for n in SMALL_REPL]
    rows_r = _rows_for(repl_shapes)
    dl, mn, vn = adamw_flat(_pack([gsm[n] for n in SMALL_REPL], rows_r),
                            _pack([wts[n] for n in SMALL_REPL], rows_r),
                            _pack([mom[n] for n in SMALL_REPL], rows_r),
                            _pack([var[n] for n in SMALL_REPL], rows_r), name="adamw_small")
    for n, a, b, c in zip(SMALL_REPL, _unpack(dl, repl_shapes), _unpack(mn, repl_shapes),
                          _unpack(vn, repl_shapes)):
        out[n] = (gsm[n], a, b, c)
    cs = w // N_DEV
    gsh = {n: lax.dynamic_slice_in_dim(gsm[n], me * cs, cs, axis=2) for n in SMALL_SHARD}
    sh_shapes = [wts[n].shape for n in SMALL_SHARD]
    rows_s = _rows_for(sh_shapes)
    dl, mn, vn = adamw_flat(_pack([gsh[n] for n in SMALL_SHARD], rows_s),
                            _pack([wts[n] for n in SMALL_SHARD], rows_s),
                            _pack([mom[n] for n in SMALL_SHARD], rows_s),
                            _pack([var[n] for n in SMALL_SHARD], rows_s), name="adamw_small_sharded")
    for n, a, b, c in zip(SMALL_SHARD, _unpack(dl, sh_shapes), _unpack(mn, sh_shapes),
                          _unpack(vn, sh_shapes)):
        out[n] = (gsh[n], a, b, c)
    for n in TRANSPOSED:
        out[n] = tuple(jnp.swapaxes(a, 1, 2) for a in out[n])

    grad_x = dx.reshape(1, t, d)
    return (loss, grad_x, *[out[n][0] for n in WEIGHTS], *[out[n][1] for n in WEIGHTS],
            *[out[n][2] for n in WEIGHTS], *[out[n][3] for n in WEIGHTS])
```

```python
import functools

import jax
import jax.numpy as jnp
from jax import lax
from jax.experimental import pallas as pl
from jax.experimental.pallas import tpu as pltpu

F32 = jnp.float32
BF16 = jnp.bfloat16
MESH = pl.DeviceIdType.MESH
N_DEV = 8
EPS = 1e-6
HALO = 32
SGU_CHUNK = 128
CCONV_K = 31
SCONV_K = 3
MIX_W = 256
N_HEADS = 4
VMEM_LIMIT = 56 * 1024 * 1024
ROW_TILE = 512
TN_TILE = 2048
FFN_FWD_TILE = 1024
FFN_BWD_SPLIT = 2
FFN_FWD_SPLIT = 2
MIX_TILE = 512

ADAM_LR = 0.001
ADAM_B1 = 0.9
ADAM_B2 = 0.999
ADAM_EPS = 1e-08
ADAM_WD = 0.01
ADAM_STEP = 10

HBM_SPEC = pl.BlockSpec(memory_space=pltpu.HBM)
VMEM_SPEC = pl.BlockSpec(memory_space=pltpu.VMEM)


def _params(*sem):
    return pltpu.CompilerParams(dimension_semantics=tuple(sem), vmem_limit_bytes=VMEM_LIMIT)


def _row_tile(m, pref=None):
    t = min(m, ROW_TILE if pref is None else pref)
    assert m % t == 0, (m, t)
    return t


def _my_index():
    return lax.axis_index("x") * 4 + lax.axis_index("y") * 2 + lax.axis_index("c")


def _peer(mask):
    x, y, c = lax.axis_index("x"), lax.axis_index("y"), lax.axis_index("c")
    px = 1 - x if mask & 4 else x
    py = 1 - y if mask & 2 else y
    pc = 1 - c if mask & 1 else c
    return (px, py, pc), px * 4 + py * 2 + pc


def all_gather(arrs, name):
    n = len(arrs)

    def body(*refs):
        ins, outs = refs[:n], refs[n:2 * n]
        send_sems, recv_sems, loc_sems = refs[2 * n:]
        me = _my_index()
        local = []
        for i in range(n):
            cp = pltpu.make_async_copy(ins[i], outs[i].at[me], loc_sems.at[i])
            cp.start()
            local.append(cp)
        sends = []
        for i in range(n):
            for m in range(1, N_DEV):
                peer, _ = _peer(m)
                cp = pltpu.make_async_remote_copy(
                    src_ref=ins[i], dst_ref=outs[i].at[me],
                    send_sem=send_sems.at[i, m - 1], recv_sem=recv_sems.at[i, m - 1],
                    device_id=peer, device_id_type=MESH)
                cp.start()
                sends.append(cp)
        for i in range(n):
            for m in range(1, N_DEV):
                peer, pidx = _peer(m)
                pltpu.make_async_remote_copy(
                    src_ref=ins[i], dst_ref=outs[i].at[pidx],
                    send_sem=send_sems.at[i, m - 1], recv_sem=recv_sems.at[i, m - 1],
                    device_id=peer, device_id_type=MESH).wait_recv()
        for cp in sends:
            cp.wait_send()
        for cp in local:
            cp.wait()

    return pl.pallas_call(
        body, name=name,
        out_shape=[jax.ShapeDtypeStruct((N_DEV,) + a.shape, a.dtype) for a in arrs],
        in_specs=[HBM_SPEC] * n, out_specs=[HBM_SPEC] * n,
        scratch_shapes=[pltpu.SemaphoreType.DMA((n, N_DEV - 1)),
                        pltpu.SemaphoreType.DMA((n, N_DEV - 1)),
                        pltpu.SemaphoreType.DMA((n,))],
    )(*arrs)


SEM_SPEC = pl.BlockSpec(memory_space=pltpu.SEMAPHORE)
ANY_SPEC = pl.BlockSpec(memory_space=pl.ANY)
SIDE_EFFECT = pltpu.SideEffectType.DATAFLOW_SIDE_EFFECTING


def _hbm(a):
    return pltpu.with_memory_space_constraint(a, pltpu.HBM)


def _sem_pairs(n):
    return (pltpu.SemaphoreType.DMA((n * (N_DEV - 1),)), pltpu.SemaphoreType.DMA((n * (N_DEV - 1),)))


def _sem(i, m):
    return i * (N_DEV - 1) + m - 1


def _gather_copy(g_ref, i, m, send_sems, recv_sems, origin):
    peer, _ = _peer(m)
    return pltpu.make_async_remote_copy(
        src_ref=g_ref.at[origin], dst_ref=g_ref.at[origin],
        send_sem=send_sems.at[_sem(i, m)], recv_sem=recv_sems.at[_sem(i, m)],
        device_id=peer, device_id_type=MESH)


GATHER_MASKS = (1, 2, 4, 6)
FORWARD_MASKS = (2, 4, 6)


ALL_MASKS = tuple(range(1, N_DEV))


def gather_start(gs, after, name, masks=GATHER_MASKS):
    n = len(gs)

    def body(*refs):
        g_in = refs[:n]
        send_sems, recv_sems = refs[n + 1], refs[n + 2]
        token = refs[-1]
        me = _my_index()
        for i in range(n):
            for m in masks:
                _gather_copy(g_in[i], i, m, send_sems, recv_sems, me).start()
        token[...] = jnp.zeros_like(token)

    outs = pl.pallas_call(
        body, name=name,
        out_shape=(*_sem_pairs(n), *[pltpu.HBM(g.shape, g.dtype) for g in gs],
                   jax.ShapeDtypeStruct((8, 128), F32)),
        in_specs=[HBM_SPEC] * n + [ANY_SPEC],
        out_specs=(SEM_SPEC, SEM_SPEC, *[HBM_SPEC] * n, VMEM_SPEC),
        input_output_aliases={i: 2 + i for i in range(n)},
        compiler_params=pltpu.CompilerParams(has_side_effects=SIDE_EFFECT),
    )(*[_hbm(g) for g in gs], after)
    return outs[0], outs[1], list(outs[2:2 + n]), outs[-1]


def gather_start_groups(groups, after, name, masks=GATHER_MASKS):
    sizes = [len(g) for g in groups]
    flat = [a for g in groups for a in g]
    n, ng = len(flat), len(groups)

    def body(*refs):
        g_in = refs[:n]
        sems = refs[n + 1:n + 1 + 2 * ng]
        token = refs[-1]
        me = _my_index()
        pos = 0
        for k, size in enumerate(sizes):
            for i in range(size):
                for m in masks:
                    _gather_copy(g_in[pos + i], i, m, sems[2 * k], sems[2 * k + 1], me).start()
            pos += size
        token[...] = jnp.zeros_like(token)

    outs = pl.pallas_call(
        body, name=name,
        out_shape=(*[s for size in sizes for s in _sem_pairs(size)],
                   *[pltpu.HBM(g.shape, g.dtype) for g in flat], jax.ShapeDtypeStruct((8, 128), F32)),
        in_specs=[HBM_SPEC] * n + [ANY_SPEC],
        out_specs=(*[SEM_SPEC] * (2 * ng), *[HBM_SPEC] * n, VMEM_SPEC),
        input_output_aliases={i: 2 * ng + i for i in range(n)},
        compiler_params=pltpu.CompilerParams(has_side_effects=SIDE_EFFECT),
    )(*[_hbm(g) for g in flat], after)
    result, pos = [], 2 * ng
    for k, size in enumerate(sizes):
        result.append((outs[2 * k], outs[2 * k + 1], list(outs[pos:pos + size])))
        pos += size
    return result, outs[-1]


def gather_wait(gs, send_sems, recv_sems, after, name, masks=GATHER_MASKS):
    n = len(gs)

    def body(*refs):
        g_in = refs[:n]
        send, recv = refs[n], refs[n + 1]
        me = _my_index()
        for i in range(n):
            for m in masks:
                _, pidx = _peer(m)
                _gather_copy(g_in[i], i, m, send, recv, me).wait_send()
                _gather_copy(g_in[i], i, m, send, recv, pidx).wait_recv()

    outs = pl.pallas_call(
        body, name=name,
        out_shape=[pltpu.HBM(g.shape, g.dtype) for g in gs],
        in_specs=[HBM_SPEC] * n + [SEM_SPEC, SEM_SPEC, ANY_SPEC],
        out_specs=[HBM_SPEC] * n,
        input_output_aliases={i: i for i in range(n)},
        compiler_params=pltpu.CompilerParams(has_side_effects=SIDE_EFFECT),
    )(*gs, send_sems, recv_sems, after)
    return list(outs)


def sibling_forward(gs, name):
    n = len(gs)
    nf = len(FORWARD_MASKS)

    def body(*refs):
        g_in = refs[:n]
        send_sems, recv_sems = refs[2 * n:]
        x, y, c = lax.axis_index("x"), lax.axis_index("y"), lax.axis_index("c")
        sibling = (x, y, 1 - c)

        def copy(i, k, origin):
            return pltpu.make_async_remote_copy(
                src_ref=g_in[i].at[origin], dst_ref=g_in[i].at[origin],
                send_sem=send_sems.at[i * nf + k], recv_sem=recv_sems.at[i * nf + k],
                device_id=sibling, device_id_type=MESH)
        sends = []
        for i in range(n):
            for k, m in enumerate(FORWARD_MASKS):
                _, origin = _peer(m)
                cp = copy(i, k, origin)
                cp.start()
                sends.append(cp)
        for i in range(n):
            for k, m in enumerate(FORWARD_MASKS):
                _, origin = _peer(m ^ 1)
                copy(i, k, origin).wait_recv()
        for cp in sends:
            cp.wait_send()

    outs = pl.pallas_call(
        body, name=name,
        out_shape=[jax.ShapeDtypeStruct(g.shape, g.dtype) for g in gs],
        in_specs=[HBM_SPEC] * n, out_specs=[HBM_SPEC] * n,
        input_output_aliases={i: i for i in range(n)},
        scratch_shapes=[pltpu.SemaphoreType.DMA((n * nf,)), pltpu.SemaphoreType.DMA((n * nf,))],
    )(*gs)
    return list(outs)


def _scatter_copy(g_ref, l_ref, i, m, send_sems, recv_sems):
    peer, pidx = _peer(m)
    return pltpu.make_async_remote_copy(
        src_ref=g_ref.at[pidx], dst_ref=l_ref.at[m - 1],
        send_sem=send_sems.at[_sem(i, m)], recv_sem=recv_sems.at[_sem(i, m)],
        device_id=peer, device_id_type=MESH)


def scatter_start(grads, after, name):
    n = len(grads)
    lands = [lax.empty((N_DEV - 1,) + g.shape[1:], g.dtype) for g in grads]

    def body(*refs):
        g_in, l_in = refs[:n], refs[n:2 * n]
        send_sems, recv_sems = refs[2 * n + 1], refs[2 * n + 2]
        token = refs[-1]
        for i in range(n):
            for m in range(1, N_DEV):
                _scatter_copy(g_in[i], l_in[i], i, m, send_sems, recv_sems).start()
        token[...] = jnp.zeros_like(token)

    outs = pl.pallas_call(
        body, name=name,
        out_shape=(*_sem_pairs(n), *[pltpu.HBM(g.shape, g.dtype) for g in grads],
                   *[pltpu.HBM(l.shape, l.dtype) for l in lands], jax.ShapeDtypeStruct((8, 128), F32)),
        in_specs=[HBM_SPEC] * (2 * n) + [ANY_SPEC],
        out_specs=(SEM_SPEC, SEM_SPEC, *[HBM_SPEC] * (2 * n), VMEM_SPEC),
        input_output_aliases={i: 2 + i for i in range(2 * n)},
        compiler_params=pltpu.CompilerParams(has_side_effects=SIDE_EFFECT),
    )(*[_hbm(g) for g in grads], *[_hbm(l) for l in lands], after)
    return outs[0], outs[1], list(outs[2:2 + n]), list(outs[2 + n:2 + 2 * n]), outs[-1]


def scatter_wait(grads, lands, send_sems, recv_sems, after, name):
    n = len(grads)

    def body(*refs):
        g_in, l_in = refs[:n], refs[n:2 * n]
        send, recv = refs[2 * n], refs[2 * n + 1]
        for i in range(n):
            for m in range(1, N_DEV):
                cp = _scatter_copy(g_in[i], l_in[i], i, m, send, recv)
                cp.wait_send()
                cp.wait_recv()

    outs = pl.pallas_call(
        body, name=name,
        out_shape=[pltpu.HBM(a.shape, a.dtype) for a in list(grads) + list(lands)],
        in_specs=[HBM_SPEC] * (2 * n) + [SEM_SPEC, SEM_SPEC, ANY_SPEC],
        out_specs=[HBM_SPEC] * (2 * n),
        input_output_aliases={i: i for i in range(2 * n)},
        compiler_params=pltpu.CompilerParams(has_side_effects=SIDE_EFFECT),
    )(*grads, *lands, send_sems, recv_sems, after)
    return list(outs[:n]), list(outs[n:])


def sum_slots(g, name):
    _, r, c = g.shape

    def body(g_ref, out_ref):
        acc = g_ref[0]
        for p in range(1, N_DEV):
            acc = acc + g_ref[p]
        out_ref[...] = acc

    return pl.pallas_call(
        body, name=name, out_shape=jax.ShapeDtypeStruct((r, c), F32),
        in_specs=[VMEM_SPEC], out_specs=VMEM_SPEC,
        compiler_params=pltpu.CompilerParams(vmem_limit_bytes=VMEM_LIMIT),
    )(g)


def _sigmoid(v):
    return 1.0 / (1.0 + jnp.exp(-v))


def _rms_fwd(xf, g):
    r = lax.rsqrt(jnp.mean(xf * xf, axis=-1, keepdims=True) + EPS)
    return xf * r, r


def _rms_bwd(xhat, r, g, dy):
    dg = jnp.sum(dy * xhat, axis=0, keepdims=True)
    dxh = dy * g
    dx = r * (dxh - xhat * jnp.mean(dxh * xhat, axis=-1, keepdims=True))
    return dx, dg


def _ln_stats(v):
    mu = jnp.mean(v, axis=-1, keepdims=True)
    vc = v - mu
    r = lax.rsqrt(jnp.mean(vc * vc, axis=-1, keepdims=True) + EPS)
    return vc * r, r


def _ln_bwd(xhat, r, dxh):
    return r * (dxh - jnp.mean(dxh, axis=-1, keepdims=True)
                - xhat * jnp.mean(dxh * xhat, axis=-1, keepdims=True))


def _dot(a, b):
    return jnp.dot(a, b, preferred_element_type=F32)


def _dot_nt(a, b):
    return lax.dot_general(a, b, (((1,), (1,)), ((), ())), preferred_element_type=F32)


def _dot_tn(a, b):
    return lax.dot_general(a, b, (((0,), (0,)), ((), ())), preferred_element_type=F32)


def _full_weight(w_ref, kind):
    assert kind == "row"
    p, a, b = w_ref.shape
    return w_ref[...].reshape(p * a, b)


def _wspec(wg):
    return pl.BlockSpec(wg.shape, lambda *_: (0, 0, 0))


def mm_rows(a, wg, kind, *, gain=None, residual=None, out_dtype=F32, name, tm=None):
    m, k = a.shape
    p, wa, wb = wg.shape
    n = p * wb if kind == "col" else wb
    tm = _row_tile(m, tm)
    has_gain, has_res = gain is not None, residual is not None

    def body(*refs):
        refs = list(refs)
        a_ref = refs.pop(0)
        g_ref = refs.pop(0) if has_gain else None
        w_ref = refs.pop(0)
        r_ref = refs.pop(0) if has_res else None
        o_ref = refs.pop(0)
        if has_gain:
            xhat, _ = _rms_fwd(a_ref[...].astype(F32), None)
            h = (xhat * g_ref[...]).astype(BF16)
        else:
            h = a_ref[...].astype(BF16)
        if kind == "col":
            for j in range(p):
                o = _dot(h, w_ref[j])
                if has_res:
                    o = o + r_ref[:, j * wb:(j + 1) * wb]
                o_ref[:, j * wb:(j + 1) * wb] = o.astype(out_dtype)
        else:
            o = _dot(h, _full_weight(w_ref, "row"))
            if has_res:
                o = o + r_ref[...]
            o_ref[...] = o.astype(out_dtype)

    operands = [a]
    in_specs = [pl.BlockSpec((tm, k), lambda i: (i, 0))]
    if has_gain:
        operands.append(gain.reshape(1, k))
        in_specs.append(pl.BlockSpec((1, k), lambda i: (0, 0)))
    operands.append(wg)
    in_specs.append(_wspec(wg))
    if has_res:
        operands.append(residual)
        in_specs.append(pl.BlockSpec((tm, n), lambda i: (i, 0)))
    return pl.pallas_call(
        body, name=name, grid=(m // tm,),
        out_shape=jax.ShapeDtypeStruct((m, n), out_dtype),
        in_specs=in_specs, out_specs=pl.BlockSpec((tm, n), lambda i: (i, 0)),
        compiler_params=_params("parallel"),
    )(*operands)


def mm_nt(dz, wg, kind, *, x=None, gain=None, dx_in=None, name, tm=None):
    m, n = dz.shape
    p, wa, wb = wg.shape
    k = wa if kind == "col" else p * wa
    tm = _row_tile(m, tm)
    epi = x is not None
    has_dx = dx_in is not None

    def body(*refs):
        refs = list(refs)
        dz_ref, w_ref = refs.pop(0), refs.pop(0)
        if epi:
            x_ref, g_ref = refs.pop(0), refs.pop(0)
            dxi_ref = refs.pop(0) if has_dx else None
            dx_ref, h_ref, dg_ref = refs
        else:
            (da_ref,) = refs
        dzb = dz_ref[...].astype(BF16)
        if kind == "col":
            da = _dot_nt(dzb[:, 0:wb], w_ref[0])
            for j in range(1, p):
                da = da + _dot_nt(dzb[:, j * wb:(j + 1) * wb], w_ref[j])
        else:
            da = _dot_nt(dzb, _full_weight(w_ref, "row"))
        if not epi:
            da_ref[...] = da
            return
        g = g_ref[...]
        xhat, r = _rms_fwd(x_ref[...].astype(F32), None)
        h_ref[...] = (xhat * g).astype(BF16)
        dx, dg = _rms_bwd(xhat, r, g, da)
        if has_dx:
            dx = dx + dxi_ref[...]
        dx_ref[...] = dx

        @pl.when(pl.program_id(0) == 0)
        def _():
            dg_ref[...] = jnp.zeros_like(dg_ref)
        dg_ref[...] += dg

    row = lambda i: (i, 0)
    operands = [dz, wg]
    in_specs = [pl.BlockSpec((tm, n), row), _wspec(wg)]
    if epi:
        operands += [x, gain.reshape(1, k)]
        in_specs += [pl.BlockSpec((tm, k), row), pl.BlockSpec((1, k), lambda i: (0, 0))]
        if has_dx:
            operands.append(dx_in)
            in_specs.append(pl.BlockSpec((tm, k), row))
        out_shape = [jax.ShapeDtypeStruct((m, k), F32), jax.ShapeDtypeStruct((m, k), BF16),
                     jax.ShapeDtypeStruct((1, k), F32)]
        out_specs = [pl.BlockSpec((tm, k), row), pl.BlockSpec((tm, k), row),
                     pl.BlockSpec((1, k), lambda i: (0, 0))]
    else:
        out_shape = jax.ShapeDtypeStruct((m, k), F32)
        out_specs = pl.BlockSpec((tm, k), row)
    return pl.pallas_call(
        body, name=name, grid=(m // tm,), out_shape=out_shape,
        in_specs=in_specs, out_specs=out_specs,
        compiler_params=_params("arbitrary"),
    )(*operands)


def mm_tn(a, b, *, nb, a_spec, b_spec, ka, nbk, tm, m, scale=1.0, out_dtype=BF16, col_slots=1,
          after=None, name):
    ni = m // tm
    assert col_slots == 1 or nb == 1
    cw = nbk // col_slots
    extra = [] if after is None else [after]

    def body(a_ref, b_ref, *rest):
        o_ref, acc = rest[len(extra):]
        i = pl.program_id(1)

        @pl.when(i == 0)
        def _():
            acc[...] = jnp.zeros_like(acc)
        acc[...] += _dot_tn(a_ref[...].astype(BF16), b_ref[...].astype(BF16))

        @pl.when(i == ni - 1)
        def _():
            if col_slots == 1:
                o_ref[...] = (acc[...] * scale).astype(out_dtype)
            else:
                for j in range(col_slots):
                    o_ref[j] = (acc[:, j * cw:(j + 1) * cw] * scale).astype(out_dtype)

    if col_slots == 1:
        out_shape = jax.ShapeDtypeStruct((nb, ka, nbk), out_dtype)
        out_spec = pl.BlockSpec((None, ka, nbk), lambda s, i: (s, 0, 0))
    else:
        out_shape = jax.ShapeDtypeStruct((col_slots, ka, cw), out_dtype)
        out_spec = pl.BlockSpec((col_slots, ka, cw), lambda s, i: (0, 0, 0))
    return pl.pallas_call(
        body, name=name, grid=(nb, ni), out_shape=out_shape,
        in_specs=[a_spec, b_spec] + [ANY_SPEC] * len(extra), out_specs=out_spec,
        scratch_shapes=[pltpu.VMEM((ka, nbk), F32)],
        compiler_params=_params("parallel", "arbitrary"),
    )(a, b, *extra)


def _ffn_specs(w_in_g, w_out_g, d):
    nf = w_in_g.shape[1]
    hr = w_out_g.shape[1]
    assert 2 * hr == nf
    w_in5 = w_in_g.reshape(2, 4, nf, d)
    w_out5 = w_out_g.reshape(4, 2, hr, d)
    in_spec = pl.BlockSpec((2, None, nf, d), lambda i, j: (0, j, 0, 0))
    out_spec = pl.BlockSpec((None, 2, hr, d), lambda i, j: (j, 0, 0, 0))
    return w_in5, w_out5, in_spec, out_spec, nf


def ffn_fwd(x, gain, w_in_g, w_out_g, *, name, tm=None):
    t, d = x.shape
    tm = _row_tile(t, tm)
    w_in5, w_out5, wi_spec, wo_spec, nf = _ffn_specs(w_in_g, w_out_g, d)

    def body(x_ref, g_ref, wi_ref, wo_ref, o_ref, gu_ref, h_scr, acc):
        j = pl.program_id(1)

        @pl.when(j == 0)
        def _():
            xhat, _ = _rms_fwd(x_ref[...], None)
            h_scr[...] = (xhat * g_ref[...]).astype(BF16)
            acc[...] = jnp.zeros_like(acc)
        wo = wo_ref[...].reshape(nf, d)

        def project(rows):
            h = h_scr[rows]
            return _dot_nt(h, wi_ref[0]), _dot_nt(h, wi_ref[1])

        sub = tm // FFN_FWD_SPLIT
        parts = [slice(k * sub, (k + 1) * sub) for k in range(FFN_FWD_SPLIT)]
        gt, up = project(parts[0])
        for k, rows in enumerate(parts):
            if k + 1 < len(parts):
                nxt = project(parts[k + 1])
            gu_ref[0, rows] = gt.astype(BF16)
            gu_ref[1, rows] = up.astype(BF16)
            act = (gt * _sigmoid(gt) * up).astype(BF16)
            acc[rows] += _dot(act, wo)
            if k + 1 < len(parts):
                gt, up = nxt

        @pl.when(j == 3)
        def _():
            o_ref[...] = x_ref[...] + 0.5 * acc[...]

    return pl.pallas_call(
        body, name=name, grid=(t // tm, 4),
        out_shape=[jax.ShapeDtypeStruct((t, d), F32), jax.ShapeDtypeStruct((2, 4, t, nf), BF16)],
        in_specs=[pl.BlockSpec((tm, d), lambda i, j: (i, 0)),
                  pl.BlockSpec((1, d), lambda i, j: (0, 0)), wi_spec, wo_spec],
        out_specs=[pl.BlockSpec((tm, d), lambda i, j: (i, 0)),
                   pl.BlockSpec((2, None, tm, nf), lambda i, j: (0, j, i, 0))],
        scratch_shapes=[pltpu.VMEM((tm, d), BF16), pltpu.VMEM((tm, d), F32)],
        compiler_params=_params("parallel", "arbitrary"),
    )(x, gain.reshape(1, d), w_in5, w_out5)


def ffn_bwd_rows(x, dy, gu, gain, w_in_g, w_out_g, *, name, tm=None):
    t, d = x.shape
    tm = _row_tile(t, tm)
    w_in5, w_out5, wi_spec, wo_spec, nf = _ffn_specs(w_in_g, w_out_g, d)

    def body(x_ref, dy_ref, gu_ref, g_ref, wi_ref, wo_ref, dx_ref, h_ref, act_ref, dgu_ref, dg_ref,
             dyh_scr, dh_acc):
        i, j = pl.program_id(0), pl.program_id(1)

        @pl.when(j == 0)
        def _():
            xhat, _ = _rms_fwd(x_ref[...], None)
            h_ref[...] = (xhat * g_ref[...]).astype(BF16)
            dyh_scr[...] = (0.5 * dy_ref[...]).astype(BF16)
            dh_acc[...] = jnp.zeros_like(dh_acc)
        wo = wo_ref[...].reshape(nf, d)

        def gates(rows):
            gt = gu_ref[0, rows].astype(F32)
            up = gu_ref[1, rows].astype(F32)
            sg = _sigmoid(gt)
            silu = gt * sg
            act_ref[rows] = (silu * up).astype(BF16)
            return up * (sg * (1.0 + gt * (1.0 - sg))), silu

        def grads(rows, dact, dsilu_up, silu):
            dgt = (dact * dsilu_up).astype(BF16)
            dup = (dact * silu).astype(BF16)
            dgu_ref[0, rows] = dgt
            dgu_ref[1, rows] = dup
            return dgt, dup

        sub = tm // FFN_BWD_SPLIT
        parts = [slice(k * sub, (k + 1) * sub) for k in range(FFN_BWD_SPLIT)]
        dact = _dot_nt(dyh_scr[parts[0]], wo)
        gate = gates(parts[0])
        for k, rows in enumerate(parts):
            if k + 1 < len(parts):
                dact_next = _dot_nt(dyh_scr[parts[k + 1]], wo)
            dgt, dup = grads(rows, dact, *gate)
            dh_acc[rows] += _dot(dgt, wi_ref[0]) + _dot(dup, wi_ref[1])
            if k + 1 < len(parts):
                gate = gates(parts[k + 1])
                dact = dact_next

        @pl.when(j == 3)
        def _():
            g = g_ref[...]
            xhat, r = _rms_fwd(x_ref[...], None)
            dx, dg = _rms_bwd(xhat, r, g, dh_acc[...])
            dx_ref[...] = dy_ref[...] + dx

            @pl.when(i == 0)
            def _():
                dg_ref[...] = jnp.zeros_like(dg_ref)
            dg_ref[...] += dg

    row = lambda i, j: (i, 0)
    return pl.pallas_call(
        body, name=name, grid=(t // tm, 4),
        out_shape=[jax.ShapeDtypeStruct((t, d), F32), jax.ShapeDtypeStruct((t, d), BF16),
                   jax.ShapeDtypeStruct((4, t, nf), BF16), jax.ShapeDtypeStruct((2, 4, t, nf), BF16),
                   jax.ShapeDtypeStruct((1, d), F32), jax.ShapeDtypeStruct((t, d), BF16)],
        in_specs=[pl.BlockSpec((tm, d), row), pl.BlockSpec((tm, d), row),
                  pl.BlockSpec((2, None, tm, nf), lambda i, j: (0, j, i, 0)),
                  pl.BlockSpec((1, d), lambda i, j: (0, 0)), wi_spec, wo_spec],
        out_specs=[pl.BlockSpec((tm, d), row), pl.BlockSpec((tm, d), row),
                   pl.BlockSpec((None, tm, nf), lambda i, j: (j, i, 0)),
                   pl.BlockSpec((2, None, tm, nf), lambda i, j: (0, j, i, 0)),
                   pl.BlockSpec((1, d), lambda i, j: (0, 0)), pl.BlockSpec((tm, d), row)],
        scratch_shapes=[pltpu.VMEM((tm, d), F32)],
        compiler_params=_params("arbitrary", "arbitrary"),
    )(x, dy, gu, gain.reshape(1, d), w_in5, w_out5)


def ffn_grad_w_in(h, dgu, after, *, name):
    t, d = h.shape
    nf = dgu.shape[-1]
    tm = _row_tile(t, TN_TILE)
    return mm_tn(dgu.reshape(8, t, nf), h, nb=8, ka=nf, nbk=d, tm=tm, m=t, after=after,
                 a_spec=pl.BlockSpec((None, tm, nf), lambda s, i: (s, i, 0)),
                 b_spec=pl.BlockSpec((tm, d), lambda s, i: (i, 0)), name=name)


def ffn_grad_w_out(act, dyh, after, *, name):
    _, t, nf = act.shape
    d = dyh.shape[1]
    tm = _row_tile(t, TN_TILE)
    d_w_out = mm_tn(act, dyh, nb=4, ka=nf, nbk=d, tm=tm, m=t, after=after,
                    a_spec=pl.BlockSpec((None, tm, nf), lambda s, i: (s, i, 0)),
                    b_spec=pl.BlockSpec((tm, d), lambda s, i: (i, 0)), name=name)
    return d_w_out.reshape(8, nf // 2, d)


def _lane_group(shape):
    return lax.shift_right_logical(lax.broadcasted_iota(jnp.int32, shape, 1), 6)


def _pool_count(t0, rows):
    t = (t0 + lax.broadcasted_iota(jnp.int32, (rows, MIX_W), 0) + 1).astype(F32)
    return jnp.minimum(t, _by_group(_lane_group((rows, MIX_W)), 2.0, 4.0, 8.0, 16.0))


def _by_group(grp, v0, v1, v2, v3):
    return jnp.where(grp == 0, v0, jnp.where(grp == 1, v1, jnp.where(grp == 2, v2, v3)))


def _sgu_mix(wt_ref, vnc):
    grp = _lane_group((SGU_CHUNK, MIX_W))
    out = jnp.zeros((SGU_CHUNK, MIX_W), F32)
    for hd in range(N_HEADS):
        out = jnp.where(grp == hd, _dot(wt_ref[hd], vnc), out)
    return out


def _pool_fwd(s1, s2, s3, t0, ts, lo):
    h = lo
    s2[h - 24:h + ts] = s1[h - 24:h + ts] + s1[h - 25:h + ts - 1]
    s3[h - 16:h + ts] = s2[h - 16:h + ts] + s2[h - 18:h + ts - 2]
    sum2 = s2[h:h + ts]
    sum4 = s3[h:h + ts]
    s2[h - 8:h + ts] = s3[h - 8:h + ts] + s3[h - 12:h + ts - 4]
    sum8 = s2[h:h + ts]
    sum16 = sum8 + s2[h - 8:h + ts - 8]
    grp = _lane_group((ts, MIX_W))
    return _by_group(grp, sum2, sum4, sum8, sum16) / _pool_count(t0, ts) - s1[h:h + ts]


def _make_shifts(src, sh, rows):
    for b in range(1, 8):
        sh[b, 0:rows] = src[b:b + rows]


def _rows_at(src, sh, start, n):
    a, b = divmod(start, 8)
    return src[8 * a:8 * a + n] if b == 0 else sh[b, 8 * a:8 * a + n]


def mixer_fwd(z, sconv, cconv, vecs, wt, bexp, pbd, x_res, wmo_g, *, name, ts=None):
    t = z.shape[0]
    ts = _row_tile(t, MIX_TILE if ts is None else ts)
    hl = HALO
    w = MIX_W
    nch = ts // SGU_CHUNK

    def body(zc, zp, sconv_ref, cconv_ref, vec_ref, wt_ref, bexp_ref, pbd_ref, xr_ref, wmo_ref,
             y_ref, xo_ref, s1, s2, s3, sh):
        i = pl.program_id(0)
        has_prev = i > 0

        def col(ref, c):
            return ref[:, c * w:(c + 1) * w]

        def prev(c):
            return jnp.where(has_prev, col(zp, c), 0.0)

        s1[0:hl] = prev(1) * prev(2)
        s1[hl:hl + ts] = col(zc, 1) * col(zc, 2)
        cv = sconv_ref[0:1] * s1[hl - 2:hl - 2 + ts]
        for k in range(1, SCONV_K):
            cv = cv + sconv_ref[k:k + 1] * s1[hl - 2 + k:hl - 2 + k + ts]
        y_ref[:, 0:w] = (col(zc, 0) * cv).astype(BF16)

        xhat, _ = _ln_stats(col(zc, 4))
        vn = (xhat * vec_ref[0:1]).astype(BF16)
        for c in range(nch):
            rows = slice(c * SGU_CHUNK, (c + 1) * SGU_CHUNK)
            mixed = _sgu_mix(wt_ref, vn[rows]) + bexp_ref[...]
            y_ref[rows, w:2 * w] = (zc[rows, 3 * w:4 * w] * mixed).astype(BF16)

        s1[0:hl] = prev(5) * _sigmoid(prev(6))
        s1[hl:hl + ts] = col(zc, 5) * _sigmoid(col(zc, 6))
        off = hl - (CCONV_K - 1)
        _make_shifts(s1, sh, hl + ts - 8)
        cv = cconv_ref[0:1] * _rows_at(s1, sh, off, ts)
        for k in range(1, CCONV_K):
            cv = cv + cconv_ref[k:k + 1] * _rows_at(s1, sh, off + k, ts)
        xhat, _ = _ln_stats(cv)
        ln = xhat * vec_ref[1:2] + vec_ref[2:3]
        y_ref[:, 2 * w:3 * w] = (ln * _sigmoid(ln)).astype(BF16)

        s1[0:hl] = prev(7)
        s1[hl:hl + ts] = col(zc, 7)
        pooled = _pool_fwd(s1, s2, s3, i * ts, ts, hl)
        y_ref[:, 3 * w:4 * w] = (_dot(pooled.astype(BF16), pbd_ref[...]) * vec_ref[3:4]).astype(BF16)

        xo_ref[...] = xr_ref[...] + _dot(y_ref[...], _full_weight(wmo_ref, "row"))

    full = lambda shape: pl.BlockSpec(shape, lambda i: (0,) * len(shape))
    row = lambda i: (i, 0)
    return pl.pallas_call(
        body, name=name, grid=(t // ts,),
        out_shape=[jax.ShapeDtypeStruct((t, 4 * w), BF16), jax.ShapeDtypeStruct((t, 4 * w), F32)],
        in_specs=[pl.BlockSpec((ts, 8 * w), row),
                  pl.BlockSpec((hl, 8 * w), lambda i: (jnp.maximum(i * (ts // hl) - 1, 0), 0)),
                  full((8, w)), full((32, w)), full((8, w)), full((N_HEADS, SGU_CHUNK, SGU_CHUNK)),
                  full((SGU_CHUNK, w)), full((w, w)), pl.BlockSpec((ts, 4 * w), row), _wspec(wmo_g)],
        out_specs=[pl.BlockSpec((ts, 4 * w), row), pl.BlockSpec((ts, 4 * w), row)],
        scratch_shapes=[pltpu.VMEM((hl + ts, w), F32)] * 3 + [pltpu.VMEM((8, hl + ts, w), F32)],
        compiler_params=_params("parallel"),
    )(z, z, sconv, cconv, vecs, wt, bexp, pbd, x_res, wmo_g)


def mixer_bwd(z, dx, wmo_g, sconv, cconv, vecs, wt, bexp, pbd, *, name, ts=None):
    t = z.shape[0]
    ts = _row_tile(t, MIX_TILE if ts is None else ts)
    hl = HALO
    w = MIX_W
    nch = ts // SGU_CHUNK
    ni = t // ts
    ext = ts + hl

    def body(zc, zp, zn, dxc, dxn_, wmo_ref, sconv_ref, cconv_ref, vec_ref, wt_ref, bexp_ref, pbd_ref,
             dz_ref, gvec_ref, gcc_ref, gwt_ref, gb_ref, gpbd_ref, s1, s2, s3, sh1, sh3, dyc, dyn):
        i = pl.program_id(0)
        has_prev = i > 0
        has_next = i < ni - 1
        wmo = _full_weight(wmo_ref, "row")
        dyc[...] = _dot_nt(dxc[...].astype(BF16), wmo)
        dyn[...] = _dot_nt(dxn_[...].astype(BF16), wmo)

        @pl.when(i == 0)
        def _():
            gvec_ref[...] = jnp.zeros_like(gvec_ref)
            gcc_ref[...] = jnp.zeros_like(gcc_ref)
            gwt_ref[...] = jnp.zeros_like(gwt_ref)
            gb_ref[...] = jnp.zeros_like(gb_ref)
            gpbd_ref[...] = jnp.zeros_like(gpbd_ref)

        def col(ref, c):
            return ref[:, c * w:(c + 1) * w]

        def prev(c):
            return jnp.where(has_prev, col(zp, c), 0.0)

        def nxt(c):
            return jnp.where(has_next, col(zn, c), 0.0)

        def dnext(c):
            return jnp.where(has_next, col(dyn, c), 0.0)

        def rowsum(v):
            return jnp.sum(v, axis=0, keepdims=True)

        s1[0:hl] = prev(1) * prev(2)
        s1[hl:hl + ts] = col(zc, 1) * col(zc, 2)
        s1[hl + ts:hl + ts + hl] = nxt(1) * nxt(2)
        cv = sconv_ref[0:1] * s1[hl - 2:hl - 2 + ts]
        for k in range(1, SCONV_K):
            cv = cv + sconv_ref[k:k + 1] * s1[hl - 2 + k:hl - 2 + k + ts]
        dya = col(dyc, 0)
        dz_ref[:, 0:w] = (dya * cv).astype(BF16)
        s2[0:ts] = dya * col(zc, 0)
        s2[ts:ext] = dnext(0) * nxt(0)
        dv = sconv_ref[0:1] * s2[2:2 + ts]
        for k in range(1, SCONV_K):
            dv = dv + sconv_ref[k:k + 1] * s2[2 - k:2 - k + ts]
        dz_ref[:, w:2 * w] = (dv * col(zc, 2)).astype(BF16)
        dz_ref[:, 2 * w:3 * w] = (dv * col(zc, 1)).astype(BF16)
        dcv = s2[0:ts]
        for k in range(SCONV_K):
            gvec_ref[k:k + 1] += rowsum(dcv * s1[hl - 2 + k:hl - 2 + k + ts])

        g_sgu = vec_ref[0:1]
        xhat, rstd = _ln_stats(col(zc, 4))
        vn = (xhat * g_sgu).astype(BF16)
        grp = _lane_group((SGU_CHUNK, w))
        lane = lax.broadcasted_iota(jnp.int32, (SGU_CHUNK, SGU_CHUNK), 1)
        tril = lax.broadcasted_iota(jnp.int32, (SGU_CHUNK, SGU_CHUNK), 0) >= lane
        for c in range(nch):
            rows = slice(c * SGU_CHUNK, (c + 1) * SGU_CHUNK)
            vnc = vn[rows]
            mixed = _sgu_mix(wt_ref, vnc) + bexp_ref[...]
            dyb = dyc[rows, w:2 * w]
            dz_ref[rows, 3 * w:4 * w] = (dyb * mixed).astype(BF16)
            dmix = dyb * zc[rows, 3 * w:4 * w]
            dmixb = dmix.astype(BF16)
            dvn = jnp.zeros((SGU_CHUNK, w), F32)
            gb = jnp.zeros((SGU_CHUNK, SGU_CHUNK), F32)
            for hd in range(N_HEADS):
                dvn = jnp.where(grp == hd, _dot_tn(wt_ref[hd], dmixb), dvn)
                dm_h = jnp.where(grp == hd, dmix, 0.0)
                gwt_ref[hd] += jnp.where(tril, _dot_nt(dm_h.astype(BF16), vnc), 0.0)
                gb = gb + jnp.where(lane == hd, jnp.sum(dm_h, axis=1, keepdims=True), 0.0)
            gb_ref[...] += gb
            s3[rows] = dvn
        dvn = s3[0:ts]
        gvec_ref[3:4] += rowsum(dvn * xhat)
        dz_ref[:, 4 * w:5 * w] = _ln_bwd(xhat, rstd, dvn * g_sgu).astype(BF16)

        sig_c = _sigmoid(col(zc, 6))
        s1[0:hl] = prev(5) * _sigmoid(prev(6))
        s1[hl:hl + ts] = col(zc, 5) * sig_c
        s1[hl + ts:hl + ts + hl] = nxt(5) * _sigmoid(nxt(6))
        off = hl - (CCONV_K - 1)
        _make_shifts(s1, sh1, ts + 2 * hl - 8)
        cv = cconv_ref[0:1] * _rows_at(s1, sh1, off, ext)
        for k in range(1, CCONV_K):
            cv = cv + cconv_ref[k:k + 1] * _rows_at(s1, sh1, off + k, ext)
        xhat, rstd = _ln_stats(cv)
        ln = xhat * vec_ref[1:2] + vec_ref[2:3]
        sg = _sigmoid(ln)
        s2[0:ts] = col(dyc, 2)
        s2[ts:ext] = dnext(2)
        dln = s2[0:ext] * (sg * (1.0 + ln * (1.0 - sg)))
        gvec_ref[4:5] += rowsum(dln[0:ts] * xhat[0:ts])
        gvec_ref[5:6] += rowsum(dln[0:ts])
        s3[0:ext] = _ln_bwd(xhat, rstd, dln * vec_ref[1:2])
        _make_shifts(s3, sh3, ext - 8)
        dyg = cconv_ref[0:1] * _rows_at(s3, sh3, CCONV_K - 1, ts)
        for k in range(1, CCONV_K):
            dyg = dyg + cconv_ref[k:k + 1] * _rows_at(s3, sh3, CCONV_K - 1 - k, ts)
        dz_ref[:, 5 * w:6 * w] = (dyg * sig_c).astype(BF16)
        dz_ref[:, 6 * w:7 * w] = (dyg * col(zc, 5) * sig_c * (1.0 - sig_c)).astype(BF16)
        dcv = s3[0:ts]
        for k in range(CCONV_K):
            gcc_ref[k:k + 1] += rowsum(dcv * _rows_at(s1, sh1, off + k, ts))

        scale = vec_ref[3:4]
        s1[0:hl] = prev(7)
        s1[hl:hl + ts] = col(zc, 7)
        pooled = _pool_fwd(s1, s2, s3, i * ts, ts, hl).astype(BF16)
        q0 = _dot(pooled, pbd_ref[...])
        dyd = col(dyc, 3)
        gvec_ref[6:7] += rowsum(dyd * q0)
        dq = (dyd * scale).astype(BF16)
        gpbd_ref[...] += _dot_tn(pooled, dq)
        s1[0:ts] = _dot_nt(dq, pbd_ref[...])
        s1[ts:ext] = _dot_nt((dnext(3) * scale).astype(BF16), pbd_ref[...])
        dpool = s1[0:ts]
        s2[0:ext] = s1[0:ext] / _pool_count(i * ts, ext)
        s3[0:ts + 24] = s2[0:ts + 24] + s2[1:ts + 25]
        f2 = s3[0:ts]
        s2[0:ts + 16] = s3[0:ts + 16] + s3[2:ts + 18]
        f4 = s2[0:ts]
        s3[0:ts + 8] = s2[0:ts + 8] + s2[4:ts + 12]
        f8 = s3[0:ts]
        f16 = f8 + s3[8:ts + 8]
        dz_ref[:, 7 * w:8 * w] = (_by_group(_lane_group((ts, w)), f2, f4, f8, f16) - dpool).astype(BF16)

    full = lambda shape: pl.BlockSpec(shape, lambda i: (0,) * len(shape))
    r = ts // hl
    prev_map = lambda i: (jnp.maximum(i * r - 1, 0), 0)
    next_map = lambda i: (jnp.minimum((i + 1) * r, t // hl - 1), 0)
    return pl.pallas_call(
        body, name=name, grid=(ni,),
        out_shape=[jax.ShapeDtypeStruct((t, 8 * w), BF16), jax.ShapeDtypeStruct((8, w), F32),
                   jax.ShapeDtypeStruct((32, w), F32),
                   jax.ShapeDtypeStruct((N_HEADS, SGU_CHUNK, SGU_CHUNK), F32),
                   jax.ShapeDtypeStruct((SGU_CHUNK, SGU_CHUNK), F32), jax.ShapeDtypeStruct((w, w), F32)],
        in_specs=[pl.BlockSpec((ts, 8 * w), lambda i: (i, 0)),
                  pl.BlockSpec((hl, 8 * w), prev_map), pl.BlockSpec((hl, 8 * w), next_map),
                  pl.BlockSpec((ts, 4 * w), lambda i: (i, 0)), pl.BlockSpec((hl, 4 * w), next_map),
                  _wspec(wmo_g),
                  full((8, w)), full((32, w)), full((8, w)), full((N_HEADS, SGU_CHUNK, SGU_CHUNK)),
                  full((SGU_CHUNK, w)), full((w, w))],
        out_specs=[pl.BlockSpec((ts, 8 * w), lambda i: (i, 0)), full((8, w)), full((32, w)),
                   full((N_HEADS, SGU_CHUNK, SGU_CHUNK)), full((SGU_CHUNK, SGU_CHUNK)), full((w, w))],
        scratch_shapes=[pltpu.VMEM((ts + 2 * hl, w), F32)] * 3 + [pltpu.VMEM((8, ts + 2 * hl, w), F32)] * 2
        + [pltpu.VMEM((ts, 4 * w), F32), pltpu.VMEM((hl, 4 * w), F32)],
        compiler_params=_params("arbitrary"),
    )(z, z, z, dx, dx, wmo_g, sconv, cconv, vecs, wt, bexp, pbd)


def _attn_head(q, kv_ref, hd, d):
    hw = d // N_HEADS
    qh = q[:, hd * hw:(hd + 1) * hw]
    kh = kv_ref[:, hd * hw:(hd + 1) * hw].astype(BF16)
    vh = kv_ref[:, d + hd * hw:d + (hd + 1) * hw].astype(BF16)
    s = _dot_nt(qh, kh) * (1.0 / (hw ** 0.5))
    e = jnp.exp(s - jnp.max(s, axis=-1, keepdims=True))
    p = e / jnp.sum(e, axis=-1, keepdims=True)
    return qh, kh, vh, p


def xattn_fwd(x, gain, kv, wq_g, wo_g, *, name, tm=None):
    t, d = x.shape
    nm = kv.shape[0]
    tm = _row_tile(t, tm)
    hw = d // N_HEADS

    def body(x_ref, g_ref, kv_ref, wq_ref, wo_ref, o_ref):
        xv = x_ref[...]
        xhat, _ = _rms_fwd(xv, None)
        h = (xhat * g_ref[...]).astype(BF16)
        q = _dot(h, _full_weight(wq_ref, "row")).astype(BF16)
        wo = _full_weight(wo_ref, "row")
        out = xv
        for hd in range(N_HEADS):
            _, _, vh, p = _attn_head(q, kv_ref, hd, d)
            oh = _dot(p.astype(BF16), vh).astype(BF16)
            out = out + _dot(oh, wo[hd * hw:(hd + 1) * hw])
        o_ref[...] = out

    row = lambda i: (i, 0)
    return pl.pallas_call(
        body, name=name, grid=(t // tm,),
        out_shape=jax.ShapeDtypeStruct((t, d), F32),
        in_specs=[pl.BlockSpec((tm, d), row), pl.BlockSpec((1, d), lambda i: (0, 0)),
                  pl.BlockSpec((nm, 2 * d), lambda i: (0, 0)), _wspec(wq_g), _wspec(wo_g)],
        out_specs=pl.BlockSpec((tm, d), row),
        compiler_params=_params("parallel"),
    )(x, gain.reshape(1, d), kv, wq_g, wo_g)


def xattn_bwd_rows(x, dxn, gain, kv, wq_g, wo_g, *, name, tm=None):
    t, d = x.shape
    nm = kv.shape[0]
    tm = _row_tile(t, tm)
    hw = d // N_HEADS

    def body(x_ref, dxn_ref, g_ref, kv_ref, wq_ref, wo_ref,
             dx_ref, h_ref, dq_ref, o_ref, dkv_ref, dg_ref):
        i = pl.program_id(0)

        @pl.when(i == 0)
        def _():
            dkv_ref[...] = jnp.zeros_like(dkv_ref)
            dg_ref[...] = jnp.zeros_like(dg_ref)
        g = g_ref[...]
        xhat, r = _rms_fwd(x_ref[...], None)
        h = (xhat * g).astype(BF16)
        h_ref[...] = h
        wq = _full_weight(wq_ref, "row")
        q = _dot(h, wq).astype(BF16)
        dxn = dxn_ref[...]
        do = _dot_nt(dxn.astype(BF16), _full_weight(wo_ref, "row")).astype(BF16)
        for hd in range(N_HEADS):
            cols = slice(hd * hw, (hd + 1) * hw)
            qh, kh, vh, p = _attn_head(q, kv_ref, hd, d)
            pb = p.astype(BF16)
            o_ref[:, cols] = _dot(pb, vh).astype(BF16)
            doh = do[:, cols]
            dkv_ref[:, d + hd * hw:d + (hd + 1) * hw] += _dot_tn(pb, doh)
            dp = _dot_nt(doh, vh)
            ds = (p * (dp - jnp.sum(dp * p, axis=-1, keepdims=True)) * (1.0 / (hw ** 0.5))).astype(BF16)
            dq_ref[:, cols] = _dot(ds, kh).astype(BF16)
            dkv_ref[:, cols] += _dot_tn(ds, qh)
        dh = _dot_nt(dq_ref[...], wq)
        dx, dg = _rms_bwd(xhat, r, g, dh)
        dx_ref[...] = dxn + dx
        dg_ref[...] += dg

    row = lambda i: (i, 0)
    fix = lambda i: (0, 0)
    return pl.pallas_call(
        body, name=name, grid=(t // tm,),
        out_shape=[jax.ShapeDtypeStruct((t, d), F32), jax.ShapeDtypeStruct((t, d), BF16),
                   jax.ShapeDtypeStruct((t, d), BF16), jax.ShapeDtypeStruct((t, d), BF16),
                   jax.ShapeDtypeStruct((nm, 2 * d), F32), jax.ShapeDtypeStruct((1, d), F32)],
        in_specs=[pl.BlockSpec((tm, d), row), pl.BlockSpec((tm, d), row), pl.BlockSpec((1, d), fix),
                  pl.BlockSpec((nm, 2 * d), fix), _wspec(wq_g), _wspec(wo_g)],
        out_specs=[pl.BlockSpec((tm, d), row)] * 4 + [pl.BlockSpec((nm, 2 * d), fix),
                                                      pl.BlockSpec((1, d), fix)],
        compiler_params=_params("arbitrary"),
    )(x, dxn, gain.reshape(1, d), kv, wq_g, wo_g)


def loss_head(x, target, gain, *, name, tm=None):
    t, d = x.shape
    tm = _row_tile(t, tm)

    def body(x_ref, t_ref, g_ref, dx_ref, dg_ref, loss_ref):
        @pl.when(pl.program_id(0) == 0)
        def _():
            dg_ref[...] = jnp.zeros_like(dg_ref)
            loss_ref[...] = jnp.zeros_like(loss_ref)
        g = g_ref[...]
        xhat, r = _rms_fwd(x_ref[...], None)
        err = xhat * g - t_ref[...]
        loss_ref[...] += 0.5 * jnp.sum(jnp.sum(err * err, axis=-1, keepdims=True) / d,
                                       axis=0, keepdims=True)
        dx, dg = _rms_bwd(xhat, r, g, err / d)
        dx_ref[...] = dx
        dg_ref[...] += dg

    row = lambda i: (i, 0)
    fix = lambda i: (0, 0)
    return pl.pallas_call(
        body, name=name, grid=(t // tm,),
        out_shape=[jax.ShapeDtypeStruct((t, d), F32), jax.ShapeDtypeStruct((1, d), F32),
                   jax.ShapeDtypeStruct((1, 1), F32)],
        in_specs=[pl.BlockSpec((tm, d), row), pl.BlockSpec((tm, d), row), pl.BlockSpec((1, d), fix)],
        out_specs=[pl.BlockSpec((tm, d), row), pl.BlockSpec((1, d), fix), pl.BlockSpec((1, 1), fix)],
        compiler_params=_params("arbitrary"),
    )(x, target, gain.reshape(1, d))


def _adamw_math(w, g, m, v):
    m = ADAM_B1 * m + (1.0 - ADAM_B1) * g
    v = ADAM_B2 * v + (1.0 - ADAM_B2) * (g * g)
    m_hat = m / (1.0 - ADAM_B1 ** ADAM_STEP)
    v_hat = v / (1.0 - ADAM_B2 ** ADAM_STEP)
    delta = -ADAM_LR * (m_hat / (jnp.sqrt(v_hat) + ADAM_EPS) + ADAM_WD * w)
    return delta, m, v


def adamw_sharded(own, lands, w, m, v, me_arr, *, name):
    nl, r, c = w.shape
    assert nl == len(own) == len(lands) == 2
    tr = next(cand for cand in (256, 176, 128, r) if r % cand == 0)
    nr = r // tr

    def body(me_ref, o0, o1, l0, l1, w_ref, m_ref, v_ref, g_out, d_out, m_out, v_out):
        def total(o_ref, l_ref):
            acc = o_ref[...].astype(F32)
            for p in range(N_DEV - 1):
                acc = acc + l_ref[p].astype(F32)
            return acc
        g = jnp.where(pl.program_id(0) == 0, total(o0, l0), total(o1, l1))
        delta, mn, vn = _adamw_math(w_ref[...], g, m_ref[...], v_ref[...])
        g_out[...] = g
        d_out[...] = delta
        m_out[...] = mn
        v_out[...] = vn

    row0 = lambda l, i: jnp.where(l == 0, i, nr - 1)
    row1 = lambda l, i: jnp.where(l == 1, i, 0)
    blk = pl.BlockSpec((None, tr, c), lambda l, i, me: (l, i, 0))
    grid_spec = pltpu.PrefetchScalarGridSpec(
        num_scalar_prefetch=1, grid=(nl, nr),
        in_specs=[pl.BlockSpec((None, tr, c), lambda l, i, me: (me[0], row0(l, i), 0)),
                  pl.BlockSpec((None, tr, c), lambda l, i, me: (me[0], row1(l, i), 0)),
                  pl.BlockSpec((N_DEV - 1, tr, c), lambda l, i, me: (0, row0(l, i), 0)),
                  pl.BlockSpec((N_DEV - 1, tr, c), lambda l, i, me: (0, row1(l, i), 0)),
                  blk, blk, blk],
        out_specs=[blk] * 4)
    return pl.pallas_call(
        body, name=name, grid_spec=grid_spec,
        out_shape=[jax.ShapeDtypeStruct((nl, r, c), F32)] * 4,
        compiler_params=_params("arbitrary", "arbitrary"),
    )(me_arr, own[0], own[1], lands[0], lands[1], w, m, v)


def adamw_flat(g, w, m, v, *, name):
    def body(g_ref, w_ref, m_ref, v_ref, d_out, m_out, v_out):
        delta, mn, vn = _adamw_math(w_ref[...], g_ref[...], m_ref[...], v_ref[...])
        d_out[...] = delta
        m_out[...] = mn
        v_out[...] = vn

    return pl.pallas_call(
        body, name=name, out_shape=[jax.ShapeDtypeStruct(w.shape, F32)] * 3,
        in_specs=[VMEM_SPEC] * 4, out_specs=[VMEM_SPEC] * 3,
        compiler_params=pltpu.CompilerParams(vmem_limit_bytes=VMEM_LIMIT),
    )(g, w, m, v)


def cast_into_slot(a, layer, me_arr, *, name, dtype=None, after=None):
    dtype = BF16 if dtype is None else dtype
    _, r, c = a.shape
    tr = next(cand for cand in (256, 176, 128, r) if r % cand == 0)
    extra = [] if after is None else [after]

    def body(me_ref, a_ref, *rest):
        rest[-1][...] = a_ref[...].astype(dtype)

    grid_spec = pltpu.PrefetchScalarGridSpec(
        num_scalar_prefetch=1, grid=(r // tr,),
        in_specs=[pl.BlockSpec((None, tr, c), lambda i, me: (layer, i, 0))] + [ANY_SPEC] * len(extra),
        out_specs=pl.BlockSpec((None, tr, c), lambda i, me: (me[0], i, 0)))
    return pl.pallas_call(
        body, name=name, grid_spec=grid_spec,
        out_shape=jax.ShapeDtypeStruct((N_DEV, r, c), dtype),
        compiler_params=_params("parallel"),
    )(me_arr, a, *extra)


def _pack(arrs, rows):
    flat = jnp.concatenate([a.reshape(-1).astype(F32) for a in arrs])
    pad = rows * 128 - flat.shape[0]
    assert pad >= 0
    if pad:
        flat = jnp.concatenate([flat, jnp.zeros((pad,), F32)])
    return flat.reshape(rows, 128)


def _unpack(packed, shapes):
    flat = packed.reshape(-1)
    out, pos = [], 0
    for s in shapes:
        n = 1
        for dim in s:
            n *= dim
        out.append(flat[pos:pos + n].reshape(s))
        pos += n
    return out


def _rows_for(shapes):
    n = 0
    for s in shapes:
        k = 1
        for dim in s:
            k *= dim
        n += k
    return -(-n // 1024) * 8


GATHER_GROUPS = (("ffn1", ("ffn1_w_in", "ffn1_w_out")),
                 ("mid", ("mix_w_in", "mix_w_out", "xattn_wkv", "xattn_wq", "xattn_wo")),
                 ("ffn2", ("ffn2_w_in", "ffn2_w_out")))
SMALL_REPL = ["norm_ffn1", "norm_mix", "sgu_norm_g", "sgu_w", "sgu_b", "cconv_ln_g", "cconv_ln_b",
              "pool_w", "pool_scale", "norm_xattn", "norm_mem", "norm_ffn2", "norm_final"]
SMALL_SHARD = ["sconv_w", "cconv_w"]
TRANSPOSED = ("ffn1_w_in", "ffn2_w_in")
WEIGHTS = ["norm_ffn1", "ffn1_w_in", "ffn1_w_out", "norm_mix", "mix_w_in", "sconv_w", "sgu_norm_g",
           "sgu_w", "sgu_b", "cconv_w", "cconv_ln_g", "cconv_ln_b", "pool_w", "pool_scale", "mix_w_out",
           "norm_xattn", "norm_mem", "xattn_wq", "xattn_wkv", "xattn_wo", "norm_ffn2", "ffn2_w_in",
           "ffn2_w_out", "norm_final"]


def kernel(x, mem, norm_ffn1, ffn1_w_in, ffn1_w_out, norm_mix, mix_w_in, sconv_w, sgu_norm_g, sgu_w, sgu_b, cconv_w, cconv_ln_g, cconv_ln_b, pool_w, pool_scale, mix_w_out, norm_xattn, norm_mem, xattn_wq, xattn_wkv, xattn_wo, norm_ffn2, ffn2_w_in, ffn2_w_out, norm_final, loss_target, m_norm_ffn1, m_ffn1_w_in, m_ffn1_w_out, m_norm_mix, m_mix_w_in, m_sconv_w, m_sgu_norm_g, m_sgu_w, m_sgu_b, m_cconv_w, m_cconv_ln_g, m_cconv_ln_b, m_pool_w, m_pool_scale, m_mix_w_out, m_norm_xattn, m_norm_mem, m_xattn_wq, m_xattn_wkv, m_xattn_wo, m_norm_ffn2, m_ffn2_w_in, m_ffn2_w_out, m_norm_final, v_norm_ffn1, v_ffn1_w_in, v_ffn1_w_out, v_norm_mix, v_mix_w_in, v_sconv_w, v_sgu_norm_g, v_sgu_w, v_sgu_b, v_cconv_w, v_cconv_ln_g, v_cconv_ln_b, v_pool_w, v_pool_scale, v_mix_w_out, v_norm_xattn, v_norm_mem, v_xattn_wq, v_xattn_wkv, v_xattn_wo, v_norm_ffn2, v_ffn2_w_in, v_ffn2_w_out, v_norm_final):
    args = dict(locals())
    wts = {n: args[n] for n in WEIGHTS}
    mom = {n: args["m_" + n] for n in WEIGHTS}
    var = {n: args["v_" + n] for n in WEIGHTS}
    for n in TRANSPOSED:
        wts[n], mom[n], var[n] = (jnp.swapaxes(a, 1, 2) for a in (wts[n], mom[n], var[n]))
    x0 = x[0]
    mem0 = mem[0]
    target = loss_target[0]
    t, d = x0.shape
    nl = norm_ffn1.shape[0]
    w = MIX_W
    me = _my_index()

    me_arr = jnp.reshape(me, (1,)).astype(jnp.int32)

    small_g = all_gather([sconv_w, cconv_w], name="gather_conv_taps")
    sconv_full = jnp.transpose(small_g[0], (1, 2, 0, 3)).reshape(nl, SCONV_K, w)
    cconv_full = jnp.transpose(small_g[1], (1, 2, 0, 3)).reshape(nl, CCONV_K, w)
    pending = {}
    token = small_g[1]
    masks = GATHER_MASKS
    keys = [(gname, l, members) for l in range(nl) for gname, members in GATHER_GROUPS]
    first = [[cast_into_slot(wts[n], keys[0][1], me_arr, name=f"cast_{n}{keys[0][1]}") for n in keys[0][2]]]
    started, token = gather_start_groups(first, token, name="gather_start_first", masks=masks)
    casts = [[cast_into_slot(wts[n], l, me_arr, name=f"cast_{n}{l}", after=token) for n in members]
             for gname, l, members in keys[1:]]
    rest, token = gather_start_groups(casts, token, name="gather_start_rest", masks=masks)
    for (gname, l, members), (send, recv, gs) in zip(keys, started + rest):
        pending[gname, l] = (members, gs, send, recv, masks)
    wg = [dict() for _ in range(nl)]

    def arrive(gname, l, after):
        members, gs, send, recv, masks = pending.pop((gname, l))
        gs = gather_wait(gs, send, recv, after, name=f"gather_wait_{gname}{l}", masks=masks)
        if masks is GATHER_MASKS:
            gs = sibling_forward(gs, name=f"gather_forward_{gname}{l}")
        wg[l].update(zip(members, gs))
    sconv_pad = jnp.pad(sconv_full, ((0, 0), (0, 8 - SCONV_K), (0, 0)))
    cconv_pad = jnp.pad(cconv_full, ((0, 0), (0, 32 - CCONV_K), (0, 0)))
    zeros_w = jnp.zeros((nl, w), F32)
    vecs = jnp.stack([sgu_norm_g, cconv_ln_g, cconv_ln_b, pool_scale] + [zeros_w] * 4, axis=1)
    wt = jnp.tril(sgu_w).astype(BF16)
    bexp = jnp.repeat(jnp.swapaxes(sgu_b, 1, 2), w // N_HEADS, axis=2)
    eye = jnp.eye(4, dtype=F32)
    pbd = jnp.einsum("lgcd,gh->lgchd", pool_w, eye).reshape(nl, w, w).astype(BF16)

    def mixer_args(l):
        return sconv_pad[l], cconv_pad[l], vecs[l], wt[l], bexp[l], pbd[l]

    saved = []
    xc = x0
    after = token
    for l in range(nl):
        s = {"x_ffn1": xc}
        arrive("ffn1", l, after)
        xc, s["gu_ffn1"] = ffn_fwd(xc, norm_ffn1[l], wg[l]["ffn1_w_in"], wg[l]["ffn1_w_out"],
                                   name=f"ffn1_fwd{l}", tm=FFN_FWD_TILE)
        s["x_mix"] = xc
        arrive("mid", l, xc)
        z = mm_rows(xc, wg[l]["mix_w_in"], "col", gain=norm_mix[l], name=f"mix_in{l}")
        y, xc = mixer_fwd(z, *mixer_args(l), xc, wg[l]["mix_w_out"], name=f"mixer_fwd{l}")
        s["z"], s["y"] = z, y
        s["x_att"] = xc
        kv = mm_rows(mem0, wg[l]["xattn_wkv"], "col", gain=norm_mem[l], name=f"kv{l}")
        s["kv"] = kv
        xc = xattn_fwd(xc, norm_xattn[l], kv, wg[l]["xattn_wq"], wg[l]["xattn_wo"], name=f"xattn_fwd{l}")
        s["x_ffn2"] = xc
        arrive("ffn2", l, xc)
        xc, s["gu_ffn2"] = ffn_fwd(xc, norm_ffn2[l], wg[l]["ffn2_w_in"], wg[l]["ffn2_w_out"],
                                   name=f"ffn2_fwd{l}", tm=FFN_FWD_TILE)
        after = xc
        saved.append(s)

    dx, g_norm_final, loss_local = loss_head(xc, target, norm_final, name="loss_head")
    loss = lax.psum(loss_local[0, 0], ("x", "y", "c"))

    tm = _row_tile(t, TN_TILE)
    small ={n: [None] * nl for n in SMALL_REPL + SMALL_SHARD if n != "norm_final"}
    scattered = {}
    tie = [token]

    def send_grads(gname, l, grads):
        members = list(grads)
        send, recv, gs, lands, tie[0] = scatter_start(
            [grads[n] for n in members], tie[0], name=f"scatter_start_{gname}{l}")
        scattered[gname, l] = (members, gs, lands, send, recv)

    def tied(v):
        return v + tie[0][0, 0]

    names = SMALL_REPL + SMALL_SHARD
    small_pending = []

    def start_small():
        small_full = {n: jnp.stack(v) for n, v in small.items()}
        small_full["norm_final"] = g_norm_final[0]
        shapes = [small_full[n].shape for n in names]
        packed = _pack([small_full[n] for n in names], _rows_for(shapes))
        slot = cast_into_slot(packed[None], 0, me_arr, name="small_into_slot", dtype=F32)
        send, recv, gs, tie[0] = gather_start([slot], tie[0], name="small_gather_start", masks=ALL_MASKS)
        small_pending.append((gs, send, recv, shapes))

    def ffn_backward(which, l, x_in, dy, gu, gain):
        w_in, w_out = wg[l][which + "_w_in"], wg[l][which + "_w_out"]
        dx_, h_, act, dgu, dgn, dyh = ffn_bwd_rows(x_in, dy, gu, tied(gain), w_in, w_out,
                                                   name=f"{which}_bwd{l}_rows")
        small["norm_" + which][l] = dgn[0]
        last = which == "ffn1" and l == 0
        if last:
            start_small()
        g_in = ffn_grad_w_in(h_, dgu, tie[0], name=f"{which}_bwd{l}_dwin")
        if last:
            send_grads(which + "_in", l, {which + "_w_in": g_in})
        g_out = ffn_grad_w_out(act, dyh, tie[0], name=f"{which}_bwd{l}_dwout")
        if last:
            send_grads(which + "_out", l, {which + "_w_out": g_out})
        else:
            send_grads(which, l, {which + "_w_in": g_in, which + "_w_out": g_out})
        return dx_

    for l in reversed(range(nl)):
        s = saved[l]
        wl = wg[l]
        dx = ffn_backward("ffn2", l, s["x_ffn2"], dx, s["gu_ffn2"], norm_ffn2[l])

        bg = {}
        dxn = dx
        dx, h, dq, o, dkv, dgn = xattn_bwd_rows(
            s["x_att"], dxn, tied(norm_xattn[l]), s["kv"], wl["xattn_wq"], wl["xattn_wo"],
            name=f"xattn_bwd{l}")
        small["norm_xattn"][l] = dgn[0]
        row_spec = pl.BlockSpec((tm, d), lambda s_, i: (i, 0))
        bg["xattn_wq"] = mm_tn(h, dq, nb=1, ka=d, nbk=d, tm=tm, m=t, a_spec=row_spec, b_spec=row_spec,
                               name=f"dwq{l}").reshape(N_DEV, d // N_DEV, d)
        bg["xattn_wo"] = mm_tn(o, dxn, nb=1, ka=d, nbk=d, tm=tm, m=t, a_spec=row_spec, b_spec=row_spec,
                               name=f"dwo{l}").reshape(N_DEV, d // N_DEV, d)
        _, mhat, dgn = mm_nt(dkv, wl["xattn_wkv"], "col", x=mem0, gain=norm_mem[l], name=f"dmem{l}")
        small["norm_mem"][l] = dgn[0]
        nm = mem0.shape[0]
        bg["xattn_wkv"] = mm_tn(mhat, dkv, nb=N_DEV, ka=d, nbk=2 * d // N_DEV, tm=nm, m=nm,
                                a_spec=pl.BlockSpec((nm, d), lambda s_, i: (0, 0)),
                                b_spec=pl.BlockSpec((nm, 2 * d // N_DEV), lambda s_, i: (0, s_)),
                                name=f"dwkv{l}")
        send_grads("xattn", l, bg)

        bg = {}
        dxn = dx
        bg["mix_w_out"] = mm_tn(s["y"], dxn, nb=1, ka=d, nbk=d, tm=tm, m=t, a_spec=row_spec,
                                b_spec=row_spec, name=f"dwmo{l}").reshape(N_DEV, d // N_DEV, d)
        dz, gvec, gcc, gwt, gb, gpbd = mixer_bwd(s["z"], dxn, wl["mix_w_out"], *mixer_args(l),
                                                 name=f"mixer_bwd{l}")
        small["sconv_w"][l] = gvec[0:SCONV_K]
        small["sgu_norm_g"][l] = gvec[3]
        small["cconv_ln_g"][l] = gvec[4]
        small["cconv_ln_b"][l] = gvec[5]
        small["pool_scale"][l] = gvec[6]
        small["cconv_w"][l] = gcc[0:CCONV_K]
        small["sgu_w"][l] = gwt
        small["sgu_b"][l] = jnp.transpose(gb[:, 0:N_HEADS])
        gw = w // 4
        small["pool_w"][l] = jnp.stack([gpbd[g * gw:(g + 1) * gw, g * gw:(g + 1) * gw] for g in range(4)])
        dx, h, dgn = mm_nt(dz, wl["mix_w_in"], "col", x=s["x_mix"], gain=tied(norm_mix[l]), dx_in=dxn,
                           name=f"dh_mix{l}")
        small["norm_mix"][l] = dgn[0]
        th = _row_tile(t, TN_TILE // 2)
        bg["mix_w_in"] = mm_tn(h, dz, nb=1, ka=d, nbk=N_DEV * w, tm=th, m=t, col_slots=N_DEV,
                               a_spec=pl.BlockSpec((th, d), lambda s_, i: (i, 0)),
                               b_spec=pl.BlockSpec((th, N_DEV * w), lambda s_, i: (i, 0)), name=f"dwmi{l}")
        send_grads("mix", l, bg)

        dx = ffn_backward("ffn1", l, s["x_ffn1"], dx, s["gu_ffn1"], norm_ffn1[l])

    out = {}

    def finish(keys, after):
        own, land = {}, {}
        for gname, l in keys:
            members, gs, lands, send, recv = scattered.pop((gname, l))
            gs, lands = scatter_wait(gs, lands, send, recv, after, name=f"scatter_wait_{gname}{l}")
            for n, g_, l_ in zip(members, gs, lands):
                own.setdefault(n, {})[l] = g_
                land.setdefault(n, {})[l] = l_
        for n in own:
            out[n] = adamw_sharded([own[n][l] for l in range(nl)], [land[n][l] for l in range(nl)],
                                   wts[n], mom[n], var[n], me_arr, name="adamw_" + n)
            after = out[n][1]
        return after

    after = tie[0]
    for gname in ("ffn2", "xattn", "mix"):
        after = finish([(gname, l) for l in reversed(range(nl))], after)
    (gs, send, recv, shapes), = small_pending
    gs = gather_wait(gs, send, recv, after, name="small_gather_wait", masks=ALL_MASKS)
    summed = sum_slots(gs[0], name="small_sum")
    gsm = dict(zip(names, _unpack(summed, shapes)))
    finish([("ffn1", l) for l in reversed(range(1, nl))] + [("ffn1_in", 0), ("ffn1_out", 0)], summed)
    repl_shapes = [wts[n].shape for n in SMALL_REPL]
    rows_r = _rows_for(repl_shapes)
    dl, mn, vn = adamw_flat(_pack([gsm[n] for n in SMALL_REPL], rows_r),
                            _pack([wts[n] for n in SMALL_REPL], rows_r),
                            _pack([mom[n] for n in SMALL_REPL], rows_r),
                            _pack([var[n] for n in SMALL_REPL], rows_r), name="adamw_small")
    for n, a, b, c in zip(SMALL_REPL, _unpack(dl, repl_shapes), _unpack(mn, repl_shapes),
                          _unpack(vn, repl_shapes)):
        out[n] = (gsm[n], a, b, c)
    cs = w // N_DEV
    gsh = {n: lax.dynamic_slice_in_dim(gsm[n], me * cs, cs, axis=2) for n in SMALL_SHARD}
    sh_shapes = [wts[n].shape for n in SMALL_SHARD]
    rows_s = _rows_for(sh_shapes)
    dl, mn, vn = adamw_flat(_pack([gsh[n] for n in SMALL_SHARD], rows_s),
                            _pack([wts[n] for n in SMALL_SHARD], rows_s),
                            _pack([mom[n] for n in SMALL_SHARD], rows_s),
                            _pack([var[n] for n in SMALL_SHARD], rows_s), name="adamw_small_sharded")
    for n, a, b, c in zip(SMALL_SHARD, _unpack(dl, sh_shapes), _unpack(mn, sh_shapes),
                          _unpack(vn, sh_shapes)):
        out[n] = (gsh[n], a, b, c)
    for n in TRANSPOSED:
        out[n] = tuple(jnp.swapaxes(a, 1, 2) for a in out[n])

    grad_x = dx.reshape(1, t, d)
    return (loss, grad_x, *[out[n][0] for n in WEIGHTS], *[out[n][1] for n in WEIGHTS],
            *[out[n][2] for n in WEIGHTS], *[out[n][3] for n in WEIGHTS])
```

```python
import functools

import jax
import jax.numpy as jnp
from jax import lax
from jax.experimental import pallas as pl
from jax.experimental.pallas import tpu as pltpu

F32 = jnp.float32
BF16 = jnp.bfloat16
MESH = pl.DeviceIdType.MESH
N_DEV = 8
EPS = 1e-6
HALO = 32
SGU_CHUNK = 128
CCONV_K = 31
SCONV_K = 3
MIX_W = 256
N_HEADS = 4
VMEM_LIMIT = 56 * 1024 * 1024
ROW_TILE = 512
TN_TILE = 2048
FFN_FWD_TILE = 1024
FFN_BWD_SPLIT = 2
FFN_FWD_SPLIT = 2
MIX_TILE = 512

ADAM_LR = 0.001
ADAM_B1 = 0.9
ADAM_B2 = 0.999
ADAM_EPS = 1e-08
ADAM_WD = 0.01
ADAM_STEP = 10

HBM_SPEC = pl.BlockSpec(memory_space=pltpu.HBM)
VMEM_SPEC = pl.BlockSpec(memory_space=pltpu.VMEM)


def _params(*sem):
    return pltpu.CompilerParams(dimension_semantics=tuple(sem), vmem_limit_bytes=VMEM_LIMIT)


def _row_tile(m, pref=None):
    t = min(m, ROW_TILE if pref is None else pref)
    assert m % t == 0, (m, t)
    return t


def _my_index():
    return lax.axis_index("x") * 4 + lax.axis_index("y") * 2 + lax.axis_index("c")


def _peer(mask):
    x, y, c = lax.axis_index("x"), lax.axis_index("y"), lax.axis_index("c")
    px = 1 - x if mask & 4 else x
    py = 1 - y if mask & 2 else y
    pc = 1 - c if mask & 1 else c
    return (px, py, pc), px * 4 + py * 2 + pc


def all_gather(arrs, name):
    n = len(arrs)

    def body(*refs):
        ins, outs = refs[:n], refs[n:2 * n]
        send_sems, recv_sems, loc_sems = refs[2 * n:]
        me = _my_index()
        local = []
        for i in range(n):
            cp = pltpu.make_async_copy(ins[i], outs[i].at[me], loc_sems.at[i])
            cp.start()
            local.append(cp)
        sends = []
        for i in range(n):
            for m in range(1, N_DEV):
                peer, _ = _peer(m)
                cp = pltpu.make_async_remote_copy(
                    src_ref=ins[i], dst_ref=outs[i].at[me],
                    send_sem=send_sems.at[i, m - 1], recv_sem=recv_sems.at[i, m - 1],
                    device_id=peer, device_id_type=MESH)
                cp.start()
                sends.append(cp)
        for i in range(n):
            for m in range(1, N_DEV):
                peer, pidx = _peer(m)
                pltpu.make_async_remote_copy(
                    src_ref=ins[i], dst_ref=outs[i].at[pidx],
                    send_sem=send_sems.at[i, m - 1], recv_sem=recv_sems.at[i, m - 1],
                    device_id=peer, device_id_type=MESH).wait_recv()
        for cp in sends:
            cp.wait_send()
        for cp in local:
            cp.wait()

    return pl.pallas_call(
        body, name=name,
        out_shape=[jax.ShapeDtypeStruct((N_DEV,) + a.shape, a.dtype) for a in arrs],
        in_specs=[HBM_SPEC] * n, out_specs=[HBM_SPEC] * n,
        scratch_shapes=[pltpu.SemaphoreType.DMA((n, N_DEV - 1)),
                        pltpu.SemaphoreType.DMA((n, N_DEV - 1)),
                        pltpu.SemaphoreType.DMA((n,))],
    )(*arrs)


SEM_SPEC = pl.BlockSpec(memory_space=pltpu.SEMAPHORE)
ANY_SPEC = pl.BlockSpec(memory_space=pl.ANY)
SIDE_EFFECT = pltpu.SideEffectType.DATAFLOW_SIDE_EFFECTING


def _hbm(a):
    return pltpu.with_memory_space_constraint(a, pltpu.HBM)


def _sem_pairs(n):
    return (pltpu.SemaphoreType.DMA((n * (N_DEV - 1),)), pltpu.SemaphoreType.DMA((n * (N_DEV - 1),)))


def _sem(i, m):
    return i * (N_DEV - 1) + m - 1


def _gather_copy(g_ref, i, m, send_sems, recv_sems, origin):
    peer, _ = _peer(m)
    return pltpu.make_async_remote_copy(
        src_ref=g_ref.at[origin], dst_ref=g_ref.at[origin],
        send_sem=send_sems.at[_sem(i, m)], recv_sem=recv_sems.at[_sem(i, m)],
        device_id=peer, device_id_type=MESH)


GATHER_MASKS = (1, 2, 4, 6)
FORWARD_MASKS = (2, 4, 6)


ALL_MASKS = tuple(range(1, N_DEV))


def gather_start(gs, after, name, masks=GATHER_MASKS):
    n = len(gs)

    def body(*refs):
        g_in = refs[:n]
        send_sems, recv_sems = refs[n + 1], refs[n + 2]
        token = refs[-1]
        me = _my_index()
        for i in range(n):
            for m in masks:
                _gather_copy(g_in[i], i, m, send_sems, recv_sems, me).start()
        token[...] = jnp.zeros_like(token)

    outs = pl.pallas_call(
        body, name=name,
        out_shape=(*_sem_pairs(n), *[pltpu.HBM(g.shape, g.dtype) for g in gs],
                   jax.ShapeDtypeStruct((8, 128), F32)),
        in_specs=[HBM_SPEC] * n + [ANY_SPEC],
        out_specs=(SEM_SPEC, SEM_SPEC, *[HBM_SPEC] * n, VMEM_SPEC),
        input_output_aliases={i: 2 + i for i in range(n)},
        compiler_params=pltpu.CompilerParams(has_side_effects=SIDE_EFFECT),
    )(*[_hbm(g) for g in gs], after)
    return outs[0], outs[1], list(outs[2:2 + n]), outs[-1]


def gather_start_groups(groups, after, name, masks=GATHER_MASKS):
    sizes = [len(g) for g in groups]
    flat = [a for g in groups for a in g]
    n, ng = len(flat), len(groups)

    def body(*refs):
        g_in = refs[:n]
        sems = refs[n + 1:n + 1 + 2 * ng]
        token = refs[-1]
        me = _my_index()
        pos = 0
        for k, size in enumerate(sizes):
            for i in range(size):
                for m in masks:
                    _gather_copy(g_in[pos + i], i, m, sems[2 * k], sems[2 * k + 1], me).start()
            pos += size
        token[...] = jnp.zeros_like(token)

    outs = pl.pallas_call(
        body, name=name,
        out_shape=(*[s for size in sizes for s in _sem_pairs(size)],
                   *[pltpu.HBM(g.shape, g.dtype) for g in flat], jax.ShapeDtypeStruct((8, 128), F32)),
        in_specs=[HBM_SPEC] * n + [ANY_SPEC],
        out_specs=(*[SEM_SPEC] * (2 * ng), *[HBM_SPEC] * n, VMEM_SPEC),
        input_output_aliases={i: 2 * ng + i for i in range(n)},
        compiler_params=pltpu.CompilerParams(has_side_effects=SIDE_EFFECT),
    )(*[_hbm(g) for g in flat], after)
    result, pos = [], 2 * ng
    for k, size in enumerate(sizes):
        result.append((outs[2 * k], outs[2 * k + 1], list(outs[pos:pos + size])))
        pos += size
    return result, outs[-1]


def gather_wait(gs, send_sems, recv_sems, after, name, masks=GATHER_MASKS):
    n = len(gs)

    def body(*refs):
        g_in = refs[:n]
        send, recv = refs[n], refs[n + 1]
        me = _my_index()
        for i in range(n):
            for m in masks:
                _, pidx = _peer(m)
                _gather_copy(g_in[i], i, m, send, recv, me).wait_send()
                _gather_copy(g_in[i], i, m, send, recv, pidx).wait_recv()

    outs = pl.pallas_call(
        body, name=name,
        out_shape=[pltpu.HBM(g.shape, g.dtype) for g in gs],
        in_specs=[HBM_SPEC] * n + [SEM_SPEC, SEM_SPEC, ANY_SPEC],
        out_specs=[HBM_SPEC] * n,
        input_output_aliases={i: i for i in range(n)},
        compiler_params=pltpu.CompilerParams(has_side_effects=SIDE_EFFECT),
    )(*gs, send_sems, recv_sems, after)
    return list(outs)


def sibling_forward(gs, name):
    n = len(gs)
    nf = len(FORWARD_MASKS)

    def body(*refs):
        g_in = refs[:n]
        send_sems, recv_sems = refs[2 * n:]
        x, y, c = lax.axis_index("x"), lax.axis_index("y"), lax.axis_index("c")
        sibling = (x, y, 1 - c)

        def copy(i, k, origin):
            return pltpu.make_async_remote_copy(
                src_ref=g_in[i].at[origin], dst_ref=g_in[i].at[origin],
                send_sem=send_sems.at[i * nf + k], recv_sem=recv_sems.at[i * nf + k],
                device_id=sibling, device_id_type=MESH)
        sends = []
        for i in range(n):
            for k, m in enumerate(FORWARD_MASKS):
                _, origin = _peer(m)
                cp = copy(i, k, origin)
                cp.start()
                sends.append(cp)
        for i in range(n):
            for k, m in enumerate(FORWARD_MASKS):
                _, origin = _peer(m ^ 1)
                copy(i, k, origin).wait_recv()
        for cp in sends:
            cp.wait_send()

    outs = pl.pallas_call(
        body, name=name,
        out_shape=[jax.ShapeDtypeStruct(g.shape, g.dtype) for g in gs],
        in_specs=[HBM_SPEC] * n, out_specs=[HBM_SPEC] * n,
        input_output_aliases={i: i for i in range(n)},
        scratch_shapes=[pltpu.SemaphoreType.DMA((n * nf,)), pltpu.SemaphoreType.DMA((n * nf,))],
    )(*gs)
    return list(outs)


def _scatter_copy(g_ref, l_ref, i, m, send_sems, recv_sems):
    peer, pidx = _peer(m)
    return pltpu.make_async_remote_copy(
        src_ref=g_ref.at[pidx], dst_ref=l_ref.at[m - 1],
        send_sem=send_sems.at[_sem(i, m)], recv_sem=recv_sems.at[_sem(i, m)],
        device_id=peer, device_id_type=MESH)


def scatter_start(grads, after, name):
    n = len(grads)
    lands = [lax.empty((N_DEV - 1,) + g.shape[1:], g.dtype) for g in grads]

    def body(*refs):
        g_in, l_in = refs[:n], refs[n:2 * n]
        send_sems, recv_sems = refs[2 * n + 1], refs[2 * n + 2]
        token = refs[-1]
        for i in range(n):
            for m in range(1, N_DEV):
                _scatter_copy(g_in[i], l_in[i], i, m, send_sems, recv_sems).start()
        token[...] = jnp.zeros_like(token)

    outs = pl.pallas_call(
        body, name=name,
        out_shape=(*_sem_pairs(n), *[pltpu.HBM(g.shape, g.dtype) for g in grads],
                   *[pltpu.HBM(l.shape, l.dtype) for l in lands], jax.ShapeDtypeStruct((8, 128), F32)),
        in_specs=[HBM_SPEC] * (2 * n) + [ANY_SPEC],
        out_specs=(SEM_SPEC, SEM_SPEC, *[HBM_SPEC] * (2 * n), VMEM_SPEC),
        input_output_aliases={i: 2 + i for i in range(2 * n)},
        compiler_params=pltpu.CompilerParams(has_side_effects=SIDE_EFFECT),
    )(*[_hbm(g) for g in grads], *[_hbm(l) for l in lands], after)
    return outs[0], outs[1], list(outs[2:2 + n]), list(outs[2 + n:2 + 2 * n]), outs[-1]


def scatter_wait(grads, lands, send_sems, recv_sems, after, name):
    n = len(grads)

    def body(*refs):
        g_in, l_in = refs[:n], refs[n:2 * n]
        send, recv = refs[2 * n], refs[2 * n + 1]
        for i in range(n):
            for m in range(1, N_DEV):
                cp = _scatter_copy(g_in[i], l_in[i], i, m, send, recv)
                cp.wait_send()
                cp.wait_recv()

    outs = pl.pallas_call(
        body, name=name,
        out_shape=[pltpu.HBM(a.shape, a.dtype) for a in list(grads) + list(lands)],
        in_specs=[HBM_SPEC] * (2 * n) + [SEM_SPEC, SEM_SPEC, ANY_SPEC],
        out_specs=[HBM_SPEC] * (2 * n),
        input_output_aliases={i: i for i in range(2 * n)},
        compiler_params=pltpu.CompilerParams(has_side_effects=SIDE_EFFECT),
    )(*grads, *lands, send_sems, recv_sems, after)
    return list(outs[:n]), list(outs[n:])


def sum_slots(g, name):
    _, r, c = g.shape

    def body(g_ref, out_ref):
        acc = g_ref[0]
        for p in range(1, N_DEV):
            acc = acc + g_ref[p]
        out_ref[...] = acc

    return pl.pallas_call(
        body, name=name, out_shape=jax.ShapeDtypeStruct((r, c), F32),
        in_specs=[VMEM_SPEC], out_specs=VMEM_SPEC,
        compiler_params=pltpu.CompilerParams(vmem_limit_bytes=VMEM_LIMIT),
    )(g)


def _sigmoid(v):
    return 1.0 / (1.0 + jnp.exp(-v))


def _rms_fwd(xf, g):
    r = lax.rsqrt(jnp.mean(xf * xf, axis=-1, keepdims=True) + EPS)
    return xf * r, r


def _rms_bwd(xhat, r, g, dy):
    dg = jnp.sum(dy * xhat, axis=0, keepdims=True)
    dxh = dy * g
    dx = r * (dxh - xhat * jnp.mean(dxh * xhat, axis=-1, keepdims=True))
    return dx, dg


def _ln_stats(v):
    mu = jnp.mean(v, axis=-1, keepdims=True)
    vc = v - mu
    r = lax.rsqrt(jnp.mean(vc * vc, axis=-1, keepdims=True) + EPS)
    return vc * r, r


def _ln_bwd(xhat, r, dxh):
    return r * (dxh - jnp.mean(dxh, axis=-1, keepdims=True)
                - xhat * jnp.mean(dxh * xhat, axis=-1, keepdims=True))


def _dot(a, b):
    return jnp.dot(a, b, preferred_element_type=F32)


def _dot_nt(a, b):
    return lax.dot_general(a, b, (((1,), (1,)), ((), ())), preferred_element_type=F32)


def _dot_tn(a, b):
    return lax.dot_general(a, b, (((0,), (0,)), ((), ())), preferred_element_type=F32)


def _full_weight(w_ref, kind):
    assert kind == "row"
    p, a, b = w_ref.shape
    return w_ref[...].reshape(p * a, b)


def _wspec(wg):
    return pl.BlockSpec(wg.shape, lambda *_: (0, 0, 0))


def mm_rows(a, wg, kind, *, gain=None, residual=None, out_dtype=F32, name, tm=None):
    m, k = a.shape
    p, wa, wb = wg.shape
    n = p * wb if kind == "col" else wb
    tm = _row_tile(m, tm)
    has_gain, has_res = gain is not None, residual is not None

    def body(*refs):
        refs = list(refs)
        a_ref = refs.pop(0)
        g_ref = refs.pop(0) if has_gain else None
        w_ref = refs.pop(0)
        r_ref = refs.pop(0) if has_res else None
        o_ref = refs.pop(0)
        if has_gain:
            xhat, _ = _rms_fwd(a_ref[...].astype(F32), None)
            h = (xhat * g_ref[...]).astype(BF16)
        else:
            h = a_ref[...].astype(BF16)
        if kind == "col":
            for j in range(p):
                o = _dot(h, w_ref[j])
                if has_res:
                    o = o + r_ref[:, j * wb:(j + 1) * wb]
                o_ref[:, j * wb:(j + 1) * wb] = o.astype(out_dtype)
        else:
            o = _dot(h, _full_weight(w_ref, "row"))
            if has_res:
                o = o + r_ref[...]
            o_ref[...] = o.astype(out_dtype)

    operands = [a]
    in_specs = [pl.BlockSpec((tm, k), lambda i: (i, 0))]
    if has_gain:
        operands.append(gain.reshape(1, k))
        in_specs.append(pl.BlockSpec((1, k), lambda i: (0, 0)))
    operands.append(wg)
    in_specs.append(_wspec(wg))
    if has_res:
        operands.append(residual)
        in_specs.append(pl.BlockSpec((tm, n), lambda i: (i, 0)))
    return pl.pallas_call(
        body, name=name, grid=(m // tm,),
        out_shape=jax.ShapeDtypeStruct((m, n), out_dtype),
        in_specs=in_specs, out_specs=pl.BlockSpec((tm, n), lambda i: (i, 0)),
        compiler_params=_params("parallel"),
    )(*operands)


def mm_nt(dz, wg, kind, *, x=None, gain=None, dx_in=None, after=None, name, tm=None):
    m, n = dz.shape
    p, wa, wb = wg.shape
    k = wa if kind == "col" else p * wa
    tm = _row_tile(m, tm)
    epi = x is not None
    has_dx = dx_in is not None
    has_after = after is not None

    def body(*refs):
        refs = list(refs)
        dz_ref, w_ref = refs.pop(0), refs.pop(0)
        if epi:
            x_ref, g_ref = refs.pop(0), refs.pop(0)
            dxi_ref = refs.pop(0) if has_dx else None
        if has_after:
            refs.pop(0)
        if epi:
            dx_ref, h_ref, dg_ref = refs
        else:
            (da_ref,) = refs
        dzb = dz_ref[...].astype(BF16)
        if kind == "col":
            da = _dot_nt(dzb[:, 0:wb], w_ref[0])
            for j in range(1, p):
                da = da + _dot_nt(dzb[:, j * wb:(j + 1) * wb], w_ref[j])
        else:
            da = _dot_nt(dzb, _full_weight(w_ref, "row"))
        if not epi:
            da_ref[...] = da
            return
        g = g_ref[...]
        xhat, r = _rms_fwd(x_ref[...].astype(F32), None)
        h_ref[...] = (xhat * g).astype(BF16)
        dx, dg = _rms_bwd(xhat, r, g, da)
        if has_dx:
            dx = dx + dxi_ref[...]
        dx_ref[...] = dx

        @pl.when(pl.program_id(0) == 0)
        def _():
            dg_ref[...] = jnp.zeros_like(dg_ref)
        dg_ref[...] += dg

    row = lambda i: (i, 0)
    operands = [dz, wg]
    in_specs = [pl.BlockSpec((tm, n), row), _wspec(wg)]
    if epi:
        operands += [x, gain.reshape(1, k)]
        in_specs += [pl.BlockSpec((tm, k), row), pl.BlockSpec((1, k), lambda i: (0, 0))]
        if has_dx:
            operands.append(dx_in)
            in_specs.append(pl.BlockSpec((tm, k), row))
        out_shape = [jax.ShapeDtypeStruct((m, k), F32), jax.ShapeDtypeStruct((m, k), BF16),
                     jax.ShapeDtypeStruct((1, k), F32)]
        out_specs = [pl.BlockSpec((tm, k), row), pl.BlockSpec((tm, k), row),
                     pl.BlockSpec((1, k), lambda i: (0, 0))]
    else:
        out_shape = jax.ShapeDtypeStruct((m, k), F32)
        out_specs = pl.BlockSpec((tm, k), row)
    if has_after:
        operands.append(after)
        in_specs.append(ANY_SPEC)
    return pl.pallas_call(
        body, name=name, grid=(m // tm,), out_shape=out_shape,
        in_specs=in_specs, out_specs=out_specs,
        compiler_params=_params("arbitrary"),
    )(*operands)


def mm_tn(a, b, *, nb, a_spec, b_spec, ka, nbk, tm, m, scale=1.0, out_dtype=BF16, col_slots=1,
          after=None, name):
    ni = m // tm
    assert col_slots == 1 or nb == 1
    cw = nbk // col_slots
    extra = [] if after is None else [after]

    def body(a_ref, b_ref, *rest):
        o_ref, acc = rest[len(extra):]
        i = pl.program_id(1)

        @pl.when(i == 0)
        def _():
            acc[...] = jnp.zeros_like(acc)
        acc[...] += _dot_tn(a_ref[...].astype(BF16), b_ref[...].astype(BF16))

        @pl.when(i == ni - 1)
        def _():
            if col_slots == 1:
                o_ref[...] = (acc[...] * scale).astype(out_dtype)
            else:
                for j in range(col_slots):
                    o_ref[j] = (acc[:, j * cw:(j + 1) * cw] * scale).astype(out_dtype)

    if col_slots == 1:
        out_shape = jax.ShapeDtypeStruct((nb, ka, nbk), out_dtype)
        out_spec = pl.BlockSpec((None, ka, nbk), lambda s, i: (s, 0, 0))
    else:
        out_shape = jax.ShapeDtypeStruct((col_slots, ka, cw), out_dtype)
        out_spec = pl.BlockSpec((col_slots, ka, cw), lambda s, i: (0, 0, 0))
    return pl.pallas_call(
        body, name=name, grid=(nb, ni), out_shape=out_shape,
        in_specs=[a_spec, b_spec] + [ANY_SPEC] * len(extra), out_specs=out_spec,
        scratch_shapes=[pltpu.VMEM((ka, nbk), F32)],
        compiler_params=_params("parallel", "arbitrary"),
    )(a, b, *extra)


def _ffn_specs(w_in_g, w_out_g, d):
    nf = w_in_g.shape[1]
    hr = w_out_g.shape[1]
    assert 2 * hr == nf
    w_in5 = w_in_g.reshape(2, 4, nf, d)
    w_out5 = w_out_g.reshape(4, 2, hr, d)
    in_spec = pl.BlockSpec((2, None, nf, d), lambda i, j: (0, j, 0, 0))
    out_spec = pl.BlockSpec((None, 2, hr, d), lambda i, j: (j, 0, 0, 0))
    return w_in5, w_out5, in_spec, out_spec, nf


def ffn_fwd(x, gain, w_in_g, w_out_g, *, name, tm=None):
    t, d = x.shape
    tm = _row_tile(t, tm)
    w_in5, w_out5, wi_spec, wo_spec, nf = _ffn_specs(w_in_g, w_out_g, d)

    def body(x_ref, g_ref, wi_ref, wo_ref, o_ref, gu_ref, h_scr, acc):
        j = pl.program_id(1)

        @pl.when(j == 0)
        def _():
            xhat, _ = _rms_fwd(x_ref[...], None)
            h_scr[...] = (xhat * g_ref[...]).astype(BF16)
            acc[...] = jnp.zeros_like(acc)
        wo = wo_ref[...].reshape(nf, d)

        def project(rows):
            h = h_scr[rows]
            return _dot_nt(h, wi_ref[0]), _dot_nt(h, wi_ref[1])

        sub = tm // FFN_FWD_SPLIT
        parts = [slice(k * sub, (k + 1) * sub) for k in range(FFN_FWD_SPLIT)]
        gt, up = project(parts[0])
        for k, rows in enumerate(parts):
            if k + 1 < len(parts):
                nxt = project(parts[k + 1])
            gu_ref[0, rows] = gt.astype(BF16)
            gu_ref[1, rows] = up.astype(BF16)
            act = (gt * _sigmoid(gt) * up).astype(BF16)
            acc[rows] += _dot(act, wo)
            if k + 1 < len(parts):
                gt, up = nxt

        @pl.when(j == 3)
        def _():
            o_ref[...] = x_ref[...] + 0.5 * acc[...]

    return pl.pallas_call(
        body, name=name, grid=(t // tm, 4),
        out_shape=[jax.ShapeDtypeStruct((t, d), F32), jax.ShapeDtypeStruct((2, 4, t, nf), BF16)],
        in_specs=[pl.BlockSpec((tm, d), lambda i, j: (i, 0)),
                  pl.BlockSpec((1, d), lambda i, j: (0, 0)), wi_spec, wo_spec],
        out_specs=[pl.BlockSpec((tm, d), lambda i, j: (i, 0)),
                   pl.BlockSpec((2, None, tm, nf), lambda i, j: (0, j, i, 0))],
        scratch_shapes=[pltpu.VMEM((tm, d), BF16), pltpu.VMEM((tm, d), F32)],
        compiler_params=_params("parallel", "arbitrary"),
    )(x, gain.reshape(1, d), w_in5, w_out5)


def ffn_bwd_rows(x, dy, gu, gain, w_in_g, w_out_g, after, *, name, tm=None):
    t, d = x.shape
    tm = _row_tile(t, tm)
    w_in5, w_out5, wi_spec, wo_spec, nf = _ffn_specs(w_in_g, w_out_g, d)

    def body(x_ref, dy_ref, gu_ref, g_ref, wi_ref, wo_ref, after_ref, dx_ref, h_ref, act_ref, dgu_ref, dg_ref,
             dyh_scr, dh_acc):
        i, j = pl.program_id(0), pl.program_id(1)

        @pl.when(j == 0)
        def _():
            xhat, _ = _rms_fwd(x_ref[...], None)
            h_ref[...] = (xhat * g_ref[...]).astype(BF16)
            dyh_scr[...] = (0.5 * dy_ref[...]).astype(BF16)
            dh_acc[...] = jnp.zeros_like(dh_acc)
        wo = wo_ref[...].reshape(nf, d)

        def gates(rows):
            gt = gu_ref[0, rows].astype(F32)
            up = gu_ref[1, rows].astype(F32)
            sg = _sigmoid(gt)
            silu = gt * sg
            act_ref[rows] = (silu * up).astype(BF16)
            return up * (sg * (1.0 + gt * (1.0 - sg))), silu

        def grads(rows, dact, dsilu_up, silu):
            dgt = (dact * dsilu_up).astype(BF16)
            dup = (dact * silu).astype(BF16)
            dgu_ref[0, rows] = dgt
            dgu_ref[1, rows] = dup
            return dgt, dup

        sub = tm // FFN_BWD_SPLIT
        parts = [slice(k * sub, (k + 1) * sub) for k in range(FFN_BWD_SPLIT)]
        dact = _dot_nt(dyh_scr[parts[0]], wo)
        gate = gates(parts[0])
        for k, rows in enumerate(parts):
            if k + 1 < len(parts):
                dact_next = _dot_nt(dyh_scr[parts[k + 1]], wo)
            dgt, dup = grads(rows, dact, *gate)
            dh_acc[rows] += _dot(dgt, wi_ref[0]) + _dot(dup, wi_ref[1])
            if k + 1 < len(parts):
                gate = gates(parts[k + 1])
                dact = dact_next

        @pl.when(j == 3)
        def _():
            g = g_ref[...]
            xhat, r = _rms_fwd(x_ref[...], None)
            dx, dg = _rms_bwd(xhat, r, g, dh_acc[...])
            dx_ref[...] = dy_ref[...] + dx

            @pl.when(i == 0)
            def _():
                dg_ref[...] = jnp.zeros_like(dg_ref)
            dg_ref[...] += dg

    row = lambda i, j: (i, 0)
    return pl.pallas_call(
        body, name=name, grid=(t // tm, 4),
        out_shape=[jax.ShapeDtypeStruct((t, d), F32), jax.ShapeDtypeStruct((t, d), BF16),
                   jax.ShapeDtypeStruct((4, t, nf), BF16), jax.ShapeDtypeStruct((2, 4, t, nf), BF16),
                   jax.ShapeDtypeStruct((1, d), F32), jax.ShapeDtypeStruct((t, d), BF16)],
        in_specs=[pl.BlockSpec((tm, d), row), pl.BlockSpec((tm, d), row),
                  pl.BlockSpec((2, None, tm, nf), lambda i, j: (0, j, i, 0)),
                  pl.BlockSpec((1, d), lambda i, j: (0, 0)), wi_spec, wo_spec, ANY_SPEC],
        out_specs=[pl.BlockSpec((tm, d), row), pl.BlockSpec((tm, d), row),
                   pl.BlockSpec((None, tm, nf), lambda i, j: (j, i, 0)),
                   pl.BlockSpec((2, None, tm, nf), lambda i, j: (0, j, i, 0)),
                   pl.BlockSpec((1, d), lambda i, j: (0, 0)), pl.BlockSpec((tm, d), row)],
        scratch_shapes=[pltpu.VMEM((tm, d), F32)],
        compiler_params=_params("arbitrary", "arbitrary"),
    )(x, dy, gu, gain.reshape(1, d), w_in5, w_out5, after)


def ffn_grad_w_in(h, dgu, after, *, name):
    t, d = h.shape
    nf = dgu.shape[-1]
    tm = _row_tile(t, TN_TILE)
    return mm_tn(dgu.reshape(8, t, nf), h, nb=8, ka=nf, nbk=d, tm=tm, m=t, after=after,
                 a_spec=pl.BlockSpec((None, tm, nf), lambda s, i: (s, i, 0)),
                 b_spec=pl.BlockSpec((tm, d), lambda s, i: (i, 0)), name=name)


def ffn_grad_w_out(act, dyh, after, *, name):
    _, t, nf = act.shape
    d = dyh.shape[1]
    tm = _row_tile(t, TN_TILE)
    d_w_out = mm_tn(act, dyh, nb=4, ka=nf, nbk=d, tm=tm, m=t, after=after,
                    a_spec=pl.BlockSpec((None, tm, nf), lambda s, i: (s, i, 0)),
                    b_spec=pl.BlockSpec((tm, d), lambda s, i: (i, 0)), name=name)
    return d_w_out.reshape(8, nf // 2, d)


def _lane_group(shape):
    return lax.shift_right_logical(lax.broadcasted_iota(jnp.int32, shape, 1), 6)


def _pool_count(t0, rows):
    t = (t0 + lax.broadcasted_iota(jnp.int32, (rows, MIX_W), 0) + 1).astype(F32)
    return jnp.minimum(t, _by_group(_lane_group((rows, MIX_W)), 2.0, 4.0, 8.0, 16.0))


def _by_group(grp, v0, v1, v2, v3):
    return jnp.where(grp == 0, v0, jnp.where(grp == 1, v1, jnp.where(grp == 2, v2, v3)))


def _sgu_mix(wt_ref, vnc):
    grp = _lane_group((SGU_CHUNK, MIX_W))
    out = jnp.zeros((SGU_CHUNK, MIX_W), F32)
    for hd in range(N_HEADS):
        out = jnp.where(grp == hd, _dot(wt_ref[hd], vnc), out)
    return out


def _pool_fwd(s1, s2, s3, t0, ts, lo):
    h = lo
    s2[h - 24:h + ts] = s1[h - 24:h + ts] + s1[h - 25:h + ts - 1]
    s3[h - 16:h + ts] = s2[h - 16:h + ts] + s2[h - 18:h + ts - 2]
    sum2 = s2[h:h + ts]
    sum4 = s3[h:h + ts]
    s2[h - 8:h + ts] = s3[h - 8:h + ts] + s3[h - 12:h + ts - 4]
    sum8 = s2[h:h + ts]
    sum16 = sum8 + s2[h - 8:h + ts - 8]
    grp = _lane_group((ts, MIX_W))
    return _by_group(grp, sum2, sum4, sum8, sum16) / _pool_count(t0, ts) - s1[h:h + ts]


def _make_shifts(src, sh, rows):
    for b in range(1, 8):
        sh[b, 0:rows] = src[b:b + rows]


def _rows_at(src, sh, start, n):
    a, b = divmod(start, 8)
    return src[8 * a:8 * a + n] if b == 0 else sh[b, 8 * a:8 * a + n]


def mixer_fwd(z, sconv, cconv, vecs, wt, bexp, pbd, x_res, wmo_g, *, name, ts=None):
    t = z.shape[0]
    ts = _row_tile(t, MIX_TILE if ts is None else ts)
    hl = HALO
    w = MIX_W
    nch = ts // SGU_CHUNK

    def body(zc, zp, sconv_ref, cconv_ref, vec_ref, wt_ref, bexp_ref, pbd_ref, xr_ref, wmo_ref,
             y_ref, xo_ref, s1, s2, s3, sh):
        i = pl.program_id(0)
        has_prev = i > 0

        def col(ref, c):
            return ref[:, c * w:(c + 1) * w]

        def prev(c):
            return jnp.where(has_prev, col(zp, c), 0.0)

        s1[0:hl] = prev(1) * prev(2)
        s1[hl:hl + ts] = col(zc, 1) * col(zc, 2)
        cv = sconv_ref[0:1] * s1[hl - 2:hl - 2 + ts]
        for k in range(1, SCONV_K):
            cv = cv + sconv_ref[k:k + 1] * s1[hl - 2 + k:hl - 2 + k + ts]
        y_ref[:, 0:w] = (col(zc, 0) * cv).astype(BF16)

        xhat, _ = _ln_stats(col(zc, 4))
        vn = (xhat * vec_ref[0:1]).astype(BF16)
        for c in range(nch):
            rows = slice(c * SGU_CHUNK, (c + 1) * SGU_CHUNK)
            mixed = _sgu_mix(wt_ref, vn[rows]) + bexp_ref[...]
            y_ref[rows, w:2 * w] = (zc[rows, 3 * w:4 * w] * mixed).astype(BF16)

        s1[0:hl] = prev(5) * _sigmoid(prev(6))
        s1[hl:hl + ts] = col(zc, 5) * _sigmoid(col(zc, 6))
        off = hl - (CCONV_K - 1)
        _make_shifts(s1, sh, hl + ts - 8)
        cv = cconv_ref[0:1] * _rows_at(s1, sh, off, ts)
        for k in range(1, CCONV_K):
            cv = cv + cconv_ref[k:k + 1] * _rows_at(s1, sh, off + k, ts)
        xhat, _ = _ln_stats(cv)
        ln = xhat * vec_ref[1:2] + vec_ref[2:3]
        y_ref[:, 2 * w:3 * w] = (ln * _sigmoid(ln)).astype(BF16)

        s1[0:hl] = prev(7)
        s1[hl:hl + ts] = col(zc, 7)
        pooled = _pool_fwd(s1, s2, s3, i * ts, ts, hl)
        y_ref[:, 3 * w:4 * w] = (_dot(pooled.astype(BF16), pbd_ref[...]) * vec_ref[3:4]).astype(BF16)

        xo_ref[...] = xr_ref[...] + _dot(y_ref[...], _full_weight(wmo_ref, "row"))

    full = lambda shape: pl.BlockSpec(shape, lambda i: (0,) * len(shape))
    row = lambda i: (i, 0)
    return pl.pallas_call(
        body, name=name, grid=(t // ts,),
        out_shape=[jax.ShapeDtypeStruct((t, 4 * w), BF16), jax.ShapeDtypeStruct((t, 4 * w), F32)],
        in_specs=[pl.BlockSpec((ts, 8 * w), row),
                  pl.BlockSpec((hl, 8 * w), lambda i: (jnp.maximum(i * (ts // hl) - 1, 0), 0)),
                  full((8, w)), full((32, w)), full((8, w)), full((N_HEADS, SGU_CHUNK, SGU_CHUNK)),
                  full((SGU_CHUNK, w)), full((w, w)), pl.BlockSpec((ts, 4 * w), row), _wspec(wmo_g)],
        out_specs=[pl.BlockSpec((ts, 4 * w), row), pl.BlockSpec((ts, 4 * w), row)],
        scratch_shapes=[pltpu.VMEM((hl + ts, w), F32)] * 3 + [pltpu.VMEM((8, hl + ts, w), F32)],
        compiler_params=_params("parallel"),
    )(z, z, sconv, cconv, vecs, wt, bexp, pbd, x_res, wmo_g)


def mixer_bwd(z, dx, wmo_g, sconv, cconv, vecs, wt, bexp, pbd, *, name, ts=None):
    t = z.shape[0]
    ts = _row_tile(t, MIX_TILE if ts is None else ts)
    hl = HALO
    w = MIX_W
    nch = ts // SGU_CHUNK
    ni = t // ts
    ext = ts + hl

    def body(zc, zp, zn, dxc, dxn_, wmo_ref, sconv_ref, cconv_ref, vec_ref, wt_ref, bexp_ref, pbd_ref,
             dz_ref, gvec_ref, gcc_ref, gwt_ref, gb_ref, gpbd_ref, s1, s2, s3, sh1, sh3, dyc, dyn):
        i = pl.program_id(0)
        has_prev = i > 0
        has_next = i < ni - 1
        wmo = _full_weight(wmo_ref, "row")
        dyc[...] = _dot_nt(dxc[...].astype(BF16), wmo)
        dyn[...] = _dot_nt(dxn_[...].astype(BF16), wmo)

        @pl.when(i == 0)
        def _():
            gvec_ref[...] = jnp.zeros_like(gvec_ref)
            gcc_ref[...] = jnp.zeros_like(gcc_ref)
            gwt_ref[...] = jnp.zeros_like(gwt_ref)
            gb_ref[...] = jnp.zeros_like(gb_ref)
            gpbd_ref[...] = jnp.zeros_like(gpbd_ref)

        def col(ref, c):
            return ref[:, c * w:(c + 1) * w]

        def prev(c):
            return jnp.where(has_prev, col(zp, c), 0.0)

        def nxt(c):
            return jnp.where(has_next, col(zn, c), 0.0)

        def dnext(c):
            return jnp.where(has_next, col(dyn, c), 0.0)

        def rowsum(v):
            return jnp.sum(v, axis=0, keepdims=True)

        s1[0:hl] = prev(1) * prev(2)
        s1[hl:hl + ts] = col(zc, 1) * col(zc, 2)
        s1[hl + ts:hl + ts + hl] = nxt(1) * nxt(2)
        cv = sconv_ref[0:1] * s1[hl - 2:hl - 2 + ts]
        for k in range(1, SCONV_K):
            cv = cv + sconv_ref[k:k + 1] * s1[hl - 2 + k:hl - 2 + k + ts]
        dya = col(dyc, 0)
        dz_ref[:, 0:w] = (dya * cv).astype(BF16)
        s2[0:ts] = dya * col(zc, 0)
        s2[ts:ext] = dnext(0) * nxt(0)
        dv = sconv_ref[0:1] * s2[2:2 + ts]
        for k in range(1, SCONV_K):
            dv = dv + sconv_ref[k:k + 1] * s2[2 - k:2 - k + ts]
        dz_ref[:, w:2 * w] = (dv * col(zc, 2)).astype(BF16)
        dz_ref[:, 2 * w:3 * w] = (dv * col(zc, 1)).astype(BF16)
        dcv = s2[0:ts]
        for k in range(SCONV_K):
            gvec_ref[k:k + 1] += rowsum(dcv * s1[hl - 2 + k:hl - 2 + k + ts])

        g_sgu = vec_ref[0:1]
        xhat, rstd = _ln_stats(col(zc, 4))
        vn = (xhat * g_sgu).astype(BF16)
        grp = _lane_group((SGU_CHUNK, w))
        lane = lax.broadcasted_iota(jnp.int32, (SGU_CHUNK, SGU_CHUNK), 1)
        tril = lax.broadcasted_iota(jnp.int32, (SGU_CHUNK, SGU_CHUNK), 0) >= lane
        for c in range(nch):
            rows = slice(c * SGU_CHUNK, (c + 1) * SGU_CHUNK)
            vnc = vn[rows]
            mixed = _sgu_mix(wt_ref, vnc) + bexp_ref[...]
            dyb = dyc[rows, w:2 * w]
            dz_ref[rows, 3 * w:4 * w] = (dyb * mixed).astype(BF16)
            dmix = dyb * zc[rows, 3 * w:4 * w]
            dmixb = dmix.astype(BF16)
            dvn = jnp.zeros((SGU_CHUNK, w), F32)
            gb = jnp.zeros((SGU_CHUNK, SGU_CHUNK), F32)
            for hd in range(N_HEADS):
                dvn = jnp.where(grp == hd, _dot_tn(wt_ref[hd], dmixb), dvn)
                dm_h = jnp.where(grp == hd, dmix, 0.0)
                gwt_ref[hd] += jnp.where(tril, _dot_nt(dm_h.astype(BF16), vnc), 0.0)
                gb = gb + jnp.where(lane == hd, jnp.sum(dm_h, axis=1, keepdims=True), 0.0)
            gb_ref[...] += gb
            s3[rows] = dvn
        dvn = s3[0:ts]
        gvec_ref[3:4] += rowsum(dvn * xhat)
        dz_ref[:, 4 * w:5 * w] = _ln_bwd(xhat, rstd, dvn * g_sgu).astype(BF16)

        sig_c = _sigmoid(col(zc, 6))
        s1[0:hl] = prev(5) * _sigmoid(prev(6))
        s1[hl:hl + ts] = col(zc, 5) * sig_c
        s1[hl + ts:hl + ts + hl] = nxt(5) * _sigmoid(nxt(6))
        off = hl - (CCONV_K - 1)
        _make_shifts(s1, sh1, ts + 2 * hl - 8)
        cv = cconv_ref[0:1] * _rows_at(s1, sh1, off, ext)
        for k in range(1, CCONV_K):
            cv = cv + cconv_ref[k:k + 1] * _rows_at(s1, sh1, off + k, ext)
        xhat, rstd = _ln_stats(cv)
        ln = xhat * vec_ref[1:2] + vec_ref[2:3]
        sg = _sigmoid(ln)
        s2[0:ts] = col(dyc, 2)
        s2[ts:ext] = dnext(2)
        dln = s2[0:ext] * (sg * (1.0 + ln * (1.0 - sg)))
        gvec_ref[4:5] += rowsum(dln[0:ts] * xhat[0:ts])
        gvec_ref[5:6] += rowsum(dln[0:ts])
        s3[0:ext] = _ln_bwd(xhat, rstd, dln * vec_ref[1:2])
        _make_shifts(s3, sh3, ext - 8)
        dyg = cconv_ref[0:1] * _rows_at(s3, sh3, CCONV_K - 1, ts)
        for k in range(1, CCONV_K):
            dyg = dyg + cconv_ref[k:k + 1] * _rows_at(s3, sh3, CCONV_K - 1 - k, ts)
        dz_ref[:, 5 * w:6 * w] = (dyg * sig_c).astype(BF16)
        dz_ref[:, 6 * w:7 * w] = (dyg * col(zc, 5) * sig_c * (1.0 - sig_c)).astype(BF16)
        dcv = s3[0:ts]
        for k in range(CCONV_K):
            gcc_ref[k:k + 1] += rowsum(dcv * _rows_at(s1, sh1, off + k, ts))

        scale = vec_ref[3:4]
        s1[0:hl] = prev(7)
        s1[hl:hl + ts] = col(zc, 7)
        pooled = _pool_fwd(s1, s2, s3, i * ts, ts, hl).astype(BF16)
        q0 = _dot(pooled, pbd_ref[...])
        dyd = col(dyc, 3)
        gvec_ref[6:7] += rowsum(dyd * q0)
        dq = (dyd * scale).astype(BF16)
        gpbd_ref[...] += _dot_tn(pooled, dq)
        s1[0:ts] = _dot_nt(dq, pbd_ref[...])
        s1[ts:ext] = _dot_nt((dnext(3) * scale).astype(BF16), pbd_ref[...])
        dpool = s1[0:ts]
        s2[0:ext] = s1[0:ext] / _pool_count(i * ts, ext)
        s3[0:ts + 24] = s2[0:ts + 24] + s2[1:ts + 25]
        f2 = s3[0:ts]
        s2[0:ts + 16] = s3[0:ts + 16] + s3[2:ts + 18]
        f4 = s2[0:ts]
        s3[0:ts + 8] = s2[0:ts + 8] + s2[4:ts + 12]
        f8 = s3[0:ts]
        f16 = f8 + s3[8:ts + 8]
        dz_ref[:, 7 * w:8 * w] = (_by_group(_lane_group((ts, w)), f2, f4, f8, f16) - dpool).astype(BF16)

    full = lambda shape: pl.BlockSpec(shape, lambda i: (0,) * len(shape))
    r = ts // hl
    prev_map = lambda i: (jnp.maximum(i * r - 1, 0), 0)
    next_map = lambda i: (jnp.minimum((i + 1) * r, t // hl - 1), 0)
    return pl.pallas_call(
        body, name=name, grid=(ni,),
        out_shape=[jax.ShapeDtypeStruct((t, 8 * w), BF16), jax.ShapeDtypeStruct((8, w), F32),
                   jax.ShapeDtypeStruct((32, w), F32),
                   jax.ShapeDtypeStruct((N_HEADS, SGU_CHUNK, SGU_CHUNK), F32),
                   jax.ShapeDtypeStruct((SGU_CHUNK, SGU_CHUNK), F32), jax.ShapeDtypeStruct((w, w), F32)],
        in_specs=[pl.BlockSpec((ts, 8 * w), lambda i: (i, 0)),
                  pl.BlockSpec((hl, 8 * w), prev_map), pl.BlockSpec((hl, 8 * w), next_map),
                  pl.BlockSpec((ts, 4 * w), lambda i: (i, 0)), pl.BlockSpec((hl, 4 * w), next_map),
                  _wspec(wmo_g),
                  full((8, w)), full((32, w)), full((8, w)), full((N_HEADS, SGU_CHUNK, SGU_CHUNK)),
                  full((SGU_CHUNK, w)), full((w, w))],
        out_specs=[pl.BlockSpec((ts, 8 * w), lambda i: (i, 0)), full((8, w)), full((32, w)),
                   full((N_HEADS, SGU_CHUNK, SGU_CHUNK)), full((SGU_CHUNK, SGU_CHUNK)), full((w, w))],
        scratch_shapes=[pltpu.VMEM((ts + 2 * hl, w), F32)] * 3 + [pltpu.VMEM((8, ts + 2 * hl, w), F32)] * 2
        + [pltpu.VMEM((ts, 4 * w), F32), pltpu.VMEM((hl, 4 * w), F32)],
        compiler_params=_params("arbitrary"),
    )(z, z, z, dx, dx, wmo_g, sconv, cconv, vecs, wt, bexp, pbd)


def _attn_head(q, kv_ref, hd, d):
    hw = d // N_HEADS
    qh = q[:, hd * hw:(hd + 1) * hw]
    kh = kv_ref[:, hd * hw:(hd + 1) * hw].astype(BF16)
    vh = kv_ref[:, d + hd * hw:d + (hd + 1) * hw].astype(BF16)
    s = _dot_nt(qh, kh) * (1.0 / (hw ** 0.5))
    e = jnp.exp(s - jnp.max(s, axis=-1, keepdims=True))
    p = e / jnp.sum(e, axis=-1, keepdims=True)
    return qh, kh, vh, p


def xattn_fwd(x, gain, kv, wq_g, wo_g, *, name, tm=None):
    t, d = x.shape
    nm = kv.shape[0]
    tm = _row_tile(t, tm)
    hw = d // N_HEADS

    def body(x_ref, g_ref, kv_ref, wq_ref, wo_ref, o_ref):
        xv = x_ref[...]
        xhat, _ = _rms_fwd(xv, None)
        h = (xhat * g_ref[...]).astype(BF16)
        q = _dot(h, _full_weight(wq_ref, "row")).astype(BF16)
        wo = _full_weight(wo_ref, "row")
        out = xv
        for hd in range(N_HEADS):
            _, _, vh, p = _attn_head(q, kv_ref, hd, d)
            oh = _dot(p.astype(BF16), vh).astype(BF16)
            out = out + _dot(oh, wo[hd * hw:(hd + 1) * hw])
        o_ref[...] = out

    row = lambda i: (i, 0)
    return pl.pallas_call(
        body, name=name, grid=(t // tm,),
        out_shape=jax.ShapeDtypeStruct((t, d), F32),
        in_specs=[pl.BlockSpec((tm, d), row), pl.BlockSpec((1, d), lambda i: (0, 0)),
                  pl.BlockSpec((nm, 2 * d), lambda i: (0, 0)), _wspec(wq_g), _wspec(wo_g)],
        out_specs=pl.BlockSpec((tm, d), row),
        compiler_params=_params("parallel"),
    )(x, gain.reshape(1, d), kv, wq_g, wo_g)


def xattn_bwd_rows(x, dxn, gain, kv, wq_g, wo_g, after, *, name, tm=None):
    t, d = x.shape
    nm = kv.shape[0]
    tm = _row_tile(t, tm)
    hw = d // N_HEADS

    def body(x_ref, dxn_ref, g_ref, kv_ref, wq_ref, wo_ref, after_ref,
             dx_ref, h_ref, dq_ref, o_ref, dkv_ref, dg_ref):
        i = pl.program_id(0)

        @pl.when(i == 0)
        def _():
            dkv_ref[...] = jnp.zeros_like(dkv_ref)
            dg_ref[...] = jnp.zeros_like(dg_ref)
        g = g_ref[...]
        xhat, r = _rms_fwd(x_ref[...], None)
        h = (xhat * g).astype(BF16)
        h_ref[...] = h
        wq = _full_weight(wq_ref, "row")
        q = _dot(h, wq).astype(BF16)
        dxn = dxn_ref[...]
        do = _dot_nt(dxn.astype(BF16), _full_weight(wo_ref, "row")).astype(BF16)
        for hd in range(N_HEADS):
            cols = slice(hd * hw, (hd + 1) * hw)
            qh, kh, vh, p = _attn_head(q, kv_ref, hd, d)
            pb = p.astype(BF16)
            o_ref[:, cols] = _dot(pb, vh).astype(BF16)
            doh = do[:, cols]
            dkv_ref[:, d + hd * hw:d + (hd + 1) * hw] += _dot_tn(pb, doh)
            dp = _dot_nt(doh, vh)
            ds = (p * (dp - jnp.sum(dp * p, axis=-1, keepdims=True)) * (1.0 / (hw ** 0.5))).astype(BF16)
            dq_ref[:, cols] = _dot(ds, kh).astype(BF16)
            dkv_ref[:, cols] += _dot_tn(ds, qh)
        dh = _dot_nt(dq_ref[...], wq)
        dx, dg = _rms_bwd(xhat, r, g, dh)
        dx_ref[...] = dxn + dx
        dg_ref[...] += dg

    row = lambda i: (i, 0)
    fix = lambda i: (0, 0)
    return pl.pallas_call(
        body, name=name, grid=(t // tm,),
        out_shape=[jax.ShapeDtypeStruct((t, d), F32), jax.ShapeDtypeStruct((t, d), BF16),
                   jax.ShapeDtypeStruct((t, d), BF16), jax.ShapeDtypeStruct((t, d), BF16),
                   jax.ShapeDtypeStruct((nm, 2 * d), F32), jax.ShapeDtypeStruct((1, d), F32)],
        in_specs=[pl.BlockSpec((tm, d), row), pl.BlockSpec((tm, d), row), pl.BlockSpec((1, d), fix),
                  pl.BlockSpec((nm, 2 * d), fix), _wspec(wq_g), _wspec(wo_g), ANY_SPEC],
        out_specs=[pl.BlockSpec((tm, d), row)] * 4 + [pl.BlockSpec((nm, 2 * d), fix),
                                                      pl.BlockSpec((1, d), fix)],
        compiler_params=_params("arbitrary"),
    )(x, dxn, gain.reshape(1, d), kv, wq_g, wo_g, after)


def loss_head(x, target, gain, *, name, tm=None):
    t, d = x.shape
    tm = _row_tile(t, tm)

    def body(x_ref, t_ref, g_ref, dx_ref, dg_ref, loss_ref):
        @pl.when(pl.program_id(0) == 0)
        def _():
            dg_ref[...] = jnp.zeros_like(dg_ref)
            loss_ref[...] = jnp.zeros_like(loss_ref)
        g = g_ref[...]
        xhat, r = _rms_fwd(x_ref[...], None)
        err = xhat * g - t_ref[...]
        loss_ref[...] += 0.5 * jnp.sum(jnp.sum(err * err, axis=-1, keepdims=True) / d,
                                       axis=0, keepdims=True)
        dx, dg = _rms_bwd(xhat, r, g, err / d)
        dx_ref[...] = dx
        dg_ref[...] += dg

    row = lambda i: (i, 0)
    fix = lambda i: (0, 0)
    return pl.pallas_call(
        body, name=name, grid=(t // tm,),
        out_shape=[jax.ShapeDtypeStruct((t, d), F32), jax.ShapeDtypeStruct((1, d), F32),
                   jax.ShapeDtypeStruct((1, 1), F32)],
        in_specs=[pl.BlockSpec((tm, d), row), pl.BlockSpec((tm, d), row), pl.BlockSpec((1, d), fix)],
        out_specs=[pl.BlockSpec((tm, d), row), pl.BlockSpec((1, d), fix), pl.BlockSpec((1, 1), fix)],
        compiler_params=_params("arbitrary"),
    )(x, target, gain.reshape(1, d))


def _adamw_math(w, g, m, v):
    m = ADAM_B1 * m + (1.0 - ADAM_B1) * g
    v = ADAM_B2 * v + (1.0 - ADAM_B2) * (g * g)
    m_hat = m / (1.0 - ADAM_B1 ** ADAM_STEP)
    v_hat = v / (1.0 - ADAM_B2 ** ADAM_STEP)
    delta = -ADAM_LR * (m_hat / (jnp.sqrt(v_hat) + ADAM_EPS) + ADAM_WD * w)
    return delta, m, v


def adamw_sharded(own, lands, w, m, v, me_arr, *, name):
    nl, r, c = w.shape
    assert nl == len(own) == len(lands) == 2
    tr = next(cand for cand in (256, 176, 128, r) if r % cand == 0)
    nr = r // tr

    def body(me_ref, o0, o1, l0, l1, w_ref, m_ref, v_ref, g_out, d_out, m_out, v_out):
        def total(o_ref, l_ref):
            acc = o_ref[...].astype(F32)
            for p in range(N_DEV - 1):
                acc = acc + l_ref[p].astype(F32)
            return acc
        g = jnp.where(pl.program_id(0) == 0, total(o0, l0), total(o1, l1))
        delta, mn, vn = _adamw_math(w_ref[...], g, m_ref[...], v_ref[...])
        g_out[...] = g
        d_out[...] = delta
        m_out[...] = mn
        v_out[...] = vn

    row0 = lambda l, i: jnp.where(l == 0, i, nr - 1)
    row1 = lambda l, i: jnp.where(l == 1, i, 0)
    blk = pl.BlockSpec((None, tr, c), lambda l, i, me: (l, i, 0))
    grid_spec = pltpu.PrefetchScalarGridSpec(
        num_scalar_prefetch=1, grid=(nl, nr),
        in_specs=[pl.BlockSpec((None, tr, c), lambda l, i, me: (me[0], row0(l, i), 0)),
                  pl.BlockSpec((None, tr, c), lambda l, i, me: (me[0], row1(l, i), 0)),
                  pl.BlockSpec((N_DEV - 1, tr, c), lambda l, i, me: (0, row0(l, i), 0)),
                  pl.BlockSpec((N_DEV - 1, tr, c), lambda l, i, me: (0, row1(l, i), 0)),
                  blk, blk, blk],
        out_specs=[blk] * 4)
    return pl.pallas_call(
        body, name=name, grid_spec=grid_spec,
        out_shape=[jax.ShapeDtypeStruct((nl, r, c), F32)] * 4,
        compiler_params=_params("arbitrary", "arbitrary"),
    )(me_arr, own[0], own[1], lands[0], lands[1], w, m, v)


def adamw_many(gs, ws, ms, vs, *, name):
    n = len(ws)
    shapes = [w.shape for w in ws]
    as2d = lambda a: a.reshape(1, -1) if a.ndim == 1 else a

    def body(*refs):
        g_r, w_r, m_r, v_r = refs[:n], refs[n:2 * n], refs[2 * n:3 * n], refs[3 * n:4 * n]
        outs = refs[4 * n:]
        for i in range(n):
            delta, mn, vn = _adamw_math(w_r[i][...], g_r[i][...], m_r[i][...], v_r[i][...])
            outs[3 * i][...] = delta
            outs[3 * i + 1][...] = mn
            outs[3 * i + 2][...] = vn

    operands = [as2d(a) for group in (gs, ws, ms, vs) for a in group]
    out_shape = [jax.ShapeDtypeStruct(as2d(w).shape, F32) for w in ws for _ in range(3)]
    outs = pl.pallas_call(
        body, name=name, out_shape=out_shape,
        in_specs=[VMEM_SPEC] * (4 * n), out_specs=[VMEM_SPEC] * (3 * n),
        compiler_params=pltpu.CompilerParams(vmem_limit_bytes=VMEM_LIMIT),
    )(*operands)
    return [tuple(outs[3 * i + k].reshape(shapes[i]) for k in range(3)) for i in range(n)]


def cast_into_slot(a, layer, me_arr, *, name, dtype=None, after=None):
    dtype = BF16 if dtype is None else dtype
    _, r, c = a.shape
    tr = next(cand for cand in (256, 176, 128, r) if r % cand == 0)
    extra = [] if after is None else [after]

    def body(me_ref, a_ref, *rest):
        rest[-1][...] = a_ref[...].astype(dtype)

    grid_spec = pltpu.PrefetchScalarGridSpec(
        num_scalar_prefetch=1, grid=(r // tr,),
        in_specs=[pl.BlockSpec((None, tr, c), lambda i, me: (layer, i, 0))] + [ANY_SPEC] * len(extra),
        out_specs=pl.BlockSpec((None, tr, c), lambda i, me: (me[0], i, 0)))
    return pl.pallas_call(
        body, name=name, grid_spec=grid_spec,
        out_shape=jax.ShapeDtypeStruct((N_DEV, r, c), dtype),
        compiler_params=_params("parallel"),
    )(me_arr, a, *extra)


def _pack(arrs, rows):
    flat = jnp.concatenate([a.reshape(-1).astype(F32) for a in arrs])
    pad = rows * 128 - flat.shape[0]
    assert pad >= 0
    if pad:
        flat = jnp.concatenate([flat, jnp.zeros((pad,), F32)])
    return flat.reshape(rows, 128)


def _unpack(packed, shapes):
    flat = packed.reshape(-1)
    out, pos = [], 0
    for s in shapes:
        n = 1
        for dim in s:
            n *= dim
        out.append(flat[pos:pos + n].reshape(s))
        pos += n
    return out


def _rows_for(shapes):
    n = 0
    for s in shapes:
        k = 1
        for dim in s:
            k *= dim
        n += k
    return -(-n // 1024) * 8


GATHER_GROUPS = (("ffn1", ("ffn1_w_in", "ffn1_w_out")),
                 ("mid", ("mix_w_in", "mix_w_out", "xattn_wkv", "xattn_wq", "xattn_wo")),
                 ("ffn2", ("ffn2_w_in", "ffn2_w_out")))
SMALL_REPL = ["norm_ffn1", "norm_mix", "sgu_norm_g", "sgu_w", "sgu_b", "cconv_ln_g", "cconv_ln_b",
              "pool_w", "pool_scale", "norm_xattn", "norm_mem", "norm_ffn2", "norm_final"]
SMALL_SHARD = ["sconv_w", "cconv_w"]
TRANSPOSED = ("ffn1_w_in", "ffn2_w_in")
WEIGHTS = ["norm_ffn1", "ffn1_w_in", "ffn1_w_out", "norm_mix", "mix_w_in", "sconv_w", "sgu_norm_g",
           "sgu_w", "sgu_b", "cconv_w", "cconv_ln_g", "cconv_ln_b", "pool_w", "pool_scale", "mix_w_out",
           "norm_xattn", "norm_mem", "xattn_wq", "xattn_wkv", "xattn_wo", "norm_ffn2", "ffn2_w_in",
           "ffn2_w_out", "norm_final"]


def kernel(x, mem, norm_ffn1, ffn1_w_in, ffn1_w_out, norm_mix, mix_w_in, sconv_w, sgu_norm_g, sgu_w, sgu_b, cconv_w, cconv_ln_g, cconv_ln_b, pool_w, pool_scale, mix_w_out, norm_xattn, norm_mem, xattn_wq, xattn_wkv, xattn_wo, norm_ffn2, ffn2_w_in, ffn2_w_out, norm_final, loss_target, m_norm_ffn1, m_ffn1_w_in, m_ffn1_w_out, m_norm_mix, m_mix_w_in, m_sconv_w, m_sgu_norm_g, m_sgu_w, m_sgu_b, m_cconv_w, m_cconv_ln_g, m_cconv_ln_b, m_pool_w, m_pool_scale, m_mix_w_out, m_norm_xattn, m_norm_mem, m_xattn_wq, m_xattn_wkv, m_xattn_wo, m_norm_ffn2, m_ffn2_w_in, m_ffn2_w_out, m_norm_final, v_norm_ffn1, v_ffn1_w_in, v_ffn1_w_out, v_norm_mix, v_mix_w_in, v_sconv_w, v_sgu_norm_g, v_sgu_w, v_sgu_b, v_cconv_w, v_cconv_ln_g, v_cconv_ln_b, v_pool_w, v_pool_scale, v_mix_w_out, v_norm_xattn, v_norm_mem, v_xattn_wq, v_xattn_wkv, v_xattn_wo, v_norm_ffn2, v_ffn2_w_in, v_ffn2_w_out, v_norm_final):
    args = dict(locals())
    wts = {n: args[n] for n in WEIGHTS}
    mom = {n: args["m_" + n] for n in WEIGHTS}
    var = {n: args["v_" + n] for n in WEIGHTS}
    for n in TRANSPOSED:
        wts[n], mom[n], var[n] = (jnp.swapaxes(a, 1, 2) for a in (wts[n], mom[n], var[n]))
    x0 = x[0]
    mem0 = mem[0]
    target = loss_target[0]
    t, d = x0.shape
    nl = norm_ffn1.shape[0]
    w = MIX_W
    me = _my_index()

    me_arr = jnp.reshape(me, (1,)).astype(jnp.int32)

    small_g = all_gather([sconv_w, cconv_w], name="gather_conv_taps")
    sconv_full = jnp.transpose(small_g[0], (1, 2, 0, 3)).reshape(nl, SCONV_K, w)
    cconv_full = jnp.transpose(small_g[1], (1, 2, 0, 3)).reshape(nl, CCONV_K, w)
    pending = {}
    token = small_g[1]
    masks = GATHER_MASKS
    keys = [(gname, l, members) for l in range(nl) for gname, members in GATHER_GROUPS]
    first = [[cast_into_slot(wts[n], keys[0][1], me_arr, name=f"cast_{n}{keys[0][1]}") for n in keys[0][2]]]
    started, token = gather_start_groups(first, token, name="gather_start_first", masks=masks)
    casts = [[cast_into_slot(wts[n], l, me_arr, name=f"cast_{n}{l}", after=token) for n in members]
             for gname, l, members in keys[1:]]
    rest, token = gather_start_groups(casts, token, name="gather_start_rest", masks=masks)
    for (gname, l, members), (send, recv, gs) in zip(keys, started + rest):
        pending[gname, l] = (members, gs, send, recv, masks)
    wg = [dict() for _ in range(nl)]

    def arrive(gname, l, after):
        members, gs, send, recv, masks = pending.pop((gname, l))
        gs = gather_wait(gs, send, recv, after, name=f"gather_wait_{gname}{l}", masks=masks)
        if masks is GATHER_MASKS:
            gs = sibling_forward(gs, name=f"gather_forward_{gname}{l}")
        wg[l].update(zip(members, gs))
    sconv_pad = jnp.pad(sconv_full, ((0, 0), (0, 8 - SCONV_K), (0, 0)))
    cconv_pad = jnp.pad(cconv_full, ((0, 0), (0, 32 - CCONV_K), (0, 0)))
    zeros_w = jnp.zeros((nl, w), F32)
    vecs = jnp.stack([sgu_norm_g, cconv_ln_g, cconv_ln_b, pool_scale] + [zeros_w] * 4, axis=1)
    wt = jnp.tril(sgu_w).astype(BF16)
    bexp = jnp.repeat(jnp.swapaxes(sgu_b, 1, 2), w // N_HEADS, axis=2)
    eye = jnp.eye(4, dtype=F32)
    pbd = jnp.einsum("lgcd,gh->lgchd", pool_w, eye).reshape(nl, w, w).astype(BF16)

    def mixer_args(l):
        return sconv_pad[l], cconv_pad[l], vecs[l], wt[l], bexp[l], pbd[l]

    saved = []
    xc = x0
    after = token
    for l in range(nl):
        s = {"x_ffn1": xc}
        arrive("ffn1", l, after)
        xc, s["gu_ffn1"] = ffn_fwd(xc, norm_ffn1[l], wg[l]["ffn1_w_in"], wg[l]["ffn1_w_out"],
                                   name=f"ffn1_fwd{l}", tm=FFN_FWD_TILE)
        s["x_mix"] = xc
        arrive("mid", l, xc)
        z = mm_rows(xc, wg[l]["mix_w_in"], "col", gain=norm_mix[l], name=f"mix_in{l}")
        y, xc = mixer_fwd(z, *mixer_args(l), xc, wg[l]["mix_w_out"], name=f"mixer_fwd{l}")
        s["z"], s["y"] = z, y
        s["x_att"] = xc
        kv = mm_rows(mem0, wg[l]["xattn_wkv"], "col", gain=norm_mem[l], name=f"kv{l}")
        s["kv"] = kv
        xc = xattn_fwd(xc, norm_xattn[l], kv, wg[l]["xattn_wq"], wg[l]["xattn_wo"], name=f"xattn_fwd{l}")
        s["x_ffn2"] = xc
        arrive("ffn2", l, xc)
        xc, s["gu_ffn2"] = ffn_fwd(xc, norm_ffn2[l], wg[l]["ffn2_w_in"], wg[l]["ffn2_w_out"],
                                   name=f"ffn2_fwd{l}", tm=FFN_FWD_TILE)
        after = xc
        saved.append(s)

    dx, g_norm_final, loss_local = loss_head(xc, target, norm_final, name="loss_head")

    tm = _row_tile(t, TN_TILE)
    small ={n: [None] * nl for n in SMALL_REPL + SMALL_SHARD if n != "norm_final"}
    scattered = {}
    tie = [token]

    def send_grads(gname, l, grads):
        members = list(grads)
        send, recv, gs, lands, tie[0] = scatter_start(
            [grads[n] for n in members], tie[0], name=f"scatter_start_{gname}{l}")
        scattered[gname, l] = (members, gs, lands, send, recv)

    names = SMALL_REPL + SMALL_SHARD + ["loss"]
    small_pending = []

    def start_small():
        small_full = {n: jnp.stack(v) for n, v in small.items()}
        small_full["norm_final"] = g_norm_final[0]
        small_full["loss"] = loss_local[0]
        shapes = [small_full[n].shape for n in names]
        packed = _pack([small_full[n] for n in names], _rows_for(shapes))
        slot = cast_into_slot(packed[None], 0, me_arr, name="small_into_slot", dtype=F32)
        send, recv, gs, tie[0] = gather_start([slot], tie[0], name="small_gather_start", masks=ALL_MASKS)
        small_pending.append((gs, send, recv, shapes))

    def ffn_backward(which, l, x_in, dy, gu, gain):
        w_in, w_out = wg[l][which + "_w_in"], wg[l][which + "_w_out"]
        dx_, h_, act, dgu, dgn, dyh = ffn_bwd_rows(x_in, dy, gu, gain, w_in, w_out, tie[0],
                                                   name=f"{which}_bwd{l}_rows")
        small["norm_" + which][l] = dgn[0]
        last = which == "ffn1" and l == 0
        if last:
            start_small()
        g_in = ffn_grad_w_in(h_, dgu, tie[0], name=f"{which}_bwd{l}_dwin")
        if last:
            send_grads(which + "_in", l, {which + "_w_in": g_in})
        g_out = ffn_grad_w_out(act, dyh, tie[0], name=f"{which}_bwd{l}_dwout")
        if last:
            send_grads(which + "_out", l, {which + "_w_out": g_out})
        else:
            send_grads(which, l, {which + "_w_in": g_in, which + "_w_out": g_out})
        return dx_

    for l in reversed(range(nl)):
        s = saved[l]
        wl = wg[l]
        dx = ffn_backward("ffn2", l, s["x_ffn2"], dx, s["gu_ffn2"], norm_ffn2[l])

        bg = {}
        dxn = dx
        dx, h, dq, o, dkv, dgn = xattn_bwd_rows(
            s["x_att"], dxn, norm_xattn[l], s["kv"], wl["xattn_wq"], wl["xattn_wo"], tie[0],
            name=f"xattn_bwd{l}")
        small["norm_xattn"][l] = dgn[0]
        row_spec = pl.BlockSpec((tm, d), lambda s_, i: (i, 0))
        bg["xattn_wq"] = mm_tn(h, dq, nb=1, ka=d, nbk=d, tm=tm, m=t, a_spec=row_spec, b_spec=row_spec,
                               name=f"dwq{l}").reshape(N_DEV, d // N_DEV, d)
        bg["xattn_wo"] = mm_tn(o, dxn, nb=1, ka=d, nbk=d, tm=tm, m=t, a_spec=row_spec, b_spec=row_spec,
                               name=f"dwo{l}").reshape(N_DEV, d // N_DEV, d)
        _, mhat, dgn = mm_nt(dkv, wl["xattn_wkv"], "col", x=mem0, gain=norm_mem[l], name=f"dmem{l}")
        small["norm_mem"][l] = dgn[0]
        nm = mem0.shape[0]
        bg["xattn_wkv"] = mm_tn(mhat, dkv, nb=N_DEV, ka=d, nbk=2 * d // N_DEV, tm=nm, m=nm,
                                a_spec=pl.BlockSpec((nm, d), lambda s_, i: (0, 0)),
                                b_spec=pl.BlockSpec((nm, 2 * d // N_DEV), lambda s_, i: (0, s_)),
                                name=f"dwkv{l}")
        send_grads("xattn", l, bg)

        bg = {}
        dxn = dx
        bg["mix_w_out"] = mm_tn(s["y"], dxn, nb=1, ka=d, nbk=d, tm=tm, m=t, a_spec=row_spec,
                                b_spec=row_spec, name=f"dwmo{l}").reshape(N_DEV, d // N_DEV, d)
        dz, gvec, gcc, gwt, gb, gpbd = mixer_bwd(s["z"], dxn, wl["mix_w_out"], *mixer_args(l),
                                                 name=f"mixer_bwd{l}")
        small["sconv_w"][l] = gvec[0:SCONV_K]
        small["sgu_norm_g"][l] = gvec[3]
        small["cconv_ln_g"][l] = gvec[4]
        small["cconv_ln_b"][l] = gvec[5]
        small["pool_scale"][l] = gvec[6]
        small["cconv_w"][l] = gcc[0:CCONV_K]
        small["sgu_w"][l] = gwt
        small["sgu_b"][l] = jnp.transpose(gb[:, 0:N_HEADS])
        gw = w // 4
        small["pool_w"][l] = jnp.stack([gpbd[g * gw:(g + 1) * gw, g * gw:(g + 1) * gw] for g in range(4)])
        dx, h, dgn = mm_nt(dz, wl["mix_w_in"], "col", x=s["x_mix"], gain=norm_mix[l], dx_in=dxn,
                           after=tie[0], name=f"dh_mix{l}")
        small["norm_mix"][l] = dgn[0]
        th = _row_tile(t, TN_TILE // 2)
        bg["mix_w_in"] = mm_tn(h, dz, nb=1, ka=d, nbk=N_DEV * w, tm=th, m=t, col_slots=N_DEV,
                               a_spec=pl.BlockSpec((th, d), lambda s_, i: (i, 0)),
                               b_spec=pl.BlockSpec((th, N_DEV * w), lambda s_, i: (i, 0)), name=f"dwmi{l}")
        send_grads("mix", l, bg)

        dx = ffn_backward("ffn1", l, s["x_ffn1"], dx, s["gu_ffn1"], norm_ffn1[l])

    out = {}

    def finish(keys, after):
        own, land = {}, {}
        for gname, l in keys:
            members, gs, lands, send, recv = scattered.pop((gname, l))
            gs, lands = scatter_wait(gs, lands, send, recv, after, name=f"scatter_wait_{gname}{l}")
            for n, g_, l_ in zip(members, gs, lands):
                own.setdefault(n, {})[l] = g_
                land.setdefault(n, {})[l] = l_
        for n in own:
            out[n] = adamw_sharded([own[n][l] for l in range(nl)], [land[n][l] for l in range(nl)],
                                   wts[n], mom[n], var[n], me_arr, name="adamw_" + n)
            after = out[n][1]
        return after

    after = tie[0]
    for gname in ("ffn2", "xattn", "mix"):
        after = finish([(gname, l) for l in reversed(range(nl))], after)
    (gs, send, recv, shapes), = small_pending
    gs = gather_wait(gs, send, recv, after, name="small_gather_wait", masks=ALL_MASKS)
    summed = sum_slots(gs[0], name="small_sum")
    gsm = dict(zip(names, _unpack(summed, shapes)))
    loss = gsm["loss"][0]
    finish([("ffn1", l) for l in reversed(range(1, nl))] + [("ffn1_in", 0), ("ffn1_out", 0)], summed)
    cs = w // N_DEV
    for n in SMALL_SHARD:
        gsm[n] = lax.dynamic_slice_in_dim(gsm[n], me * cs, cs, axis=2)
    small_names = SMALL_REPL + SMALL_SHARD
    upd = adamw_many([gsm[n] for n in small_names], [wts[n] for n in small_names],
                     [mom[n] for n in small_names], [var[n] for n in small_names], name="adamw_small")
    for n, (a, b, c) in zip(small_names, upd):
        out[n] = (gsm[n], a, b, c)
    for n in TRANSPOSED:
        out[n] = tuple(jnp.swapaxes(a, 1, 2) for a in out[n])

    grad_x = dx.reshape(1, t, d)
    return (loss, grad_x, *[out[n][0] for n in WEIGHTS], *[out[n][1] for n in WEIGHTS],
            *[out[n][2] for n in WEIGHTS], *[out[n][3] for n in WEIGHTS])
```

```python
import jax
import jax.numpy as jnp
from jax import lax
from jax.experimental import pallas as pl
from jax.experimental.pallas import tpu as pltpu

F32 = jnp.float32
BF16 = jnp.bfloat16
MESH = pl.DeviceIdType.MESH
N_DEV = 8
EPS = 1e-6
HALO = 32
SGU_CHUNK = 128
CCONV_K = 31
SCONV_K = 3
MIX_W = 256
N_HEADS = 4
VMEM_LIMIT = 56 * 1024 * 1024
ROW_TILE = 512
TN_TILE = 2048
FFN_FWD_TILE = 1024
FFN_BWD_SPLIT = 2
FFN_FWD_SPLIT = 2
MIX_TILE = 512

ADAM_LR = 0.001
ADAM_B1 = 0.9
ADAM_B2 = 0.999
ADAM_EPS = 1e-08
ADAM_WD = 0.01
ADAM_STEP = 10

HBM_SPEC = pl.BlockSpec(memory_space=pltpu.HBM)
VMEM_SPEC = pl.BlockSpec(memory_space=pltpu.VMEM)


def _params(*sem):
    return pltpu.CompilerParams(dimension_semantics=tuple(sem), vmem_limit_bytes=VMEM_LIMIT)


def _row_tile(m, pref=None):
    t = min(m, ROW_TILE if pref is None else pref)
    assert m % t == 0, (m, t)
    return t


def _my_index():
    return lax.axis_index("x") * 4 + lax.axis_index("y") * 2 + lax.axis_index("c")


def _peer(mask):
    x, y, c = lax.axis_index("x"), lax.axis_index("y"), lax.axis_index("c")
    px = 1 - x if mask & 4 else x
    py = 1 - y if mask & 2 else y
    pc = 1 - c if mask & 1 else c
    return (px, py, pc), px * 4 + py * 2 + pc


def all_gather(arrs, name):
    n = len(arrs)

    def body(*refs):
        ins, outs = refs[:n], refs[n:2 * n]
        send_sems, recv_sems, loc_sems = refs[2 * n:]
        me = _my_index()
        local = []
        for i in range(n):
            cp = pltpu.make_async_copy(ins[i], outs[i].at[me], loc_sems.at[i])
            cp.start()
            local.append(cp)
        sends = []
        for i in range(n):
            for m in range(1, N_DEV):
                peer, _ = _peer(m)
                cp = pltpu.make_async_remote_copy(
                    src_ref=ins[i], dst_ref=outs[i].at[me],
                    send_sem=send_sems.at[i, m - 1], recv_sem=recv_sems.at[i, m - 1],
                    device_id=peer, device_id_type=MESH)
                cp.start()
                sends.append(cp)
        for i in range(n):
            for m in range(1, N_DEV):
                peer, pidx = _peer(m)
                pltpu.make_async_remote_copy(
                    src_ref=ins[i], dst_ref=outs[i].at[pidx],
                    send_sem=send_sems.at[i, m - 1], recv_sem=recv_sems.at[i, m - 1],
                    device_id=peer, device_id_type=MESH).wait_recv()
        for cp in sends:
            cp.wait_send()
        for cp in local:
            cp.wait()

    return pl.pallas_call(
        body, name=name,
        out_shape=[jax.ShapeDtypeStruct((N_DEV,) + a.shape, a.dtype) for a in arrs],
        in_specs=[HBM_SPEC] * n, out_specs=[HBM_SPEC] * n,
        scratch_shapes=[pltpu.SemaphoreType.DMA((n, N_DEV - 1)),
                        pltpu.SemaphoreType.DMA((n, N_DEV - 1)),
                        pltpu.SemaphoreType.DMA((n,))],
    )(*arrs)


SEM_SPEC = pl.BlockSpec(memory_space=pltpu.SEMAPHORE)
ANY_SPEC = pl.BlockSpec(memory_space=pl.ANY)
SIDE_EFFECT = pltpu.SideEffectType.DATAFLOW_SIDE_EFFECTING


def _hbm(a):
    return pltpu.with_memory_space_constraint(a, pltpu.HBM)


def _sem_pairs(n):
    return (pltpu.SemaphoreType.DMA((n * (N_DEV - 1),)), pltpu.SemaphoreType.DMA((n * (N_DEV - 1),)))


def _sem(i, m):
    return i * (N_DEV - 1) + m - 1


def _gather_copy(g_ref, i, m, send_sems, recv_sems, origin):
    peer, _ = _peer(m)
    return pltpu.make_async_remote_copy(
        src_ref=g_ref.at[origin], dst_ref=g_ref.at[origin],
        send_sem=send_sems.at[_sem(i, m)], recv_sem=recv_sems.at[_sem(i, m)],
        device_id=peer, device_id_type=MESH)


GATHER_MASKS = (1, 2, 4, 6)
FORWARD_MASKS = (2, 4, 6)


ALL_MASKS = tuple(range(1, N_DEV))


def gather_start(gs, after, name, masks=GATHER_MASKS):
    n = len(gs)

    def body(*refs):
        g_in = refs[:n]
        send_sems, recv_sems = refs[n + 1], refs[n + 2]
        token = refs[-1]
        me = _my_index()
        for i in range(n):
            for m in masks:
                _gather_copy(g_in[i], i, m, send_sems, recv_sems, me).start()
        token[...] = jnp.zeros_like(token)

    outs = pl.pallas_call(
        body, name=name,
        out_shape=(*_sem_pairs(n), *[pltpu.HBM(g.shape, g.dtype) for g in gs],
                   jax.ShapeDtypeStruct((8, 128), F32)),
        in_specs=[HBM_SPEC] * n + [ANY_SPEC],
        out_specs=(SEM_SPEC, SEM_SPEC, *[HBM_SPEC] * n, VMEM_SPEC),
        input_output_aliases={i: 2 + i for i in range(n)},
        compiler_params=pltpu.CompilerParams(has_side_effects=SIDE_EFFECT),
    )(*[_hbm(g) for g in gs], after)
    return outs[0], outs[1], list(outs[2:2 + n]), outs[-1]


def gather_start_groups(groups, after, name, masks=GATHER_MASKS):
    sizes = [len(g) for g in groups]
    flat = [a for g in groups for a in g]
    n, ng = len(flat), len(groups)

    def body(*refs):
        g_in = refs[:n]
        sems = refs[n + 1:n + 1 + 2 * ng]
        token = refs[-1]
        me = _my_index()
        pos = 0
        for k, size in enumerate(sizes):
            for i in range(size):
                for m in masks:
                    _gather_copy(g_in[pos + i], i, m, sems[2 * k], sems[2 * k + 1], me).start()
            pos += size
        token[...] = jnp.zeros_like(token)

    outs = pl.pallas_call(
        body, name=name,
        out_shape=(*[s for size in sizes for s in _sem_pairs(size)],
                   *[pltpu.HBM(g.shape, g.dtype) for g in flat], jax.ShapeDtypeStruct((8, 128), F32)),
        in_specs=[HBM_SPEC] * n + [ANY_SPEC],
        out_specs=(*[SEM_SPEC] * (2 * ng), *[HBM_SPEC] * n, VMEM_SPEC),
        input_output_aliases={i: 2 * ng + i for i in range(n)},
        compiler_params=pltpu.CompilerParams(has_side_effects=SIDE_EFFECT),
    )(*[_hbm(g) for g in flat], after)
    result, pos = [], 2 * ng
    for k, size in enumerate(sizes):
        result.append((outs[2 * k], outs[2 * k + 1], list(outs[pos:pos + size])))
        pos += size
    return result, outs[-1]


def gather_wait(gs, send_sems, recv_sems, after, name, masks=GATHER_MASKS):
    n = len(gs)

    def body(*refs):
        g_in = refs[:n]
        send, recv = refs[n], refs[n + 1]
        me = _my_index()
        for i in range(n):
            for m in masks:
                _, pidx = _peer(m)
                _gather_copy(g_in[i], i, m, send, recv, me).wait_send()
                _gather_copy(g_in[i], i, m, send, recv, pidx).wait_recv()

    outs = pl.pallas_call(
        body, name=name,
        out_shape=[pltpu.HBM(g.shape, g.dtype) for g in gs],
        in_specs=[HBM_SPEC] * n + [SEM_SPEC, SEM_SPEC, ANY_SPEC],
        out_specs=[HBM_SPEC] * n,
        input_output_aliases={i: i for i in range(n)},
        compiler_params=pltpu.CompilerParams(has_side_effects=SIDE_EFFECT),
    )(*gs, send_sems, recv_sems, after)
    return list(outs)


def sibling_forward(gs, name):
    n = len(gs)
    nf = len(FORWARD_MASKS)

    def body(*refs):
        g_in = refs[:n]
        send_sems, recv_sems = refs[2 * n:]
        x, y, c = lax.axis_index("x"), lax.axis_index("y"), lax.axis_index("c")
        sibling = (x, y, 1 - c)

        def copy(i, k, origin):
            return pltpu.make_async_remote_copy(
                src_ref=g_in[i].at[origin], dst_ref=g_in[i].at[origin],
                send_sem=send_sems.at[i * nf + k], recv_sem=recv_sems.at[i * nf + k],
                device_id=sibling, device_id_type=MESH)
        sends = []
        for i in range(n):
            for k, m in enumerate(FORWARD_MASKS):
                _, origin = _peer(m)
                cp = copy(i, k, origin)
                cp.start()
                sends.append(cp)
        for i in range(n):
            for k, m in enumerate(FORWARD_MASKS):
                _, origin = _peer(m ^ 1)
                copy(i, k, origin).wait_recv()
        for cp in sends:
            cp.wait_send()

    outs = pl.pallas_call(
        body, name=name,
        out_shape=[jax.ShapeDtypeStruct(g.shape, g.dtype) for g in gs],
        in_specs=[HBM_SPEC] * n, out_specs=[HBM_SPEC] * n,
        input_output_aliases={i: i for i in range(n)},
        scratch_shapes=[pltpu.SemaphoreType.DMA((n * nf,)), pltpu.SemaphoreType.DMA((n * nf,))],
    )(*gs)
    return list(outs)


def _forward_copy(g_ref, i, k, send_sems, recv_sems, origin):
    sibling = (lax.axis_index("x"), lax.axis_index("y"), 1 - lax.axis_index("c"))
    slot = i * len(FORWARD_MASKS) + k
    return pltpu.make_async_remote_copy(
        src_ref=g_ref.at[origin], dst_ref=g_ref.at[origin],
        send_sem=send_sems.at[slot], recv_sem=recv_sems.at[slot],
        device_id=sibling, device_id_type=MESH)


def forward_start(gs, after, name):
    n = len(gs)
    nsem = n * len(FORWARD_MASKS)

    def body(*refs):
        g_in = refs[:n]
        send_sems, recv_sems = refs[n + 1], refs[n + 2]
        token = refs[-1]
        for i in range(n):
            for k, m in enumerate(FORWARD_MASKS):
                _, origin = _peer(m)
                _forward_copy(g_in[i], i, k, send_sems, recv_sems, origin).start()
        token[...] = jnp.zeros_like(token)

    outs = pl.pallas_call(
        body, name=name,
        out_shape=(pltpu.SemaphoreType.DMA((nsem,)), pltpu.SemaphoreType.DMA((nsem,)),
                   *[pltpu.HBM(g.shape, g.dtype) for g in gs], jax.ShapeDtypeStruct((8, 128), F32)),
        in_specs=[HBM_SPEC] * n + [ANY_SPEC],
        out_specs=(SEM_SPEC, SEM_SPEC, *[HBM_SPEC] * n, VMEM_SPEC),
        input_output_aliases={i: 2 + i for i in range(n)},
        compiler_params=pltpu.CompilerParams(has_side_effects=SIDE_EFFECT),
    )(*[_hbm(g) for g in gs], after)
    return outs[0], outs[1], list(outs[2:2 + n]), outs[-1]


def forward_wait(gs, send_sems, recv_sems, after, name):
    n = len(gs)

    def body(*refs):
        g_in = refs[:n]
        send, recv = refs[n], refs[n + 1]
        for i in range(n):
            for k, m in enumerate(FORWARD_MASKS):
                _, mine = _peer(m)
                _, theirs = _peer(m ^ 1)
                _forward_copy(g_in[i], i, k, send, recv, mine).wait_send()
                _forward_copy(g_in[i], i, k, send, recv, theirs).wait_recv()

    outs = pl.pallas_call(
        body, name=name,
        out_shape=[pltpu.HBM(g.shape, g.dtype) for g in gs],
        in_specs=[HBM_SPEC] * n + [SEM_SPEC, SEM_SPEC, ANY_SPEC],
        out_specs=[HBM_SPEC] * n,
        input_output_aliases={i: i for i in range(n)},
        compiler_params=pltpu.CompilerParams(has_side_effects=SIDE_EFFECT),
    )(*gs, send_sems, recv_sems, after)
    return list(outs)


def _scatter_copy(g_ref, l_ref, i, m, send_sems, recv_sems):
    peer, pidx = _peer(m)
    return pltpu.make_async_remote_copy(
        src_ref=g_ref.at[pidx], dst_ref=l_ref.at[m - 1],
        send_sem=send_sems.at[_sem(i, m)], recv_sem=recv_sems.at[_sem(i, m)],
        device_id=peer, device_id_type=MESH)


def scatter_start(grads, after, name):
    n = len(grads)
    lands = [lax.empty((N_DEV - 1,) + g.shape[1:], g.dtype) for g in grads]

    def body(*refs):
        g_in, l_in = refs[:n], refs[n:2 * n]
        send_sems, recv_sems = refs[2 * n + 1], refs[2 * n + 2]
        token = refs[-1]
        for i in range(n):
            for m in range(1, N_DEV):
                _scatter_copy(g_in[i], l_in[i], i, m, send_sems, recv_sems).start()
        token[...] = jnp.zeros_like(token)

    outs = pl.pallas_call(
        body, name=name,
        out_shape=(*_sem_pairs(n), *[pltpu.HBM(g.shape, g.dtype) for g in grads],
                   *[pltpu.HBM(l.shape, l.dtype) for l in lands], jax.ShapeDtypeStruct((8, 128), F32)),
        in_specs=[HBM_SPEC] * (2 * n) + [ANY_SPEC],
        out_specs=(SEM_SPEC, SEM_SPEC, *[HBM_SPEC] * (2 * n), VMEM_SPEC),
        input_output_aliases={i: 2 + i for i in range(2 * n)},
        compiler_params=pltpu.CompilerParams(has_side_effects=SIDE_EFFECT),
    )(*[_hbm(g) for g in grads], *[_hbm(l) for l in lands], after)
    return outs[0], outs[1], list(outs[2:2 + n]), list(outs[2 + n:2 + 2 * n]), outs[-1]


def scatter_wait(grads, lands, send_sems, recv_sems, after, name):
    n = len(grads)

    def body(*refs):
        g_in, l_in = refs[:n], refs[n:2 * n]
        send, recv = refs[2 * n], refs[2 * n + 1]
        for i in range(n):
            for m in range(1, N_DEV):
                cp = _scatter_copy(g_in[i], l_in[i], i, m, send, recv)
                cp.wait_send()
                cp.wait_recv()

    outs = pl.pallas_call(
        body, name=name,
        out_shape=[pltpu.HBM(a.shape, a.dtype) for a in list(grads) + list(lands)],
        in_specs=[HBM_SPEC] * (2 * n) + [SEM_SPEC, SEM_SPEC, ANY_SPEC],
        out_specs=[HBM_SPEC] * (2 * n),
        input_output_aliases={i: i for i in range(2 * n)},
        compiler_params=pltpu.CompilerParams(has_side_effects=SIDE_EFFECT),
    )(*grads, *lands, send_sems, recv_sems, after)
    return list(outs[:n]), list(outs[n:])


def sum_slots(g, name):
    _, r, c = g.shape

    def body(g_ref, out_ref):
        acc = g_ref[0]
        for p in range(1, N_DEV):
            acc = acc + g_ref[p]
        out_ref[...] = acc

    return pl.pallas_call(
        body, name=name, out_shape=jax.ShapeDtypeStruct((r, c), F32),
        in_specs=[VMEM_SPEC], out_specs=VMEM_SPEC,
        compiler_params=pltpu.CompilerParams(vmem_limit_bytes=VMEM_LIMIT),
    )(g)


def _sigmoid(v):
    return 1.0 / (1.0 + jnp.exp(-v))


def _rms_fwd(xf, g):
    r = lax.rsqrt(jnp.mean(xf * xf, axis=-1, keepdims=True) + EPS)
    return xf * r, r


def _rms_bwd(xhat, r, g, dy):
    dg = jnp.sum(dy * xhat, axis=0, keepdims=True)
    dxh = dy * g
    dx = r * (dxh - xhat * jnp.mean(dxh * xhat, axis=-1, keepdims=True))
    return dx, dg


def _ln_stats(v):
    mu = jnp.mean(v, axis=-1, keepdims=True)
    vc = v - mu
    r = lax.rsqrt(jnp.mean(vc * vc, axis=-1, keepdims=True) + EPS)
    return vc * r, r


def _ln_bwd(xhat, r, dxh):
    return r * (dxh - jnp.mean(dxh, axis=-1, keepdims=True)
                - xhat * jnp.mean(dxh * xhat, axis=-1, keepdims=True))


def _dot(a, b):
    return jnp.dot(a, b, preferred_element_type=F32)


def _dot_nt(a, b):
    return lax.dot_general(a, b, (((1,), (1,)), ((), ())), preferred_element_type=F32)


def _dot_tn(a, b):
    return lax.dot_general(a, b, (((0,), (0,)), ((), ())), preferred_element_type=F32)


def _full_weight(w_ref, kind):
    assert kind == "row"
    p, a, b = w_ref.shape
    return w_ref[...].reshape(p * a, b)


def _wspec(wg):
    return pl.BlockSpec(wg.shape, lambda *_: (0, 0, 0))


def mm_rows(a, wg, kind, *, gain=None, residual=None, out_dtype=F32, name, tm=None):
    m, k = a.shape
    p, wa, wb = wg.shape
    n = p * wb if kind == "col" else wb
    tm = _row_tile(m, tm)
    has_gain, has_res = gain is not None, residual is not None

    def body(*refs):
        refs = list(refs)
        a_ref = refs.pop(0)
        g_ref = refs.pop(0) if has_gain else None
        w_ref = refs.pop(0)
        r_ref = refs.pop(0) if has_res else None
        o_ref = refs.pop(0)
        if has_gain:
            xhat, _ = _rms_fwd(a_ref[...].astype(F32), None)
            h = (xhat * g_ref[...]).astype(BF16)
        else:
            h = a_ref[...].astype(BF16)
        if kind == "col":
            for j in range(p):
                o = _dot(h, w_ref[j])
                if has_res:
                    o = o + r_ref[:, j * wb:(j + 1) * wb]
                o_ref[:, j * wb:(j + 1) * wb] = o.astype(out_dtype)
        else:
            o = _dot(h, _full_weight(w_ref, "row"))
            if has_res:
                o = o + r_ref[...]
            o_ref[...] = o.astype(out_dtype)

    operands = [a]
    in_specs = [pl.BlockSpec((tm, k), lambda i: (i, 0))]
    if has_gain:
        operands.append(gain.reshape(1, k))
        in_specs.append(pl.BlockSpec((1, k), lambda i: (0, 0)))
    operands.append(wg)
    in_specs.append(_wspec(wg))
    if has_res:
        operands.append(residual)
        in_specs.append(pl.BlockSpec((tm, n), lambda i: (i, 0)))
    return pl.pallas_call(
        body, name=name, grid=(m // tm,),
        out_shape=jax.ShapeDtypeStruct((m, n), out_dtype),
        in_specs=in_specs, out_specs=pl.BlockSpec((tm, n), lambda i: (i, 0)),
        compiler_params=_params("parallel"),
    )(*operands)


def mm_nt(dz, wg, kind, *, x=None, gain=None, dx_in=None, after=None, name, tm=None):
    m, n = dz.shape
    p, wa, wb = wg.shape
    k = wa if kind == "col" else p * wa
    tm = _row_tile(m, tm)
    epi = x is not None
    has_dx = dx_in is not None
    has_after = after is not None

    def body(*refs):
        refs = list(refs)
        dz_ref, w_ref = refs.pop(0), refs.pop(0)
        if epi:
            x_ref, g_ref = refs.pop(0), refs.pop(0)
            dxi_ref = refs.pop(0) if has_dx else None
        if has_after:
            refs.pop(0)
        if epi:
            dx_ref, h_ref, dg_ref = refs
        else:
            (da_ref,) = refs
        dzb = dz_ref[...].astype(BF16)
        if kind == "col":
            da = _dot_nt(dzb[:, 0:wb], w_ref[0])
            for j in range(1, p):
                da = da + _dot_nt(dzb[:, j * wb:(j + 1) * wb], w_ref[j])
        else:
            da = _dot_nt(dzb, _full_weight(w_ref, "row"))
        if not epi:
            da_ref[...] = da
            return
        g = g_ref[...]
        xhat, r = _rms_fwd(x_ref[...].astype(F32), None)
        h_ref[...] = (xhat * g).astype(BF16)
        dx, dg = _rms_bwd(xhat, r, g, da)
        if has_dx:
            dx = dx + dxi_ref[...]
        dx_ref[...] = dx

        @pl.when(pl.program_id(0) == 0)
        def _():
            dg_ref[...] = jnp.zeros_like(dg_ref)
        dg_ref[...] += dg

    row = lambda i: (i, 0)
    operands = [dz, wg]
    in_specs = [pl.BlockSpec((tm, n), row), _wspec(wg)]
    if epi:
        operands += [x, gain.reshape(1, k)]
        in_specs += [pl.BlockSpec((tm, k), row), pl.BlockSpec((1, k), lambda i: (0, 0))]
        if has_dx:
            operands.append(dx_in)
            in_specs.append(pl.BlockSpec((tm, k), row))
        out_shape = [jax.ShapeDtypeStruct((m, k), F32), jax.ShapeDtypeStruct((m, k), BF16),
                     jax.ShapeDtypeStruct((1, k), F32)]
        out_specs = [pl.BlockSpec((tm, k), row), pl.BlockSpec((tm, k), row),
                     pl.BlockSpec((1, k), lambda i: (0, 0))]
    else:
        out_shape = jax.ShapeDtypeStruct((m, k), F32)
        out_specs = pl.BlockSpec((tm, k), row)
    if has_after:
        operands.append(after)
        in_specs.append(ANY_SPEC)
    return pl.pallas_call(
        body, name=name, grid=(m // tm,), out_shape=out_shape,
        in_specs=in_specs, out_specs=out_specs,
        compiler_params=_params("arbitrary"),
    )(*operands)


def mm_tn(a, b, *, nb, a_spec, b_spec, ka, nbk, tm, m, scale=1.0, out_dtype=BF16, col_slots=1,
          after=None, name):
    ni = m // tm
    assert col_slots == 1 or nb == 1
    cw = nbk // col_slots
    extra = [] if after is None else [after]

    def body(a_ref, b_ref, *rest):
        o_ref, acc = rest[len(extra):]
        i = pl.program_id(1)

        @pl.when(i == 0)
        def _():
            acc[...] = jnp.zeros_like(acc)
        acc[...] += _dot_tn(a_ref[...].astype(BF16), b_ref[...].astype(BF16))

        @pl.when(i == ni - 1)
        def _():
            if col_slots == 1:
                o_ref[...] = (acc[...] * scale).astype(out_dtype)
            else:
                for j in range(col_slots):
                    o_ref[j] = (acc[:, j * cw:(j + 1) * cw] * scale).astype(out_dtype)

    if col_slots == 1:
        out_shape = jax.ShapeDtypeStruct((nb, ka, nbk), out_dtype)
        out_spec = pl.BlockSpec((None, ka, nbk), lambda s, i: (s, 0, 0))
    else:
        out_shape = jax.ShapeDtypeStruct((col_slots, ka, cw), out_dtype)
        out_spec = pl.BlockSpec((col_slots, ka, cw), lambda s, i: (0, 0, 0))
    return pl.pallas_call(
        body, name=name, grid=(nb, ni), out_shape=out_shape,
        in_specs=[a_spec, b_spec] + [ANY_SPEC] * len(extra), out_specs=out_spec,
        scratch_shapes=[pltpu.VMEM((ka, nbk), F32)],
        compiler_params=_params("parallel", "arbitrary"),
    )(a, b, *extra)


def _ffn_specs(w_in_g, w_out_g, d):
    nf = w_in_g.shape[1]
    hr = w_out_g.shape[1]
    assert 2 * hr == nf
    w_in5 = w_in_g.reshape(2, 4, nf, d)
    w_out5 = w_out_g.reshape(4, 2, hr, d)
    in_spec = pl.BlockSpec((2, None, nf, d), lambda i, j: (0, j, 0, 0))
    out_spec = pl.BlockSpec((None, 2, hr, d), lambda i, j: (j, 0, 0, 0))
    return w_in5, w_out5, in_spec, out_spec, nf


def ffn_fwd(x, gain, w_in_g, w_out_g, *, name, tm=None):
    t, d = x.shape
    tm = _row_tile(t, tm)
    w_in5, w_out5, wi_spec, wo_spec, nf = _ffn_specs(w_in_g, w_out_g, d)

    def body(x_ref, g_ref, wi_ref, wo_ref, o_ref, gu_ref, h_scr, acc):
        j = pl.program_id(1)

        @pl.when(j == 0)
        def _():
            xhat, _ = _rms_fwd(x_ref[...], None)
            h_scr[...] = (xhat * g_ref[...]).astype(BF16)
            acc[...] = jnp.zeros_like(acc)
        wo = wo_ref[...].reshape(nf, d)

        def project(rows):
            h = h_scr[rows]
            return _dot_nt(h, wi_ref[0]), _dot_nt(h, wi_ref[1])

        sub = tm // FFN_FWD_SPLIT
        parts = [slice(k * sub, (k + 1) * sub) for k in range(FFN_FWD_SPLIT)]
        gt, up = project(parts[0])
        for k, rows in enumerate(parts):
            if k + 1 < len(parts):
                nxt = project(parts[k + 1])
            gu_ref[0, rows] = gt.astype(BF16)
            gu_ref[1, rows] = up.astype(BF16)
            act = (gt * _sigmoid(gt) * up).astype(BF16)
            acc[rows] += _dot(act, wo)
            if k + 1 < len(parts):
                gt, up = nxt

        @pl.when(j == 3)
        def _():
            o_ref[...] = x_ref[...] + 0.5 * acc[...]

    return pl.pallas_call(
        body, name=name, grid=(t // tm, 4),
        out_shape=[jax.ShapeDtypeStruct((t, d), F32), jax.ShapeDtypeStruct((2, 4, t, nf), BF16)],
        in_specs=[pl.BlockSpec((tm, d), lambda i, j: (i, 0)),
                  pl.BlockSpec((1, d), lambda i, j: (0, 0)), wi_spec, wo_spec],
        out_specs=[pl.BlockSpec((tm, d), lambda i, j: (i, 0)),
                   pl.BlockSpec((2, None, tm, nf), lambda i, j: (0, j, i, 0))],
        scratch_shapes=[pltpu.VMEM((tm, d), BF16), pltpu.VMEM((tm, d), F32)],
        compiler_params=_params("parallel", "arbitrary"),
    )(x, gain.reshape(1, d), w_in5, w_out5)


def ffn_bwd_rows(x, dy, gu, gain, w_in_g, w_out_g, after, *, name, tm=None):
    t, d = x.shape
    tm = _row_tile(t, tm)
    w_in5, w_out5, wi_spec, wo_spec, nf = _ffn_specs(w_in_g, w_out_g, d)

    def body(x_ref, dy_ref, gu_ref, g_ref, wi_ref, wo_ref, after_ref, dx_ref, h_ref, act_ref, dgu_ref, dg_ref,
             dyh_scr, dh_acc):
        i, j = pl.program_id(0), pl.program_id(1)

        @pl.when(j == 0)
        def _():
            xhat, _ = _rms_fwd(x_ref[...], None)
            h_ref[...] = (xhat * g_ref[...]).astype(BF16)
            dyh_scr[...] = (0.5 * dy_ref[...]).astype(BF16)
            dh_acc[...] = jnp.zeros_like(dh_acc)
        wo = wo_ref[...].reshape(nf, d)

        def gates(rows):
            gt = gu_ref[0, rows].astype(F32)
            up = gu_ref[1, rows].astype(F32)
            sg = _sigmoid(gt)
            silu = gt * sg
            act_ref[rows] = (silu * up).astype(BF16)
            return up * (sg * (1.0 + gt * (1.0 - sg))), silu

        def grads(rows, dact, dsilu_up, silu):
            dgt = (dact * dsilu_up).astype(BF16)
            dup = (dact * silu).astype(BF16)
            dgu_ref[0, rows] = dgt
            dgu_ref[1, rows] = dup
            return dgt, dup

        sub = tm // FFN_BWD_SPLIT
        parts = [slice(k * sub, (k + 1) * sub) for k in range(FFN_BWD_SPLIT)]
        dact = _dot_nt(dyh_scr[parts[0]], wo)
        gate = gates(parts[0])
        for k, rows in enumerate(parts):
            if k + 1 < len(parts):
                dact_next = _dot_nt(dyh_scr[parts[k + 1]], wo)
            dgt, dup = grads(rows, dact, *gate)
            dh_acc[rows] += _dot(dgt, wi_ref[0]) + _dot(dup, wi_ref[1])
            if k + 1 < len(parts):
                gate = gates(parts[k + 1])
                dact = dact_next

        @pl.when(j == 3)
        def _():
            g = g_ref[...]
            xhat, r = _rms_fwd(x_ref[...], None)
            dx, dg = _rms_bwd(xhat, r, g, dh_acc[...])
            dx_ref[...] = dy_ref[...] + dx

            @pl.when(i == 0)
            def _():
                dg_ref[...] = jnp.zeros_like(dg_ref)
            dg_ref[...] += dg

    row = lambda i, j: (i, 0)
    return pl.pallas_call(
        body, name=name, grid=(t // tm, 4),
        out_shape=[jax.ShapeDtypeStruct((t, d), F32), jax.ShapeDtypeStruct((t, d), BF16),
                   jax.ShapeDtypeStruct((4, t, nf), BF16), jax.ShapeDtypeStruct((2, 4, t, nf), BF16),
                   jax.ShapeDtypeStruct((1, d), F32), jax.ShapeDtypeStruct((t, d), BF16)],
        in_specs=[pl.BlockSpec((tm, d), row), pl.BlockSpec((tm, d), row),
                  pl.BlockSpec((2, None, tm, nf), lambda i, j: (0, j, i, 0)),
                  pl.BlockSpec((1, d), lambda i, j: (0, 0)), wi_spec, wo_spec, ANY_SPEC],
        out_specs=[pl.BlockSpec((tm, d), row), pl.BlockSpec((tm, d), row),
                   pl.BlockSpec((None, tm, nf), lambda i, j: (j, i, 0)),
                   pl.BlockSpec((2, None, tm, nf), lambda i, j: (0, j, i, 0)),
                   pl.BlockSpec((1, d), lambda i, j: (0, 0)), pl.BlockSpec((tm, d), row)],
        scratch_shapes=[pltpu.VMEM((tm, d), F32)],
        compiler_params=_params("arbitrary", "arbitrary"),
    )(x, dy, gu, gain.reshape(1, d), w_in5, w_out5, after)


def ffn_grad_w_in(h, dgu, after, *, name):
    t, d = h.shape
    nf = dgu.shape[-1]
    tm = _row_tile(t, TN_TILE)
    return mm_tn(dgu.reshape(8, t, nf), h, nb=8, ka=nf, nbk=d, tm=tm, m=t, after=after,
                 a_spec=pl.BlockSpec((None, tm, nf), lambda s, i: (s, i, 0)),
                 b_spec=pl.BlockSpec((tm, d), lambda s, i: (i, 0)), name=name)


def ffn_grad_w_out(act, dyh, after, *, name):
    _, t, nf = act.shape
    d = dyh.shape[1]
    tm = _row_tile(t, TN_TILE)
    d_w_out = mm_tn(act, dyh, nb=4, ka=nf, nbk=d, tm=tm, m=t, after=after,
                    a_spec=pl.BlockSpec((None, tm, nf), lambda s, i: (s, i, 0)),
                    b_spec=pl.BlockSpec((tm, d), lambda s, i: (i, 0)), name=name)
    return d_w_out.reshape(8, nf // 2, d)


def _lane_group(shape):
    return lax.shift_right_logical(lax.broadcasted_iota(jnp.int32, shape, 1), 6)


def _pool_count(t0, rows):
    t = (t0 + lax.broadcasted_iota(jnp.int32, (rows, MIX_W), 0) + 1).astype(F32)
    return jnp.minimum(t, _by_group(_lane_group((rows, MIX_W)), 2.0, 4.0, 8.0, 16.0))


def _by_group(grp, v0, v1, v2, v3):
    return jnp.where(grp == 0, v0, jnp.where(grp == 1, v1, jnp.where(grp == 2, v2, v3)))


def _sgu_mix(wt_ref, vnc):
    grp = _lane_group((SGU_CHUNK, MIX_W))
    out = jnp.zeros((SGU_CHUNK, MIX_W), F32)
    for hd in range(N_HEADS):
        out = jnp.where(grp == hd, _dot(wt_ref[hd], vnc), out)
    return out


def _pool_fwd(s1, s2, s3, t0, ts, lo):
    h = lo
    s2[h - 24:h + ts] = s1[h - 24:h + ts] + s1[h - 25:h + ts - 1]
    s3[h - 16:h + ts] = s2[h - 16:h + ts] + s2[h - 18:h + ts - 2]
    sum2 = s2[h:h + ts]
    sum4 = s3[h:h + ts]
    s2[h - 8:h + ts] = s3[h - 8:h + ts] + s3[h - 12:h + ts - 4]
    sum8 = s2[h:h + ts]
    sum16 = sum8 + s2[h - 8:h + ts - 8]
    grp = _lane_group((ts, MIX_W))
    return _by_group(grp, sum2, sum4, sum8, sum16) / _pool_count(t0, ts) - s1[h:h + ts]


def _make_shifts(src, sh, rows):
    for b in range(1, 8):
        sh[b, 0:rows] = src[b:b + rows]


def _rows_at(src, sh, start, n):
    a, b = divmod(start, 8)
    return src[8 * a:8 * a + n] if b == 0 else sh[b, 8 * a:8 * a + n]


def mixer_fwd(z, sconv, cconv, vecs, wt, bexp, pbd, x_res, wmo_g, *, name, ts=None):
    t = z.shape[0]
    ts = _row_tile(t, MIX_TILE if ts is None else ts)
    hl = HALO
    w = MIX_W
    nch = ts // SGU_CHUNK

    def body(zc, zp, sconv_ref, cconv_ref, vec_ref, wt_ref, bexp_ref, pbd_ref, xr_ref, wmo_ref,
             y_ref, xo_ref, s1, s2, s3, sh):
        i = pl.program_id(0)
        has_prev = i > 0

        def col(ref, c):
            return ref[:, c * w:(c + 1) * w]

        def prev(c):
            return jnp.where(has_prev, col(zp, c), 0.0)

        s1[0:hl] = prev(1) * prev(2)
        s1[hl:hl + ts] = col(zc, 1) * col(zc, 2)
        cv = sconv_ref[0:1] * s1[hl - 2:hl - 2 + ts]
        for k in range(1, SCONV_K):
            cv = cv + sconv_ref[k:k + 1] * s1[hl - 2 + k:hl - 2 + k + ts]
        y_ref[:, 0:w] = (col(zc, 0) * cv).astype(BF16)

        xhat, _ = _ln_stats(col(zc, 4))
        vn = (xhat * vec_ref[0:1]).astype(BF16)
        for c in range(nch):
            rows = slice(c * SGU_CHUNK, (c + 1) * SGU_CHUNK)
            mixed = _sgu_mix(wt_ref, vn[rows]) + bexp_ref[...]
            y_ref[rows, w:2 * w] = (zc[rows, 3 * w:4 * w] * mixed).astype(BF16)

        s1[0:hl] = prev(5) * _sigmoid(prev(6))
        s1[hl:hl + ts] = col(zc, 5) * _sigmoid(col(zc, 6))
        off = hl - (CCONV_K - 1)
        _make_shifts(s1, sh, hl + ts - 8)
        cv = cconv_ref[0:1] * _rows_at(s1, sh, off, ts)
        for k in range(1, CCONV_K):
            cv = cv + cconv_ref[k:k + 1] * _rows_at(s1, sh, off + k, ts)
        xhat, _ = _ln_stats(cv)
        ln = xhat * vec_ref[1:2] + vec_ref[2:3]
        y_ref[:, 2 * w:3 * w] = (ln * _sigmoid(ln)).astype(BF16)

        s1[0:hl] = prev(7)
        s1[hl:hl + ts] = col(zc, 7)
        pooled = _pool_fwd(s1, s2, s3, i * ts, ts, hl)
        y_ref[:, 3 * w:4 * w] = (_dot(pooled.astype(BF16), pbd_ref[...]) * vec_ref[3:4]).astype(BF16)

        xo_ref[...] = xr_ref[...] + _dot(y_ref[...], _full_weight(wmo_ref, "row"))

    full = lambda shape: pl.BlockSpec(shape, lambda i: (0,) * len(shape))
    row = lambda i: (i, 0)
    return pl.pallas_call(
        body, name=name, grid=(t // ts,),
        out_shape=[jax.ShapeDtypeStruct((t, 4 * w), BF16), jax.ShapeDtypeStruct((t, 4 * w), F32)],
        in_specs=[pl.BlockSpec((ts, 8 * w), row),
                  pl.BlockSpec((hl, 8 * w), lambda i: (jnp.maximum(i * (ts // hl) - 1, 0), 0)),
                  full((8, w)), full((32, w)), full((8, w)), full((N_HEADS, SGU_CHUNK, SGU_CHUNK)),
                  full((SGU_CHUNK, w)), full((w, w)), pl.BlockSpec((ts, 4 * w), row), _wspec(wmo_g)],
        out_specs=[pl.BlockSpec((ts, 4 * w), row), pl.BlockSpec((ts, 4 * w), row)],
        scratch_shapes=[pltpu.VMEM((hl + ts, w), F32)] * 3 + [pltpu.VMEM((8, hl + ts, w), F32)],
        compiler_params=_params("parallel"),
    )(z, z, sconv, cconv, vecs, wt, bexp, pbd, x_res, wmo_g)


def mixer_bwd(z, dx, wmo_g, sconv, cconv, vecs, wt, bexp, pbd, *, name, ts=None):
    t = z.shape[0]
    ts = _row_tile(t, MIX_TILE if ts is None else ts)
    hl = HALO
    w = MIX_W
    nch = ts // SGU_CHUNK
    ni = t // ts
    ext = ts + hl

    def body(zc, zp, zn, dxc, dxn_, wmo_ref, sconv_ref, cconv_ref, vec_ref, wt_ref, bexp_ref, pbd_ref,
             dz_ref, gvec_ref, gcc_ref, gwt_ref, gb_ref, gpbd_ref, s1, s2, s3, sh1, sh3, dyc, dyn):
        i = pl.program_id(0)
        has_prev = i > 0
        has_next = i < ni - 1
        wmo = _full_weight(wmo_ref, "row")
        dyc[...] = _dot_nt(dxc[...].astype(BF16), wmo)
        dyn[...] = _dot_nt(dxn_[...].astype(BF16), wmo)

        @pl.when(i == 0)
        def _():
            gvec_ref[...] = jnp.zeros_like(gvec_ref)
            gcc_ref[...] = jnp.zeros_like(gcc_ref)
            gwt_ref[...] = jnp.zeros_like(gwt_ref)
            gb_ref[...] = jnp.zeros_like(gb_ref)
            gpbd_ref[...] = jnp.zeros_like(gpbd_ref)

        def col(ref, c):
            return ref[:, c * w:(c + 1) * w]

        def prev(c):
            return jnp.where(has_prev, col(zp, c), 0.0)

        def nxt(c):
            return jnp.where(has_next, col(zn, c), 0.0)

        def dnext(c):
            return jnp.where(has_next, col(dyn, c), 0.0)

        def rowsum(v):
            return jnp.sum(v, axis=0, keepdims=True)

        s1[0:hl] = prev(1) * prev(2)
        s1[hl:hl + ts] = col(zc, 1) * col(zc, 2)
        s1[hl + ts:hl + ts + hl] = nxt(1) * nxt(2)
        cv = sconv_ref[0:1] * s1[hl - 2:hl - 2 + ts]
        for k in range(1, SCONV_K):
            cv = cv + sconv_ref[k:k + 1] * s1[hl - 2 + k:hl - 2 + k + ts]
        dya = col(dyc, 0)
        dz_ref[:, 0:w] = (dya * cv).astype(BF16)
        s2[0:ts] = dya * col(zc, 0)
        s2[ts:ext] = dnext(0) * nxt(0)
        dv = sconv_ref[0:1] * s2[2:2 + ts]
        for k in range(1, SCONV_K):
            dv = dv + sconv_ref[k:k + 1] * s2[2 - k:2 - k + ts]
        dz_ref[:, w:2 * w] = (dv * col(zc, 2)).astype(BF16)
        dz_ref[:, 2 * w:3 * w] = (dv * col(zc, 1)).astype(BF16)
        dcv = s2[0:ts]
        for k in range(SCONV_K):
            gvec_ref[k:k + 1] += rowsum(dcv * s1[hl - 2 + k:hl - 2 + k + ts])

        g_sgu = vec_ref[0:1]
        xhat, rstd = _ln_stats(col(zc, 4))
        vn = (xhat * g_sgu).astype(BF16)
        grp = _lane_group((SGU_CHUNK, w))
        lane = lax.broadcasted_iota(jnp.int32, (SGU_CHUNK, SGU_CHUNK), 1)
        tril = lax.broadcasted_iota(jnp.int32, (SGU_CHUNK, SGU_CHUNK), 0) >= lane
        for c in range(nch):
            rows = slice(c * SGU_CHUNK, (c + 1) * SGU_CHUNK)
            vnc = vn[rows]
            mixed = _sgu_mix(wt_ref, vnc) + bexp_ref[...]
            dyb = dyc[rows, w:2 * w]
            dz_ref[rows, 3 * w:4 * w] = (dyb * mixed).astype(BF16)
            dmix = dyb * zc[rows, 3 * w:4 * w]
            dmixb = dmix.astype(BF16)
            dvn = jnp.zeros((SGU_CHUNK, w), F32)
            gb = jnp.zeros((SGU_CHUNK, SGU_CHUNK), F32)
            for hd in range(N_HEADS):
                dvn = jnp.where(grp == hd, _dot_tn(wt_ref[hd], dmixb), dvn)
                dm_h = jnp.where(grp == hd, dmix, 0.0)
                gwt_ref[hd] += jnp.where(tril, _dot_nt(dm_h.astype(BF16), vnc), 0.0)
                gb = gb + jnp.where(lane == hd, jnp.sum(dm_h, axis=1, keepdims=True), 0.0)
            gb_ref[...] += gb
            s3[rows] = dvn
        dvn = s3[0:ts]
        gvec_ref[3:4] += rowsum(dvn * xhat)
        dz_ref[:, 4 * w:5 * w] = _ln_bwd(xhat, rstd, dvn * g_sgu).astype(BF16)

        sig_c = _sigmoid(col(zc, 6))
        s1[0:hl] = prev(5) * _sigmoid(prev(6))
        s1[hl:hl + ts] = col(zc, 5) * sig_c
        s1[hl + ts:hl + ts + hl] = nxt(5) * _sigmoid(nxt(6))
        off = hl - (CCONV_K - 1)
        _make_shifts(s1, sh1, ts + 2 * hl - 8)
        cv = cconv_ref[0:1] * _rows_at(s1, sh1, off, ext)
        for k in range(1, CCONV_K):
            cv = cv + cconv_ref[k:k + 1] * _rows_at(s1, sh1, off + k, ext)
        xhat, rstd = _ln_stats(cv)
        ln = xhat * vec_ref[1:2] + vec_ref[2:3]
        sg = _sigmoid(ln)
        s2[0:ts] = col(dyc, 2)
        s2[ts:ext] = dnext(2)
        dln = s2[0:ext] * (sg * (1.0 + ln * (1.0 - sg)))
        gvec_ref[4:5] += rowsum(dln[0:ts] * xhat[0:ts])
        gvec_ref[5:6] += rowsum(dln[0:ts])
        s3[0:ext] = _ln_bwd(xhat, rstd, dln * vec_ref[1:2])
        _make_shifts(s3, sh3, ext - 8)
        dyg = cconv_ref[0:1] * _rows_at(s3, sh3, CCONV_K - 1, ts)
        for k in range(1, CCONV_K):
            dyg = dyg + cconv_ref[k:k + 1] * _rows_at(s3, sh3, CCONV_K - 1 - k, ts)
        dz_ref[:, 5 * w:6 * w] = (dyg * sig_c).astype(BF16)
        dz_ref[:, 6 * w:7 * w] = (dyg * col(zc, 5) * sig_c * (1.0 - sig_c)).astype(BF16)
        dcv = s3[0:ts]
        for k in range(CCONV_K):
            gcc_ref[k:k + 1] += rowsum(dcv * _rows_at(s1, sh1, off + k, ts))

        scale = vec_ref[3:4]
        s1[0:hl] = prev(7)
        s1[hl:hl + ts] = col(zc, 7)
        pooled = _pool_fwd(s1, s2, s3, i * ts, ts, hl).astype(BF16)
        q0 = _dot(pooled, pbd_ref[...])
        dyd = col(dyc, 3)
        gvec_ref[6:7] += rowsum(dyd * q0)
        dq = (dyd * scale).astype(BF16)
        gpbd_ref[...] += _dot_tn(pooled, dq)
        s1[0:ts] = _dot_nt(dq, pbd_ref[...])
        s1[ts:ext] = _dot_nt((dnext(3) * scale).astype(BF16), pbd_ref[...])
        dpool = s1[0:ts]
        s2[0:ext] = s1[0:ext] / _pool_count(i * ts, ext)
        s3[0:ts + 24] = s2[0:ts + 24] + s2[1:ts + 25]
        f2 = s3[0:ts]
        s2[0:ts + 16] = s3[0:ts + 16] + s3[2:ts + 18]
        f4 = s2[0:ts]
        s3[0:ts + 8] = s2[0:ts + 8] + s2[4:ts + 12]
        f8 = s3[0:ts]
        f16 = f8 + s3[8:ts + 8]
        dz_ref[:, 7 * w:8 * w] = (_by_group(_lane_group((ts, w)), f2, f4, f8, f16) - dpool).astype(BF16)

    full = lambda shape: pl.BlockSpec(shape, lambda i: (0,) * len(shape))
    r = ts // hl
    prev_map = lambda i: (jnp.maximum(i * r - 1, 0), 0)
    next_map = lambda i: (jnp.minimum((i + 1) * r, t // hl - 1), 0)
    return pl.pallas_call(
        body, name=name, grid=(ni,),
        out_shape=[jax.ShapeDtypeStruct((t, 8 * w), BF16), jax.ShapeDtypeStruct((8, w), F32),
                   jax.ShapeDtypeStruct((32, w), F32),
                   jax.ShapeDtypeStruct((N_HEADS, SGU_CHUNK, SGU_CHUNK), F32),
                   jax.ShapeDtypeStruct((SGU_CHUNK, SGU_CHUNK), F32), jax.ShapeDtypeStruct((w, w), F32)],
        in_specs=[pl.BlockSpec((ts, 8 * w), lambda i: (i, 0)),
                  pl.BlockSpec((hl, 8 * w), prev_map), pl.BlockSpec((hl, 8 * w), next_map),
                  pl.BlockSpec((ts, 4 * w), lambda i: (i, 0)), pl.BlockSpec((hl, 4 * w), next_map),
                  _wspec(wmo_g),
                  full((8, w)), full((32, w)), full((8, w)), full((N_HEADS, SGU_CHUNK, SGU_CHUNK)),
                  full((SGU_CHUNK, w)), full((w, w))],
        out_specs=[pl.BlockSpec((ts, 8 * w), lambda i: (i, 0)), full((8, w)), full((32, w)),
                   full((N_HEADS, SGU_CHUNK, SGU_CHUNK)), full((SGU_CHUNK, SGU_CHUNK)), full((w, w))],
        scratch_shapes=[pltpu.VMEM((ts + 2 * hl, w), F32)] * 3 + [pltpu.VMEM((8, ts + 2 * hl, w), F32)] * 2
        + [pltpu.VMEM((ts, 4 * w), F32), pltpu.VMEM((hl, 4 * w), F32)],
        compiler_params=_params("arbitrary"),
    )(z, z, z, dx, dx, wmo_g, sconv, cconv, vecs, wt, bexp, pbd)


def _attn_head(q, kv_ref, hd, d):
    hw = d // N_HEADS
    qh = q[:, hd * hw:(hd + 1) * hw]
    kh = kv_ref[:, hd * hw:(hd + 1) * hw].astype(BF16)
    vh = kv_ref[:, d + hd * hw:d + (hd + 1) * hw].astype(BF16)
    s = _dot_nt(qh, kh) * (1.0 / (hw ** 0.5))
    e = jnp.exp(s - jnp.max(s, axis=-1, keepdims=True))
    p = e / jnp.sum(e, axis=-1, keepdims=True)
    return qh, kh, vh, p


def xattn_fwd(x, gain, kv, wq_g, wo_g, *, name, tm=None):
    t, d = x.shape
    nm = kv.shape[0]
    tm = _row_tile(t, tm)
    hw = d // N_HEADS

    def body(x_ref, g_ref, kv_ref, wq_ref, wo_ref, o_ref):
        xv = x_ref[...]
        xhat, _ = _rms_fwd(xv, None)
        h = (xhat * g_ref[...]).astype(BF16)
        q = _dot(h, _full_weight(wq_ref, "row")).astype(BF16)
        wo = _full_weight(wo_ref, "row")
        out = xv
        for hd in range(N_HEADS):
            _, _, vh, p = _attn_head(q, kv_ref, hd, d)
            oh = _dot(p.astype(BF16), vh).astype(BF16)
            out = out + _dot(oh, wo[hd * hw:(hd + 1) * hw])
        o_ref[...] = out

    row = lambda i: (i, 0)
    return pl.pallas_call(
        body, name=name, grid=(t // tm,),
        out_shape=jax.ShapeDtypeStruct((t, d), F32),
        in_specs=[pl.BlockSpec((tm, d), row), pl.BlockSpec((1, d), lambda i: (0, 0)),
                  pl.BlockSpec((nm, 2 * d), lambda i: (0, 0)), _wspec(wq_g), _wspec(wo_g)],
        out_specs=pl.BlockSpec((tm, d), row),
        compiler_params=_params("parallel"),
    )(x, gain.reshape(1, d), kv, wq_g, wo_g)


def xattn_bwd_rows(x, dxn, gain, kv, wq_g, wo_g, after, *, name, tm=None):
    t, d = x.shape
    nm = kv.shape[0]
    tm = _row_tile(t, tm)
    hw = d // N_HEADS

    def body(x_ref, dxn_ref, g_ref, kv_ref, wq_ref, wo_ref, after_ref,
             dx_ref, h_ref, dq_ref, o_ref, dkv_ref, dg_ref):
        i = pl.program_id(0)

        @pl.when(i == 0)
        def _():
            dkv_ref[...] = jnp.zeros_like(dkv_ref)
            dg_ref[...] = jnp.zeros_like(dg_ref)
        g = g_ref[...]
        xhat, r = _rms_fwd(x_ref[...], None)
        h = (xhat * g).astype(BF16)
        h_ref[...] = h
        wq = _full_weight(wq_ref, "row")
        q = _dot(h, wq).astype(BF16)
        dxn = dxn_ref[...]
        do = _dot_nt(dxn.astype(BF16), _full_weight(wo_ref, "row")).astype(BF16)
        for hd in range(N_HEADS):
            cols = slice(hd * hw, (hd + 1) * hw)
            qh, kh, vh, p = _attn_head(q, kv_ref, hd, d)
            pb = p.astype(BF16)
            o_ref[:, cols] = _dot(pb, vh).astype(BF16)
            doh = do[:, cols]
            dkv_ref[:, d + hd * hw:d + (hd + 1) * hw] += _dot_tn(pb, doh)
            dp = _dot_nt(doh, vh)
            ds = (p * (dp - jnp.sum(dp * p, axis=-1, keepdims=True)) * (1.0 / (hw ** 0.5))).astype(BF16)
            dq_ref[:, cols] = _dot(ds, kh).astype(BF16)
            dkv_ref[:, cols] += _dot_tn(ds, qh)
        dh = _dot_nt(dq_ref[...], wq)
        dx, dg = _rms_bwd(xhat, r, g, dh)
        dx_ref[...] = dxn + dx
        dg_ref[...] += dg

    row = lambda i: (i, 0)
    fix = lambda i: (0, 0)
    return pl.pallas_call(
        body, name=name, grid=(t // tm,),
        out_shape=[jax.ShapeDtypeStruct((t, d), F32), jax.ShapeDtypeStruct((t, d), BF16),
                   jax.ShapeDtypeStruct((t, d), BF16), jax.ShapeDtypeStruct((t, d), BF16),
                   jax.ShapeDtypeStruct((nm, 2 * d), F32), jax.ShapeDtypeStruct((1, d), F32)],
        in_specs=[pl.BlockSpec((tm, d), row), pl.BlockSpec((tm, d), row), pl.BlockSpec((1, d), fix),
                  pl.BlockSpec((nm, 2 * d), fix), _wspec(wq_g), _wspec(wo_g), ANY_SPEC],
        out_specs=[pl.BlockSpec((tm, d), row)] * 4 + [pl.BlockSpec((nm, 2 * d), fix),
                                                      pl.BlockSpec((1, d), fix)],
        compiler_params=_params("arbitrary"),
    )(x, dxn, gain.reshape(1, d), kv, wq_g, wo_g, after)


def loss_head(x, target, gain, *, name, tm=None):
    t, d = x.shape
    tm = _row_tile(t, tm)

    def body(x_ref, t_ref, g_ref, dx_ref, dg_ref, loss_ref):
        @pl.when(pl.program_id(0) == 0)
        def _():
            dg_ref[...] = jnp.zeros_like(dg_ref)
            loss_ref[...] = jnp.zeros_like(loss_ref)
        g = g_ref[...]
        xhat, r = _rms_fwd(x_ref[...], None)
        err = xhat * g - t_ref[...]
        loss_ref[...] += 0.5 * jnp.sum(jnp.sum(err * err, axis=-1, keepdims=True) / d,
                                       axis=0, keepdims=True)
        dx, dg = _rms_bwd(xhat, r, g, err / d)
        dx_ref[...] = dx
        dg_ref[...] += dg

    row = lambda i: (i, 0)
    fix = lambda i: (0, 0)
    return pl.pallas_call(
        body, name=name, grid=(t // tm,),
        out_shape=[jax.ShapeDtypeStruct((t, d), F32), jax.ShapeDtypeStruct((1, d), F32),
                   jax.ShapeDtypeStruct((1, 1), F32)],
        in_specs=[pl.BlockSpec((tm, d), row), pl.BlockSpec((tm, d), row), pl.BlockSpec((1, d), fix)],
        out_specs=[pl.BlockSpec((tm, d), row), pl.BlockSpec((1, d), fix), pl.BlockSpec((1, 1), fix)],
        compiler_params=_params("arbitrary"),
    )(x, target, gain.reshape(1, d))


def _adamw_math(w, g, m, v):
    m = ADAM_B1 * m + (1.0 - ADAM_B1) * g
    v = ADAM_B2 * v + (1.0 - ADAM_B2) * (g * g)
    m_hat = m / (1.0 - ADAM_B1 ** ADAM_STEP)
    v_hat = v / (1.0 - ADAM_B2 ** ADAM_STEP)
    delta = -ADAM_LR * (m_hat / (jnp.sqrt(v_hat) + ADAM_EPS) + ADAM_WD * w)
    return delta, m, v


def adamw_sharded(own, lands, w, m, v, me_arr, *, name):
    nl, r, c = w.shape
    assert nl == len(own) == len(lands) == 2
    tr = next(cand for cand in (256, 176, 128, r) if r % cand == 0)
    nr = r // tr

    def body(me_ref, o0, o1, l0, l1, w_ref, m_ref, v_ref, g_out, d_out, m_out, v_out):
        def total(o_ref, l_ref):
            acc = o_ref[...].astype(F32)
            for p in range(N_DEV - 1):
                acc = acc + l_ref[p].astype(F32)
            return acc
        g = jnp.where(pl.program_id(0) == 0, total(o0, l0), total(o1, l1))
        delta, mn, vn = _adamw_math(w_ref[...], g, m_ref[...], v_ref[...])
        g_out[...] = g
        d_out[...] = delta
        m_out[...] = mn
        v_out[...] = vn

    row0 = lambda l, i: jnp.where(l == 0, i, nr - 1)
    row1 = lambda l, i: jnp.where(l == 1, i, 0)
    blk = pl.BlockSpec((None, tr, c), lambda l, i, me: (l, i, 0))
    grid_spec = pltpu.PrefetchScalarGridSpec(
        num_scalar_prefetch=1, grid=(nl, nr),
        in_specs=[pl.BlockSpec((None, tr, c), lambda l, i, me: (me[0], row0(l, i), 0)),
                  pl.BlockSpec((None, tr, c), lambda l, i, me: (me[0], row1(l, i), 0)),
                  pl.BlockSpec((N_DEV - 1, tr, c), lambda l, i, me: (0, row0(l, i), 0)),
                  pl.BlockSpec((N_DEV - 1, tr, c), lambda l, i, me: (0, row1(l, i), 0)),
                  blk, blk, blk],
        out_specs=[blk] * 4)
    return pl.pallas_call(
        body, name=name, grid_spec=grid_spec,
        out_shape=[jax.ShapeDtypeStruct((nl, r, c), F32)] * 4,
        compiler_params=_params("arbitrary", "arbitrary"),
    )(me_arr, own[0], own[1], lands[0], lands[1], w, m, v)


def adamw_many(gs, ws, ms, vs, *, name):
    n = len(ws)
    shapes = [w.shape for w in ws]
    as2d = lambda a: a.reshape(1, -1) if a.ndim == 1 else a

    def body(*refs):
        g_r, w_r, m_r, v_r = refs[:n], refs[n:2 * n], refs[2 * n:3 * n], refs[3 * n:4 * n]
        outs = refs[4 * n:]
        for i in range(n):
            delta, mn, vn = _adamw_math(w_r[i][...], g_r[i][...], m_r[i][...], v_r[i][...])
            outs[3 * i][...] = delta
            outs[3 * i + 1][...] = mn
            outs[3 * i + 2][...] = vn

    operands = [as2d(a) for group in (gs, ws, ms, vs) for a in group]
    out_shape = [jax.ShapeDtypeStruct(as2d(w).shape, F32) for w in ws for _ in range(3)]
    outs = pl.pallas_call(
        body, name=name, out_shape=out_shape,
        in_specs=[VMEM_SPEC] * (4 * n), out_specs=[VMEM_SPEC] * (3 * n),
        compiler_params=pltpu.CompilerParams(vmem_limit_bytes=VMEM_LIMIT),
    )(*operands)
    return [tuple(outs[3 * i + k].reshape(shapes[i]) for k in range(3)) for i in range(n)]


def cast_into_slot(a, layer, me_arr, *, name, dtype=None, after=None):
    dtype = BF16 if dtype is None else dtype
    _, r, c = a.shape
    tr = next(cand for cand in (256, 176, 128, r) if r % cand == 0)
    extra = [] if after is None else [after]

    def body(me_ref, a_ref, *rest):
        rest[-1][...] = a_ref[...].astype(dtype)

    grid_spec = pltpu.PrefetchScalarGridSpec(
        num_scalar_prefetch=1, grid=(r // tr,),
        in_specs=[pl.BlockSpec((None, tr, c), lambda i, me: (layer, i, 0))] + [ANY_SPEC] * len(extra),
        out_specs=pl.BlockSpec((None, tr, c), lambda i, me: (me[0], i, 0)))
    return pl.pallas_call(
        body, name=name, grid_spec=grid_spec,
        out_shape=jax.ShapeDtypeStruct((N_DEV, r, c), dtype),
        compiler_params=_params("parallel"),
    )(me_arr, a, *extra)


def _pack(arrs, rows):
    flat = jnp.concatenate([a.reshape(-1).astype(F32) for a in arrs])
    pad = rows * 128 - flat.shape[0]
    assert pad >= 0
    if pad:
        flat = jnp.concatenate([flat, jnp.zeros((pad,), F32)])
    return flat.reshape(rows, 128)


def _unpack(packed, shapes):
    flat = packed.reshape(-1)
    out, pos = [], 0
    for s in shapes:
        n = 1
        for dim in s:
            n *= dim
        out.append(flat[pos:pos + n].reshape(s))
        pos += n
    return out


def _rows_for(shapes):
    n = 0
    for s in shapes:
        k = 1
        for dim in s:
            k *= dim
        n += k
    return -(-n // 1024) * 8


GATHER_GROUPS = (("ffn1", ("ffn1_w_in", "ffn1_w_out")),
                 ("mid", ("mix_w_in", "mix_w_out", "xattn_wkv", "xattn_wq", "xattn_wo")),
                 ("ffn2", ("ffn2_w_in", "ffn2_w_out")))
SMALL_REPL = ["norm_ffn1", "norm_mix", "sgu_norm_g", "sgu_w", "sgu_b", "cconv_ln_g", "cconv_ln_b",
              "pool_w", "pool_scale", "norm_xattn", "norm_mem", "norm_ffn2", "norm_final"]
SMALL_SHARD = ["sconv_w", "cconv_w"]
TRANSPOSED = ("ffn1_w_in", "ffn2_w_in")
WEIGHTS = ["norm_ffn1", "ffn1_w_in", "ffn1_w_out", "norm_mix", "mix_w_in", "sconv_w", "sgu_norm_g",
           "sgu_w", "sgu_b", "cconv_w", "cconv_ln_g", "cconv_ln_b", "pool_w", "pool_scale", "mix_w_out",
           "norm_xattn", "norm_mem", "xattn_wq", "xattn_wkv", "xattn_wo", "norm_ffn2", "ffn2_w_in",
           "ffn2_w_out", "norm_final"]


def kernel(x, mem, norm_ffn1, ffn1_w_in, ffn1_w_out, norm_mix, mix_w_in, sconv_w, sgu_norm_g, sgu_w, sgu_b, cconv_w, cconv_ln_g, cconv_ln_b, pool_w, pool_scale, mix_w_out, norm_xattn, norm_mem, xattn_wq, xattn_wkv, xattn_wo, norm_ffn2, ffn2_w_in, ffn2_w_out, norm_final, loss_target, m_norm_ffn1, m_ffn1_w_in, m_ffn1_w_out, m_norm_mix, m_mix_w_in, m_sconv_w, m_sgu_norm_g, m_sgu_w, m_sgu_b, m_cconv_w, m_cconv_ln_g, m_cconv_ln_b, m_pool_w, m_pool_scale, m_mix_w_out, m_norm_xattn, m_norm_mem, m_xattn_wq, m_xattn_wkv, m_xattn_wo, m_norm_ffn2, m_ffn2_w_in, m_ffn2_w_out, m_norm_final, v_norm_ffn1, v_ffn1_w_in, v_ffn1_w_out, v_norm_mix, v_mix_w_in, v_sconv_w, v_sgu_norm_g, v_sgu_w, v_sgu_b, v_cconv_w, v_cconv_ln_g, v_cconv_ln_b, v_pool_w, v_pool_scale, v_mix_w_out, v_norm_xattn, v_norm_mem, v_xattn_wq, v_xattn_wkv, v_xattn_wo, v_norm_ffn2, v_ffn2_w_in, v_ffn2_w_out, v_norm_final):
    args = dict(locals())
    wts = {n: args[n] for n in WEIGHTS}
    mom = {n: args["m_" + n] for n in WEIGHTS}
    var = {n: args["v_" + n] for n in WEIGHTS}
    for n in TRANSPOSED:
        wts[n], mom[n], var[n] = (jnp.swapaxes(a, 1, 2) for a in (wts[n], mom[n], var[n]))
    x0 = x[0]
    mem0 = mem[0]
    target = loss_target[0]
    t, d = x0.shape
    nl = norm_ffn1.shape[0]
    w = MIX_W
    me = _my_index()

    me_arr = jnp.reshape(me, (1,)).astype(jnp.int32)

    small_g = all_gather([sconv_w, cconv_w], name="gather_conv_taps")
    sconv_full = jnp.transpose(small_g[0], (1, 2, 0, 3)).reshape(nl, SCONV_K, w)
    cconv_full = jnp.transpose(small_g[1], (1, 2, 0, 3)).reshape(nl, CCONV_K, w)
    pending = {}
    token = small_g[1]
    masks = GATHER_MASKS
    keys = [(gname, l, members) for l in range(nl) for gname, members in GATHER_GROUPS]
    first = [[cast_into_slot(wts[n], keys[0][1], me_arr, name=f"cast_{n}{keys[0][1]}") for n in keys[0][2]]]
    started, token = gather_start_groups(first, token, name="gather_start_first", masks=masks)
    casts = [[cast_into_slot(wts[n], l, me_arr, name=f"cast_{n}{l}", after=token) for n in members]
             for gname, l, members in keys[1:]]
    rest, token = gather_start_groups(casts, token, name="gather_start_rest", masks=masks)
    for (gname, l, members), (send, recv, gs) in zip(keys, started + rest):
        pending[gname, l] = (members, gs, send, recv, masks)
    wg = [dict() for _ in range(nl)]

    handing_over = {}

    def arrive_early(gname, l, after):
        members, gs, send, recv, masks = pending.pop((gname, l))
        gs = gather_wait(gs, send, recv, after, name=f"gather_wait_{gname}{l}", masks=masks)
        fsend, frecv, gs, _ = forward_start(gs, after, name=f"gather_forward_start_{gname}{l}")
        handing_over[gname, l] = (members, gs, fsend, frecv)

    def arrive(gname, l, after):
        if (gname, l) in handing_over:
            members, gs, fsend, frecv = handing_over.pop((gname, l))
            gs = forward_wait(gs, fsend, frecv, after, name=f"gather_forward_wait_{gname}{l}")
        else:
            members, gs, send, recv, masks = pending.pop((gname, l))
            gs = gather_wait(gs, send, recv, after, name=f"gather_wait_{gname}{l}", masks=masks)
            gs = sibling_forward(gs, name=f"gather_forward_{gname}{l}")
        wg[l].update(zip(members, gs))
    sconv_pad = jnp.pad(sconv_full, ((0, 0), (0, 8 - SCONV_K), (0, 0)))
    cconv_pad = jnp.pad(cconv_full, ((0, 0), (0, 32 - CCONV_K), (0, 0)))
    zeros_w = jnp.zeros((nl, w), F32)
    vecs = jnp.stack([sgu_norm_g, cconv_ln_g, cconv_ln_b, pool_scale] + [zeros_w] * 4, axis=1)
    wt = jnp.tril(sgu_w).astype(BF16)
    bexp = jnp.repeat(jnp.swapaxes(sgu_b, 1, 2), w // N_HEADS, axis=2)
    eye = jnp.eye(4, dtype=F32)
    pbd = jnp.einsum("lgcd,gh->lgchd", pool_w, eye).reshape(nl, w, w).astype(BF16)

    def mixer_args(l):
        return sconv_pad[l], cconv_pad[l], vecs[l], wt[l], bexp[l], pbd[l]

    saved = []
    xc = x0
    after = token
    for l in range(nl):
        s = {"x_ffn1": xc}
        arrive("ffn1", l, after)
        xc, s["gu_ffn1"] = ffn_fwd(xc, norm_ffn1[l], wg[l]["ffn1_w_in"], wg[l]["ffn1_w_out"],
                                   name=f"ffn1_fwd{l}", tm=FFN_FWD_TILE)
        s["x_mix"] = xc
        arrive("mid", l, xc)
        z = mm_rows(xc, wg[l]["mix_w_in"], "col", gain=norm_mix[l], name=f"mix_in{l}")
        y, xc = mixer_fwd(z, *mixer_args(l), xc, wg[l]["mix_w_out"], name=f"mixer_fwd{l}")
        s["z"], s["y"] = z, y
        s["x_att"] = xc
        kv = mm_rows(mem0, wg[l]["xattn_wkv"], "col", gain=norm_mem[l], name=f"kv{l}")
        s["kv"] = kv
        arrive_early("ffn2", l, kv)
        xc = xattn_fwd(xc, norm_xattn[l], kv, wg[l]["xattn_wq"], wg[l]["xattn_wo"], name=f"xattn_fwd{l}")
        s["x_ffn2"] = xc
        arrive("ffn2", l, xc)
        xc, s["gu_ffn2"] = ffn_fwd(xc, norm_ffn2[l], wg[l]["ffn2_w_in"], wg[l]["ffn2_w_out"],
                                   name=f"ffn2_fwd{l}", tm=FFN_FWD_TILE)
        after = xc
        saved.append(s)

    dx, g_norm_final, loss_local = loss_head(xc, target, norm_final, name="loss_head")

    tm = _row_tile(t, TN_TILE)
    small ={n: [None] * nl for n in SMALL_REPL + SMALL_SHARD if n != "norm_final"}
    scattered = {}
    tie = [token]

    def send_grads(gname, l, grads):
        members = list(grads)
        send, recv, gs, lands, tie[0] = scatter_start(
            [grads[n] for n in members], tie[0], name=f"scatter_start_{gname}{l}")
        scattered[gname, l] = (members, gs, lands, send, recv)

    names = SMALL_REPL + SMALL_SHARD + ["loss"]
    small_pending = []

    def start_small():
        small_full = {n: jnp.stack(v) for n, v in small.items()}
        small_full["norm_final"] = g_norm_final[0]
        small_full["loss"] = loss_local[0]
        shapes = [small_full[n].shape for n in names]
        packed = _pack([small_full[n] for n in names], _rows_for(shapes))
        slot = cast_into_slot(packed[None], 0, me_arr, name="small_into_slot", dtype=F32)
        send, recv, gs, tie[0] = gather_start([slot], tie[0], name="small_gather_start", masks=ALL_MASKS)
        small_pending.append((gs, send, recv, shapes))

    def ffn_backward(which, l, x_in, dy, gu, gain):
        w_in, w_out = wg[l][which + "_w_in"], wg[l][which + "_w_out"]
        dx_, h_, act, dgu, dgn, dyh = ffn_bwd_rows(x_in, dy, gu, gain, w_in, w_out, tie[0],
                                                   name=f"{which}_bwd{l}_rows")
        small["norm_" + which][l] = dgn[0]
        last = which == "ffn1" and l == 0
        if last:
            start_small()
        g_in = ffn_grad_w_in(h_, dgu, tie[0], name=f"{which}_bwd{l}_dwin")
        if last:
            send_grads(which + "_in", l, {which + "_w_in": g_in})
        g_out = ffn_grad_w_out(act, dyh, tie[0], name=f"{which}_bwd{l}_dwout")
        if last:
            send_grads(which + "_out", l, {which + "_w_out": g_out})
        else:
            send_grads(which, l, {which + "_w_in": g_in, which + "_w_out": g_out})
        return dx_

    for l in reversed(range(nl)):
        s = saved[l]
        wl = wg[l]
        dx = ffn_backward("ffn2", l, s["x_ffn2"], dx, s["gu_ffn2"], norm_ffn2[l])

        bg = {}
        dxn = dx
        dx, h, dq, o, dkv, dgn = xattn_bwd_rows(
            s["x_att"], dxn, norm_xattn[l], s["kv"], wl["xattn_wq"], wl["xattn_wo"], tie[0],
            name=f"xattn_bwd{l}")
        small["norm_xattn"][l] = dgn[0]
        row_spec = pl.BlockSpec((tm, d), lambda s_, i: (i, 0))
        bg["xattn_wq"] = mm_tn(h, dq, nb=1, ka=d, nbk=d, tm=tm, m=t, a_spec=row_spec, b_spec=row_spec,
                               name=f"dwq{l}").reshape(N_DEV, d // N_DEV, d)
        bg["xattn_wo"] = mm_tn(o, dxn, nb=1, ka=d, nbk=d, tm=tm, m=t, a_spec=row_spec, b_spec=row_spec,
                               name=f"dwo{l}").reshape(N_DEV, d // N_DEV, d)
        _, mhat, dgn = mm_nt(dkv, wl["xattn_wkv"], "col", x=mem0, gain=norm_mem[l], name=f"dmem{l}")
        small["norm_mem"][l] = dgn[0]
        nm = mem0.shape[0]
        bg["xattn_wkv"] = mm_tn(mhat, dkv, nb=N_DEV, ka=d, nbk=2 * d // N_DEV, tm=nm, m=nm,
                                a_spec=pl.BlockSpec((nm, d), lambda s_, i: (0, 0)),
                                b_spec=pl.BlockSpec((nm, 2 * d // N_DEV), lambda s_, i: (0, s_)),
                                name=f"dwkv{l}")

        dxn = dx
        bg["mix_w_out"] = mm_tn(s["y"], dxn, nb=1, ka=d, nbk=d, tm=tm, m=t, a_spec=row_spec,
                                b_spec=row_spec, name=f"dwmo{l}").reshape(N_DEV, d // N_DEV, d)
        dz, gvec, gcc, gwt, gb, gpbd = mixer_bwd(s["z"], dxn, wl["mix_w_out"], *mixer_args(l),
                                                 name=f"mixer_bwd{l}")
        small["sconv_w"][l] = gvec[0:SCONV_K]
        small["sgu_norm_g"][l] = gvec[3]
        small["cconv_ln_g"][l] = gvec[4]
        small["cconv_ln_b"][l] = gvec[5]
        small["pool_scale"][l] = gvec[6]
        small["cconv_w"][l] = gcc[0:CCONV_K]
        small["sgu_w"][l] = gwt
        small["sgu_b"][l] = jnp.transpose(gb[:, 0:N_HEADS])
        gw = w // 4
        small["pool_w"][l] = jnp.stack([gpbd[g * gw:(g + 1) * gw, g * gw:(g + 1) * gw] for g in range(4)])
        dx, h, dgn = mm_nt(dz, wl["mix_w_in"], "col", x=s["x_mix"], gain=norm_mix[l], dx_in=dxn,
                           after=tie[0], name=f"dh_mix{l}")
        small["norm_mix"][l] = dgn[0]
        th = _row_tile(t, TN_TILE // 2)
        bg["mix_w_in"] = mm_tn(h, dz, nb=1, ka=d, nbk=N_DEV * w, tm=th, m=t, col_slots=N_DEV,
                               a_spec=pl.BlockSpec((th, d), lambda s_, i: (i, 0)),
                               b_spec=pl.BlockSpec((th, N_DEV * w), lambda s_, i: (i, 0)), name=f"dwmi{l}")
        send_grads("mid", l, bg)

        dx = ffn_backward("ffn1", l, s["x_ffn1"], dx, s["gu_ffn1"], norm_ffn1[l])

    out = {}

    def finish(keys, after):
        own, land = {}, {}
        for gname, l in keys:
            members, gs, lands, send, recv = scattered.pop((gname, l))
            gs, lands = scatter_wait(gs, lands, send, recv, after, name=f"scatter_wait_{gname}{l}")
            for n, g_, l_ in zip(members, gs, lands):
                own.setdefault(n, {})[l] = g_
                land.setdefault(n, {})[l] = l_
        for n in own:
            out[n] = adamw_sharded([own[n][l] for l in range(nl)], [land[n][l] for l in range(nl)],
                                   wts[n], mom[n], var[n], me_arr, name="adamw_" + n)
            after = out[n][1]
        return after

    after = tie[0]
    for gname in ("ffn2", "mid"):
        after = finish([(gname, l) for l in reversed(range(nl))], after)
    (gs, send, recv, shapes), = small_pending
    gs = gather_wait(gs, send, recv, after, name="small_gather_wait", masks=ALL_MASKS)
    summed = sum_slots(gs[0], name="small_sum")
    gsm = dict(zip(names, _unpack(summed, shapes)))
    loss = gsm["loss"][0]
    finish([("ffn1", l) for l in reversed(range(1, nl))] + [("ffn1_in", 0), ("ffn1_out", 0)], summed)
    cs = w // N_DEV
    for n in SMALL_SHARD:
        gsm[n] = lax.dynamic_slice_in_dim(gsm[n], me * cs, cs, axis=2)
    small_names = SMALL_REPL + SMALL_SHARD
    upd = adamw_many([gsm[n] for n in small_names], [wts[n] for n in small_names],
                     [mom[n] for n in small_names], [var[n] for n in small_names], name="adamw_small")
    for n, (a, b, c) in zip(small_names, upd):
        out[n] = (gsm[n], a, b, c)
    for n in TRANSPOSED:
        out[n] = tuple(jnp.swapaxes(a, 1, 2) for a in out[n])

    grad_x = dx.reshape(1, t, d)
    return (loss, grad_x, *[out[n][0] for n in WEIGHTS], *[out[n][1] for n in WEIGHTS],
            *[out[n][2] for n in WEIGHTS], *[out[n][3] for n in WEIGHTS])
```

```python
import jax
import jax.numpy as jnp
from jax import lax
from jax.experimental import pallas as pl
from jax.experimental.pallas import tpu as pltpu

F32 = jnp.float32
BF16 = jnp.bfloat16
MESH = pl.DeviceIdType.MESH
N_DEV = 8
EPS = 1e-6
HALO = 32
SGU_CHUNK = 128
CCONV_K = 31
SCONV_K = 3
MIX_W = 256
N_HEADS = 4
VMEM_LIMIT = 56 * 1024 * 1024
ROW_TILE = 512
TN_TILE = 2048
FFN_FWD_TILE = 1024
FFN_BWD_SPLIT = 2
FFN_FWD_SPLIT = 2
MIX_TILE = 512

ADAM_LR = 0.001
ADAM_B1 = 0.9
ADAM_B2 = 0.999
ADAM_EPS = 1e-08
ADAM_WD = 0.01
ADAM_STEP = 10

HBM_SPEC = pl.BlockSpec(memory_space=pltpu.HBM)
VMEM_SPEC = pl.BlockSpec(memory_space=pltpu.VMEM)


def _params(*sem):
    return pltpu.CompilerParams(dimension_semantics=tuple(sem), vmem_limit_bytes=VMEM_LIMIT)


def _row_tile(m, pref=None):
    t = min(m, ROW_TILE if pref is None else pref)
    assert m % t == 0, (m, t)
    return t


def _my_index():
    return lax.axis_index("x") * 4 + lax.axis_index("y") * 2 + lax.axis_index("c")


def _peer(mask):
    x, y, c = lax.axis_index("x"), lax.axis_index("y"), lax.axis_index("c")
    px = 1 - x if mask & 4 else x
    py = 1 - y if mask & 2 else y
    pc = 1 - c if mask & 1 else c
    return (px, py, pc), px * 4 + py * 2 + pc


def all_gather(arrs, name):
    n = len(arrs)

    def body(*refs):
        ins, outs = refs[:n], refs[n:2 * n]
        send_sems, recv_sems, loc_sems = refs[2 * n:]
        me = _my_index()
        local = []
        for i in range(n):
            cp = pltpu.make_async_copy(ins[i], outs[i].at[me], loc_sems.at[i])
            cp.start()
            local.append(cp)
        sends = []
        for i in range(n):
            for m in range(1, N_DEV):
                peer, _ = _peer(m)
                cp = pltpu.make_async_remote_copy(
                    src_ref=ins[i], dst_ref=outs[i].at[me],
                    send_sem=send_sems.at[i, m - 1], recv_sem=recv_sems.at[i, m - 1],
                    device_id=peer, device_id_type=MESH)
                cp.start()
                sends.append(cp)
        for i in range(n):
            for m in range(1, N_DEV):
                peer, pidx = _peer(m)
                pltpu.make_async_remote_copy(
                    src_ref=ins[i], dst_ref=outs[i].at[pidx],
                    send_sem=send_sems.at[i, m - 1], recv_sem=recv_sems.at[i, m - 1],
                    device_id=peer, device_id_type=MESH).wait_recv()
        for cp in sends:
            cp.wait_send()
        for cp in local:
            cp.wait()

    return pl.pallas_call(
        body, name=name,
        out_shape=[jax.ShapeDtypeStruct((N_DEV,) + a.shape, a.dtype) for a in arrs],
        in_specs=[HBM_SPEC] * n, out_specs=[HBM_SPEC] * n,
        scratch_shapes=[pltpu.SemaphoreType.DMA((n, N_DEV - 1)),
                        pltpu.SemaphoreType.DMA((n, N_DEV - 1)),
                        pltpu.SemaphoreType.DMA((n,))],
    )(*arrs)


SEM_SPEC = pl.BlockSpec(memory_space=pltpu.SEMAPHORE)
ANY_SPEC = pl.BlockSpec(memory_space=pl.ANY)
SIDE_EFFECT = pltpu.SideEffectType.DATAFLOW_SIDE_EFFECTING


def _hbm(a):
    return pltpu.with_memory_space_constraint(a, pltpu.HBM)


def _sem_pairs(n):
    return (pltpu.SemaphoreType.DMA((n * (N_DEV - 1),)), pltpu.SemaphoreType.DMA((n * (N_DEV - 1),)))


def _sem(i, m):
    return i * (N_DEV - 1) + m - 1


def _gather_copy(g_ref, i, m, send_sems, recv_sems, origin):
    peer, _ = _peer(m)
    return pltpu.make_async_remote_copy(
        src_ref=g_ref.at[origin], dst_ref=g_ref.at[origin],
        send_sem=send_sems.at[_sem(i, m)], recv_sem=recv_sems.at[_sem(i, m)],
        device_id=peer, device_id_type=MESH)


GATHER_MASKS = (1, 2, 4, 6)
FORWARD_MASKS = (2, 4, 6)


ALL_MASKS = tuple(range(1, N_DEV))


def gather_start(gs, after, name, masks=GATHER_MASKS):
    n = len(gs)

    def body(*refs):
        g_in = refs[:n]
        send_sems, recv_sems = refs[n + 1], refs[n + 2]
        token = refs[-1]
        me = _my_index()
        for i in range(n):
            for m in masks:
                _gather_copy(g_in[i], i, m, send_sems, recv_sems, me).start()
        token[...] = jnp.zeros_like(token)

    outs = pl.pallas_call(
        body, name=name,
        out_shape=(*_sem_pairs(n), *[pltpu.HBM(g.shape, g.dtype) for g in gs],
                   jax.ShapeDtypeStruct((8, 128), F32)),
        in_specs=[HBM_SPEC] * n + [ANY_SPEC],
        out_specs=(SEM_SPEC, SEM_SPEC, *[HBM_SPEC] * n, VMEM_SPEC),
        input_output_aliases={i: 2 + i for i in range(n)},
        compiler_params=pltpu.CompilerParams(has_side_effects=SIDE_EFFECT),
    )(*[_hbm(g) for g in gs], after)
    return outs[0], outs[1], list(outs[2:2 + n]), outs[-1]


def gather_start_groups(groups, after, name, masks=GATHER_MASKS):
    sizes = [len(g) for g in groups]
    flat = [a for g in groups for a in g]
    n, ng = len(flat), len(groups)

    def body(*refs):
        g_in = refs[:n]
        sems = refs[n + 1:n + 1 + 2 * ng]
        token = refs[-1]
        me = _my_index()
        pos = 0
        for k, size in enumerate(sizes):
            for i in range(size):
                for m in masks:
                    _gather_copy(g_in[pos + i], i, m, sems[2 * k], sems[2 * k + 1], me).start()
            pos += size
        token[...] = jnp.zeros_like(token)

    outs = pl.pallas_call(
        body, name=name,
        out_shape=(*[s for size in sizes for s in _sem_pairs(size)],
                   *[pltpu.HBM(g.shape, g.dtype) for g in flat], jax.ShapeDtypeStruct((8, 128), F32)),
        in_specs=[HBM_SPEC] * n + [ANY_SPEC],
        out_specs=(*[SEM_SPEC] * (2 * ng), *[HBM_SPEC] * n, VMEM_SPEC),
        input_output_aliases={i: 2 * ng + i for i in range(n)},
        compiler_params=pltpu.CompilerParams(has_side_effects=SIDE_EFFECT),
    )(*[_hbm(g) for g in flat], after)
    result, pos = [], 2 * ng
    for k, size in enumerate(sizes):
        result.append((outs[2 * k], outs[2 * k + 1], list(outs[pos:pos + size])))
        pos += size
    return result, outs[-1]


def gather_wait(gs, send_sems, recv_sems, after, name, masks=GATHER_MASKS):
    n = len(gs)

    def body(*refs):
        g_in = refs[:n]
        send, recv = refs[n], refs[n + 1]
        me = _my_index()
        for i in range(n):
            for m in masks:
                _, pidx = _peer(m)
                _gather_copy(g_in[i], i, m, send, recv, me).wait_send()
                _gather_copy(g_in[i], i, m, send, recv, pidx).wait_recv()

    outs = pl.pallas_call(
        body, name=name,
        out_shape=[pltpu.HBM(g.shape, g.dtype) for g in gs],
        in_specs=[HBM_SPEC] * n + [SEM_SPEC, SEM_SPEC, ANY_SPEC],
        out_specs=[HBM_SPEC] * n,
        input_output_aliases={i: i for i in range(n)},
        compiler_params=pltpu.CompilerParams(has_side_effects=SIDE_EFFECT),
    )(*gs, send_sems, recv_sems, after)
    return list(outs)


def sibling_forward(gs, name):
    n = len(gs)
    nf = len(FORWARD_MASKS)

    def body(*refs):
        g_in = refs[:n]
        send_sems, recv_sems = refs[2 * n:]
        x, y, c = lax.axis_index("x"), lax.axis_index("y"), lax.axis_index("c")
        sibling = (x, y, 1 - c)

        def copy(i, k, origin):
            return pltpu.make_async_remote_copy(
                src_ref=g_in[i].at[origin], dst_ref=g_in[i].at[origin],
                send_sem=send_sems.at[i * nf + k], recv_sem=recv_sems.at[i * nf + k],
                device_id=sibling, device_id_type=MESH)
        sends = []
        for i in range(n):
            for k, m in enumerate(FORWARD_MASKS):
                _, origin = _peer(m)
                cp = copy(i, k, origin)
                cp.start()
                sends.append(cp)
        for i in range(n):
            for k, m in enumerate(FORWARD_MASKS):
                _, origin = _peer(m ^ 1)
                copy(i, k, origin).wait_recv()
        for cp in sends:
            cp.wait_send()

    outs = pl.pallas_call(
        body, name=name,
        out_shape=[jax.ShapeDtypeStruct(g.shape, g.dtype) for g in gs],
        in_specs=[HBM_SPEC] * n, out_specs=[HBM_SPEC] * n,
        input_output_aliases={i: i for i in range(n)},
        scratch_shapes=[pltpu.SemaphoreType.DMA((n * nf,)), pltpu.SemaphoreType.DMA((n * nf,))],
    )(*gs)
    return list(outs)


def _forward_copy(g_ref, i, k, send_sems, recv_sems, origin):
    sibling = (lax.axis_index("x"), lax.axis_index("y"), 1 - lax.axis_index("c"))
    slot = i * len(FORWARD_MASKS) + k
    return pltpu.make_async_remote_copy(
        src_ref=g_ref.at[origin], dst_ref=g_ref.at[origin],
        send_sem=send_sems.at[slot], recv_sem=recv_sems.at[slot],
        device_id=sibling, device_id_type=MESH)


def forward_start(gs, after, name):
    n = len(gs)
    nsem = n * len(FORWARD_MASKS)

    def body(*refs):
        g_in = refs[:n]
        send_sems, recv_sems = refs[n + 1], refs[n + 2]
        token = refs[-1]
        for i in range(n):
            for k, m in enumerate(FORWARD_MASKS):
                _, origin = _peer(m)
                _forward_copy(g_in[i], i, k, send_sems, recv_sems, origin).start()
        token[...] = jnp.zeros_like(token)

    outs = pl.pallas_call(
        body, name=name,
        out_shape=(pltpu.SemaphoreType.DMA((nsem,)), pltpu.SemaphoreType.DMA((nsem,)),
                   *[pltpu.HBM(g.shape, g.dtype) for g in gs], jax.ShapeDtypeStruct((8, 128), F32)),
        in_specs=[HBM_SPEC] * n + [ANY_SPEC],
        out_specs=(SEM_SPEC, SEM_SPEC, *[HBM_SPEC] * n, VMEM_SPEC),
        input_output_aliases={i: 2 + i for i in range(n)},
        compiler_params=pltpu.CompilerParams(has_side_effects=SIDE_EFFECT),
    )(*[_hbm(g) for g in gs], after)
    return outs[0], outs[1], list(outs[2:2 + n]), outs[-1]


def forward_wait(gs, send_sems, recv_sems, after, name):
    n = len(gs)

    def body(*refs):
        g_in = refs[:n]
        send, recv = refs[n], refs[n + 1]
        for i in range(n):
            for k, m in enumerate(FORWARD_MASKS):
                _, mine = _peer(m)
                _, theirs = _peer(m ^ 1)
                _forward_copy(g_in[i], i, k, send, recv, mine).wait_send()
                _forward_copy(g_in[i], i, k, send, recv, theirs).wait_recv()

    outs = pl.pallas_call(
        body, name=name,
        out_shape=[pltpu.HBM(g.shape, g.dtype) for g in gs],
        in_specs=[HBM_SPEC] * n + [SEM_SPEC, SEM_SPEC, ANY_SPEC],
        out_specs=[HBM_SPEC] * n,
        input_output_aliases={i: i for i in range(n)},
        compiler_params=pltpu.CompilerParams(has_side_effects=SIDE_EFFECT),
    )(*gs, send_sems, recv_sems, after)
    return list(outs)


def _scatter_copy(g_ref, l_ref, i, m, send_sems, recv_sems):
    peer, pidx = _peer(m)
    return pltpu.make_async_remote_copy(
        src_ref=g_ref.at[pidx], dst_ref=l_ref.at[m - 1],
        send_sem=send_sems.at[_sem(i, m)], recv_sem=recv_sems.at[_sem(i, m)],
        device_id=peer, device_id_type=MESH)


def scatter_start(grads, after, name):
    n = len(grads)
    lands = [lax.empty((N_DEV - 1,) + g.shape[1:], g.dtype) for g in grads]

    def body(*refs):
        g_in, l_in = refs[:n], refs[n:2 * n]
        send_sems, recv_sems = refs[2 * n + 1], refs[2 * n + 2]
        token = refs[-1]
        for i in range(n):
            for m in range(1, N_DEV):
                _scatter_copy(g_in[i], l_in[i], i, m, send_sems, recv_sems).start()
        token[...] = jnp.zeros_like(token)

    outs = pl.pallas_call(
        body, name=name,
        out_shape=(*_sem_pairs(n), *[pltpu.HBM(g.shape, g.dtype) for g in grads],
                   *[pltpu.HBM(l.shape, l.dtype) for l in lands], jax.ShapeDtypeStruct((8, 128), F32)),
        in_specs=[HBM_SPEC] * (2 * n) + [ANY_SPEC],
        out_specs=(SEM_SPEC, SEM_SPEC, *[HBM_SPEC] * (2 * n), VMEM_SPEC),
        input_output_aliases={i: 2 + i for i in range(2 * n)},
        compiler_params=pltpu.CompilerParams(has_side_effects=SIDE_EFFECT),
    )(*[_hbm(g) for g in grads], *[_hbm(l) for l in lands], after)
    return outs[0], outs[1], list(outs[2:2 + n]), list(outs[2 + n:2 + 2 * n]), outs[-1]


def scatter_wait(grads, lands, send_sems, recv_sems, after, name):
    n = len(grads)

    def body(*refs):
        g_in, l_in = refs[:n], refs[n:2 * n]
        send, recv = refs[2 * n], refs[2 * n + 1]
        for i in range(n):
            for m in range(1, N_DEV):
                cp = _scatter_copy(g_in[i], l_in[i], i, m, send, recv)
                cp.wait_send()
                cp.wait_recv()

    outs = pl.pallas_call(
        body, name=name,
        out_shape=[pltpu.HBM(a.shape, a.dtype) for a in list(grads) + list(lands)],
        in_specs=[HBM_SPEC] * (2 * n) + [SEM_SPEC, SEM_SPEC, ANY_SPEC],
        out_specs=[HBM_SPEC] * (2 * n),
        input_output_aliases={i: i for i in range(2 * n)},
        compiler_params=pltpu.CompilerParams(has_side_effects=SIDE_EFFECT),
    )(*grads, *lands, send_sems, recv_sems, after)
    return list(outs[:n]), list(outs[n:])


def sum_slots(g, name):
    _, r, c = g.shape

    def body(g_ref, out_ref):
        acc = g_ref[0]
        for p in range(1, N_DEV):
            acc = acc + g_ref[p]
        out_ref[...] = acc

    return pl.pallas_call(
        body, name=name, out_shape=jax.ShapeDtypeStruct((r, c), F32),
        in_specs=[VMEM_SPEC], out_specs=VMEM_SPEC,
        compiler_params=pltpu.CompilerParams(vmem_limit_bytes=VMEM_LIMIT),
    )(g)


def _sigmoid(v):
    return 1.0 / (1.0 + jnp.exp(-v))


def _rms_fwd(xf, g):
    r = lax.rsqrt(jnp.mean(xf * xf, axis=-1, keepdims=True) + EPS)
    return xf * r, r


def _rms_bwd(xhat, r, g, dy):
    dg = jnp.sum(dy * xhat, axis=0, keepdims=True)
    dxh = dy * g
    dx = r * (dxh - xhat * jnp.mean(dxh * xhat, axis=-1, keepdims=True))
    return dx, dg


def _ln_stats(v):
    mu = jnp.mean(v, axis=-1, keepdims=True)
    vc = v - mu
    r = lax.rsqrt(jnp.mean(vc * vc, axis=-1, keepdims=True) + EPS)
    return vc * r, r


def _ln_bwd(xhat, r, dxh):
    return r * (dxh - jnp.mean(dxh, axis=-1, keepdims=True)
                - xhat * jnp.mean(dxh * xhat, axis=-1, keepdims=True))


def _dot(a, b):
    return jnp.dot(a, b, preferred_element_type=F32)


def _dot_nt(a, b):
    return lax.dot_general(a, b, (((1,), (1,)), ((), ())), preferred_element_type=F32)


def _dot_tn(a, b):
    return lax.dot_general(a, b, (((0,), (0,)), ((), ())), preferred_element_type=F32)


def _full_weight(w_ref, kind):
    assert kind == "row"
    p, a, b = w_ref.shape
    return w_ref[...].reshape(p * a, b)


def _wspec(wg):
    return pl.BlockSpec(wg.shape, lambda *_: (0, 0, 0))


def mm_rows(a, wg, kind, *, gain=None, residual=None, out_dtype=F32, name, tm=None):
    m, k = a.shape
    p, wa, wb = wg.shape
    n = p * wb if kind == "col" else wb
    tm = _row_tile(m, tm)
    has_gain, has_res = gain is not None, residual is not None

    def body(*refs):
        refs = list(refs)
        a_ref = refs.pop(0)
        g_ref = refs.pop(0) if has_gain else None
        w_ref = refs.pop(0)
        r_ref = refs.pop(0) if has_res else None
        o_ref = refs.pop(0)
        if has_gain:
            xhat, _ = _rms_fwd(a_ref[...].astype(F32), None)
            h = (xhat * g_ref[...]).astype(BF16)
        else:
            h = a_ref[...].astype(BF16)
        if kind == "col":
            for j in range(p):
                o = _dot(h, w_ref[j])
                if has_res:
                    o = o + r_ref[:, j * wb:(j + 1) * wb]
                o_ref[:, j * wb:(j + 1) * wb] = o.astype(out_dtype)
        else:
            o = _dot(h, _full_weight(w_ref, "row"))
            if has_res:
                o = o + r_ref[...]
            o_ref[...] = o.astype(out_dtype)

    operands = [a]
    in_specs = [pl.BlockSpec((tm, k), lambda i: (i, 0))]
    if has_gain:
        operands.append(gain.reshape(1, k))
        in_specs.append(pl.BlockSpec((1, k), lambda i: (0, 0)))
    operands.append(wg)
    in_specs.append(_wspec(wg))
    if has_res:
        operands.append(residual)
        in_specs.append(pl.BlockSpec((tm, n), lambda i: (i, 0)))
    return pl.pallas_call(
        body, name=name, grid=(m // tm,),
        out_shape=jax.ShapeDtypeStruct((m, n), out_dtype),
        in_specs=in_specs, out_specs=pl.BlockSpec((tm, n), lambda i: (i, 0)),
        compiler_params=_params("parallel"),
    )(*operands)


def mm_nt(dz, wg, kind, *, x=None, gain=None, dx_in=None, after=None, name, tm=None):
    m, n = dz.shape
    p, wa, wb = wg.shape
    k = wa if kind == "col" else p * wa
    tm = _row_tile(m, tm)
    epi = x is not None
    has_dx = dx_in is not None
    has_after = after is not None

    def body(*refs):
        refs = list(refs)
        dz_ref, w_ref = refs.pop(0), refs.pop(0)
        if epi:
            x_ref, g_ref = refs.pop(0), refs.pop(0)
            dxi_ref = refs.pop(0) if has_dx else None
        if has_after:
            refs.pop(0)
        if epi:
            dx_ref, h_ref, dg_ref = refs
        else:
            (da_ref,) = refs
        dzb = dz_ref[...].astype(BF16)
        if kind == "col":
            da = _dot_nt(dzb[:, 0:wb], w_ref[0])
            for j in range(1, p):
                da = da + _dot_nt(dzb[:, j * wb:(j + 1) * wb], w_ref[j])
        else:
            da = _dot_nt(dzb, _full_weight(w_ref, "row"))
        if not epi:
            da_ref[...] = da
            return
        g = g_ref[...]
        xhat, r = _rms_fwd(x_ref[...].astype(F32), None)
        h_ref[...] = (xhat * g).astype(BF16)
        dx, dg = _rms_bwd(xhat, r, g, da)
        if has_dx:
            dx = dx + dxi_ref[...]
        dx_ref[...] = dx

        @pl.when(pl.program_id(0) == 0)
        def _():
            dg_ref[...] = jnp.zeros_like(dg_ref)
        dg_ref[...] += dg

    row = lambda i: (i, 0)
    operands = [dz, wg]
    in_specs = [pl.BlockSpec((tm, n), row), _wspec(wg)]
    if epi:
        operands += [x, gain.reshape(1, k)]
        in_specs += [pl.BlockSpec((tm, k), row), pl.BlockSpec((1, k), lambda i: (0, 0))]
        if has_dx:
            operands.append(dx_in)
            in_specs.append(pl.BlockSpec((tm, k), row))
        out_shape = [jax.ShapeDtypeStruct((m, k), F32), jax.ShapeDtypeStruct((m, k), BF16),
                     jax.ShapeDtypeStruct((1, k), F32)]
        out_specs = [pl.BlockSpec((tm, k), row), pl.BlockSpec((tm, k), row),
                     pl.BlockSpec((1, k), lambda i: (0, 0))]
    else:
        out_shape = jax.ShapeDtypeStruct((m, k), F32)
        out_specs = pl.BlockSpec((tm, k), row)
    if has_after:
        operands.append(after)
        in_specs.append(ANY_SPEC)
    return pl.pallas_call(
        body, name=name, grid=(m // tm,), out_shape=out_shape,
        in_specs=in_specs, out_specs=out_specs,
        compiler_params=_params("arbitrary"),
    )(*operands)


def mm_tn(a, b, *, nb, a_spec, b_spec, ka, nbk, tm, m, scale=1.0, out_dtype=BF16, col_slots=1,
          after=None, name):
    ni = m // tm
    assert col_slots == 1 or nb == 1
    cw = nbk // col_slots
    extra = [] if after is None else [after]

    def body(a_ref, b_ref, *rest):
        o_ref, acc = rest[len(extra):]
        i = pl.program_id(1)

        @pl.when(i == 0)
        def _():
            acc[...] = jnp.zeros_like(acc)
        acc[...] += _dot_tn(a_ref[...].astype(BF16), b_ref[...].astype(BF16))

        @pl.when(i == ni - 1)
        def _():
            if col_slots == 1:
                o_ref[...] = (acc[...] * scale).astype(out_dtype)
            else:
                for j in range(col_slots):
                    o_ref[j] = (acc[:, j * cw:(j + 1) * cw] * scale).astype(out_dtype)

    if col_slots == 1:
        out_shape = jax.ShapeDtypeStruct((nb, ka, nbk), out_dtype)
        out_spec = pl.BlockSpec((None, ka, nbk), lambda s, i: (s, 0, 0))
    else:
        out_shape = jax.ShapeDtypeStruct((col_slots, ka, cw), out_dtype)
        out_spec = pl.BlockSpec((col_slots, ka, cw), lambda s, i: (0, 0, 0))
    return pl.pallas_call(
        body, name=name, grid=(nb, ni), out_shape=out_shape,
        in_specs=[a_spec, b_spec] + [ANY_SPEC] * len(extra), out_specs=out_spec,
        scratch_shapes=[pltpu.VMEM((ka, nbk), F32)],
        compiler_params=_params("parallel", "arbitrary"),
    )(a, b, *extra)


def _ffn_specs(w_in_g, w_out_g, d):
    nf = w_in_g.shape[1]
    hr = w_out_g.shape[1]
    assert 2 * hr == nf
    w_in5 = w_in_g.reshape(2, 4, nf, d)
    w_out5 = w_out_g.reshape(4, 2, hr, d)
    in_spec = pl.BlockSpec((2, None, nf, d), lambda i, j: (0, j, 0, 0))
    out_spec = pl.BlockSpec((None, 2, hr, d), lambda i, j: (j, 0, 0, 0))
    return w_in5, w_out5, in_spec, out_spec, nf


def ffn_fwd(x, gain, w_in_g, w_out_g, *, name, tm=None):
    t, d = x.shape
    tm = _row_tile(t, tm)
    w_in5, w_out5, wi_spec, wo_spec, nf = _ffn_specs(w_in_g, w_out_g, d)

    def body(x_ref, g_ref, wi_ref, wo_ref, o_ref, gu_ref, h_scr, acc):
        j = pl.program_id(1)

        @pl.when(j == 0)
        def _():
            xhat, _ = _rms_fwd(x_ref[...], None)
            h_scr[...] = (xhat * g_ref[...]).astype(BF16)
            acc[...] = jnp.zeros_like(acc)
        wo = wo_ref[...].reshape(nf, d)

        def project(rows):
            h = h_scr[rows]
            return _dot_nt(h, wi_ref[0]), _dot_nt(h, wi_ref[1])

        sub = tm // FFN_FWD_SPLIT
        parts = [slice(k * sub, (k + 1) * sub) for k in range(FFN_FWD_SPLIT)]
        gt, up = project(parts[0])
        for k, rows in enumerate(parts):
            if k + 1 < len(parts):
                nxt = project(parts[k + 1])
            gu_ref[0, rows] = gt.astype(BF16)
            gu_ref[1, rows] = up.astype(BF16)
            act = (gt * _sigmoid(gt) * up).astype(BF16)
            acc[rows] += _dot(act, wo)
            if k + 1 < len(parts):
                gt, up = nxt

        @pl.when(j == 3)
        def _():
            o_ref[...] = x_ref[...] + 0.5 * acc[...]

    return pl.pallas_call(
        body, name=name, grid=(t // tm, 4),
        out_shape=[jax.ShapeDtypeStruct((t, d), F32), jax.ShapeDtypeStruct((2, 4, t, nf), BF16)],
        in_specs=[pl.BlockSpec((tm, d), lambda i, j: (i, 0)),
                  pl.BlockSpec((1, d), lambda i, j: (0, 0)), wi_spec, wo_spec],
        out_specs=[pl.BlockSpec((tm, d), lambda i, j: (i, 0)),
                   pl.BlockSpec((2, None, tm, nf), lambda i, j: (0, j, i, 0))],
        scratch_shapes=[pltpu.VMEM((tm, d), BF16), pltpu.VMEM((tm, d), F32)],
        compiler_params=_params("parallel", "arbitrary"),
    )(x, gain.reshape(1, d), w_in5, w_out5)


def ffn_bwd_rows(x, dy, gu, gain, w_in_g, w_out_g, after, *, name, tm=None):
    t, d = x.shape
    tm = _row_tile(t, tm)
    w_in5, w_out5, wi_spec, wo_spec, nf = _ffn_specs(w_in_g, w_out_g, d)

    def body(x_ref, dy_ref, gu_ref, g_ref, wi_ref, wo_ref, after_ref, dx_ref, h_ref, act_ref, dgu_ref, dg_ref,
             dyh_scr, dh_acc):
        i, j = pl.program_id(0), pl.program_id(1)

        @pl.when(j == 0)
        def _():
            xhat, _ = _rms_fwd(x_ref[...], None)
            h_ref[...] = (xhat * g_ref[...]).astype(BF16)
            dyh_scr[...] = (0.5 * dy_ref[...]).astype(BF16)
            dh_acc[...] = jnp.zeros_like(dh_acc)
        wo = wo_ref[...].reshape(nf, d)

        def gates(rows):
            gt = gu_ref[0, rows].astype(F32)
            up = gu_ref[1, rows].astype(F32)
            sg = _sigmoid(gt)
            silu = gt * sg
            act_ref[rows] = (silu * up).astype(BF16)
            return up * (sg * (1.0 + gt * (1.0 - sg))), silu

        def grads(rows, dact, dsilu_up, silu):
            dgt = (dact * dsilu_up).astype(BF16)
            dup = (dact * silu).astype(BF16)
            dgu_ref[0, rows] = dgt
            dgu_ref[1, rows] = dup
            return dgt, dup

        sub = tm // FFN_BWD_SPLIT
        parts = [slice(k * sub, (k + 1) * sub) for k in range(FFN_BWD_SPLIT)]
        dact = _dot_nt(dyh_scr[parts[0]], wo)
        gate = gates(parts[0])
        for k, rows in enumerate(parts):
            if k + 1 < len(parts):
                dact_next = _dot_nt(dyh_scr[parts[k + 1]], wo)
            dgt, dup = grads(rows, dact, *gate)
            dh_acc[rows] += _dot(dgt, wi_ref[0]) + _dot(dup, wi_ref[1])
            if k + 1 < len(parts):
                gate = gates(parts[k + 1])
                dact = dact_next

        @pl.when(j == 3)
        def _():
            g = g_ref[...]
            xhat, r = _rms_fwd(x_ref[...], None)
            dx, dg = _rms_bwd(xhat, r, g, dh_acc[...])
            dx_ref[...] = dy_ref[...] + dx

            @pl.when(i == 0)
            def _():
                dg_ref[...] = jnp.zeros_like(dg_ref)
            dg_ref[...] += dg

    row = lambda i, j: (i, 0)
    return pl.pallas_call(
        body, name=name, grid=(t // tm, 4),
        out_shape=[jax.ShapeDtypeStruct((t, d), F32), jax.ShapeDtypeStruct((t, d), BF16),
                   jax.ShapeDtypeStruct((4, t, nf), BF16), jax.ShapeDtypeStruct((2, 4, t, nf), BF16),
                   jax.ShapeDtypeStruct((1, d), F32), jax.ShapeDtypeStruct((t, d), BF16)],
        in_specs=[pl.BlockSpec((tm, d), row), pl.BlockSpec((tm, d), row),
                  pl.BlockSpec((2, None, tm, nf), lambda i, j: (0, j, i, 0)),
                  pl.BlockSpec((1, d), lambda i, j: (0, 0)), wi_spec, wo_spec, ANY_SPEC],
        out_specs=[pl.BlockSpec((tm, d), row), pl.BlockSpec((tm, d), row),
                   pl.BlockSpec((None, tm, nf), lambda i, j: (j, i, 0)),
                   pl.BlockSpec((2, None, tm, nf), lambda i, j: (0, j, i, 0)),
                   pl.BlockSpec((1, d), lambda i, j: (0, 0)), pl.BlockSpec((tm, d), row)],
        scratch_shapes=[pltpu.VMEM((tm, d), F32)],
        compiler_params=_params("arbitrary", "arbitrary"),
    )(x, dy, gu, gain.reshape(1, d), w_in5, w_out5, after)


def ffn_grad_w_in(h, dgu, after, *, name):
    t, d = h.shape
    nf = dgu.shape[-1]
    tm = _row_tile(t, TN_TILE)
    return mm_tn(dgu.reshape(8, t, nf), h, nb=8, ka=nf, nbk=d, tm=tm, m=t, after=after,
                 a_spec=pl.BlockSpec((None, tm, nf), lambda s, i: (s, i, 0)),
                 b_spec=pl.BlockSpec((tm, d), lambda s, i: (i, 0)), name=name)


def ffn_grad_w_out(act, dyh, after, *, name):
    _, t, nf = act.shape
    d = dyh.shape[1]
    tm = _row_tile(t, TN_TILE)
    d_w_out = mm_tn(act, dyh, nb=4, ka=nf, nbk=d, tm=tm, m=t, after=after,
                    a_spec=pl.BlockSpec((None, tm, nf), lambda s, i: (s, i, 0)),
                    b_spec=pl.BlockSpec((tm, d), lambda s, i: (i, 0)), name=name)
    return d_w_out.reshape(8, nf // 2, d)


def _lane_group(shape):
    return lax.shift_right_logical(lax.broadcasted_iota(jnp.int32, shape, 1), 6)


def _pool_count(t0, rows):
    t = (t0 + lax.broadcasted_iota(jnp.int32, (rows, MIX_W), 0) + 1).astype(F32)
    return jnp.minimum(t, _by_group(_lane_group((rows, MIX_W)), 2.0, 4.0, 8.0, 16.0))


def _by_group(grp, v0, v1, v2, v3):
    return jnp.where(grp == 0, v0, jnp.where(grp == 1, v1, jnp.where(grp == 2, v2, v3)))


def _sgu_mix(wt_ref, vnc):
    grp = _lane_group((SGU_CHUNK, MIX_W))
    out = jnp.zeros((SGU_CHUNK, MIX_W), F32)
    for hd in range(N_HEADS):
        out = jnp.where(grp == hd, _dot(wt_ref[hd], vnc), out)
    return out


def _pool_fwd(s1, s2, s3, t0, ts, lo):
    h = lo
    s2[h - 24:h + ts] = s1[h - 24:h + ts] + s1[h - 25:h + ts - 1]
    s3[h - 16:h + ts] = s2[h - 16:h + ts] + s2[h - 18:h + ts - 2]
    sum2 = s2[h:h + ts]
    sum4 = s3[h:h + ts]
    s2[h - 8:h + ts] = s3[h - 8:h + ts] + s3[h - 12:h + ts - 4]
    sum8 = s2[h:h + ts]
    sum16 = sum8 + s2[h - 8:h + ts - 8]
    grp = _lane_group((ts, MIX_W))
    return _by_group(grp, sum2, sum4, sum8, sum16) / _pool_count(t0, ts) - s1[h:h + ts]


def _make_shifts(src, sh, rows):
    for b in range(1, 8):
        sh[b, 0:rows] = src[b:b + rows]


def _rows_at(src, sh, start, n):
    a, b = divmod(start, 8)
    return src[8 * a:8 * a + n] if b == 0 else sh[b, 8 * a:8 * a + n]


def mixer_fwd(z, sconv, cconv, vecs, wt, bexp, pbd, x_res, wmo_g, *, name, ts=None):
    t = z.shape[0]
    ts = _row_tile(t, MIX_TILE if ts is None else ts)
    hl = HALO
    w = MIX_W
    nch = ts // SGU_CHUNK

    def body(zc, zp, sconv_ref, cconv_ref, vec_ref, wt_ref, bexp_ref, pbd_ref, xr_ref, wmo_ref,
             y_ref, xo_ref, s1, s2, s3, sh):
        i = pl.program_id(0)
        has_prev = i > 0

        def col(ref, c):
            return ref[:, c * w:(c + 1) * w]

        def prev(c):
            return jnp.where(has_prev, col(zp, c), 0.0)

        s1[0:hl] = prev(1) * prev(2)
        s1[hl:hl + ts] = col(zc, 1) * col(zc, 2)
        cv = sconv_ref[0:1] * s1[hl - 2:hl - 2 + ts]
        for k in range(1, SCONV_K):
            cv = cv + sconv_ref[k:k + 1] * s1[hl - 2 + k:hl - 2 + k + ts]
        y_ref[:, 0:w] = (col(zc, 0) * cv).astype(BF16)

        xhat, _ = _ln_stats(col(zc, 4))
        vn = (xhat * vec_ref[0:1]).astype(BF16)
        for c in range(nch):
            rows = slice(c * SGU_CHUNK, (c + 1) * SGU_CHUNK)
            mixed = _sgu_mix(wt_ref, vn[rows]) + bexp_ref[...]
            y_ref[rows, w:2 * w] = (zc[rows, 3 * w:4 * w] * mixed).astype(BF16)

        s1[0:hl] = prev(5) * _sigmoid(prev(6))
        s1[hl:hl + ts] = col(zc, 5) * _sigmoid(col(zc, 6))
        off = hl - (CCONV_K - 1)
        _make_shifts(s1, sh, hl + ts - 8)
        cv = cconv_ref[0:1] * _rows_at(s1, sh, off, ts)
        for k in range(1, CCONV_K):
            cv = cv + cconv_ref[k:k + 1] * _rows_at(s1, sh, off + k, ts)
        xhat, _ = _ln_stats(cv)
        ln = xhat * vec_ref[1:2] + vec_ref[2:3]
        y_ref[:, 2 * w:3 * w] = (ln * _sigmoid(ln)).astype(BF16)

        s1[0:hl] = prev(7)
        s1[hl:hl + ts] = col(zc, 7)
        pooled = _pool_fwd(s1, s2, s3, i * ts, ts, hl)
        y_ref[:, 3 * w:4 * w] = (_dot(pooled.astype(BF16), pbd_ref[...]) * vec_ref[3:4]).astype(BF16)

        xo_ref[...] = xr_ref[...] + _dot(y_ref[...], _full_weight(wmo_ref, "row"))

    full = lambda shape: pl.BlockSpec(shape, lambda i: (0,) * len(shape))
    row = lambda i: (i, 0)
    return pl.pallas_call(
        body, name=name, grid=(t // ts,),
        out_shape=[jax.ShapeDtypeStruct((t, 4 * w), BF16), jax.ShapeDtypeStruct((t, 4 * w), F32)],
        in_specs=[pl.BlockSpec((ts, 8 * w), row),
                  pl.BlockSpec((hl, 8 * w), lambda i: (jnp.maximum(i * (ts // hl) - 1, 0), 0)),
                  full((8, w)), full((32, w)), full((8, w)), full((N_HEADS, SGU_CHUNK, SGU_CHUNK)),
                  full((SGU_CHUNK, w)), full((w, w)), pl.BlockSpec((ts, 4 * w), row), _wspec(wmo_g)],
        out_specs=[pl.BlockSpec((ts, 4 * w), row), pl.BlockSpec((ts, 4 * w), row)],
        scratch_shapes=[pltpu.VMEM((hl + ts, w), F32)] * 3 + [pltpu.VMEM((8, hl + ts, w), F32)],
        compiler_params=_params("parallel"),
    )(z, z, sconv, cconv, vecs, wt, bexp, pbd, x_res, wmo_g)


def mixer_bwd(z, dx, wmo_g, sconv, cconv, vecs, wt, bexp, pbd, *, name, ts=None):
    t = z.shape[0]
    ts = _row_tile(t, MIX_TILE if ts is None else ts)
    hl = HALO
    w = MIX_W
    nch = ts // SGU_CHUNK
    ni = t // ts
    ext = ts + hl

    def body(zc, zp, zn, dxc, dxn_, wmo_ref, sconv_ref, cconv_ref, vec_ref, wt_ref, bexp_ref, pbd_ref,
             dz_ref, gvec_ref, gcc_ref, gwt_ref, gb_ref, gpbd_ref, s1, s2, s3, sh1, sh3, dyc, dyn):
        i = pl.program_id(0)
        has_prev = i > 0
        has_next = i < ni - 1
        wmo = _full_weight(wmo_ref, "row")
        dyc[...] = _dot_nt(dxc[...].astype(BF16), wmo)
        dyn[...] = _dot_nt(dxn_[...].astype(BF16), wmo)

        @pl.when(i == 0)
        def _():
            gvec_ref[...] = jnp.zeros_like(gvec_ref)
            gcc_ref[...] = jnp.zeros_like(gcc_ref)
            gwt_ref[...] = jnp.zeros_like(gwt_ref)
            gb_ref[...] = jnp.zeros_like(gb_ref)
            gpbd_ref[...] = jnp.zeros_like(gpbd_ref)

        def col(ref, c):
            return ref[:, c * w:(c + 1) * w]

        def prev(c):
            return jnp.where(has_prev, col(zp, c), 0.0)

        def nxt(c):
            return jnp.where(has_next, col(zn, c), 0.0)

        def dnext(c):
            return jnp.where(has_next, col(dyn, c), 0.0)

        def rowsum(v):
            return jnp.sum(v, axis=0, keepdims=True)

        s1[0:hl] = prev(1) * prev(2)
        s1[hl:hl + ts] = col(zc, 1) * col(zc, 2)
        s1[hl + ts:hl + ts + hl] = nxt(1) * nxt(2)
        cv = sconv_ref[0:1] * s1[hl - 2:hl - 2 + ts]
        for k in range(1, SCONV_K):
            cv = cv + sconv_ref[k:k + 1] * s1[hl - 2 + k:hl - 2 + k + ts]
        dya = col(dyc, 0)
        dz_ref[:, 0:w] = (dya * cv).astype(BF16)
        s2[0:ts] = dya * col(zc, 0)
        s2[ts:ext] = dnext(0) * nxt(0)
        dv = sconv_ref[0:1] * s2[2:2 + ts]
        for k in range(1, SCONV_K):
            dv = dv + sconv_ref[k:k + 1] * s2[2 - k:2 - k + ts]
        dz_ref[:, w:2 * w] = (dv * col(zc, 2)).astype(BF16)
        dz_ref[:, 2 * w:3 * w] = (dv * col(zc, 1)).astype(BF16)
        dcv = s2[0:ts]
        for k in range(SCONV_K):
            gvec_ref[k:k + 1] += rowsum(dcv * s1[hl - 2 + k:hl - 2 + k + ts])

        g_sgu = vec_ref[0:1]
        xhat, rstd = _ln_stats(col(zc, 4))
        vn = (xhat * g_sgu).astype(BF16)
        grp = _lane_group((SGU_CHUNK, w))
        lane = lax.broadcasted_iota(jnp.int32, (SGU_CHUNK, SGU_CHUNK), 1)
        tril = lax.broadcasted_iota(jnp.int32, (SGU_CHUNK, SGU_CHUNK), 0) >= lane
        for c in range(nch):
            rows = slice(c * SGU_CHUNK, (c + 1) * SGU_CHUNK)
            vnc = vn[rows]
            mixed = _sgu_mix(wt_ref, vnc) + bexp_ref[...]
            dyb = dyc[rows, w:2 * w]
            dz_ref[rows, 3 * w:4 * w] = (dyb * mixed).astype(BF16)
            dmix = dyb * zc[rows, 3 * w:4 * w]
            dmixb = dmix.astype(BF16)
            dvn = jnp.zeros((SGU_CHUNK, w), F32)
            gb = jnp.zeros((SGU_CHUNK, SGU_CHUNK), F32)
            for hd in range(N_HEADS):
                dvn = jnp.where(grp == hd, _dot_tn(wt_ref[hd], dmixb), dvn)
                dm_h = jnp.where(grp == hd, dmix, 0.0)
                gwt_ref[hd] += jnp.where(tril, _dot_nt(dm_h.astype(BF16), vnc), 0.0)
                gb = gb + jnp.where(lane == hd, jnp.sum(dm_h, axis=1, keepdims=True), 0.0)
            gb_ref[...] += gb
            s3[rows] = dvn
        dvn = s3[0:ts]
        gvec_ref[3:4] += rowsum(dvn * xhat)
        dz_ref[:, 4 * w:5 * w] = _ln_bwd(xhat, rstd, dvn * g_sgu).astype(BF16)

        sig_c = _sigmoid(col(zc, 6))
        s1[0:hl] = prev(5) * _sigmoid(prev(6))
        s1[hl:hl + ts] = col(zc, 5) * sig_c
        s1[hl + ts:hl + ts + hl] = nxt(5) * _sigmoid(nxt(6))
        off = hl - (CCONV_K - 1)
        _make_shifts(s1, sh1, ts + 2 * hl - 8)
        cv = cconv_ref[0:1] * _rows_at(s1, sh1, off, ext)
        for k in range(1, CCONV_K):
            cv = cv + cconv_ref[k:k + 1] * _rows_at(s1, sh1, off + k, ext)
        xhat, rstd = _ln_stats(cv)
        ln = xhat * vec_ref[1:2] + vec_ref[2:3]
        sg = _sigmoid(ln)
        s2[0:ts] = col(dyc, 2)
        s2[ts:ext] = dnext(2)
        dln = s2[0:ext] * (sg * (1.0 + ln * (1.0 - sg)))
        gvec_ref[4:5] += rowsum(dln[0:ts] * xhat[0:ts])
        gvec_ref[5:6] += rowsum(dln[0:ts])
        s3[0:ext] = _ln_bwd(xhat, rstd, dln * vec_ref[1:2])
        _make_shifts(s3, sh3, ext - 8)
        dyg = cconv_ref[0:1] * _rows_at(s3, sh3, CCONV_K - 1, ts)
        for k in range(1, CCONV_K):
            dyg = dyg + cconv_ref[k:k + 1] * _rows_at(s3, sh3, CCONV_K - 1 - k, ts)
        dz_ref[:, 5 * w:6 * w] = (dyg * sig_c).astype(BF16)
        dz_ref[:, 6 * w:7 * w] = (dyg * col(zc, 5) * sig_c * (1.0 - sig_c)).astype(BF16)
        dcv = s3[0:ts]
        for k in range(CCONV_K):
            gcc_ref[k:k + 1] += rowsum(dcv * _rows_at(s1, sh1, off + k, ts))

        scale = vec_ref[3:4]
        s1[0:hl] = prev(7)
        s1[hl:hl + ts] = col(zc, 7)
        pooled = _pool_fwd(s1, s2, s3, i * ts, ts, hl).astype(BF16)
        q0 = _dot(pooled, pbd_ref[...])
        dyd = col(dyc, 3)
        gvec_ref[6:7] += rowsum(dyd * q0)
        dq = (dyd * scale).astype(BF16)
        gpbd_ref[...] += _dot_tn(pooled, dq)
        s1[0:ts] = _dot_nt(dq, pbd_ref[...])
        s1[ts:ext] = _dot_nt((dnext(3) * scale).astype(BF16), pbd_ref[...])
        dpool = s1[0:ts]
        s2[0:ext] = s1[0:ext] / _pool_count(i * ts, ext)
        s3[0:ts + 24] = s2[0:ts + 24] + s2[1:ts + 25]
        f2 = s3[0:ts]
        s2[0:ts + 16] = s3[0:ts + 16] + s3[2:ts + 18]
        f4 = s2[0:ts]
        s3[0:ts + 8] = s2[0:ts + 8] + s2[4:ts + 12]
        f8 = s3[0:ts]
        f16 = f8 + s3[8:ts + 8]
        dz_ref[:, 7 * w:8 * w] = (_by_group(_lane_group((ts, w)), f2, f4, f8, f16) - dpool).astype(BF16)

    full = lambda shape: pl.BlockSpec(shape, lambda i: (0,) * len(shape))
    r = ts // hl
    prev_map = lambda i: (jnp.maximum(i * r - 1, 0), 0)
    next_map = lambda i: (jnp.minimum((i + 1) * r, t // hl - 1), 0)
    return pl.pallas_call(
        body, name=name, grid=(ni,),
        out_shape=[jax.ShapeDtypeStruct((t, 8 * w), BF16), jax.ShapeDtypeStruct((8, w), F32),
                   jax.ShapeDtypeStruct((32, w), F32),
                   jax.ShapeDtypeStruct((N_HEADS, SGU_CHUNK, SGU_CHUNK), F32),
                   jax.ShapeDtypeStruct((SGU_CHUNK, SGU_CHUNK), F32), jax.ShapeDtypeStruct((w, w), F32)],
        in_specs=[pl.BlockSpec((ts, 8 * w), lambda i: (i, 0)),
                  pl.BlockSpec((hl, 8 * w), prev_map), pl.BlockSpec((hl, 8 * w), next_map),
                  pl.BlockSpec((ts, 4 * w), lambda i: (i, 0)), pl.BlockSpec((hl, 4 * w), next_map),
                  _wspec(wmo_g),
                  full((8, w)), full((32, w)), full((8, w)), full((N_HEADS, SGU_CHUNK, SGU_CHUNK)),
                  full((SGU_CHUNK, w)), full((w, w))],
        out_specs=[pl.BlockSpec((ts, 8 * w), lambda i: (i, 0)), full((8, w)), full((32, w)),
                   full((N_HEADS, SGU_CHUNK, SGU_CHUNK)), full((SGU_CHUNK, SGU_CHUNK)), full((w, w))],
        scratch_shapes=[pltpu.VMEM((ts + 2 * hl, w), F32)] * 3 + [pltpu.VMEM((8, ts + 2 * hl, w), F32)] * 2
        + [pltpu.VMEM((ts, 4 * w), F32), pltpu.VMEM((hl, 4 * w), F32)],
        compiler_params=_params("arbitrary"),
    )(z, z, z, dx, dx, wmo_g, sconv, cconv, vecs, wt, bexp, pbd)


def _attn_head(q, kv_ref, hd, d):
    hw = d // N_HEADS
    qh = q[:, hd * hw:(hd + 1) * hw]
    kh = kv_ref[:, hd * hw:(hd + 1) * hw].astype(BF16)
    vh = kv_ref[:, d + hd * hw:d + (hd + 1) * hw].astype(BF16)
    s = _dot_nt(qh, kh) * (1.0 / (hw ** 0.5))
    e = jnp.exp(s - jnp.max(s, axis=-1, keepdims=True))
    p = e / jnp.sum(e, axis=-1, keepdims=True)
    return qh, kh, vh, p


def xattn_fwd(x, gain, kv, wq_g, wo_g, *, name, tm=None):
    t, d = x.shape
    nm = kv.shape[0]
    tm = _row_tile(t, tm)
    hw = d // N_HEADS

    def body(x_ref, g_ref, kv_ref, wq_ref, wo_ref, o_ref):
        xv = x_ref[...]
        xhat, _ = _rms_fwd(xv, None)
        h = (xhat * g_ref[...]).astype(BF16)
        q = _dot(h, _full_weight(wq_ref, "row")).astype(BF16)
        wo = _full_weight(wo_ref, "row")
        out = xv
        for hd in range(N_HEADS):
            _, _, vh, p = _attn_head(q, kv_ref, hd, d)
            oh = _dot(p.astype(BF16), vh).astype(BF16)
            out = out + _dot(oh, wo[hd * hw:(hd + 1) * hw])
        o_ref[...] = out

    row = lambda i: (i, 0)
    return pl.pallas_call(
        body, name=name, grid=(t // tm,),
        out_shape=jax.ShapeDtypeStruct((t, d), F32),
        in_specs=[pl.BlockSpec((tm, d), row), pl.BlockSpec((1, d), lambda i: (0, 0)),
                  pl.BlockSpec((nm, 2 * d), lambda i: (0, 0)), _wspec(wq_g), _wspec(wo_g)],
        out_specs=pl.BlockSpec((tm, d), row),
        compiler_params=_params("parallel"),
    )(x, gain.reshape(1, d), kv, wq_g, wo_g)


def xattn_bwd_rows(x, dxn, gain, kv, wq_g, wo_g, after, *, name, tm=None):
    t, d = x.shape
    nm = kv.shape[0]
    tm = _row_tile(t, tm)
    hw = d // N_HEADS

    def body(x_ref, dxn_ref, g_ref, kv_ref, wq_ref, wo_ref, after_ref,
             dx_ref, h_ref, dq_ref, o_ref, dkv_ref, dg_ref):
        i = pl.program_id(0)

        @pl.when(i == 0)
        def _():
            dkv_ref[...] = jnp.zeros_like(dkv_ref)
            dg_ref[...] = jnp.zeros_like(dg_ref)
        g = g_ref[...]
        xhat, r = _rms_fwd(x_ref[...], None)
        h = (xhat * g).astype(BF16)
        h_ref[...] = h
        wq = _full_weight(wq_ref, "row")
        q = _dot(h, wq).astype(BF16)
        dxn = dxn_ref[...]
        do = _dot_nt(dxn.astype(BF16), _full_weight(wo_ref, "row")).astype(BF16)
        for hd in range(N_HEADS):
            cols = slice(hd * hw, (hd + 1) * hw)
            qh, kh, vh, p = _attn_head(q, kv_ref, hd, d)
            pb = p.astype(BF16)
            o_ref[:, cols] = _dot(pb, vh).astype(BF16)
            doh = do[:, cols]
            dkv_ref[:, d + hd * hw:d + (hd + 1) * hw] += _dot_tn(pb, doh)
            dp = _dot_nt(doh, vh)
            ds = (p * (dp - jnp.sum(dp * p, axis=-1, keepdims=True)) * (1.0 / (hw ** 0.5))).astype(BF16)
            dq_ref[:, cols] = _dot(ds, kh).astype(BF16)
            dkv_ref[:, cols] += _dot_tn(ds, qh)
        dh = _dot_nt(dq_ref[...], wq)
        dx, dg = _rms_bwd(xhat, r, g, dh)
        dx_ref[...] = dxn + dx
        dg_ref[...] += dg

    row = lambda i: (i, 0)
    fix = lambda i: (0, 0)
    return pl.pallas_call(
        body, name=name, grid=(t // tm,),
        out_shape=[jax.ShapeDtypeStruct((t, d), F32), jax.ShapeDtypeStruct((t, d), BF16),
                   jax.ShapeDtypeStruct((t, d), BF16), jax.ShapeDtypeStruct((t, d), BF16),
                   jax.ShapeDtypeStruct((nm, 2 * d), F32), jax.ShapeDtypeStruct((1, d), F32)],
        in_specs=[pl.BlockSpec((tm, d), row), pl.BlockSpec((tm, d), row), pl.BlockSpec((1, d), fix),
                  pl.BlockSpec((nm, 2 * d), fix), _wspec(wq_g), _wspec(wo_g), ANY_SPEC],
        out_specs=[pl.BlockSpec((tm, d), row)] * 4 + [pl.BlockSpec((nm, 2 * d), fix),
                                                      pl.BlockSpec((1, d), fix)],
        compiler_params=_params("arbitrary"),
    )(x, dxn, gain.reshape(1, d), kv, wq_g, wo_g, after)


def loss_head(x, target, gain, *, name, tm=None):
    t, d = x.shape
    tm = _row_tile(t, tm)

    def body(x_ref, t_ref, g_ref, dx_ref, dg_ref, loss_ref):
        @pl.when(pl.program_id(0) == 0)
        def _():
            dg_ref[...] = jnp.zeros_like(dg_ref)
            loss_ref[...] = jnp.zeros_like(loss_ref)
        g = g_ref[...]
        xhat, r = _rms_fwd(x_ref[...], None)
        err = xhat * g - t_ref[...]
        loss_ref[...] += 0.5 * jnp.sum(jnp.sum(err * err, axis=-1, keepdims=True) / d,
                                       axis=0, keepdims=True)
        dx, dg = _rms_bwd(xhat, r, g, err / d)
        dx_ref[...] = dx
        dg_ref[...] += dg

    row = lambda i: (i, 0)
    fix = lambda i: (0, 0)
    return pl.pallas_call(
        body, name=name, grid=(t // tm,),
        out_shape=[jax.ShapeDtypeStruct((t, d), F32), jax.ShapeDtypeStruct((1, d), F32),
                   jax.ShapeDtypeStruct((1, 1), F32)],
        in_specs=[pl.BlockSpec((tm, d), row), pl.BlockSpec((tm, d), row), pl.BlockSpec((1, d), fix)],
        out_specs=[pl.BlockSpec((tm, d), row), pl.BlockSpec((1, d), fix), pl.BlockSpec((1, 1), fix)],
        compiler_params=_params("arbitrary"),
    )(x, target, gain.reshape(1, d))


def _adamw_math(w, g, m, v):
    m = ADAM_B1 * m + (1.0 - ADAM_B1) * g
    v = ADAM_B2 * v + (1.0 - ADAM_B2) * (g * g)
    m_hat = m / (1.0 - ADAM_B1 ** ADAM_STEP)
    v_hat = v / (1.0 - ADAM_B2 ** ADAM_STEP)
    delta = -ADAM_LR * (m_hat / (jnp.sqrt(v_hat) + ADAM_EPS) + ADAM_WD * w)
    return delta, m, v


def adamw_sharded(own, lands, w, m, v, me_arr, *, name):
    nl, r, c = w.shape
    assert nl == len(own) == len(lands) == 2
    tr = next(cand for cand in (256, 176, 128, r) if r % cand == 0)
    nr = r // tr

    def body(me_ref, o0, o1, l0, l1, w_ref, m_ref, v_ref, g_out, d_out, m_out, v_out):
        def total(o_ref, l_ref):
            acc = o_ref[...].astype(F32)
            for p in range(N_DEV - 1):
                acc = acc + l_ref[p].astype(F32)
            return acc
        g = jnp.where(pl.program_id(0) == 0, total(o0, l0), total(o1, l1))
        delta, mn, vn = _adamw_math(w_ref[...], g, m_ref[...], v_ref[...])
        g_out[...] = g
        d_out[...] = delta
        m_out[...] = mn
        v_out[...] = vn

    row0 = lambda l, i: jnp.where(l == 0, i, nr - 1)
    row1 = lambda l, i: jnp.where(l == 1, i, 0)
    blk = pl.BlockSpec((None, tr, c), lambda l, i, me: (l, i, 0))
    grid_spec = pltpu.PrefetchScalarGridSpec(
        num_scalar_prefetch=1, grid=(nl, nr),
        in_specs=[pl.BlockSpec((None, tr, c), lambda l, i, me: (me[0], row0(l, i), 0)),
                  pl.BlockSpec((None, tr, c), lambda l, i, me: (me[0], row1(l, i), 0)),
                  pl.BlockSpec((N_DEV - 1, tr, c), lambda l, i, me: (0, row0(l, i), 0)),
                  pl.BlockSpec((N_DEV - 1, tr, c), lambda l, i, me: (0, row1(l, i), 0)),
                  blk, blk, blk],
        out_specs=[blk] * 4)
    return pl.pallas_call(
        body, name=name, grid_spec=grid_spec,
        out_shape=[jax.ShapeDtypeStruct((nl, r, c), F32)] * 4,
        compiler_params=_params("arbitrary", "arbitrary"),
    )(me_arr, own[0], own[1], lands[0], lands[1], w, m, v)


def adamw_many(gs, ws, ms, vs, *, name):
    n = len(ws)
    shapes = [w.shape for w in ws]
    as2d = lambda a: a.reshape(1, -1) if a.ndim == 1 else a

    def body(*refs):
        g_r, w_r, m_r, v_r = refs[:n], refs[n:2 * n], refs[2 * n:3 * n], refs[3 * n:4 * n]
        outs = refs[4 * n:]
        for i in range(n):
            delta, mn, vn = _adamw_math(w_r[i][...], g_r[i][...], m_r[i][...], v_r[i][...])
            outs[3 * i][...] = delta
            outs[3 * i + 1][...] = mn
            outs[3 * i + 2][...] = vn

    operands = [as2d(a) for group in (gs, ws, ms, vs) for a in group]
    out_shape = [jax.ShapeDtypeStruct(as2d(w).shape, F32) for w in ws for _ in range(3)]
    outs = pl.pallas_call(
        body, name=name, out_shape=out_shape,
        in_specs=[VMEM_SPEC] * (4 * n), out_specs=[VMEM_SPEC] * (3 * n),
        compiler_params=pltpu.CompilerParams(vmem_limit_bytes=VMEM_LIMIT),
    )(*operands)
    return [tuple(outs[3 * i + k].reshape(shapes[i]) for k in range(3)) for i in range(n)]


def cast_into_slot(a, layer, me_arr, *, name, dtype=None, after=None):
    dtype = BF16 if dtype is None else dtype
    _, r, c = a.shape
    tr = next(cand for cand in (256, 176, 128, r) if r % cand == 0)
    extra = [] if after is None else [after]

    def body(me_ref, a_ref, *rest):
        rest[-1][...] = a_ref[...].astype(dtype)

    grid_spec = pltpu.PrefetchScalarGridSpec(
        num_scalar_prefetch=1, grid=(r // tr,),
        in_specs=[pl.BlockSpec((None, tr, c), lambda i, me: (layer, i, 0))] + [ANY_SPEC] * len(extra),
        out_specs=pl.BlockSpec((None, tr, c), lambda i, me: (me[0], i, 0)))
    return pl.pallas_call(
        body, name=name, grid_spec=grid_spec,
        out_shape=jax.ShapeDtypeStruct((N_DEV, r, c), dtype),
        compiler_params=_params("parallel"),
    )(me_arr, a, *extra)


def _pack(arrs, rows):
    flat = jnp.concatenate([a.reshape(-1).astype(F32) for a in arrs])
    pad = rows * 128 - flat.shape[0]
    assert pad >= 0
    if pad:
        flat = jnp.concatenate([flat, jnp.zeros((pad,), F32)])
    return flat.reshape(rows, 128)


def _unpack(packed, shapes):
    flat = packed.reshape(-1)
    out, pos = [], 0
    for s in shapes:
        n = 1
        for dim in s:
            n *= dim
        out.append(flat[pos:pos + n].reshape(s))
        pos += n
    return out


def _rows_for(shapes):
    n = 0
    for s in shapes:
        k = 1
        for dim in s:
            k *= dim
        n += k
    return -(-n // 1024) * 8


GATHER_GROUPS = (("ffn1", ("ffn1_w_in", "ffn1_w_out")),
                 ("mid", ("mix_w_in", "mix_w_out", "xattn_wkv", "xattn_wq", "xattn_wo")),
                 ("ffn2", ("ffn2_w_in", "ffn2_w_out")))
SMALL_REPL = ["norm_ffn1", "norm_mix", "sgu_norm_g", "sgu_w", "sgu_b", "cconv_ln_g", "cconv_ln_b",
              "pool_w", "pool_scale", "norm_xattn", "norm_mem", "norm_ffn2", "norm_final"]
SMALL_SHARD = ["sconv_w", "cconv_w"]
TRANSPOSED = ("ffn1_w_in", "ffn2_w_in")
WEIGHTS = ["norm_ffn1", "ffn1_w_in", "ffn1_w_out", "norm_mix", "mix_w_in", "sconv_w", "sgu_norm_g",
           "sgu_w", "sgu_b", "cconv_w", "cconv_ln_g", "cconv_ln_b", "pool_w", "pool_scale", "mix_w_out",
           "norm_xattn", "norm_mem", "xattn_wq", "xattn_wkv", "xattn_wo", "norm_ffn2", "ffn2_w_in",
           "ffn2_w_out", "norm_final"]


def kernel(x, mem, norm_ffn1, ffn1_w_in, ffn1_w_out, norm_mix, mix_w_in, sconv_w, sgu_norm_g, sgu_w, sgu_b, cconv_w, cconv_ln_g, cconv_ln_b, pool_w, pool_scale, mix_w_out, norm_xattn, norm_mem, xattn_wq, xattn_wkv, xattn_wo, norm_ffn2, ffn2_w_in, ffn2_w_out, norm_final, loss_target, m_norm_ffn1, m_ffn1_w_in, m_ffn1_w_out, m_norm_mix, m_mix_w_in, m_sconv_w, m_sgu_norm_g, m_sgu_w, m_sgu_b, m_cconv_w, m_cconv_ln_g, m_cconv_ln_b, m_pool_w, m_pool_scale, m_mix_w_out, m_norm_xattn, m_norm_mem, m_xattn_wq, m_xattn_wkv, m_xattn_wo, m_norm_ffn2, m_ffn2_w_in, m_ffn2_w_out, m_norm_final, v_norm_ffn1, v_ffn1_w_in, v_ffn1_w_out, v_norm_mix, v_mix_w_in, v_sconv_w, v_sgu_norm_g, v_sgu_w, v_sgu_b, v_cconv_w, v_cconv_ln_g, v_cconv_ln_b, v_pool_w, v_pool_scale, v_mix_w_out, v_norm_xattn, v_norm_mem, v_xattn_wq, v_xattn_wkv, v_xattn_wo, v_norm_ffn2, v_ffn2_w_in, v_ffn2_w_out, v_norm_final):
    args = dict(locals())
    wts = {n: args[n] for n in WEIGHTS}
    mom = {n: args["m_" + n] for n in WEIGHTS}
    var = {n: args["v_" + n] for n in WEIGHTS}
    for n in TRANSPOSED:
        wts[n], mom[n], var[n] = (jnp.swapaxes(a, 1, 2) for a in (wts[n], mom[n], var[n]))
    x0 = x[0]
    mem0 = mem[0]
    target = loss_target[0]
    t, d = x0.shape
    nl = norm_ffn1.shape[0]
    w = MIX_W
    me = _my_index()

    me_arr = jnp.reshape(me, (1,)).astype(jnp.int32)

    small_g = all_gather([sconv_w, cconv_w], name="gather_conv_taps")
    sconv_full = jnp.transpose(small_g[0], (1, 2, 0, 3)).reshape(nl, SCONV_K, w)
    cconv_full = jnp.transpose(small_g[1], (1, 2, 0, 3)).reshape(nl, CCONV_K, w)
    pending = {}
    token = small_g[1]
    masks = GATHER_MASKS
    keys = [(gname, l, members) for l in range(nl) for gname, members in GATHER_GROUPS]
    first = [[cast_into_slot(wts[n], keys[0][1], me_arr, name=f"cast_{n}{keys[0][1]}") for n in keys[0][2]]]
    started, token = gather_start_groups(first, token, name="gather_start_first", masks=masks)
    casts = [[cast_into_slot(wts[n], l, me_arr, name=f"cast_{n}{l}", after=token) for n in members]
             for gname, l, members in keys[1:]]
    rest, token = gather_start_groups(casts, token, name="gather_start_rest", masks=masks)
    for (gname, l, members), (send, recv, gs) in zip(keys, started + rest):
        pending[gname, l] = (members, gs, send, recv, masks)
    wg = [dict() for _ in range(nl)]

    handing_over = {}

    def arrive_early(gname, l, after):
        members, gs, send, recv, masks = pending.pop((gname, l))
        gs = gather_wait(gs, send, recv, after, name=f"gather_wait_{gname}{l}", masks=masks)
        fsend, frecv, gs, _ = forward_start(gs, after, name=f"gather_forward_start_{gname}{l}")
        handing_over[gname, l] = (members, gs, fsend, frecv)

    def arrive(gname, l, after):
        if (gname, l) in handing_over:
            members, gs, fsend, frecv = handing_over.pop((gname, l))
            gs = forward_wait(gs, fsend, frecv, after, name=f"gather_forward_wait_{gname}{l}")
        else:
            members, gs, send, recv, masks = pending.pop((gname, l))
            gs = gather_wait(gs, send, recv, after, name=f"gather_wait_{gname}{l}", masks=masks)
            gs = sibling_forward(gs, name=f"gather_forward_{gname}{l}")
        wg[l].update(zip(members, gs))
    sconv_pad = jnp.pad(sconv_full, ((0, 0), (0, 8 - SCONV_K), (0, 0)))
    cconv_pad = jnp.pad(cconv_full, ((0, 0), (0, 32 - CCONV_K), (0, 0)))
    zeros_w = jnp.zeros((nl, w), F32)
    vecs = jnp.stack([sgu_norm_g, cconv_ln_g, cconv_ln_b, pool_scale] + [zeros_w] * 4, axis=1)
    wt = jnp.tril(sgu_w).astype(BF16)
    bexp = jnp.repeat(jnp.swapaxes(sgu_b, 1, 2), w // N_HEADS, axis=2)
    eye = jnp.eye(4, dtype=F32)
    pbd = jnp.einsum("lgcd,gh->lgchd", pool_w, eye).reshape(nl, w, w).astype(BF16)

    def mixer_args(l):
        return sconv_pad[l], cconv_pad[l], vecs[l], wt[l], bexp[l], pbd[l]

    saved = []
    xc = x0
    after = token
    for l in range(nl):
        s = {"x_ffn1": xc}
        arrive("ffn1", l, after)
        xc, s["gu_ffn1"] = ffn_fwd(xc, norm_ffn1[l], wg[l]["ffn1_w_in"], wg[l]["ffn1_w_out"],
                                   name=f"ffn1_fwd{l}", tm=FFN_FWD_TILE)
        s["x_mix"] = xc
        arrive("mid", l, xc)
        z = mm_rows(xc, wg[l]["mix_w_in"], "col", gain=norm_mix[l], name=f"mix_in{l}")
        y, xc = mixer_fwd(z, *mixer_args(l), xc, wg[l]["mix_w_out"], name=f"mixer_fwd{l}")
        s["z"], s["y"] = z, y
        s["x_att"] = xc
        kv = mm_rows(mem0, wg[l]["xattn_wkv"], "col", gain=norm_mem[l], name=f"kv{l}")
        s["kv"] = kv
        if l > 0:
            arrive_early("ffn2", l, kv)
        xc = xattn_fwd(xc, norm_xattn[l], kv, wg[l]["xattn_wq"], wg[l]["xattn_wo"], name=f"xattn_fwd{l}")
        s["x_ffn2"] = xc
        arrive("ffn2", l, xc)
        xc, s["gu_ffn2"] = ffn_fwd(xc, norm_ffn2[l], wg[l]["ffn2_w_in"], wg[l]["ffn2_w_out"],
                                   name=f"ffn2_fwd{l}", tm=FFN_FWD_TILE)
        after = xc
        saved.append(s)

    dx, g_norm_final, loss_local = loss_head(xc, target, norm_final, name="loss_head")

    tm = _row_tile(t, TN_TILE)
    small ={n: [None] * nl for n in SMALL_REPL + SMALL_SHARD if n != "norm_final"}
    scattered = {}
    tie = [token]

    def send_grads(gname, l, grads):
        members = list(grads)
        send, recv, gs, lands, tie[0] = scatter_start(
            [grads[n] for n in members], tie[0], name=f"scatter_start_{gname}{l}")
        scattered[gname, l] = (members, gs, lands, send, recv)

    names = SMALL_REPL + SMALL_SHARD + ["loss"]
    small_pending = []

    def start_small():
        small_full = {n: jnp.stack(v) for n, v in small.items()}
        small_full["norm_final"] = g_norm_final[0]
        small_full["loss"] = loss_local[0]
        shapes = [small_full[n].shape for n in names]
        packed = _pack([small_full[n] for n in names], _rows_for(shapes))
        slot = cast_into_slot(packed[None], 0, me_arr, name="small_into_slot", dtype=F32)
        send, recv, gs, tie[0] = gather_start([slot], tie[0], name="small_gather_start", masks=ALL_MASKS)
        small_pending.append((gs, send, recv, shapes))

    def ffn_backward(which, l, x_in, dy, gu, gain):
        w_in, w_out = wg[l][which + "_w_in"], wg[l][which + "_w_out"]
        dx_, h_, act, dgu, dgn, dyh = ffn_bwd_rows(x_in, dy, gu, gain, w_in, w_out, tie[0],
                                                   name=f"{which}_bwd{l}_rows")
        small["norm_" + which][l] = dgn[0]
        last = which == "ffn1" and l == 0
        if last:
            start_small()
        g_in = ffn_grad_w_in(h_, dgu, tie[0], name=f"{which}_bwd{l}_dwin")
        if last:
            send_grads(which + "_in", l, {which + "_w_in": g_in})
        g_out = ffn_grad_w_out(act, dyh, tie[0], name=f"{which}_bwd{l}_dwout")
        if last:
            send_grads(which + "_out", l, {which + "_w_out": g_out})
        else:
            send_grads(which, l, {which + "_w_in": g_in, which + "_w_out": g_out})
        return dx_

    for l in reversed(range(nl)):
        s = saved[l]
        wl = wg[l]
        dx = ffn_backward("ffn2", l, s["x_ffn2"], dx, s["gu_ffn2"], norm_ffn2[l])

        bg = {}
        dxn = dx
        dx, h, dq, o, dkv, dgn = xattn_bwd_rows(
            s["x_att"], dxn, norm_xattn[l], s["kv"], wl["xattn_wq"], wl["xattn_wo"], tie[0],
            name=f"xattn_bwd{l}")
        small["norm_xattn"][l] = dgn[0]
        row_spec = pl.BlockSpec((tm, d), lambda s_, i: (i, 0))
        bg["xattn_wq"] = mm_tn(h, dq, nb=1, ka=d, nbk=d, tm=tm, m=t, a_spec=row_spec, b_spec=row_spec,
                               name=f"dwq{l}").reshape(N_DEV, d // N_DEV, d)
        bg["xattn_wo"] = mm_tn(o, dxn, nb=1, ka=d, nbk=d, tm=tm, m=t, a_spec=row_spec, b_spec=row_spec,
                               name=f"dwo{l}").reshape(N_DEV, d // N_DEV, d)
        _, mhat, dgn = mm_nt(dkv, wl["xattn_wkv"], "col", x=mem0, gain=norm_mem[l], name=f"dmem{l}")
        small["norm_mem"][l] = dgn[0]
        nm = mem0.shape[0]
        bg["xattn_wkv"] = mm_tn(mhat, dkv, nb=N_DEV, ka=d, nbk=2 * d // N_DEV, tm=nm, m=nm,
                                a_spec=pl.BlockSpec((nm, d), lambda s_, i: (0, 0)),
                                b_spec=pl.BlockSpec((nm, 2 * d // N_DEV), lambda s_, i: (0, s_)),
                                name=f"dwkv{l}")
        send_grads("xattn", l, bg)

        bg = {}
        dxn = dx
        bg["mix_w_out"] = mm_tn(s["y"], dxn, nb=1, ka=d, nbk=d, tm=tm, m=t, a_spec=row_spec,
                                b_spec=row_spec, name=f"dwmo{l}").reshape(N_DEV, d // N_DEV, d)
        dz, gvec, gcc, gwt, gb, gpbd = mixer_bwd(s["z"], dxn, wl["mix_w_out"], *mixer_args(l),
                                                 name=f"mixer_bwd{l}")
        small["sconv_w"][l] = gvec[0:SCONV_K]
        small["sgu_norm_g"][l] = gvec[3]
        small["cconv_ln_g"][l] = gvec[4]
        small["cconv_ln_b"][l] = gvec[5]
        small["pool_scale"][l] = gvec[6]
        small["cconv_w"][l] = gcc[0:CCONV_K]
        small["sgu_w"][l] = gwt
        small["sgu_b"][l] = jnp.transpose(gb[:, 0:N_HEADS])
        gw = w // 4
        small["pool_w"][l] = jnp.stack([gpbd[g * gw:(g + 1) * gw, g * gw:(g + 1) * gw] for g in range(4)])
        dx, h, dgn = mm_nt(dz, wl["mix_w_in"], "col", x=s["x_mix"], gain=norm_mix[l], dx_in=dxn,
                           after=tie[0], name=f"dh_mix{l}")
        small["norm_mix"][l] = dgn[0]
        th = _row_tile(t, TN_TILE // 2)
        bg["mix_w_in"] = mm_tn(h, dz, nb=1, ka=d, nbk=N_DEV * w, tm=th, m=t, col_slots=N_DEV,
                               a_spec=pl.BlockSpec((th, d), lambda s_, i: (i, 0)),
                               b_spec=pl.BlockSpec((th, N_DEV * w), lambda s_, i: (i, 0)), name=f"dwmi{l}")
        send_grads("mix", l, bg)

        dx = ffn_backward("ffn1", l, s["x_ffn1"], dx, s["gu_ffn1"], norm_ffn1[l])

    out = {}

    def finish(keys, after):
        own, land = {}, {}
        for gname, l in keys:
            members, gs, lands, send, recv = scattered.pop((gname, l))
            gs, lands = scatter_wait(gs, lands, send, recv, after, name=f"scatter_wait_{gname}{l}")
            for n, g_, l_ in zip(members, gs, lands):
                own.setdefault(n, {})[l] = g_
                land.setdefault(n, {})[l] = l_
        for n in own:
            out[n] = adamw_sharded([own[n][l] for l in range(nl)], [land[n][l] for l in range(nl)],
                                   wts[n], mom[n], var[n], me_arr, name="adamw_" + n)
            after = out[n][1]
        return after

    after = tie[0]
    for gname in ("ffn2", "xattn", "mix"):
        after = finish([(gname, l) for l in reversed(range(nl))], after)
    (gs, send, recv, shapes), = small_pending
    gs = gather_wait(gs, send, recv, after, name="small_gather_wait", masks=ALL_MASKS)
    summed = sum_slots(gs[0], name="small_sum")
    gsm = dict(zip(names, _unpack(summed, shapes)))
    loss = gsm["loss"][0]
    finish([("ffn1", l) for l in reversed(range(1, nl))] + [("ffn1_in", 0), ("ffn1_out", 0)], summed)
    cs = w // N_DEV
    for n in SMALL_SHARD:
        gsm[n] = lax.dynamic_slice_in_dim(gsm[n], me * cs, cs, axis=2)
    small_names = SMALL_REPL + SMALL_SHARD
    upd = adamw_many([gsm[n] for n in small_names], [wts[n] for n in small_names],
                     [mom[n] for n in small_names], [var[n] for n in small_names], name="adamw_small")
    for n, (a, b, c) in zip(small_names, upd):
        out[n] = (gsm[n], a, b, c)
    for n in TRANSPOSED:
        out[n] = tuple(jnp.swapaxes(a, 1, 2) for a in out[n])

    grad_x = dx.reshape(1, t, d)
    return (loss, grad_x, *[out[n][0] for n in WEIGHTS], *[out[n][1] for n in WEIGHTS],
            *[out[n][2] for n in WEIGHTS], *[out[n][3] for n in WEIGHTS])
```

```python
import jax
import jax.numpy as jnp
from jax import lax
from jax.experimental import pallas as pl
from jax.experimental.pallas import tpu as pltpu

F32 = jnp.float32
BF16 = jnp.bfloat16
MESH = pl.DeviceIdType.MESH
N_DEV = 8
EPS = 1e-6
HALO = 32
SGU_CHUNK = 128
CCONV_K = 31
SCONV_K = 3
MIX_W = 256
N_HEADS = 4
VMEM_LIMIT = 56 * 1024 * 1024
ROW_TILE = 512
TN_TILE = 2048
FFN_FWD_TILE = 1024
FFN_BWD_SPLIT = 2
FFN_FWD_SPLIT = 2
MIX_TILE = 512

ADAM_LR = 0.001
ADAM_B1 = 0.9
ADAM_B2 = 0.999
ADAM_EPS = 1e-08
ADAM_WD = 0.01
ADAM_STEP = 10

HBM_SPEC = pl.BlockSpec(memory_space=pltpu.HBM)
VMEM_SPEC = pl.BlockSpec(memory_space=pltpu.VMEM)


def _params(*sem):
    return pltpu.CompilerParams(dimension_semantics=tuple(sem), vmem_limit_bytes=VMEM_LIMIT)


def _row_tile(m, pref=None):
    t = min(m, ROW_TILE if pref is None else pref)
    assert m % t == 0, (m, t)
    return t


def _my_index():
    return lax.axis_index("x") * 4 + lax.axis_index("y") * 2 + lax.axis_index("c")


def _peer(mask):
    x, y, c = lax.axis_index("x"), lax.axis_index("y"), lax.axis_index("c")
    px = 1 - x if mask & 4 else x
    py = 1 - y if mask & 2 else y
    pc = 1 - c if mask & 1 else c
    return (px, py, pc), px * 4 + py * 2 + pc


def all_gather(arrs, name):
    n = len(arrs)

    def body(*refs):
        ins, outs = refs[:n], refs[n:2 * n]
        send_sems, recv_sems, loc_sems = refs[2 * n:]
        me = _my_index()
        local = []
        for i in range(n):
            cp = pltpu.make_async_copy(ins[i], outs[i].at[me], loc_sems.at[i])
            cp.start()
            local.append(cp)
        sends = []
        for i in range(n):
            for m in range(1, N_DEV):
                peer, _ = _peer(m)
                cp = pltpu.make_async_remote_copy(
                    src_ref=ins[i], dst_ref=outs[i].at[me],
                    send_sem=send_sems.at[i, m - 1], recv_sem=recv_sems.at[i, m - 1],
                    device_id=peer, device_id_type=MESH)
                cp.start()
                sends.append(cp)
        for i in range(n):
            for m in range(1, N_DEV):
                peer, pidx = _peer(m)
                pltpu.make_async_remote_copy(
                    src_ref=ins[i], dst_ref=outs[i].at[pidx],
                    send_sem=send_sems.at[i, m - 1], recv_sem=recv_sems.at[i, m - 1],
                    device_id=peer, device_id_type=MESH).wait_recv()
        for cp in sends:
            cp.wait_send()
        for cp in local:
            cp.wait()

    return pl.pallas_call(
        body, name=name,
        out_shape=[jax.ShapeDtypeStruct((N_DEV,) + a.shape, a.dtype) for a in arrs],
        in_specs=[HBM_SPEC] * n, out_specs=[HBM_SPEC] * n,
        scratch_shapes=[pltpu.SemaphoreType.DMA((n, N_DEV - 1)),
                        pltpu.SemaphoreType.DMA((n, N_DEV - 1)),
                        pltpu.SemaphoreType.DMA((n,))],
    )(*arrs)


SEM_SPEC = pl.BlockSpec(memory_space=pltpu.SEMAPHORE)
ANY_SPEC = pl.BlockSpec(memory_space=pl.ANY)
SIDE_EFFECT = pltpu.SideEffectType.DATAFLOW_SIDE_EFFECTING


def _hbm(a):
    return pltpu.with_memory_space_constraint(a, pltpu.HBM)


def _sem_pairs(n):
    return (pltpu.SemaphoreType.DMA((n * (N_DEV - 1),)), pltpu.SemaphoreType.DMA((n * (N_DEV - 1),)))


def _sem(i, m):
    return i * (N_DEV - 1) + m - 1


def _gather_copy(g_ref, i, m, send_sems, recv_sems, origin):
    peer, _ = _peer(m)
    return pltpu.make_async_remote_copy(
        src_ref=g_ref.at[origin], dst_ref=g_ref.at[origin],
        send_sem=send_sems.at[_sem(i, m)], recv_sem=recv_sems.at[_sem(i, m)],
        device_id=peer, device_id_type=MESH)


GATHER_MASKS = (1, 2, 4, 6)
FORWARD_MASKS = (2, 4, 6)


ALL_MASKS = tuple(range(1, N_DEV))


def gather_start(gs, after, name, masks=GATHER_MASKS):
    n = len(gs)

    def body(*refs):
        g_in = refs[:n]
        send_sems, recv_sems = refs[n + 1], refs[n + 2]
        token = refs[-1]
        me = _my_index()
        for i in range(n):
            for m in masks:
                _gather_copy(g_in[i], i, m, send_sems, recv_sems, me).start()
        token[...] = jnp.zeros_like(token)

    outs = pl.pallas_call(
        body, name=name,
        out_shape=(*_sem_pairs(n), *[pltpu.HBM(g.shape, g.dtype) for g in gs],
                   jax.ShapeDtypeStruct((8, 128), F32)),
        in_specs=[HBM_SPEC] * n + [ANY_SPEC],
        out_specs=(SEM_SPEC, SEM_SPEC, *[HBM_SPEC] * n, VMEM_SPEC),
        input_output_aliases={i: 2 + i for i in range(n)},
        compiler_params=pltpu.CompilerParams(has_side_effects=SIDE_EFFECT),
    )(*[_hbm(g) for g in gs], after)
    return outs[0], outs[1], list(outs[2:2 + n]), outs[-1]


def gather_start_groups(groups, after, name, masks=GATHER_MASKS):
    sizes = [len(g) for g in groups]
    flat = [a for g in groups for a in g]
    n, ng = len(flat), len(groups)

    def body(*refs):
        g_in = refs[:n]
        sems = refs[n + 1:n + 1 + 2 * ng]
        token = refs[-1]
        me = _my_index()
        pos = 0
        for k, size in enumerate(sizes):
            for i in range(size):
                for m in masks:
                    _gather_copy(g_in[pos + i], i, m, sems[2 * k], sems[2 * k + 1], me).start()
            pos += size
        token[...] = jnp.zeros_like(token)

    outs = pl.pallas_call(
        body, name=name,
        out_shape=(*[s for size in sizes for s in _sem_pairs(size)],
                   *[pltpu.HBM(g.shape, g.dtype) for g in flat], jax.ShapeDtypeStruct((8, 128), F32)),
        in_specs=[HBM_SPEC] * n + [ANY_SPEC],
        out_specs=(*[SEM_SPEC] * (2 * ng), *[HBM_SPEC] * n, VMEM_SPEC),
        input_output_aliases={i: 2 * ng + i for i in range(n)},
        compiler_params=pltpu.CompilerParams(has_side_effects=SIDE_EFFECT),
    )(*[_hbm(g) for g in flat], after)
    result, pos = [], 2 * ng
    for k, size in enumerate(sizes):
        result.append((outs[2 * k], outs[2 * k + 1], list(outs[pos:pos + size])))
        pos += size
    return result, outs[-1]


def gather_wait(gs, send_sems, recv_sems, after, name, masks=GATHER_MASKS):
    n = len(gs)

    def body(*refs):
        g_in = refs[:n]
        send, recv = refs[n], refs[n + 1]
        me = _my_index()
        for i in range(n):
            for m in masks:
                _, pidx = _peer(m)
                _gather_copy(g_in[i], i, m, send, recv, me).wait_send()
                _gather_copy(g_in[i], i, m, send, recv, pidx).wait_recv()

    outs = pl.pallas_call(
        body, name=name,
        out_shape=[pltpu.HBM(g.shape, g.dtype) for g in gs],
        in_specs=[HBM_SPEC] * n + [SEM_SPEC, SEM_SPEC, ANY_SPEC],
        out_specs=[HBM_SPEC] * n,
        input_output_aliases={i: i for i in range(n)},
        compiler_params=pltpu.CompilerParams(has_side_effects=SIDE_EFFECT),
    )(*gs, send_sems, recv_sems, after)
    return list(outs)


def sibling_forward(gs, name):
    n = len(gs)
    nf = len(FORWARD_MASKS)

    def body(*refs):
        g_in = refs[:n]
        send_sems, recv_sems = refs[2 * n:]
        x, y, c = lax.axis_index("x"), lax.axis_index("y"), lax.axis_index("c")
        sibling = (x, y, 1 - c)

        def copy(i, k, origin):
            return pltpu.make_async_remote_copy(
                src_ref=g_in[i].at[origin], dst_ref=g_in[i].at[origin],
                send_sem=send_sems.at[i * nf + k], recv_sem=recv_sems.at[i * nf + k],
                device_id=sibling, device_id_type=MESH)
        sends = []
        for i in range(n):
            for k, m in enumerate(FORWARD_MASKS):
                _, origin = _peer(m)
                cp = copy(i, k, origin)
                cp.start()
                sends.append(cp)
        for i in range(n):
            for k, m in enumerate(FORWARD_MASKS):
                _, origin = _peer(m ^ 1)
                copy(i, k, origin).wait_recv()
        for cp in sends:
            cp.wait_send()

    outs = pl.pallas_call(
        body, name=name,
        out_shape=[jax.ShapeDtypeStruct(g.shape, g.dtype) for g in gs],
        in_specs=[HBM_SPEC] * n, out_specs=[HBM_SPEC] * n,
        input_output_aliases={i: i for i in range(n)},
        scratch_shapes=[pltpu.SemaphoreType.DMA((n * nf,)), pltpu.SemaphoreType.DMA((n * nf,))],
    )(*gs)
    return list(outs)


def _forward_copy(g_ref, i, k, send_sems, recv_sems, origin):
    sibling = (lax.axis_index("x"), lax.axis_index("y"), 1 - lax.axis_index("c"))
    slot = i * len(FORWARD_MASKS) + k
    return pltpu.make_async_remote_copy(
        src_ref=g_ref.at[origin], dst_ref=g_ref.at[origin],
        send_sem=send_sems.at[slot], recv_sem=recv_sems.at[slot],
        device_id=sibling, device_id_type=MESH)


def forward_start(gs, after, name):
    n = len(gs)
    nsem = n * len(FORWARD_MASKS)

    def body(*refs):
        g_in = refs[:n]
        send_sems, recv_sems = refs[n + 1], refs[n + 2]
        token = refs[-1]
        for i in range(n):
            for k, m in enumerate(FORWARD_MASKS):
                _, origin = _peer(m)
                _forward_copy(g_in[i], i, k, send_sems, recv_sems, origin).start()
        token[...] = jnp.zeros_like(token)

    outs = pl.pallas_call(
        body, name=name,
        out_shape=(pltpu.SemaphoreType.DMA((nsem,)), pltpu.SemaphoreType.DMA((nsem,)),
                   *[pltpu.HBM(g.shape, g.dtype) for g in gs], jax.ShapeDtypeStruct((8, 128), F32)),
        in_specs=[HBM_SPEC] * n + [ANY_SPEC],
        out_specs=(SEM_SPEC, SEM_SPEC, *[HBM_SPEC] * n, VMEM_SPEC),
        input_output_aliases={i: 2 + i for i in range(n)},
        compiler_params=pltpu.CompilerParams(has_side_effects=SIDE_EFFECT),
    )(*[_hbm(g) for g in gs], after)
    return outs[0], outs[1], list(outs[2:2 + n]), outs[-1]


def forward_wait(gs, send_sems, recv_sems, after, name):
    n = len(gs)

    def body(*refs):
        g_in = refs[:n]
        send, recv = refs[n], refs[n + 1]
        for i in range(n):
            for k, m in enumerate(FORWARD_MASKS):
                _, mine = _peer(m)
                _, theirs = _peer(m ^ 1)
                _forward_copy(g_in[i], i, k, send, recv, mine).wait_send()
                _forward_copy(g_in[i], i, k, send, recv, theirs).wait_recv()

    outs = pl.pallas_call(
        body, name=name,
        out_shape=[pltpu.HBM(g.shape, g.dtype) for g in gs],
        in_specs=[HBM_SPEC] * n + [SEM_SPEC, SEM_SPEC, ANY_SPEC],
        out_specs=[HBM_SPEC] * n,
        input_output_aliases={i: i for i in range(n)},
        compiler_params=pltpu.CompilerParams(has_side_effects=SIDE_EFFECT),
    )(*gs, send_sems, recv_sems, after)
    return list(outs)


def _scatter_copy(g_ref, l_ref, i, m, send_sems, recv_sems):
    peer, pidx = _peer(m)
    return pltpu.make_async_remote_copy(
        src_ref=g_ref.at[pidx], dst_ref=l_ref.at[m - 1],
        send_sem=send_sems.at[_sem(i, m)], recv_sem=recv_sems.at[_sem(i, m)],
        device_id=peer, device_id_type=MESH)


def scatter_start(grads, after, name):
    n = len(grads)
    lands = [lax.empty((N_DEV - 1,) + g.shape[1:], g.dtype) for g in grads]

    def body(*refs):
        g_in, l_in = refs[:n], refs[n:2 * n]
        send_sems, recv_sems = refs[2 * n + 1], refs[2 * n + 2]
        token = refs[-1]
        for i in range(n):
            for m in range(1, N_DEV):
                _scatter_copy(g_in[i], l_in[i], i, m, send_sems, recv_sems).start()
        token[...] = jnp.zeros_like(token)

    outs = pl.pallas_call(
        body, name=name,
        out_shape=(*_sem_pairs(n), *[pltpu.HBM(g.shape, g.dtype) for g in grads],
                   *[pltpu.HBM(l.shape, l.dtype) for l in lands], jax.ShapeDtypeStruct((8, 128), F32)),
        in_specs=[HBM_SPEC] * (2 * n) + [ANY_SPEC],
        out_specs=(SEM_SPEC, SEM_SPEC, *[HBM_SPEC] * (2 * n), VMEM_SPEC),
        input_output_aliases={i: 2 + i for i in range(2 * n)},
        compiler_params=pltpu.CompilerParams(has_side_effects=SIDE_EFFECT),
    )(*[_hbm(g) for g in grads], *[_hbm(l) for l in lands], after)
    return outs[0], outs[1], list(outs[2:2 + n]), list(outs[2 + n:2 + 2 * n]), outs[-1]


def scatter_wait(grads, lands, send_sems, recv_sems, after, name):
    n = len(grads)

    def body(*refs):
        g_in, l_in = refs[:n], refs[n:2 * n]
        send, recv = refs[2 * n], refs[2 * n + 1]
        for i in range(n):
            for m in range(1, N_DEV):
                cp = _scatter_copy(g_in[i], l_in[i], i, m, send, recv)
                cp.wait_send()
                cp.wait_recv()

    outs = pl.pallas_call(
        body, name=name,
        out_shape=[pltpu.HBM(a.shape, a.dtype) for a in list(grads) + list(lands)],
        in_specs=[HBM_SPEC] * (2 * n) + [SEM_SPEC, SEM_SPEC, ANY_SPEC],
        out_specs=[HBM_SPEC] * (2 * n),
        input_output_aliases={i: i for i in range(2 * n)},
        compiler_params=pltpu.CompilerParams(has_side_effects=SIDE_EFFECT),
    )(*grads, *lands, send_sems, recv_sems, after)
    return list(outs[:n]), list(outs[n:])


def sum_slots(g, name):
    _, r, c = g.shape

    def body(g_ref, out_ref):
        acc = g_ref[0]
        for p in range(1, N_DEV):
            acc = acc + g_ref[p]
        out_ref[...] = acc

    return pl.pallas_call(
        body, name=name, out_shape=jax.ShapeDtypeStruct((r, c), F32),
        in_specs=[VMEM_SPEC], out_specs=VMEM_SPEC,
        compiler_params=pltpu.CompilerParams(vmem_limit_bytes=VMEM_LIMIT),
    )(g)


def _sigmoid(v):
    return 1.0 / (1.0 + jnp.exp(-v))


def _rms_fwd(xf, g):
    r = lax.rsqrt(jnp.mean(xf * xf, axis=-1, keepdims=True) + EPS)
    return xf * r, r


def _rms_bwd(xhat, r, g, dy):
    dg = jnp.sum(dy * xhat, axis=0, keepdims=True)
    dxh = dy * g
    dx = r * (dxh - xhat * jnp.mean(dxh * xhat, axis=-1, keepdims=True))
    return dx, dg


def _ln_stats(v):
    mu = jnp.mean(v, axis=-1, keepdims=True)
    vc = v - mu
    r = lax.rsqrt(jnp.mean(vc * vc, axis=-1, keepdims=True) + EPS)
    return vc * r, r


def _ln_bwd(xhat, r, dxh):
    return r * (dxh - jnp.mean(dxh, axis=-1, keepdims=True)
                - xhat * jnp.mean(dxh * xhat, axis=-1, keepdims=True))


def _dot(a, b):
    return jnp.dot(a, b, preferred_element_type=F32)


def _dot_nt(a, b):
    return lax.dot_general(a, b, (((1,), (1,)), ((), ())), preferred_element_type=F32)


def _dot_tn(a, b):
    return lax.dot_general(a, b, (((0,), (0,)), ((), ())), preferred_element_type=F32)


def _full_weight(w_ref, kind):
    assert kind == "row"
    p, a, b = w_ref.shape
    return w_ref[...].reshape(p * a, b)


def _wspec(wg):
    return pl.BlockSpec(wg.shape, lambda *_: (0, 0, 0))


def mm_rows(a, wg, kind, *, gain=None, residual=None, out_dtype=F32, name, tm=None):
    m, k = a.shape
    p, wa, wb = wg.shape
    n = p * wb if kind == "col" else wb
    tm = _row_tile(m, tm)
    has_gain, has_res = gain is not None, residual is not None

    def body(*refs):
        refs = list(refs)
        a_ref = refs.pop(0)
        g_ref = refs.pop(0) if has_gain else None
        w_ref = refs.pop(0)
        r_ref = refs.pop(0) if has_res else None
        o_ref = refs.pop(0)
        if has_gain:
            xhat, _ = _rms_fwd(a_ref[...].astype(F32), None)
            h = (xhat * g_ref[...]).astype(BF16)
        else:
            h = a_ref[...].astype(BF16)
        if kind == "col":
            for j in range(p):
                o = _dot(h, w_ref[j])
                if has_res:
                    o = o + r_ref[:, j * wb:(j + 1) * wb]
                o_ref[:, j * wb:(j + 1) * wb] = o.astype(out_dtype)
        else:
            o = _dot(h, _full_weight(w_ref, "row"))
            if has_res:
                o = o + r_ref[...]
            o_ref[...] = o.astype(out_dtype)

    operands = [a]
    in_specs = [pl.BlockSpec((tm, k), lambda i: (i, 0))]
    if has_gain:
        operands.append(gain.reshape(1, k))
        in_specs.append(pl.BlockSpec((1, k), lambda i: (0, 0)))
    operands.append(wg)
    in_specs.append(_wspec(wg))
    if has_res:
        operands.append(residual)
        in_specs.append(pl.BlockSpec((tm, n), lambda i: (i, 0)))
    return pl.pallas_call(
        body, name=name, grid=(m // tm,),
        out_shape=jax.ShapeDtypeStruct((m, n), out_dtype),
        in_specs=in_specs, out_specs=pl.BlockSpec((tm, n), lambda i: (i, 0)),
        compiler_params=_params("parallel"),
    )(*operands)


def mm_nt(dz, wg, kind, *, x=None, gain=None, dx_in=None, after=None, name, tm=None):
    m, n = dz.shape
    p, wa, wb = wg.shape
    k = wa if kind == "col" else p * wa
    tm = _row_tile(m, tm)
    epi = x is not None
    has_dx = dx_in is not None
    has_after = after is not None

    def body(*refs):
        refs = list(refs)
        dz_ref, w_ref = refs.pop(0), refs.pop(0)
        if epi:
            x_ref, g_ref = refs.pop(0), refs.pop(0)
            dxi_ref = refs.pop(0) if has_dx else None
        if has_after:
            refs.pop(0)
        if epi:
            dx_ref, h_ref, dg_ref = refs
        else:
            (da_ref,) = refs
        dzb = dz_ref[...].astype(BF16)
        if kind == "col":
            da = _dot_nt(dzb[:, 0:wb], w_ref[0])
            for j in range(1, p):
                da = da + _dot_nt(dzb[:, j * wb:(j + 1) * wb], w_ref[j])
        else:
            da = _dot_nt(dzb, _full_weight(w_ref, "row"))
        if not epi:
            da_ref[...] = da
            return
        g = g_ref[...]
        xhat, r = _rms_fwd(x_ref[...].astype(F32), None)
        h_ref[...] = (xhat * g).astype(BF16)
        dx, dg = _rms_bwd(xhat, r, g, da)
        if has_dx:
            dx = dx + dxi_ref[...]
        dx_ref[...] = dx

        @pl.when(pl.program_id(0) == 0)
        def _():
            dg_ref[...] = jnp.zeros_like(dg_ref)
        dg_ref[...] += dg

    row = lambda i: (i, 0)
    operands = [dz, wg]
    in_specs = [pl.BlockSpec((tm, n), row), _wspec(wg)]
    if epi:
        operands += [x, gain.reshape(1, k)]
        in_specs += [pl.BlockSpec((tm, k), row), pl.BlockSpec((1, k), lambda i: (0, 0))]
        if has_dx:
            operands.append(dx_in)
            in_specs.append(pl.BlockSpec((tm, k), row))
        out_shape = [jax.ShapeDtypeStruct((m, k), F32), jax.ShapeDtypeStruct((m, k), BF16),
                     jax.ShapeDtypeStruct((1, k), F32)]
        out_specs = [pl.BlockSpec((tm, k), row), pl.BlockSpec((tm, k), row),
                     pl.BlockSpec((1, k), lambda i: (0, 0))]
    else:
        out_shape = jax.ShapeDtypeStruct((m, k), F32)
        out_specs = pl.BlockSpec((tm, k), row)
    if has_after:
        operands.append(after)
        in_specs.append(ANY_SPEC)
    return pl.pallas_call(
        body, name=name, grid=(m // tm,), out_shape=out_shape,
        in_specs=in_specs, out_specs=out_specs,
        compiler_params=_params("arbitrary"),
    )(*operands)


def mm_tn(a, b, *, nb, a_spec, b_spec, ka, nbk, tm, m, scale=1.0, out_dtype=BF16, col_slots=1,
          after=None, name):
    ni = m // tm
    assert col_slots == 1 or nb == 1
    cw = nbk // col_slots
    extra = [] if after is None else [after]

    def body(a_ref, b_ref, *rest):
        o_ref, acc = rest[len(extra):]
        i = pl.program_id(1)

        @pl.when(i == 0)
        def _():
            acc[...] = jnp.zeros_like(acc)
        acc[...] += _dot_tn(a_ref[...].astype(BF16), b_ref[...].astype(BF16))

        @pl.when(i == ni - 1)
        def _():
            if col_slots == 1:
                o_ref[...] = (acc[...] * scale).astype(out_dtype)
            else:
                for j in range(col_slots):
                    o_ref[j] = (acc[:, j * cw:(j + 1) * cw] * scale).astype(out_dtype)

    if col_slots == 1:
        out_shape = jax.ShapeDtypeStruct((nb, ka, nbk), out_dtype)
        out_spec = pl.BlockSpec((None, ka, nbk), lambda s, i: (s, 0, 0))
    else:
        out_shape = jax.ShapeDtypeStruct((col_slots, ka, cw), out_dtype)
        out_spec = pl.BlockSpec((col_slots, ka, cw), lambda s, i: (0, 0, 0))
    return pl.pallas_call(
        body, name=name, grid=(nb, ni), out_shape=out_shape,
        in_specs=[a_spec, b_spec] + [ANY_SPEC] * len(extra), out_specs=out_spec,
        scratch_shapes=[pltpu.VMEM((ka, nbk), F32)],
        compiler_params=_params("parallel", "arbitrary"),
    )(a, b, *extra)


def _ffn_specs(w_in_g, w_out_g, d):
    nf = w_in_g.shape[1]
    hr = w_out_g.shape[1]
    assert 2 * hr == nf
    w_in5 = w_in_g.reshape(2, 4, nf, d)
    w_out5 = w_out_g.reshape(4, 2, hr, d)
    in_spec = pl.BlockSpec((2, None, nf, d), lambda i, j: (0, j, 0, 0))
    out_spec = pl.BlockSpec((None, 2, hr, d), lambda i, j: (j, 0, 0, 0))
    return w_in5, w_out5, in_spec, out_spec, nf


def ffn_fwd(x, gain, w_in_g, w_out_g, *, name, tm=None):
    t, d = x.shape
    tm = _row_tile(t, tm)
    w_in5, w_out5, wi_spec, wo_spec, nf = _ffn_specs(w_in_g, w_out_g, d)

    def body(x_ref, g_ref, wi_ref, wo_ref, o_ref, gu_ref, h_scr, acc):
        j = pl.program_id(1)

        @pl.when(j == 0)
        def _():
            xhat, _ = _rms_fwd(x_ref[...], None)
            h_scr[...] = (xhat * g_ref[...]).astype(BF16)
            acc[...] = jnp.zeros_like(acc)
        wo = wo_ref[...].reshape(nf, d)

        def project(rows):
            h = h_scr[rows]
            return _dot_nt(h, wi_ref[0]), _dot_nt(h, wi_ref[1])

        sub = tm // FFN_FWD_SPLIT
        parts = [slice(k * sub, (k + 1) * sub) for k in range(FFN_FWD_SPLIT)]
        gt, up = project(parts[0])
        for k, rows in enumerate(parts):
            if k + 1 < len(parts):
                nxt = project(parts[k + 1])
            gu_ref[0, rows] = gt.astype(BF16)
            gu_ref[1, rows] = up.astype(BF16)
            act = (gt * _sigmoid(gt) * up).astype(BF16)
            acc[rows] += _dot(act, wo)
            if k + 1 < len(parts):
                gt, up = nxt

        @pl.when(j == 3)
        def _():
            o_ref[...] = x_ref[...] + 0.5 * acc[...]

    return pl.pallas_call(
        body, name=name, grid=(t // tm, 4),
        out_shape=[jax.ShapeDtypeStruct((t, d), F32), jax.ShapeDtypeStruct((2, 4, t, nf), BF16)],
        in_specs=[pl.BlockSpec((tm, d), lambda i, j: (i, 0)),
                  pl.BlockSpec((1, d), lambda i, j: (0, 0)), wi_spec, wo_spec],
        out_specs=[pl.BlockSpec((tm, d), lambda i, j: (i, 0)),
                   pl.BlockSpec((2, None, tm, nf), lambda i, j: (0, j, i, 0))],
        scratch_shapes=[pltpu.VMEM((tm, d), BF16), pltpu.VMEM((tm, d), F32)],
        compiler_params=_params("parallel", "arbitrary"),
    )(x, gain.reshape(1, d), w_in5, w_out5)


def ffn_bwd_rows(x, dy, gu, gain, w_in_g, w_out_g, after, *, name, tm=None):
    t, d = x.shape
    tm = _row_tile(t, tm)
    w_in5, w_out5, wi_spec, wo_spec, nf = _ffn_specs(w_in_g, w_out_g, d)

    def body(x_ref, dy_ref, gu_ref, g_ref, wi_ref, wo_ref, after_ref, dx_ref, h_ref, act_ref, dgu_ref, dg_ref,
             dyh_scr, dh_acc):
        i, j = pl.program_id(0), pl.program_id(1)

        @pl.when(j == 0)
        def _():
            xhat, _ = _rms_fwd(x_ref[...], None)
            h_ref[...] = (xhat * g_ref[...]).astype(BF16)
            dyh_scr[...] = (0.5 * dy_ref[...]).astype(BF16)
            dh_acc[...] = jnp.zeros_like(dh_acc)
        wo = wo_ref[...].reshape(nf, d)

        def gates(rows):
            gt = gu_ref[0, rows].astype(F32)
            up = gu_ref[1, rows].astype(F32)
            sg = _sigmoid(gt)
            silu = gt * sg
            act_ref[rows] = (silu * up).astype(BF16)
            return up * (sg * (1.0 + gt * (1.0 - sg))), silu

        def grads(rows, dact, dsilu_up, silu):
            dgt = (dact * dsilu_up).astype(BF16)
            dup = (dact * silu).astype(BF16)
            dgu_ref[0, rows] = dgt
            dgu_ref[1, rows] = dup
            return dgt, dup

        sub = tm // FFN_BWD_SPLIT
        parts = [slice(k * sub, (k + 1) * sub) for k in range(FFN_BWD_SPLIT)]
        dact = _dot_nt(dyh_scr[parts[0]], wo)
        gate = gates(parts[0])
        for k, rows in enumerate(parts):
            if k + 1 < len(parts):
                dact_next = _dot_nt(dyh_scr[parts[k + 1]], wo)
            dgt, dup = grads(rows, dact, *gate)
            dh_acc[rows] += _dot(dgt, wi_ref[0]) + _dot(dup, wi_ref[1])
            if k + 1 < len(parts):
                gate = gates(parts[k + 1])
                dact = dact_next

        @pl.when(j == 3)
        def _():
            g = g_ref[...]
            xhat, r = _rms_fwd(x_ref[...], None)
            dx, dg = _rms_bwd(xhat, r, g, dh_acc[...])
            dx_ref[...] = dy_ref[...] + dx

            @pl.when(i == 0)
            def _():
                dg_ref[...] = jnp.zeros_like(dg_ref)
            dg_ref[...] += dg

    row = lambda i, j: (i, 0)
    return pl.pallas_call(
        body, name=name, grid=(t // tm, 4),
        out_shape=[jax.ShapeDtypeStruct((t, d), F32), jax.ShapeDtypeStruct((t, d), BF16),
                   jax.ShapeDtypeStruct((4, t, nf), BF16), jax.ShapeDtypeStruct((2, 4, t, nf), BF16),
                   jax.ShapeDtypeStruct((1, d), F32), jax.ShapeDtypeStruct((t, d), BF16)],
        in_specs=[pl.BlockSpec((tm, d), row), pl.BlockSpec((tm, d), row),
                  pl.BlockSpec((2, None, tm, nf), lambda i, j: (0, j, i, 0)),
                  pl.BlockSpec((1, d), lambda i, j: (0, 0)), wi_spec, wo_spec, ANY_SPEC],
        out_specs=[pl.BlockSpec((tm, d), row), pl.BlockSpec((tm, d), row),
                   pl.BlockSpec((None, tm, nf), lambda i, j: (j, i, 0)),
                   pl.BlockSpec((2, None, tm, nf), lambda i, j: (0, j, i, 0)),
                   pl.BlockSpec((1, d), lambda i, j: (0, 0)), pl.BlockSpec((tm, d), row)],
        scratch_shapes=[pltpu.VMEM((tm, d), F32)],
        compiler_params=_params("arbitrary", "arbitrary"),
    )(x, dy, gu, gain.reshape(1, d), w_in5, w_out5, after)


def ffn_grad_w_in(h, dgu, after, *, name):
    t, d = h.shape
    nf = dgu.shape[-1]
    tm = _row_tile(t, TN_TILE)
    return mm_tn(dgu.reshape(8, t, nf), h, nb=8, ka=nf, nbk=d, tm=tm, m=t, after=after,
                 a_spec=pl.BlockSpec((None, tm, nf), lambda s, i: (s, i, 0)),
                 b_spec=pl.BlockSpec((tm, d), lambda s, i: (i, 0)), name=name)


def ffn_grad_w_out(act, dyh, after, *, name):
    _, t, nf = act.shape
    d = dyh.shape[1]
    tm = _row_tile(t, TN_TILE)
    d_w_out = mm_tn(act, dyh, nb=4, ka=nf, nbk=d, tm=tm, m=t, after=after,
                    a_spec=pl.BlockSpec((None, tm, nf), lambda s, i: (s, i, 0)),
                    b_spec=pl.BlockSpec((tm, d), lambda s, i: (i, 0)), name=name)
    return d_w_out.reshape(8, nf // 2, d)


def _lane_group(shape):
    return lax.shift_right_logical(lax.broadcasted_iota(jnp.int32, shape, 1), 6)


def _pool_count(t0, rows):
    t = (t0 + lax.broadcasted_iota(jnp.int32, (rows, MIX_W), 0) + 1).astype(F32)
    return jnp.minimum(t, _by_group(_lane_group((rows, MIX_W)), 2.0, 4.0, 8.0, 16.0))


def _by_group(grp, v0, v1, v2, v3):
    return jnp.where(grp == 0, v0, jnp.where(grp == 1, v1, jnp.where(grp == 2, v2, v3)))


def _sgu_mix(wt_ref, vnc):
    grp = _lane_group((SGU_CHUNK, MIX_W))
    out = jnp.zeros((SGU_CHUNK, MIX_W), F32)
    for hd in range(N_HEADS):
        out = jnp.where(grp == hd, _dot(wt_ref[hd], vnc), out)
    return out


def _pool_fwd(s1, s2, s3, t0, ts, lo):
    h = lo
    s2[h - 24:h + ts] = s1[h - 24:h + ts] + s1[h - 25:h + ts - 1]
    s3[h - 16:h + ts] = s2[h - 16:h + ts] + s2[h - 18:h + ts - 2]
    sum2 = s2[h:h + ts]
    sum4 = s3[h:h + ts]
    s2[h - 8:h + ts] = s3[h - 8:h + ts] + s3[h - 12:h + ts - 4]
    sum8 = s2[h:h + ts]
    sum16 = sum8 + s2[h - 8:h + ts - 8]
    grp = _lane_group((ts, MIX_W))
    return _by_group(grp, sum2, sum4, sum8, sum16) / _pool_count(t0, ts) - s1[h:h + ts]


def _make_shifts(src, sh, rows):
    for b in range(1, 8):
        sh[b, 0:rows] = src[b:b + rows]


def _rows_at(src, sh, start, n):
    a, b = divmod(start, 8)
    return src[8 * a:8 * a + n] if b == 0 else sh[b, 8 * a:8 * a + n]


def mixer_fwd(z, sconv, cconv, vecs, wt, bexp, pbd, x_res, wmo_g, *, name, ts=None):
    t = z.shape[0]
    ts = _row_tile(t, MIX_TILE if ts is None else ts)
    hl = HALO
    w = MIX_W
    nch = ts // SGU_CHUNK

    def body(zc, zp, sconv_ref, cconv_ref, vec_ref, wt_ref, bexp_ref, pbd_ref, xr_ref, wmo_ref,
             y_ref, xo_ref, s1, s2, s3, sh):
        i = pl.program_id(0)
        has_prev = i > 0

        def col(ref, c):
            return ref[:, c * w:(c + 1) * w]

        def prev(c):
            return jnp.where(has_prev, col(zp, c), 0.0)

        s1[0:hl] = prev(1) * prev(2)
        s1[hl:hl + ts] = col(zc, 1) * col(zc, 2)
        cv = sconv_ref[0:1] * s1[hl - 2:hl - 2 + ts]
        for k in range(1, SCONV_K):
            cv = cv + sconv_ref[k:k + 1] * s1[hl - 2 + k:hl - 2 + k + ts]
        y_ref[:, 0:w] = (col(zc, 0) * cv).astype(BF16)

        xhat, _ = _ln_stats(col(zc, 4))
        vn = (xhat * vec_ref[0:1]).astype(BF16)
        for c in range(nch):
            rows = slice(c * SGU_CHUNK, (c + 1) * SGU_CHUNK)
            mixed = _sgu_mix(wt_ref, vn[rows]) + bexp_ref[...]
            y_ref[rows, w:2 * w] = (zc[rows, 3 * w:4 * w] * mixed).astype(BF16)

        s1[0:hl] = prev(5) * _sigmoid(prev(6))
        s1[hl:hl + ts] = col(zc, 5) * _sigmoid(col(zc, 6))
        off = hl - (CCONV_K - 1)
        _make_shifts(s1, sh, hl + ts - 8)
        cv = cconv_ref[0:1] * _rows_at(s1, sh, off, ts)
        for k in range(1, CCONV_K):
            cv = cv + cconv_ref[k:k + 1] * _rows_at(s1, sh, off + k, ts)
        xhat, _ = _ln_stats(cv)
        ln = xhat * vec_ref[1:2] + vec_ref[2:3]
        y_ref[:, 2 * w:3 * w] = (ln * _sigmoid(ln)).astype(BF16)

        s1[0:hl] = prev(7)
        s1[hl:hl + ts] = col(zc, 7)
        pooled = _pool_fwd(s1, s2, s3, i * ts, ts, hl)
        y_ref[:, 3 * w:4 * w] = (_dot(pooled.astype(BF16), pbd_ref[...]) * vec_ref[3:4]).astype(BF16)

        xo_ref[...] = xr_ref[...] + _dot(y_ref[...], _full_weight(wmo_ref, "row"))

    full = lambda shape: pl.BlockSpec(shape, lambda i: (0,) * len(shape))
    row = lambda i: (i, 0)
    return pl.pallas_call(
        body, name=name, grid=(t // ts,),
        out_shape=[jax.ShapeDtypeStruct((t, 4 * w), BF16), jax.ShapeDtypeStruct((t, 4 * w), F32)],
        in_specs=[pl.BlockSpec((ts, 8 * w), row),
                  pl.BlockSpec((hl, 8 * w), lambda i: (jnp.maximum(i * (ts // hl) - 1, 0), 0)),
                  full((8, w)), full((32, w)), full((8, w)), full((N_HEADS, SGU_CHUNK, SGU_CHUNK)),
                  full((SGU_CHUNK, w)), full((w, w)), pl.BlockSpec((ts, 4 * w), row), _wspec(wmo_g)],
        out_specs=[pl.BlockSpec((ts, 4 * w), row), pl.BlockSpec((ts, 4 * w), row)],
        scratch_shapes=[pltpu.VMEM((hl + ts, w), F32)] * 3 + [pltpu.VMEM((8, hl + ts, w), F32)],
        compiler_params=_params("parallel"),
    )(z, z, sconv, cconv, vecs, wt, bexp, pbd, x_res, wmo_g)


def mixer_bwd(z, dx, wmo_g, sconv, cconv, vecs, wt, bexp, pbd, *, name, ts=None):
    t = z.shape[0]
    ts = _row_tile(t, MIX_TILE if ts is None else ts)
    hl = HALO
    w = MIX_W
    nch = ts // SGU_CHUNK
    ni = t // ts
    ext = ts + hl

    def body(zc, zp, zn, dxc, dxn_, wmo_ref, sconv_ref, cconv_ref, vec_ref, wt_ref, bexp_ref, pbd_ref,
             dz_ref, gvec_ref, gcc_ref, gwt_ref, gb_ref, gpbd_ref, s1, s2, s3, sh1, sh3, dyc, dyn):
        i = pl.program_id(0)
        has_prev = i > 0
        has_next = i < ni - 1
        wmo = _full_weight(wmo_ref, "row")
        dyc[...] = _dot_nt(dxc[...].astype(BF16), wmo)
        dyn[...] = _dot_nt(dxn_[...].astype(BF16), wmo)

        @pl.when(i == 0)
        def _():
            gvec_ref[...] = jnp.zeros_like(gvec_ref)
            gcc_ref[...] = jnp.zeros_like(gcc_ref)
            gwt_ref[...] = jnp.zeros_like(gwt_ref)
            gb_ref[...] = jnp.zeros_like(gb_ref)
            gpbd_ref[...] = jnp.zeros_like(gpbd_ref)

        def col(ref, c):
            return ref[:, c * w:(c + 1) * w]

        def prev(c):
            return jnp.where(has_prev, col(zp, c), 0.0)

        def nxt(c):
            return jnp.where(has_next, col(zn, c), 0.0)

        def dnext(c):
            return jnp.where(has_next, col(dyn, c), 0.0)

        def rowsum(v):
            return jnp.sum(v, axis=0, keepdims=True)

        s1[0:hl] = prev(1) * prev(2)
        s1[hl:hl + ts] = col(zc, 1) * col(zc, 2)
        s1[hl + ts:hl + ts + hl] = nxt(1) * nxt(2)
        cv = sconv_ref[0:1] * s1[hl - 2:hl - 2 + ts]
        for k in range(1, SCONV_K):
            cv = cv + sconv_ref[k:k + 1] * s1[hl - 2 + k:hl - 2 + k + ts]
        dya = col(dyc, 0)
        dz_ref[:, 0:w] = (dya * cv).astype(BF16)
        s2[0:ts] = dya * col(zc, 0)
        s2[ts:ext] = dnext(0) * nxt(0)
        dv = sconv_ref[0:1] * s2[2:2 + ts]
        for k in range(1, SCONV_K):
            dv = dv + sconv_ref[k:k + 1] * s2[2 - k:2 - k + ts]
        dz_ref[:, w:2 * w] = (dv * col(zc, 2)).astype(BF16)
        dz_ref[:, 2 * w:3 * w] = (dv * col(zc, 1)).astype(BF16)
        dcv = s2[0:ts]
        for k in range(SCONV_K):
            gvec_ref[k:k + 1] += rowsum(dcv * s1[hl - 2 + k:hl - 2 + k + ts])

        g_sgu = vec_ref[0:1]
        xhat, rstd = _ln_stats(col(zc, 4))
        vn = (xhat * g_sgu).astype(BF16)
        grp = _lane_group((SGU_CHUNK, w))
        lane = lax.broadcasted_iota(jnp.int32, (SGU_CHUNK, SGU_CHUNK), 1)
        tril = lax.broadcasted_iota(jnp.int32, (SGU_CHUNK, SGU_CHUNK), 0) >= lane
        for c in range(nch):
            rows = slice(c * SGU_CHUNK, (c + 1) * SGU_CHUNK)
            vnc = vn[rows]
            mixed = _sgu_mix(wt_ref, vnc) + bexp_ref[...]
            dyb = dyc[rows, w:2 * w]
            dz_ref[rows, 3 * w:4 * w] = (dyb * mixed).astype(BF16)
            dmix = dyb * zc[rows, 3 * w:4 * w]
            dmixb = dmix.astype(BF16)
            dvn = jnp.zeros((SGU_CHUNK, w), F32)
            gb = jnp.zeros((SGU_CHUNK, SGU_CHUNK), F32)
            for hd in range(N_HEADS):
                dvn = jnp.where(grp == hd, _dot_tn(wt_ref[hd], dmixb), dvn)
                dm_h = jnp.where(grp == hd, dmix, 0.0)
                gwt_ref[hd] += jnp.where(tril, _dot_nt(dm_h.astype(BF16), vnc), 0.0)
                gb = gb + jnp.where(lane == hd, jnp.sum(dm_h, axis=1, keepdims=True), 0.0)
            gb_ref[...] += gb
            s3[rows] = dvn
        dvn = s3[0:ts]
        gvec_ref[3:4] += rowsum(dvn * xhat)
        dz_ref[:, 4 * w:5 * w] = _ln_bwd(xhat, rstd, dvn * g_sgu).astype(BF16)

        sig_c = _sigmoid(col(zc, 6))
        s1[0:hl] = prev(5) * _sigmoid(prev(6))
        s1[hl:hl + ts] = col(zc, 5) * sig_c
        s1[hl + ts:hl + ts + hl] = nxt(5) * _sigmoid(nxt(6))
        off = hl - (CCONV_K - 1)
        _make_shifts(s1, sh1, ts + 2 * hl - 8)
        cv = cconv_ref[0:1] * _rows_at(s1, sh1, off, ext)
        for k in range(1, CCONV_K):
            cv = cv + cconv_ref[k:k + 1] * _rows_at(s1, sh1, off + k, ext)
        xhat, rstd = _ln_stats(cv)
        ln = xhat * vec_ref[1:2] + vec_ref[2:3]
        sg = _sigmoid(ln)
        s2[0:ts] = col(dyc, 2)
        s2[ts:ext] = dnext(2)
        dln = s2[0:ext] * (sg * (1.0 + ln * (1.0 - sg)))
        gvec_ref[4:5] += rowsum(dln[0:ts] * xhat[0:ts])
        gvec_ref[5:6] += rowsum(dln[0:ts])
        s3[0:ext] = _ln_bwd(xhat, rstd, dln * vec_ref[1:2])
        _make_shifts(s3, sh3, ext - 8)
        dyg = cconv_ref[0:1] * _rows_at(s3, sh3, CCONV_K - 1, ts)
        for k in range(1, CCONV_K):
            dyg = dyg + cconv_ref[k:k + 1] * _rows_at(s3, sh3, CCONV_K - 1 - k, ts)
        dz_ref[:, 5 * w:6 * w] = (dyg * sig_c).astype(BF16)
        dz_ref[:, 6 * w:7 * w] = (dyg * col(zc, 5) * sig_c * (1.0 - sig_c)).astype(BF16)
        dcv = s3[0:ts]
        for k in range(CCONV_K):
            gcc_ref[k:k + 1] += rowsum(dcv * _rows_at(s1, sh1, off + k, ts))

        scale = vec_ref[3:4]
        s1[0:hl] = prev(7)
        s1[hl:hl + ts] = col(zc, 7)
        pooled = _pool_fwd(s1, s2, s3, i * ts, ts, hl).astype(BF16)
        q0 = _dot(pooled, pbd_ref[...])
        dyd = col(dyc, 3)
        gvec_ref[6:7] += rowsum(dyd * q0)
        dq = (dyd * scale).astype(BF16)
        gpbd_ref[...] += _dot_tn(pooled, dq)
        s1[0:ts] = _dot_nt(dq, pbd_ref[...])
        s1[ts:ext] = _dot_nt((dnext(3) * scale).astype(BF16), pbd_ref[...])
        dpool = s1[0:ts]
        s2[0:ext] = s1[0:ext] / _pool_count(i * ts, ext)
        s3[0:ts + 24] = s2[0:ts + 24] + s2[1:ts + 25]
        f2 = s3[0:ts]
        s2[0:ts + 16] = s3[0:ts + 16] + s3[2:ts + 18]
        f4 = s2[0:ts]
        s3[0:ts + 8] = s2[0:ts + 8] + s2[4:ts + 12]
        f8 = s3[0:ts]
        f16 = f8 + s3[8:ts + 8]
        dz_ref[:, 7 * w:8 * w] = (_by_group(_lane_group((ts, w)), f2, f4, f8, f16) - dpool).astype(BF16)

    full = lambda shape: pl.BlockSpec(shape, lambda i: (0,) * len(shape))
    r = ts // hl
    prev_map = lambda i: (jnp.maximum(i * r - 1, 0), 0)
    next_map = lambda i: (jnp.minimum((i + 1) * r, t // hl - 1), 0)
    return pl.pallas_call(
        body, name=name, grid=(ni,),
        out_shape=[jax.ShapeDtypeStruct((t, 8 * w), BF16), jax.ShapeDtypeStruct((8, w), F32),
                   jax.ShapeDtypeStruct((32, w), F32),
                   jax.ShapeDtypeStruct((N_HEADS, SGU_CHUNK, SGU_CHUNK), F32),
                   jax.ShapeDtypeStruct((SGU_CHUNK, SGU_CHUNK), F32), jax.ShapeDtypeStruct((w, w), F32)],
        in_specs=[pl.BlockSpec((ts, 8 * w), lambda i: (i, 0)),
                  pl.BlockSpec((hl, 8 * w), prev_map), pl.BlockSpec((hl, 8 * w), next_map),
                  pl.BlockSpec((ts, 4 * w), lambda i: (i, 0)), pl.BlockSpec((hl, 4 * w), next_map),
                  _wspec(wmo_g),
                  full((8, w)), full((32, w)), full((8, w)), full((N_HEADS, SGU_CHUNK, SGU_CHUNK)),
                  full((SGU_CHUNK, w)), full((w, w))],
        out_specs=[pl.BlockSpec((ts, 8 * w), lambda i: (i, 0)), full((8, w)), full((32, w)),
                   full((N_HEADS, SGU_CHUNK, SGU_CHUNK)), full((SGU_CHUNK, SGU_CHUNK)), full((w, w))],
        scratch_shapes=[pltpu.VMEM((ts + 2 * hl, w), F32)] * 3 + [pltpu.VMEM((8, ts + 2 * hl, w), F32)] * 2
        + [pltpu.VMEM((ts, 4 * w), F32), pltpu.VMEM((hl, 4 * w), F32)],
        compiler_params=_params("arbitrary"),
    )(z, z, z, dx, dx, wmo_g, sconv, cconv, vecs, wt, bexp, pbd)


def _attn_head(q, kv_ref, hd, d):
    hw = d // N_HEADS
    qh = q[:, hd * hw:(hd + 1) * hw]
    kh = kv_ref[:, hd * hw:(hd + 1) * hw].astype(BF16)
    vh = kv_ref[:, d + hd * hw:d + (hd + 1) * hw].astype(BF16)
    s = _dot_nt(qh, kh) * (1.0 / (hw ** 0.5))
    e = jnp.exp(s - jnp.max(s, axis=-1, keepdims=True))
    p = e / jnp.sum(e, axis=-1, keepdims=True)
    return qh, kh, vh, p


def xattn_fwd(x, gain, kv, wq_g, wo_g, *, name, tm=None):
    t, d = x.shape
    nm = kv.shape[0]
    tm = _row_tile(t, tm)
    hw = d // N_HEADS

    def body(x_ref, g_ref, kv_ref, wq_ref, wo_ref, o_ref):
        xv = x_ref[...]
        xhat, _ = _rms_fwd(xv, None)
        h = (xhat * g_ref[...]).astype(BF16)
        q = _dot(h, _full_weight(wq_ref, "row")).astype(BF16)
        wo = _full_weight(wo_ref, "row")
        out = xv
        for hd in range(N_HEADS):
            _, _, vh, p = _attn_head(q, kv_ref, hd, d)
            oh = _dot(p.astype(BF16), vh).astype(BF16)
            out = out + _dot(oh, wo[hd * hw:(hd + 1) * hw])
        o_ref[...] = out

    row = lambda i: (i, 0)
    return pl.pallas_call(
        body, name=name, grid=(t // tm,),
        out_shape=jax.ShapeDtypeStruct((t, d), F32),
        in_specs=[pl.BlockSpec((tm, d), row), pl.BlockSpec((1, d), lambda i: (0, 0)),
                  pl.BlockSpec((nm, 2 * d), lambda i: (0, 0)), _wspec(wq_g), _wspec(wo_g)],
        out_specs=pl.BlockSpec((tm, d), row),
        compiler_params=_params("parallel"),
    )(x, gain.reshape(1, d), kv, wq_g, wo_g)


def xattn_bwd_rows(x, dxn, gain, kv, wq_g, wo_g, after, *, name, tm=None):
    t, d = x.shape
    nm = kv.shape[0]
    tm = _row_tile(t, tm)
    hw = d // N_HEADS

    def body(x_ref, dxn_ref, g_ref, kv_ref, wq_ref, wo_ref, after_ref,
             dx_ref, h_ref, dq_ref, o_ref, dkv_ref, dg_ref):
        i = pl.program_id(0)

        @pl.when(i == 0)
        def _():
            dkv_ref[...] = jnp.zeros_like(dkv_ref)
            dg_ref[...] = jnp.zeros_like(dg_ref)
        g = g_ref[...]
        xhat, r = _rms_fwd(x_ref[...], None)
        h = (xhat * g).astype(BF16)
        h_ref[...] = h
        wq = _full_weight(wq_ref, "row")
        q = _dot(h, wq).astype(BF16)
        dxn = dxn_ref[...]
        do = _dot_nt(dxn.astype(BF16), _full_weight(wo_ref, "row")).astype(BF16)
        for hd in range(N_HEADS):
            cols = slice(hd * hw, (hd + 1) * hw)
            qh, kh, vh, p = _attn_head(q, kv_ref, hd, d)
            pb = p.astype(BF16)
            o_ref[:, cols] = _dot(pb, vh).astype(BF16)
            doh = do[:, cols]
            dkv_ref[:, d + hd * hw:d + (hd + 1) * hw] += _dot_tn(pb, doh)
            dp = _dot_nt(doh, vh)
            ds = (p * (dp - jnp.sum(dp * p, axis=-1, keepdims=True)) * (1.0 / (hw ** 0.5))).astype(BF16)
            dq_ref[:, cols] = _dot(ds, kh).astype(BF16)
            dkv_ref[:, cols] += _dot_tn(ds, qh)
        dh = _dot_nt(dq_ref[...], wq)
        dx, dg = _rms_bwd(xhat, r, g, dh)
        dx_ref[...] = dxn + dx
        dg_ref[...] += dg

    row = lambda i: (i, 0)
    fix = lambda i: (0, 0)
    return pl.pallas_call(
        body, name=name, grid=(t // tm,),
        out_shape=[jax.ShapeDtypeStruct((t, d), F32), jax.ShapeDtypeStruct((t, d), BF16),
                   jax.ShapeDtypeStruct((t, d), BF16), jax.ShapeDtypeStruct((t, d), BF16),
                   jax.ShapeDtypeStruct((nm, 2 * d), F32), jax.ShapeDtypeStruct((1, d), F32)],
        in_specs=[pl.BlockSpec((tm, d), row), pl.BlockSpec((tm, d), row), pl.BlockSpec((1, d), fix),
                  pl.BlockSpec((nm, 2 * d), fix), _wspec(wq_g), _wspec(wo_g), ANY_SPEC],
        out_specs=[pl.BlockSpec((tm, d), row)] * 4 + [pl.BlockSpec((nm, 2 * d), fix),
                                                      pl.BlockSpec((1, d), fix)],
        compiler_params=_params("arbitrary"),
    )(x, dxn, gain.reshape(1, d), kv, wq_g, wo_g, after)


def loss_head(x, target, gain, *, name, tm=None):
    t, d = x.shape
    tm = _row_tile(t, tm)

    def body(x_ref, t_ref, g_ref, dx_ref, dg_ref, loss_ref):
        @pl.when(pl.program_id(0) == 0)
        def _():
            dg_ref[...] = jnp.zeros_like(dg_ref)
            loss_ref[...] = jnp.zeros_like(loss_ref)
        g = g_ref[...]
        xhat, r = _rms_fwd(x_ref[...], None)
        err = xhat * g - t_ref[...]
        loss_ref[...] += 0.5 * jnp.sum(jnp.sum(err * err, axis=-1, keepdims=True) / d,
                                       axis=0, keepdims=True)
        dx, dg = _rms_bwd(xhat, r, g, err / d)
        dx_ref[...] = dx
        dg_ref[...] += dg

    row = lambda i: (i, 0)
    fix = lambda i: (0, 0)
    return pl.pallas_call(
        body, name=name, grid=(t // tm,),
        out_shape=[jax.ShapeDtypeStruct((t, d), F32), jax.ShapeDtypeStruct((1, d), F32),
                   jax.ShapeDtypeStruct((1, 1), F32)],
        in_specs=[pl.BlockSpec((tm, d), row), pl.BlockSpec((tm, d), row), pl.BlockSpec((1, d), fix)],
        out_specs=[pl.BlockSpec((tm, d), row), pl.BlockSpec((1, d), fix), pl.BlockSpec((1, 1), fix)],
        compiler_params=_params("arbitrary"),
    )(x, target, gain.reshape(1, d))


def _adamw_math(w, g, m, v):
    m = ADAM_B1 * m + (1.0 - ADAM_B1) * g
    v = ADAM_B2 * v + (1.0 - ADAM_B2) * (g * g)
    m_hat = m / (1.0 - ADAM_B1 ** ADAM_STEP)
    v_hat = v / (1.0 - ADAM_B2 ** ADAM_STEP)
    delta = -ADAM_LR * (m_hat / (jnp.sqrt(v_hat) + ADAM_EPS) + ADAM_WD * w)
    return delta, m, v


def adamw_sharded(own, lands, w, m, v, me_arr, *, name):
    nl, r, c = w.shape
    assert nl == len(own) == len(lands) == 2
    tr = next(cand for cand in (256, 176, 128, r) if r % cand == 0)
    nr = r // tr

    def body(me_ref, o0, o1, l0, l1, w_ref, m_ref, v_ref, g_out, d_out, m_out, v_out):
        def total(o_ref, l_ref):
            acc = o_ref[...].astype(F32)
            for p in range(N_DEV - 1):
                acc = acc + l_ref[p].astype(F32)
            return acc
        g = jnp.where(pl.program_id(0) == 0, total(o0, l0), total(o1, l1))
        delta, mn, vn = _adamw_math(w_ref[...], g, m_ref[...], v_ref[...])
        g_out[...] = g
        d_out[...] = delta
        m_out[...] = mn
        v_out[...] = vn

    row0 = lambda l, i: jnp.where(l == 0, i, nr - 1)
    row1 = lambda l, i: jnp.where(l == 1, i, 0)
    blk = pl.BlockSpec((None, tr, c), lambda l, i, me: (l, i, 0))
    grid_spec = pltpu.PrefetchScalarGridSpec(
        num_scalar_prefetch=1, grid=(nl, nr),
        in_specs=[pl.BlockSpec((None, tr, c), lambda l, i, me: (me[0], row0(l, i), 0)),
                  pl.BlockSpec((None, tr, c), lambda l, i, me: (me[0], row1(l, i), 0)),
                  pl.BlockSpec((N_DEV - 1, tr, c), lambda l, i, me: (0, row0(l, i), 0)),
                  pl.BlockSpec((N_DEV - 1, tr, c), lambda l, i, me: (0, row1(l, i), 0)),
                  blk, blk, blk],
        out_specs=[blk] * 4)
    return pl.pallas_call(
        body, name=name, grid_spec=grid_spec,
        out_shape=[jax.ShapeDtypeStruct((nl, r, c), F32)] * 4,
        compiler_params=_params("arbitrary", "arbitrary"),
    )(me_arr, own[0], own[1], lands[0], lands[1], w, m, v)


def adamw_many(gs, ws, ms, vs, *, name):
    n = len(ws)
    shapes = [w.shape for w in ws]
    as2d = lambda a: a.reshape(1, -1) if a.ndim == 1 else a

    def body(*refs):
        g_r, w_r, m_r, v_r = refs[:n], refs[n:2 * n], refs[2 * n:3 * n], refs[3 * n:4 * n]
        outs = refs[4 * n:]
        for i in range(n):
            delta, mn, vn = _adamw_math(w_r[i][...], g_r[i][...], m_r[i][...], v_r[i][...])
            outs[3 * i][...] = delta
            outs[3 * i + 1][...] = mn
            outs[3 * i + 2][...] = vn

    operands = [as2d(a) for group in (gs, ws, ms, vs) for a in group]
    out_shape = [jax.ShapeDtypeStruct(as2d(w).shape, F32) for w in ws for _ in range(3)]
    outs = pl.pallas_call(
        body, name=name, out_shape=out_shape,
        in_specs=[VMEM_SPEC] * (4 * n), out_specs=[VMEM_SPEC] * (3 * n),
        compiler_params=pltpu.CompilerParams(vmem_limit_bytes=VMEM_LIMIT),
    )(*operands)
    return [tuple(outs[3 * i + k].reshape(shapes[i]) for k in range(3)) for i in range(n)]


def cast_into_slot(a, layer, me_arr, *, name, dtype=None, after=None):
    dtype = BF16 if dtype is None else dtype
    _, r, c = a.shape
    tr = next(cand for cand in (256, 176, 128, r) if r % cand == 0)
    extra = [] if after is None else [after]

    def body(me_ref, a_ref, *rest):
        rest[-1][...] = a_ref[...].astype(dtype)

    grid_spec = pltpu.PrefetchScalarGridSpec(
        num_scalar_prefetch=1, grid=(r // tr,),
        in_specs=[pl.BlockSpec((None, tr, c), lambda i, me: (layer, i, 0))] + [ANY_SPEC] * len(extra),
        out_specs=pl.BlockSpec((None, tr, c), lambda i, me: (me[0], i, 0)))
    return pl.pallas_call(
        body, name=name, grid_spec=grid_spec,
        out_shape=jax.ShapeDtypeStruct((N_DEV, r, c), dtype),
        compiler_params=_params("parallel"),
    )(me_arr, a, *extra)


def _pack(arrs, rows):
    flat = jnp.concatenate([a.reshape(-1).astype(F32) for a in arrs])
    pad = rows * 128 - flat.shape[0]
    assert pad >= 0
    if pad:
        flat = jnp.concatenate([flat, jnp.zeros((pad,), F32)])
    return flat.reshape(rows, 128)


def _unpack(packed, shapes):
    flat = packed.reshape(-1)
    out, pos = [], 0
    for s in shapes:
        n = 1
        for dim in s:
            n *= dim
        out.append(flat[pos:pos + n].reshape(s))
        pos += n
    return out


def _rows_for(shapes):
    n = 0
    for s in shapes:
        k = 1
        for dim in s:
            k *= dim
        n += k
    return -(-n // 1024) * 8


GATHER_GROUPS = (("ffn1", ("ffn1_w_in", "ffn1_w_out")),
                 ("mid", ("mix_w_in", "mix_w_out", "xattn_wkv", "xattn_wq", "xattn_wo")),
                 ("ffn2", ("ffn2_w_in", "ffn2_w_out")))
SMALL_REPL = ["norm_ffn1", "norm_mix", "sgu_norm_g", "sgu_w", "sgu_b", "cconv_ln_g", "cconv_ln_b",
              "pool_w", "pool_scale", "norm_xattn", "norm_mem", "norm_ffn2", "norm_final"]
SMALL_SHARD = ["sconv_w", "cconv_w"]
TRANSPOSED = ("ffn1_w_in", "ffn2_w_in")
WEIGHTS = ["norm_ffn1", "ffn1_w_in", "ffn1_w_out", "norm_mix", "mix_w_in", "sconv_w", "sgu_norm_g",
           "sgu_w", "sgu_b", "cconv_w", "cconv_ln_g", "cconv_ln_b", "pool_w", "pool_scale", "mix_w_out",
           "norm_xattn", "norm_mem", "xattn_wq", "xattn_wkv", "xattn_wo", "norm_ffn2", "ffn2_w_in",
           "ffn2_w_out", "norm_final"]


def kernel(x, mem, norm_ffn1, ffn1_w_in, ffn1_w_out, norm_mix, mix_w_in, sconv_w, sgu_norm_g, sgu_w, sgu_b, cconv_w, cconv_ln_g, cconv_ln_b, pool_w, pool_scale, mix_w_out, norm_xattn, norm_mem, xattn_wq, xattn_wkv, xattn_wo, norm_ffn2, ffn2_w_in, ffn2_w_out, norm_final, loss_target, m_norm_ffn1, m_ffn1_w_in, m_ffn1_w_out, m_norm_mix, m_mix_w_in, m_sconv_w, m_sgu_norm_g, m_sgu_w, m_sgu_b, m_cconv_w, m_cconv_ln_g, m_cconv_ln_b, m_pool_w, m_pool_scale, m_mix_w_out, m_norm_xattn, m_norm_mem, m_xattn_wq, m_xattn_wkv, m_xattn_wo, m_norm_ffn2, m_ffn2_w_in, m_ffn2_w_out, m_norm_final, v_norm_ffn1, v_ffn1_w_in, v_ffn1_w_out, v_norm_mix, v_mix_w_in, v_sconv_w, v_sgu_norm_g, v_sgu_w, v_sgu_b, v_cconv_w, v_cconv_ln_g, v_cconv_ln_b, v_pool_w, v_pool_scale, v_mix_w_out, v_norm_xattn, v_norm_mem, v_xattn_wq, v_xattn_wkv, v_xattn_wo, v_norm_ffn2, v_ffn2_w_in, v_ffn2_w_out, v_norm_final):
    args = dict(locals())
    wts = {n: args[n] for n in WEIGHTS}
    mom = {n: args["m_" + n] for n in WEIGHTS}
    var = {n: args["v_" + n] for n in WEIGHTS}
    for n in TRANSPOSED:
        wts[n], mom[n], var[n] = (jnp.swapaxes(a, 1, 2) for a in (wts[n], mom[n], var[n]))
    x0 = x[0]
    mem0 = mem[0]
    target = loss_target[0]
    t, d = x0.shape
    nl = norm_ffn1.shape[0]
    w = MIX_W
    me = _my_index()

    me_arr = jnp.reshape(me, (1,)).astype(jnp.int32)

    small_g = all_gather([sconv_w, cconv_w], name="gather_conv_taps")
    sconv_full = jnp.transpose(small_g[0], (1, 2, 0, 3)).reshape(nl, SCONV_K, w)
    cconv_full = jnp.transpose(small_g[1], (1, 2, 0, 3)).reshape(nl, CCONV_K, w)
    pending = {}
    token = small_g[1]
    masks = GATHER_MASKS
    keys = [(gname, l, members) for l in range(nl) for gname, members in GATHER_GROUPS]
    first = [[cast_into_slot(wts[n], keys[0][1], me_arr, name=f"cast_{n}{keys[0][1]}") for n in keys[0][2]]]
    started, token = gather_start_groups(first, token, name="gather_start_first", masks=masks)
    casts = [[cast_into_slot(wts[n], l, me_arr, name=f"cast_{n}{l}", after=token) for n in members]
             for gname, l, members in keys[1:]]
    rest, token = gather_start_groups(casts, token, name="gather_start_rest", masks=masks)
    for (gname, l, members), (send, recv, gs) in zip(keys, started + rest):
        pending[gname, l] = (members, gs, send, recv, masks)
    wg = [dict() for _ in range(nl)]

    handing_over = {}

    def arrive_early(gname, l, after):
        members, gs, send, recv, masks = pending.pop((gname, l))
        gs = gather_wait(gs, send, recv, after, name=f"gather_wait_{gname}{l}", masks=masks)
        fsend, frecv, gs, _ = forward_start(gs, after, name=f"gather_forward_start_{gname}{l}")
        handing_over[gname, l] = (members, gs, fsend, frecv)

    def arrive(gname, l, after):
        if (gname, l) in handing_over:
            members, gs, fsend, frecv = handing_over.pop((gname, l))
            gs = forward_wait(gs, fsend, frecv, after, name=f"gather_forward_wait_{gname}{l}")
        else:
            members, gs, send, recv, masks = pending.pop((gname, l))
            gs = gather_wait(gs, send, recv, after, name=f"gather_wait_{gname}{l}", masks=masks)
            gs = sibling_forward(gs, name=f"gather_forward_{gname}{l}")
        wg[l].update(zip(members, gs))
    sconv_pad = jnp.pad(sconv_full, ((0, 0), (0, 8 - SCONV_K), (0, 0)))
    cconv_pad = jnp.pad(cconv_full, ((0, 0), (0, 32 - CCONV_K), (0, 0)))
    zeros_w = jnp.zeros((nl, w), F32)
    vecs = jnp.stack([sgu_norm_g, cconv_ln_g, cconv_ln_b, pool_scale] + [zeros_w] * 4, axis=1)
    wt = jnp.tril(sgu_w).astype(BF16)
    bexp = jnp.repeat(jnp.swapaxes(sgu_b, 1, 2), w // N_HEADS, axis=2)
    eye = jnp.eye(4, dtype=F32)
    pbd = jnp.einsum("lgcd,gh->lgchd", pool_w, eye).reshape(nl, w, w).astype(BF16)

    def mixer_args(l):
        return sconv_pad[l], cconv_pad[l], vecs[l], wt[l], bexp[l], pbd[l]

    saved = []
    xc = x0
    after = token
    for l in range(nl):
        s = {"x_ffn1": xc}
        arrive("ffn1", l, after)
        xc, s["gu_ffn1"] = ffn_fwd(xc, norm_ffn1[l], wg[l]["ffn1_w_in"], wg[l]["ffn1_w_out"],
                                   name=f"ffn1_fwd{l}", tm=FFN_FWD_TILE)
        s["x_mix"] = xc
        arrive("mid", l, xc)
        z = mm_rows(xc, wg[l]["mix_w_in"], "col", gain=norm_mix[l], name=f"mix_in{l}")
        y, xc = mixer_fwd(z, *mixer_args(l), xc, wg[l]["mix_w_out"], name=f"mixer_fwd{l}")
        s["z"], s["y"] = z, y
        s["x_att"] = xc
        kv = mm_rows(mem0, wg[l]["xattn_wkv"], "col", gain=norm_mem[l], name=f"kv{l}")
        s["kv"] = kv
        if l > 0:
            arrive_early("ffn2", l, kv)
        xc = xattn_fwd(xc, norm_xattn[l], kv, wg[l]["xattn_wq"], wg[l]["xattn_wo"], name=f"xattn_fwd{l}")
        s["x_ffn2"] = xc
        arrive("ffn2", l, xc)
        xc, s["gu_ffn2"] = ffn_fwd(xc, norm_ffn2[l], wg[l]["ffn2_w_in"], wg[l]["ffn2_w_out"],
                                   name=f"ffn2_fwd{l}", tm=FFN_FWD_TILE)
        after = xc
        saved.append(s)

    dx, g_norm_final, loss_local = loss_head(xc, target, norm_final, name="loss_head")

    tm = _row_tile(t, TN_TILE)
    small ={n: [None] * nl for n in SMALL_REPL + SMALL_SHARD if n != "norm_final"}
    scattered = {}
    tie = [token]

    def send_grads(gname, l, grads):
        members = list(grads)
        send, recv, gs, lands, tie[0] = scatter_start(
            [grads[n] for n in members], tie[0], name=f"scatter_start_{gname}{l}")
        scattered[gname, l] = (members, gs, lands, send, recv)

    names = SMALL_REPL + SMALL_SHARD + ["loss"]
    small_pending = []

    def start_small():
        small_full = {n: jnp.stack(v) for n, v in small.items()}
        small_full["norm_final"] = g_norm_final[0]
        small_full["loss"] = loss_local[0]
        shapes = [small_full[n].shape for n in names]
        packed = _pack([small_full[n] for n in names], _rows_for(shapes))
        slot = cast_into_slot(packed[None], 0, me_arr, name="small_into_slot", dtype=F32)
        send, recv, gs, tie[0] = gather_start([slot], tie[0], name="small_gather_start", masks=ALL_MASKS)
        small_pending.append((gs, send, recv, shapes))

    def ffn_backward(which, l, x_in, dy, gu, gain):
        w_in, w_out = wg[l][which + "_w_in"], wg[l][which + "_w_out"]
        dx_, h_, act, dgu, dgn, dyh = ffn_bwd_rows(x_in, dy, gu, gain, w_in, w_out, tie[0],
                                                   name=f"{which}_bwd{l}_rows")
        small["norm_" + which][l] = dgn[0]
        last = which == "ffn1" and l == 0
        if last:
            start_small()
        g_in = ffn_grad_w_in(h_, dgu, tie[0], name=f"{which}_bwd{l}_dwin")
        if last:
            send_grads(which + "_in", l, {which + "_w_in": g_in})
        g_out = ffn_grad_w_out(act, dyh, tie[0], name=f"{which}_bwd{l}_dwout")
        if last:
            send_grads(which + "_out", l, {which + "_w_out": g_out})
        else:
            send_grads(which, l, {which + "_w_in": g_in, which + "_w_out": g_out})
        return dx_

    for l in reversed(range(nl)):
        s = saved[l]
        wl = wg[l]
        dx = ffn_backward("ffn2", l, s["x_ffn2"], dx, s["gu_ffn2"], norm_ffn2[l])

        bg = {}
        dxn = dx
        dx, h, dq, o, dkv, dgn = xattn_bwd_rows(
            s["x_att"], dxn, norm_xattn[l], s["kv"], wl["xattn_wq"], wl["xattn_wo"], tie[0],
            name=f"xattn_bwd{l}")
        small["norm_xattn"][l] = dgn[0]
        row_spec = pl.BlockSpec((tm, d), lambda s_, i: (i, 0))
        bg["xattn_wq"] = mm_tn(h, dq, nb=1, ka=d, nbk=d, tm=tm, m=t, a_spec=row_spec, b_spec=row_spec,
                               name=f"dwq{l}").reshape(N_DEV, d // N_DEV, d)
        bg["xattn_wo"] = mm_tn(o, dxn, nb=1, ka=d, nbk=d, tm=tm, m=t, a_spec=row_spec, b_spec=row_spec,
                               name=f"dwo{l}").reshape(N_DEV, d // N_DEV, d)
        _, mhat, dgn = mm_nt(dkv, wl["xattn_wkv"], "col", x=mem0, gain=norm_mem[l], name=f"dmem{l}")
        small["norm_mem"][l] = dgn[0]
        nm = mem0.shape[0]
        bg["xattn_wkv"] = mm_tn(mhat, dkv, nb=N_DEV, ka=d, nbk=2 * d // N_DEV, tm=nm, m=nm,
                                a_spec=pl.BlockSpec((nm, d), lambda s_, i: (0, 0)),
                                b_spec=pl.BlockSpec((nm, 2 * d // N_DEV), lambda s_, i: (0, s_)),
                                name=f"dwkv{l}")
        send_grads("xattn", l, bg)

        bg = {}
        dxn = dx
        bg["mix_w_out"] = mm_tn(s["y"], dxn, nb=1, ka=d, nbk=d, tm=tm, m=t, a_spec=row_spec,
                                b_spec=row_spec, name=f"dwmo{l}").reshape(N_DEV, d // N_DEV, d)
        dz, gvec, gcc, gwt, gb, gpbd = mixer_bwd(s["z"], dxn, wl["mix_w_out"], *mixer_args(l),
                                                 name=f"mixer_bwd{l}")
        small["sconv_w"][l] = gvec[0:SCONV_K]
        small["sgu_norm_g"][l] = gvec[3]
        small["cconv_ln_g"][l] = gvec[4]
        small["cconv_ln_b"][l] = gvec[5]
        small["pool_scale"][l] = gvec[6]
        small["cconv_w"][l] = gcc[0:CCONV_K]
        small["sgu_w"][l] = gwt
        small["sgu_b"][l] = jnp.transpose(gb[:, 0:N_HEADS])
        gw = w // 4
        small["pool_w"][l] = jnp.stack([gpbd[g * gw:(g + 1) * gw, g * gw:(g + 1) * gw] for g in range(4)])
        dx, h, dgn = mm_nt(dz, wl["mix_w_in"], "col", x=s["x_mix"], gain=norm_mix[l], dx_in=dxn,
                           after=tie[0], name=f"dh_mix{l}")
        small["norm_mix"][l] = dgn[0]
        th = _row_tile(t, TN_TILE // 2)
        bg["mix_w_in"] = mm_tn(h, dz, nb=1, ka=d, nbk=N_DEV * w, tm=th, m=t, col_slots=N_DEV,
                               a_spec=pl.BlockSpec((th, d), lambda s_, i: (i, 0)),
                               b_spec=pl.BlockSpec((th, N_DEV * w), lambda s_, i: (i, 0)), name=f"dwmi{l}")
        send_grads("mix", l, bg)

        dx = ffn_backward("ffn1", l, s["x_ffn1"], dx, s["gu_ffn1"], norm_ffn1[l])

    out = {}
    own, land = {}, {}

    def collect(keys, after):
        for gname, l in keys:
            members, gs, lands, send, recv = scattered.pop((gname, l))
            gs, lands = scatter_wait(gs, lands, send, recv, after, name=f"scatter_wait_{gname}{l}")
            for n, g_, l_ in zip(members, gs, lands):
                own.setdefault(n, {})[l] = g_
                land.setdefault(n, {})[l] = l_

    def update(ns, after):
        for n in ns:
            out[n] = adamw_sharded([own[n][l] for l in range(nl)], [land[n][l] for l in range(nl)],
                                   wts[n], mom[n], var[n], me_arr, name="adamw_" + n)
            after = out[n][1]
        return after

    after = tie[0]
    for gname in ("ffn2", "xattn", "mix"):
        collect([(gname, l) for l in reversed(range(nl))], after)
        after = update([n for n in own if n not in out], after)
    (gs, send, recv, shapes), = small_pending
    gs = gather_wait(gs, send, recv, after, name="small_gather_wait", masks=ALL_MASKS)
    summed = sum_slots(gs[0], name="small_sum")
    gsm = dict(zip(names, _unpack(summed, shapes)))
    loss = gsm["loss"][0]
    cs = w // N_DEV
    for n in SMALL_SHARD:
        gsm[n] = lax.dynamic_slice_in_dim(gsm[n], me * cs, cs, axis=2)
    small_names = SMALL_REPL + SMALL_SHARD
    upd = adamw_many([gsm[n] for n in small_names], [wts[n] for n in small_names],
                     [mom[n] for n in small_names], [var[n] for n in small_names], name="adamw_small")
    for n, (a, b, c) in zip(small_names, upd):
        out[n] = (gsm[n], a, b, c)
    after = upd[0][0]
    collect([("ffn1", l) for l in reversed(range(1, nl))] + [("ffn1_in", 0)], after)
    after = update(["ffn1_w_in"], after)
    collect([("ffn1_out", 0)], after)
    update(["ffn1_w_out"], after)
    for n in TRANSPOSED:
        out[n] = tuple(jnp.swapaxes(a, 1, 2) for a in out[n])

    grad_x = dx.reshape(1, t, d)
    return (loss, grad_x, *[out[n][0] for n in WEIGHTS], *[out[n][1] for n in WEIGHTS],
            *[out[n][2] for n in WEIGHTS], *[out[n][3] for n in WEIGHTS])
```

```python
import jax
import jax.numpy as jnp
from jax import lax
from jax.experimental import pallas as pl
from jax.experimental.pallas import tpu as pltpu

F32 = jnp.float32
BF16 = jnp.bfloat16
MESH = pl.DeviceIdType.MESH
N_DEV = 8
EPS = 1e-6
HALO = 32
SGU_CHUNK = 128
CCONV_K = 31
SCONV_K = 3
MIX_W = 256
N_HEADS = 4
VMEM_LIMIT = 56 * 1024 * 1024
ROW_TILE = 512
TN_TILE = 2048
FFN_FWD_TILE = 1024
FFN_BWD_SPLIT = 2
FFN_FWD_SPLIT = 2
MIX_TILE = 512

ADAM_LR = 0.001
ADAM_B1 = 0.9
ADAM_B2 = 0.999
ADAM_EPS = 1e-08
ADAM_WD = 0.01
ADAM_STEP = 10

HBM_SPEC = pl.BlockSpec(memory_space=pltpu.HBM)
VMEM_SPEC = pl.BlockSpec(memory_space=pltpu.VMEM)


def _params(*sem):
    return pltpu.CompilerParams(dimension_semantics=tuple(sem), vmem_limit_bytes=VMEM_LIMIT)


def _row_tile(m, pref=None):
    t = min(m, ROW_TILE if pref is None else pref)
    assert m % t == 0, (m, t)
    return t


def _my_index():
    return lax.axis_index("x") * 4 + lax.axis_index("y") * 2 + lax.axis_index("c")


def _peer(mask):
    x, y, c = lax.axis_index("x"), lax.axis_index("y"), lax.axis_index("c")
    px = 1 - x if mask & 4 else x
    py = 1 - y if mask & 2 else y
    pc = 1 - c if mask & 1 else c
    return (px, py, pc), px * 4 + py * 2 + pc


def all_gather(arrs, name):
    n = len(arrs)

    def body(*refs):
        ins, outs = refs[:n], refs[n:2 * n]
        send_sems, recv_sems, loc_sems = refs[2 * n:]
        me = _my_index()
        local = []
        for i in range(n):
            cp = pltpu.make_async_copy(ins[i], outs[i].at[me], loc_sems.at[i])
            cp.start()
            local.append(cp)
        sends = []
        for i in range(n):
            for m in range(1, N_DEV):
                peer, _ = _peer(m)
                cp = pltpu.make_async_remote_copy(
                    src_ref=ins[i], dst_ref=outs[i].at[me],
                    send_sem=send_sems.at[i, m - 1], recv_sem=recv_sems.at[i, m - 1],
                    device_id=peer, device_id_type=MESH)
                cp.start()
                sends.append(cp)
        for i in range(n):
            for m in range(1, N_DEV):
                peer, pidx = _peer(m)
                pltpu.make_async_remote_copy(
                    src_ref=ins[i], dst_ref=outs[i].at[pidx],
                    send_sem=send_sems.at[i, m - 1], recv_sem=recv_sems.at[i, m - 1],
                    device_id=peer, device_id_type=MESH).wait_recv()
        for cp in sends:
            cp.wait_send()
        for cp in local:
            cp.wait()

    return pl.pallas_call(
        body, name=name,
        out_shape=[jax.ShapeDtypeStruct((N_DEV,) + a.shape, a.dtype) for a in arrs],
        in_specs=[HBM_SPEC] * n, out_specs=[HBM_SPEC] * n,
        scratch_shapes=[pltpu.SemaphoreType.DMA((n, N_DEV - 1)),
                        pltpu.SemaphoreType.DMA((n, N_DEV - 1)),
                        pltpu.SemaphoreType.DMA((n,))],
    )(*arrs)


SEM_SPEC = pl.BlockSpec(memory_space=pltpu.SEMAPHORE)
ANY_SPEC = pl.BlockSpec(memory_space=pl.ANY)
SIDE_EFFECT = pltpu.SideEffectType.DATAFLOW_SIDE_EFFECTING


def _hbm(a):
    return pltpu.with_memory_space_constraint(a, pltpu.HBM)


def _sem_pairs(n):
    return (pltpu.SemaphoreType.DMA((n * (N_DEV - 1),)), pltpu.SemaphoreType.DMA((n * (N_DEV - 1),)))


def _sem(i, m):
    return i * (N_DEV - 1) + m - 1


def _gather_copy(g_ref, i, m, send_sems, recv_sems, origin):
    peer, _ = _peer(m)
    return pltpu.make_async_remote_copy(
        src_ref=g_ref.at[origin], dst_ref=g_ref.at[origin],
        send_sem=send_sems.at[_sem(i, m)], recv_sem=recv_sems.at[_sem(i, m)],
        device_id=peer, device_id_type=MESH)


GATHER_MASKS = (1, 2, 4, 6)
FORWARD_MASKS = (2, 4, 6)


ALL_MASKS = tuple(range(1, N_DEV))


def gather_start(gs, after, name, masks=GATHER_MASKS):
    n = len(gs)

    def body(*refs):
        g_in = refs[:n]
        send_sems, recv_sems = refs[n + 1], refs[n + 2]
        token = refs[-1]
        me = _my_index()
        for i in range(n):
            for m in masks:
                _gather_copy(g_in[i], i, m, send_sems, recv_sems, me).start()
        token[...] = jnp.zeros_like(token)

    outs = pl.pallas_call(
        body, name=name,
        out_shape=(*_sem_pairs(n), *[pltpu.HBM(g.shape, g.dtype) for g in gs],
                   jax.ShapeDtypeStruct((8, 128), F32)),
        in_specs=[HBM_SPEC] * n + [ANY_SPEC],
        out_specs=(SEM_SPEC, SEM_SPEC, *[HBM_SPEC] * n, VMEM_SPEC),
        input_output_aliases={i: 2 + i for i in range(n)},
        compiler_params=pltpu.CompilerParams(has_side_effects=SIDE_EFFECT),
    )(*[_hbm(g) for g in gs], after)
    return outs[0], outs[1], list(outs[2:2 + n]), outs[-1]


def gather_start_groups(groups, after, name, masks=GATHER_MASKS):
    sizes = [len(g) for g in groups]
    flat = [a for g in groups for a in g]
    n, ng = len(flat), len(groups)

    def body(*refs):
        g_in = refs[:n]
        sems = refs[n + 1:n + 1 + 2 * ng]
        token = refs[-1]
        me = _my_index()
        pos = 0
        for k, size in enumerate(sizes):
            for i in range(size):
                for m in masks:
                    _gather_copy(g_in[pos + i], i, m, sems[2 * k], sems[2 * k + 1], me).start()
            pos += size
        token[...] = jnp.zeros_like(token)

    outs = pl.pallas_call(
        body, name=name,
        out_shape=(*[s for size in sizes for s in _sem_pairs(size)],
                   *[pltpu.HBM(g.shape, g.dtype) for g in flat], jax.ShapeDtypeStruct((8, 128), F32)),
        in_specs=[HBM_SPEC] * n + [ANY_SPEC],
        out_specs=(*[SEM_SPEC] * (2 * ng), *[HBM_SPEC] * n, VMEM_SPEC),
        input_output_aliases={i: 2 * ng + i for i in range(n)},
        compiler_params=pltpu.CompilerParams(has_side_effects=SIDE_EFFECT),
    )(*[_hbm(g) for g in flat], after)
    result, pos = [], 2 * ng
    for k, size in enumerate(sizes):
        result.append((outs[2 * k], outs[2 * k + 1], list(outs[pos:pos + size])))
        pos += size
    return result, outs[-1]


def gather_wait(gs, send_sems, recv_sems, after, name, masks=GATHER_MASKS):
    n = len(gs)

    def body(*refs):
        g_in = refs[:n]
        send, recv = refs[n], refs[n + 1]
        me = _my_index()
        for i in range(n):
            for m in masks:
                _, pidx = _peer(m)
                _gather_copy(g_in[i], i, m, send, recv, me).wait_send()
                _gather_copy(g_in[i], i, m, send, recv, pidx).wait_recv()

    outs = pl.pallas_call(
        body, name=name,
        out_shape=[pltpu.HBM(g.shape, g.dtype) for g in gs],
        in_specs=[HBM_SPEC] * n + [SEM_SPEC, SEM_SPEC, ANY_SPEC],
        out_specs=[HBM_SPEC] * n,
        input_output_aliases={i: i for i in range(n)},
        compiler_params=pltpu.CompilerParams(has_side_effects=SIDE_EFFECT),
    )(*gs, send_sems, recv_sems, after)
    return list(outs)


def sibling_forward(gs, name):
    n = len(gs)
    nf = len(FORWARD_MASKS)

    def body(*refs):
        g_in = refs[:n]
        send_sems, recv_sems = refs[2 * n:]
        x, y, c = lax.axis_index("x"), lax.axis_index("y"), lax.axis_index("c")
        sibling = (x, y, 1 - c)

        def copy(i, k, origin):
            return pltpu.make_async_remote_copy(
                src_ref=g_in[i].at[origin], dst_ref=g_in[i].at[origin],
                send_sem=send_sems.at[i * nf + k], recv_sem=recv_sems.at[i * nf + k],
                device_id=sibling, device_id_type=MESH)
        sends = []
        for i in range(n):
            for k, m in enumerate(FORWARD_MASKS):
                _, origin = _peer(m)
                cp = copy(i, k, origin)
                cp.start()
                sends.append(cp)
        for i in range(n):
            for k, m in enumerate(FORWARD_MASKS):
                _, origin = _peer(m ^ 1)
                copy(i, k, origin).wait_recv()
        for cp in sends:
            cp.wait_send()

    outs = pl.pallas_call(
        body, name=name,
        out_shape=[jax.ShapeDtypeStruct(g.shape, g.dtype) for g in gs],
        in_specs=[HBM_SPEC] * n, out_specs=[HBM_SPEC] * n,
        input_output_aliases={i: i for i in range(n)},
        scratch_shapes=[pltpu.SemaphoreType.DMA((n * nf,)), pltpu.SemaphoreType.DMA((n * nf,))],
    )(*gs)
    return list(outs)


def _forward_copy(g_ref, i, k, send_sems, recv_sems, origin):
    sibling = (lax.axis_index("x"), lax.axis_index("y"), 1 - lax.axis_index("c"))
    slot = i * len(FORWARD_MASKS) + k
    return pltpu.make_async_remote_copy(
        src_ref=g_ref.at[origin], dst_ref=g_ref.at[origin],
        send_sem=send_sems.at[slot], recv_sem=recv_sems.at[slot],
        device_id=sibling, device_id_type=MESH)


def forward_start(gs, after, name):
    n = len(gs)
    nsem = n * len(FORWARD_MASKS)

    def body(*refs):
        g_in = refs[:n]
        send_sems, recv_sems = refs[n + 1], refs[n + 2]
        token = refs[-1]
        for i in range(n):
            for k, m in enumerate(FORWARD_MASKS):
                _, origin = _peer(m)
                _forward_copy(g_in[i], i, k, send_sems, recv_sems, origin).start()
        token[...] = jnp.zeros_like(token)

    outs = pl.pallas_call(
        body, name=name,
        out_shape=(pltpu.SemaphoreType.DMA((nsem,)), pltpu.SemaphoreType.DMA((nsem,)),
                   *[pltpu.HBM(g.shape, g.dtype) for g in gs], jax.ShapeDtypeStruct((8, 128), F32)),
        in_specs=[HBM_SPEC] * n + [ANY_SPEC],
        out_specs=(SEM_SPEC, SEM_SPEC, *[HBM_SPEC] * n, VMEM_SPEC),
        input_output_aliases={i: 2 + i for i in range(n)},
        compiler_params=pltpu.CompilerParams(has_side_effects=SIDE_EFFECT),
    )(*[_hbm(g) for g in gs], after)
    return outs[0], outs[1], list(outs[2:2 + n]), outs[-1]


def forward_wait(gs, send_sems, recv_sems, after, name):
    n = len(gs)

    def body(*refs):
        g_in = refs[:n]
        send, recv = refs[n], refs[n + 1]
        for i in range(n):
            for k, m in enumerate(FORWARD_MASKS):
                _, mine = _peer(m)
                _, theirs = _peer(m ^ 1)
                _forward_copy(g_in[i], i, k, send, recv, mine).wait_send()
                _forward_copy(g_in[i], i, k, send, recv, theirs).wait_recv()

    outs = pl.pallas_call(
        body, name=name,
        out_shape=[pltpu.HBM(g.shape, g.dtype) for g in gs],
        in_specs=[HBM_SPEC] * n + [SEM_SPEC, SEM_SPEC, ANY_SPEC],
        out_specs=[HBM_SPEC] * n,
        input_output_aliases={i: i for i in range(n)},
        compiler_params=pltpu.CompilerParams(has_side_effects=SIDE_EFFECT),
    )(*gs, send_sems, recv_sems, after)
    return list(outs)


def _scatter_copy(g_ref, l_ref, i, m, send_sems, recv_sems):
    peer, pidx = _peer(m)
    return pltpu.make_async_remote_copy(
        src_ref=g_ref.at[pidx], dst_ref=l_ref.at[m - 1],
        send_sem=send_sems.at[_sem(i, m)], recv_sem=recv_sems.at[_sem(i, m)],
        device_id=peer, device_id_type=MESH)


def scatter_start(grads, after, name):
    n = len(grads)
    lands = [lax.empty((N_DEV - 1,) + g.shape[1:], g.dtype) for g in grads]

    def body(*refs):
        g_in, l_in = refs[:n], refs[n:2 * n]
        send_sems, recv_sems = refs[2 * n + 1], refs[2 * n + 2]
        token = refs[-1]
        for i in range(n):
            for m in range(1, N_DEV):
                _scatter_copy(g_in[i], l_in[i], i, m, send_sems, recv_sems).start()
        token[...] = jnp.zeros_like(token)

    outs = pl.pallas_call(
        body, name=name,
        out_shape=(*_sem_pairs(n), *[pltpu.HBM(g.shape, g.dtype) for g in grads],
                   *[pltpu.HBM(l.shape, l.dtype) for l in lands], jax.ShapeDtypeStruct((8, 128), F32)),
        in_specs=[HBM_SPEC] * (2 * n) + [ANY_SPEC],
        out_specs=(SEM_SPEC, SEM_SPEC, *[HBM_SPEC] * (2 * n), VMEM_SPEC),
        input_output_aliases={i: 2 + i for i in range(2 * n)},
        compiler_params=pltpu.CompilerParams(has_side_effects=SIDE_EFFECT),
    )(*[_hbm(g) for g in grads], *[_hbm(l) for l in lands], after)
    return outs[0], outs[1], list(outs[2:2 + n]), list(outs[2 + n:2 + 2 * n]), outs[-1]


def scatter_wait(grads, lands, send_sems, recv_sems, after, name):
    n = len(grads)

    def body(*refs):
        g_in, l_in = refs[:n], refs[n:2 * n]
        send, recv = refs[2 * n], refs[2 * n + 1]
        for i in range(n):
            for m in range(1, N_DEV):
                cp = _scatter_copy(g_in[i], l_in[i], i, m, send, recv)
                cp.wait_send()
                cp.wait_recv()

    outs = pl.pallas_call(
        body, name=name,
        out_shape=[pltpu.HBM(a.shape, a.dtype) for a in list(grads) + list(lands)],
        in_specs=[HBM_SPEC] * (2 * n) + [SEM_SPEC, SEM_SPEC, ANY_SPEC],
        out_specs=[HBM_SPEC] * (2 * n),
        input_output_aliases={i: i for i in range(2 * n)},
        compiler_params=pltpu.CompilerParams(has_side_effects=SIDE_EFFECT),
    )(*grads, *lands, send_sems, recv_sems, after)
    return list(outs[:n]), list(outs[n:])


def sum_slots(g, name):
    _, r, c = g.shape

    def body(g_ref, out_ref):
        acc = g_ref[0]
        for p in range(1, N_DEV):
            acc = acc + g_ref[p]
        out_ref[...] = acc

    return pl.pallas_call(
        body, name=name, out_shape=jax.ShapeDtypeStruct((r, c), F32),
        in_specs=[VMEM_SPEC], out_specs=VMEM_SPEC,
        compiler_params=pltpu.CompilerParams(vmem_limit_bytes=VMEM_LIMIT),
    )(g)


def _sigmoid(v):
    return 1.0 / (1.0 + jnp.exp(-v))


def _rms_fwd(xf, g):
    r = lax.rsqrt(jnp.mean(xf * xf, axis=-1, keepdims=True) + EPS)
    return xf * r, r


def _rms_bwd(xhat, r, g, dy):
    dg = jnp.sum(dy * xhat, axis=0, keepdims=True)
    dxh = dy * g
    dx = r * (dxh - xhat * jnp.mean(dxh * xhat, axis=-1, keepdims=True))
    return dx, dg


def _ln_stats(v):
    mu = jnp.mean(v, axis=-1, keepdims=True)
    vc = v - mu
    r = lax.rsqrt(jnp.mean(vc * vc, axis=-1, keepdims=True) + EPS)
    return vc * r, r


def _ln_bwd(xhat, r, dxh):
    return r * (dxh - jnp.mean(dxh, axis=-1, keepdims=True)
                - xhat * jnp.mean(dxh * xhat, axis=-1, keepdims=True))


def _dot(a, b):
    return jnp.dot(a, b, preferred_element_type=F32)


def _dot_nt(a, b):
    return lax.dot_general(a, b, (((1,), (1,)), ((), ())), preferred_element_type=F32)


def _dot_tn(a, b):
    return lax.dot_general(a, b, (((0,), (0,)), ((), ())), preferred_element_type=F32)


def _full_weight(w_ref, kind):
    assert kind == "row"
    p, a, b = w_ref.shape
    return w_ref[...].reshape(p * a, b)


def _wspec(wg):
    return pl.BlockSpec(wg.shape, lambda *_: (0, 0, 0))


def mm_rows(a, wg, kind, *, gain=None, residual=None, out_dtype=F32, name, tm=None):
    m, k = a.shape
    p, wa, wb = wg.shape
    n = p * wb if kind == "col" else wb
    tm = _row_tile(m, tm)
    has_gain, has_res = gain is not None, residual is not None

    def body(*refs):
        refs = list(refs)
        a_ref = refs.pop(0)
        g_ref = refs.pop(0) if has_gain else None
        w_ref = refs.pop(0)
        r_ref = refs.pop(0) if has_res else None
        o_ref = refs.pop(0)
        if has_gain:
            xhat, _ = _rms_fwd(a_ref[...].astype(F32), None)
            h = (xhat * g_ref[...]).astype(BF16)
        else:
            h = a_ref[...].astype(BF16)
        if kind == "col":
            for j in range(p):
                o = _dot(h, w_ref[j])
                if has_res:
                    o = o + r_ref[:, j * wb:(j + 1) * wb]
                o_ref[:, j * wb:(j + 1) * wb] = o.astype(out_dtype)
        else:
            o = _dot(h, _full_weight(w_ref, "row"))
            if has_res:
                o = o + r_ref[...]
            o_ref[...] = o.astype(out_dtype)

    operands = [a]
    in_specs = [pl.BlockSpec((tm, k), lambda i: (i, 0))]
    if has_gain:
        operands.append(gain.reshape(1, k))
        in_specs.append(pl.BlockSpec((1, k), lambda i: (0, 0)))
    operands.append(wg)
    in_specs.append(_wspec(wg))
    if has_res:
        operands.append(residual)
        in_specs.append(pl.BlockSpec((tm, n), lambda i: (i, 0)))
    return pl.pallas_call(
        body, name=name, grid=(m // tm,),
        out_shape=jax.ShapeDtypeStruct((m, n), out_dtype),
        in_specs=in_specs, out_specs=pl.BlockSpec((tm, n), lambda i: (i, 0)),
        compiler_params=_params("parallel"),
    )(*operands)


def mm_nt(dz, wg, kind, *, x=None, gain=None, dx_in=None, after=None, name, tm=None):
    m, n = dz.shape
    p, wa, wb = wg.shape
    k = wa if kind == "col" else p * wa
    tm = _row_tile(m, tm)
    epi = x is not None
    has_dx = dx_in is not None
    has_after = after is not None

    def body(*refs):
        refs = list(refs)
        dz_ref, w_ref = refs.pop(0), refs.pop(0)
        if epi:
            x_ref, g_ref = refs.pop(0), refs.pop(0)
            dxi_ref = refs.pop(0) if has_dx else None
        if has_after:
            refs.pop(0)
        if epi:
            dx_ref, h_ref, dg_ref = refs
        else:
            (da_ref,) = refs
        dzb = dz_ref[...].astype(BF16)
        if kind == "col":
            da = _dot_nt(dzb[:, 0:wb], w_ref[0])
            for j in range(1, p):
                da = da + _dot_nt(dzb[:, j * wb:(j + 1) * wb], w_ref[j])
        else:
            da = _dot_nt(dzb, _full_weight(w_ref, "row"))
        if not epi:
            da_ref[...] = da
            return
        g = g_ref[...]
        xhat, r = _rms_fwd(x_ref[...].astype(F32), None)
        h_ref[...] = (xhat * g).astype(BF16)
        dx, dg = _rms_bwd(xhat, r, g, da)
        if has_dx:
            dx = dx + dxi_ref[...]
        dx_ref[...] = dx

        @pl.when(pl.program_id(0) == 0)
        def _():
            dg_ref[...] = jnp.zeros_like(dg_ref)
        dg_ref[...] += dg

    row = lambda i: (i, 0)
    operands = [dz, wg]
    in_specs = [pl.BlockSpec((tm, n), row), _wspec(wg)]
    if epi:
        operands += [x, gain.reshape(1, k)]
        in_specs += [pl.BlockSpec((tm, k), row), pl.BlockSpec((1, k), lambda i: (0, 0))]
        if has_dx:
            operands.append(dx_in)
            in_specs.append(pl.BlockSpec((tm, k), row))
        out_shape = [jax.ShapeDtypeStruct((m, k), F32), jax.ShapeDtypeStruct((m, k), BF16),
                     jax.ShapeDtypeStruct((1, k), F32)]
        out_specs = [pl.BlockSpec((tm, k), row), pl.BlockSpec((tm, k), row),
                     pl.BlockSpec((1, k), lambda i: (0, 0))]
    else:
        out_shape = jax.ShapeDtypeStruct((m, k), F32)
        out_specs = pl.BlockSpec((tm, k), row)
    if has_after:
        operands.append(after)
        in_specs.append(ANY_SPEC)
    return pl.pallas_call(
        body, name=name, grid=(m // tm,), out_shape=out_shape,
        in_specs=in_specs, out_specs=out_specs,
        compiler_params=_params("arbitrary"),
    )(*operands)


def mm_tn(a, b, *, nb, a_spec, b_spec, ka, nbk, tm, m, scale=1.0, out_dtype=BF16, col_slots=1,
          after=None, name):
    ni = m // tm
    assert col_slots == 1 or nb == 1
    cw = nbk // col_slots
    extra = [] if after is None else [after]

    def body(a_ref, b_ref, *rest):
        o_ref, acc = rest[len(extra):]
        i = pl.program_id(1)

        @pl.when(i == 0)
        def _():
            acc[...] = jnp.zeros_like(acc)
        acc[...] += _dot_tn(a_ref[...].astype(BF16), b_ref[...].astype(BF16))

        @pl.when(i == ni - 1)
        def _():
            if col_slots == 1:
                o_ref[...] = (acc[...] * scale).astype(out_dtype)
            else:
                for j in range(col_slots):
                    o_ref[j] = (acc[:, j * cw:(j + 1) * cw] * scale).astype(out_dtype)

    if col_slots == 1:
        out_shape = jax.ShapeDtypeStruct((nb, ka, nbk), out_dtype)
        out_spec = pl.BlockSpec((None, ka, nbk), lambda s, i: (s, 0, 0))
    else:
        out_shape = jax.ShapeDtypeStruct((col_slots, ka, cw), out_dtype)
        out_spec = pl.BlockSpec((col_slots, ka, cw), lambda s, i: (0, 0, 0))
    return pl.pallas_call(
        body, name=name, grid=(nb, ni), out_shape=out_shape,
        in_specs=[a_spec, b_spec] + [ANY_SPEC] * len(extra), out_specs=out_spec,
        scratch_shapes=[pltpu.VMEM((ka, nbk), F32)],
        compiler_params=_params("parallel", "arbitrary"),
    )(a, b, *extra)


def _ffn_specs(w_in_g, w_out_g, d):
    nf = w_in_g.shape[1]
    hr = w_out_g.shape[1]
    assert 2 * hr == nf
    w_in5 = w_in_g.reshape(2, 4, nf, d)
    w_out5 = w_out_g.reshape(4, 2, hr, d)
    in_spec = pl.BlockSpec((2, None, nf, d), lambda i, j: (0, j, 0, 0))
    out_spec = pl.BlockSpec((None, 2, hr, d), lambda i, j: (j, 0, 0, 0))
    return w_in5, w_out5, in_spec, out_spec, nf


def ffn_fwd(x, gain, w_in_g, w_out_g, *, name, tm=None):
    t, d = x.shape
    tm = _row_tile(t, tm)
    w_in5, w_out5, wi_spec, wo_spec, nf = _ffn_specs(w_in_g, w_out_g, d)

    def body(x_ref, g_ref, wi_ref, wo_ref, o_ref, gu_ref, h_scr, acc):
        j = pl.program_id(1)

        @pl.when(j == 0)
        def _():
            xhat, _ = _rms_fwd(x_ref[...], None)
            h_scr[...] = (xhat * g_ref[...]).astype(BF16)
            acc[...] = jnp.zeros_like(acc)
        wo = wo_ref[...].reshape(nf, d)

        def project(rows):
            h = h_scr[rows]
            return _dot_nt(h, wi_ref[0]), _dot_nt(h, wi_ref[1])

        sub = tm // FFN_FWD_SPLIT
        parts = [slice(k * sub, (k + 1) * sub) for k in range(FFN_FWD_SPLIT)]
        gt, up = project(parts[0])
        for k, rows in enumerate(parts):
            if k + 1 < len(parts):
                nxt = project(parts[k + 1])
            gu_ref[0, rows] = gt.astype(BF16)
            gu_ref[1, rows] = up.astype(BF16)
            act = (gt * _sigmoid(gt) * up).astype(BF16)
            acc[rows] += _dot(act, wo)
            if k + 1 < len(parts):
                gt, up = nxt

        @pl.when(j == 3)
        def _():
            o_ref[...] = x_ref[...] + 0.5 * acc[...]

    return pl.pallas_call(
        body, name=name, grid=(t // tm, 4),
        out_shape=[jax.ShapeDtypeStruct((t, d), F32), jax.ShapeDtypeStruct((2, 4, t, nf), BF16)],
        in_specs=[pl.BlockSpec((tm, d), lambda i, j: (i, 0)),
                  pl.BlockSpec((1, d), lambda i, j: (0, 0)), wi_spec, wo_spec],
        out_specs=[pl.BlockSpec((tm, d), lambda i, j: (i, 0)),
                   pl.BlockSpec((2, None, tm, nf), lambda i, j: (0, j, i, 0))],
        scratch_shapes=[pltpu.VMEM((tm, d), BF16), pltpu.VMEM((tm, d), F32)],
        compiler_params=_params("parallel", "arbitrary"),
    )(x, gain.reshape(1, d), w_in5, w_out5)


def ffn_bwd_rows(x, dy, gu, gain, w_in_g, w_out_g, after, *, name, tm=None):
    t, d = x.shape
    tm = _row_tile(t, tm)
    w_in5, w_out5, wi_spec, wo_spec, nf = _ffn_specs(w_in_g, w_out_g, d)

    def body(x_ref, dy_ref, gu_ref, g_ref, wi_ref, wo_ref, after_ref, dx_ref, h_ref, act_ref, dgu_ref, dg_ref,
             dyh_scr, dh_acc):
        i, j = pl.program_id(0), pl.program_id(1)

        @pl.when(j == 0)
        def _():
            xhat, _ = _rms_fwd(x_ref[...], None)
            h_ref[...] = (xhat * g_ref[...]).astype(BF16)
            dyh_scr[...] = (0.5 * dy_ref[...]).astype(BF16)
            dh_acc[...] = jnp.zeros_like(dh_acc)
        wo = wo_ref[...].reshape(nf, d)

        def gates(rows):
            gt = gu_ref[0, rows].astype(F32)
            up = gu_ref[1, rows].astype(F32)
            sg = _sigmoid(gt)
            silu = gt * sg
            act_ref[rows] = (silu * up).astype(BF16)
            return up * (sg * (1.0 + gt * (1.0 - sg))), silu

        def grads(rows, dact, dsilu_up, silu):
            dgt = (dact * dsilu_up).astype(BF16)
            dup = (dact * silu).astype(BF16)
            dgu_ref[0, rows] = dgt
            dgu_ref[1, rows] = dup
            return dgt, dup

        sub = tm // FFN_BWD_SPLIT
        parts = [slice(k * sub, (k + 1) * sub) for k in range(FFN_BWD_SPLIT)]
        dact = _dot_nt(dyh_scr[parts[0]], wo)
        gate = gates(parts[0])
        for k, rows in enumerate(parts):
            if k + 1 < len(parts):
                dact_next = _dot_nt(dyh_scr[parts[k + 1]], wo)
            dgt, dup = grads(rows, dact, *gate)
            dh_acc[rows] += _dot(dgt, wi_ref[0]) + _dot(dup, wi_ref[1])
            if k + 1 < len(parts):
                gate = gates(parts[k + 1])
                dact = dact_next

        @pl.when(j == 3)
        def _():
            g = g_ref[...]
            xhat, r = _rms_fwd(x_ref[...], None)
            dx, dg = _rms_bwd(xhat, r, g, dh_acc[...])
            dx_ref[...] = dy_ref[...] + dx

            @pl.when(i == 0)
            def _():
                dg_ref[...] = jnp.zeros_like(dg_ref)
            dg_ref[...] += dg

    row = lambda i, j: (i, 0)
    return pl.pallas_call(
        body, name=name, grid=(t // tm, 4),
        out_shape=[jax.ShapeDtypeStruct((t, d), F32), jax.ShapeDtypeStruct((t, d), BF16),
                   jax.ShapeDtypeStruct((4, t, nf), BF16), jax.ShapeDtypeStruct((2, 4, t, nf), BF16),
                   jax.ShapeDtypeStruct((1, d), F32), jax.ShapeDtypeStruct((t, d), BF16)],
        in_specs=[pl.BlockSpec((tm, d), row), pl.BlockSpec((tm, d), row),
                  pl.BlockSpec((2, None, tm, nf), lambda i, j: (0, j, i, 0)),
                  pl.BlockSpec((1, d), lambda i, j: (0, 0)), wi_spec, wo_spec, ANY_SPEC],
        out_specs=[pl.BlockSpec((tm, d), row), pl.BlockSpec((tm, d), row),
                   pl.BlockSpec((None, tm, nf), lambda i, j: (j, i, 0)),
                   pl.BlockSpec((2, None, tm, nf), lambda i, j: (0, j, i, 0)),
                   pl.BlockSpec((1, d), lambda i, j: (0, 0)), pl.BlockSpec((tm, d), row)],
        scratch_shapes=[pltpu.VMEM((tm, d), F32)],
        compiler_params=_params("arbitrary", "arbitrary"),
    )(x, dy, gu, gain.reshape(1, d), w_in5, w_out5, after)


def ffn_grad_w_in(h, dgu, after, *, name):
    t, d = h.shape
    nf = dgu.shape[-1]
    tm = _row_tile(t, TN_TILE)
    return mm_tn(dgu.reshape(8, t, nf), h, nb=8, ka=nf, nbk=d, tm=tm, m=t, after=after,
                 a_spec=pl.BlockSpec((None, tm, nf), lambda s, i: (s, i, 0)),
                 b_spec=pl.BlockSpec((tm, d), lambda s, i: (i, 0)), name=name)


def ffn_grad_w_out(act, dyh, after, *, name):
    _, t, nf = act.shape
    d = dyh.shape[1]
    tm = _row_tile(t, TN_TILE)
    d_w_out = mm_tn(act, dyh, nb=4, ka=nf, nbk=d, tm=tm, m=t, after=after,
                    a_spec=pl.BlockSpec((None, tm, nf), lambda s, i: (s, i, 0)),
                    b_spec=pl.BlockSpec((tm, d), lambda s, i: (i, 0)), name=name)
    return d_w_out.reshape(8, nf // 2, d)


def _lane_group(shape):
    return lax.shift_right_logical(lax.broadcasted_iota(jnp.int32, shape, 1), 6)


def _pool_count(t0, rows):
    t = (t0 + lax.broadcasted_iota(jnp.int32, (rows, MIX_W), 0) + 1).astype(F32)
    return jnp.minimum(t, _by_group(_lane_group((rows, MIX_W)), 2.0, 4.0, 8.0, 16.0))


def _by_group(grp, v0, v1, v2, v3):
    return jnp.where(grp == 0, v0, jnp.where(grp == 1, v1, jnp.where(grp == 2, v2, v3)))


def _sgu_mix(wt_ref, vnc):
    grp = _lane_group((SGU_CHUNK, MIX_W))
    out = jnp.zeros((SGU_CHUNK, MIX_W), F32)
    for hd in range(N_HEADS):
        out = jnp.where(grp == hd, _dot(wt_ref[hd], vnc), out)
    return out


def _pool_fwd(s1, s2, s3, t0, ts, lo):
    h = lo
    s2[h - 24:h + ts] = s1[h - 24:h + ts] + s1[h - 25:h + ts - 1]
    s3[h - 16:h + ts] = s2[h - 16:h + ts] + s2[h - 18:h + ts - 2]
    sum2 = s2[h:h + ts]
    sum4 = s3[h:h + ts]
    s2[h - 8:h + ts] = s3[h - 8:h + ts] + s3[h - 12:h + ts - 4]
    sum8 = s2[h:h + ts]
    sum16 = sum8 + s2[h - 8:h + ts - 8]
    grp = _lane_group((ts, MIX_W))
    return _by_group(grp, sum2, sum4, sum8, sum16) / _pool_count(t0, ts) - s1[h:h + ts]


def _make_shifts(src, sh, rows):
    for b in range(1, 8):
        sh[b, 0:rows] = src[b:b + rows]


def _rows_at(src, sh, start, n):
    a, b = divmod(start, 8)
    return src[8 * a:8 * a + n] if b == 0 else sh[b, 8 * a:8 * a + n]


def mixer_fwd(z, sconv, cconv, vecs, wt, bexp, pbd, x_res, wmo_g, *, name, ts=None):
    t = z.shape[0]
    ts = _row_tile(t, MIX_TILE if ts is None else ts)
    hl = HALO
    w = MIX_W
    nch = ts // SGU_CHUNK

    def body(zc, zp, sconv_ref, cconv_ref, vec_ref, wt_ref, bexp_ref, pbd_ref, xr_ref, wmo_ref,
             y_ref, xo_ref, s1, s2, s3, sh):
        i = pl.program_id(0)
        has_prev = i > 0

        def col(ref, c):
            return ref[:, c * w:(c + 1) * w]

        def prev(c):
            return jnp.where(has_prev, col(zp, c), 0.0)

        s1[0:hl] = prev(1) * prev(2)
        s1[hl:hl + ts] = col(zc, 1) * col(zc, 2)
        cv = sconv_ref[0:1] * s1[hl - 2:hl - 2 + ts]
        for k in range(1, SCONV_K):
            cv = cv + sconv_ref[k:k + 1] * s1[hl - 2 + k:hl - 2 + k + ts]
        y_ref[:, 0:w] = (col(zc, 0) * cv).astype(BF16)

        xhat, _ = _ln_stats(col(zc, 4))
        vn = (xhat * vec_ref[0:1]).astype(BF16)
        for c in range(nch):
            rows = slice(c * SGU_CHUNK, (c + 1) * SGU_CHUNK)
            mixed = _sgu_mix(wt_ref, vn[rows]) + bexp_ref[...]
            y_ref[rows, w:2 * w] = (zc[rows, 3 * w:4 * w] * mixed).astype(BF16)

        s1[0:hl] = prev(5) * _sigmoid(prev(6))
        s1[hl:hl + ts] = col(zc, 5) * _sigmoid(col(zc, 6))
        off = hl - (CCONV_K - 1)
        _make_shifts(s1, sh, hl + ts - 8)
        cv = cconv_ref[0:1] * _rows_at(s1, sh, off, ts)
        for k in range(1, CCONV_K):
            cv = cv + cconv_ref[k:k + 1] * _rows_at(s1, sh, off + k, ts)
        xhat, _ = _ln_stats(cv)
        ln = xhat * vec_ref[1:2] + vec_ref[2:3]
        y_ref[:, 2 * w:3 * w] = (ln * _sigmoid(ln)).astype(BF16)

        s1[0:hl] = prev(7)
        s1[hl:hl + ts] = col(zc, 7)
        pooled = _pool_fwd(s1, s2, s3, i * ts, ts, hl)
        y_ref[:, 3 * w:4 * w] = (_dot(pooled.astype(BF16), pbd_ref[...]) * vec_ref[3:4]).astype(BF16)

        xo_ref[...] = xr_ref[...] + _dot(y_ref[...], _full_weight(wmo_ref, "row"))

    full = lambda shape: pl.BlockSpec(shape, lambda i: (0,) * len(shape))
    row = lambda i: (i, 0)
    return pl.pallas_call(
        body, name=name, grid=(t // ts,),
        out_shape=[jax.ShapeDtypeStruct((t, 4 * w), BF16), jax.ShapeDtypeStruct((t, 4 * w), F32)],
        in_specs=[pl.BlockSpec((ts, 8 * w), row),
                  pl.BlockSpec((hl, 8 * w), lambda i: (jnp.maximum(i * (ts // hl) - 1, 0), 0)),
                  full((8, w)), full((32, w)), full((8, w)), full((N_HEADS, SGU_CHUNK, SGU_CHUNK)),
                  full((SGU_CHUNK, w)), full((w, w)), pl.BlockSpec((ts, 4 * w), row), _wspec(wmo_g)],
        out_specs=[pl.BlockSpec((ts, 4 * w), row), pl.BlockSpec((ts, 4 * w), row)],
        scratch_shapes=[pltpu.VMEM((hl + ts, w), F32)] * 3 + [pltpu.VMEM((8, hl + ts, w), F32)],
        compiler_params=_params("parallel"),
    )(z, z, sconv, cconv, vecs, wt, bexp, pbd, x_res, wmo_g)


def mixer_bwd(z, dx, wmo_g, sconv, cconv, vecs, wt, bexp, pbd, *, name, ts=None):
    t = z.shape[0]
    ts = _row_tile(t, MIX_TILE if ts is None else ts)
    hl = HALO
    w = MIX_W
    nch = ts // SGU_CHUNK
    ni = t // ts
    ext = ts + hl

    def body(zc, zp, zn, dxc, dxn_, wmo_ref, sconv_ref, cconv_ref, vec_ref, wt_ref, bexp_ref, pbd_ref,
             dz_ref, gvec_ref, gcc_ref, gwt_ref, gb_ref, gpbd_ref, s1, s2, s3, sh1, sh3, dyc, dyn):
        i = pl.program_id(0)
        has_prev = i > 0
        has_next = i < ni - 1
        wmo = _full_weight(wmo_ref, "row")
        dyc[...] = _dot_nt(dxc[...].astype(BF16), wmo)
        dyn[...] = _dot_nt(dxn_[...].astype(BF16), wmo)

        @pl.when(i == 0)
        def _():
            gvec_ref[...] = jnp.zeros_like(gvec_ref)
            gcc_ref[...] = jnp.zeros_like(gcc_ref)
            gwt_ref[...] = jnp.zeros_like(gwt_ref)
            gb_ref[...] = jnp.zeros_like(gb_ref)
            gpbd_ref[...] = jnp.zeros_like(gpbd_ref)

        def col(ref, c):
            return ref[:, c * w:(c + 1) * w]

        def prev(c):
            return jnp.where(has_prev, col(zp, c), 0.0)

        def nxt(c):
            return jnp.where(has_next, col(zn, c), 0.0)

        def dnext(c):
            return jnp.where(has_next, col(dyn, c), 0.0)

        def rowsum(v):
            return jnp.sum(v, axis=0, keepdims=True)

        s1[0:hl] = prev(1) * prev(2)
        s1[hl:hl + ts] = col(zc, 1) * col(zc, 2)
        s1[hl + ts:hl + ts + hl] = nxt(1) * nxt(2)
        cv = sconv_ref[0:1] * s1[hl - 2:hl - 2 + ts]
        for k in range(1, SCONV_K):
            cv = cv + sconv_ref[k:k + 1] * s1[hl - 2 + k:hl - 2 + k + ts]
        dya = col(dyc, 0)
        dz_ref[:, 0:w] = (dya * cv).astype(BF16)
        s2[0:ts] = dya * col(zc, 0)
        s2[ts:ext] = dnext(0) * nxt(0)
        dv = sconv_ref[0:1] * s2[2:2 + ts]
        for k in range(1, SCONV_K):
            dv = dv + sconv_ref[k:k + 1] * s2[2 - k:2 - k + ts]
        dz_ref[:, w:2 * w] = (dv * col(zc, 2)).astype(BF16)
        dz_ref[:, 2 * w:3 * w] = (dv * col(zc, 1)).astype(BF16)
        dcv = s2[0:ts]
        for k in range(SCONV_K):
            gvec_ref[k:k + 1] += rowsum(dcv * s1[hl - 2 + k:hl - 2 + k + ts])

        g_sgu = vec_ref[0:1]
        xhat, rstd = _ln_stats(col(zc, 4))
        vn = (xhat * g_sgu).astype(BF16)
        grp = _lane_group((SGU_CHUNK, w))
        lane = lax.broadcasted_iota(jnp.int32, (SGU_CHUNK, SGU_CHUNK), 1)
        tril = lax.broadcasted_iota(jnp.int32, (SGU_CHUNK, SGU_CHUNK), 0) >= lane
        for c in range(nch):
            rows = slice(c * SGU_CHUNK, (c + 1) * SGU_CHUNK)
            vnc = vn[rows]
            mixed = _sgu_mix(wt_ref, vnc) + bexp_ref[...]
            dyb = dyc[rows, w:2 * w]
            dz_ref[rows, 3 * w:4 * w] = (dyb * mixed).astype(BF16)
            dmix = dyb * zc[rows, 3 * w:4 * w]
            dmixb = dmix.astype(BF16)
            dvn = jnp.zeros((SGU_CHUNK, w), F32)
            gb = jnp.zeros((SGU_CHUNK, SGU_CHUNK), F32)
            for hd in range(N_HEADS):
                dvn = jnp.where(grp == hd, _dot_tn(wt_ref[hd], dmixb), dvn)
                dm_h = jnp.where(grp == hd, dmix, 0.0)
                gwt_ref[hd] += jnp.where(tril, _dot_nt(dm_h.astype(BF16), vnc), 0.0)
                gb = gb + jnp.where(lane == hd, jnp.sum(dm_h, axis=1, keepdims=True), 0.0)
            gb_ref[...] += gb
            s3[rows] = dvn
        dvn = s3[0:ts]
        gvec_ref[3:4] += rowsum(dvn * xhat)
        dz_ref[:, 4 * w:5 * w] = _ln_bwd(xhat, rstd, dvn * g_sgu).astype(BF16)

        sig_c = _sigmoid(col(zc, 6))
        s1[0:hl] = prev(5) * _sigmoid(prev(6))
        s1[hl:hl + ts] = col(zc, 5) * sig_c
        s1[hl + ts:hl + ts + hl] = nxt(5) * _sigmoid(nxt(6))
        off = hl - (CCONV_K - 1)
        _make_shifts(s1, sh1, ts + 2 * hl - 8)
        cv = cconv_ref[0:1] * _rows_at(s1, sh1, off, ext)
        for k in range(1, CCONV_K):
            cv = cv + cconv_ref[k:k + 1] * _rows_at(s1, sh1, off + k, ext)
        xhat, rstd = _ln_stats(cv)
        ln = xhat * vec_ref[1:2] + vec_ref[2:3]
        sg = _sigmoid(ln)
        s2[0:ts] = col(dyc, 2)
        s2[ts:ext] = dnext(2)
        dln = s2[0:ext] * (sg * (1.0 + ln * (1.0 - sg)))
        gvec_ref[4:5] += rowsum(dln[0:ts] * xhat[0:ts])
        gvec_ref[5:6] += rowsum(dln[0:ts])
        s3[0:ext] = _ln_bwd(xhat, rstd, dln * vec_ref[1:2])
        _make_shifts(s3, sh3, ext - 8)
        dyg = cconv_ref[0:1] * _rows_at(s3, sh3, CCONV_K - 1, ts)
        for k in range(1, CCONV_K):
            dyg = dyg + cconv_ref[k:k + 1] * _rows_at(s3, sh3, CCONV_K - 1 - k, ts)
        dz_ref[:, 5 * w:6 * w] = (dyg * sig_c).astype(BF16)
        dz_ref[:, 6 * w:7 * w] = (dyg * col(zc, 5) * sig_c * (1.0 - sig_c)).astype(BF16)
        dcv = s3[0:ts]
        for k in range(CCONV_K):
            gcc_ref[k:k + 1] += rowsum(dcv * _rows_at(s1, sh1, off + k, ts))

        scale = vec_ref[3:4]
        s1[0:hl] = prev(7)
        s1[hl:hl + ts] = col(zc, 7)
        pooled = _pool_fwd(s1, s2, s3, i * ts, ts, hl).astype(BF16)
        q0 = _dot(pooled, pbd_ref[...])
        dyd = col(dyc, 3)
        gvec_ref[6:7] += rowsum(dyd * q0)
        dq = (dyd * scale).astype(BF16)
        gpbd_ref[...] += _dot_tn(pooled, dq)
        s1[0:ts] = _dot_nt(dq, pbd_ref[...])
        s1[ts:ext] = _dot_nt((dnext(3) * scale).astype(BF16), pbd_ref[...])
        dpool = s1[0:ts]
        s2[0:ext] = s1[0:ext] / _pool_count(i * ts, ext)
        s3[0:ts + 24] = s2[0:ts + 24] + s2[1:ts + 25]
        f2 = s3[0:ts]
        s2[0:ts + 16] = s3[0:ts + 16] + s3[2:ts + 18]
        f4 = s2[0:ts]
        s3[0:ts + 8] = s2[0:ts + 8] + s2[4:ts + 12]
        f8 = s3[0:ts]
        f16 = f8 + s3[8:ts + 8]
        dz_ref[:, 7 * w:8 * w] = (_by_group(_lane_group((ts, w)), f2, f4, f8, f16) - dpool).astype(BF16)

    full = lambda shape: pl.BlockSpec(shape, lambda i: (0,) * len(shape))
    r = ts // hl
    prev_map = lambda i: (jnp.maximum(i * r - 1, 0), 0)
    next_map = lambda i: (jnp.minimum((i + 1) * r, t // hl - 1), 0)
    return pl.pallas_call(
        body, name=name, grid=(ni,),
        out_shape=[jax.ShapeDtypeStruct((t, 8 * w), BF16), jax.ShapeDtypeStruct((8, w), F32),
                   jax.ShapeDtypeStruct((32, w), F32),
                   jax.ShapeDtypeStruct((N_HEADS, SGU_CHUNK, SGU_CHUNK), F32),
                   jax.ShapeDtypeStruct((SGU_CHUNK, SGU_CHUNK), F32), jax.ShapeDtypeStruct((w, w), F32)],
        in_specs=[pl.BlockSpec((ts, 8 * w), lambda i: (i, 0)),
                  pl.BlockSpec((hl, 8 * w), prev_map), pl.BlockSpec((hl, 8 * w), next_map),
                  pl.BlockSpec((ts, 4 * w), lambda i: (i, 0)), pl.BlockSpec((hl, 4 * w), next_map),
                  _wspec(wmo_g),
                  full((8, w)), full((32, w)), full((8, w)), full((N_HEADS, SGU_CHUNK, SGU_CHUNK)),
                  full((SGU_CHUNK, w)), full((w, w))],
        out_specs=[pl.BlockSpec((ts, 8 * w), lambda i: (i, 0)), full((8, w)), full((32, w)),
                   full((N_HEADS, SGU_CHUNK, SGU_CHUNK)), full((SGU_CHUNK, SGU_CHUNK)), full((w, w))],
        scratch_shapes=[pltpu.VMEM((ts + 2 * hl, w), F32)] * 3 + [pltpu.VMEM((8, ts + 2 * hl, w), F32)] * 2
        + [pltpu.VMEM((ts, 4 * w), F32), pltpu.VMEM((hl, 4 * w), F32)],
        compiler_params=_params("arbitrary"),
    )(z, z, z, dx, dx, wmo_g, sconv, cconv, vecs, wt, bexp, pbd)


def _attn_head(q, kv_ref, hd, d):
    hw = d // N_HEADS
    qh = q[:, hd * hw:(hd + 1) * hw]
    kh = kv_ref[:, hd * hw:(hd + 1) * hw].astype(BF16)
    vh = kv_ref[:, d + hd * hw:d + (hd + 1) * hw].astype(BF16)
    s = _dot_nt(qh, kh) * (1.0 / (hw ** 0.5))
    e = jnp.exp(s - jnp.max(s, axis=-1, keepdims=True))
    p = e / jnp.sum(e, axis=-1, keepdims=True)
    return qh, kh, vh, p


def xattn_fwd(x, gain, kv, wq_g, wo_g, *, name, tm=None):
    t, d = x.shape
    nm = kv.shape[0]
    tm = _row_tile(t, tm)
    hw = d // N_HEADS

    def body(x_ref, g_ref, kv_ref, wq_ref, wo_ref, o_ref):
        xv = x_ref[...]
        xhat, _ = _rms_fwd(xv, None)
        h = (xhat * g_ref[...]).astype(BF16)
        q = _dot(h, _full_weight(wq_ref, "row")).astype(BF16)
        wo = _full_weight(wo_ref, "row")
        out = xv
        for hd in range(N_HEADS):
            _, _, vh, p = _attn_head(q, kv_ref, hd, d)
            oh = _dot(p.astype(BF16), vh).astype(BF16)
            out = out + _dot(oh, wo[hd * hw:(hd + 1) * hw])
        o_ref[...] = out

    row = lambda i: (i, 0)
    return pl.pallas_call(
        body, name=name, grid=(t // tm,),
        out_shape=jax.ShapeDtypeStruct((t, d), F32),
        in_specs=[pl.BlockSpec((tm, d), row), pl.BlockSpec((1, d), lambda i: (0, 0)),
                  pl.BlockSpec((nm, 2 * d), lambda i: (0, 0)), _wspec(wq_g), _wspec(wo_g)],
        out_specs=pl.BlockSpec((tm, d), row),
        compiler_params=_params("parallel"),
    )(x, gain.reshape(1, d), kv, wq_g, wo_g)


def xattn_bwd_rows(x, dxn, gain, kv, wq_g, wo_g, after, *, name, tm=None):
    t, d = x.shape
    nm = kv.shape[0]
    tm = _row_tile(t, tm)
    hw = d // N_HEADS

    def body(x_ref, dxn_ref, g_ref, kv_ref, wq_ref, wo_ref, after_ref,
             dx_ref, h_ref, dq_ref, o_ref, dkv_ref, dg_ref):
        i = pl.program_id(0)

        @pl.when(i == 0)
        def _():
            dkv_ref[...] = jnp.zeros_like(dkv_ref)
            dg_ref[...] = jnp.zeros_like(dg_ref)
        g = g_ref[...]
        xhat, r = _rms_fwd(x_ref[...], None)
        h = (xhat * g).astype(BF16)
        h_ref[...] = h
        wq = _full_weight(wq_ref, "row")
        q = _dot(h, wq).astype(BF16)
        dxn = dxn_ref[...]
        do = _dot_nt(dxn.astype(BF16), _full_weight(wo_ref, "row")).astype(BF16)
        for hd in range(N_HEADS):
            cols = slice(hd * hw, (hd + 1) * hw)
            qh, kh, vh, p = _attn_head(q, kv_ref, hd, d)
            pb = p.astype(BF16)
            o_ref[:, cols] = _dot(pb, vh).astype(BF16)
            doh = do[:, cols]
            dkv_ref[:, d + hd * hw:d + (hd + 1) * hw] += _dot_tn(pb, doh)
            dp = _dot_nt(doh, vh)
            ds = (p * (dp - jnp.sum(dp * p, axis=-1, keepdims=True)) * (1.0 / (hw ** 0.5))).astype(BF16)
            dq_ref[:, cols] = _dot(ds, kh).astype(BF16)
            dkv_ref[:, cols] += _dot_tn(ds, qh)
        dh = _dot_nt(dq_ref[...], wq)
        dx, dg = _rms_bwd(xhat, r, g, dh)
        dx_ref[...] = dxn + dx
        dg_ref[...] += dg

    row = lambda i: (i, 0)
    fix = lambda i: (0, 0)
    return pl.pallas_call(
        body, name=name, grid=(t // tm,),
        out_shape=[jax.ShapeDtypeStruct((t, d), F32), jax.ShapeDtypeStruct((t, d), BF16),
                   jax.ShapeDtypeStruct((t, d), BF16), jax.ShapeDtypeStruct((t, d), BF16),
                   jax.ShapeDtypeStruct((nm, 2 * d), F32), jax.ShapeDtypeStruct((1, d), F32)],
        in_specs=[pl.BlockSpec((tm, d), row), pl.BlockSpec((tm, d), row), pl.BlockSpec((1, d), fix),
                  pl.BlockSpec((nm, 2 * d), fix), _wspec(wq_g), _wspec(wo_g), ANY_SPEC],
        out_specs=[pl.BlockSpec((tm, d), row)] * 4 + [pl.BlockSpec((nm, 2 * d), fix),
                                                      pl.BlockSpec((1, d), fix)],
        compiler_params=_params("arbitrary"),
    )(x, dxn, gain.reshape(1, d), kv, wq_g, wo_g, after)


def loss_head(x, target, gain, *, name, tm=None):
    t, d = x.shape
    tm = _row_tile(t, tm)

    def body(x_ref, t_ref, g_ref, dx_ref, dg_ref, loss_ref):
        @pl.when(pl.program_id(0) == 0)
        def _():
            dg_ref[...] = jnp.zeros_like(dg_ref)
            loss_ref[...] = jnp.zeros_like(loss_ref)
        g = g_ref[...]
        xhat, r = _rms_fwd(x_ref[...], None)
        err = xhat * g - t_ref[...]
        loss_ref[...] += 0.5 * jnp.sum(jnp.sum(err * err, axis=-1, keepdims=True) / d,
                                       axis=0, keepdims=True)
        dx, dg = _rms_bwd(xhat, r, g, err / d)
        dx_ref[...] = dx
        dg_ref[...] += dg

    row = lambda i: (i, 0)
    fix = lambda i: (0, 0)
    return pl.pallas_call(
        body, name=name, grid=(t // tm,),
        out_shape=[jax.ShapeDtypeStruct((t, d), F32), jax.ShapeDtypeStruct((1, d), F32),
                   jax.ShapeDtypeStruct((1, 1), F32)],
        in_specs=[pl.BlockSpec((tm, d), row), pl.BlockSpec((tm, d), row), pl.BlockSpec((1, d), fix)],
        out_specs=[pl.BlockSpec((tm, d), row), pl.BlockSpec((1, d), fix), pl.BlockSpec((1, 1), fix)],
        compiler_params=_params("arbitrary"),
    )(x, target, gain.reshape(1, d))


def _adamw_math(w, g, m, v):
    m = ADAM_B1 * m + (1.0 - ADAM_B1) * g
    v = ADAM_B2 * v + (1.0 - ADAM_B2) * (g * g)
    m_hat = m / (1.0 - ADAM_B1 ** ADAM_STEP)
    v_hat = v / (1.0 - ADAM_B2 ** ADAM_STEP)
    delta = -ADAM_LR * (m_hat / (jnp.sqrt(v_hat) + ADAM_EPS) + ADAM_WD * w)
    return delta, m, v


def adamw_sharded(own, lands, w, m, v, me_arr, *, name):
    nl, r, c = w.shape
    assert nl == len(own) == len(lands) == 2
    tr = next(cand for cand in (256, 176, 128, r) if r % cand == 0)
    nr = r // tr

    def body(me_ref, o0, o1, l0, l1, w_ref, m_ref, v_ref, g_out, d_out, m_out, v_out):
        def total(o_ref, l_ref):
            acc = o_ref[...].astype(F32)
            for p in range(N_DEV - 1):
                acc = acc + l_ref[p].astype(F32)
            return acc
        g = jnp.where(pl.program_id(0) == 0, total(o0, l0), total(o1, l1))
        delta, mn, vn = _adamw_math(w_ref[...], g, m_ref[...], v_ref[...])
        g_out[...] = g
        d_out[...] = delta
        m_out[...] = mn
        v_out[...] = vn

    row0 = lambda l, i: jnp.where(l == 0, i, nr - 1)
    row1 = lambda l, i: jnp.where(l == 1, i, 0)
    blk = pl.BlockSpec((None, tr, c), lambda l, i, me: (l, i, 0))
    grid_spec = pltpu.PrefetchScalarGridSpec(
        num_scalar_prefetch=1, grid=(nl, nr),
        in_specs=[pl.BlockSpec((None, tr, c), lambda l, i, me: (me[0], row0(l, i), 0)),
                  pl.BlockSpec((None, tr, c), lambda l, i, me: (me[0], row1(l, i), 0)),
                  pl.BlockSpec((N_DEV - 1, tr, c), lambda l, i, me: (0, row0(l, i), 0)),
                  pl.BlockSpec((N_DEV - 1, tr, c), lambda l, i, me: (0, row1(l, i), 0)),
                  blk, blk, blk],
        out_specs=[blk] * 4)
    return pl.pallas_call(
        body, name=name, grid_spec=grid_spec,
        out_shape=[jax.ShapeDtypeStruct((nl, r, c), F32)] * 4,
        compiler_params=_params("arbitrary", "arbitrary"),
    )(me_arr, own[0], own[1], lands[0], lands[1], w, m, v)


def adamw_many(gs, ws, ms, vs, *, name):
    n = len(ws)
    shapes = [w.shape for w in ws]
    as2d = lambda a: a.reshape(1, -1) if a.ndim == 1 else a

    def body(*refs):
        g_r, w_r, m_r, v_r = refs[:n], refs[n:2 * n], refs[2 * n:3 * n], refs[3 * n:4 * n]
        outs = refs[4 * n:]
        for i in range(n):
            delta, mn, vn = _adamw_math(w_r[i][...], g_r[i][...], m_r[i][...], v_r[i][...])
            outs[3 * i][...] = delta
            outs[3 * i + 1][...] = mn
            outs[3 * i + 2][...] = vn

    operands = [as2d(a) for group in (gs, ws, ms, vs) for a in group]
    out_shape = [jax.ShapeDtypeStruct(as2d(w).shape, F32) for w in ws for _ in range(3)]
    outs = pl.pallas_call(
        body, name=name, out_shape=out_shape,
        in_specs=[VMEM_SPEC] * (4 * n), out_specs=[VMEM_SPEC] * (3 * n),
        compiler_params=pltpu.CompilerParams(vmem_limit_bytes=VMEM_LIMIT),
    )(*operands)
    return [tuple(outs[3 * i + k].reshape(shapes[i]) for k in range(3)) for i in range(n)]


def cast_into_slot(a, layer, me_arr, *, name, dtype=None, after=None):
    dtype = BF16 if dtype is None else dtype
    _, r, c = a.shape
    tr = next(cand for cand in (256, 176, 128, r) if r % cand == 0)
    extra = [] if after is None else [after]

    def body(me_ref, a_ref, *rest):
        rest[-1][...] = a_ref[...].astype(dtype)

    grid_spec = pltpu.PrefetchScalarGridSpec(
        num_scalar_prefetch=1, grid=(r // tr,),
        in_specs=[pl.BlockSpec((None, tr, c), lambda i, me: (layer, i, 0))] + [ANY_SPEC] * len(extra),
        out_specs=pl.BlockSpec((None, tr, c), lambda i, me: (me[0], i, 0)))
    return pl.pallas_call(
        body, name=name, grid_spec=grid_spec,
        out_shape=jax.ShapeDtypeStruct((N_DEV, r, c), dtype),
        compiler_params=_params("parallel"),
    )(me_arr, a, *extra)


def _pack(arrs, rows):
    flat = jnp.concatenate([a.reshape(-1).astype(F32) for a in arrs])
    pad = rows * 128 - flat.shape[0]
    assert pad >= 0
    if pad:
        flat = jnp.concatenate([flat, jnp.zeros((pad,), F32)])
    return flat.reshape(rows, 128)


def _unpack(packed, shapes):
    flat = packed.reshape(-1)
    out, pos = [], 0
    for s in shapes:
        n = 1
        for dim in s:
            n *= dim
        out.append(flat[pos:pos + n].reshape(s))
        pos += n
    return out


def _rows_for(shapes):
    n = 0
    for s in shapes:
        k = 1
        for dim in s:
            k *= dim
        n += k
    return -(-n // 1024) * 8


GATHER_GROUPS = (("ffn1", ("ffn1_w_in", "ffn1_w_out")),
                 ("mid", ("mix_w_in", "mix_w_out", "xattn_wkv", "xattn_wq", "xattn_wo")),
                 ("ffn2", ("ffn2_w_in", "ffn2_w_out")))
SMALL_REPL = ["norm_ffn1", "norm_mix", "sgu_norm_g", "sgu_w", "sgu_b", "cconv_ln_g", "cconv_ln_b",
              "pool_w", "pool_scale", "norm_xattn", "norm_mem", "norm_ffn2", "norm_final"]
SMALL_SHARD = ["sconv_w", "cconv_w"]
TRANSPOSED = ("ffn1_w_in", "ffn2_w_in")
WEIGHTS = ["norm_ffn1", "ffn1_w_in", "ffn1_w_out", "norm_mix", "mix_w_in", "sconv_w", "sgu_norm_g",
           "sgu_w", "sgu_b", "cconv_w", "cconv_ln_g", "cconv_ln_b", "pool_w", "pool_scale", "mix_w_out",
           "norm_xattn", "norm_mem", "xattn_wq", "xattn_wkv", "xattn_wo", "norm_ffn2", "ffn2_w_in",
           "ffn2_w_out", "norm_final"]


def kernel(x, mem, norm_ffn1, ffn1_w_in, ffn1_w_out, norm_mix, mix_w_in, sconv_w, sgu_norm_g, sgu_w, sgu_b, cconv_w, cconv_ln_g, cconv_ln_b, pool_w, pool_scale, mix_w_out, norm_xattn, norm_mem, xattn_wq, xattn_wkv, xattn_wo, norm_ffn2, ffn2_w_in, ffn2_w_out, norm_final, loss_target, m_norm_ffn1, m_ffn1_w_in, m_ffn1_w_out, m_norm_mix, m_mix_w_in, m_sconv_w, m_sgu_norm_g, m_sgu_w, m_sgu_b, m_cconv_w, m_cconv_ln_g, m_cconv_ln_b, m_pool_w, m_pool_scale, m_mix_w_out, m_norm_xattn, m_norm_mem, m_xattn_wq, m_xattn_wkv, m_xattn_wo, m_norm_ffn2, m_ffn2_w_in, m_ffn2_w_out, m_norm_final, v_norm_ffn1, v_ffn1_w_in, v_ffn1_w_out, v_norm_mix, v_mix_w_in, v_sconv_w, v_sgu_norm_g, v_sgu_w, v_sgu_b, v_cconv_w, v_cconv_ln_g, v_cconv_ln_b, v_pool_w, v_pool_scale, v_mix_w_out, v_norm_xattn, v_norm_mem, v_xattn_wq, v_xattn_wkv, v_xattn_wo, v_norm_ffn2, v_ffn2_w_in, v_ffn2_w_out, v_norm_final):
    args = dict(locals())
    wts = {n: args[n] for n in WEIGHTS}
    mom = {n: args["m_" + n] for n in WEIGHTS}
    var = {n: args["v_" + n] for n in WEIGHTS}
    for n in TRANSPOSED:
        wts[n], mom[n], var[n] = (jnp.swapaxes(a, 1, 2) for a in (wts[n], mom[n], var[n]))
    x0 = x[0]
    mem0 = mem[0]
    target = loss_target[0]
    t, d = x0.shape
    nl = norm_ffn1.shape[0]
    w = MIX_W
    me = _my_index()

    me_arr = jnp.reshape(me, (1,)).astype(jnp.int32)

    small_g = all_gather([sconv_w, cconv_w], name="gather_conv_taps")
    sconv_full = jnp.transpose(small_g[0], (1, 2, 0, 3)).reshape(nl, SCONV_K, w)
    cconv_full = jnp.transpose(small_g[1], (1, 2, 0, 3)).reshape(nl, CCONV_K, w)
    pending = {}
    token = small_g[1]
    masks = GATHER_MASKS
    keys = [(gname, l, members) for l in range(nl) for gname, members in GATHER_GROUPS]
    first = [[cast_into_slot(wts[n], keys[0][1], me_arr, name=f"cast_{n}{keys[0][1]}") for n in keys[0][2]]]
    started, token = gather_start_groups(first, token, name="gather_start_first", masks=masks)
    casts = [[cast_into_slot(wts[n], l, me_arr, name=f"cast_{n}{l}", after=token) for n in members]
             for gname, l, members in keys[1:]]
    rest, token = gather_start_groups(casts, token, name="gather_start_rest", masks=masks)
    for (gname, l, members), (send, recv, gs) in zip(keys, started + rest):
        pending[gname, l] = (members, gs, send, recv, masks)
    wg = [dict() for _ in range(nl)]

    handing_over = {}

    def arrive_early(gname, l, after):
        members, gs, send, recv, masks = pending.pop((gname, l))
        gs = gather_wait(gs, send, recv, after, name=f"gather_wait_{gname}{l}", masks=masks)
        fsend, frecv, gs, _ = forward_start(gs, after, name=f"gather_forward_start_{gname}{l}")
        handing_over[gname, l] = (members, gs, fsend, frecv)

    def arrive(gname, l, after):
        if (gname, l) in handing_over:
            members, gs, fsend, frecv = handing_over.pop((gname, l))
            gs = forward_wait(gs, fsend, frecv, after, name=f"gather_forward_wait_{gname}{l}")
        else:
            members, gs, send, recv, masks = pending.pop((gname, l))
            gs = gather_wait(gs, send, recv, after, name=f"gather_wait_{gname}{l}", masks=masks)
            gs = sibling_forward(gs, name=f"gather_forward_{gname}{l}")
        wg[l].update(zip(members, gs))
    sconv_pad = jnp.pad(sconv_full, ((0, 0), (0, 8 - SCONV_K), (0, 0)))
    cconv_pad = jnp.pad(cconv_full, ((0, 0), (0, 32 - CCONV_K), (0, 0)))
    zeros_w = jnp.zeros((nl, w), F32)
    vecs = jnp.stack([sgu_norm_g, cconv_ln_g, cconv_ln_b, pool_scale] + [zeros_w] * 4, axis=1)
    wt = jnp.tril(sgu_w).astype(BF16)
    bexp = jnp.repeat(jnp.swapaxes(sgu_b, 1, 2), w // N_HEADS, axis=2)
    eye = jnp.eye(4, dtype=F32)
    pbd = jnp.einsum("lgcd,gh->lgchd", pool_w, eye).reshape(nl, w, w).astype(BF16)

    def mixer_args(l):
        return sconv_pad[l], cconv_pad[l], vecs[l], wt[l], bexp[l], pbd[l]

    saved = []
    xc = x0
    after = token
    for l in range(nl):
        s = {"x_ffn1": xc}
        arrive("ffn1", l, after)
        xc, s["gu_ffn1"] = ffn_fwd(xc, norm_ffn1[l], wg[l]["ffn1_w_in"], wg[l]["ffn1_w_out"],
                                   name=f"ffn1_fwd{l}", tm=FFN_FWD_TILE)
        s["x_mix"] = xc
        arrive("mid", l, xc)
        z = mm_rows(xc, wg[l]["mix_w_in"], "col", gain=norm_mix[l], name=f"mix_in{l}")
        y, xc = mixer_fwd(z, *mixer_args(l), xc, wg[l]["mix_w_out"], name=f"mixer_fwd{l}")
        s["z"], s["y"] = z, y
        s["x_att"] = xc
        kv = mm_rows(mem0, wg[l]["xattn_wkv"], "col", gain=norm_mem[l], name=f"kv{l}")
        s["kv"] = kv
        if l > 0:
            arrive_early("ffn2", l, kv)
        xc = xattn_fwd(xc, norm_xattn[l], kv, wg[l]["xattn_wq"], wg[l]["xattn_wo"], name=f"xattn_fwd{l}")
        s["x_ffn2"] = xc
        arrive("ffn2", l, xc)
        xc, s["gu_ffn2"] = ffn_fwd(xc, norm_ffn2[l], wg[l]["ffn2_w_in"], wg[l]["ffn2_w_out"],
                                   name=f"ffn2_fwd{l}", tm=FFN_FWD_TILE)
        after = xc
        saved.append(s)

    dx, g_norm_final, loss_local = loss_head(xc, target, norm_final, name="loss_head")

    tm = _row_tile(t, TN_TILE)
    small ={n: [None] * nl for n in SMALL_REPL + SMALL_SHARD if n != "norm_final"}
    scattered = {}
    tie = [token]

    def send_grads(gname, l, grads):
        members = list(grads)
        send, recv, gs, lands, tie[0] = scatter_start(
            [grads[n] for n in members], tie[0], name=f"scatter_start_{gname}{l}")
        scattered[gname, l] = (members, gs, lands, send, recv)

    names = SMALL_REPL + SMALL_SHARD + ["loss"]
    small_pending = []

    def start_small():
        small_full = {n: jnp.stack(v) for n, v in small.items()}
        small_full["norm_final"] = g_norm_final[0]
        small_full["loss"] = loss_local[0]
        shapes = [small_full[n].shape for n in names]
        packed = _pack([small_full[n] for n in names], _rows_for(shapes))
        slot = cast_into_slot(packed[None], 0, me_arr, name="small_into_slot", dtype=F32)
        send, recv, gs, tie[0] = gather_start([slot], tie[0], name="small_gather_start", masks=ALL_MASKS)
        small_pending.append((gs, send, recv, shapes))

    def ffn_backward(which, l, x_in, dy, gu, gain):
        w_in, w_out = wg[l][which + "_w_in"], wg[l][which + "_w_out"]
        dx_, h_, act, dgu, dgn, dyh = ffn_bwd_rows(x_in, dy, gu, gain, w_in, w_out, tie[0],
                                                   name=f"{which}_bwd{l}_rows")
        small["norm_" + which][l] = dgn[0]
        last = which == "ffn1" and l == 0
        if last:
            start_small()
        if last:
            g_out = ffn_grad_w_out(act, dyh, tie[0], name=f"{which}_bwd{l}_dwout")
            send_grads(which + "_out", l, {which + "_w_out": g_out})
            g_in = ffn_grad_w_in(h_, dgu, tie[0], name=f"{which}_bwd{l}_dwin")
            send_grads(which + "_in", l, {which + "_w_in": g_in})
        else:
            g_in = ffn_grad_w_in(h_, dgu, tie[0], name=f"{which}_bwd{l}_dwin")
            g_out = ffn_grad_w_out(act, dyh, tie[0], name=f"{which}_bwd{l}_dwout")
            send_grads(which, l, {which + "_w_in": g_in, which + "_w_out": g_out})
        return dx_

    for l in reversed(range(nl)):
        s = saved[l]
        wl = wg[l]
        dx = ffn_backward("ffn2", l, s["x_ffn2"], dx, s["gu_ffn2"], norm_ffn2[l])

        bg = {}
        dxn = dx
        dx, h, dq, o, dkv, dgn = xattn_bwd_rows(
            s["x_att"], dxn, norm_xattn[l], s["kv"], wl["xattn_wq"], wl["xattn_wo"], tie[0],
            name=f"xattn_bwd{l}")
        small["norm_xattn"][l] = dgn[0]
        row_spec = pl.BlockSpec((tm, d), lambda s_, i: (i, 0))
        bg["xattn_wq"] = mm_tn(h, dq, nb=1, ka=d, nbk=d, tm=tm, m=t, a_spec=row_spec, b_spec=row_spec,
                               name=f"dwq{l}").reshape(N_DEV, d // N_DEV, d)
        bg["xattn_wo"] = mm_tn(o, dxn, nb=1, ka=d, nbk=d, tm=tm, m=t, a_spec=row_spec, b_spec=row_spec,
                               name=f"dwo{l}").reshape(N_DEV, d // N_DEV, d)
        _, mhat, dgn = mm_nt(dkv, wl["xattn_wkv"], "col", x=mem0, gain=norm_mem[l], name=f"dmem{l}")
        small["norm_mem"][l] = dgn[0]
        nm = mem0.shape[0]
        bg["xattn_wkv"] = mm_tn(mhat, dkv, nb=N_DEV, ka=d, nbk=2 * d // N_DEV, tm=nm, m=nm,
                                a_spec=pl.BlockSpec((nm, d), lambda s_, i: (0, 0)),
                                b_spec=pl.BlockSpec((nm, 2 * d // N_DEV), lambda s_, i: (0, s_)),
                                name=f"dwkv{l}")
        send_grads("xattn", l, bg)

        bg = {}
        dxn = dx
        bg["mix_w_out"] = mm_tn(s["y"], dxn, nb=1, ka=d, nbk=d, tm=tm, m=t, a_spec=row_spec,
                                b_spec=row_spec, name=f"dwmo{l}").reshape(N_DEV, d // N_DEV, d)
        dz, gvec, gcc, gwt, gb, gpbd = mixer_bwd(s["z"], dxn, wl["mix_w_out"], *mixer_args(l),
                                                 name=f"mixer_bwd{l}")
        small["sconv_w"][l] = gvec[0:SCONV_K]
        small["sgu_norm_g"][l] = gvec[3]
        small["cconv_ln_g"][l] = gvec[4]
        small["cconv_ln_b"][l] = gvec[5]
        small["pool_scale"][l] = gvec[6]
        small["cconv_w"][l] = gcc[0:CCONV_K]
        small["sgu_w"][l] = gwt
        small["sgu_b"][l] = jnp.transpose(gb[:, 0:N_HEADS])
        gw = w // 4
        small["pool_w"][l] = jnp.stack([gpbd[g * gw:(g + 1) * gw, g * gw:(g + 1) * gw] for g in range(4)])
        dx, h, dgn = mm_nt(dz, wl["mix_w_in"], "col", x=s["x_mix"], gain=norm_mix[l], dx_in=dxn,
                           after=tie[0], name=f"dh_mix{l}")
        small["norm_mix"][l] = dgn[0]
        th = _row_tile(t, TN_TILE // 2)
        bg["mix_w_in"] = mm_tn(h, dz, nb=1, ka=d, nbk=N_DEV * w, tm=th, m=t, col_slots=N_DEV,
                               a_spec=pl.BlockSpec((th, d), lambda s_, i: (i, 0)),
                               b_spec=pl.BlockSpec((th, N_DEV * w), lambda s_, i: (i, 0)), name=f"dwmi{l}")
        send_grads("mix", l, bg)

        dx = ffn_backward("ffn1", l, s["x_ffn1"], dx, s["gu_ffn1"], norm_ffn1[l])

    out = {}
    own, land = {}, {}

    def collect(keys, after):
        for gname, l in keys:
            members, gs, lands, send, recv = scattered.pop((gname, l))
            gs, lands = scatter_wait(gs, lands, send, recv, after, name=f"scatter_wait_{gname}{l}")
            for n, g_, l_ in zip(members, gs, lands):
                own.setdefault(n, {})[l] = g_
                land.setdefault(n, {})[l] = l_

    def update(ns, after):
        for n in ns:
            out[n] = adamw_sharded([own[n][l] for l in range(nl)], [land[n][l] for l in range(nl)],
                                   wts[n], mom[n], var[n], me_arr, name="adamw_" + n)
            after = out[n][1]
        return after

    after = tie[0]
    for gname in ("ffn2", "xattn", "mix"):
        collect([(gname, l) for l in reversed(range(nl))], after)
        after = update([n for n in own if n not in out], after)
    (gs, send, recv, shapes), = small_pending
    gs = gather_wait(gs, send, recv, after, name="small_gather_wait", masks=ALL_MASKS)
    summed = sum_slots(gs[0], name="small_sum")
    gsm = dict(zip(names, _unpack(summed, shapes)))
    loss = gsm["loss"][0]
    cs = w // N_DEV
    for n in SMALL_SHARD:
        gsm[n] = lax.dynamic_slice_in_dim(gsm[n], me * cs, cs, axis=2)
    small_names = SMALL_REPL + SMALL_SHARD
    upd = adamw_many([gsm[n] for n in small_names], [wts[n] for n in small_names],
                     [mom[n] for n in small_names], [var[n] for n in small_names], name="adamw_small")
    for n, (a, b, c) in zip(small_names, upd):
        out[n] = (gsm[n], a, b, c)
    after = upd[0][0]
    collect([("ffn1", l) for l in reversed(range(1, nl))] + [("ffn1_out", 0)], after)
    after = update(["ffn1_w_out"], after)
    collect([("ffn1_in", 0)], after)
    update(["ffn1_w_in"], after)
    for n in TRANSPOSED:
        out[n] = tuple(jnp.swapaxes(a, 1, 2) for a in out[n])

    grad_x = dx.reshape(1, t, d)
    return (loss, grad_x, *[out[n][0] for n in WEIGHTS], *[out[n][1] for n in WEIGHTS],
            *[out[n][2] for n in WEIGHTS], *[out[n][3] for n in WEIGHTS])
```

```python
import jax
import jax.numpy as jnp
from jax import lax
from jax.experimental import pallas as pl
from jax.experimental.pallas import tpu as pltpu

F32 = jnp.float32
BF16 = jnp.bfloat16
MESH = pl.DeviceIdType.MESH
N_DEV = 8
EPS = 1e-6
HALO = 32
SGU_CHUNK = 128
CCONV_K = 31
SCONV_K = 3
MIX_W = 256
N_HEADS = 4
VMEM_LIMIT = 56 * 1024 * 1024
ROW_TILE = 512
TN_TILE = 2048
FFN_FWD_TILE = 1024
FFN_BWD_SPLIT = 2
ROW_BLOCKS = (256, 176, 128)
MIX_TILE = 512

ADAM_LR = 0.001
ADAM_B1 = 0.9
ADAM_B2 = 0.999
ADAM_EPS = 1e-08
ADAM_WD = 0.01
ADAM_STEP = 10

HBM_SPEC = pl.BlockSpec(memory_space=pltpu.HBM)
VMEM_SPEC = pl.BlockSpec(memory_space=pltpu.VMEM)


def _params(*sem):
    return pltpu.CompilerParams(dimension_semantics=tuple(sem), vmem_limit_bytes=VMEM_LIMIT)


def _row_tile(m, pref=None):
    t = min(m, ROW_TILE if pref is None else pref)
    assert m % t == 0, (m, t)
    return t


def _my_index():
    return lax.axis_index("x") * 4 + lax.axis_index("y") * 2 + lax.axis_index("c")


def _peer(mask):
    x, y, c = lax.axis_index("x"), lax.axis_index("y"), lax.axis_index("c")
    px = 1 - x if mask & 4 else x
    py = 1 - y if mask & 2 else y
    pc = 1 - c if mask & 1 else c
    return (px, py, pc), px * 4 + py * 2 + pc


def all_gather(arrs, name):
    n = len(arrs)

    def body(*refs):
        ins, outs = refs[:n], refs[n:2 * n]
        send_sems, recv_sems, loc_sems = refs[2 * n:]
        me = _my_index()
        local = []
        for i in range(n):
            cp = pltpu.make_async_copy(ins[i], outs[i].at[me], loc_sems.at[i])
            cp.start()
            local.append(cp)
        sends = []
        for i in range(n):
            for m in range(1, N_DEV):
                peer, _ = _peer(m)
                cp = pltpu.make_async_remote_copy(
                    src_ref=ins[i], dst_ref=outs[i].at[me],
                    send_sem=send_sems.at[i, m - 1], recv_sem=recv_sems.at[i, m - 1],
                    device_id=peer, device_id_type=MESH)
                cp.start()
                sends.append(cp)
        for i in range(n):
            for m in range(1, N_DEV):
                peer, pidx = _peer(m)
                pltpu.make_async_remote_copy(
                    src_ref=ins[i], dst_ref=outs[i].at[pidx],
                    send_sem=send_sems.at[i, m - 1], recv_sem=recv_sems.at[i, m - 1],
                    device_id=peer, device_id_type=MESH).wait_recv()
        for cp in sends:
            cp.wait_send()
        for cp in local:
            cp.wait()

    return pl.pallas_call(
        body, name=name,
        out_shape=[jax.ShapeDtypeStruct((N_DEV,) + a.shape, a.dtype) for a in arrs],
        in_specs=[HBM_SPEC] * n, out_specs=[HBM_SPEC] * n,
        scratch_shapes=[pltpu.SemaphoreType.DMA((n, N_DEV - 1)),
                        pltpu.SemaphoreType.DMA((n, N_DEV - 1)),
                        pltpu.SemaphoreType.DMA((n,))],
    )(*arrs)


SEM_SPEC = pl.BlockSpec(memory_space=pltpu.SEMAPHORE)
ANY_SPEC = pl.BlockSpec(memory_space=pl.ANY)
SIDE_EFFECT = pltpu.SideEffectType.DATAFLOW_SIDE_EFFECTING


def _hbm(a):
    return pltpu.with_memory_space_constraint(a, pltpu.HBM)


def _sem_pairs(n):
    return (pltpu.SemaphoreType.DMA((n * (N_DEV - 1),)), pltpu.SemaphoreType.DMA((n * (N_DEV - 1),)))


def _sem(i, m):
    return i * (N_DEV - 1) + m - 1


def _gather_copy(g_ref, i, m, send_sems, recv_sems, origin):
    peer, _ = _peer(m)
    return pltpu.make_async_remote_copy(
        src_ref=g_ref.at[origin], dst_ref=g_ref.at[origin],
        send_sem=send_sems.at[_sem(i, m)], recv_sem=recv_sems.at[_sem(i, m)],
        device_id=peer, device_id_type=MESH)


GATHER_MASKS = (1, 2, 4, 6)
FORWARD_MASKS = (2, 4, 6)


ALL_MASKS = tuple(range(1, N_DEV))


def gather_start(gs, after, name, masks=GATHER_MASKS):
    n = len(gs)

    def body(*refs):
        g_in = refs[:n]
        send_sems, recv_sems = refs[n + 1], refs[n + 2]
        token = refs[-1]
        me = _my_index()
        for i in range(n):
            for m in masks:
                _gather_copy(g_in[i], i, m, send_sems, recv_sems, me).start()
        token[...] = jnp.zeros_like(token)

    outs = pl.pallas_call(
        body, name=name,
        out_shape=(*_sem_pairs(n), *[pltpu.HBM(g.shape, g.dtype) for g in gs],
                   jax.ShapeDtypeStruct((8, 128), F32)),
        in_specs=[HBM_SPEC] * n + [ANY_SPEC],
        out_specs=(SEM_SPEC, SEM_SPEC, *[HBM_SPEC] * n, VMEM_SPEC),
        input_output_aliases={i: 2 + i for i in range(n)},
        compiler_params=pltpu.CompilerParams(has_side_effects=SIDE_EFFECT),
    )(*[_hbm(g) for g in gs], after)
    return outs[0], outs[1], list(outs[2:2 + n]), outs[-1]


def gather_start_groups(groups, after, name, masks=GATHER_MASKS):
    sizes = [len(g) for g in groups]
    flat = [a for g in groups for a in g]
    n, ng = len(flat), len(groups)

    def body(*refs):
        g_in = refs[:n]
        sems = refs[n + 1:n + 1 + 2 * ng]
        token = refs[-1]
        me = _my_index()
        pos = 0
        for k, size in enumerate(sizes):
            for i in range(size):
                for m in masks:
                    _gather_copy(g_in[pos + i], i, m, sems[2 * k], sems[2 * k + 1], me).start()
            pos += size
        token[...] = jnp.zeros_like(token)

    outs = pl.pallas_call(
        body, name=name,
        out_shape=(*[s for size in sizes for s in _sem_pairs(size)],
                   *[pltpu.HBM(g.shape, g.dtype) for g in flat], jax.ShapeDtypeStruct((8, 128), F32)),
        in_specs=[HBM_SPEC] * n + [ANY_SPEC],
        out_specs=(*[SEM_SPEC] * (2 * ng), *[HBM_SPEC] * n, VMEM_SPEC),
        input_output_aliases={i: 2 * ng + i for i in range(n)},
        compiler_params=pltpu.CompilerParams(has_side_effects=SIDE_EFFECT),
    )(*[_hbm(g) for g in flat], after)
    result, pos = [], 2 * ng
    for k, size in enumerate(sizes):
        result.append((outs[2 * k], outs[2 * k + 1], list(outs[pos:pos + size])))
        pos += size
    return result, outs[-1]


def gather_wait(gs, send_sems, recv_sems, after, name, masks=GATHER_MASKS):
    n = len(gs)

    def body(*refs):
        g_in = refs[:n]
        send, recv = refs[n], refs[n + 1]
        me = _my_index()
        for i in range(n):
            for m in masks:
                _, pidx = _peer(m)
                _gather_copy(g_in[i], i, m, send, recv, me).wait_send()
                _gather_copy(g_in[i], i, m, send, recv, pidx).wait_recv()

    outs = pl.pallas_call(
        body, name=name,
        out_shape=[pltpu.HBM(g.shape, g.dtype) for g in gs],
        in_specs=[HBM_SPEC] * n + [SEM_SPEC, SEM_SPEC, ANY_SPEC],
        out_specs=[HBM_SPEC] * n,
        input_output_aliases={i: i for i in range(n)},
        compiler_params=pltpu.CompilerParams(has_side_effects=SIDE_EFFECT),
    )(*gs, send_sems, recv_sems, after)
    return list(outs)


def sibling_forward(gs, name):
    n = len(gs)
    nf = len(FORWARD_MASKS)

    def body(*refs):
        g_in = refs[:n]
        send_sems, recv_sems = refs[2 * n:]
        x, y, c = lax.axis_index("x"), lax.axis_index("y"), lax.axis_index("c")
        sibling = (x, y, 1 - c)

        def copy(i, k, origin):
            return pltpu.make_async_remote_copy(
                src_ref=g_in[i].at[origin], dst_ref=g_in[i].at[origin],
                send_sem=send_sems.at[i * nf + k], recv_sem=recv_sems.at[i * nf + k],
                device_id=sibling, device_id_type=MESH)
        sends = []
        for i in range(n):
            for k, m in enumerate(FORWARD_MASKS):
                _, origin = _peer(m)
                cp = copy(i, k, origin)
                cp.start()
                sends.append(cp)
        for i in range(n):
            for k, m in enumerate(FORWARD_MASKS):
                _, origin = _peer(m ^ 1)
                copy(i, k, origin).wait_recv()
        for cp in sends:
            cp.wait_send()

    outs = pl.pallas_call(
        body, name=name,
        out_shape=[jax.ShapeDtypeStruct(g.shape, g.dtype) for g in gs],
        in_specs=[HBM_SPEC] * n, out_specs=[HBM_SPEC] * n,
        input_output_aliases={i: i for i in range(n)},
        scratch_shapes=[pltpu.SemaphoreType.DMA((n * nf,)), pltpu.SemaphoreType.DMA((n * nf,))],
    )(*gs)
    return list(outs)


def _forward_copy(g_ref, i, k, send_sems, recv_sems, origin):
    sibling = (lax.axis_index("x"), lax.axis_index("y"), 1 - lax.axis_index("c"))
    slot = i * len(FORWARD_MASKS) + k
    return pltpu.make_async_remote_copy(
        src_ref=g_ref.at[origin], dst_ref=g_ref.at[origin],
        send_sem=send_sems.at[slot], recv_sem=recv_sems.at[slot],
        device_id=sibling, device_id_type=MESH)


def forward_start(gs, after, name):
    n = len(gs)
    nsem = n * len(FORWARD_MASKS)

    def body(*refs):
        g_in = refs[:n]
        send_sems, recv_sems = refs[n + 1], refs[n + 2]
        token = refs[-1]
        for i in range(n):
            for k, m in enumerate(FORWARD_MASKS):
                _, origin = _peer(m)
                _forward_copy(g_in[i], i, k, send_sems, recv_sems, origin).start()
        token[...] = jnp.zeros_like(token)

    outs = pl.pallas_call(
        body, name=name,
        out_shape=(pltpu.SemaphoreType.DMA((nsem,)), pltpu.SemaphoreType.DMA((nsem,)),
                   *[pltpu.HBM(g.shape, g.dtype) for g in gs], jax.ShapeDtypeStruct((8, 128), F32)),
        in_specs=[HBM_SPEC] * n + [ANY_SPEC],
        out_specs=(SEM_SPEC, SEM_SPEC, *[HBM_SPEC] * n, VMEM_SPEC),
        input_output_aliases={i: 2 + i for i in range(n)},
        compiler_params=pltpu.CompilerParams(has_side_effects=SIDE_EFFECT),
    )(*[_hbm(g) for g in gs], after)
    return outs[0], outs[1], list(outs[2:2 + n]), outs[-1]


def forward_wait(gs, send_sems, recv_sems, after, name):
    n = len(gs)

    def body(*refs):
        g_in = refs[:n]
        send, recv = refs[n], refs[n + 1]
        for i in range(n):
            for k, m in enumerate(FORWARD_MASKS):
                _, mine = _peer(m)
                _, theirs = _peer(m ^ 1)
                _forward_copy(g_in[i], i, k, send, recv, mine).wait_send()
                _forward_copy(g_in[i], i, k, send, recv, theirs).wait_recv()

    outs = pl.pallas_call(
        body, name=name,
        out_shape=[pltpu.HBM(g.shape, g.dtype) for g in gs],
        in_specs=[HBM_SPEC] * n + [SEM_SPEC, SEM_SPEC, ANY_SPEC],
        out_specs=[HBM_SPEC] * n,
        input_output_aliases={i: i for i in range(n)},
        compiler_params=pltpu.CompilerParams(has_side_effects=SIDE_EFFECT),
    )(*gs, send_sems, recv_sems, after)
    return list(outs)


def _scatter_copy(g_ref, l_ref, i, m, send_sems, recv_sems):
    peer, pidx = _peer(m)
    return pltpu.make_async_remote_copy(
        src_ref=g_ref.at[pidx], dst_ref=l_ref.at[m - 1],
        send_sem=send_sems.at[_sem(i, m)], recv_sem=recv_sems.at[_sem(i, m)],
        device_id=peer, device_id_type=MESH)


def scatter_start(grads, after, name):
    n = len(grads)
    lands = [lax.empty((N_DEV - 1,) + g.shape[1:], g.dtype) for g in grads]

    def body(*refs):
        g_in, l_in = refs[:n], refs[n:2 * n]
        send_sems, recv_sems = refs[2 * n + 1], refs[2 * n + 2]
        token = refs[-1]
        for i in range(n):
            for m in range(1, N_DEV):
                _scatter_copy(g_in[i], l_in[i], i, m, send_sems, recv_sems).start()
        token[...] = jnp.zeros_like(token)

    outs = pl.pallas_call(
        body, name=name,
        out_shape=(*_sem_pairs(n), *[pltpu.HBM(g.shape, g.dtype) for g in grads],
                   *[pltpu.HBM(l.shape, l.dtype) for l in lands], jax.ShapeDtypeStruct((8, 128), F32)),
        in_specs=[HBM_SPEC] * (2 * n) + [ANY_SPEC],
        out_specs=(SEM_SPEC, SEM_SPEC, *[HBM_SPEC] * (2 * n), VMEM_SPEC),
        input_output_aliases={i: 2 + i for i in range(2 * n)},
        compiler_params=pltpu.CompilerParams(has_side_effects=SIDE_EFFECT),
    )(*[_hbm(g) for g in grads], *[_hbm(l) for l in lands], after)
    return outs[0], outs[1], list(outs[2:2 + n]), list(outs[2 + n:2 + 2 * n]), outs[-1]


def scatter_wait(grads, lands, send_sems, recv_sems, after, name):
    n = len(grads)

    def body(*refs):
        g_in, l_in = refs[:n], refs[n:2 * n]
        send, recv = refs[2 * n], refs[2 * n + 1]
        for i in range(n):
            for m in range(1, N_DEV):
                cp = _scatter_copy(g_in[i], l_in[i], i, m, send, recv)
                cp.wait_send()
                cp.wait_recv()

    outs = pl.pallas_call(
        body, name=name,
        out_shape=[pltpu.HBM(a.shape, a.dtype) for a in list(grads) + list(lands)],
        in_specs=[HBM_SPEC] * (2 * n) + [SEM_SPEC, SEM_SPEC, ANY_SPEC],
        out_specs=[HBM_SPEC] * (2 * n),
        input_output_aliases={i: i for i in range(2 * n)},
        compiler_params=pltpu.CompilerParams(has_side_effects=SIDE_EFFECT),
    )(*grads, *lands, send_sems, recv_sems, after)
    return list(outs[:n]), list(outs[n:])


def sum_slots(g, name):
    _, r, c = g.shape

    def body(g_ref, out_ref):
        acc = g_ref[0]
        for p in range(1, N_DEV):
            acc = acc + g_ref[p]
        out_ref[...] = acc

    return pl.pallas_call(
        body, name=name, out_shape=jax.ShapeDtypeStruct((r, c), F32),
        in_specs=[VMEM_SPEC], out_specs=VMEM_SPEC,
        compiler_params=pltpu.CompilerParams(vmem_limit_bytes=VMEM_LIMIT),
    )(g)


def _sigmoid(v):
    return 1.0 / (1.0 + jnp.exp(-v))


def _rms_fwd(xf, g):
    r = lax.rsqrt(jnp.mean(xf * xf, axis=-1, keepdims=True) + EPS)
    return xf * r, r


def _rms_bwd(xhat, r, g, dy):
    dg = jnp.sum(dy * xhat, axis=0, keepdims=True)
    dxh = dy * g
    dx = r * (dxh - xhat * jnp.mean(dxh * xhat, axis=-1, keepdims=True))
    return dx, dg


def _ln_stats(v):
    mu = jnp.mean(v, axis=-1, keepdims=True)
    vc = v - mu
    r = lax.rsqrt(jnp.mean(vc * vc, axis=-1, keepdims=True) + EPS)
    return vc * r, r


def _ln_bwd(xhat, r, dxh):
    return r * (dxh - jnp.mean(dxh, axis=-1, keepdims=True)
                - xhat * jnp.mean(dxh * xhat, axis=-1, keepdims=True))


def _dot(a, b):
    return jnp.dot(a, b, preferred_element_type=F32)


def _dot_nt(a, b):
    return lax.dot_general(a, b, (((1,), (1,)), ((), ())), preferred_element_type=F32)


def _dot_tn(a, b):
    return lax.dot_general(a, b, (((0,), (0,)), ((), ())), preferred_element_type=F32)


def _full_weight(w_ref, kind):
    assert kind == "row"
    p, a, b = w_ref.shape
    return w_ref[...].reshape(p * a, b)


def _wspec(wg):
    return pl.BlockSpec(wg.shape, lambda *_: (0, 0, 0))


def mm_rows(a, wg, kind, *, gain=None, out_dtype=F32, name, tm=None):
    m, k = a.shape
    p, wa, wb = wg.shape
    n = p * wb if kind == "col" else wb
    tm = _row_tile(m, tm)
    has_gain = gain is not None

    def body(*refs):
        refs = list(refs)
        a_ref = refs.pop(0)
        g_ref = refs.pop(0) if has_gain else None
        w_ref = refs.pop(0)
        o_ref = refs.pop(0)
        if has_gain:
            xhat, _ = _rms_fwd(a_ref[...].astype(F32), None)
            h = (xhat * g_ref[...]).astype(BF16)
        else:
            h = a_ref[...].astype(BF16)
        if kind == "col":
            for j in range(p):
                o_ref[:, j * wb:(j + 1) * wb] = _dot(h, w_ref[j]).astype(out_dtype)
        else:
            o_ref[...] = _dot(h, _full_weight(w_ref, "row")).astype(out_dtype)

    operands = [a]
    in_specs = [pl.BlockSpec((tm, k), lambda i: (i, 0))]
    if has_gain:
        operands.append(gain.reshape(1, k))
        in_specs.append(pl.BlockSpec((1, k), lambda i: (0, 0)))
    operands.append(wg)
    in_specs.append(_wspec(wg))
    return pl.pallas_call(
        body, name=name, grid=(m // tm,),
        out_shape=jax.ShapeDtypeStruct((m, n), out_dtype),
        in_specs=in_specs, out_specs=pl.BlockSpec((tm, n), lambda i: (i, 0)),
        compiler_params=_params("parallel"),
    )(*operands)


def mm_nt(dz, wg, kind, *, x=None, gain=None, dx_in=None, after=None, name, tm=None):
    m, n = dz.shape
    p, wa, wb = wg.shape
    k = wa if kind == "col" else p * wa
    tm = _row_tile(m, tm)
    epi = x is not None
    has_dx = dx_in is not None
    has_after = after is not None

    def body(*refs):
        refs = list(refs)
        dz_ref, w_ref = refs.pop(0), refs.pop(0)
        if epi:
            x_ref, g_ref = refs.pop(0), refs.pop(0)
            dxi_ref = refs.pop(0) if has_dx else None
        if has_after:
            refs.pop(0)
        if epi:
            dx_ref, h_ref, dg_ref = refs
        else:
            (da_ref,) = refs
        dzb = dz_ref[...].astype(BF16)
        if kind == "col":
            da = _dot_nt(dzb[:, 0:wb], w_ref[0])
            for j in range(1, p):
                da = da + _dot_nt(dzb[:, j * wb:(j + 1) * wb], w_ref[j])
        else:
            da = _dot_nt(dzb, _full_weight(w_ref, "row"))
        if not epi:
            da_ref[...] = da
            return
        g = g_ref[...]
        xhat, r = _rms_fwd(x_ref[...].astype(F32), None)
        h_ref[...] = (xhat * g).astype(BF16)
        dx, dg = _rms_bwd(xhat, r, g, da)
        if has_dx:
            dx = dx + dxi_ref[...]
        dx_ref[...] = dx

        @pl.when(pl.program_id(0) == 0)
        def _():
            dg_ref[...] = jnp.zeros_like(dg_ref)
        dg_ref[...] += dg

    row = lambda i: (i, 0)
    operands = [dz, wg]
    in_specs = [pl.BlockSpec((tm, n), row), _wspec(wg)]
    if epi:
        operands += [x, gain.reshape(1, k)]
        in_specs += [pl.BlockSpec((tm, k), row), pl.BlockSpec((1, k), lambda i: (0, 0))]
        if has_dx:
            operands.append(dx_in)
            in_specs.append(pl.BlockSpec((tm, k), row))
        out_shape = [jax.ShapeDtypeStruct((m, k), F32), jax.ShapeDtypeStruct((m, k), BF16),
                     jax.ShapeDtypeStruct((1, k), F32)]
        out_specs = [pl.BlockSpec((tm, k), row), pl.BlockSpec((tm, k), row),
                     pl.BlockSpec((1, k), lambda i: (0, 0))]
    else:
        out_shape = jax.ShapeDtypeStruct((m, k), F32)
        out_specs = pl.BlockSpec((tm, k), row)
    if has_after:
        operands.append(after)
        in_specs.append(ANY_SPEC)
    return pl.pallas_call(
        body, name=name, grid=(m // tm,), out_shape=out_shape,
        in_specs=in_specs, out_specs=out_specs,
        compiler_params=_params("arbitrary"),
    )(*operands)


def mm_tn(a, b, *, nb, a_spec, b_spec, ka, nbk, tm, m, scale=1.0, out_dtype=BF16, col_slots=1,
          after=None, name):
    ni = m // tm
    assert col_slots == 1 or nb == 1
    cw = nbk // col_slots
    extra = [] if after is None else [after]

    def body(a_ref, b_ref, *rest):
        o_ref, acc = rest[len(extra):]
        i = pl.program_id(1)

        @pl.when(i == 0)
        def _():
            acc[...] = jnp.zeros_like(acc)
        acc[...] += _dot_tn(a_ref[...].astype(BF16), b_ref[...].astype(BF16))

        @pl.when(i == ni - 1)
        def _():
            if col_slots == 1:
                o_ref[...] = (acc[...] * scale).astype(out_dtype)
            else:
                for j in range(col_slots):
                    o_ref[j] = (acc[:, j * cw:(j + 1) * cw] * scale).astype(out_dtype)

    if col_slots == 1:
        out_shape = jax.ShapeDtypeStruct((nb, ka, nbk), out_dtype)
        out_spec = pl.BlockSpec((None, ka, nbk), lambda s, i: (s, 0, 0))
    else:
        out_shape = jax.ShapeDtypeStruct((col_slots, ka, cw), out_dtype)
        out_spec = pl.BlockSpec((col_slots, ka, cw), lambda s, i: (0, 0, 0))
    return pl.pallas_call(
        body, name=name, grid=(nb, ni), out_shape=out_shape,
        in_specs=[a_spec, b_spec] + [ANY_SPEC] * len(extra), out_specs=out_spec,
        scratch_shapes=[pltpu.VMEM((ka, nbk), F32)],
        compiler_params=_params("parallel", "arbitrary"),
    )(a, b, *extra)


def _ffn_specs(w_in_g, w_out_g, d):
    nf = w_in_g.shape[1]
    hr = w_out_g.shape[1]
    assert 2 * hr == nf
    w_in5 = w_in_g.reshape(2, 4, nf, d)
    w_out5 = w_out_g.reshape(4, 2, hr, d)
    in_spec = pl.BlockSpec((2, None, nf, d), lambda i, j: (0, j, 0, 0))
    out_spec = pl.BlockSpec((None, 2, hr, d), lambda i, j: (j, 0, 0, 0))
    return w_in5, w_out5, in_spec, out_spec, nf


def ffn_fwd(x, gain, w_in_g, w_out_g, *, name, tm=None):
    t, d = x.shape
    tm = _row_tile(t, tm)
    w_in5, w_out5, wi_spec, wo_spec, nf = _ffn_specs(w_in_g, w_out_g, d)

    def body(x_ref, g_ref, wi_ref, wo_ref, o_ref, gu_ref, h_scr, acc):
        j = pl.program_id(1)

        @pl.when(j == 0)
        def _():
            xhat, _ = _rms_fwd(x_ref[...], None)
            h_scr[...] = (xhat * g_ref[...]).astype(BF16)
            acc[...] = jnp.zeros_like(acc)
        h = h_scr[...]
        gt = _dot_nt(h, wi_ref[0])
        up = _dot_nt(h, wi_ref[1])
        gu_ref[0] = gt.astype(BF16)
        gu_ref[1] = up.astype(BF16)
        act = (gt * _sigmoid(gt) * up).astype(BF16)
        acc[...] += _dot(act, wo_ref[...].reshape(nf, d))

        @pl.when(j == 3)
        def _():
            o_ref[...] = x_ref[...] + 0.5 * acc[...]

    return pl.pallas_call(
        body, name=name, grid=(t // tm, 4),
        out_shape=[jax.ShapeDtypeStruct((t, d), F32), jax.ShapeDtypeStruct((2, 4, t, nf), BF16)],
        in_specs=[pl.BlockSpec((tm, d), lambda i, j: (i, 0)),
                  pl.BlockSpec((1, d), lambda i, j: (0, 0)), wi_spec, wo_spec],
        out_specs=[pl.BlockSpec((tm, d), lambda i, j: (i, 0)),
                   pl.BlockSpec((2, None, tm, nf), lambda i, j: (0, j, i, 0))],
        scratch_shapes=[pltpu.VMEM((tm, d), BF16), pltpu.VMEM((tm, d), F32)],
        compiler_params=_params("parallel", "arbitrary"),
    )(x, gain.reshape(1, d), w_in5, w_out5)


def ffn_bwd_rows(x, dy, gu, gain, w_in_g, w_out_g, after, *, name, tm=None):
    t, d = x.shape
    tm = _row_tile(t, tm)
    w_in5, w_out5, wi_spec, wo_spec, nf = _ffn_specs(w_in_g, w_out_g, d)

    def body(x_ref, dy_ref, gu_ref, g_ref, wi_ref, wo_ref, after_ref, dx_ref, h_ref, act_ref, dgu_ref, dg_ref,
             dyh_scr, dh_acc):
        i, j = pl.program_id(0), pl.program_id(1)

        @pl.when(j == 0)
        def _():
            xhat, _ = _rms_fwd(x_ref[...], None)
            h_ref[...] = (xhat * g_ref[...]).astype(BF16)
            dyh_scr[...] = (0.5 * dy_ref[...]).astype(BF16)
            dh_acc[...] = jnp.zeros_like(dh_acc)
        wo = wo_ref[...].reshape(nf, d)

        def gates(rows):
            gt = gu_ref[0, rows].astype(F32)
            up = gu_ref[1, rows].astype(F32)
            sg = _sigmoid(gt)
            silu = gt * sg
            act_ref[rows] = (silu * up).astype(BF16)
            return up * (sg * (1.0 + gt * (1.0 - sg))), silu

        def grads(rows, dact, dsilu_up, silu):
            dgt = (dact * dsilu_up).astype(BF16)
            dup = (dact * silu).astype(BF16)
            dgu_ref[0, rows] = dgt
            dgu_ref[1, rows] = dup
            return dgt, dup

        sub = tm // FFN_BWD_SPLIT
        parts = [slice(k * sub, (k + 1) * sub) for k in range(FFN_BWD_SPLIT)]
        dact = _dot_nt(dyh_scr[parts[0]], wo)
        gate = gates(parts[0])
        for k, rows in enumerate(parts):
            if k + 1 < len(parts):
                dact_next = _dot_nt(dyh_scr[parts[k + 1]], wo)
            dgt, dup = grads(rows, dact, *gate)
            dh_acc[rows] += _dot(dgt, wi_ref[0]) + _dot(dup, wi_ref[1])
            if k + 1 < len(parts):
                gate = gates(parts[k + 1])
                dact = dact_next

        @pl.when(j == 3)
        def _():
            g = g_ref[...]
            xhat, r = _rms_fwd(x_ref[...], None)
            dx, dg = _rms_bwd(xhat, r, g, dh_acc[...])
            dx_ref[...] = dy_ref[...] + dx

            @pl.when(i == 0)
            def _():
                dg_ref[...] = jnp.zeros_like(dg_ref)
            dg_ref[...] += dg

    row = lambda i, j: (i, 0)
    return pl.pallas_call(
        body, name=name, grid=(t // tm, 4),
        out_shape=[jax.ShapeDtypeStruct((t, d), F32), jax.ShapeDtypeStruct((t, d), BF16),
                   jax.ShapeDtypeStruct((4, t, nf), BF16), jax.ShapeDtypeStruct((2, 4, t, nf), BF16),
                   jax.ShapeDtypeStruct((1, d), F32), jax.ShapeDtypeStruct((t, d), BF16)],
        in_specs=[pl.BlockSpec((tm, d), row), pl.BlockSpec((tm, d), row),
                  pl.BlockSpec((2, None, tm, nf), lambda i, j: (0, j, i, 0)),
                  pl.BlockSpec((1, d), lambda i, j: (0, 0)), wi_spec, wo_spec, ANY_SPEC],
        out_specs=[pl.BlockSpec((tm, d), row), pl.BlockSpec((tm, d), row),
                   pl.BlockSpec((None, tm, nf), lambda i, j: (j, i, 0)),
                   pl.BlockSpec((2, None, tm, nf), lambda i, j: (0, j, i, 0)),
                   pl.BlockSpec((1, d), lambda i, j: (0, 0)), pl.BlockSpec((tm, d), row)],
        scratch_shapes=[pltpu.VMEM((tm, d), F32)],
        compiler_params=_params("arbitrary", "arbitrary"),
    )(x, dy, gu, gain.reshape(1, d), w_in5, w_out5, after)


def ffn_grad_w_in(h, dgu, after, *, name):
    t, d = h.shape
    nf = dgu.shape[-1]
    tm = _row_tile(t, TN_TILE)
    return mm_tn(dgu.reshape(8, t, nf), h, nb=8, ka=nf, nbk=d, tm=tm, m=t, after=after,
                 a_spec=pl.BlockSpec((None, tm, nf), lambda s, i: (s, i, 0)),
                 b_spec=pl.BlockSpec((tm, d), lambda s, i: (i, 0)), name=name)


def ffn_grad_w_out(act, dyh, after, *, name):
    _, t, nf = act.shape
    d = dyh.shape[1]
    tm = _row_tile(t, TN_TILE)
    d_w_out = mm_tn(act, dyh, nb=4, ka=nf, nbk=d, tm=tm, m=t, after=after,
                    a_spec=pl.BlockSpec((None, tm, nf), lambda s, i: (s, i, 0)),
                    b_spec=pl.BlockSpec((tm, d), lambda s, i: (i, 0)), name=name)
    return d_w_out.reshape(8, nf // 2, d)


def _lane_group(shape):
    return lax.shift_right_logical(lax.broadcasted_iota(jnp.int32, shape, 1), 6)


def _pool_count(t0, rows):
    t = (t0 + lax.broadcasted_iota(jnp.int32, (rows, MIX_W), 0) + 1).astype(F32)
    return jnp.minimum(t, _by_group(_lane_group((rows, MIX_W)), 2.0, 4.0, 8.0, 16.0))


def _by_group(grp, v0, v1, v2, v3):
    return jnp.where(grp == 0, v0, jnp.where(grp == 1, v1, jnp.where(grp == 2, v2, v3)))


def _sgu_mix(wt_ref, vnc):
    grp = _lane_group((SGU_CHUNK, MIX_W))
    out = jnp.zeros((SGU_CHUNK, MIX_W), F32)
    for hd in range(N_HEADS):
        out = jnp.where(grp == hd, _dot(wt_ref[hd], vnc), out)
    return out


def _pool_fwd(s1, s2, s3, t0, ts, lo):
    h = lo
    s2[h - 24:h + ts] = s1[h - 24:h + ts] + s1[h - 25:h + ts - 1]
    s3[h - 16:h + ts] = s2[h - 16:h + ts] + s2[h - 18:h + ts - 2]
    sum2 = s2[h:h + ts]
    sum4 = s3[h:h + ts]
    s2[h - 8:h + ts] = s3[h - 8:h + ts] + s3[h - 12:h + ts - 4]
    sum8 = s2[h:h + ts]
    sum16 = sum8 + s2[h - 8:h + ts - 8]
    grp = _lane_group((ts, MIX_W))
    return _by_group(grp, sum2, sum4, sum8, sum16) / _pool_count(t0, ts) - s1[h:h + ts]


def _make_shifts(src, sh, rows):
    for b in range(1, 8):
        sh[b, 0:rows] = src[b:b + rows]


def _rows_at(src, sh, start, n):
    a, b = divmod(start, 8)
    return src[8 * a:8 * a + n] if b == 0 else sh[b, 8 * a:8 * a + n]


def mixer_fwd(z, sconv, cconv, vecs, wt, bexp, pbd, x_res, wmo_g, *, name, ts=None):
    t = z.shape[0]
    ts = _row_tile(t, MIX_TILE if ts is None else ts)
    hl = HALO
    w = MIX_W
    nch = ts // SGU_CHUNK

    def body(zc, zp, sconv_ref, cconv_ref, vec_ref, wt_ref, bexp_ref, pbd_ref, xr_ref, wmo_ref,
             y_ref, xo_ref, s1, s2, s3, sh):
        i = pl.program_id(0)
        has_prev = i > 0

        def col(ref, c):
            return ref[:, c * w:(c + 1) * w]

        def prev(c):
            return jnp.where(has_prev, col(zp, c), 0.0)

        s1[0:hl] = prev(1) * prev(2)
        s1[hl:hl + ts] = col(zc, 1) * col(zc, 2)
        cv = sconv_ref[0:1] * s1[hl - 2:hl - 2 + ts]
        for k in range(1, SCONV_K):
            cv = cv + sconv_ref[k:k + 1] * s1[hl - 2 + k:hl - 2 + k + ts]
        y_ref[:, 0:w] = (col(zc, 0) * cv).astype(BF16)

        xhat, _ = _ln_stats(col(zc, 4))
        vn = (xhat * vec_ref[0:1]).astype(BF16)
        for c in range(nch):
            rows = slice(c * SGU_CHUNK, (c + 1) * SGU_CHUNK)
            mixed = _sgu_mix(wt_ref, vn[rows]) + bexp_ref[...]
            y_ref[rows, w:2 * w] = (zc[rows, 3 * w:4 * w] * mixed).astype(BF16)

        s1[0:hl] = prev(5) * _sigmoid(prev(6))
        s1[hl:hl + ts] = col(zc, 5) * _sigmoid(col(zc, 6))
        off = hl - (CCONV_K - 1)
        _make_shifts(s1, sh, hl + ts - 8)
        cv = cconv_ref[0:1] * _rows_at(s1, sh, off, ts)
        for k in range(1, CCONV_K):
            cv = cv + cconv_ref[k:k + 1] * _rows_at(s1, sh, off + k, ts)
        xhat, _ = _ln_stats(cv)
        ln = xhat * vec_ref[1:2] + vec_ref[2:3]
        y_ref[:, 2 * w:3 * w] = (ln * _sigmoid(ln)).astype(BF16)

        s1[0:hl] = prev(7)
        s1[hl:hl + ts] = col(zc, 7)
        pooled = _pool_fwd(s1, s2, s3, i * ts, ts, hl)
        y_ref[:, 3 * w:4 * w] = (_dot(pooled.astype(BF16), pbd_ref[...]) * vec_ref[3:4]).astype(BF16)

        xo_ref[...] = xr_ref[...] + _dot(y_ref[...], _full_weight(wmo_ref, "row"))

    full = lambda shape: pl.BlockSpec(shape, lambda i: (0,) * len(shape))
    row = lambda i: (i, 0)
    return pl.pallas_call(
        body, name=name, grid=(t // ts,),
        out_shape=[jax.ShapeDtypeStruct((t, 4 * w), BF16), jax.ShapeDtypeStruct((t, 4 * w), F32)],
        in_specs=[pl.BlockSpec((ts, 8 * w), row),
                  pl.BlockSpec((hl, 8 * w), lambda i: (jnp.maximum(i * (ts // hl) - 1, 0), 0)),
                  full((8, w)), full((32, w)), full((8, w)), full((N_HEADS, SGU_CHUNK, SGU_CHUNK)),
                  full((SGU_CHUNK, w)), full((w, w)), pl.BlockSpec((ts, 4 * w), row), _wspec(wmo_g)],
        out_specs=[pl.BlockSpec((ts, 4 * w), row), pl.BlockSpec((ts, 4 * w), row)],
        scratch_shapes=[pltpu.VMEM((hl + ts, w), F32)] * 3 + [pltpu.VMEM((8, hl + ts, w), F32)],
        compiler_params=_params("parallel"),
    )(z, z, sconv, cconv, vecs, wt, bexp, pbd, x_res, wmo_g)


def mixer_bwd(z, dx, wmo_g, sconv, cconv, vecs, wt, bexp, pbd, *, name, ts=None):
    t = z.shape[0]
    ts = _row_tile(t, MIX_TILE if ts is None else ts)
    hl = HALO
    w = MIX_W
    nch = ts // SGU_CHUNK
    ni = t // ts
    ext = ts + hl

    def body(zc, zp, zn, dxc, dxn_, wmo_ref, sconv_ref, cconv_ref, vec_ref, wt_ref, bexp_ref, pbd_ref,
             dz_ref, gvec_ref, gcc_ref, gwt_ref, gb_ref, gpbd_ref, s1, s2, s3, sh1, sh3, dyc, dyn):
        i = pl.program_id(0)
        has_prev = i > 0
        has_next = i < ni - 1
        wmo = _full_weight(wmo_ref, "row")
        dyc[...] = _dot_nt(dxc[...].astype(BF16), wmo)
        dyn[...] = _dot_nt(dxn_[...].astype(BF16), wmo)

        @pl.when(i == 0)
        def _():
            gvec_ref[...] = jnp.zeros_like(gvec_ref)
            gcc_ref[...] = jnp.zeros_like(gcc_ref)
            gwt_ref[...] = jnp.zeros_like(gwt_ref)
            gb_ref[...] = jnp.zeros_like(gb_ref)
            gpbd_ref[...] = jnp.zeros_like(gpbd_ref)

        def col(ref, c):
            return ref[:, c * w:(c + 1) * w]

        def prev(c):
            return jnp.where(has_prev, col(zp, c), 0.0)

        def nxt(c):
            return jnp.where(has_next, col(zn, c), 0.0)

        def dnext(c):
            return jnp.where(has_next, col(dyn, c), 0.0)

        def rowsum(v):
            return jnp.sum(v, axis=0, keepdims=True)

        s1[0:hl] = prev(1) * prev(2)
        s1[hl:hl + ts] = col(zc, 1) * col(zc, 2)
        s1[hl + ts:hl + ts + hl] = nxt(1) * nxt(2)
        cv = sconv_ref[0:1] * s1[hl - 2:hl - 2 + ts]
        for k in range(1, SCONV_K):
            cv = cv + sconv_ref[k:k + 1] * s1[hl - 2 + k:hl - 2 + k + ts]
        dya = col(dyc, 0)
        dz_ref[:, 0:w] = (dya * cv).astype(BF16)
        s2[0:ts] = dya * col(zc, 0)
        s2[ts:ext] = dnext(0) * nxt(0)
        dv = sconv_ref[0:1] * s2[2:2 + ts]
        for k in range(1, SCONV_K):
            dv = dv + sconv_ref[k:k + 1] * s2[2 - k:2 - k + ts]
        dz_ref[:, w:2 * w] = (dv * col(zc, 2)).astype(BF16)
        dz_ref[:, 2 * w:3 * w] = (dv * col(zc, 1)).astype(BF16)
        dcv = s2[0:ts]
        for k in range(SCONV_K):
            gvec_ref[k:k + 1] += rowsum(dcv * s1[hl - 2 + k:hl - 2 + k + ts])

        g_sgu = vec_ref[0:1]
        xhat, rstd = _ln_stats(col(zc, 4))
        vn = (xhat * g_sgu).astype(BF16)
        grp = _lane_group((SGU_CHUNK, w))
        lane = lax.broadcasted_iota(jnp.int32, (SGU_CHUNK, SGU_CHUNK), 1)
        tril = lax.broadcasted_iota(jnp.int32, (SGU_CHUNK, SGU_CHUNK), 0) >= lane
        for c in range(nch):
            rows = slice(c * SGU_CHUNK, (c + 1) * SGU_CHUNK)
            vnc = vn[rows]
            mixed = _sgu_mix(wt_ref, vnc) + bexp_ref[...]
            dyb = dyc[rows, w:2 * w]
            dz_ref[rows, 3 * w:4 * w] = (dyb * mixed).astype(BF16)
            dmix = dyb * zc[rows, 3 * w:4 * w]
            dmixb = dmix.astype(BF16)
            dvn = jnp.zeros((SGU_CHUNK, w), F32)
            gb = jnp.zeros((SGU_CHUNK, SGU_CHUNK), F32)
            for hd in range(N_HEADS):
                dvn = jnp.where(grp == hd, _dot_tn(wt_ref[hd], dmixb), dvn)
                dm_h = jnp.where(grp == hd, dmix, 0.0)
                gwt_ref[hd] += jnp.where(tril, _dot_nt(dm_h.astype(BF16), vnc), 0.0)
                gb = gb + jnp.where(lane == hd, jnp.sum(dm_h, axis=1, keepdims=True), 0.0)
            gb_ref[...] += gb
            s3[rows] = dvn
        dvn = s3[0:ts]
        gvec_ref[3:4] += rowsum(dvn * xhat)
        dz_ref[:, 4 * w:5 * w] = _ln_bwd(xhat, rstd, dvn * g_sgu).astype(BF16)

        sig_c = _sigmoid(col(zc, 6))
        s1[0:hl] = prev(5) * _sigmoid(prev(6))
        s1[hl:hl + ts] = col(zc, 5) * sig_c
        s1[hl + ts:hl + ts + hl] = nxt(5) * _sigmoid(nxt(6))
        off = hl - (CCONV_K - 1)
        _make_shifts(s1, sh1, ts + 2 * hl - 8)
        cv = cconv_ref[0:1] * _rows_at(s1, sh1, off, ext)
        for k in range(1, CCONV_K):
            cv = cv + cconv_ref[k:k + 1] * _rows_at(s1, sh1, off + k, ext)
        xhat, rstd = _ln_stats(cv)
        ln = xhat * vec_ref[1:2] + vec_ref[2:3]
        sg = _sigmoid(ln)
        s2[0:ts] = col(dyc, 2)
        s2[ts:ext] = dnext(2)
        dln = s2[0:ext] * (sg * (1.0 + ln * (1.0 - sg)))
        gvec_ref[4:5] += rowsum(dln[0:ts] * xhat[0:ts])
        gvec_ref[5:6] += rowsum(dln[0:ts])
        s3[0:ext] = _ln_bwd(xhat, rstd, dln * vec_ref[1:2])
        _make_shifts(s3, sh3, ext - 8)
        dyg = cconv_ref[0:1] * _rows_at(s3, sh3, CCONV_K - 1, ts)
        for k in range(1, CCONV_K):
            dyg = dyg + cconv_ref[k:k + 1] * _rows_at(s3, sh3, CCONV_K - 1 - k, ts)
        dz_ref[:, 5 * w:6 * w] = (dyg * sig_c).astype(BF16)
        dz_ref[:, 6 * w:7 * w] = (dyg * col(zc, 5) * sig_c * (1.0 - sig_c)).astype(BF16)
        dcv = s3[0:ts]
        for k in range(CCONV_K):
            gcc_ref[k:k + 1] += rowsum(dcv * _rows_at(s1, sh1, off + k, ts))

        scale = vec_ref[3:4]
        s1[0:hl] = prev(7)
        s1[hl:hl + ts] = col(zc, 7)
        pooled = _pool_fwd(s1, s2, s3, i * ts, ts, hl).astype(BF16)
        q0 = _dot(pooled, pbd_ref[...])
        dyd = col(dyc, 3)
        gvec_ref[6:7] += rowsum(dyd * q0)
        dq = (dyd * scale).astype(BF16)
        gpbd_ref[...] += _dot_tn(pooled, dq)
        s1[0:ts] = _dot_nt(dq, pbd_ref[...])
        s1[ts:ext] = _dot_nt((dnext(3) * scale).astype(BF16), pbd_ref[...])
        dpool = s1[0:ts]
        s2[0:ext] = s1[0:ext] / _pool_count(i * ts, ext)
        s3[0:ts + 24] = s2[0:ts + 24] + s2[1:ts + 25]
        f2 = s3[0:ts]
        s2[0:ts + 16] = s3[0:ts + 16] + s3[2:ts + 18]
        f4 = s2[0:ts]
        s3[0:ts + 8] = s2[0:ts + 8] + s2[4:ts + 12]
        f8 = s3[0:ts]
        f16 = f8 + s3[8:ts + 8]
        dz_ref[:, 7 * w:8 * w] = (_by_group(_lane_group((ts, w)), f2, f4, f8, f16) - dpool).astype(BF16)

    full = lambda shape: pl.BlockSpec(shape, lambda i: (0,) * len(shape))
    r = ts // hl
    prev_map = lambda i: (jnp.maximum(i * r - 1, 0), 0)
    next_map = lambda i: (jnp.minimum((i + 1) * r, t // hl - 1), 0)
    return pl.pallas_call(
        body, name=name, grid=(ni,),
        out_shape=[jax.ShapeDtypeStruct((t, 8 * w), BF16), jax.ShapeDtypeStruct((8, w), F32),
                   jax.ShapeDtypeStruct((32, w), F32),
                   jax.ShapeDtypeStruct((N_HEADS, SGU_CHUNK, SGU_CHUNK), F32),
                   jax.ShapeDtypeStruct((SGU_CHUNK, SGU_CHUNK), F32), jax.ShapeDtypeStruct((w, w), F32)],
        in_specs=[pl.BlockSpec((ts, 8 * w), lambda i: (i, 0)),
                  pl.BlockSpec((hl, 8 * w), prev_map), pl.BlockSpec((hl, 8 * w), next_map),
                  pl.BlockSpec((ts, 4 * w), lambda i: (i, 0)), pl.BlockSpec((hl, 4 * w), next_map),
                  _wspec(wmo_g),
                  full((8, w)), full((32, w)), full((8, w)), full((N_HEADS, SGU_CHUNK, SGU_CHUNK)),
                  full((SGU_CHUNK, w)), full((w, w))],
        out_specs=[pl.BlockSpec((ts, 8 * w), lambda i: (i, 0)), full((8, w)), full((32, w)),
                   full((N_HEADS, SGU_CHUNK, SGU_CHUNK)), full((SGU_CHUNK, SGU_CHUNK)), full((w, w))],
        scratch_shapes=[pltpu.VMEM((ts + 2 * hl, w), F32)] * 3 + [pltpu.VMEM((8, ts + 2 * hl, w), F32)] * 2
        + [pltpu.VMEM((ts, 4 * w), F32), pltpu.VMEM((hl, 4 * w), F32)],
        compiler_params=_params("arbitrary"),
    )(z, z, z, dx, dx, wmo_g, sconv, cconv, vecs, wt, bexp, pbd)


def _attn_head(q, kv_ref, hd, d):
    hw = d // N_HEADS
    qh = q[:, hd * hw:(hd + 1) * hw]
    kh = kv_ref[:, hd * hw:(hd + 1) * hw].astype(BF16)
    vh = kv_ref[:, d + hd * hw:d + (hd + 1) * hw].astype(BF16)
    s = _dot_nt(qh, kh) * (1.0 / (hw ** 0.5))
    e = jnp.exp(s - jnp.max(s, axis=-1, keepdims=True))
    p = e / jnp.sum(e, axis=-1, keepdims=True)
    return qh, kh, vh, p


def xattn_fwd(x, gain, kv, wq_g, wo_g, *, name, tm=None):
    t, d = x.shape
    nm = kv.shape[0]
    tm = _row_tile(t, tm)
    hw = d // N_HEADS

    def body(x_ref, g_ref, kv_ref, wq_ref, wo_ref, o_ref):
        xv = x_ref[...]
        xhat, _ = _rms_fwd(xv, None)
        h = (xhat * g_ref[...]).astype(BF16)
        q = _dot(h, _full_weight(wq_ref, "row")).astype(BF16)
        wo = _full_weight(wo_ref, "row")
        out = xv
        for hd in range(N_HEADS):
            _, _, vh, p = _attn_head(q, kv_ref, hd, d)
            oh = _dot(p.astype(BF16), vh).astype(BF16)
            out = out + _dot(oh, wo[hd * hw:(hd + 1) * hw])
        o_ref[...] = out

    row = lambda i: (i, 0)
    return pl.pallas_call(
        body, name=name, grid=(t // tm,),
        out_shape=jax.ShapeDtypeStruct((t, d), F32),
        in_specs=[pl.BlockSpec((tm, d), row), pl.BlockSpec((1, d), lambda i: (0, 0)),
                  pl.BlockSpec((nm, 2 * d), lambda i: (0, 0)), _wspec(wq_g), _wspec(wo_g)],
        out_specs=pl.BlockSpec((tm, d), row),
        compiler_params=_params("parallel"),
    )(x, gain.reshape(1, d), kv, wq_g, wo_g)


def xattn_bwd_rows(x, dxn, gain, kv, wq_g, wo_g, after, *, name, tm=None):
    t, d = x.shape
    nm = kv.shape[0]
    tm = _row_tile(t, tm)
    hw = d // N_HEADS

    def body(x_ref, dxn_ref, g_ref, kv_ref, wq_ref, wo_ref, after_ref,
             dx_ref, h_ref, dq_ref, o_ref, dkv_ref, dg_ref):
        i = pl.program_id(0)

        @pl.when(i == 0)
        def _():
            dkv_ref[...] = jnp.zeros_like(dkv_ref)
            dg_ref[...] = jnp.zeros_like(dg_ref)
        g = g_ref[...]
        xhat, r = _rms_fwd(x_ref[...], None)
        h = (xhat * g).astype(BF16)
        h_ref[...] = h
        wq = _full_weight(wq_ref, "row")
        q = _dot(h, wq).astype(BF16)
        dxn = dxn_ref[...]
        do = _dot_nt(dxn.astype(BF16), _full_weight(wo_ref, "row")).astype(BF16)
        for hd in range(N_HEADS):
            cols = slice(hd * hw, (hd + 1) * hw)
            qh, kh, vh, p = _attn_head(q, kv_ref, hd, d)
            pb = p.astype(BF16)
            o_ref[:, cols] = _dot(pb, vh).astype(BF16)
            doh = do[:, cols]
            dkv_ref[:, d + hd * hw:d + (hd + 1) * hw] += _dot_tn(pb, doh)
            dp = _dot_nt(doh, vh)
            ds = (p * (dp - jnp.sum(dp * p, axis=-1, keepdims=True)) * (1.0 / (hw ** 0.5))).astype(BF16)
            dq_ref[:, cols] = _dot(ds, kh).astype(BF16)
            dkv_ref[:, cols] += _dot_tn(ds, qh)
        dh = _dot_nt(dq_ref[...], wq)
        dx, dg = _rms_bwd(xhat, r, g, dh)
        dx_ref[...] = dxn + dx
        dg_ref[...] += dg

    row = lambda i: (i, 0)
    fix = lambda i: (0, 0)
    return pl.pallas_call(
        body, name=name, grid=(t // tm,),
        out_shape=[jax.ShapeDtypeStruct((t, d), F32), jax.ShapeDtypeStruct((t, d), BF16),
                   jax.ShapeDtypeStruct((t, d), BF16), jax.ShapeDtypeStruct((t, d), BF16),
                   jax.ShapeDtypeStruct((nm, 2 * d), F32), jax.ShapeDtypeStruct((1, d), F32)],
        in_specs=[pl.BlockSpec((tm, d), row), pl.BlockSpec((tm, d), row), pl.BlockSpec((1, d), fix),
                  pl.BlockSpec((nm, 2 * d), fix), _wspec(wq_g), _wspec(wo_g), ANY_SPEC],
        out_specs=[pl.BlockSpec((tm, d), row)] * 4 + [pl.BlockSpec((nm, 2 * d), fix),
                                                      pl.BlockSpec((1, d), fix)],
        compiler_params=_params("arbitrary"),
    )(x, dxn, gain.reshape(1, d), kv, wq_g, wo_g, after)


def loss_head(x, target, gain, *, name, tm=None):
    t, d = x.shape
    tm = _row_tile(t, tm)

    def body(x_ref, t_ref, g_ref, dx_ref, dg_ref, loss_ref):
        @pl.when(pl.program_id(0) == 0)
        def _():
            dg_ref[...] = jnp.zeros_like(dg_ref)
            loss_ref[...] = jnp.zeros_like(loss_ref)
        g = g_ref[...]
        xhat, r = _rms_fwd(x_ref[...], None)
        err = xhat * g - t_ref[...]
        loss_ref[...] += 0.5 * jnp.sum(jnp.sum(err * err, axis=-1, keepdims=True) / d,
                                       axis=0, keepdims=True)
        dx, dg = _rms_bwd(xhat, r, g, err / d)
        dx_ref[...] = dx
        dg_ref[...] += dg

    row = lambda i: (i, 0)
    fix = lambda i: (0, 0)
    return pl.pallas_call(
        body, name=name, grid=(t // tm,),
        out_shape=[jax.ShapeDtypeStruct((t, d), F32), jax.ShapeDtypeStruct((1, d), F32),
                   jax.ShapeDtypeStruct((1, 1), F32)],
        in_specs=[pl.BlockSpec((tm, d), row), pl.BlockSpec((tm, d), row), pl.BlockSpec((1, d), fix)],
        out_specs=[pl.BlockSpec((tm, d), row), pl.BlockSpec((1, d), fix), pl.BlockSpec((1, 1), fix)],
        compiler_params=_params("arbitrary"),
    )(x, target, gain.reshape(1, d))


def _adamw_math(w, g, m, v):
    m = ADAM_B1 * m + (1.0 - ADAM_B1) * g
    v = ADAM_B2 * v + (1.0 - ADAM_B2) * (g * g)
    m_hat = m / (1.0 - ADAM_B1 ** ADAM_STEP)
    v_hat = v / (1.0 - ADAM_B2 ** ADAM_STEP)
    delta = -ADAM_LR * (m_hat / (jnp.sqrt(v_hat) + ADAM_EPS) + ADAM_WD * w)
    return delta, m, v


def adamw_sharded(own, lands, w, m, v, me_arr, *, name):
    nl, r, c = w.shape
    assert nl == len(own) == len(lands) == 2
    tr = next(cand for cand in (*ROW_BLOCKS, r) if r % cand == 0)
    nr = r // tr

    def body(me_ref, o0, o1, l0, l1, w_ref, m_ref, v_ref, g_out, d_out, m_out, v_out):
        def total(o_ref, l_ref):
            acc = o_ref[...].astype(F32)
            for p in range(N_DEV - 1):
                acc = acc + l_ref[p].astype(F32)
            return acc
        g = jnp.where(pl.program_id(0) == 0, total(o0, l0), total(o1, l1))
        delta, mn, vn = _adamw_math(w_ref[...], g, m_ref[...], v_ref[...])
        g_out[...] = g
        d_out[...] = delta
        m_out[...] = mn
        v_out[...] = vn

    row0 = lambda l, i: jnp.where(l == 0, i, nr - 1)
    row1 = lambda l, i: jnp.where(l == 1, i, 0)
    blk = pl.BlockSpec((None, tr, c), lambda l, i, me: (l, i, 0))
    grid_spec = pltpu.PrefetchScalarGridSpec(
        num_scalar_prefetch=1, grid=(nl, nr),
        in_specs=[pl.BlockSpec((None, tr, c), lambda l, i, me: (me[0], row0(l, i), 0)),
                  pl.BlockSpec((None, tr, c), lambda l, i, me: (me[0], row1(l, i), 0)),
                  pl.BlockSpec((N_DEV - 1, tr, c), lambda l, i, me: (0, row0(l, i), 0)),
                  pl.BlockSpec((N_DEV - 1, tr, c), lambda l, i, me: (0, row1(l, i), 0)),
                  blk, blk, blk],
        out_specs=[blk] * 4)
    return pl.pallas_call(
        body, name=name, grid_spec=grid_spec,
        out_shape=[jax.ShapeDtypeStruct((nl, r, c), F32)] * 4,
        compiler_params=_params("arbitrary", "arbitrary"),
    )(me_arr, own[0], own[1], lands[0], lands[1], w, m, v)


def adamw_many(gs, ws, ms, vs, *, name):
    n = len(ws)
    shapes = [w.shape for w in ws]
    as2d = lambda a: a.reshape(1, -1) if a.ndim == 1 else a

    def body(*refs):
        g_r, w_r, m_r, v_r = refs[:n], refs[n:2 * n], refs[2 * n:3 * n], refs[3 * n:4 * n]
        outs = refs[4 * n:]
        for i in range(n):
            delta, mn, vn = _adamw_math(w_r[i][...], g_r[i][...], m_r[i][...], v_r[i][...])
            outs[3 * i][...] = delta
            outs[3 * i + 1][...] = mn
            outs[3 * i + 2][...] = vn

    operands = [as2d(a) for group in (gs, ws, ms, vs) for a in group]
    out_shape = [jax.ShapeDtypeStruct(as2d(w).shape, F32) for w in ws for _ in range(3)]
    outs = pl.pallas_call(
        body, name=name, out_shape=out_shape,
        in_specs=[VMEM_SPEC] * (4 * n), out_specs=[VMEM_SPEC] * (3 * n),
        compiler_params=pltpu.CompilerParams(vmem_limit_bytes=VMEM_LIMIT),
    )(*operands)
    return [tuple(outs[3 * i + k].reshape(shapes[i]) for k in range(3)) for i in range(n)]


def cast_into_slot(a, layer, me_arr, *, name, dtype=None, after=None):
    dtype = BF16 if dtype is None else dtype
    _, r, c = a.shape
    tr = next(cand for cand in (*ROW_BLOCKS, r) if r % cand == 0)
    extra = [] if after is None else [after]

    def body(me_ref, a_ref, *rest):
        rest[-1][...] = a_ref[...].astype(dtype)

    grid_spec = pltpu.PrefetchScalarGridSpec(
        num_scalar_prefetch=1, grid=(r // tr,),
        in_specs=[pl.BlockSpec((None, tr, c), lambda i, me: (layer, i, 0))] + [ANY_SPEC] * len(extra),
        out_specs=pl.BlockSpec((None, tr, c), lambda i, me: (me[0], i, 0)))
    return pl.pallas_call(
        body, name=name, grid_spec=grid_spec,
        out_shape=jax.ShapeDtypeStruct((N_DEV, r, c), dtype),
        compiler_params=_params("parallel"),
    )(me_arr, a, *extra)


def _pack(arrs, rows):
    flat = jnp.concatenate([a.reshape(-1).astype(F32) for a in arrs])
    pad = rows * 128 - flat.shape[0]
    assert pad >= 0
    if pad:
        flat = jnp.concatenate([flat, jnp.zeros((pad,), F32)])
    return flat.reshape(rows, 128)


def _unpack(packed, shapes):
    flat = packed.reshape(-1)
    out, pos = [], 0
    for s in shapes:
        n = 1
        for dim in s:
            n *= dim
        out.append(flat[pos:pos + n].reshape(s))
        pos += n
    return out


def _rows_for(shapes):
    n = 0
    for s in shapes:
        k = 1
        for dim in s:
            k *= dim
        n += k
    return -(-n // 1024) * 8


GATHER_GROUPS = (("ffn1", ("ffn1_w_in", "ffn1_w_out")),
                 ("mid", ("mix_w_in", "mix_w_out", "xattn_wkv", "xattn_wq", "xattn_wo")),
                 ("ffn2", ("ffn2_w_in", "ffn2_w_out")))
SMALL_REPL = ["norm_ffn1", "norm_mix", "sgu_norm_g", "sgu_w", "sgu_b", "cconv_ln_g", "cconv_ln_b",
              "pool_w", "pool_scale", "norm_xattn", "norm_mem", "norm_ffn2", "norm_final"]
SMALL_SHARD = ["sconv_w", "cconv_w"]
TRANSPOSED = ("ffn1_w_in", "ffn2_w_in")
WEIGHTS = ["norm_ffn1", "ffn1_w_in", "ffn1_w_out", "norm_mix", "mix_w_in", "sconv_w", "sgu_norm_g",
           "sgu_w", "sgu_b", "cconv_w", "cconv_ln_g", "cconv_ln_b", "pool_w", "pool_scale", "mix_w_out",
           "norm_xattn", "norm_mem", "xattn_wq", "xattn_wkv", "xattn_wo", "norm_ffn2", "ffn2_w_in",
           "ffn2_w_out", "norm_final"]


def kernel(x, mem, norm_ffn1, ffn1_w_in, ffn1_w_out, norm_mix, mix_w_in, sconv_w, sgu_norm_g, sgu_w, sgu_b, cconv_w, cconv_ln_g, cconv_ln_b, pool_w, pool_scale, mix_w_out, norm_xattn, norm_mem, xattn_wq, xattn_wkv, xattn_wo, norm_ffn2, ffn2_w_in, ffn2_w_out, norm_final, loss_target, m_norm_ffn1, m_ffn1_w_in, m_ffn1_w_out, m_norm_mix, m_mix_w_in, m_sconv_w, m_sgu_norm_g, m_sgu_w, m_sgu_b, m_cconv_w, m_cconv_ln_g, m_cconv_ln_b, m_pool_w, m_pool_scale, m_mix_w_out, m_norm_xattn, m_norm_mem, m_xattn_wq, m_xattn_wkv, m_xattn_wo, m_norm_ffn2, m_ffn2_w_in, m_ffn2_w_out, m_norm_final, v_norm_ffn1, v_ffn1_w_in, v_ffn1_w_out, v_norm_mix, v_mix_w_in, v_sconv_w, v_sgu_norm_g, v_sgu_w, v_sgu_b, v_cconv_w, v_cconv_ln_g, v_cconv_ln_b, v_pool_w, v_pool_scale, v_mix_w_out, v_norm_xattn, v_norm_mem, v_xattn_wq, v_xattn_wkv, v_xattn_wo, v_norm_ffn2, v_ffn2_w_in, v_ffn2_w_out, v_norm_final):
    args = dict(locals())
    wts = {n: args[n] for n in WEIGHTS}
    mom = {n: args["m_" + n] for n in WEIGHTS}
    var = {n: args["v_" + n] for n in WEIGHTS}
    for n in TRANSPOSED:
        wts[n], mom[n], var[n] = (jnp.swapaxes(a, 1, 2) for a in (wts[n], mom[n], var[n]))
    x0 = x[0]
    mem0 = mem[0]
    target = loss_target[0]
    t, d = x0.shape
    nl = norm_ffn1.shape[0]
    w = MIX_W
    me = _my_index()

    me_arr = jnp.reshape(me, (1,)).astype(jnp.int32)

    small_g = all_gather([sconv_w, cconv_w], name="gather_conv_taps")
    sconv_full = jnp.transpose(small_g[0], (1, 2, 0, 3)).reshape(nl, SCONV_K, w)
    cconv_full = jnp.transpose(small_g[1], (1, 2, 0, 3)).reshape(nl, CCONV_K, w)
    pending = {}
    token = small_g[1]
    masks = GATHER_MASKS
    keys = [(gname, l, members) for l in range(nl) for gname, members in GATHER_GROUPS]
    first = [[cast_into_slot(wts[n], keys[0][1], me_arr, name=f"cast_{n}{keys[0][1]}") for n in keys[0][2]]]
    started, token = gather_start_groups(first, token, name="gather_start_first", masks=masks)
    casts = [[cast_into_slot(wts[n], l, me_arr, name=f"cast_{n}{l}", after=token) for n in members]
             for gname, l, members in keys[1:]]
    rest, token = gather_start_groups(casts, token, name="gather_start_rest", masks=masks)
    for (gname, l, members), (send, recv, gs) in zip(keys, started + rest):
        pending[gname, l] = (members, gs, send, recv, masks)
    wg = [dict() for _ in range(nl)]

    handing_over = {}

    def arrive_early(gname, l, after):
        members, gs, send, recv, masks = pending.pop((gname, l))
        gs = gather_wait(gs, send, recv, after, name=f"gather_wait_{gname}{l}", masks=masks)
        fsend, frecv, gs, _ = forward_start(gs, after, name=f"gather_forward_start_{gname}{l}")
        handing_over[gname, l] = (members, gs, fsend, frecv)

    def arrive(gname, l, after):
        if (gname, l) in handing_over:
            members, gs, fsend, frecv = handing_over.pop((gname, l))
            gs = forward_wait(gs, fsend, frecv, after, name=f"gather_forward_wait_{gname}{l}")
        else:
            members, gs, send, recv, masks = pending.pop((gname, l))
            gs = gather_wait(gs, send, recv, after, name=f"gather_wait_{gname}{l}", masks=masks)
            gs = sibling_forward(gs, name=f"gather_forward_{gname}{l}")
        wg[l].update(zip(members, gs))
    sconv_pad = jnp.pad(sconv_full, ((0, 0), (0, 8 - SCONV_K), (0, 0)))
    cconv_pad = jnp.pad(cconv_full, ((0, 0), (0, 32 - CCONV_K), (0, 0)))
    zeros_w = jnp.zeros((nl, w), F32)
    vecs = jnp.stack([sgu_norm_g, cconv_ln_g, cconv_ln_b, pool_scale] + [zeros_w] * 4, axis=1)
    wt = jnp.tril(sgu_w).astype(BF16)
    bexp = jnp.repeat(jnp.swapaxes(sgu_b, 1, 2), w // N_HEADS, axis=2)
    eye = jnp.eye(4, dtype=F32)
    pbd = jnp.einsum("lgcd,gh->lgchd", pool_w, eye).reshape(nl, w, w).astype(BF16)

    def mixer_args(l):
        return sconv_pad[l], cconv_pad[l], vecs[l], wt[l], bexp[l], pbd[l]

    saved = []
    xc = x0
    after = token
    for l in range(nl):
        s = {"x_ffn1": xc}
        arrive("ffn1", l, after)
        xc, s["gu_ffn1"] = ffn_fwd(xc, norm_ffn1[l], wg[l]["ffn1_w_in"], wg[l]["ffn1_w_out"],
                                   name=f"ffn1_fwd{l}", tm=FFN_FWD_TILE)
        s["x_mix"] = xc
        arrive("mid", l, xc)
        z = mm_rows(xc, wg[l]["mix_w_in"], "col", gain=norm_mix[l], name=f"mix_in{l}")
        y, xc = mixer_fwd(z, *mixer_args(l), xc, wg[l]["mix_w_out"], name=f"mixer_fwd{l}")
        s["z"], s["y"] = z, y
        s["x_att"] = xc
        kv = mm_rows(mem0, wg[l]["xattn_wkv"], "col", gain=norm_mem[l], name=f"kv{l}")
        s["kv"] = kv
        if l > 0:
            arrive_early("ffn2", l, kv)
        xc = xattn_fwd(xc, norm_xattn[l], kv, wg[l]["xattn_wq"], wg[l]["xattn_wo"], name=f"xattn_fwd{l}")
        s["x_ffn2"] = xc
        arrive("ffn2", l, xc)
        xc, s["gu_ffn2"] = ffn_fwd(xc, norm_ffn2[l], wg[l]["ffn2_w_in"], wg[l]["ffn2_w_out"],
                                   name=f"ffn2_fwd{l}", tm=FFN_FWD_TILE)
        after = xc
        saved.append(s)

    dx, g_norm_final, loss_local = loss_head(xc, target, norm_final, name="loss_head")

    tm = _row_tile(t, TN_TILE)
    small ={n: [None] * nl for n in SMALL_REPL + SMALL_SHARD if n != "norm_final"}
    scattered = {}
    tie = [token]

    def send_grads(gname, l, grads):
        members = list(grads)
        send, recv, gs, lands, tie[0] = scatter_start(
            [grads[n] for n in members], tie[0], name=f"scatter_start_{gname}{l}")
        scattered[gname, l] = (members, gs, lands, send, recv)

    names = SMALL_REPL + SMALL_SHARD + ["loss"]
    small_pending = []

    def start_small():
        small_full = {n: jnp.stack(v) for n, v in small.items()}
        small_full["norm_final"] = g_norm_final[0]
        small_full["loss"] = loss_local[0]
        shapes = [small_full[n].shape for n in names]
        packed = _pack([small_full[n] for n in names], _rows_for(shapes))
        slot = cast_into_slot(packed[None], 0, me_arr, name="small_into_slot", dtype=F32)
        send, recv, gs, tie[0] = gather_start([slot], tie[0], name="small_gather_start", masks=ALL_MASKS)
        small_pending.append((gs, send, recv, shapes))

    def ffn_backward(which, l, x_in, dy, gu, gain):
        w_in, w_out = wg[l][which + "_w_in"], wg[l][which + "_w_out"]
        dx_, h_, act, dgu, dgn, dyh = ffn_bwd_rows(x_in, dy, gu, gain, w_in, w_out, tie[0],
                                                   name=f"{which}_bwd{l}_rows")
        small["norm_" + which][l] = dgn[0]
        last = which == "ffn1" and l == 0
        if last:
            start_small()
        g_in = ffn_grad_w_in(h_, dgu, tie[0], name=f"{which}_bwd{l}_dwin")
        if last:
            send_grads(which + "_in", l, {which + "_w_in": g_in})
        g_out = ffn_grad_w_out(act, dyh, tie[0], name=f"{which}_bwd{l}_dwout")
        if last:
            send_grads(which + "_out", l, {which + "_w_out": g_out})
        else:
            send_grads(which, l, {which + "_w_in": g_in, which + "_w_out": g_out})
        return dx_

    for l in reversed(range(nl)):
        s = saved[l]
        wl = wg[l]
        dx = ffn_backward("ffn2", l, s["x_ffn2"], dx, s["gu_ffn2"], norm_ffn2[l])

        bg = {}
        dxn = dx
        dx, h, dq, o, dkv, dgn = xattn_bwd_rows(
            s["x_att"], dxn, norm_xattn[l], s["kv"], wl["xattn_wq"], wl["xattn_wo"], tie[0],
            name=f"xattn_bwd{l}")
        small["norm_xattn"][l] = dgn[0]
        row_spec = pl.BlockSpec((tm, d), lambda s_, i: (i, 0))
        bg["xattn_wq"] = mm_tn(h, dq, nb=1, ka=d, nbk=d, tm=tm, m=t, a_spec=row_spec, b_spec=row_spec,
                               name=f"dwq{l}").reshape(N_DEV, d // N_DEV, d)
        bg["xattn_wo"] = mm_tn(o, dxn, nb=1, ka=d, nbk=d, tm=tm, m=t, a_spec=row_spec, b_spec=row_spec,
                               name=f"dwo{l}").reshape(N_DEV, d // N_DEV, d)
        _, mhat, dgn = mm_nt(dkv, wl["xattn_wkv"], "col", x=mem0, gain=norm_mem[l], name=f"dmem{l}")
        small["norm_mem"][l] = dgn[0]
        nm = mem0.shape[0]
        bg["xattn_wkv"] = mm_tn(mhat, dkv, nb=N_DEV, ka=d, nbk=2 * d // N_DEV, tm=nm, m=nm,
                                a_spec=pl.BlockSpec((nm, d), lambda s_, i: (0, 0)),
                                b_spec=pl.BlockSpec((nm, 2 * d // N_DEV), lambda s_, i: (0, s_)),
                                name=f"dwkv{l}")
        send_grads("xattn", l, bg)

        bg = {}
        dxn = dx
        bg["mix_w_out"] = mm_tn(s["y"], dxn, nb=1, ka=d, nbk=d, tm=tm, m=t, a_spec=row_spec,
                                b_spec=row_spec, name=f"dwmo{l}").reshape(N_DEV, d // N_DEV, d)
        dz, gvec, gcc, gwt, gb, gpbd = mixer_bwd(s["z"], dxn, wl["mix_w_out"], *mixer_args(l),
                                                 name=f"mixer_bwd{l}")
        small["sconv_w"][l] = gvec[0:SCONV_K]
        small["sgu_norm_g"][l] = gvec[3]
        small["cconv_ln_g"][l] = gvec[4]
        small["cconv_ln_b"][l] = gvec[5]
        small["pool_scale"][l] = gvec[6]
        small["cconv_w"][l] = gcc[0:CCONV_K]
        small["sgu_w"][l] = gwt
        small["sgu_b"][l] = jnp.transpose(gb[:, 0:N_HEADS])
        gw = w // 4
        small["pool_w"][l] = jnp.stack([gpbd[g * gw:(g + 1) * gw, g * gw:(g + 1) * gw] for g in range(4)])
        dx, h, dgn = mm_nt(dz, wl["mix_w_in"], "col", x=s["x_mix"], gain=norm_mix[l], dx_in=dxn,
                           after=tie[0], name=f"dh_mix{l}")
        small["norm_mix"][l] = dgn[0]
        th = _row_tile(t, TN_TILE // 2)
        bg["mix_w_in"] = mm_tn(h, dz, nb=1, ka=d, nbk=N_DEV * w, tm=th, m=t, col_slots=N_DEV,
                               a_spec=pl.BlockSpec((th, d), lambda s_, i: (i, 0)),
                               b_spec=pl.BlockSpec((th, N_DEV * w), lambda s_, i: (i, 0)), name=f"dwmi{l}")
        send_grads("mix", l, bg)

        dx = ffn_backward("ffn1", l, s["x_ffn1"], dx, s["gu_ffn1"], norm_ffn1[l])

    out = {}
    own, land = {}, {}

    def collect(keys, after):
        for gname, l in keys:
            members, gs, lands, send, recv = scattered.pop((gname, l))
            gs, lands = scatter_wait(gs, lands, send, recv, after, name=f"scatter_wait_{gname}{l}")
            for n, g_, l_ in zip(members, gs, lands):
                own.setdefault(n, {})[l] = g_
                land.setdefault(n, {})[l] = l_

    def update(ns, after):
        for n in ns:
            out[n] = adamw_sharded([own[n][l] for l in range(nl)], [land[n][l] for l in range(nl)],
                                   wts[n], mom[n], var[n], me_arr, name="adamw_" + n)
            after = out[n][1]
        return after

    after = tie[0]
    for gname in ("ffn2", "xattn", "mix"):
        collect([(gname, l) for l in reversed(range(nl))], after)
        after = update([n for n in own if n not in out], after)
    (gs, send, recv, shapes), = small_pending
    gs = gather_wait(gs, send, recv, after, name="small_gather_wait", masks=ALL_MASKS)
    summed = sum_slots(gs[0], name="small_sum")
    gsm = dict(zip(names, _unpack(summed, shapes)))
    loss = gsm["loss"][0]
    cs = w // N_DEV
    for n in SMALL_SHARD:
        gsm[n] = lax.dynamic_slice_in_dim(gsm[n], me * cs, cs, axis=2)
    small_names = SMALL_REPL + SMALL_SHARD
    upd = adamw_many([gsm[n] for n in small_names], [wts[n] for n in small_names],
                     [mom[n] for n in small_names], [var[n] for n in small_names], name="adamw_small")
    for n, (a, b, c) in zip(small_names, upd):
        out[n] = (gsm[n], a, b, c)
    after = upd[0][0]
    collect([("ffn1", l) for l in reversed(range(1, nl))] + [("ffn1_in", 0)], after)
    after = update(["ffn1_w_in"], after)
    collect([("ffn1_out", 0)], after)
    update(["ffn1_w_out"], after)
    for n in TRANSPOSED:
        out[n] = tuple(jnp.swapaxes(a, 1, 2) for a in out[n])

    grad_x = dx.reshape(1, t, d)
    return (loss, grad_x, *[out[n][0] for n in WEIGHTS], *[out[n][1] for n in WEIGHTS],
            *[out[n][2] for n in WEIGHTS], *[out[n][3] for n in WEIGHTS])
```

```python
import jax
import jax.numpy as jnp
from jax import lax
from jax.experimental import pallas as pl
from jax.experimental.pallas import tpu as pltpu

F32 = jnp.float32
BF16 = jnp.bfloat16
MESH = pl.DeviceIdType.MESH
N_DEV = 8
EPS = 1e-6
HALO = 32
SGU_CHUNK = 128
CCONV_K = 31
SCONV_K = 3
MIX_W = 256
N_HEADS = 4
VMEM_LIMIT = 56 * 1024 * 1024
ROW_TILE = 512
TN_TILE = 2048
FFN_FWD_TILE = 1024
FFN_BWD_SPLIT = 2
ROW_BLOCKS = (256, 176, 128)
MIX_TILE = 512

ADAM_LR = 0.001
ADAM_B1 = 0.9
ADAM_B2 = 0.999
ADAM_EPS = 1e-08
ADAM_WD = 0.01
ADAM_STEP = 10

HBM_SPEC = pl.BlockSpec(memory_space=pltpu.HBM)
VMEM_SPEC = pl.BlockSpec(memory_space=pltpu.VMEM)


def _params(*sem):
    return pltpu.CompilerParams(dimension_semantics=tuple(sem), vmem_limit_bytes=VMEM_LIMIT)


def _row_tile(m, pref=None):
    t = min(m, ROW_TILE if pref is None else pref)
    assert m % t == 0, (m, t)
    return t


def _my_index():
    return lax.axis_index("x") * 4 + lax.axis_index("y") * 2 + lax.axis_index("c")


def _peer(mask):
    x, y, c = lax.axis_index("x"), lax.axis_index("y"), lax.axis_index("c")
    px = 1 - x if mask & 4 else x
    py = 1 - y if mask & 2 else y
    pc = 1 - c if mask & 1 else c
    return (px, py, pc), px * 4 + py * 2 + pc


SEM_SPEC =pl.BlockSpec(memory_space=pltpu.SEMAPHORE)
ANY_SPEC = pl.BlockSpec(memory_space=pl.ANY)
SIDE_EFFECT = pltpu.SideEffectType.DATAFLOW_SIDE_EFFECTING


def _hbm(a):
    return pltpu.with_memory_space_constraint(a, pltpu.HBM)


def _sem_pairs(n):
    return (pltpu.SemaphoreType.DMA((n * (N_DEV - 1),)), pltpu.SemaphoreType.DMA((n * (N_DEV - 1),)))


def _sem(i, m):
    return i * (N_DEV - 1) + m - 1


def _gather_copy(g_ref, i, m, send_sems, recv_sems, origin):
    peer, _ = _peer(m)
    return pltpu.make_async_remote_copy(
        src_ref=g_ref.at[origin], dst_ref=g_ref.at[origin],
        send_sem=send_sems.at[_sem(i, m)], recv_sem=recv_sems.at[_sem(i, m)],
        device_id=peer, device_id_type=MESH)


GATHER_MASKS = (1, 2, 4, 6)
FORWARD_MASKS = (2, 4, 6)


ALL_MASKS = tuple(range(1, N_DEV))


def gather_start(gs, after, name, masks=GATHER_MASKS):
    n = len(gs)

    def body(*refs):
        g_in = refs[:n]
        send_sems, recv_sems = refs[n + 1], refs[n + 2]
        token = refs[-1]
        me = _my_index()
        for i in range(n):
            for m in masks:
                _gather_copy(g_in[i], i, m, send_sems, recv_sems, me).start()
        token[...] = jnp.zeros_like(token)

    outs = pl.pallas_call(
        body, name=name,
        out_shape=(*_sem_pairs(n), *[pltpu.HBM(g.shape, g.dtype) for g in gs],
                   jax.ShapeDtypeStruct((8, 128), F32)),
        in_specs=[HBM_SPEC] * n + [ANY_SPEC],
        out_specs=(SEM_SPEC, SEM_SPEC, *[HBM_SPEC] * n, VMEM_SPEC),
        input_output_aliases={i: 2 + i for i in range(n)},
        compiler_params=pltpu.CompilerParams(has_side_effects=SIDE_EFFECT),
    )(*[_hbm(g) for g in gs], after)
    return outs[0], outs[1], list(outs[2:2 + n]), outs[-1]


def gather_start_groups(groups, after, name, masks=GATHER_MASKS):
    sizes = [len(g) for g in groups]
    flat = [a for g in groups for a in g]
    n, ng = len(flat), len(groups)
    masks_of = list(masks) if isinstance(masks, list) else [masks] * ng

    def body(*refs):
        g_in = refs[:n]
        sems = refs[n + 1:n + 1 + 2 * ng]
        token = refs[-1]
        me = _my_index()
        pos = 0
        for k, size in enumerate(sizes):
            for i in range(size):
                for m in masks_of[k]:
                    _gather_copy(g_in[pos + i], i, m, sems[2 * k], sems[2 * k + 1], me).start()
            pos += size
        token[...] = jnp.zeros_like(token)

    outs = pl.pallas_call(
        body, name=name,
        out_shape=(*[s for size in sizes for s in _sem_pairs(size)],
                   *[pltpu.HBM(g.shape, g.dtype) for g in flat], jax.ShapeDtypeStruct((8, 128), F32)),
        in_specs=[HBM_SPEC] * n + [ANY_SPEC],
        out_specs=(*[SEM_SPEC] * (2 * ng), *[HBM_SPEC] * n, VMEM_SPEC),
        input_output_aliases={i: 2 * ng + i for i in range(n)},
        compiler_params=pltpu.CompilerParams(has_side_effects=SIDE_EFFECT),
    )(*[_hbm(g) for g in flat], after)
    result, pos = [], 2 * ng
    for k, size in enumerate(sizes):
        result.append((outs[2 * k], outs[2 * k + 1], list(outs[pos:pos + size])))
        pos += size
    return result, outs[-1]


def gather_wait(gs, send_sems, recv_sems, after, name, masks=GATHER_MASKS):
    n = len(gs)

    def body(*refs):
        g_in = refs[:n]
        send, recv = refs[n], refs[n + 1]
        me = _my_index()
        for i in range(n):
            for m in masks:
                _, pidx = _peer(m)
                _gather_copy(g_in[i], i, m, send, recv, me).wait_send()
                _gather_copy(g_in[i], i, m, send, recv, pidx).wait_recv()

    outs = pl.pallas_call(
        body, name=name,
        out_shape=[pltpu.HBM(g.shape, g.dtype) for g in gs],
        in_specs=[HBM_SPEC] * n + [SEM_SPEC, SEM_SPEC, ANY_SPEC],
        out_specs=[HBM_SPEC] * n,
        input_output_aliases={i: i for i in range(n)},
        compiler_params=pltpu.CompilerParams(has_side_effects=SIDE_EFFECT),
    )(*gs, send_sems, recv_sems, after)
    return list(outs)


def sibling_forward(gs, name):
    n = len(gs)
    nf = len(FORWARD_MASKS)

    def body(*refs):
        g_in = refs[:n]
        send_sems, recv_sems = refs[2 * n:]
        x, y, c = lax.axis_index("x"), lax.axis_index("y"), lax.axis_index("c")
        sibling = (x, y, 1 - c)

        def copy(i, k, origin):
            return pltpu.make_async_remote_copy(
                src_ref=g_in[i].at[origin], dst_ref=g_in[i].at[origin],
                send_sem=send_sems.at[i * nf + k], recv_sem=recv_sems.at[i * nf + k],
                device_id=sibling, device_id_type=MESH)
        sends = []
        for i in range(n):
            for k, m in enumerate(FORWARD_MASKS):
                _, origin = _peer(m)
                cp = copy(i, k, origin)
                cp.start()
                sends.append(cp)
        for i in range(n):
            for k, m in enumerate(FORWARD_MASKS):
                _, origin = _peer(m ^ 1)
                copy(i, k, origin).wait_recv()
        for cp in sends:
            cp.wait_send()

    outs = pl.pallas_call(
        body, name=name,
        out_shape=[jax.ShapeDtypeStruct(g.shape, g.dtype) for g in gs],
        in_specs=[HBM_SPEC] * n, out_specs=[HBM_SPEC] * n,
        input_output_aliases={i: i for i in range(n)},
        scratch_shapes=[pltpu.SemaphoreType.DMA((n * nf,)), pltpu.SemaphoreType.DMA((n * nf,))],
    )(*gs)
    return list(outs)


def _forward_copy(g_ref, i, k, send_sems, recv_sems, origin):
    sibling = (lax.axis_index("x"), lax.axis_index("y"), 1 - lax.axis_index("c"))
    slot = i * len(FORWARD_MASKS) + k
    return pltpu.make_async_remote_copy(
        src_ref=g_ref.at[origin], dst_ref=g_ref.at[origin],
        send_sem=send_sems.at[slot], recv_sem=recv_sems.at[slot],
        device_id=sibling, device_id_type=MESH)


def forward_start(gs, after, name):
    n = len(gs)
    nsem = n * len(FORWARD_MASKS)

    def body(*refs):
        g_in = refs[:n]
        send_sems, recv_sems = refs[n + 1], refs[n + 2]
        token = refs[-1]
        for i in range(n):
            for k, m in enumerate(FORWARD_MASKS):
                _, origin = _peer(m)
                _forward_copy(g_in[i], i, k, send_sems, recv_sems, origin).start()
        token[...] = jnp.zeros_like(token)

    outs = pl.pallas_call(
        body, name=name,
        out_shape=(pltpu.SemaphoreType.DMA((nsem,)), pltpu.SemaphoreType.DMA((nsem,)),
                   *[pltpu.HBM(g.shape, g.dtype) for g in gs], jax.ShapeDtypeStruct((8, 128), F32)),
        in_specs=[HBM_SPEC] * n + [ANY_SPEC],
        out_specs=(SEM_SPEC, SEM_SPEC, *[HBM_SPEC] * n, VMEM_SPEC),
        input_output_aliases={i: 2 + i for i in range(n)},
        compiler_params=pltpu.CompilerParams(has_side_effects=SIDE_EFFECT),
    )(*[_hbm(g) for g in gs], after)
    return outs[0], outs[1], list(outs[2:2 + n]), outs[-1]


def forward_wait(gs, send_sems, recv_sems, after, name):
    n = len(gs)

    def body(*refs):
        g_in = refs[:n]
        send, recv = refs[n], refs[n + 1]
        for i in range(n):
            for k, m in enumerate(FORWARD_MASKS):
                _, mine = _peer(m)
                _, theirs = _peer(m ^ 1)
                _forward_copy(g_in[i], i, k, send, recv, mine).wait_send()
                _forward_copy(g_in[i], i, k, send, recv, theirs).wait_recv()

    outs = pl.pallas_call(
        body, name=name,
        out_shape=[pltpu.HBM(g.shape, g.dtype) for g in gs],
        in_specs=[HBM_SPEC] * n + [SEM_SPEC, SEM_SPEC, ANY_SPEC],
        out_specs=[HBM_SPEC] * n,
        input_output_aliases={i: i for i in range(n)},
        compiler_params=pltpu.CompilerParams(has_side_effects=SIDE_EFFECT),
    )(*gs, send_sems, recv_sems, after)
    return list(outs)


def _scatter_copy(g_ref, l_ref, i, m, send_sems, recv_sems):
    peer, pidx = _peer(m)
    return pltpu.make_async_remote_copy(
        src_ref=g_ref.at[pidx], dst_ref=l_ref.at[m - 1],
        send_sem=send_sems.at[_sem(i, m)], recv_sem=recv_sems.at[_sem(i, m)],
        device_id=peer, device_id_type=MESH)


def scatter_start(grads, after, name):
    n = len(grads)
    lands = [lax.empty((N_DEV - 1,) + g.shape[1:], g.dtype) for g in grads]

    def body(*refs):
        g_in, l_in = refs[:n], refs[n:2 * n]
        send_sems, recv_sems = refs[2 * n + 1], refs[2 * n + 2]
        token = refs[-1]
        for i in range(n):
            for m in range(1, N_DEV):
                _scatter_copy(g_in[i], l_in[i], i, m, send_sems, recv_sems).start()
        token[...] = jnp.zeros_like(token)

    outs = pl.pallas_call(
        body, name=name,
        out_shape=(*_sem_pairs(n), *[pltpu.HBM(g.shape, g.dtype) for g in grads],
                   *[pltpu.HBM(l.shape, l.dtype) for l in lands], jax.ShapeDtypeStruct((8, 128), F32)),
        in_specs=[HBM_SPEC] * (2 * n) + [ANY_SPEC],
        out_specs=(SEM_SPEC, SEM_SPEC, *[HBM_SPEC] * (2 * n), VMEM_SPEC),
        input_output_aliases={i: 2 + i for i in range(2 * n)},
        compiler_params=pltpu.CompilerParams(has_side_effects=SIDE_EFFECT),
    )(*[_hbm(g) for g in grads], *[_hbm(l) for l in lands], after)
    return outs[0], outs[1], list(outs[2:2 + n]), list(outs[2 + n:2 + 2 * n]), outs[-1]


def scatter_wait(grads, lands, send_sems, recv_sems, after, name):
    n = len(grads)

    def body(*refs):
        g_in, l_in = refs[:n], refs[n:2 * n]
        send, recv = refs[2 * n], refs[2 * n + 1]
        for i in range(n):
            for m in range(1, N_DEV):
                cp = _scatter_copy(g_in[i], l_in[i], i, m, send, recv)
                cp.wait_send()
                cp.wait_recv()

    outs = pl.pallas_call(
        body, name=name,
        out_shape=[pltpu.HBM(a.shape, a.dtype) for a in list(grads) + list(lands)],
        in_specs=[HBM_SPEC] * (2 * n) + [SEM_SPEC, SEM_SPEC, ANY_SPEC],
        out_specs=[HBM_SPEC] * (2 * n),
        input_output_aliases={i: i for i in range(2 * n)},
        compiler_params=pltpu.CompilerParams(has_side_effects=SIDE_EFFECT),
    )(*grads, *lands, send_sems, recv_sems, after)
    return list(outs[:n]), list(outs[n:])


def sum_slots(g, name):
    _, r, c = g.shape

    def body(g_ref, out_ref):
        acc = g_ref[0]
        for p in range(1, N_DEV):
            acc = acc + g_ref[p]
        out_ref[...] = acc

    return pl.pallas_call(
        body, name=name, out_shape=jax.ShapeDtypeStruct((r, c), F32),
        in_specs=[VMEM_SPEC], out_specs=VMEM_SPEC,
        compiler_params=pltpu.CompilerParams(vmem_limit_bytes=VMEM_LIMIT),
    )(g)


def _sigmoid(v):
    return 1.0 / (1.0 + jnp.exp(-v))


def _rms_fwd(xf, g):
    r = lax.rsqrt(jnp.mean(xf * xf, axis=-1, keepdims=True) + EPS)
    return xf * r, r


def _rms_bwd(xhat, r, g, dy):
    dg = jnp.sum(dy * xhat, axis=0, keepdims=True)
    dxh = dy * g
    dx = r * (dxh - xhat * jnp.mean(dxh * xhat, axis=-1, keepdims=True))
    return dx, dg


def _ln_stats(v):
    mu = jnp.mean(v, axis=-1, keepdims=True)
    vc = v - mu
    r = lax.rsqrt(jnp.mean(vc * vc, axis=-1, keepdims=True) + EPS)
    return vc * r, r


def _ln_bwd(xhat, r, dxh):
    return r * (dxh - jnp.mean(dxh, axis=-1, keepdims=True)
                - xhat * jnp.mean(dxh * xhat, axis=-1, keepdims=True))


def _dot(a, b):
    return jnp.dot(a, b, preferred_element_type=F32)


def _dot_nt(a, b):
    return lax.dot_general(a, b, (((1,), (1,)), ((), ())), preferred_element_type=F32)


def _dot_tn(a, b):
    return lax.dot_general(a, b, (((0,), (0,)), ((), ())), preferred_element_type=F32)


def _full_weight(w_ref, kind):
    assert kind == "row"
    p, a, b = w_ref.shape
    return w_ref[...].reshape(p * a, b)


def _wspec(wg):
    return pl.BlockSpec(wg.shape, lambda *_: (0, 0, 0))


def mm_rows(a, wg, kind, *, gain=None, out_dtype=F32, name, tm=None):
    m, k = a.shape
    p, wa, wb = wg.shape
    n = p * wb if kind == "col" else wb
    tm = _row_tile(m, tm)
    has_gain = gain is not None

    def body(*refs):
        refs = list(refs)
        a_ref = refs.pop(0)
        g_ref = refs.pop(0) if has_gain else None
        w_ref = refs.pop(0)
        o_ref = refs.pop(0)
        if has_gain:
            xhat, _ = _rms_fwd(a_ref[...].astype(F32), None)
            h = (xhat * g_ref[...]).astype(BF16)
        else:
            h = a_ref[...].astype(BF16)
        if kind == "col":
            for j in range(p):
                o_ref[:, j * wb:(j + 1) * wb] = _dot(h, w_ref[j]).astype(out_dtype)
        else:
            o_ref[...] = _dot(h, _full_weight(w_ref, "row")).astype(out_dtype)

    operands = [a]
    in_specs = [pl.BlockSpec((tm, k), lambda i: (i, 0))]
    if has_gain:
        operands.append(gain.reshape(1, k))
        in_specs.append(pl.BlockSpec((1, k), lambda i: (0, 0)))
    operands.append(wg)
    in_specs.append(_wspec(wg))
    return pl.pallas_call(
        body, name=name, grid=(m // tm,),
        out_shape=jax.ShapeDtypeStruct((m, n), out_dtype),
        in_specs=in_specs, out_specs=pl.BlockSpec((tm, n), lambda i: (i, 0)),
        compiler_params=_params("parallel"),
    )(*operands)


def mm_nt(dz, wg, kind, *, x=None, gain=None, dx_in=None, after=None, name, tm=None):
    m, n = dz.shape
    p, wa, wb = wg.shape
    k = wa if kind == "col" else p * wa
    tm = _row_tile(m, tm)
    epi = x is not None
    has_dx = dx_in is not None
    has_after = after is not None

    def body(*refs):
        refs = list(refs)
        dz_ref, w_ref = refs.pop(0), refs.pop(0)
        if epi:
            x_ref, g_ref = refs.pop(0), refs.pop(0)
            dxi_ref = refs.pop(0) if has_dx else None
        if has_after:
            refs.pop(0)
        if epi:
            dx_ref, h_ref, dg_ref = refs
        else:
            (da_ref,) = refs
        dzb = dz_ref[...].astype(BF16)
        if kind == "col":
            da = _dot_nt(dzb[:, 0:wb], w_ref[0])
            for j in range(1, p):
                da = da + _dot_nt(dzb[:, j * wb:(j + 1) * wb], w_ref[j])
        else:
            da = _dot_nt(dzb, _full_weight(w_ref, "row"))
        if not epi:
            da_ref[...] = da
            return
        g = g_ref[...]
        xhat, r = _rms_fwd(x_ref[...].astype(F32), None)
        h_ref[...] = (xhat * g).astype(BF16)
        dx, dg = _rms_bwd(xhat, r, g, da)
        if has_dx:
            dx = dx + dxi_ref[...]
        dx_ref[...] = dx

        @pl.when(pl.program_id(0) == 0)
        def _():
            dg_ref[...] = jnp.zeros_like(dg_ref)
        dg_ref[...] += dg

    row = lambda i: (i, 0)
    operands = [dz, wg]
    in_specs = [pl.BlockSpec((tm, n), row), _wspec(wg)]
    if epi:
        operands += [x, gain.reshape(1, k)]
        in_specs += [pl.BlockSpec((tm, k), row), pl.BlockSpec((1, k), lambda i: (0, 0))]
        if has_dx:
            operands.append(dx_in)
            in_specs.append(pl.BlockSpec((tm, k), row))
        out_shape = [jax.ShapeDtypeStruct((m, k), F32), jax.ShapeDtypeStruct((m, k), BF16),
                     jax.ShapeDtypeStruct((1, k), F32)]
        out_specs = [pl.BlockSpec((tm, k), row), pl.BlockSpec((tm, k), row),
                     pl.BlockSpec((1, k), lambda i: (0, 0))]
    else:
        out_shape = jax.ShapeDtypeStruct((m, k), F32)
        out_specs = pl.BlockSpec((tm, k), row)
    if has_after:
        operands.append(after)
        in_specs.append(ANY_SPEC)
    return pl.pallas_call(
        body, name=name, grid=(m // tm,), out_shape=out_shape,
        in_specs=in_specs, out_specs=out_specs,
        compiler_params=_params("arbitrary"),
    )(*operands)


def mm_tn(a, b, *, nb, a_spec, b_spec, ka, nbk, tm, m, scale=1.0, out_dtype=BF16, col_slots=1,
          after=None, name):
    ni = m // tm
    assert col_slots == 1 or nb == 1
    cw = nbk // col_slots
    extra = [] if after is None else [after]

    def body(a_ref, b_ref, *rest):
        o_ref, acc = rest[len(extra):]
        i = pl.program_id(1)

        @pl.when(i == 0)
        def _():
            acc[...] = jnp.zeros_like(acc)
        acc[...] += _dot_tn(a_ref[...].astype(BF16), b_ref[...].astype(BF16))

        @pl.when(i == ni - 1)
        def _():
            if col_slots == 1:
                o_ref[...] = (acc[...] * scale).astype(out_dtype)
            else:
                for j in range(col_slots):
                    o_ref[j] = (acc[:, j * cw:(j + 1) * cw] * scale).astype(out_dtype)

    if col_slots == 1:
        out_shape = jax.ShapeDtypeStruct((nb, ka, nbk), out_dtype)
        out_spec = pl.BlockSpec((None, ka, nbk), lambda s, i: (s, 0, 0))
    else:
        out_shape = jax.ShapeDtypeStruct((col_slots, ka, cw), out_dtype)
        out_spec = pl.BlockSpec((col_slots, ka, cw), lambda s, i: (0, 0, 0))
    return pl.pallas_call(
        body, name=name, grid=(nb, ni), out_shape=out_shape,
        in_specs=[a_spec, b_spec] + [ANY_SPEC] * len(extra), out_specs=out_spec,
        scratch_shapes=[pltpu.VMEM((ka, nbk), F32)],
        compiler_params=_params("parallel", "arbitrary"),
    )(a, b, *extra)


def _ffn_specs(w_in_g, w_out_g, d):
    nf = w_in_g.shape[1]
    hr = w_out_g.shape[1]
    assert 2 * hr == nf
    w_in5 = w_in_g.reshape(2, 4, nf, d)
    w_out5 = w_out_g.reshape(4, 2, hr, d)
    in_spec = pl.BlockSpec((2, None, nf, d), lambda i, j: (0, j, 0, 0))
    out_spec = pl.BlockSpec((None, 2, hr, d), lambda i, j: (j, 0, 0, 0))
    return w_in5, w_out5, in_spec, out_spec, nf


def ffn_fwd(x, gain, w_in_g, w_out_g, *, name, tm=None):
    t, d = x.shape
    tm = _row_tile(t, tm)
    w_in5, w_out5, wi_spec, wo_spec, nf = _ffn_specs(w_in_g, w_out_g, d)

    def body(x_ref, g_ref, wi_ref, wo_ref, o_ref, gu_ref, h_scr, acc):
        j = pl.program_id(1)

        @pl.when(j == 0)
        def _():
            xhat, _ = _rms_fwd(x_ref[...], None)
            h_scr[...] = (xhat * g_ref[...]).astype(BF16)
            acc[...] = jnp.zeros_like(acc)
        h = h_scr[...]
        gt = _dot_nt(h, wi_ref[0])
        up = _dot_nt(h, wi_ref[1])
        gu_ref[0] = gt.astype(BF16)
        gu_ref[1] = up.astype(BF16)
        act = (gt * _sigmoid(gt) * up).astype(BF16)
        acc[...] += _dot(act, wo_ref[...].reshape(nf, d))

        @pl.when(j == 3)
        def _():
            o_ref[...] = x_ref[...] + 0.5 * acc[...]

    return pl.pallas_call(
        body, name=name, grid=(t // tm, 4),
        out_shape=[jax.ShapeDtypeStruct((t, d), F32), jax.ShapeDtypeStruct((2, 4, t, nf), BF16)],
        in_specs=[pl.BlockSpec((tm, d), lambda i, j: (i, 0)),
                  pl.BlockSpec((1, d), lambda i, j: (0, 0)), wi_spec, wo_spec],
        out_specs=[pl.BlockSpec((tm, d), lambda i, j: (i, 0)),
                   pl.BlockSpec((2, None, tm, nf), lambda i, j: (0, j, i, 0))],
        scratch_shapes=[pltpu.VMEM((tm, d), BF16), pltpu.VMEM((tm, d), F32)],
        compiler_params=_params("parallel", "arbitrary"),
    )(x, gain.reshape(1, d), w_in5, w_out5)


def ffn_bwd_rows(x, dy, gu, gain, w_in_g, w_out_g, after, *, name, tm=None):
    t, d = x.shape
    tm = _row_tile(t, tm)
    w_in5, w_out5, wi_spec, wo_spec, nf = _ffn_specs(w_in_g, w_out_g, d)

    def body(x_ref, dy_ref, gu_ref, g_ref, wi_ref, wo_ref, after_ref, dx_ref, h_ref, act_ref, dgu_ref, dg_ref,
             dyh_scr, dh_acc):
        i, j = pl.program_id(0), pl.program_id(1)

        @pl.when(j == 0)
        def _():
            xhat, _ = _rms_fwd(x_ref[...], None)
            h_ref[...] = (xhat * g_ref[...]).astype(BF16)
            dyh_scr[...] = (0.5 * dy_ref[...]).astype(BF16)
            dh_acc[...] = jnp.zeros_like(dh_acc)
        wo = wo_ref[...].reshape(nf, d)

        def gates(rows):
            gt = gu_ref[0, rows].astype(F32)
            up = gu_ref[1, rows].astype(F32)
            sg = _sigmoid(gt)
            silu = gt * sg
            act_ref[rows] = (silu * up).astype(BF16)
            return up * (sg * (1.0 + gt * (1.0 - sg))), silu

        def grads(rows, dact, dsilu_up, silu):
            dgt = (dact * dsilu_up).astype(BF16)
            dup = (dact * silu).astype(BF16)
            dgu_ref[0, rows] = dgt
            dgu_ref[1, rows] = dup
            return dgt, dup

        sub = tm // FFN_BWD_SPLIT
        parts = [slice(k * sub, (k + 1) * sub) for k in range(FFN_BWD_SPLIT)]
        dact = _dot_nt(dyh_scr[parts[0]], wo)
        gate = gates(parts[0])
        for k, rows in enumerate(parts):
            if k + 1 < len(parts):
                dact_next = _dot_nt(dyh_scr[parts[k + 1]], wo)
            dgt, dup = grads(rows, dact, *gate)
            dh_acc[rows] += _dot(dgt, wi_ref[0]) + _dot(dup, wi_ref[1])
            if k + 1 < len(parts):
                gate = gates(parts[k + 1])
                dact = dact_next

        @pl.when(j == 3)
        def _():
            g = g_ref[...]
            xhat, r = _rms_fwd(x_ref[...], None)
            dx, dg = _rms_bwd(xhat, r, g, dh_acc[...])
            dx_ref[...] = dy_ref[...] + dx

            @pl.when(i == 0)
            def _():
                dg_ref[...] = jnp.zeros_like(dg_ref)
            dg_ref[...] += dg

    row = lambda i, j: (i, 0)
    return pl.pallas_call(
        body, name=name, grid=(t // tm, 4),
        out_shape=[jax.ShapeDtypeStruct((t, d), F32), jax.ShapeDtypeStruct((t, d), BF16),
                   jax.ShapeDtypeStruct((4, t, nf), BF16), jax.ShapeDtypeStruct((2, 4, t, nf), BF16),
                   jax.ShapeDtypeStruct((1, d), F32), jax.ShapeDtypeStruct((t, d), BF16)],
        in_specs=[pl.BlockSpec((tm, d), row), pl.BlockSpec((tm, d), row),
                  pl.BlockSpec((2, None, tm, nf), lambda i, j: (0, j, i, 0)),
                  pl.BlockSpec((1, d), lambda i, j: (0, 0)), wi_spec, wo_spec, ANY_SPEC],
        out_specs=[pl.BlockSpec((tm, d), row), pl.BlockSpec((tm, d), row),
                   pl.BlockSpec((None, tm, nf), lambda i, j: (j, i, 0)),
                   pl.BlockSpec((2, None, tm, nf), lambda i, j: (0, j, i, 0)),
                   pl.BlockSpec((1, d), lambda i, j: (0, 0)), pl.BlockSpec((tm, d), row)],
        scratch_shapes=[pltpu.VMEM((tm, d), F32)],
        compiler_params=_params("arbitrary", "arbitrary"),
    )(x, dy, gu, gain.reshape(1, d), w_in5, w_out5, after)


def ffn_grad_w_in(h, dgu, after, *, name):
    t, d = h.shape
    nf = dgu.shape[-1]
    tm = _row_tile(t, TN_TILE)
    return mm_tn(dgu.reshape(8, t, nf), h, nb=8, ka=nf, nbk=d, tm=tm, m=t, after=after,
                 a_spec=pl.BlockSpec((None, tm, nf), lambda s, i: (s, i, 0)),
                 b_spec=pl.BlockSpec((tm, d), lambda s, i: (i, 0)), name=name)


def ffn_grad_w_out(act, dyh, after, *, name):
    _, t, nf = act.shape
    d = dyh.shape[1]
    tm = _row_tile(t, TN_TILE)
    d_w_out = mm_tn(act, dyh, nb=4, ka=nf, nbk=d, tm=tm, m=t, after=after,
                    a_spec=pl.BlockSpec((None, tm, nf), lambda s, i: (s, i, 0)),
                    b_spec=pl.BlockSpec((tm, d), lambda s, i: (i, 0)), name=name)
    return d_w_out.reshape(8, nf // 2, d)


def _lane_group(shape):
    return lax.shift_right_logical(lax.broadcasted_iota(jnp.int32, shape, 1), 6)


def _pool_count(t0, rows):
    t = (t0 + lax.broadcasted_iota(jnp.int32, (rows, MIX_W), 0) + 1).astype(F32)
    return jnp.minimum(t, _by_group(_lane_group((rows, MIX_W)), 2.0, 4.0, 8.0, 16.0))


def _by_group(grp, v0, v1, v2, v3):
    return jnp.where(grp == 0, v0, jnp.where(grp == 1, v1, jnp.where(grp == 2, v2, v3)))


def _sgu_mix(wt_ref, vnc):
    grp = _lane_group((SGU_CHUNK, MIX_W))
    out = jnp.zeros((SGU_CHUNK, MIX_W), F32)
    for hd in range(N_HEADS):
        out = jnp.where(grp == hd, _dot(wt_ref[hd], vnc), out)
    return out


def _pool_fwd(s1, s2, s3, t0, ts, lo):
    h = lo
    s2[h - 24:h + ts] = s1[h - 24:h + ts] + s1[h - 25:h + ts - 1]
    s3[h - 16:h + ts] = s2[h - 16:h + ts] + s2[h - 18:h + ts - 2]
    sum2 = s2[h:h + ts]
    sum4 = s3[h:h + ts]
    s2[h - 8:h + ts] = s3[h - 8:h + ts] + s3[h - 12:h + ts - 4]
    sum8 = s2[h:h + ts]
    sum16 = sum8 + s2[h - 8:h + ts - 8]
    grp = _lane_group((ts, MIX_W))
    return _by_group(grp, sum2, sum4, sum8, sum16) / _pool_count(t0, ts) - s1[h:h + ts]


def _make_shifts(src, sh, rows):
    for b in range(1, 8):
        sh[b, 0:rows] = src[b:b + rows]


def _rows_at(src, sh, start, n):
    a, b = divmod(start, 8)
    return src[8 * a:8 * a + n] if b == 0 else sh[b, 8 * a:8 * a + n]


def mixer_fwd(z, sconv, cconv, vecs, wt, bexp, pbd, x_res, wmo_g, *, name, ts=None):
    t = z.shape[0]
    ts = _row_tile(t, MIX_TILE if ts is None else ts)
    hl = HALO
    w = MIX_W
    nch = ts // SGU_CHUNK

    def body(zc, zp, sconv_ref, cconv_ref, vec_ref, wt_ref, bexp_ref, pbd_ref, xr_ref, wmo_ref,
             y_ref, xo_ref, s1, s2, s3, sh):
        i = pl.program_id(0)
        has_prev = i > 0

        def col(ref, c):
            return ref[:, c * w:(c + 1) * w]

        def prev(c):
            return jnp.where(has_prev, col(zp, c), 0.0)

        s1[0:hl] = prev(1) * prev(2)
        s1[hl:hl + ts] = col(zc, 1) * col(zc, 2)
        cv = sconv_ref[0:1] * s1[hl - 2:hl - 2 + ts]
        for k in range(1, SCONV_K):
            cv = cv + sconv_ref[k:k + 1] * s1[hl - 2 + k:hl - 2 + k + ts]
        y_ref[:, 0:w] = (col(zc, 0) * cv).astype(BF16)

        xhat, _ = _ln_stats(col(zc, 4))
        vn = (xhat * vec_ref[0:1]).astype(BF16)
        for c in range(nch):
            rows = slice(c * SGU_CHUNK, (c + 1) * SGU_CHUNK)
            mixed = _sgu_mix(wt_ref, vn[rows]) + bexp_ref[...]
            y_ref[rows, w:2 * w] = (zc[rows, 3 * w:4 * w] * mixed).astype(BF16)

        s1[0:hl] = prev(5) * _sigmoid(prev(6))
        s1[hl:hl + ts] = col(zc, 5) * _sigmoid(col(zc, 6))
        off = hl - (CCONV_K - 1)
        _make_shifts(s1, sh, hl + ts - 8)
        cv = cconv_ref[0:1] * _rows_at(s1, sh, off, ts)
        for k in range(1, CCONV_K):
            cv = cv + cconv_ref[k:k + 1] * _rows_at(s1, sh, off + k, ts)
        xhat, _ = _ln_stats(cv)
        ln = xhat * vec_ref[1:2] + vec_ref[2:3]
        y_ref[:, 2 * w:3 * w] = (ln * _sigmoid(ln)).astype(BF16)

        s1[0:hl] = prev(7)
        s1[hl:hl + ts] = col(zc, 7)
        pooled = _pool_fwd(s1, s2, s3, i * ts, ts, hl)
        y_ref[:, 3 * w:4 * w] = (_dot(pooled.astype(BF16), pbd_ref[...]) * vec_ref[3:4]).astype(BF16)

        xo_ref[...] = xr_ref[...] + _dot(y_ref[...], _full_weight(wmo_ref, "row"))

    full = lambda shape: pl.BlockSpec(shape, lambda i: (0,) * len(shape))
    row = lambda i: (i, 0)
    return pl.pallas_call(
        body, name=name, grid=(t // ts,),
        out_shape=[jax.ShapeDtypeStruct((t, 4 * w), BF16), jax.ShapeDtypeStruct((t, 4 * w), F32)],
        in_specs=[pl.BlockSpec((ts, 8 * w), row),
                  pl.BlockSpec((hl, 8 * w), lambda i: (jnp.maximum(i * (ts // hl) - 1, 0), 0)),
                  full((8, w)), full((32, w)), full((8, w)), full((N_HEADS, SGU_CHUNK, SGU_CHUNK)),
                  full((SGU_CHUNK, w)), full((w, w)), pl.BlockSpec((ts, 4 * w), row), _wspec(wmo_g)],
        out_specs=[pl.BlockSpec((ts, 4 * w), row), pl.BlockSpec((ts, 4 * w), row)],
        scratch_shapes=[pltpu.VMEM((hl + ts, w), F32)] * 3 + [pltpu.VMEM((8, hl + ts, w), F32)],
        compiler_params=_params("parallel"),
    )(z, z, sconv, cconv, vecs, wt, bexp, pbd, x_res, wmo_g)


def mixer_bwd(z, dx, wmo_g, sconv, cconv, vecs, wt, bexp, pbd, *, name, ts=None):
    t = z.shape[0]
    ts = _row_tile(t, MIX_TILE if ts is None else ts)
    hl = HALO
    w = MIX_W
    nch = ts // SGU_CHUNK
    ni = t // ts
    ext = ts + hl

    def body(zc, zp, zn, dxc, dxn_, wmo_ref, sconv_ref, cconv_ref, vec_ref, wt_ref, bexp_ref, pbd_ref,
             dz_ref, gvec_ref, gcc_ref, gwt_ref, gb_ref, gpbd_ref, s1, s2, s3, sh1, sh3, dyc, dyn):
        i = pl.program_id(0)
        has_prev = i > 0
        has_next = i < ni - 1
        wmo = _full_weight(wmo_ref, "row")
        dyc[...] = _dot_nt(dxc[...].astype(BF16), wmo)
        dyn[...] = _dot_nt(dxn_[...].astype(BF16), wmo)

        @pl.when(i == 0)
        def _():
            gvec_ref[...] = jnp.zeros_like(gvec_ref)
            gcc_ref[...] = jnp.zeros_like(gcc_ref)
            gwt_ref[...] = jnp.zeros_like(gwt_ref)
            gb_ref[...] = jnp.zeros_like(gb_ref)
            gpbd_ref[...] = jnp.zeros_like(gpbd_ref)

        def col(ref, c):
            return ref[:, c * w:(c + 1) * w]

        def prev(c):
            return jnp.where(has_prev, col(zp, c), 0.0)

        def nxt(c):
            return jnp.where(has_next, col(zn, c), 0.0)

        def dnext(c):
            return jnp.where(has_next, col(dyn, c), 0.0)

        def rowsum(v):
            return jnp.sum(v, axis=0, keepdims=True)

        s1[0:hl] = prev(1) * prev(2)
        s1[hl:hl + ts] = col(zc, 1) * col(zc, 2)
        s1[hl + ts:hl + ts + hl] = nxt(1) * nxt(2)
        cv = sconv_ref[0:1] * s1[hl - 2:hl - 2 + ts]
        for k in range(1, SCONV_K):
            cv = cv + sconv_ref[k:k + 1] * s1[hl - 2 + k:hl - 2 + k + ts]
        dya = col(dyc, 0)
        dz_ref[:, 0:w] = (dya * cv).astype(BF16)
        s2[0:ts] = dya * col(zc, 0)
        s2[ts:ext] = dnext(0) * nxt(0)
        dv = sconv_ref[0:1] * s2[2:2 + ts]
        for k in range(1, SCONV_K):
            dv = dv + sconv_ref[k:k + 1] * s2[2 - k:2 - k + ts]
        dz_ref[:, w:2 * w] = (dv * col(zc, 2)).astype(BF16)
        dz_ref[:, 2 * w:3 * w] = (dv * col(zc, 1)).astype(BF16)
        dcv = s2[0:ts]
        for k in range(SCONV_K):
            gvec_ref[k:k + 1] += rowsum(dcv * s1[hl - 2 + k:hl - 2 + k + ts])

        g_sgu = vec_ref[0:1]
        xhat, rstd = _ln_stats(col(zc, 4))
        vn = (xhat * g_sgu).astype(BF16)
        grp = _lane_group((SGU_CHUNK, w))
        lane = lax.broadcasted_iota(jnp.int32, (SGU_CHUNK, SGU_CHUNK), 1)
        tril = lax.broadcasted_iota(jnp.int32, (SGU_CHUNK, SGU_CHUNK), 0) >= lane
        for c in range(nch):
            rows = slice(c * SGU_CHUNK, (c + 1) * SGU_CHUNK)
            vnc = vn[rows]
            mixed = _sgu_mix(wt_ref, vnc) + bexp_ref[...]
            dyb = dyc[rows, w:2 * w]
            dz_ref[rows, 3 * w:4 * w] = (dyb * mixed).astype(BF16)
            dmix = dyb * zc[rows, 3 * w:4 * w]
            dmixb = dmix.astype(BF16)
            dvn = jnp.zeros((SGU_CHUNK, w), F32)
            gb = jnp.zeros((SGU_CHUNK, SGU_CHUNK), F32)
            for hd in range(N_HEADS):
                dvn = jnp.where(grp == hd, _dot_tn(wt_ref[hd], dmixb), dvn)
                dm_h = jnp.where(grp == hd, dmix, 0.0)
                gwt_ref[hd] += jnp.where(tril, _dot_nt(dm_h.astype(BF16), vnc), 0.0)
                gb = gb + jnp.where(lane == hd, jnp.sum(dm_h, axis=1, keepdims=True), 0.0)
            gb_ref[...] += gb
            s3[rows] = dvn
        dvn = s3[0:ts]
        gvec_ref[3:4] += rowsum(dvn * xhat)
        dz_ref[:, 4 * w:5 * w] = _ln_bwd(xhat, rstd, dvn * g_sgu).astype(BF16)

        sig_c = _sigmoid(col(zc, 6))
        s1[0:hl] = prev(5) * _sigmoid(prev(6))
        s1[hl:hl + ts] = col(zc, 5) * sig_c
        s1[hl + ts:hl + ts + hl] = nxt(5) * _sigmoid(nxt(6))
        off = hl - (CCONV_K - 1)
        _make_shifts(s1, sh1, ts + 2 * hl - 8)
        cv = cconv_ref[0:1] * _rows_at(s1, sh1, off, ext)
        for k in range(1, CCONV_K):
            cv = cv + cconv_ref[k:k + 1] * _rows_at(s1, sh1, off + k, ext)
        xhat, rstd = _ln_stats(cv)
        ln = xhat * vec_ref[1:2] + vec_ref[2:3]
        sg = _sigmoid(ln)
        s2[0:ts] = col(dyc, 2)
        s2[ts:ext] = dnext(2)
        dln = s2[0:ext] * (sg * (1.0 + ln * (1.0 - sg)))
        gvec_ref[4:5] += rowsum(dln[0:ts] * xhat[0:ts])
        gvec_ref[5:6] += rowsum(dln[0:ts])
        s3[0:ext] = _ln_bwd(xhat, rstd, dln * vec_ref[1:2])
        _make_shifts(s3, sh3, ext - 8)
        dyg = cconv_ref[0:1] * _rows_at(s3, sh3, CCONV_K - 1, ts)
        for k in range(1, CCONV_K):
            dyg = dyg + cconv_ref[k:k + 1] * _rows_at(s3, sh3, CCONV_K - 1 - k, ts)
        dz_ref[:, 5 * w:6 * w] = (dyg * sig_c).astype(BF16)
        dz_ref[:, 6 * w:7 * w] = (dyg * col(zc, 5) * sig_c * (1.0 - sig_c)).astype(BF16)
        dcv = s3[0:ts]
        for k in range(CCONV_K):
            gcc_ref[k:k + 1] += rowsum(dcv * _rows_at(s1, sh1, off + k, ts))

        scale = vec_ref[3:4]
        s1[0:hl] = prev(7)
        s1[hl:hl + ts] = col(zc, 7)
        pooled = _pool_fwd(s1, s2, s3, i * ts, ts, hl).astype(BF16)
        q0 = _dot(pooled, pbd_ref[...])
        dyd = col(dyc, 3)
        gvec_ref[6:7] += rowsum(dyd * q0)
        dq = (dyd * scale).astype(BF16)
        gpbd_ref[...] += _dot_tn(pooled, dq)
        s1[0:ts] = _dot_nt(dq, pbd_ref[...])
        s1[ts:ext] = _dot_nt((dnext(3) * scale).astype(BF16), pbd_ref[...])
        dpool = s1[0:ts]
        s2[0:ext] = s1[0:ext] / _pool_count(i * ts, ext)
        s3[0:ts + 24] = s2[0:ts + 24] + s2[1:ts + 25]
        f2 = s3[0:ts]
        s2[0:ts + 16] = s3[0:ts + 16] + s3[2:ts + 18]
        f4 = s2[0:ts]
        s3[0:ts + 8] = s2[0:ts + 8] + s2[4:ts + 12]
        f8 = s3[0:ts]
        f16 = f8 + s3[8:ts + 8]
        dz_ref[:, 7 * w:8 * w] = (_by_group(_lane_group((ts, w)), f2, f4, f8, f16) - dpool).astype(BF16)

    full = lambda shape: pl.BlockSpec(shape, lambda i: (0,) * len(shape))
    r = ts // hl
    prev_map = lambda i: (jnp.maximum(i * r - 1, 0), 0)
    next_map = lambda i: (jnp.minimum((i + 1) * r, t // hl - 1), 0)
    return pl.pallas_call(
        body, name=name, grid=(ni,),
        out_shape=[jax.ShapeDtypeStruct((t, 8 * w), BF16), jax.ShapeDtypeStruct((8, w), F32),
                   jax.ShapeDtypeStruct((32, w), F32),
                   jax.ShapeDtypeStruct((N_HEADS, SGU_CHUNK, SGU_CHUNK), F32),
                   jax.ShapeDtypeStruct((SGU_CHUNK, SGU_CHUNK), F32), jax.ShapeDtypeStruct((w, w), F32)],
        in_specs=[pl.BlockSpec((ts, 8 * w), lambda i: (i, 0)),
                  pl.BlockSpec((hl, 8 * w), prev_map), pl.BlockSpec((hl, 8 * w), next_map),
                  pl.BlockSpec((ts, 4 * w), lambda i: (i, 0)), pl.BlockSpec((hl, 4 * w), next_map),
                  _wspec(wmo_g),
                  full((8, w)), full((32, w)), full((8, w)), full((N_HEADS, SGU_CHUNK, SGU_CHUNK)),
                  full((SGU_CHUNK, w)), full((w, w))],
        out_specs=[pl.BlockSpec((ts, 8 * w), lambda i: (i, 0)), full((8, w)), full((32, w)),
                   full((N_HEADS, SGU_CHUNK, SGU_CHUNK)), full((SGU_CHUNK, SGU_CHUNK)), full((w, w))],
        scratch_shapes=[pltpu.VMEM((ts + 2 * hl, w), F32)] * 3 + [pltpu.VMEM((8, ts + 2 * hl, w), F32)] * 2
        + [pltpu.VMEM((ts, 4 * w), F32), pltpu.VMEM((hl, 4 * w), F32)],
        compiler_params=_params("arbitrary"),
    )(z, z, z, dx, dx, wmo_g, sconv, cconv, vecs, wt, bexp, pbd)


def _attn_head(q, kv_ref, hd, d):
    hw = d // N_HEADS
    qh = q[:, hd * hw:(hd + 1) * hw]
    kh = kv_ref[:, hd * hw:(hd + 1) * hw].astype(BF16)
    vh = kv_ref[:, d + hd * hw:d + (hd + 1) * hw].astype(BF16)
    s = _dot_nt(qh, kh) * (1.0 / (hw ** 0.5))
    e = jnp.exp(s - jnp.max(s, axis=-1, keepdims=True))
    p = e / jnp.sum(e, axis=-1, keepdims=True)
    return qh, kh, vh, p


def xattn_fwd(x, gain, kv, wq_g, wo_g, *, name, tm=None):
    t, d = x.shape
    nm = kv.shape[0]
    tm = _row_tile(t, tm)
    hw = d // N_HEADS

    def body(x_ref, g_ref, kv_ref, wq_ref, wo_ref, o_ref):
        xv = x_ref[...]
        xhat, _ = _rms_fwd(xv, None)
        h = (xhat * g_ref[...]).astype(BF16)
        q = _dot(h, _full_weight(wq_ref, "row")).astype(BF16)
        wo = _full_weight(wo_ref, "row")
        out = xv
        for hd in range(N_HEADS):
            _, _, vh, p = _attn_head(q, kv_ref, hd, d)
            oh = _dot(p.astype(BF16), vh).astype(BF16)
            out = out + _dot(oh, wo[hd * hw:(hd + 1) * hw])
        o_ref[...] = out

    row = lambda i: (i, 0)
    return pl.pallas_call(
        body, name=name, grid=(t // tm,),
        out_shape=jax.ShapeDtypeStruct((t, d), F32),
        in_specs=[pl.BlockSpec((tm, d), row), pl.BlockSpec((1, d), lambda i: (0, 0)),
                  pl.BlockSpec((nm, 2 * d), lambda i: (0, 0)), _wspec(wq_g), _wspec(wo_g)],
        out_specs=pl.BlockSpec((tm, d), row),
        compiler_params=_params("parallel"),
    )(x, gain.reshape(1, d), kv, wq_g, wo_g)


def xattn_bwd_rows(x, dxn, gain, kv, wq_g, wo_g, after, *, name, tm=None):
    t, d = x.shape
    nm = kv.shape[0]
    tm = _row_tile(t, tm)
    hw = d // N_HEADS

    def body(x_ref, dxn_ref, g_ref, kv_ref, wq_ref, wo_ref, after_ref,
             dx_ref, h_ref, dq_ref, o_ref, dkv_ref, dg_ref):
        i = pl.program_id(0)

        @pl.when(i == 0)
        def _():
            dkv_ref[...] = jnp.zeros_like(dkv_ref)
            dg_ref[...] = jnp.zeros_like(dg_ref)
        g = g_ref[...]
        xhat, r = _rms_fwd(x_ref[...], None)
        h = (xhat * g).astype(BF16)
        h_ref[...] = h
        wq = _full_weight(wq_ref, "row")
        q = _dot(h, wq).astype(BF16)
        dxn = dxn_ref[...]
        do = _dot_nt(dxn.astype(BF16), _full_weight(wo_ref, "row")).astype(BF16)
        for hd in range(N_HEADS):
            cols = slice(hd * hw, (hd + 1) * hw)
            qh, kh, vh, p = _attn_head(q, kv_ref, hd, d)
            pb = p.astype(BF16)
            o_ref[:, cols] = _dot(pb, vh).astype(BF16)
            doh = do[:, cols]
            dkv_ref[:, d + hd * hw:d + (hd + 1) * hw] += _dot_tn(pb, doh)
            dp = _dot_nt(doh, vh)
            ds = (p * (dp - jnp.sum(dp * p, axis=-1, keepdims=True)) * (1.0 / (hw ** 0.5))).astype(BF16)
            dq_ref[:, cols] = _dot(ds, kh).astype(BF16)
            dkv_ref[:, cols] += _dot_tn(ds, qh)
        dh = _dot_nt(dq_ref[...], wq)
        dx, dg = _rms_bwd(xhat, r, g, dh)
        dx_ref[...] = dxn + dx
        dg_ref[...] += dg

    row = lambda i: (i, 0)
    fix = lambda i: (0, 0)
    return pl.pallas_call(
        body, name=name, grid=(t // tm,),
        out_shape=[jax.ShapeDtypeStruct((t, d), F32), jax.ShapeDtypeStruct((t, d), BF16),
                   jax.ShapeDtypeStruct((t, d), BF16), jax.ShapeDtypeStruct((t, d), BF16),
                   jax.ShapeDtypeStruct((nm, 2 * d), F32), jax.ShapeDtypeStruct((1, d), F32)],
        in_specs=[pl.BlockSpec((tm, d), row), pl.BlockSpec((tm, d), row), pl.BlockSpec((1, d), fix),
                  pl.BlockSpec((nm, 2 * d), fix), _wspec(wq_g), _wspec(wo_g), ANY_SPEC],
        out_specs=[pl.BlockSpec((tm, d), row)] * 4 + [pl.BlockSpec((nm, 2 * d), fix),
                                                      pl.BlockSpec((1, d), fix)],
        compiler_params=_params("arbitrary"),
    )(x, dxn, gain.reshape(1, d), kv, wq_g, wo_g, after)


def loss_head(x, target, gain, *, name, tm=None):
    t, d = x.shape
    tm = _row_tile(t, tm)

    def body(x_ref, t_ref, g_ref, dx_ref, dg_ref, loss_ref):
        @pl.when(pl.program_id(0) == 0)
        def _():
            dg_ref[...] = jnp.zeros_like(dg_ref)
            loss_ref[...] = jnp.zeros_like(loss_ref)
        g = g_ref[...]
        xhat, r = _rms_fwd(x_ref[...], None)
        err = xhat * g - t_ref[...]
        loss_ref[...] += 0.5 * jnp.sum(jnp.sum(err * err, axis=-1, keepdims=True) / d,
                                       axis=0, keepdims=True)
        dx, dg = _rms_bwd(xhat, r, g, err / d)
        dx_ref[...] = dx
        dg_ref[...] += dg

    row = lambda i: (i, 0)
    fix = lambda i: (0, 0)
    return pl.pallas_call(
        body, name=name, grid=(t // tm,),
        out_shape=[jax.ShapeDtypeStruct((t, d), F32), jax.ShapeDtypeStruct((1, d), F32),
                   jax.ShapeDtypeStruct((1, 1), F32)],
        in_specs=[pl.BlockSpec((tm, d), row), pl.BlockSpec((tm, d), row), pl.BlockSpec((1, d), fix)],
        out_specs=[pl.BlockSpec((tm, d), row), pl.BlockSpec((1, d), fix), pl.BlockSpec((1, 1), fix)],
        compiler_params=_params("arbitrary"),
    )(x, target, gain.reshape(1, d))


def _adamw_math(w, g, m, v):
    m = ADAM_B1 * m + (1.0 - ADAM_B1) * g
    v = ADAM_B2 * v + (1.0 - ADAM_B2) * (g * g)
    m_hat = m / (1.0 - ADAM_B1 ** ADAM_STEP)
    v_hat = v / (1.0 - ADAM_B2 ** ADAM_STEP)
    delta = -ADAM_LR * (m_hat / (jnp.sqrt(v_hat) + ADAM_EPS) + ADAM_WD * w)
    return delta, m, v


def adamw_sharded(own, lands, w, m, v, me_arr, *, name):
    nl, r, c = w.shape
    assert nl == len(own) == len(lands) == 2
    tr = next(cand for cand in (*ROW_BLOCKS, r) if r % cand == 0)
    nr = r // tr

    def body(me_ref, o0, o1, l0, l1, w_ref, m_ref, v_ref, g_out, d_out, m_out, v_out):
        def total(o_ref, l_ref):
            acc = o_ref[...].astype(F32)
            for p in range(N_DEV - 1):
                acc = acc + l_ref[p].astype(F32)
            return acc
        g = jnp.where(pl.program_id(0) == 0, total(o0, l0), total(o1, l1))
        delta, mn, vn = _adamw_math(w_ref[...], g, m_ref[...], v_ref[...])
        g_out[...] = g
        d_out[...] = delta
        m_out[...] = mn
        v_out[...] = vn

    row0 = lambda l, i: jnp.where(l == 0, i, nr - 1)
    row1 = lambda l, i: jnp.where(l == 1, i, 0)
    blk = pl.BlockSpec((None, tr, c), lambda l, i, me: (l, i, 0))
    grid_spec = pltpu.PrefetchScalarGridSpec(
        num_scalar_prefetch=1, grid=(nl, nr),
        in_specs=[pl.BlockSpec((None, tr, c), lambda l, i, me: (me[0], row0(l, i), 0)),
                  pl.BlockSpec((None, tr, c), lambda l, i, me: (me[0], row1(l, i), 0)),
                  pl.BlockSpec((N_DEV - 1, tr, c), lambda l, i, me: (0, row0(l, i), 0)),
                  pl.BlockSpec((N_DEV - 1, tr, c), lambda l, i, me: (0, row1(l, i), 0)),
                  blk, blk, blk],
        out_specs=[blk] * 4)
    return pl.pallas_call(
        body, name=name, grid_spec=grid_spec,
        out_shape=[jax.ShapeDtypeStruct((nl, r, c), F32)] * 4,
        compiler_params=_params("arbitrary", "arbitrary"),
    )(me_arr, own[0], own[1], lands[0], lands[1], w, m, v)


def adamw_many(gs, ws, ms, vs, *, name):
    n = len(ws)
    shapes = [w.shape for w in ws]
    as2d = lambda a: a.reshape(1, -1) if a.ndim == 1 else a

    def body(*refs):
        g_r, w_r, m_r, v_r = refs[:n], refs[n:2 * n], refs[2 * n:3 * n], refs[3 * n:4 * n]
        outs = refs[4 * n:]
        for i in range(n):
            delta, mn, vn = _adamw_math(w_r[i][...], g_r[i][...], m_r[i][...], v_r[i][...])
            outs[3 * i][...] = delta
            outs[3 * i + 1][...] = mn
            outs[3 * i + 2][...] = vn

    operands = [as2d(a) for group in (gs, ws, ms, vs) for a in group]
    out_shape = [jax.ShapeDtypeStruct(as2d(w).shape, F32) for w in ws for _ in range(3)]
    outs = pl.pallas_call(
        body, name=name, out_shape=out_shape,
        in_specs=[VMEM_SPEC] * (4 * n), out_specs=[VMEM_SPEC] * (3 * n),
        compiler_params=pltpu.CompilerParams(vmem_limit_bytes=VMEM_LIMIT),
    )(*operands)
    return [tuple(outs[3 * i + k].reshape(shapes[i]) for k in range(3)) for i in range(n)]


def cast_into_slot(a, layer, me_arr, *, name, dtype=None, after=None):
    dtype = BF16 if dtype is None else dtype
    _, r, c = a.shape
    tr = next(cand for cand in (*ROW_BLOCKS, r) if r % cand == 0)
    extra = [] if after is None else [after]

    def body(me_ref, a_ref, *rest):
        rest[-1][...] = a_ref[...].astype(dtype)

    grid_spec = pltpu.PrefetchScalarGridSpec(
        num_scalar_prefetch=1, grid=(r // tr,),
        in_specs=[pl.BlockSpec((None, tr, c), lambda i, me: (layer, i, 0))] + [ANY_SPEC] * len(extra),
        out_specs=pl.BlockSpec((None, tr, c), lambda i, me: (me[0], i, 0)))
    return pl.pallas_call(
        body, name=name, grid_spec=grid_spec,
        out_shape=jax.ShapeDtypeStruct((N_DEV, r, c), dtype),
        compiler_params=_params("parallel"),
    )(me_arr, a, *extra)


def _pack(arrs, rows):
    flat = jnp.concatenate([a.reshape(-1).astype(F32) for a in arrs])
    pad = rows * 128 - flat.shape[0]
    assert pad >= 0
    if pad:
        flat = jnp.concatenate([flat, jnp.zeros((pad,), F32)])
    return flat.reshape(rows, 128)


def _unpack(packed, shapes):
    flat = packed.reshape(-1)
    out, pos = [], 0
    for s in shapes:
        n = 1
        for dim in s:
            n *= dim
        out.append(flat[pos:pos + n].reshape(s))
        pos += n
    return out


def _rows_for(shapes):
    n = 0
    for s in shapes:
        k = 1
        for dim in s:
            k *= dim
        n += k
    return -(-n // 1024) * 8


GATHER_GROUPS = (("ffn1", ("ffn1_w_in", "ffn1_w_out")),
                 ("mid", ("mix_w_in", "mix_w_out", "xattn_wkv", "xattn_wq", "xattn_wo")),
                 ("ffn2", ("ffn2_w_in", "ffn2_w_out")))
SMALL_REPL = ["norm_ffn1", "norm_mix", "sgu_norm_g", "sgu_w", "sgu_b", "cconv_ln_g", "cconv_ln_b",
              "pool_w", "pool_scale", "norm_xattn", "norm_mem", "norm_ffn2", "norm_final"]
SMALL_SHARD = ["sconv_w", "cconv_w"]
TRANSPOSED = ("ffn1_w_in", "ffn2_w_in")
WEIGHTS = ["norm_ffn1", "ffn1_w_in", "ffn1_w_out", "norm_mix", "mix_w_in", "sconv_w", "sgu_norm_g",
           "sgu_w", "sgu_b", "cconv_w", "cconv_ln_g", "cconv_ln_b", "pool_w", "pool_scale", "mix_w_out",
           "norm_xattn", "norm_mem", "xattn_wq", "xattn_wkv", "xattn_wo", "norm_ffn2", "ffn2_w_in",
           "ffn2_w_out", "norm_final"]


def kernel(x, mem, norm_ffn1, ffn1_w_in, ffn1_w_out, norm_mix, mix_w_in, sconv_w, sgu_norm_g, sgu_w, sgu_b, cconv_w, cconv_ln_g, cconv_ln_b, pool_w, pool_scale, mix_w_out, norm_xattn, norm_mem, xattn_wq, xattn_wkv, xattn_wo, norm_ffn2, ffn2_w_in, ffn2_w_out, norm_final, loss_target, m_norm_ffn1, m_ffn1_w_in, m_ffn1_w_out, m_norm_mix, m_mix_w_in, m_sconv_w, m_sgu_norm_g, m_sgu_w, m_sgu_b, m_cconv_w, m_cconv_ln_g, m_cconv_ln_b, m_pool_w, m_pool_scale, m_mix_w_out, m_norm_xattn, m_norm_mem, m_xattn_wq, m_xattn_wkv, m_xattn_wo, m_norm_ffn2, m_ffn2_w_in, m_ffn2_w_out, m_norm_final, v_norm_ffn1, v_ffn1_w_in, v_ffn1_w_out, v_norm_mix, v_mix_w_in, v_sconv_w, v_sgu_norm_g, v_sgu_w, v_sgu_b, v_cconv_w, v_cconv_ln_g, v_cconv_ln_b, v_pool_w, v_pool_scale, v_mix_w_out, v_norm_xattn, v_norm_mem, v_xattn_wq, v_xattn_wkv, v_xattn_wo, v_norm_ffn2, v_ffn2_w_in, v_ffn2_w_out, v_norm_final):
    args = dict(locals())
    wts = {n: args[n] for n in WEIGHTS}
    mom = {n: args["m_" + n] for n in WEIGHTS}
    var = {n: args["v_" + n] for n in WEIGHTS}
    for n in TRANSPOSED:
        wts[n], mom[n], var[n] = (jnp.swapaxes(a, 1, 2) for a in (wts[n], mom[n], var[n]))
    x0 = x[0]
    mem0 = mem[0]
    target = loss_target[0]
    t, d = x0.shape
    nl = norm_ffn1.shape[0]
    w = MIX_W
    me = _my_index()

    me_arr = jnp.reshape(me, (1,)).astype(jnp.int32)

    pending = {}
    masks = GATHER_MASKS
    keys = [(gname, l, members) for l in range(nl) for gname, members in GATHER_GROUPS]
    taps = [cast_into_slot(a.reshape(1, -1, a.shape[-1]), 0, me_arr, name=f"slot_{n}", dtype=F32)
            for n, a in (("sconv_w", sconv_w), ("cconv_w", cconv_w))]
    first = [cast_into_slot(wts[n], keys[0][1], me_arr, name=f"cast_{n}{keys[0][1]}") for n in keys[0][2]]
    started, token = gather_start_groups([taps, first], x0, name="gather_start_first",
                                         masks=[ALL_MASKS, masks])
    casts = [[cast_into_slot(wts[n], l, me_arr, name=f"cast_{n}{l}", after=token) for n in members]
             for gname, l, members in keys[1:]]
    rest, token = gather_start_groups(casts, token, name="gather_start_rest", masks=masks)
    for (gname, l, members), (send, recv, gs) in zip(keys, started[1:] + rest):
        pending[gname, l] = (members, gs, send, recv, masks)
    send, recv, taps = started[0]
    taps = gather_wait(taps, send, recv, token, name="gather_wait_taps", masks=ALL_MASKS)
    sconv_full = jnp.transpose(taps[0].reshape(N_DEV, nl, SCONV_K, -1), (1, 2, 0, 3)).reshape(nl, SCONV_K, w)
    cconv_full = jnp.transpose(taps[1].reshape(N_DEV, nl, CCONV_K, -1), (1, 2, 0, 3)).reshape(nl, CCONV_K, w)
    wg = [dict() for _ in range(nl)]

    handing_over = {}

    def arrive_early(gname, l, after):
        members, gs, send, recv, masks = pending.pop((gname, l))
        gs = gather_wait(gs, send, recv, after, name=f"gather_wait_{gname}{l}", masks=masks)
        fsend, frecv, gs, _ = forward_start(gs, after, name=f"gather_forward_start_{gname}{l}")
        handing_over[gname, l] = (members, gs, fsend, frecv)

    def arrive(gname, l, after):
        if (gname, l) in handing_over:
            members, gs, fsend, frecv = handing_over.pop((gname, l))
            gs = forward_wait(gs, fsend, frecv, after, name=f"gather_forward_wait_{gname}{l}")
        else:
            members, gs, send, recv, masks = pending.pop((gname, l))
            gs = gather_wait(gs, send, recv, after, name=f"gather_wait_{gname}{l}", masks=masks)
            gs = sibling_forward(gs, name=f"gather_forward_{gname}{l}")
        wg[l].update(zip(members, gs))
    sconv_pad = jnp.pad(sconv_full, ((0, 0), (0, 8 - SCONV_K), (0, 0)))
    cconv_pad = jnp.pad(cconv_full, ((0, 0), (0, 32 - CCONV_K), (0, 0)))
    zeros_w = jnp.zeros((nl, w), F32)
    vecs = jnp.stack([sgu_norm_g, cconv_ln_g, cconv_ln_b, pool_scale] + [zeros_w] * 4, axis=1)
    wt = jnp.tril(sgu_w).astype(BF16)
    bexp = jnp.repeat(jnp.swapaxes(sgu_b, 1, 2), w // N_HEADS, axis=2)
    eye = jnp.eye(4, dtype=F32)
    pbd = jnp.einsum("lgcd,gh->lgchd", pool_w, eye).reshape(nl, w, w).astype(BF16)

    def mixer_args(l):
        return sconv_pad[l], cconv_pad[l], vecs[l], wt[l], bexp[l], pbd[l]

    saved = []
    xc = x0
    after = token
    for l in range(nl):
        s = {"x_ffn1": xc}
        arrive("ffn1", l, after)
        xc, s["gu_ffn1"] = ffn_fwd(xc, norm_ffn1[l], wg[l]["ffn1_w_in"], wg[l]["ffn1_w_out"],
                                   name=f"ffn1_fwd{l}", tm=FFN_FWD_TILE)
        s["x_mix"] = xc
        arrive("mid", l, xc)
        z = mm_rows(xc, wg[l]["mix_w_in"], "col", gain=norm_mix[l], name=f"mix_in{l}")
        y, xc = mixer_fwd(z, *mixer_args(l), xc, wg[l]["mix_w_out"], name=f"mixer_fwd{l}")
        s["z"], s["y"] = z, y
        s["x_att"] = xc
        kv = mm_rows(mem0, wg[l]["xattn_wkv"], "col", gain=norm_mem[l], name=f"kv{l}")
        s["kv"] = kv
        if l > 0:
            arrive_early("ffn2", l, kv)
        xc = xattn_fwd(xc, norm_xattn[l], kv, wg[l]["xattn_wq"], wg[l]["xattn_wo"], name=f"xattn_fwd{l}")
        s["x_ffn2"] = xc
        arrive("ffn2", l, xc)
        xc, s["gu_ffn2"] = ffn_fwd(xc, norm_ffn2[l], wg[l]["ffn2_w_in"], wg[l]["ffn2_w_out"],
                                   name=f"ffn2_fwd{l}", tm=FFN_FWD_TILE)
        after = xc
        saved.append(s)

    dx, g_norm_final, loss_local = loss_head(xc, target, norm_final, name="loss_head")

    tm = _row_tile(t, TN_TILE)
    small ={n: [None] * nl for n in SMALL_REPL + SMALL_SHARD if n != "norm_final"}
    scattered = {}
    tie = [token]

    def send_grads(gname, l, grads):
        members = list(grads)
        send, recv, gs, lands, tie[0] = scatter_start(
            [grads[n] for n in members], tie[0], name=f"scatter_start_{gname}{l}")
        scattered[gname, l] = (members, gs, lands, send, recv)

    names = SMALL_REPL + SMALL_SHARD + ["loss"]
    small_pending = []

    def start_small():
        small_full = {n: jnp.stack(v) for n, v in small.items()}
        small_full["norm_final"] = g_norm_final[0]
        small_full["loss"] = loss_local[0]
        shapes = [small_full[n].shape for n in names]
        packed = _pack([small_full[n] for n in names], _rows_for(shapes))
        slot = cast_into_slot(packed[None], 0, me_arr, name="small_into_slot", dtype=F32)
        send, recv, gs, tie[0] = gather_start([slot], tie[0], name="small_gather_start", masks=ALL_MASKS)
        small_pending.append((gs, send, recv, shapes))

    def ffn_backward(which, l, x_in, dy, gu, gain):
        w_in, w_out = wg[l][which + "_w_in"], wg[l][which + "_w_out"]
        dx_, h_, act, dgu, dgn, dyh = ffn_bwd_rows(x_in, dy, gu, gain, w_in, w_out, tie[0],
                                                   name=f"{which}_bwd{l}_rows")
        small["norm_" + which][l] = dgn[0]
        last = which == "ffn1" and l == 0
        if last:
            start_small()
        g_in = ffn_grad_w_in(h_, dgu, tie[0], name=f"{which}_bwd{l}_dwin")
        if last:
            send_grads(which + "_in", l, {which + "_w_in": g_in})
        g_out = ffn_grad_w_out(act, dyh, tie[0], name=f"{which}_bwd{l}_dwout")
        if last:
            send_grads(which + "_out", l, {which + "_w_out": g_out})
        else:
            send_grads(which, l, {which + "_w_in": g_in, which + "_w_out": g_out})
        return dx_

    for l in reversed(range(nl)):
        s = saved[l]
        wl = wg[l]
        dx = ffn_backward("ffn2", l, s["x_ffn2"], dx, s["gu_ffn2"], norm_ffn2[l])

        bg = {}
        dxn = dx
        dx, h, dq, o, dkv, dgn = xattn_bwd_rows(
            s["x_att"], dxn, norm_xattn[l], s["kv"], wl["xattn_wq"], wl["xattn_wo"], tie[0],
            name=f"xattn_bwd{l}")
        small["norm_xattn"][l] = dgn[0]
        row_spec = pl.BlockSpec((tm, d), lambda s_, i: (i, 0))
        bg["xattn_wq"] = mm_tn(h, dq, nb=1, ka=d, nbk=d, tm=tm, m=t, a_spec=row_spec, b_spec=row_spec,
                               name=f"dwq{l}").reshape(N_DEV, d // N_DEV, d)
        bg["xattn_wo"] = mm_tn(o, dxn, nb=1, ka=d, nbk=d, tm=tm, m=t, a_spec=row_spec, b_spec=row_spec,
                               name=f"dwo{l}").reshape(N_DEV, d // N_DEV, d)
        _, mhat, dgn = mm_nt(dkv, wl["xattn_wkv"], "col", x=mem0, gain=norm_mem[l], name=f"dmem{l}")
        small["norm_mem"][l] = dgn[0]
        nm = mem0.shape[0]
        bg["xattn_wkv"] = mm_tn(mhat, dkv, nb=N_DEV, ka=d, nbk=2 * d // N_DEV, tm=nm, m=nm,
                                a_spec=pl.BlockSpec((nm, d), lambda s_, i: (0, 0)),
                                b_spec=pl.BlockSpec((nm, 2 * d // N_DEV), lambda s_, i: (0, s_)),
                                name=f"dwkv{l}")
        send_grads("xattn", l, bg)

        bg = {}
        dxn = dx
        bg["mix_w_out"] = mm_tn(s["y"], dxn, nb=1, ka=d, nbk=d, tm=tm, m=t, a_spec=row_spec,
                                b_spec=row_spec, name=f"dwmo{l}").reshape(N_DEV, d // N_DEV, d)
        dz, gvec, gcc, gwt, gb, gpbd = mixer_bwd(s["z"], dxn, wl["mix_w_out"], *mixer_args(l),
                                                 name=f"mixer_bwd{l}")
        small["sconv_w"][l] = gvec[0:SCONV_K]
        small["sgu_norm_g"][l] = gvec[3]
        small["cconv_ln_g"][l] = gvec[4]
        small["cconv_ln_b"][l] = gvec[5]
        small["pool_scale"][l] = gvec[6]
        small["cconv_w"][l] = gcc[0:CCONV_K]
        small["sgu_w"][l] = gwt
        small["sgu_b"][l] = jnp.transpose(gb[:, 0:N_HEADS])
        gw = w // 4
        small["pool_w"][l] = jnp.stack([gpbd[g * gw:(g + 1) * gw, g * gw:(g + 1) * gw] for g in range(4)])
        dx, h, dgn = mm_nt(dz, wl["mix_w_in"], "col", x=s["x_mix"], gain=norm_mix[l], dx_in=dxn,
                           after=tie[0], name=f"dh_mix{l}")
        small["norm_mix"][l] = dgn[0]
        th = _row_tile(t, TN_TILE // 2)
        bg["mix_w_in"] = mm_tn(h, dz, nb=1, ka=d, nbk=N_DEV * w, tm=th, m=t, col_slots=N_DEV,
                               a_spec=pl.BlockSpec((th, d), lambda s_, i: (i, 0)),
                               b_spec=pl.BlockSpec((th, N_DEV * w), lambda s_, i: (i, 0)), name=f"dwmi{l}")
        send_grads("mix", l, bg)

        dx = ffn_backward("ffn1", l, s["x_ffn1"], dx, s["gu_ffn1"], norm_ffn1[l])

    out = {}
    own, land = {}, {}

    def collect(keys, after):
        for gname, l in keys:
            members, gs, lands, send, recv = scattered.pop((gname, l))
            gs, lands = scatter_wait(gs, lands, send, recv, after, name=f"scatter_wait_{gname}{l}")
            for n, g_, l_ in zip(members, gs, lands):
                own.setdefault(n, {})[l] = g_
                land.setdefault(n, {})[l] = l_

    def update(ns, after):
        for n in ns:
            out[n] = adamw_sharded([own[n][l] for l in range(nl)], [land[n][l] for l in range(nl)],
                                   wts[n], mom[n], var[n], me_arr, name="adamw_" + n)
            after = out[n][1]
        return after

    after = tie[0]
    for gname in ("ffn2", "xattn", "mix"):
        collect([(gname, l) for l in reversed(range(nl))], after)
        after = update([n for n in own if n not in out], after)
    (gs, send, recv, shapes), = small_pending
    gs = gather_wait(gs, send, recv, after, name="small_gather_wait", masks=ALL_MASKS)
    summed = sum_slots(gs[0], name="small_sum")
    gsm = dict(zip(names, _unpack(summed, shapes)))
    loss = gsm["loss"][0]
    cs = w // N_DEV
    for n in SMALL_SHARD:
        gsm[n] = lax.dynamic_slice_in_dim(gsm[n], me * cs, cs, axis=2)
    small_names = SMALL_REPL + SMALL_SHARD
    upd = adamw_many([gsm[n] for n in small_names], [wts[n] for n in small_names],
                     [mom[n] for n in small_names], [var[n] for n in small_names], name="adamw_small")
    for n, (a, b, c) in zip(small_names, upd):
        out[n] = (gsm[n], a, b, c)
    after = upd[0][0]
    collect([("ffn1", l) for l in reversed(range(1, nl))] + [("ffn1_in", 0)], after)
    after = update(["ffn1_w_in"], after)
    collect([("ffn1_out", 0)], after)
    update(["ffn1_w_out"], after)
    for n in TRANSPOSED:
        out[n] = tuple(jnp.swapaxes(a, 1, 2) for a in out[n])

    grad_x = dx.reshape(1, t, d)
    return (loss, grad_x, *[out[n][0] for n in WEIGHTS], *[out[n][1] for n in WEIGHTS],
            *[out[n][2] for n in WEIGHTS], *[out[n][3] for n in WEIGHTS])
```

```python
import jax
import jax.numpy as jnp
from jax import lax
from jax.experimental import pallas as pl
from jax.experimental.pallas import tpu as pltpu

F32 = jnp.float32
BF16 = jnp.bfloat16
MESH = pl.DeviceIdType.MESH
N_DEV = 8
EPS = 1e-6
HALO = 32
SGU_CHUNK = 128
CCONV_K = 31
SCONV_K = 3
MIX_W = 256
N_HEADS = 4
VMEM_LIMIT = 56 * 1024 * 1024
ROW_TILE = 512
TN_TILE = 2048
FFN_FWD_TILE = 1024
FFN_BWD_SPLIT = 2
ROW_BLOCKS = (256, 176, 128)
MIX_TILE = 512

ADAM_LR = 0.001
ADAM_B1 = 0.9
ADAM_B2 = 0.999
ADAM_EPS = 1e-08
ADAM_WD = 0.01
ADAM_STEP = 10

HBM_SPEC = pl.BlockSpec(memory_space=pltpu.HBM)
VMEM_SPEC = pl.BlockSpec(memory_space=pltpu.VMEM)


def _params(*sem):
    return pltpu.CompilerParams(dimension_semantics=tuple(sem), vmem_limit_bytes=VMEM_LIMIT)


def _row_tile(m, pref=None):
    t = min(m, ROW_TILE if pref is None else pref)
    assert m % t == 0, (m, t)
    return t


def _my_index():
    return lax.axis_index("x") * 4 + lax.axis_index("y") * 2 + lax.axis_index("c")


def _peer(mask):
    x, y, c = lax.axis_index("x"), lax.axis_index("y"), lax.axis_index("c")
    px = 1 - x if mask & 4 else x
    py = 1 - y if mask & 2 else y
    pc = 1 - c if mask & 1 else c
    return (px, py, pc), px * 4 + py * 2 + pc


def all_gather(arrs, name):
    n = len(arrs)

    def body(*refs):
        ins, outs = refs[:n], refs[n:2 * n]
        send_sems, recv_sems, loc_sems = refs[2 * n:]
        me = _my_index()
        local = []
        for i in range(n):
            cp = pltpu.make_async_copy(ins[i], outs[i].at[me], loc_sems.at[i])
            cp.start()
            local.append(cp)
        sends = []
        for i in range(n):
            for m in range(1, N_DEV):
                peer, _ = _peer(m)
                cp = pltpu.make_async_remote_copy(
                    src_ref=ins[i], dst_ref=outs[i].at[me],
                    send_sem=send_sems.at[i, m - 1], recv_sem=recv_sems.at[i, m - 1],
                    device_id=peer, device_id_type=MESH)
                cp.start()
                sends.append(cp)
        for i in range(n):
            for m in range(1, N_DEV):
                peer, pidx = _peer(m)
                pltpu.make_async_remote_copy(
                    src_ref=ins[i], dst_ref=outs[i].at[pidx],
                    send_sem=send_sems.at[i, m - 1], recv_sem=recv_sems.at[i, m - 1],
                    device_id=peer, device_id_type=MESH).wait_recv()
        for cp in sends:
            cp.wait_send()
        for cp in local:
            cp.wait()

    return pl.pallas_call(
        body, name=name,
        out_shape=[jax.ShapeDtypeStruct((N_DEV,) + a.shape, a.dtype) for a in arrs],
        in_specs=[HBM_SPEC] * n, out_specs=[HBM_SPEC] * n,
        scratch_shapes=[pltpu.SemaphoreType.DMA((n, N_DEV - 1)),
                        pltpu.SemaphoreType.DMA((n, N_DEV - 1)),
                        pltpu.SemaphoreType.DMA((n,))],
    )(*arrs)


SEM_SPEC = pl.BlockSpec(memory_space=pltpu.SEMAPHORE)
ANY_SPEC = pl.BlockSpec(memory_space=pl.ANY)
SIDE_EFFECT = pltpu.SideEffectType.DATAFLOW_SIDE_EFFECTING


def _hbm(a):
    return pltpu.with_memory_space_constraint(a, pltpu.HBM)


def _sem_pairs(n):
    return (pltpu.SemaphoreType.DMA((n * (N_DEV - 1),)), pltpu.SemaphoreType.DMA((n * (N_DEV - 1),)))


def _sem(i, m):
    return i * (N_DEV - 1) + m - 1


def _gather_copy(g_ref, i, m, send_sems, recv_sems, origin):
    peer, _ = _peer(m)
    return pltpu.make_async_remote_copy(
        src_ref=g_ref.at[origin], dst_ref=g_ref.at[origin],
        send_sem=send_sems.at[_sem(i, m)], recv_sem=recv_sems.at[_sem(i, m)],
        device_id=peer, device_id_type=MESH)


GATHER_MASKS = (1, 2, 4, 6)
FORWARD_MASKS = (2, 4, 6)


ALL_MASKS = tuple(range(1, N_DEV))


def gather_start(gs, after, name, masks=GATHER_MASKS):
    n = len(gs)

    def body(*refs):
        g_in = refs[:n]
        send_sems, recv_sems = refs[n + 1], refs[n + 2]
        token = refs[-1]
        me = _my_index()
        for i in range(n):
            for m in masks:
                _gather_copy(g_in[i], i, m, send_sems, recv_sems, me).start()
        token[...] = jnp.zeros_like(token)

    outs = pl.pallas_call(
        body, name=name,
        out_shape=(*_sem_pairs(n), *[pltpu.HBM(g.shape, g.dtype) for g in gs],
                   jax.ShapeDtypeStruct((8, 128), F32)),
        in_specs=[HBM_SPEC] * n + [ANY_SPEC],
        out_specs=(SEM_SPEC, SEM_SPEC, *[HBM_SPEC] * n, VMEM_SPEC),
        input_output_aliases={i: 2 + i for i in range(n)},
        compiler_params=pltpu.CompilerParams(has_side_effects=SIDE_EFFECT),
    )(*[_hbm(g) for g in gs], after)
    return outs[0], outs[1], list(outs[2:2 + n]), outs[-1]


def gather_start_groups(groups, after, name, masks=GATHER_MASKS):
    sizes = [len(g) for g in groups]
    flat = [a for g in groups for a in g]
    n, ng = len(flat), len(groups)

    def body(*refs):
        g_in = refs[:n]
        sems = refs[n + 1:n + 1 + 2 * ng]
        token = refs[-1]
        me = _my_index()
        pos = 0
        for k, size in enumerate(sizes):
            for i in range(size):
                for m in masks:
                    _gather_copy(g_in[pos + i], i, m, sems[2 * k], sems[2 * k + 1], me).start()
            pos += size
        token[...] = jnp.zeros_like(token)

    outs = pl.pallas_call(
        body, name=name,
        out_shape=(*[s for size in sizes for s in _sem_pairs(size)],
                   *[pltpu.HBM(g.shape, g.dtype) for g in flat], jax.ShapeDtypeStruct((8, 128), F32)),
        in_specs=[HBM_SPEC] * n + [ANY_SPEC],
        out_specs=(*[SEM_SPEC] * (2 * ng), *[HBM_SPEC] * n, VMEM_SPEC),
        input_output_aliases={i: 2 * ng + i for i in range(n)},
        compiler_params=pltpu.CompilerParams(has_side_effects=SIDE_EFFECT),
    )(*[_hbm(g) for g in flat], after)
    result, pos = [], 2 * ng
    for k, size in enumerate(sizes):
        result.append((outs[2 * k], outs[2 * k + 1], list(outs[pos:pos + size])))
        pos += size
    return result, outs[-1]


def gather_wait(gs, send_sems, recv_sems, after, name, masks=GATHER_MASKS):
    n = len(gs)

    def body(*refs):
        g_in = refs[:n]
        send, recv = refs[n], refs[n + 1]
        me = _my_index()
        for i in range(n):
            for m in masks:
                _, pidx = _peer(m)
                _gather_copy(g_in[i], i, m, send, recv, me).wait_send()
                _gather_copy(g_in[i], i, m, send, recv, pidx).wait_recv()

    outs = pl.pallas_call(
        body, name=name,
        out_shape=[pltpu.HBM(g.shape, g.dtype) for g in gs],
        in_specs=[HBM_SPEC] * n + [SEM_SPEC, SEM_SPEC, ANY_SPEC],
        out_specs=[HBM_SPEC] * n,
        input_output_aliases={i: i for i in range(n)},
        compiler_params=pltpu.CompilerParams(has_side_effects=SIDE_EFFECT),
    )(*gs, send_sems, recv_sems, after)
    return list(outs)


def sibling_forward(gs, name):
    n = len(gs)
    nf = len(FORWARD_MASKS)

    def body(*refs):
        g_in = refs[:n]
        send_sems, recv_sems = refs[2 * n:]
        x, y, c = lax.axis_index("x"), lax.axis_index("y"), lax.axis_index("c")
        sibling = (x, y, 1 - c)

        def copy(i, k, origin):
            return pltpu.make_async_remote_copy(
                src_ref=g_in[i].at[origin], dst_ref=g_in[i].at[origin],
                send_sem=send_sems.at[i * nf + k], recv_sem=recv_sems.at[i * nf + k],
                device_id=sibling, device_id_type=MESH)
        sends = []
        for i in range(n):
            for k, m in enumerate(FORWARD_MASKS):
                _, origin = _peer(m)
                cp = copy(i, k, origin)
                cp.start()
                sends.append(cp)
        for i in range(n):
            for k, m in enumerate(FORWARD_MASKS):
                _, origin = _peer(m ^ 1)
                copy(i, k, origin).wait_recv()
        for cp in sends:
            cp.wait_send()

    outs = pl.pallas_call(
        body, name=name,
        out_shape=[jax.ShapeDtypeStruct(g.shape, g.dtype) for g in gs],
        in_specs=[HBM_SPEC] * n, out_specs=[HBM_SPEC] * n,
        input_output_aliases={i: i for i in range(n)},
        scratch_shapes=[pltpu.SemaphoreType.DMA((n * nf,)), pltpu.SemaphoreType.DMA((n * nf,))],
    )(*gs)
    return list(outs)


def _forward_copy(g_ref, i, k, send_sems, recv_sems, origin):
    sibling = (lax.axis_index("x"), lax.axis_index("y"), 1 - lax.axis_index("c"))
    slot = i * len(FORWARD_MASKS) + k
    return pltpu.make_async_remote_copy(
        src_ref=g_ref.at[origin], dst_ref=g_ref.at[origin],
        send_sem=send_sems.at[slot], recv_sem=recv_sems.at[slot],
        device_id=sibling, device_id_type=MESH)


def forward_start(gs, after, name):
    n = len(gs)
    nsem = n * len(FORWARD_MASKS)

    def body(*refs):
        g_in = refs[:n]
        send_sems, recv_sems = refs[n + 1], refs[n + 2]
        token = refs[-1]
        for i in range(n):
            for k, m in enumerate(FORWARD_MASKS):
                _, origin = _peer(m)
                _forward_copy(g_in[i], i, k, send_sems, recv_sems, origin).start()
        token[...] = jnp.zeros_like(token)

    outs = pl.pallas_call(
        body, name=name,
        out_shape=(pltpu.SemaphoreType.DMA((nsem,)), pltpu.SemaphoreType.DMA((nsem,)),
                   *[pltpu.HBM(g.shape, g.dtype) for g in gs], jax.ShapeDtypeStruct((8, 128), F32)),
        in_specs=[HBM_SPEC] * n + [ANY_SPEC],
        out_specs=(SEM_SPEC, SEM_SPEC, *[HBM_SPEC] * n, VMEM_SPEC),
        input_output_aliases={i: 2 + i for i in range(n)},
        compiler_params=pltpu.CompilerParams(has_side_effects=SIDE_EFFECT),
    )(*[_hbm(g) for g in gs], after)
    return outs[0], outs[1], list(outs[2:2 + n]), outs[-1]


def forward_wait(gs, send_sems, recv_sems, after, name):
    n = len(gs)

    def body(*refs):
        g_in = refs[:n]
        send, recv = refs[n], refs[n + 1]
        for i in range(n):
            for k, m in enumerate(FORWARD_MASKS):
                _, mine = _peer(m)
                _, theirs = _peer(m ^ 1)
                _forward_copy(g_in[i], i, k, send, recv, mine).wait_send()
                _forward_copy(g_in[i], i, k, send, recv, theirs).wait_recv()

    outs = pl.pallas_call(
        body, name=name,
        out_shape=[pltpu.HBM(g.shape, g.dtype) for g in gs],
        in_specs=[HBM_SPEC] * n + [SEM_SPEC, SEM_SPEC, ANY_SPEC],
        out_specs=[HBM_SPEC] * n,
        input_output_aliases={i: i for i in range(n)},
        compiler_params=pltpu.CompilerParams(has_side_effects=SIDE_EFFECT),
    )(*gs, send_sems, recv_sems, after)
    return list(outs)


def _scatter_copy(g_ref, l_ref, i, m, send_sems, recv_sems):
    peer, pidx = _peer(m)
    return pltpu.make_async_remote_copy(
        src_ref=g_ref.at[pidx], dst_ref=l_ref.at[m - 1],
        send_sem=send_sems.at[_sem(i, m)], recv_sem=recv_sems.at[_sem(i, m)],
        device_id=peer, device_id_type=MESH)


def scatter_start(grads, after, name):
    n = len(grads)
    lands = [lax.empty((N_DEV - 1,) + g.shape[1:], g.dtype) for g in grads]

    def body(*refs):
        g_in, l_in = refs[:n], refs[n:2 * n]
        send_sems, recv_sems = refs[2 * n + 1], refs[2 * n + 2]
        token = refs[-1]
        for i in range(n):
            for m in range(1, N_DEV):
                _scatter_copy(g_in[i], l_in[i], i, m, send_sems, recv_sems).start()
        token[...] = jnp.zeros_like(token)

    outs = pl.pallas_call(
        body, name=name,
        out_shape=(*_sem_pairs(n), *[pltpu.HBM(g.shape, g.dtype) for g in grads],
                   *[pltpu.HBM(l.shape, l.dtype) for l in lands], jax.ShapeDtypeStruct((8, 128), F32)),
        in_specs=[HBM_SPEC] * (2 * n) + [ANY_SPEC],
        out_specs=(SEM_SPEC, SEM_SPEC, *[HBM_SPEC] * (2 * n), VMEM_SPEC),
        input_output_aliases={i: 2 + i for i in range(2 * n)},
        compiler_params=pltpu.CompilerParams(has_side_effects=SIDE_EFFECT),
    )(*[_hbm(g) for g in grads], *[_hbm(l) for l in lands], after)
    return outs[0], outs[1], list(outs[2:2 + n]), list(outs[2 + n:2 + 2 * n]), outs[-1]


def scatter_wait(grads, lands, send_sems, recv_sems, after, name):
    n = len(grads)

    def body(*refs):
        g_in, l_in = refs[:n], refs[n:2 * n]
        send, recv = refs[2 * n], refs[2 * n + 1]
        for i in range(n):
            for m in range(1, N_DEV):
                cp = _scatter_copy(g_in[i], l_in[i], i, m, send, recv)
                cp.wait_send()
                cp.wait_recv()

    outs = pl.pallas_call(
        body, name=name,
        out_shape=[pltpu.HBM(a.shape, a.dtype) for a in list(grads) + list(lands)],
        in_specs=[HBM_SPEC] * (2 * n) + [SEM_SPEC, SEM_SPEC, ANY_SPEC],
        out_specs=[HBM_SPEC] * (2 * n),
        input_output_aliases={i: i for i in range(2 * n)},
        compiler_params=pltpu.CompilerParams(has_side_effects=SIDE_EFFECT),
    )(*grads, *lands, send_sems, recv_sems, after)
    return list(outs[:n]), list(outs[n:])


def sum_slots(g, name):
    _, r, c = g.shape

    def body(g_ref, out_ref):
        acc = g_ref[0]
        for p in range(1, N_DEV):
            acc = acc + g_ref[p]
        out_ref[...] = acc

    return pl.pallas_call(
        body, name=name, out_shape=jax.ShapeDtypeStruct((r, c), F32),
        in_specs=[VMEM_SPEC], out_specs=VMEM_SPEC,
        compiler_params=pltpu.CompilerParams(vmem_limit_bytes=VMEM_LIMIT),
    )(g)


def _sigmoid(v):
    return 1.0 / (1.0 + jnp.exp(-v))


def _rms_fwd(xf, g):
    r = lax.rsqrt(jnp.mean(xf * xf, axis=-1, keepdims=True) + EPS)
    return xf * r, r


def _rms_bwd(xhat, r, g, dy):
    dg = jnp.sum(dy * xhat, axis=0, keepdims=True)
    dxh = dy * g
    dx = r * (dxh - xhat * jnp.mean(dxh * xhat, axis=-1, keepdims=True))
    return dx, dg


def _ln_stats(v):
    mu = jnp.mean(v, axis=-1, keepdims=True)
    vc = v - mu
    r = lax.rsqrt(jnp.mean(vc * vc, axis=-1, keepdims=True) + EPS)
    return vc * r, r


def _ln_bwd(xhat, r, dxh):
    return r * (dxh - jnp.mean(dxh, axis=-1, keepdims=True)
                - xhat * jnp.mean(dxh * xhat, axis=-1, keepdims=True))


def _dot(a, b):
    return jnp.dot(a, b, preferred_element_type=F32)


def _dot_nt(a, b):
    return lax.dot_general(a, b, (((1,), (1,)), ((), ())), preferred_element_type=F32)


def _dot_tn(a, b):
    return lax.dot_general(a, b, (((0,), (0,)), ((), ())), preferred_element_type=F32)


def _full_weight(w_ref, kind):
    assert kind == "row"
    p, a, b = w_ref.shape
    return w_ref[...].reshape(p * a, b)


def _wspec(wg):
    return pl.BlockSpec(wg.shape, lambda *_: (0, 0, 0))


def mm_rows(a, wg, kind, *, gain=None, out_dtype=F32, name, tm=None):
    m, k = a.shape
    p, wa, wb = wg.shape
    n = p * wb if kind == "col" else wb
    tm = _row_tile(m, tm)
    has_gain = gain is not None

    def body(*refs):
        refs = list(refs)
        a_ref = refs.pop(0)
        g_ref = refs.pop(0) if has_gain else None
        w_ref = refs.pop(0)
        o_ref = refs.pop(0)
        if has_gain:
            xhat, _ = _rms_fwd(a_ref[...].astype(F32), None)
            h = (xhat * g_ref[...]).astype(BF16)
        else:
            h = a_ref[...].astype(BF16)
        if kind == "col":
            for j in range(p):
                o_ref[:, j * wb:(j + 1) * wb] = _dot(h, w_ref[j]).astype(out_dtype)
        else:
            o_ref[...] = _dot(h, _full_weight(w_ref, "row")).astype(out_dtype)

    operands = [a]
    in_specs = [pl.BlockSpec((tm, k), lambda i: (i, 0))]
    if has_gain:
        operands.append(gain.reshape(1, k))
        in_specs.append(pl.BlockSpec((1, k), lambda i: (0, 0)))
    operands.append(wg)
    in_specs.append(_wspec(wg))
    return pl.pallas_call(
        body, name=name, grid=(m // tm,),
        out_shape=jax.ShapeDtypeStruct((m, n), out_dtype),
        in_specs=in_specs, out_specs=pl.BlockSpec((tm, n), lambda i: (i, 0)),
        compiler_params=_params("parallel"),
    )(*operands)


def mm_nt(dz, wg, kind, *, x=None, gain=None, dx_in=None, after=None, name, tm=None):
    m, n = dz.shape
    p, wa, wb = wg.shape
    k = wa if kind == "col" else p * wa
    tm = _row_tile(m, tm)
    epi = x is not None
    has_dx = dx_in is not None
    has_after = after is not None

    def body(*refs):
        refs = list(refs)
        dz_ref, w_ref = refs.pop(0), refs.pop(0)
        if epi:
            x_ref, g_ref = refs.pop(0), refs.pop(0)
            dxi_ref = refs.pop(0) if has_dx else None
        if has_after:
            refs.pop(0)
        if epi:
            dx_ref, h_ref, dg_ref = refs
        else:
            (da_ref,) = refs
        dzb = dz_ref[...].astype(BF16)
        if kind == "col":
            da = _dot_nt(dzb[:, 0:wb], w_ref[0])
            for j in range(1, p):
                da = da + _dot_nt(dzb[:, j * wb:(j + 1) * wb], w_ref[j])
        else:
            da = _dot_nt(dzb, _full_weight(w_ref, "row"))
        if not epi:
            da_ref[...] = da
            return
        g = g_ref[...]
        xhat, r = _rms_fwd(x_ref[...].astype(F32), None)
        h_ref[...] = (xhat * g).astype(BF16)
        dx, dg = _rms_bwd(xhat, r, g, da)
        if has_dx:
            dx = dx + dxi_ref[...]
        dx_ref[...] = dx

        @pl.when(pl.program_id(0) == 0)
        def _():
            dg_ref[...] = jnp.zeros_like(dg_ref)
        dg_ref[...] += dg

    row = lambda i: (i, 0)
    operands = [dz, wg]
    in_specs = [pl.BlockSpec((tm, n), row), _wspec(wg)]
    if epi:
        operands += [x, gain.reshape(1, k)]
        in_specs += [pl.BlockSpec((tm, k), row), pl.BlockSpec((1, k), lambda i: (0, 0))]
        if has_dx:
            operands.append(dx_in)
            in_specs.append(pl.BlockSpec((tm, k), row))
        out_shape = [jax.ShapeDtypeStruct((m, k), F32), jax.ShapeDtypeStruct((m, k), BF16),
                     jax.ShapeDtypeStruct((1, k), F32)]
        out_specs = [pl.BlockSpec((tm, k), row), pl.BlockSpec((tm, k), row),
                     pl.BlockSpec((1, k), lambda i: (0, 0))]
    else:
        out_shape = jax.ShapeDtypeStruct((m, k), F32)
        out_specs = pl.BlockSpec((tm, k), row)
    if has_after:
        operands.append(after)
        in_specs.append(ANY_SPEC)
    return pl.pallas_call(
        body, name=name, grid=(m // tm,), out_shape=out_shape,
        in_specs=in_specs, out_specs=out_specs,
        compiler_params=_params("arbitrary"),
    )(*operands)


def mm_tn(a, b, *, nb, a_spec, b_spec, ka, nbk, tm, m, scale=1.0, out_dtype=BF16, col_slots=1,
          after=None, name):
    ni = m // tm
    assert col_slots == 1 or nb == 1
    cw = nbk // col_slots
    extra = [] if after is None else [after]

    def body(a_ref, b_ref, *rest):
        o_ref, acc = rest[len(extra):]
        i = pl.program_id(1)

        @pl.when(i == 0)
        def _():
            acc[...] = jnp.zeros_like(acc)
        acc[...] += _dot_tn(a_ref[...].astype(BF16), b_ref[...].astype(BF16))

        @pl.when(i == ni - 1)
        def _():
            if col_slots == 1:
                o_ref[...] = (acc[...] * scale).astype(out_dtype)
            else:
                for j in range(col_slots):
                    o_ref[j] = (acc[:, j * cw:(j + 1) * cw] * scale).astype(out_dtype)

    if col_slots == 1:
        out_shape = jax.ShapeDtypeStruct((nb, ka, nbk), out_dtype)
        out_spec = pl.BlockSpec((None, ka, nbk), lambda s, i: (s, 0, 0))
    else:
        out_shape = jax.ShapeDtypeStruct((col_slots, ka, cw), out_dtype)
        out_spec = pl.BlockSpec((col_slots, ka, cw), lambda s, i: (0, 0, 0))
    return pl.pallas_call(
        body, name=name, grid=(nb, ni), out_shape=out_shape,
        in_specs=[a_spec, b_spec] + [ANY_SPEC] * len(extra), out_specs=out_spec,
        scratch_shapes=[pltpu.VMEM((ka, nbk), F32)],
        compiler_params=_params("parallel", "arbitrary"),
    )(a, b, *extra)


def _ffn_specs(w_in_g, w_out_g, d):
    nf = w_in_g.shape[1]
    hr = w_out_g.shape[1]
    assert 2 * hr == nf
    w_in5 = w_in_g.reshape(2, 4, nf, d)
    w_out5 = w_out_g.reshape(4, 2, hr, d)
    in_spec = pl.BlockSpec((2, None, nf, d), lambda i, j: (0, j, 0, 0))
    out_spec = pl.BlockSpec((None, 2, hr, d), lambda i, j: (j, 0, 0, 0))
    return w_in5, w_out5, in_spec, out_spec, nf


def ffn_fwd(x, gain, w_in_g, w_out_g, *, name, tm=None):
    t, d = x.shape
    tm = _row_tile(t, tm)
    w_in5, w_out5, wi_spec, wo_spec, nf = _ffn_specs(w_in_g, w_out_g, d)

    def body(x_ref, g_ref, wi_ref, wo_ref, o_ref, gu_ref, h_scr, acc):
        j = pl.program_id(1)

        @pl.when(j == 0)
        def _():
            xhat, _ = _rms_fwd(x_ref[...], None)
            h_scr[...] = (xhat * g_ref[...]).astype(BF16)
            acc[...] = jnp.zeros_like(acc)
        h = h_scr[...]
        gt = _dot_nt(h, wi_ref[0])
        up = _dot_nt(h, wi_ref[1])
        gu_ref[0] = gt.astype(BF16)
        gu_ref[1] = up.astype(BF16)
        act = (gt * _sigmoid(gt) * up).astype(BF16)
        acc[...] += _dot(act, wo_ref[...].reshape(nf, d))

        @pl.when(j == 3)
        def _():
            o_ref[...] = x_ref[...] + 0.5 * acc[...]

    return pl.pallas_call(
        body, name=name, grid=(t // tm, 4),
        out_shape=[jax.ShapeDtypeStruct((t, d), F32), jax.ShapeDtypeStruct((2, 4, t, nf), BF16)],
        in_specs=[pl.BlockSpec((tm, d), lambda i, j: (i, 0)),
                  pl.BlockSpec((1, d), lambda i, j: (0, 0)), wi_spec, wo_spec],
        out_specs=[pl.BlockSpec((tm, d), lambda i, j: (i, 0)),
                   pl.BlockSpec((2, None, tm, nf), lambda i, j: (0, j, i, 0))],
        scratch_shapes=[pltpu.VMEM((tm, d), BF16), pltpu.VMEM((tm, d), F32)],
        compiler_params=_params("parallel", "arbitrary"),
    )(x, gain.reshape(1, d), w_in5, w_out5)


def ffn_bwd_rows(x, dy, gu, gain, w_in_g, w_out_g, after, *, name, tm=None):
    t, d = x.shape
    tm = _row_tile(t, tm)
    w_in5, w_out5, wi_spec, wo_spec, nf = _ffn_specs(w_in_g, w_out_g, d)

    def body(x_ref, dy_ref, gu_ref, g_ref, wi_ref, wo_ref, after_ref, dx_ref, h_ref, act_ref, dgu_ref, dg_ref,
             dyh_scr, dh_acc):
        i, j = pl.program_id(0), pl.program_id(1)

        @pl.when(j == 0)
        def _():
            xhat, _ = _rms_fwd(x_ref[...], None)
            h_ref[...] = (xhat * g_ref[...]).astype(BF16)
            dyh_scr[...] = (0.5 * dy_ref[...]).astype(BF16)
            dh_acc[...] = jnp.zeros_like(dh_acc)
        wo = wo_ref[...].reshape(nf, d)

        def gates(rows):
            gt = gu_ref[0, rows].astype(F32)
            up = gu_ref[1, rows].astype(F32)
            sg = _sigmoid(gt)
            silu = gt * sg
            act_ref[rows] = (silu * up).astype(BF16)
            return up * (sg * (1.0 + gt * (1.0 - sg))), silu

        def grads(rows, dact, dsilu_up, silu):
            dgt = (dact * dsilu_up).astype(BF16)
            dup = (dact * silu).astype(BF16)
            dgu_ref[0, rows] = dgt
            dgu_ref[1, rows] = dup
            return dgt, dup

        sub = tm // FFN_BWD_SPLIT
        parts = [slice(k * sub, (k + 1) * sub) for k in range(FFN_BWD_SPLIT)]
        dact = _dot_nt(dyh_scr[parts[0]], wo)
        gate = gates(parts[0])
        for k, rows in enumerate(parts):
            if k + 1 < len(parts):
                dact_next = _dot_nt(dyh_scr[parts[k + 1]], wo)
            dgt, dup = grads(rows, dact, *gate)
            dh_acc[rows] += _dot(dgt, wi_ref[0]) + _dot(dup, wi_ref[1])
            if k + 1 < len(parts):
                gate = gates(parts[k + 1])
                dact = dact_next

        @pl.when(j == 3)
        def _():
            g = g_ref[...]
            xhat, r = _rms_fwd(x_ref[...], None)
            dx, dg = _rms_bwd(xhat, r, g, dh_acc[...])
            dx_ref[...] = dy_ref[...] + dx

            @pl.when(i == 0)
            def _():
                dg_ref[...] = jnp.zeros_like(dg_ref)
            dg_ref[...] += dg

    row = lambda i, j: (i, 0)
    return pl.pallas_call(
        body, name=name, grid=(t // tm, 4),
        out_shape=[jax.ShapeDtypeStruct((t, d), F32), jax.ShapeDtypeStruct((t, d), BF16),
                   jax.ShapeDtypeStruct((4, t, nf), BF16), jax.ShapeDtypeStruct((2, 4, t, nf), BF16),
                   jax.ShapeDtypeStruct((1, d), F32), jax.ShapeDtypeStruct((t, d), BF16)],
        in_specs=[pl.BlockSpec((tm, d), row), pl.BlockSpec((tm, d), row),
                  pl.BlockSpec((2, None, tm, nf), lambda i, j: (0, j, i, 0)),
                  pl.BlockSpec((1, d), lambda i, j: (0, 0)), wi_spec, wo_spec, ANY_SPEC],
        out_specs=[pl.BlockSpec((tm, d), row), pl.BlockSpec((tm, d), row),
                   pl.BlockSpec((None, tm, nf), lambda i, j: (j, i, 0)),
                   pl.BlockSpec((2, None, tm, nf), lambda i, j: (0, j, i, 0)),
                   pl.BlockSpec((1, d), lambda i, j: (0, 0)), pl.BlockSpec((tm, d), row)],
        scratch_shapes=[pltpu.VMEM((tm, d), F32)],
        compiler_params=_params("arbitrary", "arbitrary"),
    )(x, dy, gu, gain.reshape(1, d), w_in5, w_out5, after)


def ffn_grad_w_in(h, dgu, after, *, name):
    t, d = h.shape
    nf = dgu.shape[-1]
    tm = _row_tile(t, TN_TILE)
    return mm_tn(dgu.reshape(8, t, nf), h, nb=8, ka=nf, nbk=d, tm=tm, m=t, after=after,
                 a_spec=pl.BlockSpec((None, tm, nf), lambda s, i: (s, i, 0)),
                 b_spec=pl.BlockSpec((tm, d), lambda s, i: (i, 0)), name=name)


def ffn_grad_w_out(act, dyh, after, *, name):
    _, t, nf = act.shape
    d = dyh.shape[1]
    tm = _row_tile(t, TN_TILE)
    d_w_out = mm_tn(act, dyh, nb=4, ka=nf, nbk=d, tm=tm, m=t, after=after,
                    a_spec=pl.BlockSpec((None, tm, nf), lambda s, i: (s, i, 0)),
                    b_spec=pl.BlockSpec((tm, d), lambda s, i: (i, 0)), name=name)
    return d_w_out.reshape(8, nf // 2, d)


def _lane_group(shape):
    return lax.shift_right_logical(lax.broadcasted_iota(jnp.int32, shape, 1), 6)


def _pool_count(t0, rows):
    t = (t0 + lax.broadcasted_iota(jnp.int32, (rows, MIX_W), 0) + 1).astype(F32)
    return jnp.minimum(t, _by_group(_lane_group((rows, MIX_W)), 2.0, 4.0, 8.0, 16.0))


def _by_group(grp, v0, v1, v2, v3):
    return jnp.where(grp == 0, v0, jnp.where(grp == 1, v1, jnp.where(grp == 2, v2, v3)))


def _sgu_mix(wt_ref, vnc):
    grp = _lane_group((SGU_CHUNK, MIX_W))
    out = jnp.zeros((SGU_CHUNK, MIX_W), F32)
    for hd in range(N_HEADS):
        out = jnp.where(grp == hd, _dot(wt_ref[hd], vnc), out)
    return out


def _pool_fwd(s1, s2, s3, t0, ts, lo):
    h = lo
    s2[h - 24:h + ts] = s1[h - 24:h + ts] + s1[h - 25:h + ts - 1]
    s3[h - 16:h + ts] = s2[h - 16:h + ts] + s2[h - 18:h + ts - 2]
    sum2 = s2[h:h + ts]
    sum4 = s3[h:h + ts]
    s2[h - 8:h + ts] = s3[h - 8:h + ts] + s3[h - 12:h + ts - 4]
    sum8 = s2[h:h + ts]
    sum16 = sum8 + s2[h - 8:h + ts - 8]
    grp = _lane_group((ts, MIX_W))
    return _by_group(grp, sum2, sum4, sum8, sum16) / _pool_count(t0, ts) - s1[h:h + ts]


def _make_shifts(src, sh, rows):
    for b in range(1, 8):
        sh[b, 0:rows] = src[b:b + rows]


def _rows_at(src, sh, start, n):
    a, b = divmod(start, 8)
    return src[8 * a:8 * a + n] if b == 0 else sh[b, 8 * a:8 * a + n]


def mixer_fwd(z, sconv, cconv, vecs, wt, bexp, pbd, x_res, wmo_g, *, name, ts=None):
    t = z.shape[0]
    ts = _row_tile(t, MIX_TILE if ts is None else ts)
    hl = HALO
    w = MIX_W
    nch = ts // SGU_CHUNK

    def body(zc, zp, sconv_ref, cconv_ref, vec_ref, wt_ref, bexp_ref, pbd_ref, xr_ref, wmo_ref,
             y_ref, xo_ref, s1, s2, s3, sh):
        i = pl.program_id(0)
        has_prev = i > 0

        def col(ref, c):
            return ref[:, c * w:(c + 1) * w]

        def prev(c):
            return jnp.where(has_prev, col(zp, c), 0.0)

        s1[0:hl] = prev(1) * prev(2)
        s1[hl:hl + ts] = col(zc, 1) * col(zc, 2)
        cv = sconv_ref[0:1] * s1[hl - 2:hl - 2 + ts]
        for k in range(1, SCONV_K):
            cv = cv + sconv_ref[k:k + 1] * s1[hl - 2 + k:hl - 2 + k + ts]
        y_ref[:, 0:w] = (col(zc, 0) * cv).astype(BF16)

        xhat, _ = _ln_stats(col(zc, 4))
        vn = (xhat * vec_ref[0:1]).astype(BF16)
        for c in range(nch):
            rows = slice(c * SGU_CHUNK, (c + 1) * SGU_CHUNK)
            mixed = _sgu_mix(wt_ref, vn[rows]) + bexp_ref[...]
            y_ref[rows, w:2 * w] = (zc[rows, 3 * w:4 * w] * mixed).astype(BF16)

        s1[0:hl] = prev(5) * _sigmoid(prev(6))
        s1[hl:hl + ts] = col(zc, 5) * _sigmoid(col(zc, 6))
        off = hl - (CCONV_K - 1)
        _make_shifts(s1, sh, hl + ts - 8)
        cv = cconv_ref[0:1] * _rows_at(s1, sh, off, ts)
        for k in range(1, CCONV_K):
            cv = cv + cconv_ref[k:k + 1] * _rows_at(s1, sh, off + k, ts)
        xhat, _ = _ln_stats(cv)
        ln = xhat * vec_ref[1:2] + vec_ref[2:3]
        y_ref[:, 2 * w:3 * w] = (ln * _sigmoid(ln)).astype(BF16)

        s1[0:hl] = prev(7)
        s1[hl:hl + ts] = col(zc, 7)
        pooled = _pool_fwd(s1, s2, s3, i * ts, ts, hl)
        y_ref[:, 3 * w:4 * w] = (_dot(pooled.astype(BF16), pbd_ref[...]) * vec_ref[3:4]).astype(BF16)

        xo_ref[...] = xr_ref[...] + _dot(y_ref[...], _full_weight(wmo_ref, "row"))

    full = lambda shape: pl.BlockSpec(shape, lambda i: (0,) * len(shape))
    row = lambda i: (i, 0)
    return pl.pallas_call(
        body, name=name, grid=(t // ts,),
        out_shape=[jax.ShapeDtypeStruct((t, 4 * w), BF16), jax.ShapeDtypeStruct((t, 4 * w), F32)],
        in_specs=[pl.BlockSpec((ts, 8 * w), row),
                  pl.BlockSpec((hl, 8 * w), lambda i: (jnp.maximum(i * (ts // hl) - 1, 0), 0)),
                  full((8, w)), full((32, w)), full((8, w)), full((N_HEADS, SGU_CHUNK, SGU_CHUNK)),
                  full((SGU_CHUNK, w)), full((w, w)), pl.BlockSpec((ts, 4 * w), row), _wspec(wmo_g)],
        out_specs=[pl.BlockSpec((ts, 4 * w), row), pl.BlockSpec((ts, 4 * w), row)],
        scratch_shapes=[pltpu.VMEM((hl + ts, w), F32)] * 3 + [pltpu.VMEM((8, hl + ts, w), F32)],
        compiler_params=_params("parallel"),
    )(z, z, sconv, cconv, vecs, wt, bexp, pbd, x_res, wmo_g)


def mixer_bwd(z, dx, wmo_g, sconv, cconv, vecs, wt, bexp, pbd, *, name, ts=None):
    t = z.shape[0]
    ts = _row_tile(t, MIX_TILE if ts is None else ts)
    hl = HALO
    w = MIX_W
    nch = ts // SGU_CHUNK
    ni = t // ts
    ext = ts + hl

    def body(zc, zp, zn, dxc, dxn_, wmo_ref, sconv_ref, cconv_ref, vec_ref, wt_ref, bexp_ref, pbd_ref,
             dz_ref, gvec_ref, gcc_ref, gwt_ref, gb_ref, gpbd_ref, s1, s2, s3, sh1, sh3, dyc, dyn):
        i = pl.program_id(0)
        has_prev = i > 0
        has_next = i < ni - 1
        wmo = _full_weight(wmo_ref, "row")
        dxb = dxc[...].astype(BF16)
        dxnb = dxn_[...].astype(BF16)

        def form_dy(c):
            cols = slice(c * w, (c + 1) * w)
            dyc[:, cols] = _dot_nt(dxb, wmo[cols])
            dyn[:, cols] = _dot_nt(dxnb, wmo[cols])
        form_dy(0)

        @pl.when(i == 0)
        def _():
            gvec_ref[...] = jnp.zeros_like(gvec_ref)
            gcc_ref[...] = jnp.zeros_like(gcc_ref)
            gwt_ref[...] = jnp.zeros_like(gwt_ref)
            gb_ref[...] = jnp.zeros_like(gb_ref)
            gpbd_ref[...] = jnp.zeros_like(gpbd_ref)

        def col(ref, c):
            return ref[:, c * w:(c + 1) * w]

        def prev(c):
            return jnp.where(has_prev, col(zp, c), 0.0)

        def nxt(c):
            return jnp.where(has_next, col(zn, c), 0.0)

        def dnext(c):
            return jnp.where(has_next, col(dyn, c), 0.0)

        def rowsum(v):
            return jnp.sum(v, axis=0, keepdims=True)

        form_dy(1)
        s1[0:hl] = prev(1) * prev(2)
        s1[hl:hl + ts] = col(zc, 1) * col(zc, 2)
        s1[hl + ts:hl + ts + hl] = nxt(1) * nxt(2)
        cv = sconv_ref[0:1] * s1[hl - 2:hl - 2 + ts]
        for k in range(1, SCONV_K):
            cv = cv + sconv_ref[k:k + 1] * s1[hl - 2 + k:hl - 2 + k + ts]
        dya = col(dyc, 0)
        dz_ref[:, 0:w] = (dya * cv).astype(BF16)
        s2[0:ts] = dya * col(zc, 0)
        s2[ts:ext] = dnext(0) * nxt(0)
        dv = sconv_ref[0:1] * s2[2:2 + ts]
        for k in range(1, SCONV_K):
            dv = dv + sconv_ref[k:k + 1] * s2[2 - k:2 - k + ts]
        dz_ref[:, w:2 * w] = (dv * col(zc, 2)).astype(BF16)
        dz_ref[:, 2 * w:3 * w] = (dv * col(zc, 1)).astype(BF16)
        dcv = s2[0:ts]
        for k in range(SCONV_K):
            gvec_ref[k:k + 1] += rowsum(dcv * s1[hl - 2 + k:hl - 2 + k + ts])

        form_dy(2)
        g_sgu = vec_ref[0:1]
        xhat, rstd = _ln_stats(col(zc, 4))
        vn = (xhat * g_sgu).astype(BF16)
        grp = _lane_group((SGU_CHUNK, w))
        lane = lax.broadcasted_iota(jnp.int32, (SGU_CHUNK, SGU_CHUNK), 1)
        tril = lax.broadcasted_iota(jnp.int32, (SGU_CHUNK, SGU_CHUNK), 0) >= lane
        for c in range(nch):
            rows = slice(c * SGU_CHUNK, (c + 1) * SGU_CHUNK)
            vnc = vn[rows]
            mixed = _sgu_mix(wt_ref, vnc) + bexp_ref[...]
            dyb = dyc[rows, w:2 * w]
            dz_ref[rows, 3 * w:4 * w] = (dyb * mixed).astype(BF16)
            dmix = dyb * zc[rows, 3 * w:4 * w]
            dmixb = dmix.astype(BF16)
            dvn = jnp.zeros((SGU_CHUNK, w), F32)
            gb = jnp.zeros((SGU_CHUNK, SGU_CHUNK), F32)
            for hd in range(N_HEADS):
                dvn = jnp.where(grp == hd, _dot_tn(wt_ref[hd], dmixb), dvn)
                dm_h = jnp.where(grp == hd, dmix, 0.0)
                gwt_ref[hd] += jnp.where(tril, _dot_nt(dm_h.astype(BF16), vnc), 0.0)
                gb = gb + jnp.where(lane == hd, jnp.sum(dm_h, axis=1, keepdims=True), 0.0)
            gb_ref[...] += gb
            s3[rows] = dvn
        dvn = s3[0:ts]
        gvec_ref[3:4] += rowsum(dvn * xhat)
        dz_ref[:, 4 * w:5 * w] = _ln_bwd(xhat, rstd, dvn * g_sgu).astype(BF16)

        form_dy(3)
        sig_c = _sigmoid(col(zc, 6))
        s1[0:hl] = prev(5) * _sigmoid(prev(6))
        s1[hl:hl + ts] = col(zc, 5) * sig_c
        s1[hl + ts:hl + ts + hl] = nxt(5) * _sigmoid(nxt(6))
        off = hl - (CCONV_K - 1)
        _make_shifts(s1, sh1, ts + 2 * hl - 8)
        cv = cconv_ref[0:1] * _rows_at(s1, sh1, off, ext)
        for k in range(1, CCONV_K):
            cv = cv + cconv_ref[k:k + 1] * _rows_at(s1, sh1, off + k, ext)
        xhat, rstd = _ln_stats(cv)
        ln = xhat * vec_ref[1:2] + vec_ref[2:3]
        sg = _sigmoid(ln)
        s2[0:ts] = col(dyc, 2)
        s2[ts:ext] = dnext(2)
        dln = s2[0:ext] * (sg * (1.0 + ln * (1.0 - sg)))
        gvec_ref[4:5] += rowsum(dln[0:ts] * xhat[0:ts])
        gvec_ref[5:6] += rowsum(dln[0:ts])
        s3[0:ext] = _ln_bwd(xhat, rstd, dln * vec_ref[1:2])
        _make_shifts(s3, sh3, ext - 8)
        dyg = cconv_ref[0:1] * _rows_at(s3, sh3, CCONV_K - 1, ts)
        for k in range(1, CCONV_K):
            dyg = dyg + cconv_ref[k:k + 1] * _rows_at(s3, sh3, CCONV_K - 1 - k, ts)
        dz_ref[:, 5 * w:6 * w] = (dyg * sig_c).astype(BF16)
        dz_ref[:, 6 * w:7 * w] = (dyg * col(zc, 5) * sig_c * (1.0 - sig_c)).astype(BF16)
        dcv = s3[0:ts]
        for k in range(CCONV_K):
            gcc_ref[k:k + 1] += rowsum(dcv * _rows_at(s1, sh1, off + k, ts))

        scale = vec_ref[3:4]
        s1[0:hl] = prev(7)
        s1[hl:hl + ts] = col(zc, 7)
        pooled = _pool_fwd(s1, s2, s3, i * ts, ts, hl).astype(BF16)
        q0 = _dot(pooled, pbd_ref[...])
        dyd = col(dyc, 3)
        gvec_ref[6:7] += rowsum(dyd * q0)
        dq = (dyd * scale).astype(BF16)
        gpbd_ref[...] += _dot_tn(pooled, dq)
        s1[0:ts] = _dot_nt(dq, pbd_ref[...])
        s1[ts:ext] = _dot_nt((dnext(3) * scale).astype(BF16), pbd_ref[...])
        dpool = s1[0:ts]
        s2[0:ext] = s1[0:ext] / _pool_count(i * ts, ext)
        s3[0:ts + 24] = s2[0:ts + 24] + s2[1:ts + 25]
        f2 = s3[0:ts]
        s2[0:ts + 16] = s3[0:ts + 16] + s3[2:ts + 18]
        f4 = s2[0:ts]
        s3[0:ts + 8] = s2[0:ts + 8] + s2[4:ts + 12]
        f8 = s3[0:ts]
        f16 = f8 + s3[8:ts + 8]
        dz_ref[:, 7 * w:8 * w] = (_by_group(_lane_group((ts, w)), f2, f4, f8, f16) - dpool).astype(BF16)

    full = lambda shape: pl.BlockSpec(shape, lambda i: (0,) * len(shape))
    r = ts // hl
    prev_map = lambda i: (jnp.maximum(i * r - 1, 0), 0)
    next_map = lambda i: (jnp.minimum((i + 1) * r, t // hl - 1), 0)
    return pl.pallas_call(
        body, name=name, grid=(ni,),
        out_shape=[jax.ShapeDtypeStruct((t, 8 * w), BF16), jax.ShapeDtypeStruct((8, w), F32),
                   jax.ShapeDtypeStruct((32, w), F32),
                   jax.ShapeDtypeStruct((N_HEADS, SGU_CHUNK, SGU_CHUNK), F32),
                   jax.ShapeDtypeStruct((SGU_CHUNK, SGU_CHUNK), F32), jax.ShapeDtypeStruct((w, w), F32)],
        in_specs=[pl.BlockSpec((ts, 8 * w), lambda i: (i, 0)),
                  pl.BlockSpec((hl, 8 * w), prev_map), pl.BlockSpec((hl, 8 * w), next_map),
                  pl.BlockSpec((ts, 4 * w), lambda i: (i, 0)), pl.BlockSpec((hl, 4 * w), next_map),
                  _wspec(wmo_g),
                  full((8, w)), full((32, w)), full((8, w)), full((N_HEADS, SGU_CHUNK, SGU_CHUNK)),
                  full((SGU_CHUNK, w)), full((w, w))],
        out_specs=[pl.BlockSpec((ts, 8 * w), lambda i: (i, 0)), full((8, w)), full((32, w)),
                   full((N_HEADS, SGU_CHUNK, SGU_CHUNK)), full((SGU_CHUNK, SGU_CHUNK)), full((w, w))],
        scratch_shapes=[pltpu.VMEM((ts + 2 * hl, w), F32)] * 3 + [pltpu.VMEM((8, ts + 2 * hl, w), F32)] * 2
        + [pltpu.VMEM((ts, 4 * w), F32), pltpu.VMEM((hl, 4 * w), F32)],
        compiler_params=_params("arbitrary"),
    )(z, z, z, dx, dx, wmo_g, sconv, cconv, vecs, wt, bexp, pbd)


def _attn_head(q, kv_ref, hd, d):
    hw = d // N_HEADS
    qh = q[:, hd * hw:(hd + 1) * hw]
    kh = kv_ref[:, hd * hw:(hd + 1) * hw].astype(BF16)
    vh = kv_ref[:, d + hd * hw:d + (hd + 1) * hw].astype(BF16)
    s = _dot_nt(qh, kh) * (1.0 / (hw ** 0.5))
    e = jnp.exp(s - jnp.max(s, axis=-1, keepdims=True))
    p = e / jnp.sum(e, axis=-1, keepdims=True)
    return qh, kh, vh, p


def xattn_fwd(x, gain, kv, wq_g, wo_g, *, name, tm=None):
    t, d = x.shape
    nm = kv.shape[0]
    tm = _row_tile(t, tm)
    hw = d // N_HEADS

    def body(x_ref, g_ref, kv_ref, wq_ref, wo_ref, o_ref):
        xv = x_ref[...]
        xhat, _ = _rms_fwd(xv, None)
        h = (xhat * g_ref[...]).astype(BF16)
        q = _dot(h, _full_weight(wq_ref, "row")).astype(BF16)
        wo = _full_weight(wo_ref, "row")
        out = xv
        for hd in range(N_HEADS):
            _, _, vh, p = _attn_head(q, kv_ref, hd, d)
            oh = _dot(p.astype(BF16), vh).astype(BF16)
            out = out + _dot(oh, wo[hd * hw:(hd + 1) * hw])
        o_ref[...] = out

    row = lambda i: (i, 0)
    return pl.pallas_call(
        body, name=name, grid=(t // tm,),
        out_shape=jax.ShapeDtypeStruct((t, d), F32),
        in_specs=[pl.BlockSpec((tm, d), row), pl.BlockSpec((1, d), lambda i: (0, 0)),
                  pl.BlockSpec((nm, 2 * d), lambda i: (0, 0)), _wspec(wq_g), _wspec(wo_g)],
        out_specs=pl.BlockSpec((tm, d), row),
        compiler_params=_params("parallel"),
    )(x, gain.reshape(1, d), kv, wq_g, wo_g)


def xattn_bwd_rows(x, dxn, gain, kv, wq_g, wo_g, after, *, name, tm=None):
    t, d = x.shape
    nm = kv.shape[0]
    tm = _row_tile(t, tm)
    hw = d // N_HEADS

    def body(x_ref, dxn_ref, g_ref, kv_ref, wq_ref, wo_ref, after_ref,
             dx_ref, h_ref, dq_ref, o_ref, dkv_ref, dg_ref):
        i = pl.program_id(0)

        @pl.when(i == 0)
        def _():
            dkv_ref[...] = jnp.zeros_like(dkv_ref)
            dg_ref[...] = jnp.zeros_like(dg_ref)
        g = g_ref[...]
        xhat, r = _rms_fwd(x_ref[...], None)
        h = (xhat * g).astype(BF16)
        h_ref[...] = h
        wq = _full_weight(wq_ref, "row")
        q = _dot(h, wq).astype(BF16)
        dxn = dxn_ref[...]
        do = _dot_nt(dxn.astype(BF16), _full_weight(wo_ref, "row")).astype(BF16)
        for hd in range(N_HEADS):
            cols = slice(hd * hw, (hd + 1) * hw)
            qh, kh, vh, p = _attn_head(q, kv_ref, hd, d)
            pb = p.astype(BF16)
            o_ref[:, cols] = _dot(pb, vh).astype(BF16)
            doh = do[:, cols]
            dkv_ref[:, d + hd * hw:d + (hd + 1) * hw] += _dot_tn(pb, doh)
            dp = _dot_nt(doh, vh)
            ds = (p * (dp - jnp.sum(dp * p, axis=-1, keepdims=True)) * (1.0 / (hw ** 0.5))).astype(BF16)
            dq_ref[:, cols] = _dot(ds, kh).astype(BF16)
            dkv_ref[:, cols] += _dot_tn(ds, qh)
        dh = _dot_nt(dq_ref[...], wq)
        dx, dg = _rms_bwd(xhat, r, g, dh)
        dx_ref[...] = dxn + dx
        dg_ref[...] += dg

    row = lambda i: (i, 0)
    fix = lambda i: (0, 0)
    return pl.pallas_call(
        body, name=name, grid=(t // tm,),
        out_shape=[jax.ShapeDtypeStruct((t, d), F32), jax.ShapeDtypeStruct((t, d), BF16),
                   jax.ShapeDtypeStruct((t, d), BF16), jax.ShapeDtypeStruct((t, d), BF16),
                   jax.ShapeDtypeStruct((nm, 2 * d), F32), jax.ShapeDtypeStruct((1, d), F32)],
        in_specs=[pl.BlockSpec((tm, d), row), pl.BlockSpec((tm, d), row), pl.BlockSpec((1, d), fix),
                  pl.BlockSpec((nm, 2 * d), fix), _wspec(wq_g), _wspec(wo_g), ANY_SPEC],
        out_specs=[pl.BlockSpec((tm, d), row)] * 4 + [pl.BlockSpec((nm, 2 * d), fix),
                                                      pl.BlockSpec((1, d), fix)],
        compiler_params=_params("arbitrary"),
    )(x, dxn, gain.reshape(1, d), kv, wq_g, wo_g, after)


def loss_head(x, target, gain, *, name, tm=None):
    t, d = x.shape
    tm = _row_tile(t, tm)

    def body(x_ref, t_ref, g_ref, dx_ref, dg_ref, loss_ref):
        @pl.when(pl.program_id(0) == 0)
        def _():
            dg_ref[...] = jnp.zeros_like(dg_ref)
            loss_ref[...] = jnp.zeros_like(loss_ref)
        g = g_ref[...]
        xhat, r = _rms_fwd(x_ref[...], None)
        err = xhat * g - t_ref[...]
        loss_ref[...] += 0.5 * jnp.sum(jnp.sum(err * err, axis=-1, keepdims=True) / d,
                                       axis=0, keepdims=True)
        dx, dg = _rms_bwd(xhat, r, g, err / d)
        dx_ref[...] = dx
        dg_ref[...] += dg

    row = lambda i: (i, 0)
    fix = lambda i: (0, 0)
    return pl.pallas_call(
        body, name=name, grid=(t // tm,),
        out_shape=[jax.ShapeDtypeStruct((t, d), F32), jax.ShapeDtypeStruct((1, d), F32),
                   jax.ShapeDtypeStruct((1, 1), F32)],
        in_specs=[pl.BlockSpec((tm, d), row), pl.BlockSpec((tm, d), row), pl.BlockSpec((1, d), fix)],
        out_specs=[pl.BlockSpec((tm, d), row), pl.BlockSpec((1, d), fix), pl.BlockSpec((1, 1), fix)],
        compiler_params=_params("arbitrary"),
    )(x, target, gain.reshape(1, d))


def _adamw_math(w, g, m, v):
    m = ADAM_B1 * m + (1.0 - ADAM_B1) * g
    v = ADAM_B2 * v + (1.0 - ADAM_B2) * (g * g)
    m_hat = m / (1.0 - ADAM_B1 ** ADAM_STEP)
    v_hat = v / (1.0 - ADAM_B2 ** ADAM_STEP)
    delta = -ADAM_LR * (m_hat / (jnp.sqrt(v_hat) + ADAM_EPS) + ADAM_WD * w)
    return delta, m, v


def adamw_sharded(own, lands, w, m, v, me_arr, *, name):
    nl, r, c = w.shape
    assert nl == len(own) == len(lands) == 2
    tr = next(cand for cand in (*ROW_BLOCKS, r) if r % cand == 0)
    nr = r // tr

    def body(me_ref, o0, o1, l0, l1, w_ref, m_ref, v_ref, g_out, d_out, m_out, v_out):
        def total(o_ref, l_ref):
            acc = o_ref[...].astype(F32)
            for p in range(N_DEV - 1):
                acc = acc + l_ref[p].astype(F32)
            return acc
        g = jnp.where(pl.program_id(0) == 0, total(o0, l0), total(o1, l1))
        delta, mn, vn = _adamw_math(w_ref[...], g, m_ref[...], v_ref[...])
        g_out[...] = g
        d_out[...] = delta
        m_out[...] = mn
        v_out[...] = vn

    row0 = lambda l, i: jnp.where(l == 0, i, nr - 1)
    row1 = lambda l, i: jnp.where(l == 1, i, 0)
    blk = pl.BlockSpec((None, tr, c), lambda l, i, me: (l, i, 0))
    grid_spec = pltpu.PrefetchScalarGridSpec(
        num_scalar_prefetch=1, grid=(nl, nr),
        in_specs=[pl.BlockSpec((None, tr, c), lambda l, i, me: (me[0], row0(l, i), 0)),
                  pl.BlockSpec((None, tr, c), lambda l, i, me: (me[0], row1(l, i), 0)),
                  pl.BlockSpec((N_DEV - 1, tr, c), lambda l, i, me: (0, row0(l, i), 0)),
                  pl.BlockSpec((N_DEV - 1, tr, c), lambda l, i, me: (0, row1(l, i), 0)),
                  blk, blk, blk],
        out_specs=[blk] * 4)
    return pl.pallas_call(
        body, name=name, grid_spec=grid_spec,
        out_shape=[jax.ShapeDtypeStruct((nl, r, c), F32)] * 4,
        compiler_params=_params("arbitrary", "arbitrary"),
    )(me_arr, own[0], own[1], lands[0], lands[1], w, m, v)


def adamw_many(gs, ws, ms, vs, *, name):
    n = len(ws)
    shapes = [w.shape for w in ws]
    as2d = lambda a: a.reshape(1, -1) if a.ndim == 1 else a

    def body(*refs):
        g_r, w_r, m_r, v_r = refs[:n], refs[n:2 * n], refs[2 * n:3 * n], refs[3 * n:4 * n]
        outs = refs[4 * n:]
        for i in range(n):
            delta, mn, vn = _adamw_math(w_r[i][...], g_r[i][...], m_r[i][...], v_r[i][...])
            outs[3 * i][...] = delta
            outs[3 * i + 1][...] = mn
            outs[3 * i + 2][...] = vn

    operands = [as2d(a) for group in (gs, ws, ms, vs) for a in group]
    out_shape = [jax.ShapeDtypeStruct(as2d(w).shape, F32) for w in ws for _ in range(3)]
    outs = pl.pallas_call(
        body, name=name, out_shape=out_shape,
        in_specs=[VMEM_SPEC] * (4 * n), out_specs=[VMEM_SPEC] * (3 * n),
        compiler_params=pltpu.CompilerParams(vmem_limit_bytes=VMEM_LIMIT),
    )(*operands)
    return [tuple(outs[3 * i + k].reshape(shapes[i]) for k in range(3)) for i in range(n)]


def cast_into_slot(a, layer, me_arr, *, name, dtype=None, after=None):
    dtype = BF16 if dtype is None else dtype
    _, r, c = a.shape
    tr = next(cand for cand in (*ROW_BLOCKS, r) if r % cand == 0)
    extra = [] if after is None else [after]

    def body(me_ref, a_ref, *rest):
        rest[-1][...] = a_ref[...].astype(dtype)

    grid_spec = pltpu.PrefetchScalarGridSpec(
        num_scalar_prefetch=1, grid=(r // tr,),
        in_specs=[pl.BlockSpec((None, tr, c), lambda i, me: (layer, i, 0))] + [ANY_SPEC] * len(extra),
        out_specs=pl.BlockSpec((None, tr, c), lambda i, me: (me[0], i, 0)))
    return pl.pallas_call(
        body, name=name, grid_spec=grid_spec,
        out_shape=jax.ShapeDtypeStruct((N_DEV, r, c), dtype),
        compiler_params=_params("parallel"),
    )(me_arr, a, *extra)


def _pack(arrs, rows):
    flat = jnp.concatenate([a.reshape(-1).astype(F32) for a in arrs])
    pad = rows * 128 - flat.shape[0]
    assert pad >= 0
    if pad:
        flat = jnp.concatenate([flat, jnp.zeros((pad,), F32)])
    return flat.reshape(rows, 128)


def _unpack(packed, shapes):
    flat = packed.reshape(-1)
    out, pos = [], 0
    for s in shapes:
        n = 1
        for dim in s:
            n *= dim
        out.append(flat[pos:pos + n].reshape(s))
        pos += n
    return out


def _rows_for(shapes):
    n = 0
    for s in shapes:
        k = 1
        for dim in s:
            k *= dim
        n += k
    return -(-n // 1024) * 8


GATHER_GROUPS = (("ffn1", ("ffn1_w_in", "ffn1_w_out")),
                 ("mid", ("mix_w_in", "mix_w_out", "xattn_wkv", "xattn_wq", "xattn_wo")),
                 ("ffn2", ("ffn2_w_in", "ffn2_w_out")))
SMALL_REPL = ["norm_ffn1", "norm_mix", "sgu_norm_g", "sgu_w", "sgu_b", "cconv_ln_g", "cconv_ln_b",
              "pool_w", "pool_scale", "norm_xattn", "norm_mem", "norm_ffn2", "norm_final"]
SMALL_SHARD = ["sconv_w", "cconv_w"]
TRANSPOSED = ("ffn1_w_in", "ffn2_w_in")
WEIGHTS = ["norm_ffn1", "ffn1_w_in", "ffn1_w_out", "norm_mix", "mix_w_in", "sconv_w", "sgu_norm_g",
           "sgu_w", "sgu_b", "cconv_w", "cconv_ln_g", "cconv_ln_b", "pool_w", "pool_scale", "mix_w_out",
           "norm_xattn", "norm_mem", "xattn_wq", "xattn_wkv", "xattn_wo", "norm_ffn2", "ffn2_w_in",
           "ffn2_w_out", "norm_final"]


def kernel(x, mem, norm_ffn1, ffn1_w_in, ffn1_w_out, norm_mix, mix_w_in, sconv_w, sgu_norm_g, sgu_w, sgu_b, cconv_w, cconv_ln_g, cconv_ln_b, pool_w, pool_scale, mix_w_out, norm_xattn, norm_mem, xattn_wq, xattn_wkv, xattn_wo, norm_ffn2, ffn2_w_in, ffn2_w_out, norm_final, loss_target, m_norm_ffn1, m_ffn1_w_in, m_ffn1_w_out, m_norm_mix, m_mix_w_in, m_sconv_w, m_sgu_norm_g, m_sgu_w, m_sgu_b, m_cconv_w, m_cconv_ln_g, m_cconv_ln_b, m_pool_w, m_pool_scale, m_mix_w_out, m_norm_xattn, m_norm_mem, m_xattn_wq, m_xattn_wkv, m_xattn_wo, m_norm_ffn2, m_ffn2_w_in, m_ffn2_w_out, m_norm_final, v_norm_ffn1, v_ffn1_w_in, v_ffn1_w_out, v_norm_mix, v_mix_w_in, v_sconv_w, v_sgu_norm_g, v_sgu_w, v_sgu_b, v_cconv_w, v_cconv_ln_g, v_cconv_ln_b, v_pool_w, v_pool_scale, v_mix_w_out, v_norm_xattn, v_norm_mem, v_xattn_wq, v_xattn_wkv, v_xattn_wo, v_norm_ffn2, v_ffn2_w_in, v_ffn2_w_out, v_norm_final):
    args = dict(locals())
    wts = {n: args[n] for n in WEIGHTS}
    mom = {n: args["m_" + n] for n in WEIGHTS}
    var = {n: args["v_" + n] for n in WEIGHTS}
    for n in TRANSPOSED:
        wts[n], mom[n], var[n] = (jnp.swapaxes(a, 1, 2) for a in (wts[n], mom[n], var[n]))
    x0 = x[0]
    mem0 = mem[0]
    target = loss_target[0]
    t, d = x0.shape
    nl = norm_ffn1.shape[0]
    w = MIX_W
    me = _my_index()

    me_arr = jnp.reshape(me, (1,)).astype(jnp.int32)

    small_g = all_gather([sconv_w, cconv_w], name="gather_conv_taps")
    sconv_full = jnp.transpose(small_g[0], (1, 2, 0, 3)).reshape(nl, SCONV_K, w)
    cconv_full = jnp.transpose(small_g[1], (1, 2, 0, 3)).reshape(nl, CCONV_K, w)
    pending = {}
    token = small_g[1]
    masks = GATHER_MASKS
    keys = [(gname, l, members) for l in range(nl) for gname, members in GATHER_GROUPS]
    first = [[cast_into_slot(wts[n], keys[0][1], me_arr, name=f"cast_{n}{keys[0][1]}") for n in keys[0][2]]]
    started, token = gather_start_groups(first, token, name="gather_start_first", masks=masks)
    casts = [[cast_into_slot(wts[n], l, me_arr, name=f"cast_{n}{l}", after=token) for n in members]
             for gname, l, members in keys[1:]]
    rest, token = gather_start_groups(casts, token, name="gather_start_rest", masks=masks)
    for (gname, l, members), (send, recv, gs) in zip(keys, started + rest):
        pending[gname, l] = (members, gs, send, recv, masks)
    wg = [dict() for _ in range(nl)]

    handing_over = {}

    def arrive_early(gname, l, after):
        members, gs, send, recv, masks = pending.pop((gname, l))
        gs = gather_wait(gs, send, recv, after, name=f"gather_wait_{gname}{l}", masks=masks)
        fsend, frecv, gs, _ = forward_start(gs, after, name=f"gather_forward_start_{gname}{l}")
        handing_over[gname, l] = (members, gs, fsend, frecv)

    def arrive(gname, l, after):
        if (gname, l) in handing_over:
            members, gs, fsend, frecv = handing_over.pop((gname, l))
            gs = forward_wait(gs, fsend, frecv, after, name=f"gather_forward_wait_{gname}{l}")
        else:
            members, gs, send, recv, masks = pending.pop((gname, l))
            gs = gather_wait(gs, send, recv, after, name=f"gather_wait_{gname}{l}", masks=masks)
            gs = sibling_forward(gs, name=f"gather_forward_{gname}{l}")
        wg[l].update(zip(members, gs))
    sconv_pad = jnp.pad(sconv_full, ((0, 0), (0, 8 - SCONV_K), (0, 0)))
    cconv_pad = jnp.pad(cconv_full, ((0, 0), (0, 32 - CCONV_K), (0, 0)))
    zeros_w = jnp.zeros((nl, w), F32)
    vecs = jnp.stack([sgu_norm_g, cconv_ln_g, cconv_ln_b, pool_scale] + [zeros_w] * 4, axis=1)
    wt = jnp.tril(sgu_w).astype(BF16)
    bexp = jnp.repeat(jnp.swapaxes(sgu_b, 1, 2), w // N_HEADS, axis=2)
    eye = jnp.eye(4, dtype=F32)
    pbd = jnp.einsum("lgcd,gh->lgchd", pool_w, eye).reshape(nl, w, w).astype(BF16)

    def mixer_args(l):
        return sconv_pad[l], cconv_pad[l], vecs[l], wt[l], bexp[l], pbd[l]

    saved = []
    xc = x0
    after = token
    for l in range(nl):
        s = {"x_ffn1": xc}
        arrive("ffn1", l, after)
        xc, s["gu_ffn1"] = ffn_fwd(xc, norm_ffn1[l], wg[l]["ffn1_w_in"], wg[l]["ffn1_w_out"],
                                   name=f"ffn1_fwd{l}", tm=FFN_FWD_TILE)
        s["x_mix"] = xc
        arrive("mid", l, xc)
        z = mm_rows(xc, wg[l]["mix_w_in"], "col", gain=norm_mix[l], name=f"mix_in{l}")
        y, xc = mixer_fwd(z, *mixer_args(l), xc, wg[l]["mix_w_out"], name=f"mixer_fwd{l}")
        s["z"], s["y"] = z, y
        s["x_att"] = xc
        kv = mm_rows(mem0, wg[l]["xattn_wkv"], "col", gain=norm_mem[l], name=f"kv{l}")
        s["kv"] = kv
        if l > 0:
            arrive_early("ffn2", l, kv)
        xc = xattn_fwd(xc, norm_xattn[l], kv, wg[l]["xattn_wq"], wg[l]["xattn_wo"], name=f"xattn_fwd{l}")
        s["x_ffn2"] = xc
        arrive("ffn2", l, xc)
        xc, s["gu_ffn2"] = ffn_fwd(xc, norm_ffn2[l], wg[l]["ffn2_w_in"], wg[l]["ffn2_w_out"],
                                   name=f"ffn2_fwd{l}", tm=FFN_FWD_TILE)
        after = xc
        saved.append(s)

    dx, g_norm_final, loss_local = loss_head(xc, target, norm_final, name="loss_head")

    tm = _row_tile(t, TN_TILE)
    small ={n: [None] * nl for n in SMALL_REPL + SMALL_SHARD if n != "norm_final"}
    scattered = {}
    tie = [token]

    def send_grads(gname, l, grads):
        members = list(grads)
        send, recv, gs, lands, tie[0] = scatter_start(
            [grads[n] for n in members], tie[0], name=f"scatter_start_{gname}{l}")
        scattered[gname, l] = (members, gs, lands, send, recv)

    names = SMALL_REPL + SMALL_SHARD + ["loss"]
    small_pending = []

    def start_small():
        small_full = {n: jnp.stack(v) for n, v in small.items()}
        small_full["norm_final"] = g_norm_final[0]
        small_full["loss"] = loss_local[0]
        shapes = [small_full[n].shape for n in names]
        packed = _pack([small_full[n] for n in names], _rows_for(shapes))
        slot = cast_into_slot(packed[None], 0, me_arr, name="small_into_slot", dtype=F32)
        send, recv, gs, tie[0] = gather_start([slot], tie[0], name="small_gather_start", masks=ALL_MASKS)
        small_pending.append((gs, send, recv, shapes))

    def ffn_backward(which, l, x_in, dy, gu, gain):
        w_in, w_out = wg[l][which + "_w_in"], wg[l][which + "_w_out"]
        dx_, h_, act, dgu, dgn, dyh = ffn_bwd_rows(x_in, dy, gu, gain, w_in, w_out, tie[0],
                                                   name=f"{which}_bwd{l}_rows")
        small["norm_" + which][l] = dgn[0]
        last = which == "ffn1" and l == 0
        if last:
            start_small()
        g_in = ffn_grad_w_in(h_, dgu, tie[0], name=f"{which}_bwd{l}_dwin")
        if last:
            send_grads(which + "_in", l, {which + "_w_in": g_in})
        g_out = ffn_grad_w_out(act, dyh, tie[0], name=f"{which}_bwd{l}_dwout")
        if last:
            send_grads(which + "_out", l, {which + "_w_out": g_out})
        else:
            send_grads(which, l, {which + "_w_in": g_in, which + "_w_out": g_out})
        return dx_

    for l in reversed(range(nl)):
        s = saved[l]
        wl = wg[l]
        dx = ffn_backward("ffn2", l, s["x_ffn2"], dx, s["gu_ffn2"], norm_ffn2[l])

        bg = {}
        dxn = dx
        dx, h, dq, o, dkv, dgn = xattn_bwd_rows(
            s["x_att"], dxn, norm_xattn[l], s["kv"], wl["xattn_wq"], wl["xattn_wo"], tie[0],
            name=f"xattn_bwd{l}")
        small["norm_xattn"][l] = dgn[0]
        row_spec = pl.BlockSpec((tm, d), lambda s_, i: (i, 0))
        bg["xattn_wq"] = mm_tn(h, dq, nb=1, ka=d, nbk=d, tm=tm, m=t, a_spec=row_spec, b_spec=row_spec,
                               name=f"dwq{l}").reshape(N_DEV, d // N_DEV, d)
        bg["xattn_wo"] = mm_tn(o, dxn, nb=1, ka=d, nbk=d, tm=tm, m=t, a_spec=row_spec, b_spec=row_spec,
                               name=f"dwo{l}").reshape(N_DEV, d // N_DEV, d)
        _, mhat, dgn = mm_nt(dkv, wl["xattn_wkv"], "col", x=mem0, gain=norm_mem[l], name=f"dmem{l}")
        small["norm_mem"][l] = dgn[0]
        nm = mem0.shape[0]
        bg["xattn_wkv"] = mm_tn(mhat, dkv, nb=N_DEV, ka=d, nbk=2 * d // N_DEV, tm=nm, m=nm,
                                a_spec=pl.BlockSpec((nm, d), lambda s_, i: (0, 0)),
                                b_spec=pl.BlockSpec((nm, 2 * d // N_DEV), lambda s_, i: (0, s_)),
                                name=f"dwkv{l}")
        send_grads("xattn", l, bg)

        bg = {}
        dxn = dx
        bg["mix_w_out"] = mm_tn(s["y"], dxn, nb=1, ka=d, nbk=d, tm=tm, m=t, a_spec=row_spec,
                                b_spec=row_spec, name=f"dwmo{l}").reshape(N_DEV, d // N_DEV, d)
        dz, gvec, gcc, gwt, gb, gpbd = mixer_bwd(s["z"], dxn, wl["mix_w_out"], *mixer_args(l),
                                                 name=f"mixer_bwd{l}")
        small["sconv_w"][l] = gvec[0:SCONV_K]
        small["sgu_norm_g"][l] = gvec[3]
        small["cconv_ln_g"][l] = gvec[4]
        small["cconv_ln_b"][l] = gvec[5]
        small["pool_scale"][l] = gvec[6]
        small["cconv_w"][l] = gcc[0:CCONV_K]
        small["sgu_w"][l] = gwt
        small["sgu_b"][l] = jnp.transpose(gb[:, 0:N_HEADS])
        gw = w // 4
        small["pool_w"][l] = jnp.stack([gpbd[g * gw:(g + 1) * gw, g * gw:(g + 1) * gw] for g in range(4)])
        dx, h, dgn = mm_nt(dz, wl["mix_w_in"], "col", x=s["x_mix"], gain=norm_mix[l], dx_in=dxn,
                           after=tie[0], name=f"dh_mix{l}")
        small["norm_mix"][l] = dgn[0]
        th = _row_tile(t, TN_TILE // 2)
        bg["mix_w_in"] = mm_tn(h, dz, nb=1, ka=d, nbk=N_DEV * w, tm=th, m=t, col_slots=N_DEV,
                               a_spec=pl.BlockSpec((th, d), lambda s_, i: (i, 0)),
                               b_spec=pl.BlockSpec((th, N_DEV * w), lambda s_, i: (i, 0)), name=f"dwmi{l}")
        send_grads("mix", l, bg)

        dx = ffn_backward("ffn1", l, s["x_ffn1"], dx, s["gu_ffn1"], norm_ffn1[l])

    out = {}
    own, land = {}, {}

    def collect(keys, after):
        for gname, l in keys:
            members, gs, lands, send, recv = scattered.pop((gname, l))
            gs, lands = scatter_wait(gs, lands, send, recv, after, name=f"scatter_wait_{gname}{l}")
            for n, g_, l_ in zip(members, gs, lands):
                own.setdefault(n, {})[l] = g_
                land.setdefault(n, {})[l] = l_

    def update(ns, after):
        for n in ns:
            out[n] = adamw_sharded([own[n][l] for l in range(nl)], [land[n][l] for l in range(nl)],
                                   wts[n], mom[n], var[n], me_arr, name="adamw_" + n)
            after = out[n][1]
        return after

    after = tie[0]
    for gname in ("ffn2", "xattn", "mix"):
        collect([(gname, l) for l in reversed(range(nl))], after)
        after = update([n for n in own if n not in out], after)
    (gs, send, recv, shapes), = small_pending
    gs = gather_wait(gs, send, recv, after, name="small_gather_wait", masks=ALL_MASKS)
    summed = sum_slots(gs[0], name="small_sum")
    gsm = dict(zip(names, _unpack(summed, shapes)))
    loss = gsm["loss"][0]
    cs = w // N_DEV
    for n in SMALL_SHARD:
        gsm[n] = lax.dynamic_slice_in_dim(gsm[n], me * cs, cs, axis=2)
    small_names = SMALL_REPL + SMALL_SHARD
    upd = adamw_many([gsm[n] for n in small_names], [wts[n] for n in small_names],
                     [mom[n] for n in small_names], [var[n] for n in small_names], name="adamw_small")
    for n, (a, b, c) in zip(small_names, upd):
        out[n] = (gsm[n], a, b, c)
    after = upd[0][0]
    collect([("ffn1", l) for l in reversed(range(1, nl))] + [("ffn1_in", 0)], after)
    after = update(["ffn1_w_in"], after)
    collect([("ffn1_out", 0)], after)
    update(["ffn1_w_out"], after)
    for n in TRANSPOSED:
        out[n] = tuple(jnp.swapaxes(a, 1, 2) for a in out[n])

    grad_x = dx.reshape(1, t, d)
    return (loss, grad_x, *[out[n][0] for n in WEIGHTS], *[out[n][1] for n in WEIGHTS],
            *[out[n][2] for n in WEIGHTS], *[out[n][3] for n in WEIGHTS])
```

```python
import jax
import jax.numpy as jnp
from jax import lax
from jax.experimental import pallas as pl
from jax.experimental.pallas import tpu as pltpu

F32 = jnp.float32
BF16 = jnp.bfloat16
MESH = pl.DeviceIdType.MESH
N_DEV = 8
EPS = 1e-6
HALO = 32
SGU_CHUNK = 128
CCONV_K = 31
SCONV_K = 3
MIX_W = 256
N_HEADS = 4
VMEM_LIMIT = 56 * 1024 * 1024
ROW_TILE = 512
TN_TILE = 2048
FFN_FWD_TILE = 1024
FFN_BWD_SPLIT = 2
ROW_BLOCKS = (256, 176, 128)
MIX_TILE = 512

ADAM_LR = 0.001
ADAM_B1 = 0.9
ADAM_B2 = 0.999
ADAM_EPS = 1e-08
ADAM_WD = 0.01
ADAM_STEP = 10

HBM_SPEC = pl.BlockSpec(memory_space=pltpu.HBM)
VMEM_SPEC = pl.BlockSpec(memory_space=pltpu.VMEM)


def _params(*sem):
    return pltpu.CompilerParams(dimension_semantics=tuple(sem), vmem_limit_bytes=VMEM_LIMIT)


def _row_tile(m, pref=None):
    t = min(m, ROW_TILE if pref is None else pref)
    assert m % t == 0, (m, t)
    return t


def _my_index():
    return lax.axis_index("x") * 4 + lax.axis_index("y") * 2 + lax.axis_index("c")


def _peer(mask):
    x, y, c = lax.axis_index("x"), lax.axis_index("y"), lax.axis_index("c")
    px = 1 - x if mask & 4 else x
    py = 1 - y if mask & 2 else y
    pc = 1 - c if mask & 1 else c
    return (px, py, pc), px * 4 + py * 2 + pc


def all_gather(arrs, name):
    n = len(arrs)

    def body(*refs):
        ins, outs = refs[:n], refs[n:2 * n]
        send_sems, recv_sems, loc_sems = refs[2 * n:]
        me = _my_index()
        local = []
        for i in range(n):
            cp = pltpu.make_async_copy(ins[i], outs[i].at[me], loc_sems.at[i])
            cp.start()
            local.append(cp)
        sends = []
        for i in range(n):
            for m in range(1, N_DEV):
                peer, _ = _peer(m)
                cp = pltpu.make_async_remote_copy(
                    src_ref=ins[i], dst_ref=outs[i].at[me],
                    send_sem=send_sems.at[i, m - 1], recv_sem=recv_sems.at[i, m - 1],
                    device_id=peer, device_id_type=MESH)
                cp.start()
                sends.append(cp)
        for i in range(n):
            for m in range(1, N_DEV):
                peer, pidx = _peer(m)
                pltpu.make_async_remote_copy(
                    src_ref=ins[i], dst_ref=outs[i].at[pidx],
                    send_sem=send_sems.at[i, m - 1], recv_sem=recv_sems.at[i, m - 1],
                    device_id=peer, device_id_type=MESH).wait_recv()
        for cp in sends:
            cp.wait_send()
        for cp in local:
            cp.wait()

    return pl.pallas_call(
        body, name=name,
        out_shape=[jax.ShapeDtypeStruct((N_DEV,) + a.shape, a.dtype) for a in arrs],
        in_specs=[HBM_SPEC] * n, out_specs=[HBM_SPEC] * n,
        scratch_shapes=[pltpu.SemaphoreType.DMA((n, N_DEV - 1)),
                        pltpu.SemaphoreType.DMA((n, N_DEV - 1)),
                        pltpu.SemaphoreType.DMA((n,))],
    )(*arrs)


SEM_SPEC = pl.BlockSpec(memory_space=pltpu.SEMAPHORE)
ANY_SPEC = pl.BlockSpec(memory_space=pl.ANY)
SIDE_EFFECT = pltpu.SideEffectType.DATAFLOW_SIDE_EFFECTING


def _hbm(a):
    return pltpu.with_memory_space_constraint(a, pltpu.HBM)


def _sem_pairs(n):
    return (pltpu.SemaphoreType.DMA((n * (N_DEV - 1),)), pltpu.SemaphoreType.DMA((n * (N_DEV - 1),)))


def _sem(i, m):
    return i * (N_DEV - 1) + m - 1


def _gather_copy(g_ref, i, m, send_sems, recv_sems, origin):
    peer, _ = _peer(m)
    return pltpu.make_async_remote_copy(
        src_ref=g_ref.at[origin], dst_ref=g_ref.at[origin],
        send_sem=send_sems.at[_sem(i, m)], recv_sem=recv_sems.at[_sem(i, m)],
        device_id=peer, device_id_type=MESH)


GATHER_MASKS = (1, 2, 4, 6)
FORWARD_MASKS = (2, 4, 6)


ALL_MASKS = tuple(range(1, N_DEV))


def gather_start(gs, after, name, masks=GATHER_MASKS):
    n = len(gs)

    def body(*refs):
        g_in = refs[:n]
        send_sems, recv_sems = refs[n + 1], refs[n + 2]
        token = refs[-1]
        me = _my_index()
        for i in range(n):
            for m in masks:
                _gather_copy(g_in[i], i, m, send_sems, recv_sems, me).start()
        token[...] = jnp.zeros_like(token)

    outs = pl.pallas_call(
        body, name=name,
        out_shape=(*_sem_pairs(n), *[pltpu.HBM(g.shape, g.dtype) for g in gs],
                   jax.ShapeDtypeStruct((8, 128), F32)),
        in_specs=[HBM_SPEC] * n + [ANY_SPEC],
        out_specs=(SEM_SPEC, SEM_SPEC, *[HBM_SPEC] * n, VMEM_SPEC),
        input_output_aliases={i: 2 + i for i in range(n)},
        compiler_params=pltpu.CompilerParams(has_side_effects=SIDE_EFFECT),
    )(*[_hbm(g) for g in gs], after)
    return outs[0], outs[1], list(outs[2:2 + n]), outs[-1]


def gather_start_groups(groups, after, name, masks=GATHER_MASKS):
    sizes = [len(g) for g in groups]
    flat = [a for g in groups for a in g]
    n, ng = len(flat), len(groups)

    def body(*refs):
        g_in = refs[:n]
        sems = refs[n + 1:n + 1 + 2 * ng]
        token = refs[-1]
        me = _my_index()
        pos = 0
        for k, size in enumerate(sizes):
            for i in range(size):
                for m in masks:
                    _gather_copy(g_in[pos + i], i, m, sems[2 * k], sems[2 * k + 1], me).start()
            pos += size
        token[...] = jnp.zeros_like(token)

    outs = pl.pallas_call(
        body, name=name,
        out_shape=(*[s for size in sizes for s in _sem_pairs(size)],
                   *[pltpu.HBM(g.shape, g.dtype) for g in flat], jax.ShapeDtypeStruct((8, 128), F32)),
        in_specs=[HBM_SPEC] * n + [ANY_SPEC],
        out_specs=(*[SEM_SPEC] * (2 * ng), *[HBM_SPEC] * n, VMEM_SPEC),
        input_output_aliases={i: 2 * ng + i for i in range(n)},
        compiler_params=pltpu.CompilerParams(has_side_effects=SIDE_EFFECT),
    )(*[_hbm(g) for g in flat], after)
    result, pos = [], 2 * ng
    for k, size in enumerate(sizes):
        result.append((outs[2 * k], outs[2 * k + 1], list(outs[pos:pos + size])))
        pos += size
    return result, outs[-1]


def gather_wait(gs, send_sems, recv_sems, after, name, masks=GATHER_MASKS):
    n = len(gs)

    def body(*refs):
        g_in = refs[:n]
        send, recv = refs[n], refs[n + 1]
        me = _my_index()
        for i in range(n):
            for m in masks:
                _, pidx = _peer(m)
                _gather_copy(g_in[i], i, m, send, recv, me).wait_send()
                _gather_copy(g_in[i], i, m, send, recv, pidx).wait_recv()

    outs = pl.pallas_call(
        body, name=name,
        out_shape=[pltpu.HBM(g.shape, g.dtype) for g in gs],
        in_specs=[HBM_SPEC] * n + [SEM_SPEC, SEM_SPEC, ANY_SPEC],
        out_specs=[HBM_SPEC] * n,
        input_output_aliases={i: i for i in range(n)},
        compiler_params=pltpu.CompilerParams(has_side_effects=SIDE_EFFECT),
    )(*gs, send_sems, recv_sems, after)
    return list(outs)


def sibling_forward(gs, name):
    n = len(gs)
    nf = len(FORWARD_MASKS)

    def body(*refs):
        g_in = refs[:n]
        send_sems, recv_sems = refs[2 * n:]
        x, y, c = lax.axis_index("x"), lax.axis_index("y"), lax.axis_index("c")
        sibling = (x, y, 1 - c)

        def copy(i, k, origin):
            return pltpu.make_async_remote_copy(
                src_ref=g_in[i].at[origin], dst_ref=g_in[i].at[origin],
                send_sem=send_sems.at[i * nf + k], recv_sem=recv_sems.at[i * nf + k],
                device_id=sibling, device_id_type=MESH)
        sends = []
        for i in range(n):
            for k, m in enumerate(FORWARD_MASKS):
                _, origin = _peer(m)
                cp = copy(i, k, origin)
                cp.start()
                sends.append(cp)
        for i in range(n):
            for k, m in enumerate(FORWARD_MASKS):
                _, origin = _peer(m ^ 1)
                copy(i, k, origin).wait_recv()
        for cp in sends:
            cp.wait_send()

    outs = pl.pallas_call(
        body, name=name,
        out_shape=[jax.ShapeDtypeStruct(g.shape, g.dtype) for g in gs],
        in_specs=[HBM_SPEC] * n, out_specs=[HBM_SPEC] * n,
        input_output_aliases={i: i for i in range(n)},
        scratch_shapes=[pltpu.SemaphoreType.DMA((n * nf,)), pltpu.SemaphoreType.DMA((n * nf,))],
    )(*gs)
    return list(outs)


def _forward_copy(g_ref, i, k, send_sems, recv_sems, origin):
    sibling = (lax.axis_index("x"), lax.axis_index("y"), 1 - lax.axis_index("c"))
    slot = i * len(FORWARD_MASKS) + k
    return pltpu.make_async_remote_copy(
        src_ref=g_ref.at[origin], dst_ref=g_ref.at[origin],
        send_sem=send_sems.at[slot], recv_sem=recv_sems.at[slot],
        device_id=sibling, device_id_type=MESH)


def forward_start(gs, after, name):
    n = len(gs)
    nsem = n * len(FORWARD_MASKS)

    def body(*refs):
        g_in = refs[:n]
        send_sems, recv_sems = refs[n + 1], refs[n + 2]
        token = refs[-1]
        for i in range(n):
            for k, m in enumerate(FORWARD_MASKS):
                _, origin = _peer(m)
                _forward_copy(g_in[i], i, k, send_sems, recv_sems, origin).start()
        token[...] = jnp.zeros_like(token)

    outs = pl.pallas_call(
        body, name=name,
        out_shape=(pltpu.SemaphoreType.DMA((nsem,)), pltpu.SemaphoreType.DMA((nsem,)),
                   *[pltpu.HBM(g.shape, g.dtype) for g in gs], jax.ShapeDtypeStruct((8, 128), F32)),
        in_specs=[HBM_SPEC] * n + [ANY_SPEC],
        out_specs=(SEM_SPEC, SEM_SPEC, *[HBM_SPEC] * n, VMEM_SPEC),
        input_output_aliases={i: 2 + i for i in range(n)},
        compiler_params=pltpu.CompilerParams(has_side_effects=SIDE_EFFECT),
    )(*[_hbm(g) for g in gs], after)
    return outs[0], outs[1], list(outs[2:2 + n]), outs[-1]


def forward_wait(gs, send_sems, recv_sems, after, name):
    n = len(gs)

    def body(*refs):
        g_in = refs[:n]
        send, recv = refs[n], refs[n + 1]
        for i in range(n):
            for k, m in enumerate(FORWARD_MASKS):
                _, mine = _peer(m)
                _, theirs = _peer(m ^ 1)
                _forward_copy(g_in[i], i, k, send, recv, mine).wait_send()
                _forward_copy(g_in[i], i, k, send, recv, theirs).wait_recv()

    outs = pl.pallas_call(
        body, name=name,
        out_shape=[pltpu.HBM(g.shape, g.dtype) for g in gs],
        in_specs=[HBM_SPEC] * n + [SEM_SPEC, SEM_SPEC, ANY_SPEC],
        out_specs=[HBM_SPEC] * n,
        input_output_aliases={i: i for i in range(n)},
        compiler_params=pltpu.CompilerParams(has_side_effects=SIDE_EFFECT),
    )(*gs, send_sems, recv_sems, after)
    return list(outs)


def _scatter_copy(g_ref, l_ref, i, m, send_sems, recv_sems):
    peer, pidx = _peer(m)
    return pltpu.make_async_remote_copy(
        src_ref=g_ref.at[pidx], dst_ref=l_ref.at[m - 1],
        send_sem=send_sems.at[_sem(i, m)], recv_sem=recv_sems.at[_sem(i, m)],
        device_id=peer, device_id_type=MESH)


def scatter_start(grads, after, name):
    n = len(grads)
    lands = [lax.empty((N_DEV - 1,) + g.shape[1:], g.dtype) for g in grads]

    def body(*refs):
        g_in, l_in = refs[:n], refs[n:2 * n]
        send_sems, recv_sems = refs[2 * n + 1], refs[2 * n + 2]
        token = refs[-1]
        for i in range(n):
            for m in range(1, N_DEV):
                _scatter_copy(g_in[i], l_in[i], i, m, send_sems, recv_sems).start()
        token[...] = jnp.zeros_like(token)

    outs = pl.pallas_call(
        body, name=name,
        out_shape=(*_sem_pairs(n), *[pltpu.HBM(g.shape, g.dtype) for g in grads],
                   *[pltpu.HBM(l.shape, l.dtype) for l in lands], jax.ShapeDtypeStruct((8, 128), F32)),
        in_specs=[HBM_SPEC] * (2 * n) + [ANY_SPEC],
        out_specs=(SEM_SPEC, SEM_SPEC, *[HBM_SPEC] * (2 * n), VMEM_SPEC),
        input_output_aliases={i: 2 + i for i in range(2 * n)},
        compiler_params=pltpu.CompilerParams(has_side_effects=SIDE_EFFECT),
    )(*[_hbm(g) for g in grads], *[_hbm(l) for l in lands], after)
    return outs[0], outs[1], list(outs[2:2 + n]), list(outs[2 + n:2 + 2 * n]), outs[-1]


def scatter_wait(grads, lands, send_sems, recv_sems, after, name):
    n = len(grads)

    def body(*refs):
        g_in, l_in = refs[:n], refs[n:2 * n]
        send, recv = refs[2 * n], refs[2 * n + 1]
        for i in range(n):
            for m in range(1, N_DEV):
                cp = _scatter_copy(g_in[i], l_in[i], i, m, send, recv)
                cp.wait_send()
                cp.wait_recv()

    outs = pl.pallas_call(
        body, name=name,
        out_shape=[pltpu.HBM(a.shape, a.dtype) for a in list(grads) + list(lands)],
        in_specs=[HBM_SPEC] * (2 * n) + [SEM_SPEC, SEM_SPEC, ANY_SPEC],
        out_specs=[HBM_SPEC] * (2 * n),
        input_output_aliases={i: i for i in range(2 * n)},
        compiler_params=pltpu.CompilerParams(has_side_effects=SIDE_EFFECT),
    )(*grads, *lands, send_sems, recv_sems, after)
    return list(outs[:n]), list(outs[n:])


def sum_slots(g, name):
    _, r, c = g.shape

    def body(g_ref, out_ref):
        acc = g_ref[0]
        for p in range(1, N_DEV):
            acc = acc + g_ref[p]
        out_ref[...] = acc

    return pl.pallas_call(
        body, name=name, out_shape=jax.ShapeDtypeStruct((r, c), F32),
        in_specs=[VMEM_SPEC], out_specs=VMEM_SPEC,
        compiler_params=pltpu.CompilerParams(vmem_limit_bytes=VMEM_LIMIT),
    )(g)


def _sigmoid(v):
    return 1.0 / (1.0 + jnp.exp(-v))


def _rms_fwd(xf, g):
    r = lax.rsqrt(jnp.mean(xf * xf, axis=-1, keepdims=True) + EPS)
    return xf * r, r


def _rms_bwd(xhat, r, g, dy):
    dg = jnp.sum(dy * xhat, axis=0, keepdims=True)
    dxh = dy * g
    dx = r * (dxh - xhat * jnp.mean(dxh * xhat, axis=-1, keepdims=True))
    return dx, dg


def _ln_stats(v):
    mu = jnp.mean(v, axis=-1, keepdims=True)
    vc = v - mu
    r = lax.rsqrt(jnp.mean(vc * vc, axis=-1, keepdims=True) + EPS)
    return vc * r, r


def _ln_bwd(xhat, r, dxh):
    return r * (dxh - jnp.mean(dxh, axis=-1, keepdims=True)
                - xhat * jnp.mean(dxh * xhat, axis=-1, keepdims=True))


def _dot(a, b):
    return jnp.dot(a, b, preferred_element_type=F32)


def _dot_nt(a, b):
    return lax.dot_general(a, b, (((1,), (1,)), ((), ())), preferred_element_type=F32)


def _dot_tn(a, b):
    return lax.dot_general(a, b, (((0,), (0,)), ((), ())), preferred_element_type=F32)


def _full_weight(w_ref, kind):
    assert kind == "row"
    p, a, b = w_ref.shape
    return w_ref[...].reshape(p * a, b)


def _wspec(wg):
    return pl.BlockSpec(wg.shape, lambda *_: (0, 0, 0))


def mm_rows(a, wg, kind, *, gain=None, out_dtype=F32, name, tm=None):
    m, k = a.shape
    p, wa, wb = wg.shape
    n = p * wb if kind == "col" else wb
    tm = _row_tile(m, tm)
    has_gain = gain is not None

    def body(*refs):
        refs = list(refs)
        a_ref = refs.pop(0)
        g_ref = refs.pop(0) if has_gain else None
        w_ref = refs.pop(0)
        o_ref = refs.pop(0)
        if has_gain:
            xhat, _ = _rms_fwd(a_ref[...].astype(F32), None)
            h = (xhat * g_ref[...]).astype(BF16)
        else:
            h = a_ref[...].astype(BF16)
        if kind == "col":
            for j in range(p):
                o_ref[:, j * wb:(j + 1) * wb] = _dot(h, w_ref[j]).astype(out_dtype)
        else:
            o_ref[...] = _dot(h, _full_weight(w_ref, "row")).astype(out_dtype)

    operands = [a]
    in_specs = [pl.BlockSpec((tm, k), lambda i: (i, 0))]
    if has_gain:
        operands.append(gain.reshape(1, k))
        in_specs.append(pl.BlockSpec((1, k), lambda i: (0, 0)))
    operands.append(wg)
    in_specs.append(_wspec(wg))
    return pl.pallas_call(
        body, name=name, grid=(m // tm,),
        out_shape=jax.ShapeDtypeStruct((m, n), out_dtype),
        in_specs=in_specs, out_specs=pl.BlockSpec((tm, n), lambda i: (i, 0)),
        compiler_params=_params("parallel"),
    )(*operands)


def mm_nt(dz, wg, kind, *, x=None, gain=None, dx_in=None, after=None, name, tm=None):
    m, n = dz.shape
    p, wa, wb = wg.shape
    k = wa if kind == "col" else p * wa
    tm = _row_tile(m, tm)
    epi = x is not None
    has_dx = dx_in is not None
    has_after = after is not None

    def body(*refs):
        refs = list(refs)
        dz_ref, w_ref = refs.pop(0), refs.pop(0)
        if epi:
            x_ref, g_ref = refs.pop(0), refs.pop(0)
            dxi_ref = refs.pop(0) if has_dx else None
        if has_after:
            refs.pop(0)
        if epi:
            dx_ref, h_ref, dg_ref = refs
        else:
            (da_ref,) = refs
        dzb = dz_ref[...].astype(BF16)
        if kind == "col":
            da = _dot_nt(dzb[:, 0:wb], w_ref[0])
            for j in range(1, p):
                da = da + _dot_nt(dzb[:, j * wb:(j + 1) * wb], w_ref[j])
        else:
            da = _dot_nt(dzb, _full_weight(w_ref, "row"))
        if not epi:
            da_ref[...] = da
            return
        g = g_ref[...]
        xhat, r = _rms_fwd(x_ref[...].astype(F32), None)
        h_ref[...] = (xhat * g).astype(BF16)
        dx, dg = _rms_bwd(xhat, r, g, da)
        if has_dx:
            dx = dx + dxi_ref[...]
        dx_ref[...] = dx

        @pl.when(pl.program_id(0) == 0)
        def _():
            dg_ref[...] = jnp.zeros_like(dg_ref)
        dg_ref[...] += dg

    row = lambda i: (i, 0)
    operands = [dz, wg]
    in_specs = [pl.BlockSpec((tm, n), row), _wspec(wg)]
    if epi:
        operands += [x, gain.reshape(1, k)]
        in_specs += [pl.BlockSpec((tm, k), row), pl.BlockSpec((1, k), lambda i: (0, 0))]
        if has_dx:
            operands.append(dx_in)
            in_specs.append(pl.BlockSpec((tm, k), row))
        out_shape = [jax.ShapeDtypeStruct((m, k), F32), jax.ShapeDtypeStruct((m, k), BF16),
                     jax.ShapeDtypeStruct((1, k), F32)]
        out_specs = [pl.BlockSpec((tm, k), row), pl.BlockSpec((tm, k), row),
                     pl.BlockSpec((1, k), lambda i: (0, 0))]
    else:
        out_shape = jax.ShapeDtypeStruct((m, k), F32)
        out_specs = pl.BlockSpec((tm, k), row)
    if has_after:
        operands.append(after)
        in_specs.append(ANY_SPEC)
    return pl.pallas_call(
        body, name=name, grid=(m // tm,), out_shape=out_shape,
        in_specs=in_specs, out_specs=out_specs,
        compiler_params=_params("arbitrary"),
    )(*operands)


def mm_tn(a, b, *, nb, a_spec, b_spec, ka, nbk, tm, m, scale=1.0, out_dtype=BF16, col_slots=1,
          after=None, name):
    ni = m // tm
    assert col_slots == 1 or nb == 1
    cw = nbk // col_slots
    extra = [] if after is None else [after]

    def body(a_ref, b_ref, *rest):
        o_ref, acc = rest[len(extra):]
        i = pl.program_id(1)

        @pl.when(i == 0)
        def _():
            acc[...] = jnp.zeros_like(acc)
        acc[...] += _dot_tn(a_ref[...].astype(BF16), b_ref[...].astype(BF16))

        @pl.when(i == ni - 1)
        def _():
            if col_slots == 1:
                o_ref[...] = (acc[...] * scale).astype(out_dtype)
            else:
                for j in range(col_slots):
                    o_ref[j] = (acc[:, j * cw:(j + 1) * cw] * scale).astype(out_dtype)

    if col_slots == 1:
        out_shape = jax.ShapeDtypeStruct((nb, ka, nbk), out_dtype)
        out_spec = pl.BlockSpec((None, ka, nbk), lambda s, i: (s, 0, 0))
    else:
        out_shape = jax.ShapeDtypeStruct((col_slots, ka, cw), out_dtype)
        out_spec = pl.BlockSpec((col_slots, ka, cw), lambda s, i: (0, 0, 0))
    return pl.pallas_call(
        body, name=name, grid=(nb, ni), out_shape=out_shape,
        in_specs=[a_spec, b_spec] + [ANY_SPEC] * len(extra), out_specs=out_spec,
        scratch_shapes=[pltpu.VMEM((ka, nbk), F32)],
        compiler_params=_params("parallel", "arbitrary"),
    )(a, b, *extra)


def _ffn_specs(w_in_g, w_out_g, d):
    nf = w_in_g.shape[1]
    hr = w_out_g.shape[1]
    assert 2 * hr == nf
    w_in5 = w_in_g.reshape(2, 4, nf, d)
    w_out5 = w_out_g.reshape(4, 2, hr, d)
    in_spec = pl.BlockSpec((2, None, nf, d), lambda i, j: (0, j, 0, 0))
    out_spec = pl.BlockSpec((None, 2, hr, d), lambda i, j: (j, 0, 0, 0))
    return w_in5, w_out5, in_spec, out_spec, nf


def ffn_fwd(x, gain, w_in_g, w_out_g, *, name, tm=None):
    t, d = x.shape
    tm = _row_tile(t, tm)
    w_in5, w_out5, wi_spec, wo_spec, nf = _ffn_specs(w_in_g, w_out_g, d)

    def body(x_ref, g_ref, wi_ref, wo_ref, o_ref, gu_ref, h_scr, acc):
        j = pl.program_id(1)

        @pl.when(j == 0)
        def _():
            xhat, _ = _rms_fwd(x_ref[...], None)
            h_scr[...] = (xhat * g_ref[...]).astype(BF16)
            acc[...] = jnp.zeros_like(acc)
        h = h_scr[...]
        gt = _dot_nt(h, wi_ref[0])
        up = _dot_nt(h, wi_ref[1])
        gu_ref[0] = gt.astype(BF16)
        gu_ref[1] = up.astype(BF16)
        act = (gt * _sigmoid(gt) * up).astype(BF16)
        acc[...] += _dot(act, wo_ref[...].reshape(nf, d))

        @pl.when(j == 3)
        def _():
            o_ref[...] = x_ref[...] + 0.5 * acc[...]

    return pl.pallas_call(
        body, name=name, grid=(t // tm, 4),
        out_shape=[jax.ShapeDtypeStruct((t, d), F32), jax.ShapeDtypeStruct((2, 4, t, nf), BF16)],
        in_specs=[pl.BlockSpec((tm, d), lambda i, j: (i, 0)),
                  pl.BlockSpec((1, d), lambda i, j: (0, 0)), wi_spec, wo_spec],
        out_specs=[pl.BlockSpec((tm, d), lambda i, j: (i, 0)),
                   pl.BlockSpec((2, None, tm, nf), lambda i, j: (0, j, i, 0))],
        scratch_shapes=[pltpu.VMEM((tm, d), BF16), pltpu.VMEM((tm, d), F32)],
        compiler_params=_params("parallel", "arbitrary"),
    )(x, gain.reshape(1, d), w_in5, w_out5)


def ffn_bwd_rows(x, dy, gu, gain, w_in_g, w_out_g, after, *, name, tm=None):
    t, d = x.shape
    tm = _row_tile(t, tm)
    w_in5, w_out5, wi_spec, wo_spec, nf = _ffn_specs(w_in_g, w_out_g, d)

    def body(x_ref, dy_ref, gu_ref, g_ref, wi_ref, wo_ref, after_ref, dx_ref, h_ref, act_ref, dgu_ref, dg_ref,
             dyh_scr, dxb_ref, dh_acc):
        i, j = pl.program_id(0), pl.program_id(1)

        @pl.when(j == 0)
        def _():
            xhat, _ = _rms_fwd(x_ref[...], None)
            h_ref[...] = (xhat * g_ref[...]).astype(BF16)
            dyh_scr[...] = (0.5 * dy_ref[...]).astype(BF16)
            dh_acc[...] = jnp.zeros_like(dh_acc)
        wo = wo_ref[...].reshape(nf, d)

        def gates(rows):
            gt = gu_ref[0, rows].astype(F32)
            up = gu_ref[1, rows].astype(F32)
            sg = _sigmoid(gt)
            silu = gt * sg
            act_ref[rows] = (silu * up).astype(BF16)
            return up * (sg * (1.0 + gt * (1.0 - sg))), silu

        def grads(rows, dact, dsilu_up, silu):
            dgt = (dact * dsilu_up).astype(BF16)
            dup = (dact * silu).astype(BF16)
            dgu_ref[0, rows] = dgt
            dgu_ref[1, rows] = dup
            return dgt, dup

        sub = tm // FFN_BWD_SPLIT
        parts = [slice(k * sub, (k + 1) * sub) for k in range(FFN_BWD_SPLIT)]
        dact = _dot_nt(dyh_scr[parts[0]], wo)
        gate = gates(parts[0])
        for k, rows in enumerate(parts):
            if k + 1 < len(parts):
                dact_next = _dot_nt(dyh_scr[parts[k + 1]], wo)
            dgt, dup = grads(rows, dact, *gate)
            dh_acc[rows] += _dot(dgt, wi_ref[0]) + _dot(dup, wi_ref[1])
            if k + 1 < len(parts):
                gate = gates(parts[k + 1])
                dact = dact_next

        @pl.when(j == 3)
        def _():
            g = g_ref[...]
            xhat, r = _rms_fwd(x_ref[...], None)
            dx, dg = _rms_bwd(xhat, r, g, dh_acc[...])
            dx_ref[...] = dy_ref[...] + dx
            dxb_ref[...] = (dy_ref[...] + dx).astype(BF16)

            @pl.when(i == 0)
            def _():
                dg_ref[...] = jnp.zeros_like(dg_ref)
            dg_ref[...] += dg

    row = lambda i, j: (i, 0)
    return pl.pallas_call(
        body, name=name, grid=(t // tm, 4),
        out_shape=[jax.ShapeDtypeStruct((t, d), F32), jax.ShapeDtypeStruct((t, d), BF16),
                   jax.ShapeDtypeStruct((4, t, nf), BF16), jax.ShapeDtypeStruct((2, 4, t, nf), BF16),
                   jax.ShapeDtypeStruct((1, d), F32), jax.ShapeDtypeStruct((t, d), BF16),
                   jax.ShapeDtypeStruct((t, d), BF16)],
        in_specs=[pl.BlockSpec((tm, d), row), pl.BlockSpec((tm, d), row),
                  pl.BlockSpec((2, None, tm, nf), lambda i, j: (0, j, i, 0)),
                  pl.BlockSpec((1, d), lambda i, j: (0, 0)), wi_spec, wo_spec, ANY_SPEC],
        out_specs=[pl.BlockSpec((tm, d), row), pl.BlockSpec((tm, d), row),
                   pl.BlockSpec((None, tm, nf), lambda i, j: (j, i, 0)),
                   pl.BlockSpec((2, None, tm, nf), lambda i, j: (0, j, i, 0)),
                   pl.BlockSpec((1, d), lambda i, j: (0, 0)), pl.BlockSpec((tm, d), row),
                   pl.BlockSpec((tm, d), row)],
        scratch_shapes=[pltpu.VMEM((tm, d), F32)],
        compiler_params=_params("arbitrary", "arbitrary"),
    )(x, dy, gu, gain.reshape(1, d), w_in5, w_out5, after)


def ffn_grad_w_in(h, dgu, after, *, name):
    t, d = h.shape
    nf = dgu.shape[-1]
    tm = _row_tile(t, TN_TILE)
    return mm_tn(dgu.reshape(8, t, nf), h, nb=8, ka=nf, nbk=d, tm=tm, m=t, after=after,
                 a_spec=pl.BlockSpec((None, tm, nf), lambda s, i: (s, i, 0)),
                 b_spec=pl.BlockSpec((tm, d), lambda s, i: (i, 0)), name=name)


def ffn_grad_w_out(act, dyh, after, *, name):
    _, t, nf = act.shape
    d = dyh.shape[1]
    tm = _row_tile(t, TN_TILE)
    d_w_out = mm_tn(act, dyh, nb=4, ka=nf, nbk=d, tm=tm, m=t, after=after,
                    a_spec=pl.BlockSpec((None, tm, nf), lambda s, i: (s, i, 0)),
                    b_spec=pl.BlockSpec((tm, d), lambda s, i: (i, 0)), name=name)
    return d_w_out.reshape(8, nf // 2, d)


def _lane_group(shape):
    return lax.shift_right_logical(lax.broadcasted_iota(jnp.int32, shape, 1), 6)


def _pool_count(t0, rows):
    t = (t0 + lax.broadcasted_iota(jnp.int32, (rows, MIX_W), 0) + 1).astype(F32)
    return jnp.minimum(t, _by_group(_lane_group((rows, MIX_W)), 2.0, 4.0, 8.0, 16.0))


def _by_group(grp, v0, v1, v2, v3):
    return jnp.where(grp == 0, v0, jnp.where(grp == 1, v1, jnp.where(grp == 2, v2, v3)))


def _sgu_mix(wt_ref, vnc):
    grp = _lane_group((SGU_CHUNK, MIX_W))
    out = jnp.zeros((SGU_CHUNK, MIX_W), F32)
    for hd in range(N_HEADS):
        out = jnp.where(grp == hd, _dot(wt_ref[hd], vnc), out)
    return out


def _pool_fwd(s1, s2, s3, t0, ts, lo):
    h = lo
    s2[h - 24:h + ts] = s1[h - 24:h + ts] + s1[h - 25:h + ts - 1]
    s3[h - 16:h + ts] = s2[h - 16:h + ts] + s2[h - 18:h + ts - 2]
    sum2 = s2[h:h + ts]
    sum4 = s3[h:h + ts]
    s2[h - 8:h + ts] = s3[h - 8:h + ts] + s3[h - 12:h + ts - 4]
    sum8 = s2[h:h + ts]
    sum16 = sum8 + s2[h - 8:h + ts - 8]
    grp = _lane_group((ts, MIX_W))
    return _by_group(grp, sum2, sum4, sum8, sum16) / _pool_count(t0, ts) - s1[h:h + ts]


def _make_shifts(src, sh, rows):
    for b in range(1, 8):
        sh[b, 0:rows] = src[b:b + rows]


def _rows_at(src, sh, start, n):
    a, b = divmod(start, 8)
    return src[8 * a:8 * a + n] if b == 0 else sh[b, 8 * a:8 * a + n]


def mixer_fwd(z, sconv, cconv, vecs, wt, bexp, pbd, x_res, wmo_g, *, name, ts=None):
    t = z.shape[0]
    ts = _row_tile(t, MIX_TILE if ts is None else ts)
    hl = HALO
    w = MIX_W
    nch = ts // SGU_CHUNK

    def body(zc, zp, sconv_ref, cconv_ref, vec_ref, wt_ref, bexp_ref, pbd_ref, xr_ref, wmo_ref,
             y_ref, xo_ref, s1, s2, s3, sh):
        i = pl.program_id(0)
        has_prev = i > 0

        def col(ref, c):
            return ref[:, c * w:(c + 1) * w]

        def prev(c):
            return jnp.where(has_prev, col(zp, c), 0.0)

        s1[0:hl] = prev(1) * prev(2)
        s1[hl:hl + ts] = col(zc, 1) * col(zc, 2)
        cv = sconv_ref[0:1] * s1[hl - 2:hl - 2 + ts]
        for k in range(1, SCONV_K):
            cv = cv + sconv_ref[k:k + 1] * s1[hl - 2 + k:hl - 2 + k + ts]
        y_ref[:, 0:w] = (col(zc, 0) * cv).astype(BF16)

        xhat, _ = _ln_stats(col(zc, 4))
        vn = (xhat * vec_ref[0:1]).astype(BF16)
        for c in range(nch):
            rows = slice(c * SGU_CHUNK, (c + 1) * SGU_CHUNK)
            mixed = _sgu_mix(wt_ref, vn[rows]) + bexp_ref[...]
            y_ref[rows, w:2 * w] = (zc[rows, 3 * w:4 * w] * mixed).astype(BF16)

        s1[0:hl] = prev(5) * _sigmoid(prev(6))
        s1[hl:hl + ts] = col(zc, 5) * _sigmoid(col(zc, 6))
        off = hl - (CCONV_K - 1)
        _make_shifts(s1, sh, hl + ts - 8)
        cv = cconv_ref[0:1] * _rows_at(s1, sh, off, ts)
        for k in range(1, CCONV_K):
            cv = cv + cconv_ref[k:k + 1] * _rows_at(s1, sh, off + k, ts)
        xhat, _ = _ln_stats(cv)
        ln = xhat * vec_ref[1:2] + vec_ref[2:3]
        y_ref[:, 2 * w:3 * w] = (ln * _sigmoid(ln)).astype(BF16)

        s1[0:hl] = prev(7)
        s1[hl:hl + ts] = col(zc, 7)
        pooled = _pool_fwd(s1, s2, s3, i * ts, ts, hl)
        y_ref[:, 3 * w:4 * w] = (_dot(pooled.astype(BF16), pbd_ref[...]) * vec_ref[3:4]).astype(BF16)

        xo_ref[...] = xr_ref[...] + _dot(y_ref[...], _full_weight(wmo_ref, "row"))

    full = lambda shape: pl.BlockSpec(shape, lambda i: (0,) * len(shape))
    row = lambda i: (i, 0)
    return pl.pallas_call(
        body, name=name, grid=(t // ts,),
        out_shape=[jax.ShapeDtypeStruct((t, 4 * w), BF16), jax.ShapeDtypeStruct((t, 4 * w), F32)],
        in_specs=[pl.BlockSpec((ts, 8 * w), row),
                  pl.BlockSpec((hl, 8 * w), lambda i: (jnp.maximum(i * (ts // hl) - 1, 0), 0)),
                  full((8, w)), full((32, w)), full((8, w)), full((N_HEADS, SGU_CHUNK, SGU_CHUNK)),
                  full((SGU_CHUNK, w)), full((w, w)), pl.BlockSpec((ts, 4 * w), row), _wspec(wmo_g)],
        out_specs=[pl.BlockSpec((ts, 4 * w), row), pl.BlockSpec((ts, 4 * w), row)],
        scratch_shapes=[pltpu.VMEM((hl + ts, w), F32)] * 3 + [pltpu.VMEM((8, hl + ts, w), F32)],
        compiler_params=_params("parallel"),
    )(z, z, sconv, cconv, vecs, wt, bexp, pbd, x_res, wmo_g)


def mixer_bwd(z, dx, wmo_g, sconv, cconv, vecs, wt, bexp, pbd, *, name, ts=None):
    t = z.shape[0]
    ts = _row_tile(t, MIX_TILE if ts is None else ts)
    hl = HALO
    w = MIX_W
    nch = ts // SGU_CHUNK
    ni = t // ts
    ext = ts + hl

    def body(zc, zp, zn, dxc, dxn_, wmo_ref, sconv_ref, cconv_ref, vec_ref, wt_ref, bexp_ref, pbd_ref,
             dz_ref, gvec_ref, gcc_ref, gwt_ref, gb_ref, gpbd_ref, s1, s2, s3, sh1, sh3, dyc, dyn):
        i = pl.program_id(0)
        has_prev = i > 0
        has_next = i < ni - 1
        wmo = _full_weight(wmo_ref, "row")
        dyc[...] = _dot_nt(dxc[...].astype(BF16), wmo)
        dyn[...] = _dot_nt(dxn_[...].astype(BF16), wmo)

        @pl.when(i == 0)
        def _():
            gvec_ref[...] = jnp.zeros_like(gvec_ref)
            gcc_ref[...] = jnp.zeros_like(gcc_ref)
            gwt_ref[...] = jnp.zeros_like(gwt_ref)
            gb_ref[...] = jnp.zeros_like(gb_ref)
            gpbd_ref[...] = jnp.zeros_like(gpbd_ref)

        def col(ref, c):
            return ref[:, c * w:(c + 1) * w]

        def prev(c):
            return jnp.where(has_prev, col(zp, c), 0.0)

        def nxt(c):
            return jnp.where(has_next, col(zn, c), 0.0)

        def dnext(c):
            return jnp.where(has_next, col(dyn, c), 0.0)

        def rowsum(v):
            return jnp.sum(v, axis=0, keepdims=True)

        s1[0:hl] = prev(1) * prev(2)
        s1[hl:hl + ts] = col(zc, 1) * col(zc, 2)
        s1[hl + ts:hl + ts + hl] = nxt(1) * nxt(2)
        cv = sconv_ref[0:1] * s1[hl - 2:hl - 2 + ts]
        for k in range(1, SCONV_K):
            cv = cv + sconv_ref[k:k + 1] * s1[hl - 2 + k:hl - 2 + k + ts]
        dya = col(dyc, 0)
        dz_ref[:, 0:w] = (dya * cv).astype(BF16)
        s2[0:ts] = dya * col(zc, 0)
        s2[ts:ext] = dnext(0) * nxt(0)
        dv = sconv_ref[0:1] * s2[2:2 + ts]
        for k in range(1, SCONV_K):
            dv = dv + sconv_ref[k:k + 1] * s2[2 - k:2 - k + ts]
        dz_ref[:, w:2 * w] = (dv * col(zc, 2)).astype(BF16)
        dz_ref[:, 2 * w:3 * w] = (dv * col(zc, 1)).astype(BF16)
        dcv = s2[0:ts]
        for k in range(SCONV_K):
            gvec_ref[k:k + 1] += rowsum(dcv * s1[hl - 2 + k:hl - 2 + k + ts])

        g_sgu = vec_ref[0:1]
        xhat, rstd = _ln_stats(col(zc, 4))
        vn = (xhat * g_sgu).astype(BF16)
        grp = _lane_group((SGU_CHUNK, w))
        lane = lax.broadcasted_iota(jnp.int32, (SGU_CHUNK, SGU_CHUNK), 1)
        tril = lax.broadcasted_iota(jnp.int32, (SGU_CHUNK, SGU_CHUNK), 0) >= lane
        for c in range(nch):
            rows = slice(c * SGU_CHUNK, (c + 1) * SGU_CHUNK)
            vnc = vn[rows]
            mixed = _sgu_mix(wt_ref, vnc) + bexp_ref[...]
            dyb = dyc[rows, w:2 * w]
            dz_ref[rows, 3 * w:4 * w] = (dyb * mixed).astype(BF16)
            dmix = dyb * zc[rows, 3 * w:4 * w]
            dmixb = dmix.astype(BF16)
            dvn = jnp.zeros((SGU_CHUNK, w), F32)
            gb = jnp.zeros((SGU_CHUNK, SGU_CHUNK), F32)
            for hd in range(N_HEADS):
                dvn = jnp.where(grp == hd, _dot_tn(wt_ref[hd], dmixb), dvn)
                dm_h = jnp.where(grp == hd, dmix, 0.0)
                gwt_ref[hd] += jnp.where(tril, _dot_nt(dm_h.astype(BF16), vnc), 0.0)
                gb = gb + jnp.where(lane == hd, jnp.sum(dm_h, axis=1, keepdims=True), 0.0)
            gb_ref[...] += gb
            s3[rows] = dvn
        dvn = s3[0:ts]
        gvec_ref[3:4] += rowsum(dvn * xhat)
        dz_ref[:, 4 * w:5 * w] = _ln_bwd(xhat, rstd, dvn * g_sgu).astype(BF16)

        sig_c = _sigmoid(col(zc, 6))
        s1[0:hl] = prev(5) * _sigmoid(prev(6))
        s1[hl:hl + ts] = col(zc, 5) * sig_c
        s1[hl + ts:hl + ts + hl] = nxt(5) * _sigmoid(nxt(6))
        off = hl - (CCONV_K - 1)
        _make_shifts(s1, sh1, ts + 2 * hl - 8)
        cv = cconv_ref[0:1] * _rows_at(s1, sh1, off, ext)
        for k in range(1, CCONV_K):
            cv = cv + cconv_ref[k:k + 1] * _rows_at(s1, sh1, off + k, ext)
        xhat, rstd = _ln_stats(cv)
        ln = xhat * vec_ref[1:2] + vec_ref[2:3]
        sg = _sigmoid(ln)
        s2[0:ts] = col(dyc, 2)
        s2[ts:ext] = dnext(2)
        dln = s2[0:ext] * (sg * (1.0 + ln * (1.0 - sg)))
        gvec_ref[4:5] += rowsum(dln[0:ts] * xhat[0:ts])
        gvec_ref[5:6] += rowsum(dln[0:ts])
        s3[0:ext] = _ln_bwd(xhat, rstd, dln * vec_ref[1:2])
        _make_shifts(s3, sh3, ext - 8)
        dyg = cconv_ref[0:1] * _rows_at(s3, sh3, CCONV_K - 1, ts)
        for k in range(1, CCONV_K):
            dyg = dyg + cconv_ref[k:k + 1] * _rows_at(s3, sh3, CCONV_K - 1 - k, ts)
        dz_ref[:, 5 * w:6 * w] = (dyg * sig_c).astype(BF16)
        dz_ref[:, 6 * w:7 * w] = (dyg * col(zc, 5) * sig_c * (1.0 - sig_c)).astype(BF16)
        dcv = s3[0:ts]
        for k in range(CCONV_K):
            gcc_ref[k:k + 1] += rowsum(dcv * _rows_at(s1, sh1, off + k, ts))

        scale = vec_ref[3:4]
        s1[0:hl] = prev(7)
        s1[hl:hl + ts] = col(zc, 7)
        pooled = _pool_fwd(s1, s2, s3, i * ts, ts, hl).astype(BF16)
        q0 = _dot(pooled, pbd_ref[...])
        dyd = col(dyc, 3)
        gvec_ref[6:7] += rowsum(dyd * q0)
        dq = (dyd * scale).astype(BF16)
        gpbd_ref[...] += _dot_tn(pooled, dq)
        s1[0:ts] = _dot_nt(dq, pbd_ref[...])
        s1[ts:ext] = _dot_nt((dnext(3) * scale).astype(BF16), pbd_ref[...])
        dpool = s1[0:ts]
        s2[0:ext] = s1[0:ext] / _pool_count(i * ts, ext)
        s3[0:ts + 24] = s2[0:ts + 24] + s2[1:ts + 25]
        f2 = s3[0:ts]
        s2[0:ts + 16] = s3[0:ts + 16] + s3[2:ts + 18]
        f4 = s2[0:ts]
        s3[0:ts + 8] = s2[0:ts + 8] + s2[4:ts + 12]
        f8 = s3[0:ts]
        f16 = f8 + s3[8:ts + 8]
        dz_ref[:, 7 * w:8 * w] = (_by_group(_lane_group((ts, w)), f2, f4, f8, f16) - dpool).astype(BF16)

    full = lambda shape: pl.BlockSpec(shape, lambda i: (0,) * len(shape))
    r = ts // hl
    prev_map = lambda i: (jnp.maximum(i * r - 1, 0), 0)
    next_map = lambda i: (jnp.minimum((i + 1) * r, t // hl - 1), 0)
    return pl.pallas_call(
        body, name=name, grid=(ni,),
        out_shape=[jax.ShapeDtypeStruct((t, 8 * w), BF16), jax.ShapeDtypeStruct((8, w), F32),
                   jax.ShapeDtypeStruct((32, w), F32),
                   jax.ShapeDtypeStruct((N_HEADS, SGU_CHUNK, SGU_CHUNK), F32),
                   jax.ShapeDtypeStruct((SGU_CHUNK, SGU_CHUNK), F32), jax.ShapeDtypeStruct((w, w), F32)],
        in_specs=[pl.BlockSpec((ts, 8 * w), lambda i: (i, 0)),
                  pl.BlockSpec((hl, 8 * w), prev_map), pl.BlockSpec((hl, 8 * w), next_map),
                  pl.BlockSpec((ts, 4 * w), lambda i: (i, 0)), pl.BlockSpec((hl, 4 * w), next_map),
                  _wspec(wmo_g),
                  full((8, w)), full((32, w)), full((8, w)), full((N_HEADS, SGU_CHUNK, SGU_CHUNK)),
                  full((SGU_CHUNK, w)), full((w, w))],
        out_specs=[pl.BlockSpec((ts, 8 * w), lambda i: (i, 0)), full((8, w)), full((32, w)),
                   full((N_HEADS, SGU_CHUNK, SGU_CHUNK)), full((SGU_CHUNK, SGU_CHUNK)), full((w, w))],
        scratch_shapes=[pltpu.VMEM((ts + 2 * hl, w), F32)] * 3 + [pltpu.VMEM((8, ts + 2 * hl, w), F32)] * 2
        + [pltpu.VMEM((ts, 4 * w), F32), pltpu.VMEM((hl, 4 * w), F32)],
        compiler_params=_params("arbitrary"),
    )(z, z, z, dx, dx, wmo_g, sconv, cconv, vecs, wt, bexp, pbd)


def _attn_head(q, kv_ref, hd, d):
    hw = d // N_HEADS
    qh = q[:, hd * hw:(hd + 1) * hw]
    kh = kv_ref[:, hd * hw:(hd + 1) * hw].astype(BF16)
    vh = kv_ref[:, d + hd * hw:d + (hd + 1) * hw].astype(BF16)
    s = _dot_nt(qh, kh) * (1.0 / (hw ** 0.5))
    e = jnp.exp(s - jnp.max(s, axis=-1, keepdims=True))
    p = e / jnp.sum(e, axis=-1, keepdims=True)
    return qh, kh, vh, p


def xattn_fwd(x, gain, kv, wq_g, wo_g, *, name, tm=None):
    t, d = x.shape
    nm = kv.shape[0]
    tm = _row_tile(t, tm)
    hw = d // N_HEADS

    def body(x_ref, g_ref, kv_ref, wq_ref, wo_ref, o_ref):
        xv = x_ref[...]
        xhat, _ = _rms_fwd(xv, None)
        h = (xhat * g_ref[...]).astype(BF16)
        q = _dot(h, _full_weight(wq_ref, "row")).astype(BF16)
        wo = _full_weight(wo_ref, "row")
        out = xv
        for hd in range(N_HEADS):
            _, _, vh, p = _attn_head(q, kv_ref, hd, d)
            oh = _dot(p.astype(BF16), vh).astype(BF16)
            out = out + _dot(oh, wo[hd * hw:(hd + 1) * hw])
        o_ref[...] = out

    row = lambda i: (i, 0)
    return pl.pallas_call(
        body, name=name, grid=(t // tm,),
        out_shape=jax.ShapeDtypeStruct((t, d), F32),
        in_specs=[pl.BlockSpec((tm, d), row), pl.BlockSpec((1, d), lambda i: (0, 0)),
                  pl.BlockSpec((nm, 2 * d), lambda i: (0, 0)), _wspec(wq_g), _wspec(wo_g)],
        out_specs=pl.BlockSpec((tm, d), row),
        compiler_params=_params("parallel"),
    )(x, gain.reshape(1, d), kv, wq_g, wo_g)


def xattn_bwd_rows(x, dxn, gain, kv, wq_g, wo_g, after, *, name, tm=None):
    t, d = x.shape
    nm = kv.shape[0]
    tm = _row_tile(t, tm)
    hw = d // N_HEADS

    def body(x_ref, dxn_ref, g_ref, kv_ref, wq_ref, wo_ref, after_ref,
             dx_ref, h_ref, dq_ref, o_ref, dkv_ref, dg_ref):
        i = pl.program_id(0)

        @pl.when(i == 0)
        def _():
            dkv_ref[...] = jnp.zeros_like(dkv_ref)
            dg_ref[...] = jnp.zeros_like(dg_ref)
        g = g_ref[...]
        xhat, r = _rms_fwd(x_ref[...], None)
        h = (xhat * g).astype(BF16)
        h_ref[...] = h
        wq = _full_weight(wq_ref, "row")
        q = _dot(h, wq).astype(BF16)
        dxn = dxn_ref[...]
        do = _dot_nt(dxn.astype(BF16), _full_weight(wo_ref, "row")).astype(BF16)
        for hd in range(N_HEADS):
            cols = slice(hd * hw, (hd + 1) * hw)
            qh, kh, vh, p = _attn_head(q, kv_ref, hd, d)
            pb = p.astype(BF16)
            o_ref[:, cols] = _dot(pb, vh).astype(BF16)
            doh = do[:, cols]
            dkv_ref[:, d + hd * hw:d + (hd + 1) * hw] += _dot_tn(pb, doh)
            dp = _dot_nt(doh, vh)
            ds = (p * (dp - jnp.sum(dp * p, axis=-1, keepdims=True)) * (1.0 / (hw ** 0.5))).astype(BF16)
            dq_ref[:, cols] = _dot(ds, kh).astype(BF16)
            dkv_ref[:, cols] += _dot_tn(ds, qh)
        dh = _dot_nt(dq_ref[...], wq)
        dx, dg = _rms_bwd(xhat, r, g, dh)
        dx_ref[...] = dxn + dx
        dg_ref[...] += dg

    row = lambda i: (i, 0)
    fix = lambda i: (0, 0)
    return pl.pallas_call(
        body, name=name, grid=(t // tm,),
        out_shape=[jax.ShapeDtypeStruct((t, d), F32), jax.ShapeDtypeStruct((t, d), BF16),
                   jax.ShapeDtypeStruct((t, d), BF16), jax.ShapeDtypeStruct((t, d), BF16),
                   jax.ShapeDtypeStruct((nm, 2 * d), F32), jax.ShapeDtypeStruct((1, d), F32)],
        in_specs=[pl.BlockSpec((tm, d), row), pl.BlockSpec((tm, d), row), pl.BlockSpec((1, d), fix),
                  pl.BlockSpec((nm, 2 * d), fix), _wspec(wq_g), _wspec(wo_g), ANY_SPEC],
        out_specs=[pl.BlockSpec((tm, d), row)] * 4 + [pl.BlockSpec((nm, 2 * d), fix),
                                                      pl.BlockSpec((1, d), fix)],
        compiler_params=_params("arbitrary"),
    )(x, dxn, gain.reshape(1, d), kv, wq_g, wo_g, after)


def loss_head(x, target, gain, *, name, tm=None):
    t, d = x.shape
    tm = _row_tile(t, tm)

    def body(x_ref, t_ref, g_ref, dx_ref, dg_ref, loss_ref):
        @pl.when(pl.program_id(0) == 0)
        def _():
            dg_ref[...] = jnp.zeros_like(dg_ref)
            loss_ref[...] = jnp.zeros_like(loss_ref)
        g = g_ref[...]
        xhat, r = _rms_fwd(x_ref[...], None)
        err = xhat * g - t_ref[...]
        loss_ref[...] += 0.5 * jnp.sum(jnp.sum(err * err, axis=-1, keepdims=True) / d,
                                       axis=0, keepdims=True)
        dx, dg = _rms_bwd(xhat, r, g, err / d)
        dx_ref[...] = dx
        dg_ref[...] += dg

    row = lambda i: (i, 0)
    fix = lambda i: (0, 0)
    return pl.pallas_call(
        body, name=name, grid=(t // tm,),
        out_shape=[jax.ShapeDtypeStruct((t, d), F32), jax.ShapeDtypeStruct((1, d), F32),
                   jax.ShapeDtypeStruct((1, 1), F32)],
        in_specs=[pl.BlockSpec((tm, d), row), pl.BlockSpec((tm, d), row), pl.BlockSpec((1, d), fix)],
        out_specs=[pl.BlockSpec((tm, d), row), pl.BlockSpec((1, d), fix), pl.BlockSpec((1, 1), fix)],
        compiler_params=_params("arbitrary"),
    )(x, target, gain.reshape(1, d))


def _adamw_math(w, g, m, v):
    m = ADAM_B1 * m + (1.0 - ADAM_B1) * g
    v = ADAM_B2 * v + (1.0 - ADAM_B2) * (g * g)
    m_hat = m / (1.0 - ADAM_B1 ** ADAM_STEP)
    v_hat = v / (1.0 - ADAM_B2 ** ADAM_STEP)
    delta = -ADAM_LR * (m_hat / (jnp.sqrt(v_hat) + ADAM_EPS) + ADAM_WD * w)
    return delta, m, v


def adamw_sharded(own, lands, w, m, v, me_arr, *, name):
    nl, r, c = w.shape
    assert nl == len(own) == len(lands) == 2
    tr = next(cand for cand in (*ROW_BLOCKS, r) if r % cand == 0)
    nr = r // tr

    def body(me_ref, o0, o1, l0, l1, w_ref, m_ref, v_ref, g_out, d_out, m_out, v_out):
        def total(o_ref, l_ref):
            acc = o_ref[...].astype(F32)
            for p in range(N_DEV - 1):
                acc = acc + l_ref[p].astype(F32)
            return acc
        g = jnp.where(pl.program_id(0) == 0, total(o0, l0), total(o1, l1))
        delta, mn, vn = _adamw_math(w_ref[...], g, m_ref[...], v_ref[...])
        g_out[...] = g
        d_out[...] = delta
        m_out[...] = mn
        v_out[...] = vn

    row0 = lambda l, i: jnp.where(l == 0, i, nr - 1)
    row1 = lambda l, i: jnp.where(l == 1, i, 0)
    blk = pl.BlockSpec((None, tr, c), lambda l, i, me: (l, i, 0))
    grid_spec = pltpu.PrefetchScalarGridSpec(
        num_scalar_prefetch=1, grid=(nl, nr),
        in_specs=[pl.BlockSpec((None, tr, c), lambda l, i, me: (me[0], row0(l, i), 0)),
                  pl.BlockSpec((None, tr, c), lambda l, i, me: (me[0], row1(l, i), 0)),
                  pl.BlockSpec((N_DEV - 1, tr, c), lambda l, i, me: (0, row0(l, i), 0)),
                  pl.BlockSpec((N_DEV - 1, tr, c), lambda l, i, me: (0, row1(l, i), 0)),
                  blk, blk, blk],
        out_specs=[blk] * 4)
    return pl.pallas_call(
        body, name=name, grid_spec=grid_spec,
        out_shape=[jax.ShapeDtypeStruct((nl, r, c), F32)] * 4,
        compiler_params=_params("arbitrary", "arbitrary"),
    )(me_arr, own[0], own[1], lands[0], lands[1], w, m, v)


def adamw_many(gs, ws, ms, vs, *, name):
    n = len(ws)
    shapes = [w.shape for w in ws]
    as2d = lambda a: a.reshape(1, -1) if a.ndim == 1 else a

    def body(*refs):
        g_r, w_r, m_r, v_r = refs[:n], refs[n:2 * n], refs[2 * n:3 * n], refs[3 * n:4 * n]
        outs = refs[4 * n:]
        for i in range(n):
            delta, mn, vn = _adamw_math(w_r[i][...], g_r[i][...], m_r[i][...], v_r[i][...])
            outs[3 * i][...] = delta
            outs[3 * i + 1][...] = mn
            outs[3 * i + 2][...] = vn

    operands = [as2d(a) for group in (gs, ws, ms, vs) for a in group]
    out_shape = [jax.ShapeDtypeStruct(as2d(w).shape, F32) for w in ws for _ in range(3)]
    outs = pl.pallas_call(
        body, name=name, out_shape=out_shape,
        in_specs=[VMEM_SPEC] * (4 * n), out_specs=[VMEM_SPEC] * (3 * n),
        compiler_params=pltpu.CompilerParams(vmem_limit_bytes=VMEM_LIMIT),
    )(*operands)
    return [tuple(outs[3 * i + k].reshape(shapes[i]) for k in range(3)) for i in range(n)]


def cast_into_slot(a, layer, me_arr, *, name, dtype=None, after=None):
    dtype = BF16 if dtype is None else dtype
    _, r, c = a.shape
    tr = next(cand for cand in (*ROW_BLOCKS, r) if r % cand == 0)
    extra = [] if after is None else [after]

    def body(me_ref, a_ref, *rest):
        rest[-1][...] = a_ref[...].astype(dtype)

    grid_spec = pltpu.PrefetchScalarGridSpec(
        num_scalar_prefetch=1, grid=(r // tr,),
        in_specs=[pl.BlockSpec((None, tr, c), lambda i, me: (layer, i, 0))] + [ANY_SPEC] * len(extra),
        out_specs=pl.BlockSpec((None, tr, c), lambda i, me: (me[0], i, 0)))
    return pl.pallas_call(
        body, name=name, grid_spec=grid_spec,
        out_shape=jax.ShapeDtypeStruct((N_DEV, r, c), dtype),
        compiler_params=_params("parallel"),
    )(me_arr, a, *extra)


def _pack(arrs, rows):
    flat = jnp.concatenate([a.reshape(-1).astype(F32) for a in arrs])
    pad = rows * 128 - flat.shape[0]
    assert pad >= 0
    if pad:
        flat = jnp.concatenate([flat, jnp.zeros((pad,), F32)])
    return flat.reshape(rows, 128)


def _unpack(packed, shapes):
    flat = packed.reshape(-1)
    out, pos = [], 0
    for s in shapes:
        n = 1
        for dim in s:
            n *= dim
        out.append(flat[pos:pos + n].reshape(s))
        pos += n
    return out


def _rows_for(shapes):
    n = 0
    for s in shapes:
        k = 1
        for dim in s:
            k *= dim
        n += k
    return -(-n // 1024) * 8


GATHER_GROUPS = (("ffn1", ("ffn1_w_in", "ffn1_w_out")),
                 ("mid", ("mix_w_in", "mix_w_out", "xattn_wkv", "xattn_wq", "xattn_wo")),
                 ("ffn2", ("ffn2_w_in", "ffn2_w_out")))
SMALL_REPL = ["norm_ffn1", "norm_mix", "sgu_norm_g", "sgu_w", "sgu_b", "cconv_ln_g", "cconv_ln_b",
              "pool_w", "pool_scale", "norm_xattn", "norm_mem", "norm_ffn2", "norm_final"]
SMALL_SHARD = ["sconv_w", "cconv_w"]
TRANSPOSED = ("ffn1_w_in", "ffn2_w_in")
WEIGHTS = ["norm_ffn1", "ffn1_w_in", "ffn1_w_out", "norm_mix", "mix_w_in", "sconv_w", "sgu_norm_g",
           "sgu_w", "sgu_b", "cconv_w", "cconv_ln_g", "cconv_ln_b", "pool_w", "pool_scale", "mix_w_out",
           "norm_xattn", "norm_mem", "xattn_wq", "xattn_wkv", "xattn_wo", "norm_ffn2", "ffn2_w_in",
           "ffn2_w_out", "norm_final"]


def kernel(x, mem, norm_ffn1, ffn1_w_in, ffn1_w_out, norm_mix, mix_w_in, sconv_w, sgu_norm_g, sgu_w, sgu_b, cconv_w, cconv_ln_g, cconv_ln_b, pool_w, pool_scale, mix_w_out, norm_xattn, norm_mem, xattn_wq, xattn_wkv, xattn_wo, norm_ffn2, ffn2_w_in, ffn2_w_out, norm_final, loss_target, m_norm_ffn1, m_ffn1_w_in, m_ffn1_w_out, m_norm_mix, m_mix_w_in, m_sconv_w, m_sgu_norm_g, m_sgu_w, m_sgu_b, m_cconv_w, m_cconv_ln_g, m_cconv_ln_b, m_pool_w, m_pool_scale, m_mix_w_out, m_norm_xattn, m_norm_mem, m_xattn_wq, m_xattn_wkv, m_xattn_wo, m_norm_ffn2, m_ffn2_w_in, m_ffn2_w_out, m_norm_final, v_norm_ffn1, v_ffn1_w_in, v_ffn1_w_out, v_norm_mix, v_mix_w_in, v_sconv_w, v_sgu_norm_g, v_sgu_w, v_sgu_b, v_cconv_w, v_cconv_ln_g, v_cconv_ln_b, v_pool_w, v_pool_scale, v_mix_w_out, v_norm_xattn, v_norm_mem, v_xattn_wq, v_xattn_wkv, v_xattn_wo, v_norm_ffn2, v_ffn2_w_in, v_ffn2_w_out, v_norm_final):
    args = dict(locals())
    wts = {n: args[n] for n in WEIGHTS}
    mom = {n: args["m_" + n] for n in WEIGHTS}
    var = {n: args["v_" + n] for n in WEIGHTS}
    for n in TRANSPOSED:
        wts[n], mom[n], var[n] = (jnp.swapaxes(a, 1, 2) for a in (wts[n], mom[n], var[n]))
    x0 = x[0]
    mem0 = mem[0]
    target = loss_target[0]
    t, d = x0.shape
    nl = norm_ffn1.shape[0]
    w = MIX_W
    me = _my_index()

    me_arr = jnp.reshape(me, (1,)).astype(jnp.int32)

    small_g = all_gather([sconv_w, cconv_w], name="gather_conv_taps")
    sconv_full = jnp.transpose(small_g[0], (1, 2, 0, 3)).reshape(nl, SCONV_K, w)
    cconv_full = jnp.transpose(small_g[1], (1, 2, 0, 3)).reshape(nl, CCONV_K, w)
    pending = {}
    token = small_g[1]
    masks = GATHER_MASKS
    keys = [(gname, l, members) for l in range(nl) for gname, members in GATHER_GROUPS]
    first = [[cast_into_slot(wts[n], keys[0][1], me_arr, name=f"cast_{n}{keys[0][1]}") for n in keys[0][2]]]
    started, token = gather_start_groups(first, token, name="gather_start_first", masks=masks)
    casts = [[cast_into_slot(wts[n], l, me_arr, name=f"cast_{n}{l}", after=token) for n in members]
             for gname, l, members in keys[1:]]
    rest, token = gather_start_groups(casts, token, name="gather_start_rest", masks=masks)
    for (gname, l, members), (send, recv, gs) in zip(keys, started + rest):
        pending[gname, l] = (members, gs, send, recv, masks)
    wg = [dict() for _ in range(nl)]

    handing_over = {}

    def arrive_early(gname, l, after):
        members, gs, send, recv, masks = pending.pop((gname, l))
        gs = gather_wait(gs, send, recv, after, name=f"gather_wait_{gname}{l}", masks=masks)
        fsend, frecv, gs, _ = forward_start(gs, after, name=f"gather_forward_start_{gname}{l}")
        handing_over[gname, l] = (members, gs, fsend, frecv)

    def arrive(gname, l, after):
        if (gname, l) in handing_over:
            members, gs, fsend, frecv = handing_over.pop((gname, l))
            gs = forward_wait(gs, fsend, frecv, after, name=f"gather_forward_wait_{gname}{l}")
        else:
            members, gs, send, recv, masks = pending.pop((gname, l))
            gs = gather_wait(gs, send, recv, after, name=f"gather_wait_{gname}{l}", masks=masks)
            gs = sibling_forward(gs, name=f"gather_forward_{gname}{l}")
        wg[l].update(zip(members, gs))
    sconv_pad = jnp.pad(sconv_full, ((0, 0), (0, 8 - SCONV_K), (0, 0)))
    cconv_pad = jnp.pad(cconv_full, ((0, 0), (0, 32 - CCONV_K), (0, 0)))
    zeros_w = jnp.zeros((nl, w), F32)
    vecs = jnp.stack([sgu_norm_g, cconv_ln_g, cconv_ln_b, pool_scale] + [zeros_w] * 4, axis=1)
    wt = jnp.tril(sgu_w).astype(BF16)
    bexp = jnp.repeat(jnp.swapaxes(sgu_b, 1, 2), w // N_HEADS, axis=2)
    eye = jnp.eye(4, dtype=F32)
    pbd = jnp.einsum("lgcd,gh->lgchd", pool_w, eye).reshape(nl, w, w).astype(BF16)

    def mixer_args(l):
        return sconv_pad[l], cconv_pad[l], vecs[l], wt[l], bexp[l], pbd[l]

    saved = []
    xc = x0
    after = token
    for l in range(nl):
        s = {"x_ffn1": xc}
        arrive("ffn1", l, after)
        xc, s["gu_ffn1"] = ffn_fwd(xc, norm_ffn1[l], wg[l]["ffn1_w_in"], wg[l]["ffn1_w_out"],
                                   name=f"ffn1_fwd{l}", tm=FFN_FWD_TILE)
        s["x_mix"] = xc
        arrive("mid", l, xc)
        z = mm_rows(xc, wg[l]["mix_w_in"], "col", gain=norm_mix[l], name=f"mix_in{l}")
        y, xc = mixer_fwd(z, *mixer_args(l), xc, wg[l]["mix_w_out"], name=f"mixer_fwd{l}")
        s["z"], s["y"] = z, y
        s["x_att"] = xc
        kv = mm_rows(mem0, wg[l]["xattn_wkv"], "col", gain=norm_mem[l], name=f"kv{l}")
        s["kv"] = kv
        if l > 0:
            arrive_early("ffn2", l, kv)
        xc = xattn_fwd(xc, norm_xattn[l], kv, wg[l]["xattn_wq"], wg[l]["xattn_wo"], name=f"xattn_fwd{l}")
        s["x_ffn2"] = xc
        arrive("ffn2", l, xc)
        xc, s["gu_ffn2"] = ffn_fwd(xc, norm_ffn2[l], wg[l]["ffn2_w_in"], wg[l]["ffn2_w_out"],
                                   name=f"ffn2_fwd{l}", tm=FFN_FWD_TILE)
        after = xc
        saved.append(s)

    dx, g_norm_final, loss_local = loss_head(xc, target, norm_final, name="loss_head")

    tm = _row_tile(t, TN_TILE)
    small ={n: [None] * nl for n in SMALL_REPL + SMALL_SHARD if n != "norm_final"}
    scattered = {}
    tie = [token]

    def send_grads(gname, l, grads):
        members = list(grads)
        send, recv, gs, lands, tie[0] = scatter_start(
            [grads[n] for n in members], tie[0], name=f"scatter_start_{gname}{l}")
        scattered[gname, l] = (members, gs, lands, send, recv)

    names = SMALL_REPL + SMALL_SHARD + ["loss"]
    small_pending = []

    def start_small():
        small_full = {n: jnp.stack(v) for n, v in small.items()}
        small_full["norm_final"] = g_norm_final[0]
        small_full["loss"] = loss_local[0]
        shapes = [small_full[n].shape for n in names]
        packed = _pack([small_full[n] for n in names], _rows_for(shapes))
        slot = cast_into_slot(packed[None], 0, me_arr, name="small_into_slot", dtype=F32)
        send, recv, gs, tie[0] = gather_start([slot], tie[0], name="small_gather_start", masks=ALL_MASKS)
        small_pending.append((gs, send, recv, shapes))

    dx_bf = [None]

    def ffn_backward(which, l, x_in, dy, gu, gain):
        w_in, w_out = wg[l][which + "_w_in"], wg[l][which + "_w_out"]
        dx_, h_, act, dgu, dgn, dyh, dx_bf[0] = ffn_bwd_rows(x_in, dy, gu, gain, w_in, w_out, tie[0],
                                                   name=f"{which}_bwd{l}_rows")
        small["norm_" + which][l] = dgn[0]
        last = which == "ffn1" and l == 0
        if last:
            start_small()
        g_in = ffn_grad_w_in(h_, dgu, tie[0], name=f"{which}_bwd{l}_dwin")
        if last:
            send_grads(which + "_in", l, {which + "_w_in": g_in})
        g_out = ffn_grad_w_out(act, dyh, tie[0], name=f"{which}_bwd{l}_dwout")
        if last:
            send_grads(which + "_out", l, {which + "_w_out": g_out})
        else:
            send_grads(which, l, {which + "_w_in": g_in, which + "_w_out": g_out})
        return dx_

    for l in reversed(range(nl)):
        s = saved[l]
        wl = wg[l]
        dx = ffn_backward("ffn2", l, s["x_ffn2"], dx, s["gu_ffn2"], norm_ffn2[l])
        dxn_b = dx_bf[0]

        bg = {}
        dxn = dx
        dx, h, dq, o, dkv, dgn = xattn_bwd_rows(
            s["x_att"], dxn, norm_xattn[l], s["kv"], wl["xattn_wq"], wl["xattn_wo"], tie[0],
            name=f"xattn_bwd{l}")
        small["norm_xattn"][l] = dgn[0]
        row_spec = pl.BlockSpec((tm, d), lambda s_, i: (i, 0))
        bg["xattn_wq"] = mm_tn(h, dq, nb=1, ka=d, nbk=d, tm=tm, m=t, a_spec=row_spec, b_spec=row_spec,
                               name=f"dwq{l}").reshape(N_DEV, d // N_DEV, d)
        bg["xattn_wo"] = mm_tn(o, dxn_b, nb=1, ka=d, nbk=d, tm=tm, m=t, a_spec=row_spec, b_spec=row_spec,
                               name=f"dwo{l}").reshape(N_DEV, d // N_DEV, d)
        _, mhat, dgn = mm_nt(dkv, wl["xattn_wkv"], "col", x=mem0, gain=norm_mem[l], name=f"dmem{l}")
        small["norm_mem"][l] = dgn[0]
        nm = mem0.shape[0]
        bg["xattn_wkv"] = mm_tn(mhat, dkv, nb=N_DEV, ka=d, nbk=2 * d // N_DEV, tm=nm, m=nm,
                                a_spec=pl.BlockSpec((nm, d), lambda s_, i: (0, 0)),
                                b_spec=pl.BlockSpec((nm, 2 * d // N_DEV), lambda s_, i: (0, s_)),
                                name=f"dwkv{l}")
        send_grads("xattn", l, bg)

        bg = {}
        dxn = dx
        bg["mix_w_out"] = mm_tn(s["y"], dxn, nb=1, ka=d, nbk=d, tm=tm, m=t, a_spec=row_spec,
                                b_spec=row_spec, name=f"dwmo{l}").reshape(N_DEV, d // N_DEV, d)
        dz, gvec, gcc, gwt, gb, gpbd = mixer_bwd(s["z"], dxn, wl["mix_w_out"], *mixer_args(l),
                                                 name=f"mixer_bwd{l}")
        small["sconv_w"][l] = gvec[0:SCONV_K]
        small["sgu_norm_g"][l] = gvec[3]
        small["cconv_ln_g"][l] = gvec[4]
        small["cconv_ln_b"][l] = gvec[5]
        small["pool_scale"][l] = gvec[6]
        small["cconv_w"][l] = gcc[0:CCONV_K]
        small["sgu_w"][l] = gwt
        small["sgu_b"][l] = jnp.transpose(gb[:, 0:N_HEADS])
        gw = w // 4
        small["pool_w"][l] = jnp.stack([gpbd[g * gw:(g + 1) * gw, g * gw:(g + 1) * gw] for g in range(4)])
        dx, h, dgn = mm_nt(dz, wl["mix_w_in"], "col", x=s["x_mix"], gain=norm_mix[l], dx_in=dxn,
                           after=tie[0], name=f"dh_mix{l}")
        small["norm_mix"][l] = dgn[0]
        th = _row_tile(t, TN_TILE // 2)
        bg["mix_w_in"] = mm_tn(h, dz, nb=1, ka=d, nbk=N_DEV * w, tm=th, m=t, col_slots=N_DEV,
                               a_spec=pl.BlockSpec((th, d), lambda s_, i: (i, 0)),
                               b_spec=pl.BlockSpec((th, N_DEV * w), lambda s_, i: (i, 0)), name=f"dwmi{l}")
        send_grads("mix", l, bg)

        dx = ffn_backward("ffn1", l, s["x_ffn1"], dx, s["gu_ffn1"], norm_ffn1[l])

    out = {}
    own, land = {}, {}

    def collect(keys, after):
        for gname, l in keys:
            members, gs, lands, send, recv = scattered.pop((gname, l))
            gs, lands = scatter_wait(gs, lands, send, recv, after, name=f"scatter_wait_{gname}{l}")
            for n, g_, l_ in zip(members, gs, lands):
                own.setdefault(n, {})[l] = g_
                land.setdefault(n, {})[l] = l_

    def update(ns, after):
        for n in ns:
            out[n] = adamw_sharded([own[n][l] for l in range(nl)], [land[n][l] for l in range(nl)],
                                   wts[n], mom[n], var[n], me_arr, name="adamw_" + n)
            after = out[n][1]
        return after

    after = tie[0]
    for gname in ("ffn2", "xattn", "mix"):
        collect([(gname, l) for l in reversed(range(nl))], after)
        after = update([n for n in own if n not in out], after)
    (gs, send, recv, shapes), = small_pending
    gs = gather_wait(gs, send, recv, after, name="small_gather_wait", masks=ALL_MASKS)
    summed = sum_slots(gs[0], name="small_sum")
    gsm = dict(zip(names, _unpack(summed, shapes)))
    loss = gsm["loss"][0]
    cs = w // N_DEV
    for n in SMALL_SHARD:
        gsm[n] = lax.dynamic_slice_in_dim(gsm[n], me * cs, cs, axis=2)
    small_names = SMALL_REPL + SMALL_SHARD
    upd = adamw_many([gsm[n] for n in small_names], [wts[n] for n in small_names],
                     [mom[n] for n in small_names], [var[n] for n in small_names], name="adamw_small")
    for n, (a, b, c) in zip(small_names, upd):
        out[n] = (gsm[n], a, b, c)
    after = upd[0][0]
    collect([("ffn1", l) for l in reversed(range(1, nl))] + [("ffn1_in", 0)], after)
    after = update(["ffn1_w_in"], after)
    collect([("ffn1_out", 0)], after)
    update(["ffn1_w_out"], after)
    for n in TRANSPOSED:
        out[n] = tuple(jnp.swapaxes(a, 1, 2) for a in out[n])

    grad_x = dx.reshape(1, t, d)
    return (loss, grad_x, *[out[n][0] for n in WEIGHTS], *[out[n][1] for n in WEIGHTS],
            *[out[n][2] for n in WEIGHTS], *[out[n][3] for n in WEIGHTS])
```
